```python
import numpy as np
import jax, jax.numpy as jnp
from jax import lax

D_MODEL = 1024
BATCH = 4
SEQ = 4096
DEPTH = 2

HEAD_DIM = 64
NSA_HEADS = 8
NSA_KV_GROUPS = 2
NSA_CMP_BLOCK = 32
NSA_CMP_STRIDE = 16
NSA_CMP_HIDDEN = 128
NSA_SEL_BLOCK = 64
NSA_TOP_N = 16
NSA_WINDOW = 512
MOBA_HEADS = 8
MOBA_BLOCK = 256
MOBA_TOP_K = 3
GLA_HEADS = 4
GLA_DK = D_MODEL // (2 * GLA_HEADS)
GLA_DV = D_MODEL // GLA_HEADS
GLA_GATE_RANK = 16
GLA_TAU = 16.0
GLA_CHUNK = 64
Q_BLOCK = 32
LN_EPS = 1e-5
DEEPNORM_ALPHA = (2.0 * DEPTH) ** 0.25
DEEPNORM_BETA = (8.0 * DEPTH) ** -0.25
NEG = -1e30
FORCE_BONUS = 1e4

NSA_W = NSA_HEADS * HEAD_DIM
NSA_KV_W = NSA_KV_GROUPS * HEAD_DIM
MOBA_W = MOBA_HEADS * HEAD_DIM
MIX_W = NSA_W + MOBA_W
EVEN_SPLITS = (NSA_W, 6 * NSA_KV_W, 3 * NSA_HEADS, NSA_W, MOBA_W, MOBA_W, MOBA_W, MOBA_W)
EVEN_IN = sum(EVEN_SPLITS)
ODD_SPLITS = (GLA_HEADS * GLA_DK, GLA_HEADS * GLA_DK, GLA_HEADS * GLA_DV, GLA_GATE_RANK, GLA_HEADS * GLA_DV)
ODD_IN = sum(ODD_SPLITS)

kernel_name = "nsa_moba_gla_deepnorm_hybrid"


def _split(h, sizes):
    return jnp.split(h, np.cumsum(sizes)[:-1].tolist(), axis=-1)


def alibi_slopes(n):
    return jnp.asarray(2.0 ** (-8.0 * np.arange(1, n + 1) / n), dtype=jnp.float32)


def masked_softmax(s, valid):
    s = jnp.where(valid, s.astype(jnp.float32), NEG)
    return jax.nn.softmax(s, axis=-1) * valid


def layer_norm(x, g, b):
    xf = x.astype(jnp.float32)
    mu = jnp.mean(xf, axis=-1, keepdims=True)
    var = jnp.mean(jnp.square(xf - mu), axis=-1, keepdims=True)
    return ((xf - mu) * lax.rsqrt(var + LN_EPS) * g + b).astype(x.dtype)


def nsa_mix(q, k_cmp, v_cmp, k_sel, v_sel, k_win, v_win, gates, cmp_pos, cmp_w1, cmp_b1, cmp_w2):
    B, S = q.shape[0], q.shape[1]
    G, Hg, hd = NSA_KV_GROUPS, NSA_HEADS // NSA_KV_GROUPS, HEAD_DIM
    L, d, SB, W, QB = NSA_CMP_BLOCK, NSA_CMP_STRIDE, NSA_SEL_BLOCK, NSA_WINDOW, Q_BLOCK
    scale = hd ** -0.5
    slopes = alibi_slopes(NSA_HEADS).reshape(G, Hg)
    n_cmp = (S - L) // d + 1
    cmp_idx = np.arange(n_cmp)[:, None] * d + np.arange(L)[None, :]
    cmp_end = jnp.asarray(cmp_idx[:, -1])

    def compress(kv, i):
        blocks = kv[:, cmp_idx] + cmp_pos[i][None, None, :, None, :]
        blocks = blocks.transpose(0, 1, 3, 2, 4).reshape(B, n_cmp, G, L * hd)
        hid = jax.nn.silu(blocks @ cmp_w1[i] + cmp_b1[i])
        return hid @ cmp_w2[i]

    kc = compress(k_cmp, 0)
    vc = compress(v_cmp, 1)
    NS = S // SB
    sel_start = np.arange(NS) * SB
    overlap = jnp.asarray(((cmp_idx[:, 0][:, None] <= sel_start[None, :] + SB - 1)
                           & (cmp_idx[:, -1][:, None] >= sel_start[None, :])).astype(np.float32))
    n_top = min(NSA_TOP_N, NS)
    ks_blk = k_sel.reshape(B, NS, SB, G, hd).transpose(0, 3, 1, 2, 4)
    vs_blk = v_sel.reshape(B, NS, SB, G, hd).transpose(0, 3, 1, 2, 4)
    blk_ids = jnp.arange(NS)
    bi = jnp.arange(B)[:, None, None, None]
    gi = jnp.arange(G)[None, :, None, None]
    kw_pad = jnp.pad(k_win, ((0, 0), (W, 0), (0, 0), (0, 0)))
    vw_pad = jnp.pad(v_win, ((0, 0), (W, 0), (0, 0), (0, 0)))

    def block(c):
        q0 = c * QB
        tq = q0 + jnp.arange(QB)
        qc = lax.dynamic_slice_in_dim(q, q0, QB, axis=1).reshape(B, QB, G, Hg, hd)
        gc = lax.dynamic_slice_in_dim(gates, q0, QB, axis=1)
        dist_c = (tq[:, None] - cmp_end[None, :]).astype(jnp.float32)
        s = jnp.einsum('bcgjd,bngd->bgjcn', qc, kc) * scale - slopes[:, :, None, None] * dist_c
        p_cmp = masked_softmax(s, cmp_end[None, :] <= tq[:, None])
        o_cmp = jnp.einsum('bgjcn,bngd->bcgjd', p_cmp.astype(vc.dtype), vc)
        imp = jnp.einsum('bgjcn,ns->bgcs', p_cmp, overlap)
        cur = tq // SB
        forced = ((blk_ids[None, :] == 0) | (blk_ids[None, :] == cur[:, None])
                  | (blk_ids[None, :] == cur[:, None] - 1)).astype(jnp.float32)
        blk_valid = blk_ids[None, :] <= cur[:, None]
        score = jnp.where(blk_valid, imp + FORCE_BONUS * forced, NEG)
        _, idx = lax.top_k(score, n_top)
        kg = ks_blk[bi, gi, idx].reshape(B, G, QB, n_top * SB, hd)
        vg = vs_blk[bi, gi, idx].reshape(B, G, QB, n_top * SB, hd)
        kpos = (idx[..., None] * SB + jnp.arange(SB)).reshape(B, G, QB, n_top * SB)
        dist_s = (tq[:, None] - kpos).astype(jnp.float32)[:, :, None]
        s = jnp.einsum('bcgjd,bgckd->bgjck', qc, kg) * scale - slopes[None, :, :, None, None] * dist_s
        p = masked_softmax(s, (kpos <= tq[:, None])[:, :, None])
        o_sel = jnp.einsum('bgjck,bgckd->bcgjd', p.astype(vg.dtype), vg)
        kw = lax.dynamic_slice_in_dim(kw_pad, q0, W + QB, axis=1)
        vw = lax.dynamic_slice_in_dim(vw_pad, q0, W + QB, axis=1)
        kpos_w = q0 - W + jnp.arange(W + QB)
        rel = tq[:, None] - kpos_w[None, :]
        valid_w = (rel >= 0) & (rel < W) & (kpos_w[None, :] >= 0)
        s = jnp.einsum('bcgjd,bkgd->bgjck', qc, kw) * scale - slopes[:, :, None, None] * rel.astype(jnp.float32)
        p = masked_softmax(s, valid_w)
        o_win = jnp.einsum('bgjck,bkgd->bcgjd', p.astype(vw.dtype), vw)
        g = jax.nn.sigmoid(gc.astype(jnp.float32)).reshape(B, QB, G, Hg, 3)
        o = g[..., 0:1] * o_cmp + g[..., 1:2] * o_sel + g[..., 2:3] * o_win
        return o.reshape(B, QB, NSA_W).astype(q.dtype)

    out = lax.map(block, jnp.arange(S // QB))
    return out.transpose(1, 0, 2, 3).reshape(B, S, NSA_W)


def moba_mix(q, k, v):
    B, S, H, hd = q.shape
    MB, QB = MOBA_BLOCK, Q_BLOCK
    NB = -(-S // MB)
    pad = NB * MB - S
    kp = jnp.pad(k, ((0, 0), (0, pad), (0, 0), (0, 0)))
    vp = jnp.pad(v, ((0, 0), (0, pad), (0, 0), (0, 0)))
    kblk = kp.reshape(B, NB, MB, H, hd)
    kmean = jnp.mean(kblk, axis=2)
    kbt = kblk.transpose(0, 3, 1, 2, 4)
    vbt = vp.reshape(B, NB, MB, H, hd).transpose(0, 3, 1, 2, 4)
    n_top = min(MOBA_TOP_K, NB)
    scale = hd ** -0.5
    slopes = alibi_slopes(H)[:, None, None]
    bi = jnp.arange(B)[:, None, None, None]
    hi = jnp.arange(H)[None, :, None, None]

    def block(c):
        q0 = c * QB
        tq = q0 + jnp.arange(QB)
        qc = lax.dynamic_slice_in_dim(q, q0, QB, axis=1)
        cur = tq // MB
        gate = jnp.einsum('bchd,bnhd->bhcn', qc, kmean).astype(jnp.float32)
        past = jnp.arange(NB)[None, :] < cur[:, None]
        _, idx = lax.top_k(jnp.where(past, gate, NEG), n_top)
        sel_valid = idx < cur[:, None]
        kg = kbt[bi, hi, idx].reshape(B, H, QB, n_top * MB, hd)
        vg = vbt[bi, hi, idx].reshape(B, H, QB, n_top * MB, hd)
        kpos_sel = (idx[..., None] * MB + jnp.arange(MB)).reshape(B, H, QB, n_top * MB)
        valid_sel = jnp.repeat(sel_valid, MB, axis=-1)
        own0 = (q0 // MB) * MB
        ko = lax.dynamic_slice_in_dim(kp, own0, MB, axis=1)
        vo = lax.dynamic_slice_in_dim(vp, own0, MB, axis=1)
        kpos_own = own0 + jnp.arange(MB)
        full = (B, H, QB, MB)
        s = jnp.concatenate([jnp.einsum('bchd,bhckd->bhck', qc, kg),
                             jnp.einsum('bchd,bkhd->bhck', qc, ko)], axis=-1) * scale
        dist = jnp.concatenate([tq[:, None] - kpos_sel,
                                jnp.broadcast_to(tq[:, None] - kpos_own[None, :], full)], axis=-1)
        valid = jnp.concatenate([valid_sel,
                                 jnp.broadcast_to(kpos_own[None, :] <= tq[:, None], full)], axis=-1)
        p = masked_softmax(s - slopes * dist.astype(jnp.float32), valid)
        ns = n_top * MB
        o = (jnp.einsum('bhck,bhckd->bchd', p[..., :ns].astype(vg.dtype), vg)
             + jnp.einsum('bhck,bkhd->bchd', p[..., ns:].astype(vo.dtype), vo))
        return o.reshape(B, QB, H * hd).astype(q.dtype)

    out = lax.map(block, jnp.arange(S // QB))
    return out.transpose(1, 0, 2, 3).reshape(B, S, H * hd)


def gla_mix(q, k, v, log_a):
    B, S, H, dk = q.shape
    dv = v.shape[-1]
    C = GLA_CHUNK
    n = S // C

    def to_chunks(t):
        return t.reshape(B, n, C, H, t.shape[-1]).transpose(1, 0, 3, 2, 4)

    causal = jnp.tril(jnp.ones((C, C), dtype=bool))

    def step(state, inp):
        qc, kc, vc, ac = inp
        b = jnp.cumsum(ac, axis=2)
        inter = jnp.einsum('bhcd,bhde->bhce', qc * jnp.exp(b), state)
        diff = b[:, :, :, None, :] - b[:, :, None, :, :]
        decay = jnp.exp(jnp.where(causal[:, :, None], diff, NEG))
        att = jnp.einsum('bhid,bhjd,bhijd->bhij', qc, kc, decay)
        intra = jnp.einsum('bhij,bhje->bhie', att, vc)
        b_last = b[:, :, -1:, :]
        new_state = (jnp.exp(b_last[:, :, 0, :])[..., None] * state
                     + jnp.einsum('bhcd,bhce->bhde', kc * jnp.exp(b_last - b), vc))
        return new_state, inter + intra

    s0 = jnp.zeros((B, H, dk, dv), jnp.float32)
    _, out = lax.scan(step, s0, (to_chunks(q), to_chunks(k), to_chunks(v), to_chunks(log_a)))
    return out.transpose(1, 0, 3, 2, 4).reshape(B, S, H, dv)


def even_layer(x, w_in, cmp_pos, cmp_w1, cmp_b1, cmp_w2, w_out, ln_g, ln_b):
    B, S, _ = x.shape
    h = x @ w_in
    nq, nkv, ngate, nz, mq, mk, mv, mz = _split(h, EVEN_SPLITS)
    kvs = [t.reshape(B, S, NSA_KV_GROUPS, HEAD_DIM) for t in _split(nkv, (NSA_KV_W,) * 6)]
    o_nsa = nsa_mix(nq.reshape(B, S, NSA_HEADS, HEAD_DIM), *kvs,
                    ngate.reshape(B, S, NSA_HEADS, 3), cmp_pos, cmp_w1, cmp_b1, cmp_w2)
    hs = (B, S, MOBA_HEADS, HEAD_DIM)
    o_moba = moba_mix(mq.reshape(hs), mk.reshape(hs), mv.reshape(hs))
    y = jnp.concatenate([o_nsa * jax.nn.silu(nz), o_moba * jax.nn.silu(mz)], axis=-1) @ w_out
    return layer_norm(DEEPNORM_ALPHA * x + y, ln_g, ln_b)


def odd_layer(x, w_in, gate_w2, gate_b, gn_g, w_out, ln_g, ln_b):
    B, S, _ = x.shape
    h = x @ w_in
    q, k, v, g_lr, z = _split(h, ODD_SPLITS)
    f32 = jnp.float32
    q = q.reshape(B, S, GLA_HEADS, GLA_DK).astype(f32) * (GLA_DK ** -0.5)
    k = k.reshape(B, S, GLA_HEADS, GLA_DK).astype(f32)
    v = v.reshape(B, S, GLA_HEADS, GLA_DV).astype(f32)
    log_a = jax.nn.log_sigmoid((g_lr @ gate_w2 + gate_b).astype(f32)) / GLA_TAU
    o = gla_mix(q, k, v, log_a.reshape(B, S, GLA_HEADS, GLA_DK))
    o = o * lax.rsqrt(jnp.mean(jnp.square(o), axis=-1, keepdims=True) + LN_EPS)
    o = o * gn_g.reshape(GLA_HEADS, GLA_DV)
    y = (o.reshape(B, S, GLA_HEADS * GLA_DV).astype(x.dtype) * jax.nn.silu(z)) @ w_out
    return layer_norm(DEEPNORM_ALPHA * x + y, ln_g, ln_b)


def setup_inputs(seed: int = 0) -> dict:
    key = jax.random.key(seed)
    ks = jax.random.split(key, 16)
    NE = (DEPTH + 1) // 2
    NO = DEPTH // 2
    f32 = jnp.float32

    def nrm(k, shape, scale):
        return jax.random.normal(k, shape, f32) * scale

    L, hd, HID = NSA_CMP_BLOCK, HEAD_DIM, NSA_CMP_HIDDEN
    return {
        "x": jax.random.normal(ks[0], (BATCH, SEQ, D_MODEL), f32),
        "ev_w_in": nrm(ks[1], (NE, D_MODEL, EVEN_IN), D_MODEL ** -0.5),
        "ev_cmp_pos": nrm(ks[2], (NE, 2, L, hd), 0.02),
        "ev_cmp_w1": nrm(ks[3], (NE, 2, L * hd, HID), (L * hd) ** -0.5),
        "ev_cmp_b1": nrm(ks[4], (NE, 2, HID), 0.01),
        "ev_cmp_w2": nrm(ks[5], (NE, 2, HID, hd), HID ** -0.5),
        "ev_w_out": nrm(ks[6], (NE, MIX_W, D_MODEL), DEEPNORM_BETA * MIX_W ** -0.5),
        "ev_ln_g": 1.0 + nrm(ks[7], (NE, D_MODEL), 0.01),
        "ev_ln_b": nrm(ks[8], (NE, D_MODEL), 0.01),
        "od_w_in": nrm(ks[9], (NO, D_MODEL, ODD_IN), D_MODEL ** -0.5),
        "od_gate_w2": nrm(ks[10], (NO, GLA_GATE_RANK, GLA_HEADS * GLA_DK), GLA_GATE_RANK ** -0.5),
        "od_gate_b": nrm(ks[11], (NO, GLA_HEADS * GLA_DK), 0.1),
        "od_gn_g": 1.0 + nrm(ks[12], (NO, GLA_HEADS * GLA_DV), 0.01),
        "od_w_out": nrm(ks[13], (NO, GLA_HEADS * GLA_DV, D_MODEL), DEEPNORM_BETA * (GLA_HEADS * GLA_DV) ** -0.5),
        "od_ln_g": 1.0 + nrm(ks[14], (NO, D_MODEL), 0.01),
        "od_ln_b": nrm(ks[15], (NO, D_MODEL), 0.01),
    }


def reference(x, ev_w_in, ev_cmp_pos, ev_cmp_w1, ev_cmp_b1, ev_cmp_w2, ev_w_out, ev_ln_g, ev_ln_b,
              od_w_in, od_gate_w2, od_gate_b, od_gn_g, od_w_out, od_ln_g, od_ln_b):
    for layer in range(DEPTH):
        i = layer // 2
        if layer % 2 == 0:
            x = even_layer(x, ev_w_in[i], ev_cmp_pos[i], ev_cmp_w1[i], ev_cmp_b1[i], ev_cmp_w2[i],
                           ev_w_out[i], ev_ln_g[i], ev_ln_b[i])
        else:
            x = odd_layer(x, od_w_in[i], od_gate_w2[i], od_gate_b[i], od_gn_g[i],
                          od_w_out[i], od_ln_g[i], od_ln_b[i])
    return x
```

```python
import functools

import numpy as np
import jax
import jax.numpy as jnp
from jax import lax
from jax.experimental import pallas as pl
from jax.experimental.pallas import tpu as pltpu

F32 = jnp.float32
BF16 = jnp.bfloat16

HEAD_DIM = 64
NSA_HEADS = 8
NSA_KV_GROUPS = 2
NSA_HG = NSA_HEADS // NSA_KV_GROUPS
NSA_CMP_BLOCK = 32
NSA_CMP_STRIDE = 16
NSA_CMP_HIDDEN = 128
NSA_SEL_BLOCK = 64
NSA_TOP_N = 16
NSA_WINDOW = 512
MOBA_HEADS = 8
MOBA_BLOCK = 256
MOBA_TOP_K = 3
GLA_HEADS = 4
GLA_DK = 128
GLA_DV = 256
GLA_GATE_RANK = 16
GLA_TAU = 16.0
GLA_CHUNK = 64
GLA_SUB = 16
LN_EPS = 1e-5
NEG = -1e30
FORCE_BONUS = 1e4

NSA_W = NSA_HEADS * HEAD_DIM
NSA_KV_W = NSA_KV_GROUPS * HEAD_DIM
MOBA_W = MOBA_HEADS * HEAD_DIM
EVEN_SPLITS = (NSA_W, 6 * NSA_KV_W, 3 * NSA_HEADS, NSA_W, MOBA_W, MOBA_W, MOBA_W, MOBA_W)
ODD_SPLITS = (GLA_HEADS * GLA_DK, GLA_HEADS * GLA_DK, GLA_HEADS * GLA_DV, GLA_GATE_RANK, GLA_HEADS * GLA_DV)

LANES = 128
FEAT = 64
VMEM_LIMIT = 48 * 1024 * 1024


def _dot(a, b):
    return jnp.dot(a, b, preferred_element_type=F32)


def _dot_nt(a, b):
    return lax.dot_general(a, b, (((1,), (1,)), ((), ())), preferred_element_type=F32)


def _dot_tn(a, b):
    return lax.dot_general(a, b, (((0,), (0,)), ((), ())), preferred_element_type=F32)


def _hilo(a):
    hi = a.astype(BF16)
    lo = (a - hi.astype(F32)).astype(BF16)
    return hi, lo


def _pow2_neg(n):
    return lax.bitcast_convert_type(lax.shift_left(127 - n, 23), F32)


def _sigmoid(x):
    return 1.0 / (1.0 + jnp.exp(-x))


def _layer_norm(r, g, b):
    mu = jnp.mean(r, axis=-1, keepdims=True)
    d = r - mu
    var = jnp.mean(d * d, axis=-1, keepdims=True)
    return d * lax.rsqrt(var + LN_EPS) * g + b


def _proj_kernel(x_ref, w_ref, *out_refs, splits):
    xb = x_ref[...].astype(BF16)
    off = 0
    for o_ref, n in zip(out_refs, splits):
        o_ref[...] = _dot(xb, w_ref[:, off:off + n]).astype(o_ref.dtype)
        off += n


def _proj(x2d, w_bf16, splits, dtypes, tm=256):
    T, D = x2d.shape
    N = w_bf16.shape[1]
    assert sum(splits) == N and all(n % LANES == 0 for n in splits) and T % tm == 0
    return pl.pallas_call(
        functools.partial(_proj_kernel, splits=splits),
        grid=(T // tm,),
        in_specs=[pl.BlockSpec((tm, D), lambda i: (i, 0)),
                  pl.BlockSpec((D, N), lambda i: (0, 0))],
        out_specs=[pl.BlockSpec((tm, n), lambda i: (i, 0)) for n in splits],
        out_shape=[jax.ShapeDtypeStruct((T, n), dt) for n, dt in zip(splits, dtypes)],
        compiler_params=pltpu.CompilerParams(dimension_semantics=("parallel",),
                                             vmem_limit_bytes=VMEM_LIMIT),
        name="proj",
    )(x2d, w_bf16)


def _cmp_mlp_kernel(x_ref, w1_ref, pos_ref, b1_ref, w2_ref, o_ref, *, ncmp):
    x = x_ref[...]
    nc, half = x.shape
    a = _dot(x, w1_ref[0:half, :])
    bm = _dot(x, w1_ref[half:2 * half, :])
    phi, plo = _hilo(jnp.broadcast_to(pos_ref[...], (8, 2 * half)))
    c = (_dot(phi, w1_ref[...]) + _dot(plo, w1_ref[...]))[0:1] + b1_ref[...]
    hid = a + pltpu.roll(bm, nc - 1, axis=0) + c
    hid = hid * _sigmoid(hid)
    row = lax.broadcasted_iota(jnp.int32, hid.shape, 0)
    hid = jnp.where(row < ncmp, hid, 0.0)
    o_ref[...] = _dot(hid.astype(BF16), w2_ref[...]).astype(o_ref.dtype)


def _cmp_mlp(xc, w1, pos, b1, w2, ncmp):
    two, BG, NC, CW = xc.shape
    hid = w1.shape[-1]
    hd = w2.shape[-1]
    return pl.pallas_call(
        functools.partial(_cmp_mlp_kernel, ncmp=ncmp),
        grid=(two, BG),
        in_specs=[pl.BlockSpec((None, None, NC, CW), lambda i, s: (i, s, 0, 0)),
                  pl.BlockSpec((None, 2 * CW, hid), lambda i, s: (i, 0, 0)),
                  pl.BlockSpec((None, 1, 2 * CW), lambda i, s: (i, 0, 0)),
                  pl.BlockSpec((None, 1, hid), lambda i, s: (i, 0, 0)),
                  pl.BlockSpec((None, hid, hd), lambda i, s: (i, 0, 0))],
        out_specs=pl.BlockSpec((None, None, NC, hd), lambda i, s: (i, s, 0, 0)),
        out_shape=jax.ShapeDtypeStruct((two, BG, NC, hd), BF16),
        compiler_params=pltpu.CompilerParams(dimension_semantics=("parallel", "parallel"),
                                             vmem_limit_bytes=VMEM_LIMIT),
        name="cmp_mlp",
    )(xc, w1, pos, b1, w2)


def _cmp_attn_kernel(q_ref, kc_ref, vc_ref, ov_ref, o_ref, bias_ref, sc_ref,
                     *, tq, nh, groups, ncmp, ns, ntop):
    g = pl.program_id(0) % groups
    qi = pl.program_id(1)
    R = nh * tq
    nc = kc_ref.shape[0]
    s = _dot_nt(q_ref[:, 0:HEAD_DIM], kc_ref[...])
    row = lax.broadcasted_iota(jnp.int32, (R, nc), 0)
    col = lax.broadcasted_iota(jnp.int32, (R, nc), 1)
    t = qi * tq + row % tq
    slope = _pow2_neg(g * nh + row // tq + 1)
    cmp_end = col * NSA_CMP_STRIDE + (NSA_CMP_BLOCK - 1)
    s = s - slope * (t - cmp_end).astype(F32)
    valid = (cmp_end <= t) & (col < ncmp)
    s = jnp.where(valid, s, NEG)
    e = jnp.exp(s - jnp.max(s, axis=1, keepdims=True))
    e = jnp.where(col < ncmp, e, 0.0)
    p = e / jnp.sum(e, axis=1, keepdims=True) * valid.astype(F32)
    o_ref[...] = _dot(p.astype(BF16), vc_ref[...])

    psum = p[0:tq]
    for hh in range(1, nh):
        psum = psum + p[hh * tq:(hh + 1) * tq]
    phi, plo = _hilo(psum)
    imp = _dot(phi, ov_ref[...]) + _dot(plo, ov_ref[...])
    imp_t = imp.T

    nsp = sc_ref.shape[0]
    j = lax.broadcasted_iota(jnp.int32, (nsp, tq), 0)
    tl = qi * tq + lax.broadcasted_iota(jnp.int32, (nsp, tq), 1)
    cur = tl // NSA_SEL_BLOCK
    forced = ((j == 0) | (j == cur) | (j == cur - 1)).astype(F32)
    sc = jnp.where(j <= cur, imp_t[0:nsp] + FORCE_BONUS * forced, NEG)
    sc_ref[...] = sc
    rank = jnp.zeros((nsp, tq), jnp.int32)
    for jp in range(ns):
        r = sc_ref[pl.ds(jp, 1), :]
        beats = (r > sc) | ((r == sc) & (j > jp))
        rank = rank + jnp.where(beats, 1, 0)
    sel = (rank < ntop) & (j <= cur)
    bias_t = jnp.where(sel, 0.0, NEG)
    pad = jnp.zeros((FEAT, tq), F32)
    pieces = [pad, bias_t]
    if nsp < FEAT:
        pieces.append(jnp.zeros((FEAT - nsp, tq), F32))
    bias_ref[...] = jnp.concatenate(pieces, axis=0).T.astype(bias_ref.dtype)


def _cmp_attn(q_nsa, kc, vc, ov, *, S, tq, groups):
    BG = q_nsa.shape[0]
    nh = NSA_HG
    R = nh * tq
    NC = kc.shape[1]
    ncmp = (S - NSA_CMP_BLOCK) // NSA_CMP_STRIDE + 1
    ns = S // NSA_SEL_BLOCK
    assert ns <= FEAT
    nsp = max(8, ns)
    ntop = min(NSA_TOP_N, ns)
    return pl.pallas_call(
        functools.partial(_cmp_attn_kernel, tq=tq, nh=nh, groups=groups, ncmp=ncmp, ns=ns, ntop=ntop),
        grid=(BG, S // tq),
        in_specs=[pl.BlockSpec((None, R, LANES), lambda s, i: (s, i, 0)),
                  pl.BlockSpec((None, NC, HEAD_DIM), lambda s, i: (s, 0, 0)),
                  pl.BlockSpec((None, NC, HEAD_DIM), lambda s, i: (s, 0, 0)),
                  pl.BlockSpec((NC, LANES), lambda s, i: (0, 0))],
        out_specs=[pl.BlockSpec((None, R, HEAD_DIM), lambda s, i: (s, i, 0)),
                   pl.BlockSpec((None, tq, LANES), lambda s, i: (s, i, 0))],
        out_shape=[jax.ShapeDtypeStruct((BG, S * nh, HEAD_DIM), F32),
                   jax.ShapeDtypeStruct((BG, S, LANES), BF16)],
        scratch_shapes=[pltpu.VMEM((nsp, tq), F32)],
        compiler_params=pltpu.CompilerParams(dimension_semantics=("parallel", "parallel"),
                                             vmem_limit_bytes=VMEM_LIMIT),
        name="cmp_attn",
    )(q_nsa, kc, vc, ov)


def _moba_gate_kernel(q_ref, k_ref, a_ref, bias_ref, sc_ref, *, tq, nb, ntop):
    qi = pl.program_id(1)
    kmean = _dot(a_ref[...], k_ref[...])
    khi, klo = _hilo(kmean)
    q = q_ref[:, 0:HEAD_DIM]
    g_t = _dot_nt(khi, q) + _dot_nt(klo, q)
    nbp = g_t.shape[0]
    n = lax.broadcasted_iota(jnp.int32, (nbp, tq), 0)
    t = qi * tq + lax.broadcasted_iota(jnp.int32, (nbp, tq), 1)
    cur = t // MOBA_BLOCK
    sc = jnp.where(n < cur, g_t, NEG)
    sc_ref[...] = sc
    rank = jnp.zeros((nbp, tq), jnp.int32)
    for jp in range(nb):
        r = sc_ref[pl.ds(jp, 1), :]
        beats = (r > sc) | ((r == sc) & (n > jp))
        rank = rank + jnp.where(beats, 1, 0)
    sel = ((rank < ntop) & (n < cur)) | (n == cur)
    bias_t = jnp.where(sel, 0.0, NEG)
    full = jnp.concatenate([jnp.zeros((FEAT, tq), F32), bias_t,
                            jnp.zeros((FEAT - nbp, tq), F32)], axis=0)
    bias_ref[...] = full.T.astype(bias_ref.dtype)


def _moba_gate(q_m, k_m, avg, *, S, tq):
    BH = q_m.shape[0]
    nb = S // MOBA_BLOCK
    nbp = avg.shape[0]
    ntop = min(MOBA_TOP_K, nb)
    return pl.pallas_call(
        functools.partial(_moba_gate_kernel, tq=tq, nb=nb, ntop=ntop),
        grid=(BH, S // tq),
        in_specs=[pl.BlockSpec((None, tq, LANES), lambda s, i: (s, i, 0)),
                  pl.BlockSpec((None, S, HEAD_DIM), lambda s, i: (s, 0, 0)),
                  pl.BlockSpec((nbp, S), lambda s, i: (0, 0))],
        out_specs=pl.BlockSpec((None, tq, LANES), lambda s, i: (s, i, 0)),
        out_shape=jax.ShapeDtypeStruct((BH, S, LANES), BF16),
        scratch_shapes=[pltpu.VMEM((nbp, tq), F32)],
        compiler_params=pltpu.CompilerParams(dimension_semantics=("parallel", "parallel"),
                                             vmem_limit_bytes=VMEM_LIMIT),
        name="moba_gate",
    )(q_m, k_m, avg)


def _flash_kernel(qi_tab, ki_tab, first_tab, last_tab, q_ref, bias_ref, k_ref, v_ref, o_ref,
                  qa_ref, m_ref, acc_ref, *, tq, tk, nh, spb, window):
    sid = pl.program_id(0)
    p = pl.program_id(1)
    qi = qi_tab[p]
    ki = ki_tab[p]
    R = nh * tq

    @pl.when(first_tab[p] == 1)
    def _init():
        lane = lax.broadcasted_iota(jnp.int32, (tq, LANES), 1)
        t = qi * tq + lax.broadcasted_iota(jnp.int32, (tq, LANES), 0)
        tb = (t // FEAT).astype(F32)
        tr = (t % FEAT).astype(F32)
        bias = bias_ref[...]
        for hh in range(nh):
            n = jnp.full((tq, LANES), (sid % spb) * nh + hh + 1, jnp.int32)
            slope = _pow2_neg(n)
            feats = jnp.where(lane == 0, -FEAT * slope * tb,
                              jnp.where(lane == 1, -slope * tr,
                                        jnp.where(lane < 4, slope, 0.0)))
            rows = pl.ds(hh * tq, tq)
            qa_ref[rows, 0:LANES] = q_ref[rows, :] + bias
            qa_ref[rows, LANES:2 * LANES] = feats.astype(BF16)
        m_ref[...] = jnp.full(m_ref.shape, NEG, F32)
        acc_ref[...] = jnp.zeros(acc_ref.shape, F32)

    s = _dot_nt(qa_ref[...], k_ref[...])
    row = lax.broadcasted_iota(jnp.int32, (R, tk), 0)
    col = lax.broadcasted_iota(jnp.int32, (R, tk), 1)
    rel = (qi * tq - ki * tk) + row % tq - col
    ok = rel >= 0
    if window is not None:
        ok = ok & (rel < window)
    s = jnp.where(ok, s, NEG)
    m_prev = m_ref[:, 0:1]
    m_new = jnp.maximum(m_prev, jnp.max(s, axis=1, keepdims=True))
    alpha = jnp.exp(m_prev - m_new)
    pr = jnp.exp(s - m_new)
    acc_ref[...] = alpha * acc_ref[...] + _dot(pr.astype(BF16), v_ref[...])
    m_ref[...] = jnp.broadcast_to(m_new, m_ref.shape)

    @pl.when(last_tab[p] == 1)
    def _fin():
        acc = acc_ref[...]
        o_ref[...] = acc / acc[:, HEAD_DIM:HEAD_DIM + 1]


def _pair_tables(S, tq, tk, window):
    qi_l, ki_l, first_l, last_l = [], [], [], []
    for qi in range(S // tq):
        t_lo, t_hi = qi * tq, qi * tq + tq - 1
        s_lo = 0 if window is None else max(0, t_lo - window + 1)
        k0, k1 = s_lo // tk, t_hi // tk
        for ki in range(k0, k1 + 1):
            qi_l.append(qi)
            ki_l.append(ki)
            first_l.append(int(ki == k0))
            last_l.append(int(ki == k1))
    return [jnp.asarray(np.asarray(a, np.int32)) for a in (qi_l, ki_l, first_l, last_l)]


def _flash(q, bias, k_aug, v_aug, *, S, tq, tk, nh, spb, window):
    streams = q.shape[0]
    R = nh * tq
    tabs = _pair_tables(S, tq, tk, window)
    npairs = int(tabs[0].shape[0])
    grid_spec = pltpu.PrefetchScalarGridSpec(
        num_scalar_prefetch=4,
        grid=(streams, npairs),
        in_specs=[pl.BlockSpec((None, R, LANES), lambda s, p, qt, kt, ft, lt: (s, qt[p], 0)),
                  pl.BlockSpec((None, tq, LANES), lambda s, p, qt, kt, ft, lt: (s, qt[p], 0)),
                  pl.BlockSpec((None, tk, 2 * LANES), lambda s, p, qt, kt, ft, lt: (s, kt[p], 0)),
                  pl.BlockSpec((None, tk, LANES), lambda s, p, qt, kt, ft, lt: (s, kt[p], 0))],
        out_specs=pl.BlockSpec((None, R, LANES), lambda s, p, qt, kt, ft, lt: (s, qt[p], 0)),
        scratch_shapes=[pltpu.VMEM((R, 2 * LANES), BF16),
                        pltpu.VMEM((R, LANES), F32),
                        pltpu.VMEM((R, LANES), F32)],
    )
    return pl.pallas_call(
        functools.partial(_flash_kernel, tq=tq, tk=tk, nh=nh, spb=spb, window=window),
        grid_spec=grid_spec,
        out_shape=jax.ShapeDtypeStruct((streams, S * nh, LANES), F32),
        compiler_params=pltpu.CompilerParams(dimension_semantics=("parallel", "arbitrary"),
                                             vmem_limit_bytes=VMEM_LIMIT),
        name="flash",
    )(*tabs, q, bias, k_aug, v_aug)


def _even_out_kernel(x_ref, ocmp_ref, osel_ref, owin_ref, omoba_ref, gate_ref, nz_ref, mz_ref,
                     e_ref, w_ref, g_ref, b_ref, o_ref, *, alpha):
    sig = _sigmoid(gate_ref[...])
    shi, slo = _hilo(sig)
    branches = (ocmp_ref, osel_ref, owin_ref)
    o_nsa = None
    for br in range(3):
        gb = _dot(shi, e_ref[br]) + _dot(slo, e_ref[br])
        term = gb * branches[br][...]
        o_nsa = term if o_nsa is None else o_nsa + term
    nz = nz_ref[...]
    mz = mz_ref[...]
    a = jnp.concatenate([o_nsa * (nz * _sigmoid(nz)), omoba_ref[...] * (mz * _sigmoid(mz))], axis=1)
    y = _dot(a.astype(BF16), w_ref[...])
    o_ref[...] = _layer_norm(alpha * x_ref[...] + y, g_ref[...], b_ref[...])


def _even_out(x2d, ocmp, osel, owin, omoba, gate, nz, mz, expand, w_out, ln_g, ln_b, alpha, tm=256):
    T, D = x2d.shape
    W = ocmp.shape[1]
    row = lambda n: pl.BlockSpec((tm, n), lambda i: (i, 0))
    return pl.pallas_call(
        functools.partial(_even_out_kernel, alpha=alpha),
        grid=(T // tm,),
        in_specs=[row(D), row(W), row(W), row(W), row(W), row(LANES), row(W), row(W),
                  pl.BlockSpec((3, LANES, W), lambda i: (0, 0, 0)),
                  pl.BlockSpec((2 * W, D), lambda i: (0, 0)),
                  pl.BlockSpec((1, D), lambda i: (0, 0)),
                  pl.BlockSpec((1, D), lambda i: (0, 0))],
        out_specs=row(D),
        out_shape=jax.ShapeDtypeStruct((T, D), F32),
        compiler_params=pltpu.CompilerParams(dimension_semantics=("parallel",),
                                             vmem_limit_bytes=VMEM_LIMIT),
        name="even_out",
    )(x2d, ocmp, osel, owin, omoba, gate, nz, mz, expand, w_out, ln_g, ln_b)


def _gla_kernel(q_ref, k_ref, v_ref, glr_ref, w2_ref, gb_ref, tril_ref, ones_ref, o_ref,
                st_ref, b_ref, kf_ref, *, scale):
    C, H, dk, dv, SUB = GLA_CHUNK, GLA_HEADS, GLA_DK, GLA_DV, GLA_SUB

    @pl.when(pl.program_id(1) == 0)
    def _init():
        st_ref[...] = jnp.zeros(st_ref.shape, F32)

    ghi, glo = _hilo(glr_ref[...])
    whi, wlo = _hilo(w2_ref[...])
    pre = _dot(ghi, whi) + _dot(glo, whi) + _dot(ghi, wlo) + gb_ref[...]
    la = (jnp.minimum(pre, 0.0) - jnp.log1p(jnp.exp(-jnp.abs(pre)))) * (1.0 / GLA_TAU)
    lhi, llo = _hilo(la)
    b_all = _dot(tril_ref[...], lhi) + _dot(tril_ref[...], llo)
    b_ref[...] = b_all
    kf_ref[...] = k_ref[...].astype(F32)
    rows = lax.broadcasted_iota(jnp.int32, (C, dk), 0)
    sub_i = lax.broadcasted_iota(jnp.int32, (SUB, dk), 0)

    for h in range(H):
        sl = pl.ds(h * dk, dk)
        bh = b_ref[:, sl]
        kf = kf_ref[:, sl]
        qf = q_ref[:, sl].astype(F32) * scale
        vh = v_ref[:, pl.ds(h * dv, dv)]
        st = st_ref[h]
        inter = _dot_nt((qf * jnp.exp(bh)).astype(BF16), st.astype(BF16))
        bl = b_ref[pl.ds(C - 1, 1), sl]
        kd = kf * jnp.exp(bl - bh)

        def off_block(r0, r1, c0, c1):
            ref_row = b_ref[pl.ds(r0, 1), sl]
            qt = qf[r0:r1] * jnp.exp(bh[r0:r1] - ref_row)
            ek = jnp.where((rows >= c0) & (rows < c1), ref_row - bh, NEG)
            kt = kf * jnp.exp(ek)
            return _dot_nt(qt.astype(BF16), kt.astype(BF16))

        lower = off_block(2 * SUB, C, 0, 2 * SUB)
        blocks = [None, off_block(SUB, 2 * SUB, 0, SUB), lower[0:SUB],
                  lower[SUB:2 * SUB] + off_block(3 * SUB, C, 2 * SUB, 3 * SUB)]
        for blk in range(C // SUB):
            r0 = blk * SUB
            q_i = qf[r0:r0 + SUB]
            b_i = bh[r0:r0 + SUB]
            pieces = []
            for jj in range(SUB):
                b_j = b_ref[pl.ds(r0 + jj, 1), sl]
                k_j = kf_ref[pl.ds(r0 + jj, 1), sl]
                diff = jnp.where(sub_i >= jj, b_i - b_j, NEG)
                pieces.append((q_i * k_j * jnp.exp(diff)).astype(BF16))
            diag = _dot(jnp.concatenate(pieces, axis=1), ones_ref[blk])
            blocks[blk] = diag if blocks[blk] is None else blocks[blk] + diag
        att = jnp.concatenate(blocks, axis=0)
        o_ref[:, pl.ds(h * dv, dv)] = inter + _dot(att.astype(BF16), vh)
        st_ref[h] = st * jnp.exp(bl) + _dot_tn(vh, kd.astype(BF16))


def _gla(q, k, v, glr, w2p, gb, tril, ones_blk, *, B, S):
    C, H, dk, dv = GLA_CHUNK, GLA_HEADS, GLA_DK, GLA_DV
    nchunk = S // C
    tok = lambda n: pl.BlockSpec((C, n), lambda b, c: (b * nchunk + c, 0))
    const = lambda shape: pl.BlockSpec(shape, lambda b, c: (0,) * len(shape))
    return pl.pallas_call(
        functools.partial(_gla_kernel, scale=float(dk) ** -0.5),
        grid=(B, nchunk),
        in_specs=[tok(H * dk), tok(H * dk), tok(H * dv), tok(LANES),
                  const(w2p.shape), const(gb.shape), const(tril.shape), const(ones_blk.shape)],
        out_specs=tok(H * dv),
        out_shape=jax.ShapeDtypeStruct((B * S, H * dv), F32),
        scratch_shapes=[pltpu.VMEM((H, dv, dk), F32),
                        pltpu.VMEM((C, H * dk), F32),
                        pltpu.VMEM((C, H * dk), F32)],
        compiler_params=pltpu.CompilerParams(dimension_semantics=("parallel", "arbitrary"),
                                             vmem_limit_bytes=VMEM_LIMIT),
        name="gla",
    )(q, k, v, glr, w2p, gb, tril, ones_blk)


def _odd_out_kernel(x_ref, o_ref_in, z_ref, gn_ref, w_ref, g_ref, b_ref, out_ref, *, alpha):
    H, dv = GLA_HEADS, GLA_DV
    parts = []
    for h in range(H):
        sl = pl.ds(h * dv, dv)
        oh = o_ref_in[:, sl]
        r = lax.rsqrt(jnp.mean(oh * oh, axis=-1, keepdims=True) + LN_EPS)
        z = z_ref[:, sl]
        parts.append((oh * r * gn_ref[:, sl] * (z * _sigmoid(z))).astype(BF16))
    y = _dot(jnp.concatenate(parts, axis=1), w_ref[...])
    out_ref[...] = _layer_norm(alpha * x_ref[...] + y, g_ref[...], b_ref[...])


def _odd_out(x2d, o, z, gn_g, w_out, ln_g, ln_b, alpha, tm=256):
    T, D = x2d.shape
    W = o.shape[1]
    row = lambda n: pl.BlockSpec((tm, n), lambda i: (i, 0))
    return pl.pallas_call(
        functools.partial(_odd_out_kernel, alpha=alpha),
        grid=(T // tm,),
        in_specs=[row(D), row(W), row(W),
                  pl.BlockSpec((1, W), lambda i: (0, 0)),
                  pl.BlockSpec((W, D), lambda i: (0, 0)),
                  pl.BlockSpec((1, D), lambda i: (0, 0)),
                  pl.BlockSpec((1, D), lambda i: (0, 0))],
        out_specs=row(D),
        out_shape=jax.ShapeDtypeStruct((T, D), F32),
        compiler_params=pltpu.CompilerParams(dimension_semantics=("parallel",),
                                             vmem_limit_bytes=VMEM_LIMIT),
        name="odd_out",
    )(x2d, o, z, gn_g, w_out, ln_g, ln_b)


def _key_features(S, block):
    s = np.arange(S)
    f = np.zeros((S, 3 * FEAT), np.float32)
    f[s, s // block] = 1.0
    f[:, FEAT + 0] = 1.0
    f[:, FEAT + 1] = 1.0
    f[:, FEAT + 2] = FEAT * (s // FEAT)
    f[:, FEAT + 3] = s % FEAT
    return jnp.asarray(f, BF16)


def _aug_k(k, feats):
    return jnp.concatenate([k, jnp.broadcast_to(feats[None], (k.shape[0],) + feats.shape)], axis=-1)


def _aug_v(v):
    ones = jnp.ones(v.shape[:-1] + (1,), v.dtype)
    zeros = jnp.zeros(v.shape[:-1] + (LANES - v.shape[-1] - 1,), v.dtype)
    return jnp.concatenate([v, ones, zeros], axis=-1)


def _pad_lanes(a):
    return jnp.pad(a, ((0, 0),) * (a.ndim - 1) + ((0, LANES - a.shape[-1]),))


def _even_layer(x2d, B, S, w_in, cmp_pos, cmp_w1, cmp_b1, cmp_w2, w_out, ln_g, ln_b, alpha):
    T, D = x2d.shape
    G, Hg, hd = NSA_KV_GROUPS, NSA_HG, HEAD_DIM
    scale = hd ** -0.5
    wq, wkv, wgate, wnz, wmq, wmk, wmv, wmz = jnp.split(w_in, np.cumsum(EVEN_SPLITS)[:-1].tolist(), axis=1)
    w_perm = jnp.concatenate([wq * scale, wkv, wnz, wmq * scale, wmk, wmv, wmz, _pad_lanes(wgate)],
                             axis=1).astype(BF16)
    nq, nkv, nz, mq, mk, mv, mz, gate = _proj(
        x2d, w_perm, (NSA_W, 6 * NSA_KV_W, NSA_W, MOBA_W, MOBA_W, MOBA_W, MOBA_W, LANES),
        (BF16, BF16, F32, BF16, BF16, BF16, F32, F32))

    tq = 128
    q_nsa = _pad_lanes(nq.reshape(B, S // tq, tq, G, Hg, hd).transpose(0, 3, 1, 4, 2, 5)
                       .reshape(B * G, S * Hg, hd))
    kv = nkv.reshape(B, S, 6, G, hd).transpose(2, 0, 3, 1, 4).reshape(6, B * G, S, hd)
    k_cmp, v_cmp, k_sel, v_sel, k_win, v_win = (kv[i] for i in range(6))

    ncmp = (S - NSA_CMP_BLOCK) // NSA_CMP_STRIDE + 1
    NC = S // NSA_CMP_STRIDE
    xc = jnp.stack([k_cmp, v_cmp]).reshape(2, B * G, NC, NSA_CMP_STRIDE * hd)
    kvc = _cmp_mlp(xc, cmp_w1.astype(BF16), cmp_pos.reshape(2, 1, NSA_CMP_BLOCK * hd),
                   cmp_b1.reshape(2, 1, NSA_CMP_HIDDEN), cmp_w2.astype(BF16), ncmp)

    ns = S // NSA_SEL_BLOCK
    cmp_start = np.arange(NC) * NSA_CMP_STRIDE
    sel_start = np.arange(LANES) * NSA_SEL_BLOCK
    ov = ((cmp_start[:, None] <= sel_start[None, :] + NSA_SEL_BLOCK - 1)
          & (cmp_start[:, None] + NSA_CMP_BLOCK - 1 >= sel_start[None, :])
          & (np.arange(NC)[:, None] < ncmp) & (np.arange(LANES)[None, :] < ns))
    o_cmp, bias_sel = _cmp_attn(q_nsa, kvc[0], kvc[1], jnp.asarray(ov, BF16), S=S, tq=tq, groups=G)

    tk = min(512, S)
    feats_sel = _key_features(S, NSA_SEL_BLOCK)
    o_sel = _flash(q_nsa, bias_sel, _aug_k(k_sel, feats_sel), _aug_v(v_sel),
                   S=S, tq=tq, tk=tk, nh=Hg, spb=G, window=None)
    zero_bias = jnp.zeros((B * G, S, LANES), BF16)
    o_win = _flash(q_nsa, zero_bias, _aug_k(k_win, feats_sel), _aug_v(v_win),
                   S=S, tq=tq, tk=tk, nh=Hg, spb=G, window=NSA_WINDOW)

    def nsa_tokens(o):
        return (o[..., :hd].reshape(B, G, S // tq, Hg, tq, hd).transpose(0, 2, 4, 1, 3, 5)
                .reshape(T, NSA_W))

    H = MOBA_HEADS
    heads = lambda a: a.reshape(B, S, H, hd).transpose(0, 2, 1, 3).reshape(B * H, S, hd)
    q_m, k_m, v_m = _pad_lanes(heads(mq)), heads(mk), heads(mv)
    tqm = min(512, S)
    nb = S // MOBA_BLOCK
    nbp = 16
    assert nb <= nbp
    avg = np.zeros((nbp, S), np.float32)
    avg[np.arange(S) // MOBA_BLOCK, np.arange(S)] = 1.0 / MOBA_BLOCK
    bias_m = _moba_gate(q_m, k_m, jnp.asarray(avg, BF16), S=S, tq=tqm)
    o_m = _flash(q_m, bias_m, _aug_k(k_m, _key_features(S, MOBA_BLOCK)), _aug_v(v_m),
                 S=S, tq=tqm, tk=tk, nh=1, spb=H, window=None)
    o_moba = o_m[..., :hd].reshape(B, H, S, hd).transpose(0, 2, 1, 3).reshape(T, MOBA_W)

    c = np.arange(LANES)[:, None]
    n = np.arange(NSA_W)[None, :]
    expand = jnp.asarray(np.stack([(c == 3 * (n // hd) + br) for br in range(3)]), BF16)
    return _even_out(x2d, nsa_tokens(o_cmp), nsa_tokens(o_sel), nsa_tokens(o_win), o_moba, gate, nz, mz,
                     expand, w_out.astype(BF16), ln_g.reshape(1, D), ln_b.reshape(1, D), alpha)


def _odd_layer(x2d, B, S, w_in, gate_w2, gate_b, gn_g, w_out, ln_g, ln_b, alpha):
    T, D = x2d.shape
    H, dk, dv, C, SUB = GLA_HEADS, GLA_DK, GLA_DV, GLA_CHUNK, GLA_SUB
    wq, wk, wv, wg, wz = jnp.split(w_in, np.cumsum(ODD_SPLITS)[:-1].tolist(), axis=1)
    w_perm = jnp.concatenate([wq, wk, wv, wz, _pad_lanes(wg)], axis=1).astype(BF16)
    q, k, v, z, glr = _proj(x2d, w_perm, (H * dk, H * dk, H * dv, H * dv, LANES),
                            (BF16, BF16, BF16, F32, F32))
    w2p = jnp.pad(gate_w2, ((0, LANES - GLA_GATE_RANK), (0, 0)))
    tril = jnp.asarray(np.tril(np.ones((C, C), np.float32)), BF16)
    ones_blk = np.zeros((C // SUB, SUB * dk, C), np.float32)
    for blk in range(C // SUB):
        for jj in range(SUB):
            ones_blk[blk, jj * dk:(jj + 1) * dk, blk * SUB + jj] = 1.0
    o = _gla(q, k, v, glr, w2p, gate_b.reshape(1, H * dk), tril, jnp.asarray(ones_blk, BF16), B=B, S=S)
    return _odd_out(x2d, o, z, gn_g.reshape(1, H * dv), w_out.astype(BF16),
                    ln_g.reshape(1, D), ln_b.reshape(1, D), alpha)


def kernel(x, ev_w_in, ev_cmp_pos, ev_cmp_w1, ev_cmp_b1, ev_cmp_w2, ev_w_out, ev_ln_g, ev_ln_b,
           od_w_in, od_gate_w2, od_gate_b, od_gn_g, od_w_out, od_ln_g, od_ln_b):
    B, S, D = x.shape
    depth = ev_w_in.shape[0] + od_w_in.shape[0]
    alpha = (2.0 * depth) ** 0.25
    h = x.reshape(B * S, D)
    for layer in range(depth):
        i = layer // 2
        if layer % 2 == 0:
            h = _even_layer(h, B, S, ev_w_in[i], ev_cmp_pos[i], ev_cmp_w1[i], ev_cmp_b1[i], ev_cmp_w2[i],
                            ev_w_out[i], ev_ln_g[i], ev_ln_b[i], alpha)
        else:
            h = _odd_layer(h, B, S, od_w_in[i], od_gate_w2[i], od_gate_b[i], od_gn_g[i],
                           od_w_out[i], od_ln_g[i], od_ln_b[i], alpha)
    return h.reshape(B, S, D)
```

```python
import functools

import numpy as np
import jax
import jax.numpy as jnp
from jax import lax
from jax.experimental import pallas as pl
from jax.experimental.pallas import tpu as pltpu

F32 = jnp.float32
BF16 = jnp.bfloat16

HEAD_DIM = 64
NSA_HEADS = 8
NSA_KV_GROUPS = 2
NSA_HG = NSA_HEADS // NSA_KV_GROUPS
NSA_CMP_BLOCK = 32
NSA_CMP_STRIDE = 16
NSA_CMP_HIDDEN = 128
NSA_SEL_BLOCK = 64
NSA_TOP_N = 16
NSA_WINDOW = 512
MOBA_HEADS = 8
MOBA_BLOCK = 256
MOBA_TOP_K = 3
GLA_HEADS = 4
GLA_DK = 128
GLA_DV = 256
GLA_GATE_RANK = 16
GLA_TAU = 16.0
GLA_CHUNK = 64
GLA_SUB = 16
LN_EPS = 1e-5
NEG = -1e30
FORCE_BONUS = 1e4

NSA_W = NSA_HEADS * HEAD_DIM
NSA_KV_W = NSA_KV_GROUPS * HEAD_DIM
MOBA_W = MOBA_HEADS * HEAD_DIM
EVEN_SPLITS = (NSA_W, 6 * NSA_KV_W, 3 * NSA_HEADS, NSA_W, MOBA_W, MOBA_W, MOBA_W, MOBA_W)
ODD_SPLITS = (GLA_HEADS * GLA_DK, GLA_HEADS * GLA_DK, GLA_HEADS * GLA_DV, GLA_GATE_RANK, GLA_HEADS * GLA_DV)

LANES = 128
FEAT = 64
VMEM_LIMIT = 48 * 1024 * 1024


def _dot(a, b):
    return jnp.dot(a, b, preferred_element_type=F32)


def _dot_nt(a, b):
    return lax.dot_general(a, b, (((1,), (1,)), ((), ())), preferred_element_type=F32)


def _dot_tn(a, b):
    return lax.dot_general(a, b, (((0,), (0,)), ((), ())), preferred_element_type=F32)


def _hilo(a):
    hi = a.astype(BF16)
    lo = (a - hi.astype(F32)).astype(BF16)
    return hi, lo


def _pow2_neg(n):
    return lax.bitcast_convert_type(lax.shift_left(127 - n, 23), F32)


def _sigmoid(x):
    return 1.0 / (1.0 + jnp.exp(-x))


def _layer_norm(r, g, b):
    mu = jnp.mean(r, axis=-1, keepdims=True)
    d = r - mu
    var = jnp.mean(d * d, axis=-1, keepdims=True)
    return d * lax.rsqrt(var + LN_EPS) * g + b


def _proj_kernel(x_ref, w_ref, *out_refs, splits):
    xb = x_ref[...].astype(BF16)
    off = 0
    for o_ref, n in zip(out_refs, splits):
        o_ref[...] = _dot(xb, w_ref[:, off:off + n]).astype(o_ref.dtype)
        off += n


def _proj(x2d, w_bf16, splits, dtypes, tm=256):
    T, D = x2d.shape
    N = w_bf16.shape[1]
    assert sum(splits) == N and all(n % LANES == 0 for n in splits) and T % tm == 0
    return pl.pallas_call(
        functools.partial(_proj_kernel, splits=splits),
        grid=(T // tm,),
        in_specs=[pl.BlockSpec((tm, D), lambda i: (i, 0)),
                  pl.BlockSpec((D, N), lambda i: (0, 0))],
        out_specs=[pl.BlockSpec((tm, n), lambda i: (i, 0)) for n in splits],
        out_shape=[jax.ShapeDtypeStruct((T, n), dt) for n, dt in zip(splits, dtypes)],
        compiler_params=pltpu.CompilerParams(dimension_semantics=("parallel",),
                                             vmem_limit_bytes=VMEM_LIMIT),
        name="proj",
    )(x2d, w_bf16)


def _cmp_mlp_kernel(x_ref, w1_ref, pos_ref, b1_ref, w2_ref, o_ref, *, ncmp):
    x = x_ref[...]
    nc, half = x.shape
    a = _dot(x, w1_ref[0:half, :])
    bm = _dot(x, w1_ref[half:2 * half, :])
    phi, plo = _hilo(jnp.broadcast_to(pos_ref[...], (8, 2 * half)))
    c = (_dot(phi, w1_ref[...]) + _dot(plo, w1_ref[...]))[0:1] + b1_ref[...]
    hid = a + pltpu.roll(bm, nc - 1, axis=0) + c
    hid = hid * _sigmoid(hid)
    row = lax.broadcasted_iota(jnp.int32, hid.shape, 0)
    hid = jnp.where(row < ncmp, hid, 0.0)
    o_ref[...] = _dot(hid.astype(BF16), w2_ref[...]).astype(o_ref.dtype)


def _cmp_mlp(xc, w1, pos, b1, w2, ncmp):
    two, BG, NC, CW = xc.shape
    hid = w1.shape[-1]
    hd = w2.shape[-1]
    return pl.pallas_call(
        functools.partial(_cmp_mlp_kernel, ncmp=ncmp),
        grid=(two, BG),
        in_specs=[pl.BlockSpec((None, None, NC, CW), lambda i, s: (i, s, 0, 0)),
                  pl.BlockSpec((None, 2 * CW, hid), lambda i, s: (i, 0, 0)),
                  pl.BlockSpec((None, 1, 2 * CW), lambda i, s: (i, 0, 0)),
                  pl.BlockSpec((None, 1, hid), lambda i, s: (i, 0, 0)),
                  pl.BlockSpec((None, hid, hd), lambda i, s: (i, 0, 0))],
        out_specs=pl.BlockSpec((None, None, NC, hd), lambda i, s: (i, s, 0, 0)),
        out_shape=jax.ShapeDtypeStruct((two, BG, NC, hd), BF16),
        compiler_params=pltpu.CompilerParams(dimension_semantics=("parallel", "parallel"),
                                             vmem_limit_bytes=VMEM_LIMIT),
        name="cmp_mlp",
    )(xc, w1, pos, b1, w2)


def _cmp_attn_kernel(q_ref, kc_ref, vc_ref, ov_ref, o_ref, bias_ref, sc_ref,
                     *, tq, nh, groups, ncmp, ns, ntop):
    g = pl.program_id(0) % groups
    qi = pl.program_id(1)
    R = nh * tq
    nc = kc_ref.shape[0]
    s = _dot_nt(q_ref[:, 0:HEAD_DIM], kc_ref[...])
    row = lax.broadcasted_iota(jnp.int32, (R, nc), 0)
    col = lax.broadcasted_iota(jnp.int32, (R, nc), 1)
    t = qi * tq + row % tq
    slope = _pow2_neg(g * nh + row // tq + 1)
    cmp_end = col * NSA_CMP_STRIDE + (NSA_CMP_BLOCK - 1)
    s = s - slope * (t - cmp_end).astype(F32)
    valid = (cmp_end <= t) & (col < ncmp)
    s = jnp.where(valid, s, NEG)
    e = jnp.exp(s - jnp.max(s, axis=1, keepdims=True))
    e = jnp.where(col < ncmp, e, 0.0)
    p = e / jnp.sum(e, axis=1, keepdims=True) * valid.astype(F32)
    o_ref[...] = _dot(p.astype(BF16), vc_ref[...])

    psum = p[0:tq]
    for hh in range(1, nh):
        psum = psum + p[hh * tq:(hh + 1) * tq]
    phi, plo = _hilo(psum)
    imp = _dot(phi, ov_ref[...]) + _dot(plo, ov_ref[...])
    imp_t = imp.T

    nsp = sc_ref.shape[0]
    j = lax.broadcasted_iota(jnp.int32, (nsp, tq), 0)
    tl = qi * tq + lax.broadcasted_iota(jnp.int32, (nsp, tq), 1)
    cur = tl // NSA_SEL_BLOCK
    forced = ((j == 0) | (j == cur) | (j == cur - 1)).astype(F32)
    sc = jnp.where(j <= cur, imp_t[0:nsp] + FORCE_BONUS * forced, NEG)
    sc_ref[...] = sc
    rank = jnp.zeros((nsp, tq), jnp.int32)
    for jp in range(ns):
        r = sc_ref[pl.ds(jp, 1), :]
        beats = (r > sc) | ((r == sc) & (j > jp))
        rank = rank + jnp.where(beats, 1, 0)
    sel = (rank < ntop) & (j <= cur)
    bias_t = jnp.where(sel, 0.0, NEG)
    pad = jnp.zeros((FEAT, tq), F32)
    pieces = [pad, bias_t]
    if nsp < FEAT:
        pieces.append(jnp.zeros((FEAT - nsp, tq), F32))
    bias_ref[...] = jnp.concatenate(pieces, axis=0).T.astype(bias_ref.dtype)


def _cmp_attn(q_nsa, kc, vc, ov, *, S, tq, groups):
    BG = q_nsa.shape[0]
    nh = NSA_HG
    R = nh * tq
    NC = kc.shape[1]
    ncmp = (S - NSA_CMP_BLOCK) // NSA_CMP_STRIDE + 1
    ns = S // NSA_SEL_BLOCK
    assert ns <= FEAT
    nsp = max(8, ns)
    ntop = min(NSA_TOP_N, ns)
    return pl.pallas_call(
        functools.partial(_cmp_attn_kernel, tq=tq, nh=nh, groups=groups, ncmp=ncmp, ns=ns, ntop=ntop),
        grid=(BG, S // tq),
        in_specs=[pl.BlockSpec((None, R, LANES), lambda s, i: (s, i, 0)),
                  pl.BlockSpec((None, NC, HEAD_DIM), lambda s, i: (s, 0, 0)),
                  pl.BlockSpec((None, NC, HEAD_DIM), lambda s, i: (s, 0, 0)),
                  pl.BlockSpec((NC, LANES), lambda s, i: (0, 0))],
        out_specs=[pl.BlockSpec((None, R, HEAD_DIM), lambda s, i: (s, i, 0)),
                   pl.BlockSpec((None, tq, LANES), lambda s, i: (s, i, 0))],
        out_shape=[jax.ShapeDtypeStruct((BG, S * nh, HEAD_DIM), F32),
                   jax.ShapeDtypeStruct((BG, S, LANES), BF16)],
        scratch_shapes=[pltpu.VMEM((nsp, tq), F32)],
        compiler_params=pltpu.CompilerParams(dimension_semantics=("parallel", "parallel"),
                                             vmem_limit_bytes=VMEM_LIMIT),
        name="cmp_attn",
    )(q_nsa, kc, vc, ov)


def _moba_gate_kernel(q_ref, k_ref, a_ref, bias_ref, sc_ref, *, tq, nb, ntop):
    qi = pl.program_id(1)
    kmean = _dot(a_ref[...], k_ref[...])
    khi, klo = _hilo(kmean)
    q = q_ref[:, 0:HEAD_DIM]
    g_t = _dot_nt(khi, q) + _dot_nt(klo, q)
    nbp = g_t.shape[0]
    n = lax.broadcasted_iota(jnp.int32, (nbp, tq), 0)
    t = qi * tq + lax.broadcasted_iota(jnp.int32, (nbp, tq), 1)
    cur = t // MOBA_BLOCK
    sc = jnp.where(n < cur, g_t, NEG)
    sc_ref[...] = sc
    rank = jnp.zeros((nbp, tq), jnp.int32)
    for jp in range(nb):
        r = sc_ref[pl.ds(jp, 1), :]
        beats = (r > sc) | ((r == sc) & (n > jp))
        rank = rank + jnp.where(beats, 1, 0)
    sel = ((rank < ntop) & (n < cur)) | (n == cur)
    bias_t = jnp.where(sel, 0.0, NEG)
    full = jnp.concatenate([jnp.zeros((FEAT, tq), F32), bias_t,
                            jnp.zeros((FEAT - nbp, tq), F32)], axis=0)
    bias_ref[...] = full.T.astype(bias_ref.dtype)


def _moba_gate(q_m, k_m, avg, *, S, tq):
    BH = q_m.shape[0]
    nb = S // MOBA_BLOCK
    nbp = avg.shape[0]
    ntop = min(MOBA_TOP_K, nb)
    return pl.pallas_call(
        functools.partial(_moba_gate_kernel, tq=tq, nb=nb, ntop=ntop),
        grid=(BH, S // tq),
        in_specs=[pl.BlockSpec((None, tq, LANES), lambda s, i: (s, i, 0)),
                  pl.BlockSpec((None, S, HEAD_DIM), lambda s, i: (s, 0, 0)),
                  pl.BlockSpec((nbp, S), lambda s, i: (0, 0))],
        out_specs=pl.BlockSpec((None, tq, LANES), lambda s, i: (s, i, 0)),
        out_shape=jax.ShapeDtypeStruct((BH, S, LANES), BF16),
        scratch_shapes=[pltpu.VMEM((nbp, tq), F32)],
        compiler_params=pltpu.CompilerParams(dimension_semantics=("parallel", "parallel"),
                                             vmem_limit_bytes=VMEM_LIMIT),
        name="moba_gate",
    )(q_m, k_m, avg)


def _flash_kernel(q_ref, bias_ref, k_ref, v_ref, o_ref, qa_ref, m_ref, acc_ref,
                  *, tq, tk, nh, spb, window):
    sid = pl.program_id(0)
    qi = pl.program_id(1)
    R = nh * tq
    q0 = qi * tq

    lane = lax.broadcasted_iota(jnp.int32, (tq, LANES), 1)
    t = q0 + lax.broadcasted_iota(jnp.int32, (tq, LANES), 0)
    tb = (t // FEAT).astype(F32)
    tr = (t % FEAT).astype(F32)
    bias = bias_ref[...]
    for hh in range(nh):
        n = jnp.full((tq, LANES), (sid % spb) * nh + hh + 1, jnp.int32)
        slope = _pow2_neg(n)
        feats = jnp.where(lane == 0, -FEAT * slope * tb,
                          jnp.where(lane == 1, -slope * tr,
                                    jnp.where(lane < 4, slope, 0.0)))
        rows = pl.ds(hh * tq, tq)
        qa_ref[rows, 0:LANES] = q_ref[rows, :] + bias
        qa_ref[rows, LANES:2 * LANES] = feats.astype(BF16)
    m_ref[...] = jnp.full(m_ref.shape, NEG, F32)
    acc_ref[...] = jnp.zeros(acc_ref.shape, F32)

    def tile(ki, masked):
        keys = pl.ds(pl.multiple_of(ki * tk, tk), tk)
        s = _dot_nt(qa_ref[...], k_ref[keys, :])
        if masked:
            row = lax.broadcasted_iota(jnp.int32, (R, tk), 0)
            col = lax.broadcasted_iota(jnp.int32, (R, tk), 1)
            rel = (q0 - ki * tk) + row % tq - col
            ok = rel >= 0
            if window is not None:
                ok = ok & (rel < window)
            s = jnp.where(ok, s, NEG)
        m_prev = m_ref[:, 0:1]
        m_new = jnp.maximum(m_prev, jnp.max(s, axis=1, keepdims=True))
        alpha = jnp.exp(m_prev - m_new)
        pr = jnp.exp(s - m_new)
        acc_ref[...] = alpha * acc_ref[...] + _dot(pr.astype(BF16), v_ref[keys, :])
        m_ref[...] = jnp.broadcast_to(m_new, m_ref.shape)

    def run(lo, hi, masked):
        def body(ki, carry):
            tile(ki, masked)
            return carry
        lax.fori_loop(lo, hi, body, 0)

    kd = (q0 + tq - 1) // tk
    if window is None:
        run(0, kd, False)
        tile(kd, True)
    else:
        run(jnp.maximum(q0 - window + 1, 0) // tk, kd + 1, True)

    acc = acc_ref[...]
    o_ref[...] = acc / acc[:, HEAD_DIM:HEAD_DIM + 1]


def _flash(q, bias, k_aug, v_aug, *, S, tq, tk, nh, spb, window):
    streams = q.shape[0]
    R = nh * tq
    assert tk % tq == 0 and S % tk == 0
    return pl.pallas_call(
        functools.partial(_flash_kernel, tq=tq, tk=tk, nh=nh, spb=spb, window=window),
        grid=(streams, S // tq),
        in_specs=[pl.BlockSpec((None, R, LANES), lambda s, i: (s, i, 0)),
                  pl.BlockSpec((None, tq, LANES), lambda s, i: (s, i, 0)),
                  pl.BlockSpec((None, S, 2 * LANES), lambda s, i: (s, 0, 0)),
                  pl.BlockSpec((None, S, LANES), lambda s, i: (s, 0, 0))],
        out_specs=pl.BlockSpec((None, R, LANES), lambda s, i: (s, i, 0)),
        out_shape=jax.ShapeDtypeStruct((streams, S * nh, LANES), F32),
        scratch_shapes=[pltpu.VMEM((R, 2 * LANES), BF16),
                        pltpu.VMEM((R, LANES), F32),
                        pltpu.VMEM((R, LANES), F32)],
        compiler_params=pltpu.CompilerParams(dimension_semantics=("parallel", "parallel"),
                                             vmem_limit_bytes=VMEM_LIMIT),
        name="flash",
    )(q, bias, k_aug, v_aug)


def _even_out_kernel(x_ref, ocmp_ref, osel_ref, owin_ref, omoba_ref, gate_ref, nz_ref, mz_ref,
                     e_ref, w_ref, g_ref, b_ref, o_ref, *, alpha):
    sig = _sigmoid(gate_ref[...])
    shi, slo = _hilo(sig)
    branches = (ocmp_ref, osel_ref, owin_ref)
    o_nsa = None
    for br in range(3):
        gb = _dot(shi, e_ref[br]) + _dot(slo, e_ref[br])
        term = gb * branches[br][...]
        o_nsa = term if o_nsa is None else o_nsa + term
    nz = nz_ref[...]
    mz = mz_ref[...]
    a = jnp.concatenate([o_nsa * (nz * _sigmoid(nz)), omoba_ref[...] * (mz * _sigmoid(mz))], axis=1)
    y = _dot(a.astype(BF16), w_ref[...])
    o_ref[...] = _layer_norm(alpha * x_ref[...] + y, g_ref[...], b_ref[...])


def _even_out(x2d, ocmp, osel, owin, omoba, gate, nz, mz, expand, w_out, ln_g, ln_b, alpha, tm=256):
    T, D = x2d.shape
    W = ocmp.shape[1]
    row = lambda n: pl.BlockSpec((tm, n), lambda i: (i, 0))
    return pl.pallas_call(
        functools.partial(_even_out_kernel, alpha=alpha),
        grid=(T // tm,),
        in_specs=[row(D), row(W), row(W), row(W), row(W), row(LANES), row(W), row(W),
                  pl.BlockSpec((3, LANES, W), lambda i: (0, 0, 0)),
                  pl.BlockSpec((2 * W, D), lambda i: (0, 0)),
                  pl.BlockSpec((1, D), lambda i: (0, 0)),
                  pl.BlockSpec((1, D), lambda i: (0, 0))],
        out_specs=row(D),
        out_shape=jax.ShapeDtypeStruct((T, D), F32),
        compiler_params=pltpu.CompilerParams(dimension_semantics=("parallel",),
                                             vmem_limit_bytes=VMEM_LIMIT),
        name="even_out",
    )(x2d, ocmp, osel, owin, omoba, gate, nz, mz, expand, w_out, ln_g, ln_b)


def _gla_kernel(q_ref, k_ref, v_ref, glr_ref, w2_ref, gb_ref, tril_ref, ones_ref, o_ref,
                st_ref, b_ref, kf_ref, *, scale):
    C, H, dk, dv, SUB = GLA_CHUNK, GLA_HEADS, GLA_DK, GLA_DV, GLA_SUB

    @pl.when(pl.program_id(1) == 0)
    def _init():
        st_ref[...] = jnp.zeros(st_ref.shape, F32)

    ghi, glo = _hilo(glr_ref[...])
    whi, wlo = _hilo(w2_ref[...])
    pre = _dot(ghi, whi) + _dot(glo, whi) + _dot(ghi, wlo) + gb_ref[...]
    la = (jnp.minimum(pre, 0.0) - jnp.log1p(jnp.exp(-jnp.abs(pre)))) * (1.0 / GLA_TAU)
    lhi, llo = _hilo(la)
    b_all = _dot(tril_ref[...], lhi) + _dot(tril_ref[...], llo)
    b_ref[...] = b_all
    kf_ref[...] = k_ref[...].astype(F32)
    rows = lax.broadcasted_iota(jnp.int32, (C, dk), 0)
    sub_i = lax.broadcasted_iota(jnp.int32, (SUB, dk), 0)

    for h in range(H):
        sl = pl.ds(h * dk, dk)
        bh = b_ref[:, sl]
        kf = kf_ref[:, sl]
        qf = q_ref[:, sl].astype(F32) * scale
        vh = v_ref[:, pl.ds(h * dv, dv)]
        st = st_ref[h]
        inter = _dot_nt((qf * jnp.exp(bh)).astype(BF16), st.astype(BF16))
        bl = b_ref[pl.ds(C - 1, 1), sl]
        kd = kf * jnp.exp(bl - bh)

        def off_block(r0, r1, c0, c1):
            ref_row = b_ref[pl.ds(r0, 1), sl]
            qt = qf[r0:r1] * jnp.exp(bh[r0:r1] - ref_row)
            ek = jnp.where((rows >= c0) & (rows < c1), ref_row - bh, NEG)
            kt = kf * jnp.exp(ek)
            return _dot_nt(qt.astype(BF16), kt.astype(BF16))

        lower = off_block(2 * SUB, C, 0, 2 * SUB)
        blocks = [None, off_block(SUB, 2 * SUB, 0, SUB), lower[0:SUB],
                  lower[SUB:2 * SUB] + off_block(3 * SUB, C, 2 * SUB, 3 * SUB)]
        for blk in range(C // SUB):
            r0 = blk * SUB
            q_i = qf[r0:r0 + SUB]
            b_i = bh[r0:r0 + SUB]
            pieces = []
            for jj in range(SUB):
                b_j = b_ref[pl.ds(r0 + jj, 1), sl]
                k_j = kf_ref[pl.ds(r0 + jj, 1), sl]
                diff = jnp.where(sub_i >= jj, b_i - b_j, NEG)
                pieces.append((q_i * k_j * jnp.exp(diff)).astype(BF16))
            diag = _dot(jnp.concatenate(pieces, axis=1), ones_ref[blk])
            blocks[blk] = diag if blocks[blk] is None else blocks[blk] + diag
        att = jnp.concatenate(blocks, axis=0)
        o_ref[:, pl.ds(h * dv, dv)] = inter + _dot(att.astype(BF16), vh)
        st_ref[h] = st * jnp.exp(bl) + _dot_tn(vh, kd.astype(BF16))


def _gla(q, k, v, glr, w2p, gb, tril, ones_blk, *, B, S):
    C, H, dk, dv = GLA_CHUNK, GLA_HEADS, GLA_DK, GLA_DV
    nchunk = S // C
    tok = lambda n: pl.BlockSpec((C, n), lambda b, c: (b * nchunk + c, 0))
    const = lambda shape: pl.BlockSpec(shape, lambda b, c: (0,) * len(shape))
    return pl.pallas_call(
        functools.partial(_gla_kernel, scale=float(dk) ** -0.5),
        grid=(B, nchunk),
        in_specs=[tok(H * dk), tok(H * dk), tok(H * dv), tok(LANES),
                  const(w2p.shape), const(gb.shape), const(tril.shape), const(ones_blk.shape)],
        out_specs=tok(H * dv),
        out_shape=jax.ShapeDtypeStruct((B * S, H * dv), F32),
        scratch_shapes=[pltpu.VMEM((H, dv, dk), F32),
                        pltpu.VMEM((C, H * dk), F32),
                        pltpu.VMEM((C, H * dk), F32)],
        compiler_params=pltpu.CompilerParams(dimension_semantics=("parallel", "arbitrary"),
                                             vmem_limit_bytes=VMEM_LIMIT),
        name="gla",
    )(q, k, v, glr, w2p, gb, tril, ones_blk)


def _odd_out_kernel(x_ref, o_ref_in, z_ref, gn_ref, w_ref, g_ref, b_ref, out_ref, *, alpha):
    H, dv = GLA_HEADS, GLA_DV
    parts = []
    for h in range(H):
        sl = pl.ds(h * dv, dv)
        oh = o_ref_in[:, sl]
        r = lax.rsqrt(jnp.mean(oh * oh, axis=-1, keepdims=True) + LN_EPS)
        z = z_ref[:, sl]
        parts.append((oh * r * gn_ref[:, sl] * (z * _sigmoid(z))).astype(BF16))
    y = _dot(jnp.concatenate(parts, axis=1), w_ref[...])
    out_ref[...] = _layer_norm(alpha * x_ref[...] + y, g_ref[...], b_ref[...])


def _odd_out(x2d, o, z, gn_g, w_out, ln_g, ln_b, alpha, tm=256):
    T, D = x2d.shape
    W = o.shape[1]
    row = lambda n: pl.BlockSpec((tm, n), lambda i: (i, 0))
    return pl.pallas_call(
        functools.partial(_odd_out_kernel, alpha=alpha),
        grid=(T // tm,),
        in_specs=[row(D), row(W), row(W),
                  pl.BlockSpec((1, W), lambda i: (0, 0)),
                  pl.BlockSpec((W, D), lambda i: (0, 0)),
                  pl.BlockSpec((1, D), lambda i: (0, 0)),
                  pl.BlockSpec((1, D), lambda i: (0, 0))],
        out_specs=row(D),
        out_shape=jax.ShapeDtypeStruct((T, D), F32),
        compiler_params=pltpu.CompilerParams(dimension_semantics=("parallel",),
                                             vmem_limit_bytes=VMEM_LIMIT),
        name="odd_out",
    )(x2d, o, z, gn_g, w_out, ln_g, ln_b)


def _key_features(S, block):
    s = np.arange(S)
    f = np.zeros((S, 3 * FEAT), np.float32)
    f[s, s // block] = 1.0
    f[:, FEAT + 0] = 1.0
    f[:, FEAT + 1] = 1.0
    f[:, FEAT + 2] = FEAT * (s // FEAT)
    f[:, FEAT + 3] = s % FEAT
    return jnp.asarray(f, BF16)


def _aug_k(k, feats):
    return jnp.concatenate([k, jnp.broadcast_to(feats[None], (k.shape[0],) + feats.shape)], axis=-1)


def _aug_v(v):
    ones = jnp.ones(v.shape[:-1] + (1,), v.dtype)
    zeros = jnp.zeros(v.shape[:-1] + (LANES - v.shape[-1] - 1,), v.dtype)
    return jnp.concatenate([v, ones, zeros], axis=-1)


def _pad_lanes(a):
    return jnp.pad(a, ((0, 0),) * (a.ndim - 1) + ((0, LANES - a.shape[-1]),))


def _even_layer(x2d, B, S, w_in, cmp_pos, cmp_w1, cmp_b1, cmp_w2, w_out, ln_g, ln_b, alpha):
    T, D = x2d.shape
    G, Hg, hd = NSA_KV_GROUPS, NSA_HG, HEAD_DIM
    scale = hd ** -0.5
    wq, wkv, wgate, wnz, wmq, wmk, wmv, wmz = jnp.split(w_in, np.cumsum(EVEN_SPLITS)[:-1].tolist(), axis=1)
    w_perm = jnp.concatenate([wq * scale, wkv, wnz, wmq * scale, wmk, wmv, wmz, _pad_lanes(wgate)],
                             axis=1).astype(BF16)
    nq, nkv, nz, mq, mk, mv, mz, gate = _proj(
        x2d, w_perm, (NSA_W, 6 * NSA_KV_W, NSA_W, MOBA_W, MOBA_W, MOBA_W, MOBA_W, LANES),
        (BF16, BF16, F32, BF16, BF16, BF16, F32, F32))

    tq = 128
    q_nsa = _pad_lanes(nq.reshape(B, S // tq, tq, G, Hg, hd).transpose(0, 3, 1, 4, 2, 5)
                       .reshape(B * G, S * Hg, hd))
    kv = nkv.reshape(B, S, 6, G, hd).transpose(2, 0, 3, 1, 4).reshape(6, B * G, S, hd)
    k_cmp, v_cmp, k_sel, v_sel, k_win, v_win = (kv[i] for i in range(6))

    ncmp = (S - NSA_CMP_BLOCK) // NSA_CMP_STRIDE + 1
    NC = S // NSA_CMP_STRIDE
    xc = jnp.stack([k_cmp, v_cmp]).reshape(2, B * G, NC, NSA_CMP_STRIDE * hd)
    kvc = _cmp_mlp(xc, cmp_w1.astype(BF16), cmp_pos.reshape(2, 1, NSA_CMP_BLOCK * hd),
                   cmp_b1.reshape(2, 1, NSA_CMP_HIDDEN), cmp_w2.astype(BF16), ncmp)

    ns = S // NSA_SEL_BLOCK
    cmp_start = np.arange(NC) * NSA_CMP_STRIDE
    sel_start = np.arange(LANES) * NSA_SEL_BLOCK
    ov = ((cmp_start[:, None] <= sel_start[None, :] + NSA_SEL_BLOCK - 1)
          & (cmp_start[:, None] + NSA_CMP_BLOCK - 1 >= sel_start[None, :])
          & (np.arange(NC)[:, None] < ncmp) & (np.arange(LANES)[None, :] < ns))
    o_cmp, bias_sel = _cmp_attn(q_nsa, kvc[0], kvc[1], jnp.asarray(ov, BF16), S=S, tq=tq, groups=G)

    tk = min(512, S)
    feats_sel = _key_features(S, NSA_SEL_BLOCK)
    o_sel = _flash(q_nsa, bias_sel, _aug_k(k_sel, feats_sel), _aug_v(v_sel),
                   S=S, tq=tq, tk=tk, nh=Hg, spb=G, window=None)
    zero_bias = jnp.zeros((B * G, S, LANES), BF16)
    o_win = _flash(q_nsa, zero_bias, _aug_k(k_win, feats_sel), _aug_v(v_win),
                   S=S, tq=tq, tk=tk, nh=Hg, spb=G, window=NSA_WINDOW)

    def nsa_tokens(o):
        return (o[..., :hd].reshape(B, G, S // tq, Hg, tq, hd).transpose(0, 2, 4, 1, 3, 5)
                .reshape(T, NSA_W))

    H = MOBA_HEADS
    heads = lambda a: a.reshape(B, S, H, hd).transpose(0, 2, 1, 3).reshape(B * H, S, hd)
    q_m, k_m, v_m = _pad_lanes(heads(mq)), heads(mk), heads(mv)
    tqm = min(512, S)
    nb = S // MOBA_BLOCK
    nbp = 16
    assert nb <= nbp
    avg = np.zeros((nbp, S), np.float32)
    avg[np.arange(S) // MOBA_BLOCK, np.arange(S)] = 1.0 / MOBA_BLOCK
    bias_m = _moba_gate(q_m, k_m, jnp.asarray(avg, BF16), S=S, tq=tqm)
    o_m = _flash(q_m, bias_m, _aug_k(k_m, _key_features(S, MOBA_BLOCK)), _aug_v(v_m),
                 S=S, tq=tqm, tk=tk, nh=1, spb=H, window=None)
    o_moba = o_m[..., :hd].reshape(B, H, S, hd).transpose(0, 2, 1, 3).reshape(T, MOBA_W)

    c = np.arange(LANES)[:, None]
    n = np.arange(NSA_W)[None, :]
    expand = jnp.asarray(np.stack([(c == 3 * (n // hd) + br) for br in range(3)]), BF16)
    return _even_out(x2d, nsa_tokens(o_cmp), nsa_tokens(o_sel), nsa_tokens(o_win), o_moba, gate, nz, mz,
                     expand, w_out.astype(BF16), ln_g.reshape(1, D), ln_b.reshape(1, D), alpha)


def _odd_layer(x2d, B, S, w_in, gate_w2, gate_b, gn_g, w_out, ln_g, ln_b, alpha):
    T, D = x2d.shape
    H, dk, dv, C, SUB = GLA_HEADS, GLA_DK, GLA_DV, GLA_CHUNK, GLA_SUB
    wq, wk, wv, wg, wz = jnp.split(w_in, np.cumsum(ODD_SPLITS)[:-1].tolist(), axis=1)
    w_perm = jnp.concatenate([wq, wk, wv, wz, _pad_lanes(wg)], axis=1).astype(BF16)
    q, k, v, z, glr = _proj(x2d, w_perm, (H * dk, H * dk, H * dv, H * dv, LANES),
                            (BF16, BF16, BF16, F32, F32))
    w2p = jnp.pad(gate_w2, ((0, LANES - GLA_GATE_RANK), (0, 0)))
    tril = jnp.asarray(np.tril(np.ones((C, C), np.float32)), BF16)
    ones_blk = np.zeros((C // SUB, SUB * dk, C), np.float32)
    for blk in range(C // SUB):
        for jj in range(SUB):
            ones_blk[blk, jj * dk:(jj + 1) * dk, blk * SUB + jj] = 1.0
    o = _gla(q, k, v, glr, w2p, gate_b.reshape(1, H * dk), tril, jnp.asarray(ones_blk, BF16), B=B, S=S)
    return _odd_out(x2d, o, z, gn_g.reshape(1, H * dv), w_out.astype(BF16),
                    ln_g.reshape(1, D), ln_b.reshape(1, D), alpha)


def kernel(x, ev_w_in, ev_cmp_pos, ev_cmp_w1, ev_cmp_b1, ev_cmp_w2, ev_w_out, ev_ln_g, ev_ln_b,
           od_w_in, od_gate_w2, od_gate_b, od_gn_g, od_w_out, od_ln_g, od_ln_b):
    B, S, D = x.shape
    depth = ev_w_in.shape[0] + od_w_in.shape[0]
    alpha = (2.0 * depth) ** 0.25
    h = x.reshape(B * S, D)
    for layer in range(depth):
        i = layer // 2
        if layer % 2 == 0:
            h = _even_layer(h, B, S, ev_w_in[i], ev_cmp_pos[i], ev_cmp_w1[i], ev_cmp_b1[i], ev_cmp_w2[i],
                            ev_w_out[i], ev_ln_g[i], ev_ln_b[i], alpha)
        else:
            h = _odd_layer(h, B, S, od_w_in[i], od_gate_w2[i], od_gate_b[i], od_gn_g[i],
                           od_w_out[i], od_ln_g[i], od_ln_b[i], alpha)
    return h.reshape(B, S, D)
```

```python
import functools

import numpy as np
import jax
import jax.numpy as jnp
from jax import lax
from jax.experimental import pallas as pl
from jax.experimental.pallas import tpu as pltpu

F32 = jnp.float32
BF16 = jnp.bfloat16

HEAD_DIM = 64
NSA_HEADS = 8
NSA_KV_GROUPS = 2
NSA_HG = NSA_HEADS // NSA_KV_GROUPS
NSA_CMP_BLOCK = 32
NSA_CMP_STRIDE = 16
NSA_CMP_HIDDEN = 128
NSA_SEL_BLOCK = 64
NSA_TOP_N = 16
NSA_WINDOW = 512
MOBA_HEADS = 8
MOBA_BLOCK = 256
MOBA_TOP_K = 3
GLA_HEADS = 4
GLA_DK = 128
GLA_DV = 256
GLA_GATE_RANK = 16
GLA_TAU = 16.0
GLA_CHUNK = 64
GLA_SUB = 16
LN_EPS = 1e-5
NEG = -1e30
FORCE_BONUS = 1e4

NSA_W = NSA_HEADS * HEAD_DIM
NSA_KV_W = NSA_KV_GROUPS * HEAD_DIM
MOBA_W = MOBA_HEADS * HEAD_DIM
EVEN_SPLITS = (NSA_W, 6 * NSA_KV_W, 3 * NSA_HEADS, NSA_W, MOBA_W, MOBA_W, MOBA_W, MOBA_W)
ODD_SPLITS = (GLA_HEADS * GLA_DK, GLA_HEADS * GLA_DK, GLA_HEADS * GLA_DV, GLA_GATE_RANK, GLA_HEADS * GLA_DV)

LANES = 128
FEAT = 64
GATE_ROWS = 32
MOBA_NBP = 16
TQ = 128
TQM = 512
TK = 512
VMEM_LIMIT = 48 * 1024 * 1024


def _dot(a, b):
    return jnp.dot(a, b, preferred_element_type=F32)


def _dot_nt(a, b):
    return lax.dot_general(a, b, (((1,), (1,)), ((), ())), preferred_element_type=F32)


def _dot_tn(a, b):
    return lax.dot_general(a, b, (((0,), (0,)), ((), ())), preferred_element_type=F32)


def _hilo(a):
    hi = a.astype(BF16)
    lo = (a - hi.astype(F32)).astype(BF16)
    return hi, lo


def _pow2_neg(n):
    return lax.bitcast_convert_type(lax.shift_left(127 - n, 23), F32)


def _sigmoid(x):
    return 1.0 / (1.0 + jnp.exp(-x))


def _layer_norm(r, g, b):
    mu = jnp.mean(r, axis=-1, keepdims=True)
    d = r - mu
    var = jnp.mean(d * d, axis=-1, keepdims=True)
    return d * lax.rsqrt(var + LN_EPS) * g + b


def _rank_select(sc_ref, sc, idx, n_real, ntop):
    rank = jnp.zeros(sc.shape, jnp.int32)
    for jp in range(n_real):
        r = sc_ref[pl.ds(jp, 1), :]
        beats = (r > sc) | ((r == sc) & (idx > jp))
        rank = rank + jnp.where(beats, 1, 0)
    return rank < ntop


def _proj_kernel(*refs, tok_splits, tr_splits):
    x_ref, wt_ref = refs[0], refs[1]
    outs = list(refs[3:] if tr_splits else refs[2:])
    x = x_ref[...]
    xb = x.astype(BF16)
    off = 0
    for n in tok_splits:
        o_ref = outs.pop(0)
        o_ref[...] = _dot(xb, wt_ref[:, off:off + n]).astype(o_ref.dtype)
        off += n
    if tr_splits:
        wtr_ref = refs[2]
        xt = x.T.astype(BF16)
        off = 0
        for n in tr_splits:
            o_ref = outs.pop(0)
            o_ref[...] = _dot(wtr_ref[off:off + n, :], xt).astype(o_ref.dtype)
            off += n


def _proj(x2d, w_tok, tok_splits, tok_dtypes, w_tr=None, tr_splits=(), tr_dtypes=(), tm=256):
    T, D = x2d.shape
    assert T % tm == 0 and sum(tok_splits) == w_tok.shape[1]
    in_specs = [pl.BlockSpec((tm, D), lambda i: (i, 0)),
                pl.BlockSpec(w_tok.shape, lambda i: (0, 0))]
    args = [x2d, w_tok]
    if tr_splits:
        assert sum(tr_splits) == w_tr.shape[0]
        in_specs.append(pl.BlockSpec(w_tr.shape, lambda i: (0, 0)))
        args.append(w_tr)
    out_specs = ([pl.BlockSpec((tm, n), lambda i: (i, 0)) for n in tok_splits]
                 + [pl.BlockSpec((n, tm), lambda i: (0, i)) for n in tr_splits])
    out_shape = ([jax.ShapeDtypeStruct((T, n), dt) for n, dt in zip(tok_splits, tok_dtypes)]
                 + [jax.ShapeDtypeStruct((n, T), dt) for n, dt in zip(tr_splits, tr_dtypes)])
    return pl.pallas_call(
        functools.partial(_proj_kernel, tok_splits=tuple(tok_splits), tr_splits=tuple(tr_splits)),
        grid=(T // tm,), in_specs=in_specs, out_specs=out_specs, out_shape=out_shape,
        compiler_params=pltpu.CompilerParams(dimension_semantics=("parallel",),
                                             vmem_limit_bytes=VMEM_LIMIT),
        name="proj",
    )(*args)


def _cmp_mlp_kernel(kx_ref, vx_ref, w1_ref, pos_ref, b1_ref, w2_ref, o_ref, *, ncmp):
    for i, x_ref in enumerate((kx_ref, vx_ref)):
        x = x_ref[...]
        nc, cw = x.shape
        a = _dot(x, w1_ref[i, 0])
        bm = _dot(x, w1_ref[i, 1])
        c = b1_ref[i]
        for half in range(2):
            phi, plo = _hilo(jnp.broadcast_to(pos_ref[i, half], (8, cw)))
            c = c + (_dot(phi, w1_ref[i, half]) + _dot(plo, w1_ref[i, half]))[0:1]
        hid = a + pltpu.roll(bm, nc - 1, axis=0) + c
        hid = hid * _sigmoid(hid)
        row = lax.broadcasted_iota(jnp.int32, hid.shape, 0)
        hid = jnp.where(row < ncmp, hid, 0.0)
        o_ref[i] = _dot(hid.astype(BF16), w2_ref[i]).astype(o_ref.dtype)


def _cmp_mlp(kx, vx, w1p, posp, b1p, w2p, ncmp):
    B, NC, CW = kx.shape
    OW = w2p.shape[-1]
    full = lambda a: pl.BlockSpec(a.shape, lambda b: (0,) * a.ndim)
    return pl.pallas_call(
        functools.partial(_cmp_mlp_kernel, ncmp=ncmp),
        grid=(B,),
        in_specs=[pl.BlockSpec((None, NC, CW), lambda b: (b, 0, 0)),
                  pl.BlockSpec((None, NC, CW), lambda b: (b, 0, 0)),
                  full(w1p), full(posp), full(b1p), full(w2p)],
        out_specs=pl.BlockSpec((2, None, NC, OW), lambda b: (0, b, 0, 0)),
        out_shape=jax.ShapeDtypeStruct((2, B, NC, OW), BF16),
        compiler_params=pltpu.CompilerParams(dimension_semantics=("parallel",),
                                             vmem_limit_bytes=VMEM_LIMIT),
        name="cmp_mlp",
    )(kx, vx, w1p, posp, b1p, w2p)


def _cmp_attn_kernel(q_ref, kc_ref, vc_ref, ovt_ref, o_ref, bias_ref, sc_ref, *, tq, nh, ncmp, ns, ntop):
    g = pl.program_id(1)
    qi = pl.program_id(2)
    R = nh * tq
    nc = kc_ref.shape[0]
    qcat = jnp.concatenate([q_ref[hh * HEAD_DIM:(hh + 1) * HEAD_DIM, :] for hh in range(nh)], axis=1)
    s = _dot(kc_ref[:, 0:HEAD_DIM], qcat)
    n = lax.broadcasted_iota(jnp.int32, (nc, R), 0)
    col = lax.broadcasted_iota(jnp.int32, (nc, R), 1)
    t = qi * tq + col % tq
    slope = _pow2_neg(g * nh + col // tq + 1)
    cmp_end = n * NSA_CMP_STRIDE + (NSA_CMP_BLOCK - 1)
    s = s - slope * (t - cmp_end).astype(F32)
    valid = (cmp_end <= t) & (n < ncmp)
    s = jnp.where(valid, s, NEG)
    e = jnp.exp(s - jnp.max(s, axis=0, keepdims=True))
    e = jnp.where(n < ncmp, e, 0.0)
    p = e / jnp.sum(e, axis=0, keepdims=True) * jnp.where(valid, 1.0, 0.0)
    o_ref[...] = _dot_tn(vc_ref[:, 0:HEAD_DIM], p.astype(BF16))

    psum = p[:, 0:tq]
    for hh in range(1, nh):
        psum = psum + p[:, hh * tq:(hh + 1) * tq]
    phi, plo = _hilo(psum)
    imp = _dot(ovt_ref[...], phi) + _dot(ovt_ref[...], plo)

    j = lax.broadcasted_iota(jnp.int32, (FEAT, tq), 0)
    cur = (qi * tq + lax.broadcasted_iota(jnp.int32, (FEAT, tq), 1)) // NSA_SEL_BLOCK
    forced = jnp.where((j == 0) | (j == cur) | (j == cur - 1), FORCE_BONUS, 0.0)
    sc = jnp.where(j <= cur, imp + forced, NEG)
    sc_ref[...] = sc
    sel = _rank_select(sc_ref, sc, j, ns, ntop) & (j <= cur)
    bias_ref[...] = jnp.where(sel, 0.0, NEG).astype(bias_ref.dtype)


def _cmp_attn(nq_t, kvc, ovt, *, B, S):
    tq, nh, G = TQ, NSA_HG, NSA_KV_GROUPS
    R = nh * tq
    nq = S // tq
    NC = kvc.shape[2]
    ncmp = (S - NSA_CMP_BLOCK) // NSA_CMP_STRIDE + 1
    ns = S // NSA_SEL_BLOCK
    assert ns <= FEAT
    ntop = min(NSA_TOP_N, ns)
    return pl.pallas_call(
        functools.partial(_cmp_attn_kernel, tq=tq, nh=nh, ncmp=ncmp, ns=ns, ntop=ntop),
        grid=(B, G, nq),
        in_specs=[pl.BlockSpec((nh * HEAD_DIM, tq), lambda b, g, i: (g, b * nq + i)),
                  pl.BlockSpec((None, None, NC, LANES), lambda b, g, i: (0, b, 0, g)),
                  pl.BlockSpec((None, None, NC, LANES), lambda b, g, i: (1, b, 0, g)),
                  pl.BlockSpec(ovt.shape, lambda b, g, i: (0, 0))],
        out_specs=[pl.BlockSpec((None, None, HEAD_DIM, R), lambda b, g, i: (b, g, 0, i)),
                   pl.BlockSpec((None, None, FEAT, tq), lambda b, g, i: (b, g, 0, i))],
        out_shape=[jax.ShapeDtypeStruct((B, G, HEAD_DIM, S * nh), F32),
                   jax.ShapeDtypeStruct((B, G, FEAT, S), BF16)],
        scratch_shapes=[pltpu.VMEM((FEAT, tq), F32)],
        compiler_params=pltpu.CompilerParams(dimension_semantics=("parallel", "parallel", "parallel"),
                                             vmem_limit_bytes=VMEM_LIMIT),
        name="cmp_attn",
    )(nq_t, kvc, kvc, ovt)


def _moba_gate_kernel(q_ref, k_ref, a_ref, bias_ref, km_ref, sc_ref, *, tq, nb, ntop):
    h = pl.program_id(1)
    qi = pl.program_id(2)

    @pl.when(qi == 0)
    def _means():
        km_ref[...] = _dot(a_ref[...], k_ref[...])

    nbp = km_ref.shape[0]
    lane = lax.broadcasted_iota(jnp.int32, (nbp, LANES), 1)
    km = jnp.where(lane // HEAD_DIM == h % 2, km_ref[...], 0.0)
    khi, klo = _hilo(km)
    q = q_ref[...]
    qq = jnp.concatenate([q, q], axis=0)
    g_t = _dot(khi, qq) + _dot(klo, qq)
    n = lax.broadcasted_iota(jnp.int32, (nbp, tq), 0)
    cur = (qi * tq + lax.broadcasted_iota(jnp.int32, (nbp, tq), 1)) // MOBA_BLOCK
    sc = jnp.where(n < cur, g_t, NEG)
    sc_ref[...] = sc
    sel = (_rank_select(sc_ref, sc, n, nb, ntop) & (n < cur)) | (n == cur)
    bias_ref[0:nbp, :] = jnp.where(sel, 0.0, NEG).astype(bias_ref.dtype)
    bias_ref[nbp:FEAT, :] = jnp.zeros((FEAT - nbp, tq), bias_ref.dtype)


def _moba_gate(mq_t, mk, avg, *, B, S):
    tq, H = TQM, MOBA_HEADS
    nq = S // tq
    nb = S // MOBA_BLOCK
    nbp = avg.shape[0]
    assert nb <= nbp
    ntop = min(MOBA_TOP_K, nb)
    return pl.pallas_call(
        functools.partial(_moba_gate_kernel, tq=tq, nb=nb, ntop=ntop),
        grid=(B, H, nq),
        in_specs=[pl.BlockSpec((HEAD_DIM, tq), lambda b, h, i: (h, b * nq + i)),
                  pl.BlockSpec((S, LANES), lambda b, h, i: (b, h // 2)),
                  pl.BlockSpec(avg.shape, lambda b, h, i: (0, 0))],
        out_specs=pl.BlockSpec((None, None, FEAT, tq), lambda b, h, i: (b, h, 0, i)),
        out_shape=jax.ShapeDtypeStruct((B, H, FEAT, S), BF16),
        scratch_shapes=[pltpu.VMEM((nbp, LANES), F32), pltpu.VMEM((nbp, tq), F32)],
        compiler_params=pltpu.CompilerParams(dimension_semantics=("parallel", "parallel", "arbitrary"),
                                             vmem_limit_bytes=VMEM_LIMIT),
        name="moba_gate",
    )(mq_t, mk, avg)


def _flash_kernel(q_ref, bias_ref, k_ref, kf_ref, vt_ref, o_ref, ka_ref, vta_ref, qa_ref, m_ref, acc_ref,
                  *, tq, tk, nh, window):
    u = pl.program_id(1)
    qi = pl.program_id(2)
    R = nh * tq
    q0 = qi * tq
    S = k_ref.shape[0]

    @pl.when(qi == 0)
    def _stage_keys_values():
        ka_ref[:, 0:LANES] = k_ref[...]
        ka_ref[:, LANES:2 * LANES] = kf_ref[...]
        ones_row = jnp.where(lax.broadcasted_iota(jnp.int32, (HEAD_DIM, tk), 0) == 0, 1.0, 0.0)
        for jt in range(S // tk):
            vta_ref[jt, 0:HEAD_DIM, :] = vt_ref[:, jt * tk:(jt + 1) * tk]
            vta_ref[jt, HEAD_DIM:2 * HEAD_DIM, :] = ones_row.astype(BF16)

    qcat = jnp.concatenate([q_ref[hh * HEAD_DIM:(hh + 1) * HEAD_DIM, :] for hh in range(nh)], axis=1)
    qq = jnp.concatenate([qcat, qcat], axis=0).astype(F32)
    rowi = lax.broadcasted_iota(jnp.int32, (2 * HEAD_DIM, R), 0)
    qa_ref[0:2 * HEAD_DIM, :] = jnp.where(rowi // HEAD_DIM == u % 2, qq, 0.0).astype(BF16)
    bias = bias_ref[...]
    qa_ref[2 * HEAD_DIM:2 * HEAD_DIM + FEAT, :] = jnp.concatenate([bias] * nh, axis=1)
    r = lax.broadcasted_iota(jnp.int32, (FEAT, R), 0)
    col = lax.broadcasted_iota(jnp.int32, (FEAT, R), 1)
    t = q0 + col % tq
    slope = _pow2_neg(u * nh + col // tq + 1)
    tb = (t // FEAT).astype(F32)
    tr = (t % FEAT).astype(F32)
    feats = jnp.where(r == 0, -FEAT * slope * tb,
                      jnp.where(r == 1, -slope * tr, jnp.where(r < 4, slope, 0.0)))
    qa_ref[2 * HEAD_DIM + FEAT:2 * LANES, :] = feats.astype(BF16)
    m_ref[...] = jnp.full(m_ref.shape, NEG, F32)
    acc_ref[...] = jnp.zeros(acc_ref.shape, F32)

    def tile(ki, masked):
        keys = pl.ds(pl.multiple_of(ki * tk, tk), tk)
        s = _dot(ka_ref[keys, :], qa_ref[...])
        if masked:
            krow = lax.broadcasted_iota(jnp.int32, (tk, R), 0)
            qcol = lax.broadcasted_iota(jnp.int32, (tk, R), 1)
            rel = (q0 - ki * tk) + qcol % tq - krow
            ok = rel >= 0
            if window is not None:
                ok = ok & (rel < window)
            s = jnp.where(ok, s, NEG)
        m_prev = m_ref[0:1, :]
        m_new = jnp.maximum(m_prev, jnp.max(s, axis=0, keepdims=True))
        alpha = jnp.exp(m_prev - m_new)
        pr = jnp.exp(s - m_new)
        acc_ref[...] = alpha * acc_ref[...] + _dot(vta_ref[ki], pr.astype(BF16))
        m_ref[...] = jnp.broadcast_to(m_new, m_ref.shape)

    def run(lo, hi, masked):
        def body(ki, carry):
            tile(ki, masked)
            return carry
        lax.fori_loop(lo, hi, body, 0)

    kd = (q0 + tq - 1) // tk
    if window is None:
        run(0, kd, False)
        tile(kd, True)
    else:
        run(jnp.maximum(q0 - window + 1, 0) // tk, kd + 1, True)

    acc = acc_ref[...]
    o_ref[...] = acc[0:HEAD_DIM] / acc[HEAD_DIM:HEAD_DIM + 1]


def _flash(q_t, bias_t, k, kfeat, v_t, *, B, S, units, tq, nh, window):
    tk = TK
    R = nh * tq
    nq = S // tq
    assert tk % tq == 0 and S % tk == 0 and units % 2 == 0
    return pl.pallas_call(
        functools.partial(_flash_kernel, tq=tq, tk=tk, nh=nh, window=window),
        grid=(B, units, nq),
        in_specs=[pl.BlockSpec((nh * HEAD_DIM, tq), lambda b, u, i: (u, b * nq + i)),
                  pl.BlockSpec((None, None, FEAT, tq), lambda b, u, i: (b, u, 0, i)),
                  pl.BlockSpec((S, LANES), lambda b, u, i: (b, u // 2)),
                  pl.BlockSpec((S, LANES), lambda b, u, i: (0, 0)),
                  pl.BlockSpec((HEAD_DIM, S), lambda b, u, i: (u, b))],
        out_specs=pl.BlockSpec((None, None, HEAD_DIM, R), lambda b, u, i: (b, u, 0, i)),
        out_shape=jax.ShapeDtypeStruct((B, units, HEAD_DIM, S * nh), F32),
        scratch_shapes=[pltpu.VMEM((S, 2 * LANES), BF16),
                        pltpu.VMEM((S // tk, 2 * HEAD_DIM, tk), BF16),
                        pltpu.VMEM((2 * LANES, R), BF16),
                        pltpu.VMEM((8, R), F32),
                        pltpu.VMEM((2 * HEAD_DIM, R), F32)],
        compiler_params=pltpu.CompilerParams(dimension_semantics=("parallel", "parallel", "arbitrary"),
                                             vmem_limit_bytes=VMEM_LIMIT),
        name="flash",
    )(q_t, bias_t, k, kfeat, v_t)


def _even_out_kernel(x_ref, ocmp_ref, osel_ref, owin_ref, om_ref, gate_ref, nz_ref, mz_ref,
                     e_ref, w_ref, g_ref, b_ref, o_ref, *, alpha, tm):
    G, nh = NSA_KV_GROUPS, NSA_HG
    sig = _sigmoid(gate_ref[...])
    shi, slo = _hilo(sig)

    def heads_t(ref):
        return jnp.concatenate([ref[g, :, hh * tm:(hh + 1) * tm] for g in range(G) for hh in range(nh)],
                               axis=0)

    o_nsa = None
    for br, ref in enumerate((ocmp_ref, osel_ref, owin_ref)):
        gb = _dot(e_ref[br], shi) + _dot(e_ref[br], slo)
        term = gb * heads_t(ref)
        o_nsa = term if o_nsa is None else o_nsa + term
    nz = nz_ref[...]
    mz = mz_ref[...]
    o_m = jnp.concatenate([om_ref[h] for h in range(MOBA_HEADS)], axis=0)
    a_t = jnp.concatenate([o_nsa * (nz * _sigmoid(nz)), o_m * (mz * _sigmoid(mz))], axis=0)
    y = _dot_tn(a_t.astype(BF16), w_ref[...])
    o_ref[...] = _layer_norm(alpha * x_ref[...] + y, g_ref[...], b_ref[...])


def _even_out(x2d, ocmp, osel, owin, om, gate_t, nz_t, mz_t, expand, w_out, ln_g, ln_b, alpha, *, B, S):
    T, D = x2d.shape
    tm = TQ
    nq = S // tm
    G, nh, H = NSA_KV_GROUPS, NSA_HG, MOBA_HEADS
    nsa = pl.BlockSpec((None, G, HEAD_DIM, nh * tm), lambda i: (i // nq, 0, 0, i % nq))
    feat = lambda n: pl.BlockSpec((n, tm), lambda i: (0, i))
    full = lambda a: pl.BlockSpec(a.shape, lambda i: (0,) * a.ndim)
    return pl.pallas_call(
        functools.partial(_even_out_kernel, alpha=alpha, tm=tm),
        grid=(T // tm,),
        in_specs=[pl.BlockSpec((tm, D), lambda i: (i, 0)), nsa, nsa, nsa,
                  pl.BlockSpec((None, H, HEAD_DIM, tm), lambda i: (i // nq, 0, 0, i % nq)),
                  feat(GATE_ROWS), feat(NSA_W), feat(MOBA_W),
                  full(expand), full(w_out), full(ln_g), full(ln_b)],
        out_specs=pl.BlockSpec((tm, D), lambda i: (i, 0)),
        out_shape=jax.ShapeDtypeStruct((T, D), F32),
        compiler_params=pltpu.CompilerParams(dimension_semantics=("parallel",),
                                             vmem_limit_bytes=VMEM_LIMIT),
        name="even_out",
    )(x2d, ocmp, osel, owin, om, gate_t, nz_t, mz_t, expand, w_out, ln_g, ln_b)


def _gla_kernel(q_ref, k_ref, v_ref, glr_ref, w2_ref, gb_ref, tril_ref, ones_ref, o_ref,
                st_ref, b_ref, kf_ref, *, scale):
    C, H, dk, dv, SUB = GLA_CHUNK, GLA_HEADS, GLA_DK, GLA_DV, GLA_SUB

    @pl.when(pl.program_id(1) == 0)
    def _init():
        st_ref[...] = jnp.zeros(st_ref.shape, F32)

    ghi, glo = _hilo(glr_ref[...])
    whi, wlo = _hilo(w2_ref[...])
    pre = _dot(ghi, whi) + _dot(glo, whi) + _dot(ghi, wlo) + gb_ref[...]
    la = (jnp.minimum(pre, 0.0) - jnp.log1p(jnp.exp(-jnp.abs(pre)))) * (1.0 / GLA_TAU)
    lhi, llo = _hilo(la)
    b_all = _dot(tril_ref[...], lhi) + _dot(tril_ref[...], llo)
    b_ref[...] = b_all
    kf_ref[...] = k_ref[...].astype(F32)
    rows = lax.broadcasted_iota(jnp.int32, (C, dk), 0)
    sub_i = lax.broadcasted_iota(jnp.int32, (SUB, dk), 0)

    for h in range(H):
        sl = pl.ds(h * dk, dk)
        bh = b_ref[:, sl]
        kf = kf_ref[:, sl]
        qf = q_ref[:, sl].astype(F32) * scale
        vh = v_ref[:, pl.ds(h * dv, dv)]
        st = st_ref[h]
        inter = _dot_nt((qf * jnp.exp(bh)).astype(BF16), st.astype(BF16))
        bl = b_ref[pl.ds(C - 1, 1), sl]
        kd = kf * jnp.exp(bl - bh)

        def off_block(r0, r1, c0, c1):
            ref_row = b_ref[pl.ds(r0, 1), sl]
            qt = qf[r0:r1] * jnp.exp(bh[r0:r1] - ref_row)
            ek = jnp.where((rows >= c0) & (rows < c1), ref_row - bh, NEG)
            kt = kf * jnp.exp(ek)
            return _dot_nt(qt.astype(BF16), kt.astype(BF16))

        lower = off_block(2 * SUB, C, 0, 2 * SUB)
        blocks = [None, off_block(SUB, 2 * SUB, 0, SUB), lower[0:SUB],
                  lower[SUB:2 * SUB] + off_block(3 * SUB, C, 2 * SUB, 3 * SUB)]
        for blk in range(C // SUB):
            r0 = blk * SUB
            q_i = qf[r0:r0 + SUB]
            b_i = bh[r0:r0 + SUB]
            pieces = []
            for jj in range(SUB):
                b_j = b_ref[pl.ds(r0 + jj, 1), sl]
                k_j = kf_ref[pl.ds(r0 + jj, 1), sl]
                diff = jnp.where(sub_i >= jj, b_i - b_j, NEG)
                pieces.append((q_i * k_j * jnp.exp(diff)).astype(BF16))
            diag = _dot(jnp.concatenate(pieces, axis=1), ones_ref[blk])
            blocks[blk] = diag if blocks[blk] is None else blocks[blk] + diag
        att = jnp.concatenate(blocks, axis=0)
        o_ref[:, pl.ds(h * dv, dv)] = inter + _dot(att.astype(BF16), vh)
        st_ref[h] = st * jnp.exp(bl) + _dot_tn(vh, kd.astype(BF16))


def _gla(q, k, v, glr, w2p, gb, tril, ones_blk, *, B, S):
    C, H, dk, dv = GLA_CHUNK, GLA_HEADS, GLA_DK, GLA_DV
    nchunk = S // C
    tok = lambda n: pl.BlockSpec((C, n), lambda b, c: (b * nchunk + c, 0))
    const = lambda shape: pl.BlockSpec(shape, lambda b, c: (0,) * len(shape))
    return pl.pallas_call(
        functools.partial(_gla_kernel, scale=float(dk) ** -0.5),
        grid=(B, nchunk),
        in_specs=[tok(H * dk), tok(H * dk), tok(H * dv), tok(LANES),
                  const(w2p.shape), const(gb.shape), const(tril.shape), const(ones_blk.shape)],
        out_specs=tok(H * dv),
        out_shape=jax.ShapeDtypeStruct((B * S, H * dv), F32),
        scratch_shapes=[pltpu.VMEM((H, dv, dk), F32),
                        pltpu.VMEM((C, H * dk), F32),
                        pltpu.VMEM((C, H * dk), F32)],
        compiler_params=pltpu.CompilerParams(dimension_semantics=("parallel", "arbitrary"),
                                             vmem_limit_bytes=VMEM_LIMIT),
        name="gla",
    )(q, k, v, glr, w2p, gb, tril, ones_blk)


def _odd_out_kernel(x_ref, o_ref_in, z_ref, gn_ref, w_ref, g_ref, b_ref, out_ref, *, alpha):
    H, dv = GLA_HEADS, GLA_DV
    parts = []
    for h in range(H):
        sl = pl.ds(h * dv, dv)
        oh = o_ref_in[:, sl]
        r = lax.rsqrt(jnp.mean(oh * oh, axis=-1, keepdims=True) + LN_EPS)
        z = z_ref[:, sl]
        parts.append((oh * r * gn_ref[:, sl] * (z * _sigmoid(z))).astype(BF16))
    y = _dot(jnp.concatenate(parts, axis=1), w_ref[...])
    out_ref[...] = _layer_norm(alpha * x_ref[...] + y, g_ref[...], b_ref[...])


def _odd_out(x2d, o, z, gn_g, w_out, ln_g, ln_b, alpha, tm=256):
    T, D = x2d.shape
    W = o.shape[1]
    row = lambda n: pl.BlockSpec((tm, n), lambda i: (i, 0))
    return pl.pallas_call(
        functools.partial(_odd_out_kernel, alpha=alpha),
        grid=(T // tm,),
        in_specs=[row(D), row(W), row(W),
                  pl.BlockSpec((1, W), lambda i: (0, 0)),
                  pl.BlockSpec((W, D), lambda i: (0, 0)),
                  pl.BlockSpec((1, D), lambda i: (0, 0)),
                  pl.BlockSpec((1, D), lambda i: (0, 0))],
        out_specs=row(D),
        out_shape=jax.ShapeDtypeStruct((T, D), F32),
        compiler_params=pltpu.CompilerParams(dimension_semantics=("parallel",),
                                             vmem_limit_bytes=VMEM_LIMIT),
        name="odd_out",
    )(x2d, o, z, gn_g, w_out, ln_g, ln_b)


def _key_features(S, block):
    s = np.arange(S)
    f = np.zeros((S, LANES), np.float32)
    f[s, s // block] = 1.0
    f[:, FEAT + 0] = 1.0
    f[:, FEAT + 1] = 1.0
    f[:, FEAT + 2] = FEAT * (s // FEAT)
    f[:, FEAT + 3] = s % FEAT
    return jnp.asarray(f, BF16)


def _pad_to(a, n, axis):
    pad = [(0, 0)] * a.ndim
    pad[axis] = (0, n - a.shape[axis])
    return jnp.pad(a, pad)


def _even_layer(x2d, B, S, w_in, cmp_pos, cmp_w1, cmp_b1, cmp_w2, w_out, ln_g, ln_b, alpha):
    T, D = x2d.shape
    G, hd = NSA_KV_GROUPS, HEAD_DIM
    scale = hd ** -0.5
    wq, wkv, wgate, wnz, wmq, wmk, wmv, wmz = jnp.split(w_in, np.cumsum(EVEN_SPLITS)[:-1].tolist(), axis=1)
    wkc, wvc, wks, wvs, wkw, wvw = jnp.split(wkv, 6, axis=1)
    w_tok = jnp.concatenate([wkc, wvc, wks, wkw, wmk], axis=1).astype(BF16)
    w_tr = jnp.concatenate([wq * scale, wvs, wvw, wmq * scale, wmv, wnz, wmz,
                            _pad_to(wgate, GATE_ROWS, 1)], axis=1).T.astype(BF16)
    (kcmp, vcmp, ksel, kwin, mk, nq_t, vsel_t, vwin_t, mq_t, mv_t, nz_t, mz_t, gate_t) = _proj(
        x2d, w_tok, (NSA_KV_W,) * 4 + (MOBA_W,), (BF16,) * 5,
        w_tr, (NSA_W, NSA_KV_W, NSA_KV_W, MOBA_W, MOBA_W, NSA_W, MOBA_W, GATE_ROWS),
        (BF16,) * 5 + (F32, F32, F32))

    L, d, HID = NSA_CMP_BLOCK, NSA_CMP_STRIDE, NSA_CMP_HIDDEN
    ncmp = (S - L) // d + 1
    NC = S // d
    eye = jnp.eye(G, dtype=F32)
    w1r = cmp_w1.reshape(2, 2, d, hd, HID)
    w1p = jnp.einsum('ialdj,gh->ialgdhj', w1r, eye).reshape(2, 2, d * G * hd, G * HID).astype(BF16)
    posp = jnp.broadcast_to(cmp_pos.reshape(2, 2, d, 1, hd), (2, 2, d, G, hd)).reshape(2, 2, 1, d * G * hd)
    b1p = jnp.tile(cmp_b1, (1, G)).reshape(2, 1, G * HID)
    w2p = jnp.einsum('ijd,gh->igjhd', _pad_to(cmp_w2, LANES, 2), eye).reshape(2, G * HID, G * LANES).astype(BF16)
    kvc = _cmp_mlp(kcmp.reshape(B, NC, d * NSA_KV_W), vcmp.reshape(B, NC, d * NSA_KV_W),
                   w1p, posp, b1p, w2p, ncmp)

    ns = S // NSA_SEL_BLOCK
    cmp_start = np.arange(NC) * d
    sel_start = np.arange(FEAT) * NSA_SEL_BLOCK
    ovt = ((cmp_start[None, :] <= sel_start[:, None] + NSA_SEL_BLOCK - 1)
           & (cmp_start[None, :] + L - 1 >= sel_start[:, None])
           & (np.arange(NC)[None, :] < ncmp) & (np.arange(FEAT)[:, None] < ns))
    o_cmp, bias_sel = _cmp_attn(nq_t, kvc, jnp.asarray(ovt, BF16), B=B, S=S)

    feats_sel = _key_features(S, NSA_SEL_BLOCK)
    o_sel = _flash(nq_t, bias_sel, ksel, feats_sel, vsel_t, B=B, S=S, units=G, tq=TQ, nh=NSA_HG, window=None)
    o_win = _flash(nq_t, jnp.zeros_like(bias_sel), kwin, feats_sel, vwin_t,
                   B=B, S=S, units=G, tq=TQ, nh=NSA_HG, window=NSA_WINDOW)

    avg = np.zeros((MOBA_NBP, S), np.float32)
    avg[np.arange(S) // MOBA_BLOCK, np.arange(S)] = 1.0 / MOBA_BLOCK
    bias_m = _moba_gate(mq_t, mk, jnp.asarray(avg, BF16), B=B, S=S)
    o_m = _flash(mq_t, bias_m, mk, _key_features(S, MOBA_BLOCK), mv_t,
                 B=B, S=S, units=MOBA_HEADS, tq=TQM, nh=1, window=None)

    n = np.arange(NSA_W)[:, None]
    c = np.arange(GATE_ROWS)[None, :]
    expand = jnp.asarray(np.stack([(c == 3 * (n // hd) + br) for br in range(3)]), BF16)
    return _even_out(x2d, o_cmp, o_sel, o_win, o_m, gate_t, nz_t, mz_t, expand, w_out.astype(BF16),
                     ln_g.reshape(1, D), ln_b.reshape(1, D), alpha, B=B, S=S)


def _odd_layer(x2d, B, S, w_in, gate_w2, gate_b, gn_g, w_out, ln_g, ln_b, alpha):
    T, D = x2d.shape
    H, dk, dv, C, SUB = GLA_HEADS, GLA_DK, GLA_DV, GLA_CHUNK, GLA_SUB
    wq, wk, wv, wg, wz = jnp.split(w_in, np.cumsum(ODD_SPLITS)[:-1].tolist(), axis=1)
    w_perm = jnp.concatenate([wq, wk, wv, wz, _pad_to(wg, LANES, 1)], axis=1).astype(BF16)
    q, k, v, z, glr = _proj(x2d, w_perm, (H * dk, H * dk, H * dv, H * dv, LANES),
                            (BF16, BF16, BF16, F32, F32))
    w2p = _pad_to(gate_w2, LANES, 0)
    tril = jnp.asarray(np.tril(np.ones((C, C), np.float32)), BF16)
    ones_blk = np.zeros((C // SUB, SUB * dk, C), np.float32)
    for blk in range(C // SUB):
        for jj in range(SUB):
            ones_blk[blk, jj * dk:(jj + 1) * dk, blk * SUB + jj] = 1.0
    o = _gla(q, k, v, glr, w2p, gate_b.reshape(1, H * dk), tril, jnp.asarray(ones_blk, BF16), B=B, S=S)
    return _odd_out(x2d, o, z, gn_g.reshape(1, H * dv), w_out.astype(BF16),
                    ln_g.reshape(1, D), ln_b.reshape(1, D), alpha)


def kernel(x, ev_w_in, ev_cmp_pos, ev_cmp_w1, ev_cmp_b1, ev_cmp_w2, ev_w_out, ev_ln_g, ev_ln_b,
           od_w_in, od_gate_w2, od_gate_b, od_gn_g, od_w_out, od_ln_g, od_ln_b):
    B, S, D = x.shape
    depth = ev_w_in.shape[0] + od_w_in.shape[0]
    alpha = (2.0 * depth) ** 0.25
    h = x.reshape(B * S, D)
    for layer in range(depth):
        i = layer // 2
        if layer % 2 == 0:
            h = _even_layer(h, B, S, ev_w_in[i], ev_cmp_pos[i], ev_cmp_w1[i], ev_cmp_b1[i], ev_cmp_w2[i],
                            ev_w_out[i], ev_ln_g[i], ev_ln_b[i], alpha)
        else:
            h = _odd_layer(h, B, S, od_w_in[i], od_gate_w2[i], od_gate_b[i], od_gn_g[i],
                           od_w_out[i], od_ln_g[i], od_ln_b[i], alpha)
    return h.reshape(B, S, D)
```

```python
import functools

import numpy as np
import jax
import jax.numpy as jnp
from jax import lax
from jax.experimental import pallas as pl
from jax.experimental.pallas import tpu as pltpu

F32 = jnp.float32
BF16 = jnp.bfloat16

HEAD_DIM = 64
NSA_HEADS = 8
NSA_KV_GROUPS = 2
NSA_HG = NSA_HEADS // NSA_KV_GROUPS
NSA_CMP_BLOCK = 32
NSA_CMP_STRIDE = 16
NSA_CMP_HIDDEN = 128
NSA_SEL_BLOCK = 64
NSA_TOP_N = 16
NSA_WINDOW = 512
MOBA_HEADS = 8
MOBA_BLOCK = 256
MOBA_TOP_K = 3
GLA_HEADS = 4
GLA_DK = 128
GLA_DV = 256
GLA_GATE_RANK = 16
GLA_TAU = 16.0
GLA_CHUNK = 64
GLA_SUB = 16
GLA_STEP_CHUNKS = 4
GLA_FACTOR_MAX_DECAY = 60.0
LN_EPS = 1e-5
NEG = -1e30
FORCE_BONUS = 1e4

NSA_W = NSA_HEADS * HEAD_DIM
NSA_KV_W = NSA_KV_GROUPS * HEAD_DIM
MOBA_W = MOBA_HEADS * HEAD_DIM
EVEN_SPLITS = (NSA_W, 6 * NSA_KV_W, 3 * NSA_HEADS, NSA_W, MOBA_W, MOBA_W, MOBA_W, MOBA_W)
ODD_SPLITS = (GLA_HEADS * GLA_DK, GLA_HEADS * GLA_DK, GLA_HEADS * GLA_DV, GLA_GATE_RANK, GLA_HEADS * GLA_DV)

LANES = 128
SUBLANES = 8
FEAT = 64
GATE_ROWS = 32
MOBA_NBP = 16
TQ = 256
TQM = 512
TK = 512
CG = 256
VMEM_LIMIT = 48 * 1024 * 1024


def _dot(a, b):
    return jnp.dot(a, b, preferred_element_type=F32)


def _dot_nt(a, b):
    return lax.dot_general(a, b, (((1,), (1,)), ((), ())), preferred_element_type=F32)


def _dot_tn(a, b):
    return lax.dot_general(a, b, (((0,), (0,)), ((), ())), preferred_element_type=F32)


def _hilo(a):
    hi = a.astype(BF16)
    lo = (a - hi.astype(F32)).astype(BF16)
    return hi, lo


def _pow2_neg(n):
    return lax.bitcast_convert_type(lax.shift_left(127 - n, 23), F32)


def _sigmoid(x):
    return 1.0 / (1.0 + jnp.exp(-x))


def _layer_norm(r, g, b):
    mu = jnp.mean(r, axis=-1, keepdims=True)
    d = r - mu
    var = jnp.mean(d * d, axis=-1, keepdims=True)
    return d * lax.rsqrt(var + LN_EPS) * g + b


def _ranks(sc_ref, n_real):
    n_rows, width = sc_ref.shape
    out = []
    for v in range(n_rows // SUBLANES):
        lo = v * SUBLANES
        sc = sc_ref[lo:lo + SUBLANES, :]
        idx = lo + lax.broadcasted_iota(jnp.int32, (SUBLANES, width), 0)
        rank = jnp.zeros((SUBLANES, width), jnp.int32)
        for jp in range(n_real):
            r = sc_ref[pl.ds(jp, 1), :]
            if jp < lo:
                beats = r >= sc
            elif jp >= lo + SUBLANES - 1:
                beats = r > sc
            else:
                beats = (r > sc) | ((r == sc) & (idx > jp))
            rank = rank + jnp.where(beats, 1, 0)
        out.append(rank)
    return jnp.concatenate(out, axis=0)


def _proj_kernel(*refs, tok_splits, tr_splits):
    x_ref, wt_ref = refs[0], refs[1]
    outs = list(refs[3:] if tr_splits else refs[2:])
    x = x_ref[...]
    xb = x.astype(BF16)
    off = 0
    for n in tok_splits:
        o_ref = outs.pop(0)
        o_ref[...] = _dot(xb, wt_ref[:, off:off + n]).astype(o_ref.dtype)
        off += n
    if tr_splits:
        wtr_ref = refs[2]
        xt = x.T.astype(BF16)
        off = 0
        for n in tr_splits:
            o_ref = outs.pop(0)
            o_ref[...] = _dot(wtr_ref[off:off + n, :], xt).astype(o_ref.dtype)
            off += n


def _proj(x2d, w_tok, tok_splits, tok_dtypes, w_tr=None, tr_splits=(), tr_dtypes=(), tm=512):
    T, D = x2d.shape
    assert T % tm == 0 and sum(tok_splits) == w_tok.shape[1]
    in_specs = [pl.BlockSpec((tm, D), lambda i: (i, 0)),
                pl.BlockSpec(w_tok.shape, lambda i: (0, 0))]
    args = [x2d, w_tok]
    if tr_splits:
        assert sum(tr_splits) == w_tr.shape[0]
        in_specs.append(pl.BlockSpec(w_tr.shape, lambda i: (0, 0)))
        args.append(w_tr)
    out_specs = ([pl.BlockSpec((tm, n), lambda i: (i, 0)) for n in tok_splits]
                 + [pl.BlockSpec((n, tm), lambda i: (0, i)) for n in tr_splits])
    out_shape = ([jax.ShapeDtypeStruct((T, n), dt) for n, dt in zip(tok_splits, tok_dtypes)]
                 + [jax.ShapeDtypeStruct((n, T), dt) for n, dt in zip(tr_splits, tr_dtypes)])
    return pl.pallas_call(
        functools.partial(_proj_kernel, tok_splits=tuple(tok_splits), tr_splits=tuple(tr_splits)),
        grid=(T // tm,), in_specs=in_specs, out_specs=out_specs, out_shape=out_shape,
        compiler_params=pltpu.CompilerParams(dimension_semantics=("parallel",),
                                             vmem_limit_bytes=VMEM_LIMIT),
        name="proj",
    )(*args)


def _cmp_mlp_kernel(kx_ref, vx_ref, w1_ref, pos_ref, b1_ref, w2_ref, o_ref, *, ncmp):
    for i, x_ref in enumerate((kx_ref, vx_ref)):
        x = x_ref[...]
        nc, cw = x.shape
        a = _dot(x, w1_ref[i, 0])
        bm = _dot(x, w1_ref[i, 1])
        c = b1_ref[i]
        for half in range(2):
            phi, plo = _hilo(jnp.broadcast_to(pos_ref[i, half], (8, cw)))
            c = c + (_dot(phi, w1_ref[i, half]) + _dot(plo, w1_ref[i, half]))[0:1]
        hid = a + pltpu.roll(bm, nc - 1, axis=0) + c
        hid = hid * _sigmoid(hid)
        row = lax.broadcasted_iota(jnp.int32, hid.shape, 0)
        hid = jnp.where(row < ncmp, hid, 0.0)
        o_ref[i] = _dot(hid.astype(BF16), w2_ref[i]).astype(o_ref.dtype)


def _cmp_mlp(kx, vx, w1p, posp, b1p, w2p, ncmp):
    B, NC, CW = kx.shape
    OW = w2p.shape[-1]
    full = lambda a: pl.BlockSpec(a.shape, lambda b: (0,) * a.ndim)
    return pl.pallas_call(
        functools.partial(_cmp_mlp_kernel, ncmp=ncmp),
        grid=(B,),
        in_specs=[pl.BlockSpec((None, NC, CW), lambda b: (b, 0, 0)),
                  pl.BlockSpec((None, NC, CW), lambda b: (b, 0, 0)),
                  full(w1p), full(posp), full(b1p), full(w2p)],
        out_specs=pl.BlockSpec((2, None, NC, OW), lambda b: (0, b, 0, 0)),
        out_shape=jax.ShapeDtypeStruct((2, B, NC, OW), BF16),
        compiler_params=pltpu.CompilerParams(dimension_semantics=("parallel",),
                                             vmem_limit_bytes=VMEM_LIMIT),
        name="cmp_mlp",
    )(kx, vx, w1p, posp, b1p, w2p)


def _cmp_attn_kernel(q_ref, kc_ref, vc_ref, ovt_ref, o_ref, bias_ref, sc_ref, *, tq, nh, ncmp, ns, ntop):
    g = pl.program_id(1)
    qi = pl.program_id(2)
    R = nh * tq
    nc = kc_ref.shape[0]
    qcat = jnp.concatenate([q_ref[hh * HEAD_DIM:(hh + 1) * HEAD_DIM, :] for hh in range(nh)], axis=1)
    s = _dot(kc_ref[:, 0:HEAD_DIM], qcat)
    n = lax.broadcasted_iota(jnp.int32, (nc, R), 0)
    col = lax.broadcasted_iota(jnp.int32, (nc, R), 1)
    t = qi * tq + col % tq
    slope = _pow2_neg(g * nh + col // tq + 1)
    cmp_end = n * NSA_CMP_STRIDE + (NSA_CMP_BLOCK - 1)
    s = s - slope * (t - cmp_end).astype(F32)
    valid = (cmp_end <= t) & (n < ncmp)
    s = jnp.where(valid, s, NEG)
    e = jnp.exp(s - jnp.max(s, axis=0, keepdims=True))
    e = jnp.where(n < ncmp, e, 0.0)
    p = e / jnp.sum(e, axis=0, keepdims=True) * jnp.where(valid, 1.0, 0.0)
    o_ref[...] = _dot_tn(vc_ref[:, 0:HEAD_DIM], p.astype(BF16))

    psum = p[:, 0:tq]
    for hh in range(1, nh):
        psum = psum + p[:, hh * tq:(hh + 1) * tq]
    phi, plo = _hilo(psum)
    imp = _dot(ovt_ref[...], phi) + _dot(ovt_ref[...], plo)

    j = lax.broadcasted_iota(jnp.int32, (FEAT, tq), 0)
    cur = (qi * tq + lax.broadcasted_iota(jnp.int32, (FEAT, tq), 1)) // NSA_SEL_BLOCK
    forced = jnp.where((j == 0) | (j == cur) | (j == cur - 1), FORCE_BONUS, 0.0)
    sc = jnp.where(j <= cur, imp + forced, NEG)
    sc_ref[...] = sc
    sel = (_ranks(sc_ref, ns) < ntop) & (j <= cur)
    bias_ref[...] = jnp.where(sel, 0.0, NEG).astype(bias_ref.dtype)


def _cmp_attn(nq_t, kvc, ovt, *, B, S):
    tq, nh, G = TQ, NSA_HG, NSA_KV_GROUPS
    R = nh * tq
    nq = S // tq
    NC = kvc.shape[2]
    ncmp = (S - NSA_CMP_BLOCK) // NSA_CMP_STRIDE + 1
    ns = S // NSA_SEL_BLOCK
    assert ns <= FEAT
    ntop = min(NSA_TOP_N, ns)
    return pl.pallas_call(
        functools.partial(_cmp_attn_kernel, tq=tq, nh=nh, ncmp=ncmp, ns=ns, ntop=ntop),
        grid=(B, G, nq),
        in_specs=[pl.BlockSpec((nh * HEAD_DIM, tq), lambda b, g, i: (g, b * nq + i)),
                  pl.BlockSpec((None, None, NC, LANES), lambda b, g, i: (0, b, 0, g)),
                  pl.BlockSpec((None, None, NC, LANES), lambda b, g, i: (1, b, 0, g)),
                  pl.BlockSpec(ovt.shape, lambda b, g, i: (0, 0))],
        out_specs=[pl.BlockSpec((None, None, HEAD_DIM, R), lambda b, g, i: (b, g, 0, i)),
                   pl.BlockSpec((None, None, FEAT, tq), lambda b, g, i: (b, g, 0, i))],
        out_shape=[jax.ShapeDtypeStruct((B, G, HEAD_DIM, S * nh), F32),
                   jax.ShapeDtypeStruct((B, G, FEAT, S), BF16)],
        scratch_shapes=[pltpu.VMEM((FEAT, tq), F32)],
        compiler_params=pltpu.CompilerParams(dimension_semantics=("parallel", "parallel", "parallel"),
                                             vmem_limit_bytes=VMEM_LIMIT),
        name="cmp_attn",
    )(nq_t, kvc, kvc, ovt)


def _moba_gate_kernel(q_ref, k_ref, a_ref, bias_ref, km_ref, sc_ref, *, tq, nb, ntop):
    h = pl.program_id(1)
    qi = pl.program_id(2)

    @pl.when(qi == 0)
    def _means():
        km_ref[...] = _dot(a_ref[...], k_ref[...])

    nbp = km_ref.shape[0]
    lane = lax.broadcasted_iota(jnp.int32, (nbp, LANES), 1)
    km = jnp.where(lane // HEAD_DIM == h % 2, km_ref[...], 0.0)
    khi, klo = _hilo(km)
    q = q_ref[...]
    qq = jnp.concatenate([q, q], axis=0)
    g_t = _dot(khi, qq) + _dot(klo, qq)
    n = lax.broadcasted_iota(jnp.int32, (nbp, tq), 0)
    cur = (qi * tq + lax.broadcasted_iota(jnp.int32, (nbp, tq), 1)) // MOBA_BLOCK
    sc = jnp.where(n < cur, g_t, NEG)
    sc_ref[...] = sc
    sel = ((_ranks(sc_ref, nb) < ntop) & (n < cur)) | (n == cur)
    bias_ref[0:nbp, :] = jnp.where(sel, 0.0, NEG).astype(bias_ref.dtype)
    bias_ref[nbp:FEAT, :] = jnp.zeros((FEAT - nbp, tq), bias_ref.dtype)


def _moba_gate(mq_t, mk, avg, *, B, S):
    tq, H = S, MOBA_HEADS
    nq = S // tq
    nb = S // MOBA_BLOCK
    nbp = avg.shape[0]
    assert nb <= nbp
    ntop = min(MOBA_TOP_K, nb)
    return pl.pallas_call(
        functools.partial(_moba_gate_kernel, tq=tq, nb=nb, ntop=ntop),
        grid=(B, H, nq),
        in_specs=[pl.BlockSpec((HEAD_DIM, tq), lambda b, h, i: (h, b * nq + i)),
                  pl.BlockSpec((S, LANES), lambda b, h, i: (b, h // 2)),
                  pl.BlockSpec(avg.shape, lambda b, h, i: (0, 0))],
        out_specs=pl.BlockSpec((None, None, FEAT, tq), lambda b, h, i: (b, h, 0, i)),
        out_shape=jax.ShapeDtypeStruct((B, H, FEAT, S), BF16),
        scratch_shapes=[pltpu.VMEM((nbp, LANES), F32), pltpu.VMEM((nbp, tq), F32)],
        compiler_params=pltpu.CompilerParams(dimension_semantics=("parallel", "parallel", "arbitrary"),
                                             vmem_limit_bytes=VMEM_LIMIT),
        name="moba_gate",
    )(mq_t, mk, avg)


def _flash_kernel(q_ref, bias_ref, k_ref, kf_ref, vt_ref, o_ref, ka_ref, vta_ref, qa_ref, m_ref, acc_ref,
                  alpha_ref, p_ref, *, tq, tk, nh, pair, window):
    u = pl.program_id(1)
    qi = pl.program_id(2)
    R = nh * tq
    nv = nh if pair else 1
    q0 = qi * tq
    S = k_ref.shape[0]

    @pl.when(qi == 0)
    def _stage_keys_values():
        ka_ref[:, 0:LANES] = k_ref[...]
        ka_ref[:, LANES:2 * LANES] = kf_ref[...]
        ones_row = jnp.where(lax.broadcasted_iota(jnp.int32, (HEAD_DIM, tk), 0) == 0, 1.0, 0.0)
        for v in range(nv):
            for jt in range(S // tk):
                vta_ref[v, jt, 0:HEAD_DIM, :] = vt_ref[v * HEAD_DIM:(v + 1) * HEAD_DIM, jt * tk:(jt + 1) * tk]
                vta_ref[v, jt, HEAD_DIM:2 * HEAD_DIM, :] = ones_row.astype(BF16)

    qcat = jnp.concatenate([q_ref[hh * HEAD_DIM:(hh + 1) * HEAD_DIM, :] for hh in range(nh)], axis=1)
    qq = jnp.concatenate([qcat, qcat], axis=0).astype(F32)
    rowi = lax.broadcasted_iota(jnp.int32, (2 * HEAD_DIM, R), 0)
    coli = lax.broadcasted_iota(jnp.int32, (2 * HEAD_DIM, R), 1)
    half = coli // tq if pair else u % 2
    qa_ref[0:2 * HEAD_DIM, :] = jnp.where(rowi // HEAD_DIM == half, qq, 0.0).astype(BF16)
    qa_ref[2 * HEAD_DIM:2 * HEAD_DIM + FEAT, :] = jnp.concatenate(
        [bias_ref[hh if pair else 0] for hh in range(nh)], axis=1)
    r = lax.broadcasted_iota(jnp.int32, (FEAT, R), 0)
    col = lax.broadcasted_iota(jnp.int32, (FEAT, R), 1)
    t = q0 + col % tq
    slope = _pow2_neg(u * nh + col // tq + 1)
    tb = (t // FEAT).astype(F32)
    tr = (t % FEAT).astype(F32)
    feats = jnp.where(r == 0, -FEAT * slope * tb,
                      jnp.where(r == 1, -slope * tr, jnp.where(r < 4, slope, 0.0)))
    qa_ref[2 * HEAD_DIM + FEAT:2 * LANES, :] = feats.astype(BF16)
    m_ref[...] = jnp.full(m_ref.shape, NEG, F32)
    acc_ref[...] = jnp.zeros(acc_ref.shape, F32)

    ncg = R // CG

    def values(ki):
        for c in range(ncg):
            cols = pl.ds(c * CG, CG)
            v = (c * CG) // tq if pair else 0
            acc_ref[:, cols] = alpha_ref[0:1, cols] * acc_ref[:, cols] + _dot(vta_ref[v, ki], p_ref[:, cols])

    def scores(ki, masked, prev=None):
        keys = pl.ds(pl.multiple_of(ki * tk, tk), tk)
        ka = ka_ref[keys, :]
        s_groups = [_dot(ka, qa_ref[:, pl.ds(c * CG, CG)]) for c in range(ncg)]
        if prev is not None:
            values(prev)
        for c, s in enumerate(s_groups):
            cols = pl.ds(c * CG, CG)
            if masked:
                krow = lax.broadcasted_iota(jnp.int32, (tk, CG), 0)
                qcol = c * CG + lax.broadcasted_iota(jnp.int32, (tk, CG), 1)
                rel = (q0 - ki * tk) + qcol % tq - krow
                ok = rel >= 0
                if window is not None:
                    ok = ok & (rel < window)
                s = jnp.where(ok, s, NEG)
            m_prev = m_ref[0:1, cols]
            m_new = jnp.maximum(m_prev, jnp.max(s, axis=0, keepdims=True))
            alpha_ref[:, cols] = jnp.broadcast_to(jnp.exp(m_prev - m_new), (alpha_ref.shape[0], CG))
            p_ref[:, cols] = jnp.exp(s - m_new).astype(BF16)
            m_ref[:, cols] = jnp.broadcast_to(m_new, (m_ref.shape[0], CG))

    def steady(lo, hi, masked):
        def body(ki, carry):
            scores(ki, masked, prev=ki - 1)
            return carry
        lax.fori_loop(lo, hi, body, 0)

    kd = (q0 + tq - 1) // tk
    if window is None:
        @pl.when(kd == 0)
        def _only_diagonal():
            scores(0, True)

        @pl.when(kd > 0)
        def _past_then_diagonal():
            scores(0, False)
            steady(1, kd, False)
            scores(kd, True, prev=kd - 1)
    else:
        k_lo = jnp.maximum(q0 - window + 1, 0) // tk
        scores(k_lo, True)
        steady(k_lo + 1, kd + 1, True)
    values(kd)

    acc = acc_ref[...]
    out = acc[0:HEAD_DIM] / acc[HEAD_DIM:HEAD_DIM + 1]
    if pair:
        for hh in range(nh):
            o_ref[hh] = out[:, hh * tq:(hh + 1) * tq]
    else:
        o_ref[0] = out


def _flash(q_t, bias_t, k, kfeat, v_t, *, B, S, units, tq, nh, pair, window):
    tk = TK
    R = nh * tq
    nq = S // tq
    nv = nh if pair else 1
    nb = nh if pair else 1
    assert tk % tq == 0 and S % tk == 0 and R % CG == 0 and (not pair or tq % CG == 0)
    kcol = (lambda u: u) if pair else (lambda u: u // 2)
    out_w = tq if pair else R
    out_shape = (B, units * nb, HEAD_DIM, S if pair else S * nh)
    return pl.pallas_call(
        functools.partial(_flash_kernel, tq=tq, tk=tk, nh=nh, pair=pair, window=window),
        grid=(B, units, nq),
        in_specs=[pl.BlockSpec((nh * HEAD_DIM, tq), lambda b, u, i: (u, b * nq + i)),
                  pl.BlockSpec((None, nb, FEAT, tq), lambda b, u, i: (b, u, 0, i)),
                  pl.BlockSpec((S, LANES), lambda b, u, i: (b, kcol(u))),
                  pl.BlockSpec((S, LANES), lambda b, u, i: (0, 0)),
                  pl.BlockSpec((nv * HEAD_DIM, S), lambda b, u, i: (u, b))],
        out_specs=pl.BlockSpec((None, nb, HEAD_DIM, out_w), lambda b, u, i: (b, u, 0, i)),
        out_shape=jax.ShapeDtypeStruct(out_shape, F32),
        scratch_shapes=[pltpu.VMEM((S, 2 * LANES), BF16),
                        pltpu.VMEM((nv, S // tk, 2 * HEAD_DIM, tk), BF16),
                        pltpu.VMEM((2 * LANES, R), BF16),
                        pltpu.VMEM((8, R), F32),
                        pltpu.VMEM((2 * HEAD_DIM, R), F32),
                        pltpu.VMEM((8, R), F32),
                        pltpu.VMEM((tk, R), BF16)],
        compiler_params=pltpu.CompilerParams(dimension_semantics=("parallel", "parallel", "arbitrary"),
                                             vmem_limit_bytes=VMEM_LIMIT),
        name="flash",
    )(q_t, bias_t, k, kfeat, v_t)


def _even_out_kernel(x_ref, ocmp_ref, osel_ref, owin_ref, om_ref, gate_ref, nz_ref, mz_ref,
                     e_ref, w_ref, g_ref, b_ref, o_ref, *, alpha, tm):
    G, nh = NSA_KV_GROUPS, NSA_HG
    sig = _sigmoid(gate_ref[...])
    shi, slo = _hilo(sig)

    def heads_t(ref):
        return jnp.concatenate([ref[g, :, hh * tm:(hh + 1) * tm] for g in range(G) for hh in range(nh)],
                               axis=0)

    o_nsa = None
    for br, ref in enumerate((ocmp_ref, osel_ref, owin_ref)):
        gb = _dot(e_ref[br], shi) + _dot(e_ref[br], slo)
        term = gb * heads_t(ref)
        o_nsa = term if o_nsa is None else o_nsa + term
    nz = nz_ref[...]
    mz = mz_ref[...]
    o_m = jnp.concatenate([om_ref[h] for h in range(MOBA_HEADS)], axis=0)
    a_t = jnp.concatenate([o_nsa * (nz * _sigmoid(nz)), o_m * (mz * _sigmoid(mz))], axis=0)
    y = _dot_tn(a_t.astype(BF16), w_ref[...])
    o_ref[...] = _layer_norm(alpha * x_ref[...] + y, g_ref[...], b_ref[...])


def _even_out(x2d, ocmp, osel, owin, om, gate_t, nz_t, mz_t, expand, w_out, ln_g, ln_b, alpha, *, B, S):
    T, D = x2d.shape
    tm = TQ
    nq = S // tm
    G, nh, H = NSA_KV_GROUPS, NSA_HG, MOBA_HEADS
    nsa = pl.BlockSpec((None, G, HEAD_DIM, nh * tm), lambda i: (i // nq, 0, 0, i % nq))
    feat = lambda n: pl.BlockSpec((n, tm), lambda i: (0, i))
    full = lambda a: pl.BlockSpec(a.shape, lambda i: (0,) * a.ndim)
    return pl.pallas_call(
        functools.partial(_even_out_kernel, alpha=alpha, tm=tm),
        grid=(T // tm,),
        in_specs=[pl.BlockSpec((tm, D), lambda i: (i, 0)), nsa, nsa, nsa,
                  pl.BlockSpec((None, H, HEAD_DIM, tm), lambda i: (i // nq, 0, 0, i % nq)),
                  feat(GATE_ROWS), feat(NSA_W), feat(MOBA_W),
                  full(expand), full(w_out), full(ln_g), full(ln_b)],
        out_specs=pl.BlockSpec((tm, D), lambda i: (i, 0)),
        out_shape=jax.ShapeDtypeStruct((T, D), F32),
        compiler_params=pltpu.CompilerParams(dimension_semantics=("parallel",),
                                             vmem_limit_bytes=VMEM_LIMIT),
        name="even_out",
    )(x2d, ocmp, osel, owin, om, gate_t, nz_t, mz_t, expand, w_out, ln_g, ln_b)


def _gla_head_exact(row0, h, q_ref, v_ref, o_ref, st_ref, b_ref, kf_ref, scale):
    C, dk, dv, SUB = GLA_CHUNK, GLA_DK, GLA_DV, GLA_SUB
    rows = lax.broadcasted_iota(jnp.int32, (C, dk), 0)
    sub_i = lax.broadcasted_iota(jnp.int32, (SUB, LANES), 0)
    lane = lax.broadcasted_iota(jnp.int32, (SUB, LANES), 1)
    ones = jnp.ones((dk, LANES), BF16)
    sl = pl.ds(h * dk, dk)
    rs_c = pl.ds(row0, C)
    bh = b_ref[:, sl]
    kf = kf_ref[:, sl]
    qf = q_ref[rs_c, sl].astype(F32) * scale
    vh = v_ref[rs_c, pl.ds(h * dv, dv)]
    st = st_ref[h]
    inter = _dot_nt((qf * jnp.exp(bh)).astype(BF16), st.astype(BF16))
    bl = b_ref[pl.ds(C - 1, 1), sl]
    kd = kf * jnp.exp(bl - bh)

    def off_block(r0, r1, c0, c1):
        ref_row = b_ref[pl.ds(r0, 1), sl]
        qt = qf[r0:r1] * jnp.exp(bh[r0:r1] - ref_row)
        ek = jnp.where((rows >= c0) & (rows < c1), ref_row - bh, NEG)
        kt = kf * jnp.exp(ek)
        return _dot_nt(qt.astype(BF16), kt.astype(BF16))

    lower = off_block(2 * SUB, C, 0, 2 * SUB)
    blocks = [None, off_block(SUB, 2 * SUB, 0, SUB), lower[0:SUB],
              lower[SUB:2 * SUB] + off_block(3 * SUB, C, 2 * SUB, 3 * SUB)]
    pieces = []
    for blk in range(C // SUB):
        r0 = blk * SUB
        q_i = qf[r0:r0 + SUB]
        b_i = bh[r0:r0 + SUB]
        for jj in range(SUB):
            b_j = b_ref[pl.ds(r0 + jj, 1), sl]
            k_j = kf_ref[pl.ds(r0 + jj, 1), sl]
            pieces.append((q_i * k_j * jnp.exp(jnp.minimum(b_i - b_j, 0.0))).astype(BF16))
    sums = _dot(jnp.concatenate(pieces, axis=0), ones)
    for blk in range(C // SUB):
        r0 = blk * SUB
        diag = jnp.zeros((SUB, LANES), F32)
        for jj in range(SUB):
            rs = sums[(r0 + jj) * SUB:(r0 + jj + 1) * SUB]
            diag = jnp.where((lane == r0 + jj) & (sub_i >= jj), rs, diag)
        diag = diag[:, 0:C]
        blocks[blk] = diag if blocks[blk] is None else blocks[blk] + diag
    att = jnp.concatenate(blocks, axis=0)
    o_ref[rs_c, pl.ds(h * dv, dv)] = inter + _dot(att.astype(BF16), vh)
    st_ref[h] = st * jnp.exp(bl) + _dot_tn(vh, kd.astype(BF16))


def _gla_head_bounded(row0, h, q_ref, v_ref, o_ref, st_ref, b_ref, kf_ref, scale):
    C, dk, dv = GLA_CHUNK, GLA_DK, GLA_DV
    sl = pl.ds(h * dk, dk)
    rs_c = pl.ds(row0, C)
    bh = b_ref[rs_c, sl]
    kf = kf_ref[rs_c, sl]
    qf = q_ref[rs_c, sl].astype(F32) * scale
    vh = v_ref[rs_c, pl.ds(h * dv, dv)]
    st = st_ref[h]
    qs = (qf * jnp.exp(bh)).astype(BF16)
    kt = (kf * jnp.exp(-bh)).astype(BF16)
    res = _dot_nt(qs, jnp.concatenate([st.astype(BF16), kt], axis=0))
    i = lax.broadcasted_iota(jnp.int32, (C, C), 0)
    j = lax.broadcasted_iota(jnp.int32, (C, C), 1)
    att = jnp.where(i >= j, res[:, dv:dv + C], 0.0)
    o_ref[rs_c, pl.ds(h * dv, dv)] = res[:, 0:dv] + _dot(att.astype(BF16), vh)
    bl = b_ref[pl.ds(row0 + C - 1, 1), sl]
    kd = kf * jnp.exp(bl - bh)
    st_ref[h] = st * jnp.exp(bl) + _dot_tn(vh, kd.astype(BF16))


def _gla_kernel(q_ref, k_ref, v_ref, glr_ref, w2_ref, gb_ref, tril_ref, o_ref,
                st_ref, b_ref, kf_ref, bc_ref, kc_ref, *, scale):
    C, H = GLA_CHUNK, GLA_HEADS
    nch = q_ref.shape[0] // C

    @pl.when(pl.program_id(1) == 0)
    def _init():
        st_ref[...] = jnp.zeros(st_ref.shape, F32)

    ghi, glo = _hilo(glr_ref[...])
    whi, wlo = _hilo(w2_ref[...])
    pre = _dot(ghi, whi) + _dot(glo, whi) + _dot(ghi, wlo) + gb_ref[...]
    la = (jnp.minimum(pre, 0.0) - jnp.log1p(jnp.exp(-jnp.abs(pre)))) * (1.0 / GLA_TAU)
    lhi, llo = _hilo(la)
    b_all = _dot(tril_ref[...], lhi) + _dot(tril_ref[...], llo)
    b_ref[...] = b_all
    kf_ref[...] = k_ref[...].astype(F32)
    decay = -b_all[C - 1:C, :]
    for c in range(1, nch):
        decay = jnp.maximum(decay, -b_all[c * C + C - 1:(c + 1) * C, :])
    bounded = jnp.max(decay) <= GLA_FACTOR_MAX_DECAY

    @pl.when(bounded)
    def _factored():
        for c in range(nch):
            for h in range(H):
                _gla_head_bounded(c * C, h, q_ref, v_ref, o_ref, st_ref, b_ref, kf_ref, scale)

    @pl.when(jnp.logical_not(bounded))
    def _exact():
        def chunk(c, carry):
            row0 = pl.multiple_of(c * C, C)
            bc_ref[...] = b_ref[pl.ds(row0, C), :]
            kc_ref[...] = kf_ref[pl.ds(row0, C), :]
            for h in range(H):
                _gla_head_exact(row0, h, q_ref, v_ref, o_ref, st_ref, bc_ref, kc_ref, scale)
            return carry
        lax.fori_loop(0, nch, chunk, 0)


def _gla(q, k, v, glr, w2p, gb, tril, *, B, S):
    C, H, dk, dv = GLA_CHUNK, GLA_HEADS, GLA_DK, GLA_DV
    rows = tril.shape[0]
    nchunk = S // rows
    tok = lambda n: pl.BlockSpec((rows, n), lambda b, c: (b * nchunk + c, 0))
    const = lambda shape: pl.BlockSpec(shape, lambda b, c: (0,) * len(shape))
    return pl.pallas_call(
        functools.partial(_gla_kernel, scale=float(dk) ** -0.5),
        grid=(B, nchunk),
        in_specs=[tok(H * dk), tok(H * dk), tok(H * dv), tok(LANES),
                  const(w2p.shape), const(gb.shape), const(tril.shape)],
        out_specs=tok(H * dv),
        out_shape=jax.ShapeDtypeStruct((B * S, H * dv), F32),
        scratch_shapes=[pltpu.VMEM((H, dv, dk), F32),
                        pltpu.VMEM((rows, H * dk), F32),
                        pltpu.VMEM((rows, H * dk), F32),
                        pltpu.VMEM((C, H * dk), F32),
                        pltpu.VMEM((C, H * dk), F32)],
        compiler_params=pltpu.CompilerParams(dimension_semantics=("parallel", "arbitrary"),
                                             vmem_limit_bytes=VMEM_LIMIT),
        name="gla",
    )(q, k, v, glr, w2p, gb, tril)


def _odd_out_kernel(x_ref, o_ref_in, z_ref, gn_ref, w_ref, g_ref, b_ref, out_ref, *, alpha):
    H, dv = GLA_HEADS, GLA_DV
    parts = []
    for h in range(H):
        sl = pl.ds(h * dv, dv)
        oh = o_ref_in[:, sl]
        r = lax.rsqrt(jnp.mean(oh * oh, axis=-1, keepdims=True) + LN_EPS)
        z = z_ref[:, sl]
        parts.append((oh * r * gn_ref[:, sl] * (z * _sigmoid(z))).astype(BF16))
    y = _dot(jnp.concatenate(parts, axis=1), w_ref[...])
    out_ref[...] = _layer_norm(alpha * x_ref[...] + y, g_ref[...], b_ref[...])


def _odd_out(x2d, o, z, gn_g, w_out, ln_g, ln_b, alpha, tm=256):
    T, D = x2d.shape
    W = o.shape[1]
    row = lambda n: pl.BlockSpec((tm, n), lambda i: (i, 0))
    return pl.pallas_call(
        functools.partial(_odd_out_kernel, alpha=alpha),
        grid=(T // tm,),
        in_specs=[row(D), row(W), row(W),
                  pl.BlockSpec((1, W), lambda i: (0, 0)),
                  pl.BlockSpec((W, D), lambda i: (0, 0)),
                  pl.BlockSpec((1, D), lambda i: (0, 0)),
                  pl.BlockSpec((1, D), lambda i: (0, 0))],
        out_specs=row(D),
        out_shape=jax.ShapeDtypeStruct((T, D), F32),
        compiler_params=pltpu.CompilerParams(dimension_semantics=("parallel",),
                                             vmem_limit_bytes=VMEM_LIMIT),
        name="odd_out",
    )(x2d, o, z, gn_g, w_out, ln_g, ln_b)


def _key_features(S, block):
    s = np.arange(S)
    f = np.zeros((S, LANES), np.float32)
    f[s, s // block] = 1.0
    f[:, FEAT + 0] = 1.0
    f[:, FEAT + 1] = 1.0
    f[:, FEAT + 2] = FEAT * (s // FEAT)
    f[:, FEAT + 3] = s % FEAT
    return jnp.asarray(f, BF16)


def _pad_to(a, n, axis):
    pad = [(0, 0)] * a.ndim
    pad[axis] = (0, n - a.shape[axis])
    return jnp.pad(a, pad)


def _even_layer(x2d, B, S, w_in, cmp_pos, cmp_w1, cmp_b1, cmp_w2, w_out, ln_g, ln_b, alpha):
    T, D = x2d.shape
    G, hd = NSA_KV_GROUPS, HEAD_DIM
    scale = hd ** -0.5
    wq, wkv, wgate, wnz, wmq, wmk, wmv, wmz = jnp.split(w_in, np.cumsum(EVEN_SPLITS)[:-1].tolist(), axis=1)
    wkc, wvc, wks, wvs, wkw, wvw = jnp.split(wkv, 6, axis=1)
    w_tok = jnp.concatenate([wkc, wvc, wks, wkw, wmk], axis=1).astype(BF16)
    w_tr = jnp.concatenate([wq * scale, wvs, wvw, wmq * scale, wmv, wnz, wmz,
                            _pad_to(wgate, GATE_ROWS, 1)], axis=1).T.astype(BF16)
    (kcmp, vcmp, ksel, kwin, mk, nq_t, vsel_t, vwin_t, mq_t, mv_t, nz_t, mz_t, gate_t) = _proj(
        x2d, w_tok, (NSA_KV_W,) * 4 + (MOBA_W,), (BF16,) * 5,
        w_tr, (NSA_W, NSA_KV_W, NSA_KV_W, MOBA_W, MOBA_W, NSA_W, MOBA_W, GATE_ROWS),
        (BF16,) * 5 + (F32, F32, F32))

    L, d, HID = NSA_CMP_BLOCK, NSA_CMP_STRIDE, NSA_CMP_HIDDEN
    ncmp = (S - L) // d + 1
    NC = S // d
    eye = jnp.eye(G, dtype=F32)
    w1r = cmp_w1.reshape(2, 2, d, hd, HID)
    w1p = jnp.einsum('ialdj,gh->ialgdhj', w1r, eye).reshape(2, 2, d * G * hd, G * HID).astype(BF16)
    posp = jnp.broadcast_to(cmp_pos.reshape(2, 2, d, 1, hd), (2, 2, d, G, hd)).reshape(2, 2, 1, d * G * hd)
    b1p = jnp.tile(cmp_b1, (1, G)).reshape(2, 1, G * HID)
    w2p = jnp.einsum('ijd,gh->igjhd', _pad_to(cmp_w2, LANES, 2), eye).reshape(2, G * HID, G * LANES).astype(BF16)
    kvc = _cmp_mlp(kcmp.reshape(B, NC, d * NSA_KV_W), vcmp.reshape(B, NC, d * NSA_KV_W),
                   w1p, posp, b1p, w2p, ncmp)

    ns = S // NSA_SEL_BLOCK
    cmp_start = np.arange(NC) * d
    sel_start = np.arange(FEAT) * NSA_SEL_BLOCK
    ovt = ((cmp_start[None, :] <= sel_start[:, None] + NSA_SEL_BLOCK - 1)
           & (cmp_start[None, :] + L - 1 >= sel_start[:, None])
           & (np.arange(NC)[None, :] < ncmp) & (np.arange(FEAT)[:, None] < ns))
    o_cmp, bias_sel = _cmp_attn(nq_t, kvc, jnp.asarray(ovt, BF16), B=B, S=S)

    feats_sel = _key_features(S, NSA_SEL_BLOCK)
    o_sel = _flash(nq_t, bias_sel, ksel, feats_sel, vsel_t,
                   B=B, S=S, units=G, tq=TQ, nh=NSA_HG, pair=False, window=None)
    o_win = _flash(nq_t, jnp.zeros_like(bias_sel), kwin, feats_sel, vwin_t,
                   B=B, S=S, units=G, tq=TQ, nh=NSA_HG, pair=False, window=NSA_WINDOW)

    avg = np.zeros((MOBA_NBP, S), np.float32)
    avg[np.arange(S) // MOBA_BLOCK, np.arange(S)] = 1.0 / MOBA_BLOCK
    bias_m = _moba_gate(mq_t, mk, jnp.asarray(avg, BF16), B=B, S=S)
    o_m = _flash(mq_t, bias_m, mk, _key_features(S, MOBA_BLOCK), mv_t,
                 B=B, S=S, units=MOBA_HEADS // 2, tq=TQM, nh=2, pair=True, window=None)

    n = np.arange(NSA_W)[:, None]
    c = np.arange(GATE_ROWS)[None, :]
    expand = jnp.asarray(np.stack([(c == 3 * (n // hd) + br) for br in range(3)]), BF16)
    return _even_out(x2d, o_cmp, o_sel, o_win, o_m, gate_t, nz_t, mz_t, expand, w_out.astype(BF16),
                     ln_g.reshape(1, D), ln_b.reshape(1, D), alpha, B=B, S=S)


def _odd_layer(x2d, B, S, w_in, gate_w2, gate_b, gn_g, w_out, ln_g, ln_b, alpha):
    T, D = x2d.shape
    H, dk, dv, C = GLA_HEADS, GLA_DK, GLA_DV, GLA_CHUNK
    wq, wk, wv, wg, wz = jnp.split(w_in, np.cumsum(ODD_SPLITS)[:-1].tolist(), axis=1)
    w_perm = jnp.concatenate([wq, wk, wv, wz, _pad_to(wg, LANES, 1)], axis=1).astype(BF16)
    q, k, v, z, glr = _proj(x2d, w_perm, (H * dk, H * dk, H * dv, H * dv, LANES),
                            (BF16, BF16, BF16, F32, F32))
    w2p = _pad_to(gate_w2, LANES, 0)
    nch = min(GLA_STEP_CHUNKS, S // C)
    tril = jnp.asarray(np.kron(np.eye(nch), np.tril(np.ones((C, C)))), BF16)
    o = _gla(q, k, v, glr, w2p, gate_b.reshape(1, H * dk), tril, B=B, S=S)
    return _odd_out(x2d, o, z, gn_g.reshape(1, H * dv), w_out.astype(BF16),
                    ln_g.reshape(1, D), ln_b.reshape(1, D), alpha)


def kernel(x, ev_w_in, ev_cmp_pos, ev_cmp_w1, ev_cmp_b1, ev_cmp_w2, ev_w_out, ev_ln_g, ev_ln_b,
           od_w_in, od_gate_w2, od_gate_b, od_gn_g, od_w_out, od_ln_g, od_ln_b):
    B, S, D = x.shape
    depth = ev_w_in.shape[0] + od_w_in.shape[0]
    alpha = (2.0 * depth) ** 0.25
    h = x.reshape(B * S, D)
    for layer in range(depth):
        i = layer // 2
        if layer % 2 == 0:
            h = _even_layer(h, B, S, ev_w_in[i], ev_cmp_pos[i], ev_cmp_w1[i], ev_cmp_b1[i], ev_cmp_w2[i],
                            ev_w_out[i], ev_ln_g[i], ev_ln_b[i], alpha)
        else:
            h = _odd_layer(h, B, S, od_w_in[i], od_gate_w2[i], od_gate_b[i], od_gn_g[i],
                           od_w_out[i], od_ln_g[i], od_ln_b[i], alpha)
    return h.reshape(B, S, D)
```

```python
import functools

import numpy as np
import jax
import jax.numpy as jnp
from jax import lax
from jax.experimental import pallas as pl
from jax.experimental.pallas import tpu as pltpu

F32 = jnp.float32
BF16 = jnp.bfloat16

HEAD_DIM = 64
NSA_HEADS = 8
NSA_KV_GROUPS = 2
NSA_HG = NSA_HEADS // NSA_KV_GROUPS
NSA_CMP_BLOCK = 32
NSA_CMP_STRIDE = 16
NSA_CMP_HIDDEN = 128
NSA_SEL_BLOCK = 64
NSA_TOP_N = 16
NSA_WINDOW = 512
MOBA_HEADS = 8
MOBA_BLOCK = 256
MOBA_TOP_K = 3
GLA_HEADS = 4
GLA_DK = 128
GLA_DV = 256
GLA_GATE_RANK = 16
GLA_TAU = 16.0
GLA_CHUNK = 64
GLA_SUB = 16
GLA_STEP_CHUNKS = 4
GLA_FACTOR_MAX_DECAY = 60.0
LN_EPS = 1e-5
NEG = -1e30
FORCE_BONUS = 1e4

NSA_W = NSA_HEADS * HEAD_DIM
NSA_KV_W = NSA_KV_GROUPS * HEAD_DIM
MOBA_W = MOBA_HEADS * HEAD_DIM
EVEN_SPLITS = (NSA_W, 6 * NSA_KV_W, 3 * NSA_HEADS, NSA_W, MOBA_W, MOBA_W, MOBA_W, MOBA_W)
ODD_SPLITS = (GLA_HEADS * GLA_DK, GLA_HEADS * GLA_DK, GLA_HEADS * GLA_DV, GLA_GATE_RANK, GLA_HEADS * GLA_DV)

LANES = 128
SUBLANES = 8
FEAT = 64
GATE_ROWS = 32
MOBA_NBP = 16
TQ = 256
TQM = 512
TK = 512
CG = 256
MIN_WEIGHT_SUM = 1e-25
VROWS = HEAD_DIM + 16
VMEM_LIMIT = 48 * 1024 * 1024


def _dot(a, b):
    return jnp.dot(a, b, preferred_element_type=F32)


def _dot_nt(a, b):
    return lax.dot_general(a, b, (((1,), (1,)), ((), ())), preferred_element_type=F32)


def _dot_tn(a, b):
    return lax.dot_general(a, b, (((0,), (0,)), ((), ())), preferred_element_type=F32)


def _hilo(a):
    hi = a.astype(BF16)
    lo = (a - hi.astype(F32)).astype(BF16)
    return hi, lo


def _pow2_neg(n):
    return lax.bitcast_convert_type(lax.shift_left(127 - n, 23), F32)


def _sigmoid(x):
    return 1.0 / (1.0 + jnp.exp(-x))


def _layer_norm(r, g, b):
    mu = jnp.mean(r, axis=-1, keepdims=True)
    d = r - mu
    var = jnp.mean(d * d, axis=-1, keepdims=True)
    return d * lax.rsqrt(var + LN_EPS) * g + b


def _ranks(sc_ref, n_real):
    n_rows, width = sc_ref.shape
    out = []
    for v in range(n_rows // SUBLANES):
        lo = v * SUBLANES
        sc = sc_ref[lo:lo + SUBLANES, :]
        idx = lo + lax.broadcasted_iota(jnp.int32, (SUBLANES, width), 0)
        rank = jnp.zeros((SUBLANES, width), jnp.int32)
        for jp in range(n_real):
            r = sc_ref[pl.ds(jp, 1), :]
            if jp < lo:
                beats = r >= sc
            elif jp >= lo + SUBLANES - 1:
                beats = r > sc
            else:
                beats = (r > sc) | ((r == sc) & (idx > jp))
            rank = rank + jnp.where(beats, 1, 0)
        out.append(rank)
    return jnp.concatenate(out, axis=0)


def _proj_kernel(*refs, tok_splits, tr_splits):
    x_ref, wt_ref = refs[0], refs[1]
    outs = list(refs[3:] if tr_splits else refs[2:])
    x = x_ref[...]
    xb = x.astype(BF16)
    off = 0
    for n in tok_splits:
        o_ref = outs.pop(0)
        o_ref[...] = _dot(xb, wt_ref[:, off:off + n]).astype(o_ref.dtype)
        off += n
    if tr_splits:
        wtr_ref = refs[2]
        xt = x.T.astype(BF16)
        off = 0
        for n in tr_splits:
            o_ref = outs.pop(0)
            o_ref[...] = _dot(wtr_ref[off:off + n, :], xt).astype(o_ref.dtype)
            off += n


def _proj(x2d, w_tok, tok_splits, tok_dtypes, w_tr=None, tr_splits=(), tr_dtypes=(), tm=512):
    T, D = x2d.shape
    assert T % tm == 0 and sum(tok_splits) == w_tok.shape[1]
    in_specs = [pl.BlockSpec((tm, D), lambda i: (i, 0)),
                pl.BlockSpec(w_tok.shape, lambda i: (0, 0))]
    args = [x2d, w_tok]
    if tr_splits:
        assert sum(tr_splits) == w_tr.shape[0]
        in_specs.append(pl.BlockSpec(w_tr.shape, lambda i: (0, 0)))
        args.append(w_tr)
    out_specs = ([pl.BlockSpec((tm, n), lambda i: (i, 0)) for n in tok_splits]
                 + [pl.BlockSpec((n, tm), lambda i: (0, i)) for n in tr_splits])
    out_shape = ([jax.ShapeDtypeStruct((T, n), dt) for n, dt in zip(tok_splits, tok_dtypes)]
                 + [jax.ShapeDtypeStruct((n, T), dt) for n, dt in zip(tr_splits, tr_dtypes)])
    return pl.pallas_call(
        functools.partial(_proj_kernel, tok_splits=tuple(tok_splits), tr_splits=tuple(tr_splits)),
        grid=(T // tm,), in_specs=in_specs, out_specs=out_specs, out_shape=out_shape,
        compiler_params=pltpu.CompilerParams(dimension_semantics=("parallel",),
                                             vmem_limit_bytes=VMEM_LIMIT),
        name="proj",
    )(*args)


def _cmp_mlp_kernel(kx_ref, vx_ref, w1_ref, pos_ref, b1_ref, w2_ref, cf_ref, o_ref, *, ncmp):
    for i, x_ref in enumerate((kx_ref, vx_ref)):
        x = x_ref[...]
        nc, cw = x.shape
        a = _dot(x, w1_ref[i, 0])
        bm = _dot(x, w1_ref[i, 1])
        c = b1_ref[i]
        for half in range(2):
            phi, plo = _hilo(jnp.broadcast_to(pos_ref[i, half], (8, cw)))
            c = c + (_dot(phi, w1_ref[i, half]) + _dot(plo, w1_ref[i, half]))[0:1]
        hid = a + pltpu.roll(bm, nc - 1, axis=0) + c
        hid = hid * _sigmoid(hid)
        row = lax.broadcasted_iota(jnp.int32, hid.shape, 0)
        hid = jnp.where(row < ncmp, hid, 0.0)
        out = _dot(hid.astype(BF16), w2_ref[i])
        if i == 0:
            out = out + cf_ref[...]
        o_ref[i] = out.astype(o_ref.dtype)


def _cmp_mlp(kx, vx, w1p, posp, b1p, w2p, cfeat, ncmp):
    B, NC, CW = kx.shape
    OW = w2p.shape[-1]
    full = lambda a: pl.BlockSpec(a.shape, lambda b: (0,) * a.ndim)
    return pl.pallas_call(
        functools.partial(_cmp_mlp_kernel, ncmp=ncmp),
        grid=(B,),
        in_specs=[pl.BlockSpec((None, NC, CW), lambda b: (b, 0, 0)),
                  pl.BlockSpec((None, NC, CW), lambda b: (b, 0, 0)),
                  full(w1p), full(posp), full(b1p), full(w2p), full(cfeat)],
        out_specs=pl.BlockSpec((2, None, NC, OW), lambda b: (0, b, 0, 0)),
        out_shape=jax.ShapeDtypeStruct((2, B, NC, OW), BF16),
        compiler_params=pltpu.CompilerParams(dimension_semantics=("parallel",),
                                             vmem_limit_bytes=VMEM_LIMIT),
        name="cmp_mlp",
    )(kx, vx, w1p, posp, b1p, w2p, cfeat)


def _cmp_attn_kernel(q_ref, kc_ref, vc_ref, ovt_ref, o_ref, bias_ref, sc_ref, *, tq, nh, ns, ntop):
    g = pl.program_id(1)
    qi = pl.program_id(2)
    R = nh * tq
    nc = kc_ref.shape[0]
    qcat = jnp.concatenate([q_ref[hh * HEAD_DIM:(hh + 1) * HEAD_DIM, :] for hh in range(nh)], axis=1)
    r = lax.broadcasted_iota(jnp.int32, (FEAT, R), 0)
    col = lax.broadcasted_iota(jnp.int32, (FEAT, R), 1)
    t = qi * tq + col % tq
    slope = _pow2_neg(g * nh + col // tq + 1)
    feats = jnp.where(r == 0, -FEAT * slope * (t // FEAT).astype(F32),
                      jnp.where(r == 1, -slope * (t % FEAT).astype(F32), jnp.where(r < 4, slope, 0.0)))
    qa = jnp.concatenate([qcat, feats.astype(BF16)], axis=0)
    s = _dot(kc_ref[...], qa)
    n = lax.broadcasted_iota(jnp.int32, (nc, R), 0)
    t_row = qi * tq + lax.broadcasted_iota(jnp.int32, (1, R), 1) % tq
    last = (t_row - (NSA_CMP_BLOCK - 1)) // NSA_CMP_STRIDE
    s = jnp.where(n <= last, s, NEG)
    e = jnp.exp(s - jnp.max(s, axis=0, keepdims=True))
    inv = jnp.where(last >= 0, 1.0 / jnp.sum(e, axis=0, keepdims=True), 0.0)
    p = e * inv
    o_ref[...] = _dot_tn(vc_ref[:, 0:HEAD_DIM], p.astype(BF16)).astype(o_ref.dtype)

    psum = p[:, 0:tq]
    for hh in range(1, nh):
        psum = psum + p[:, hh * tq:(hh + 1) * tq]
    phi, plo = _hilo(psum)
    imp = _dot(ovt_ref[...], phi) + _dot(ovt_ref[...], plo)

    j = lax.broadcasted_iota(jnp.int32, (FEAT, tq), 0)
    cur = (qi * tq + lax.broadcasted_iota(jnp.int32, (FEAT, tq), 1)) // NSA_SEL_BLOCK
    forced = jnp.where((j == 0) | (j == cur) | (j == cur - 1), FORCE_BONUS, 0.0)
    sc = jnp.where(j <= cur, imp + forced, NEG)
    sc_ref[...] = sc
    sel = (_ranks(sc_ref, ns) < ntop) & (j <= cur)
    bias_ref[...] = jnp.where(sel, 0.0, NEG).astype(bias_ref.dtype)


def _cmp_attn(nq_t, kvc, ovt, *, B, S):
    tq, nh, G = TQ, NSA_HG, NSA_KV_GROUPS
    R = nh * tq
    nq = S // tq
    NC = kvc.shape[2]
    ns = S // NSA_SEL_BLOCK
    assert ns <= FEAT
    ntop = min(NSA_TOP_N, ns)
    return pl.pallas_call(
        functools.partial(_cmp_attn_kernel, tq=tq, nh=nh, ns=ns, ntop=ntop),
        grid=(B, G, nq),
        in_specs=[pl.BlockSpec((nh * HEAD_DIM, tq), lambda b, g, i: (g, b * nq + i)),
                  pl.BlockSpec((None, None, NC, LANES), lambda b, g, i: (0, b, 0, g)),
                  pl.BlockSpec((None, None, NC, LANES), lambda b, g, i: (1, b, 0, g)),
                  pl.BlockSpec(ovt.shape, lambda b, g, i: (0, 0))],
        out_specs=[pl.BlockSpec((None, None, HEAD_DIM, R), lambda b, g, i: (b, g, 0, i)),
                   pl.BlockSpec((None, None, FEAT, tq), lambda b, g, i: (b, g, 0, i))],
        out_shape=[jax.ShapeDtypeStruct((B, G, HEAD_DIM, S * nh), BF16),
                   jax.ShapeDtypeStruct((B, G, FEAT, S), BF16)],
        scratch_shapes=[pltpu.VMEM((FEAT, tq), F32)],
        compiler_params=pltpu.CompilerParams(dimension_semantics=("parallel", "parallel", "parallel"),
                                             vmem_limit_bytes=VMEM_LIMIT),
        name="cmp_attn",
    )(nq_t, kvc, kvc, ovt)


def _moba_gate_kernel(q_ref, k_ref, a_ref, bias_ref, km_ref, sc_ref, *, tq, nb, ntop):
    h = pl.program_id(1)
    qi = pl.program_id(2)

    @pl.when(qi == 0)
    def _means():
        km_ref[...] = _dot(a_ref[...], k_ref[...])

    nbp = km_ref.shape[0]
    lane = lax.broadcasted_iota(jnp.int32, (nbp, LANES), 1)
    km = jnp.where(lane // HEAD_DIM == h % 2, km_ref[...], 0.0)
    khi, klo = _hilo(km)
    q = q_ref[...]
    qq = jnp.concatenate([q, q], axis=0)
    g_t = _dot(khi, qq) + _dot(klo, qq)
    n = lax.broadcasted_iota(jnp.int32, (nbp, tq), 0)
    cur = (qi * tq + lax.broadcasted_iota(jnp.int32, (nbp, tq), 1)) // MOBA_BLOCK
    sc = jnp.where(n < cur, g_t, NEG)
    sc_ref[...] = sc
    sel = ((_ranks(sc_ref, nb) < ntop) & (n < cur)) | (n == cur)
    bias_ref[0:nbp, :] = jnp.where(sel, 0.0, NEG).astype(bias_ref.dtype)
    bias_ref[nbp:FEAT, :] = jnp.zeros((FEAT - nbp, tq), bias_ref.dtype)


def _moba_gate(mq_t, mk, avg, *, B, S):
    tq, H = S, MOBA_HEADS
    nq = S // tq
    nb = S // MOBA_BLOCK
    nbp = avg.shape[0]
    assert nb <= nbp
    ntop = min(MOBA_TOP_K, nb)
    return pl.pallas_call(
        functools.partial(_moba_gate_kernel, tq=tq, nb=nb, ntop=ntop),
        grid=(B, H, nq),
        in_specs=[pl.BlockSpec((HEAD_DIM, tq), lambda b, h, i: (h, b * nq + i)),
                  pl.BlockSpec((S, LANES), lambda b, h, i: (b, h // 2)),
                  pl.BlockSpec(avg.shape, lambda b, h, i: (0, 0))],
        out_specs=pl.BlockSpec((None, None, FEAT, tq), lambda b, h, i: (b, h, 0, i)),
        out_shape=jax.ShapeDtypeStruct((B, H, FEAT, S), BF16),
        scratch_shapes=[pltpu.VMEM((nbp, LANES), F32), pltpu.VMEM((nbp, tq), F32)],
        compiler_params=pltpu.CompilerParams(dimension_semantics=("parallel", "parallel", "arbitrary"),
                                             vmem_limit_bytes=VMEM_LIMIT),
        name="moba_gate",
    )(mq_t, mk, avg)


def _flash_kernel(q_ref, bias_ref, k_ref, kf_ref, vt_ref, mask_ref, o_ref, ka_ref, vta_ref, kn_ref, qa_ref,
                  bnd_ref, m_ref, acc_ref, alpha_ref, p_ref, *, tq, tk, nh, pair, window):
    u = pl.program_id(1)
    qi = pl.program_id(2)
    R = nh * tq
    nv = nh if pair else 1
    q0 = qi * tq
    S = k_ref.shape[0]

    @pl.when(qi == 0)
    def _stage_keys_values():
        k = k_ref[...]
        ka_ref[:, 0:LANES] = k
        ka_ref[:, LANES:2 * LANES] = kf_ref[...]
        k2 = jnp.square(k.astype(F32))
        lane = lax.broadcasted_iota(jnp.int32, k2.shape, 1)
        for hf in range(2):
            n2 = jnp.sum(jnp.where(lane // HEAD_DIM == hf, k2, 0.0), axis=1, keepdims=True)
            kn_ref[hf:hf + 1, :] = jnp.broadcast_to(jnp.sqrt(jnp.max(n2, axis=0, keepdims=True)), (1, LANES))
        ones_row = jnp.where(lax.broadcasted_iota(jnp.int32, (VROWS - HEAD_DIM, tk), 0) == 0, 1.0, 0.0)
        for v in range(nv):
            for jt in range(S // tk):
                vta_ref[v, jt, 0:HEAD_DIM, :] = vt_ref[v * HEAD_DIM:(v + 1) * HEAD_DIM, jt * tk:(jt + 1) * tk]
                vta_ref[v, jt, HEAD_DIM:VROWS, :] = ones_row.astype(BF16)

    qcat = jnp.concatenate([q_ref[hh * HEAD_DIM:(hh + 1) * HEAD_DIM, :] for hh in range(nh)], axis=1)
    qf = qcat.astype(F32)
    qq = jnp.concatenate([qf, qf], axis=0)
    rowi = lax.broadcasted_iota(jnp.int32, (2 * HEAD_DIM, R), 0)
    coli = lax.broadcasted_iota(jnp.int32, (2 * HEAD_DIM, R), 1)
    half = coli // tq if pair else u % 2
    qa_ref[0:2 * HEAD_DIM, :] = jnp.where(rowi // HEAD_DIM == half, qq, 0.0).astype(BF16)
    qa_ref[2 * HEAD_DIM:2 * HEAD_DIM + FEAT, :] = jnp.concatenate(
        [bias_ref[hh if pair else 0] for hh in range(nh)], axis=1)
    r = lax.broadcasted_iota(jnp.int32, (FEAT, R), 0)
    col = lax.broadcasted_iota(jnp.int32, (FEAT, R), 1)
    t = q0 + col % tq
    slope = _pow2_neg(u * nh + col // tq + 1)
    tb = (t // FEAT).astype(F32)
    tr = (t % FEAT).astype(F32)
    feats = jnp.where(r == 0, -FEAT * slope * tb,
                      jnp.where(r == 1, -slope * tr, jnp.where(r < 4, slope, 0.0)))
    qa_ref[2 * HEAD_DIM + FEAT:2 * LANES, :] = feats.astype(BF16)
    kn = [jnp.concatenate([kn_ref[hf:hf + 1, :]] * (R // LANES), axis=1) for hf in range(2)]
    half_row = lax.broadcasted_iota(jnp.int32, (1, R), 1) // tq if pair else u % 2
    qn = jnp.sqrt(jnp.sum(qf * qf, axis=0, keepdims=True))
    bnd_ref[...] = jnp.broadcast_to(qn * jnp.where(half_row == 0, kn[0], kn[1]), bnd_ref.shape)

    ncg = R // CG
    kd = (q0 + tq - 1) // tk

    def sweep(running_max):
        acc_ref[...] = jnp.zeros(acc_ref.shape, F32)
        if running_max:
            m_ref[...] = jnp.full(m_ref.shape, NEG, F32)

        def values(ki):
            for c in range(ncg):
                cols = pl.ds(c * CG, CG)
                v = (c * CG) // tq if pair else 0
                pv = _dot(vta_ref[v, ki], p_ref[:, cols])
                if running_max:
                    acc_ref[:, cols] = alpha_ref[0:1, cols] * acc_ref[:, cols] + pv
                else:
                    acc_ref[:, cols] = acc_ref[:, cols] + pv

        def scores(ki, masked, prev=None):
            keys = pl.ds(pl.multiple_of(ki * tk, tk), tk)
            ka = ka_ref[keys, :]
            s_groups = [_dot(ka, qa_ref[:, pl.ds(c * CG, CG)]) for c in range(ncg)]
            if prev is not None:
                values(prev)
            for c, s in enumerate(s_groups):
                cols = pl.ds(c * CG, CG)
                if masked:
                    s = s + mask_ref[(q0 + (c * CG) % tq - ki * tk) // CG]
                if running_max:
                    m_prev = m_ref[0:1, cols]
                    m_new = jnp.maximum(m_prev, jnp.max(s, axis=0, keepdims=True))
                    alpha_ref[:, cols] = jnp.broadcast_to(jnp.exp(m_prev - m_new), (alpha_ref.shape[0], CG))
                    m_ref[:, cols] = jnp.broadcast_to(m_new, (m_ref.shape[0], CG))
                    p_ref[:, cols] = jnp.exp(s - m_new).astype(BF16)
                else:
                    p_ref[:, cols] = jnp.exp(s - bnd_ref[0:1, cols]).astype(BF16)

        def steady(lo, hi, masked):
            def body(ki, carry):
                scores(ki, masked, prev=ki - 1)
                return carry
            lax.fori_loop(lo, hi, body, 0)

        if window is None:
            @pl.when(kd == 0)
            def _only_diagonal():
                scores(0, True)

            @pl.when(kd > 0)
            def _past_then_diagonal():
                scores(0, False)
                steady(1, kd, False)
                scores(kd, True, prev=kd - 1)
        else:
            k_lo = jnp.maximum(q0 - window + 1, 0) // tk
            scores(k_lo, True)
            steady(k_lo + 1, kd + 1, True)
        values(kd)

    def emit():
        acc = acc_ref[...]
        out = (acc[0:HEAD_DIM] / acc[HEAD_DIM:HEAD_DIM + 1]).astype(o_ref.dtype)
        if pair:
            for hh in range(nh):
                o_ref[hh] = out[:, hh * tq:(hh + 1) * tq]
        else:
            o_ref[0] = out

    sweep(False)
    trusted = jnp.min(acc_ref[HEAD_DIM:HEAD_DIM + 1, :]) >= MIN_WEIGHT_SUM

    @pl.when(trusted)
    def _done():
        emit()

    @pl.when(jnp.logical_not(trusted))
    def _redo_with_running_max():
        sweep(True)
        emit()


def _mask_table(tk, window):
    n_delta = (tk + (window or 0)) // CG
    d = np.arange(n_delta)[:, None, None] * CG + np.arange(CG)[None, None, :] - np.arange(tk)[None, :, None]
    ok = (d >= 0) & ((d < window) if window else True)
    return jnp.asarray(np.where(ok, 0.0, NEG), F32)


def _flash(q_t, bias_t, k, kfeat, v_t, *, B, S, units, tq, nh, pair, window):
    tk = TK
    R = nh * tq
    nq = S // tq
    nv = nh if pair else 1
    nb = nh if pair else 1
    assert tk % tq == 0 and S % tk == 0 and tq % CG == 0 and tk % CG == 0 and (window or 0) % CG == 0
    mask = _mask_table(tk, window)
    kcol = (lambda u: u) if pair else (lambda u: u // 2)
    out_w = tq if pair else R
    out_shape = (B, units * nb, HEAD_DIM, S if pair else S * nh)
    return pl.pallas_call(
        functools.partial(_flash_kernel, tq=tq, tk=tk, nh=nh, pair=pair, window=window),
        grid=(B, units, nq),
        in_specs=[pl.BlockSpec((nh * HEAD_DIM, tq), lambda b, u, i: (u, b * nq + i)),
                  pl.BlockSpec((None, nb, FEAT, tq), lambda b, u, i: (b, u, 0, i)),
                  pl.BlockSpec((S, LANES), lambda b, u, i: (b, kcol(u))),
                  pl.BlockSpec((S, LANES), lambda b, u, i: (0, 0)),
                  pl.BlockSpec((nv * HEAD_DIM, S), lambda b, u, i: (u, b)),
                  pl.BlockSpec(mask.shape, lambda b, u, i: (0, 0, 0))],
        out_specs=pl.BlockSpec((None, nb, HEAD_DIM, out_w), lambda b, u, i: (b, u, 0, i)),
        out_shape=jax.ShapeDtypeStruct(out_shape, BF16),
        scratch_shapes=[pltpu.VMEM((S, 2 * LANES), BF16),
                        pltpu.VMEM((nv, S // tk, VROWS, tk), BF16),
                        pltpu.VMEM((8, LANES), F32),
                        pltpu.VMEM((2 * LANES, R), BF16),
                        pltpu.VMEM((8, R), F32),
                        pltpu.VMEM((8, R), F32),
                        pltpu.VMEM((VROWS, R), F32),
                        pltpu.VMEM((8, R), F32),
                        pltpu.VMEM((tk, R), BF16)],
        compiler_params=pltpu.CompilerParams(dimension_semantics=("parallel", "parallel", "arbitrary"),
                                             vmem_limit_bytes=VMEM_LIMIT),
        name="flash",
    )(q_t, bias_t, k, kfeat, v_t, mask)


def _even_out_kernel(x_ref, ocmp_ref, osel_ref, owin_ref, om_ref, gate_ref, nz_ref, mz_ref,
                     e_ref, w_ref, g_ref, b_ref, o_ref, *, alpha, tm):
    G, nh = NSA_KV_GROUPS, NSA_HG
    sig = _sigmoid(gate_ref[...])
    shi, slo = _hilo(sig)

    def heads_t(ref):
        return jnp.concatenate([ref[g, :, hh * tm:(hh + 1) * tm] for g in range(G) for hh in range(nh)],
                               axis=0).astype(F32)

    o_nsa = None
    for br, ref in enumerate((ocmp_ref, osel_ref, owin_ref)):
        gb = _dot(e_ref[br], shi) + _dot(e_ref[br], slo)
        term = gb * heads_t(ref)
        o_nsa = term if o_nsa is None else o_nsa + term
    nz = nz_ref[...].astype(F32)
    mz = mz_ref[...].astype(F32)
    o_m = jnp.concatenate([om_ref[h] for h in range(MOBA_HEADS)], axis=0).astype(F32)
    a_t = jnp.concatenate([o_nsa * (nz * _sigmoid(nz)), o_m * (mz * _sigmoid(mz))], axis=0)
    y = _dot_tn(a_t.astype(BF16), w_ref[...])
    o_ref[...] = _layer_norm(alpha * x_ref[...] + y, g_ref[...], b_ref[...])


def _even_out(x2d, ocmp, osel, owin, om, gate_t, nz_t, mz_t, expand, w_out, ln_g, ln_b, alpha, *, B, S):
    T, D = x2d.shape
    tm = TQ
    nq = S // tm
    G, nh, H = NSA_KV_GROUPS, NSA_HG, MOBA_HEADS
    nsa = pl.BlockSpec((None, G, HEAD_DIM, nh * tm), lambda i: (i // nq, 0, 0, i % nq))
    feat = lambda n: pl.BlockSpec((n, tm), lambda i: (0, i))
    full = lambda a: pl.BlockSpec(a.shape, lambda i: (0,) * a.ndim)
    return pl.pallas_call(
        functools.partial(_even_out_kernel, alpha=alpha, tm=tm),
        grid=(T // tm,),
        in_specs=[pl.BlockSpec((tm, D), lambda i: (i, 0)), nsa, nsa, nsa,
                  pl.BlockSpec((None, H, HEAD_DIM, tm), lambda i: (i // nq, 0, 0, i % nq)),
                  feat(GATE_ROWS), feat(NSA_W), feat(MOBA_W),
                  full(expand), full(w_out), full(ln_g), full(ln_b)],
        out_specs=pl.BlockSpec((tm, D), lambda i: (i, 0)),
        out_shape=jax.ShapeDtypeStruct((T, D), F32),
        compiler_params=pltpu.CompilerParams(dimension_semantics=("parallel",),
                                             vmem_limit_bytes=VMEM_LIMIT),
        name="even_out",
    )(x2d, ocmp, osel, owin, om, gate_t, nz_t, mz_t, expand, w_out, ln_g, ln_b)


def _gla_head_exact(row0, h, q_ref, v_ref, o_ref, st_ref, b_ref, kf_ref, scale):
    C, dk, dv, SUB = GLA_CHUNK, GLA_DK, GLA_DV, GLA_SUB
    rows = lax.broadcasted_iota(jnp.int32, (C, dk), 0)
    sub_i = lax.broadcasted_iota(jnp.int32, (SUB, LANES), 0)
    lane = lax.broadcasted_iota(jnp.int32, (SUB, LANES), 1)
    ones = jnp.ones((dk, LANES), BF16)
    sl = pl.ds(h * dk, dk)
    rs_c = pl.ds(row0, C)
    bh = b_ref[:, sl]
    kf = kf_ref[:, sl]
    qf = q_ref[rs_c, sl].astype(F32) * scale
    vh = v_ref[rs_c, pl.ds(h * dv, dv)]
    st = st_ref[h]
    inter = _dot_nt((qf * jnp.exp(bh)).astype(BF16), st.astype(BF16))
    bl = b_ref[pl.ds(C - 1, 1), sl]
    kd = kf * jnp.exp(bl - bh)

    def off_block(r0, r1, c0, c1):
        ref_row = b_ref[pl.ds(r0, 1), sl]
        qt = qf[r0:r1] * jnp.exp(bh[r0:r1] - ref_row)
        ek = jnp.where((rows >= c0) & (rows < c1), ref_row - bh, NEG)
        kt = kf * jnp.exp(ek)
        return _dot_nt(qt.astype(BF16), kt.astype(BF16))

    lower = off_block(2 * SUB, C, 0, 2 * SUB)
    blocks = [None, off_block(SUB, 2 * SUB, 0, SUB), lower[0:SUB],
              lower[SUB:2 * SUB] + off_block(3 * SUB, C, 2 * SUB, 3 * SUB)]
    pieces = []
    for blk in range(C // SUB):
        r0 = blk * SUB
        q_i = qf[r0:r0 + SUB]
        b_i = bh[r0:r0 + SUB]
        for jj in range(SUB):
            b_j = b_ref[pl.ds(r0 + jj, 1), sl]
            k_j = kf_ref[pl.ds(r0 + jj, 1), sl]
            pieces.append((q_i * k_j * jnp.exp(jnp.minimum(b_i - b_j, 0.0))).astype(BF16))
    sums = _dot(jnp.concatenate(pieces, axis=0), ones)
    for blk in range(C // SUB):
        r0 = blk * SUB
        diag = jnp.zeros((SUB, LANES), F32)
        for jj in range(SUB):
            rs = sums[(r0 + jj) * SUB:(r0 + jj + 1) * SUB]
            diag = jnp.where((lane == r0 + jj) & (sub_i >= jj), rs, diag)
        diag = diag[:, 0:C]
        blocks[blk] = diag if blocks[blk] is None else blocks[blk] + diag
    att = jnp.concatenate(blocks, axis=0)
    o_ref[rs_c, pl.ds(h * dv, dv)] = (inter + _dot(att.astype(BF16), vh)).astype(o_ref.dtype)
    st_ref[h] = st * jnp.exp(bl) + _dot_tn(vh, kd.astype(BF16))


def _gla_head_bounded(row0, h, q_ref, v_ref, o_ref, st_ref, b_ref, kf_ref, scale):
    C, dk, dv = GLA_CHUNK, GLA_DK, GLA_DV
    sl = pl.ds(h * dk, dk)
    rs_c = pl.ds(row0, C)
    bh = b_ref[rs_c, sl]
    kf = kf_ref[rs_c, sl]
    qf = q_ref[rs_c, sl].astype(F32) * scale
    vh = v_ref[rs_c, pl.ds(h * dv, dv)]
    st = st_ref[h]
    qs = (qf * jnp.exp(bh)).astype(BF16)
    kt = (kf * jnp.exp(-bh)).astype(BF16)
    res = _dot_nt(qs, jnp.concatenate([st.astype(BF16), kt], axis=0))
    i = lax.broadcasted_iota(jnp.int32, (C, C), 0)
    j = lax.broadcasted_iota(jnp.int32, (C, C), 1)
    att = jnp.where(i >= j, res[:, dv:dv + C], 0.0)
    o_ref[rs_c, pl.ds(h * dv, dv)] = (res[:, 0:dv] + _dot(att.astype(BF16), vh)).astype(o_ref.dtype)
    bl = b_ref[pl.ds(row0 + C - 1, 1), sl]
    kd = kf * jnp.exp(bl - bh)
    st_ref[h] = st * jnp.exp(bl) + _dot_tn(vh, kd.astype(BF16))


def _gla_kernel(q_ref, k_ref, v_ref, glr_ref, w2_ref, gb_ref, tril_ref, o_ref,
                st_ref, b_ref, kf_ref, bc_ref, kc_ref, *, scale):
    C, H = GLA_CHUNK, GLA_HEADS
    nch = q_ref.shape[0] // C

    @pl.when(pl.program_id(1) == 0)
    def _init():
        st_ref[...] = jnp.zeros(st_ref.shape, F32)

    ghi, glo = _hilo(glr_ref[...])
    whi, wlo = _hilo(w2_ref[...])
    pre = _dot(ghi, whi) + _dot(glo, whi) + _dot(ghi, wlo) + gb_ref[...]
    la = (jnp.minimum(pre, 0.0) - jnp.log1p(jnp.exp(-jnp.abs(pre)))) * (1.0 / GLA_TAU)
    lhi, llo = _hilo(la)
    b_all = _dot(tril_ref[...], lhi) + _dot(tril_ref[...], llo)
    b_ref[...] = b_all
    kf_ref[...] = k_ref[...].astype(F32)
    decay = -b_all[C - 1:C, :]
    for c in range(1, nch):
        decay = jnp.maximum(decay, -b_all[c * C + C - 1:(c + 1) * C, :])
    bounded = jnp.max(decay) <= GLA_FACTOR_MAX_DECAY

    @pl.when(bounded)
    def _factored():
        for c in range(nch):
            for h in range(H):
                _gla_head_bounded(c * C, h, q_ref, v_ref, o_ref, st_ref, b_ref, kf_ref, scale)

    @pl.when(jnp.logical_not(bounded))
    def _exact():
        def chunk(c, carry):
            row0 = pl.multiple_of(c * C, C)
            bc_ref[...] = b_ref[pl.ds(row0, C), :]
            kc_ref[...] = kf_ref[pl.ds(row0, C), :]
            for h in range(H):
                _gla_head_exact(row0, h, q_ref, v_ref, o_ref, st_ref, bc_ref, kc_ref, scale)
            return carry
        lax.fori_loop(0, nch, chunk, 0)


def _gla(q, k, v, glr, w2p, gb, tril, *, B, S):
    C, H, dk, dv = GLA_CHUNK, GLA_HEADS, GLA_DK, GLA_DV
    rows = tril.shape[0]
    nchunk = S // rows
    tok = lambda n: pl.BlockSpec((rows, n), lambda b, c: (b * nchunk + c, 0))
    const = lambda shape: pl.BlockSpec(shape, lambda b, c: (0,) * len(shape))
    return pl.pallas_call(
        functools.partial(_gla_kernel, scale=float(dk) ** -0.5),
        grid=(B, nchunk),
        in_specs=[tok(H * dk), tok(H * dk), tok(H * dv), tok(LANES),
                  const(w2p.shape), const(gb.shape), const(tril.shape)],
        out_specs=tok(H * dv),
        out_shape=jax.ShapeDtypeStruct((B * S, H * dv), BF16),
        scratch_shapes=[pltpu.VMEM((H, dv, dk), F32),
                        pltpu.VMEM((rows, H * dk), F32),
                        pltpu.VMEM((rows, H * dk), F32),
                        pltpu.VMEM((C, H * dk), F32),
                        pltpu.VMEM((C, H * dk), F32)],
        compiler_params=pltpu.CompilerParams(dimension_semantics=("parallel", "arbitrary"),
                                             vmem_limit_bytes=VMEM_LIMIT),
        name="gla",
    )(q, k, v, glr, w2p, gb, tril)


def _odd_out_kernel(x_ref, o_ref_in, z_ref, gn_ref, w_ref, g_ref, b_ref, out_ref, *, alpha):
    H, dv = GLA_HEADS, GLA_DV
    parts = []
    for h in range(H):
        sl = pl.ds(h * dv, dv)
        oh = o_ref_in[:, sl].astype(F32)
        r = lax.rsqrt(jnp.mean(oh * oh, axis=-1, keepdims=True) + LN_EPS)
        z = z_ref[:, sl].astype(F32)
        parts.append((oh * r * gn_ref[:, sl] * (z * _sigmoid(z))).astype(BF16))
    y = _dot(jnp.concatenate(parts, axis=1), w_ref[...])
    out_ref[...] = _layer_norm(alpha * x_ref[...] + y, g_ref[...], b_ref[...])


def _odd_out(x2d, o, z, gn_g, w_out, ln_g, ln_b, alpha, tm=256):
    T, D = x2d.shape
    W = o.shape[1]
    row = lambda n: pl.BlockSpec((tm, n), lambda i: (i, 0))
    return pl.pallas_call(
        functools.partial(_odd_out_kernel, alpha=alpha),
        grid=(T // tm,),
        in_specs=[row(D), row(W), row(W),
                  pl.BlockSpec((1, W), lambda i: (0, 0)),
                  pl.BlockSpec((W, D), lambda i: (0, 0)),
                  pl.BlockSpec((1, D), lambda i: (0, 0)),
                  pl.BlockSpec((1, D), lambda i: (0, 0))],
        out_specs=row(D),
        out_shape=jax.ShapeDtypeStruct((T, D), F32),
        compiler_params=pltpu.CompilerParams(dimension_semantics=("parallel",),
                                             vmem_limit_bytes=VMEM_LIMIT),
        name="odd_out",
    )(x2d, o, z, gn_g, w_out, ln_g, ln_b)


def _key_features(S, block):
    s = np.arange(S)
    f = np.zeros((S, LANES), np.float32)
    f[s, s // block] = 1.0
    f[:, FEAT + 0] = 1.0
    f[:, FEAT + 1] = 1.0
    f[:, FEAT + 2] = FEAT * (s // FEAT)
    f[:, FEAT + 3] = s % FEAT
    return jnp.asarray(f, BF16)


def _pad_to(a, n, axis):
    pad = [(0, 0)] * a.ndim
    pad[axis] = (0, n - a.shape[axis])
    return jnp.pad(a, pad)


def _even_layer(x2d, B, S, w_in, cmp_pos, cmp_w1, cmp_b1, cmp_w2, w_out, ln_g, ln_b, alpha):
    T, D = x2d.shape
    G, hd = NSA_KV_GROUPS, HEAD_DIM
    scale = hd ** -0.5
    wq, wkv, wgate, wnz, wmq, wmk, wmv, wmz = jnp.split(w_in, np.cumsum(EVEN_SPLITS)[:-1].tolist(), axis=1)
    wkc, wvc, wks, wvs, wkw, wvw = jnp.split(wkv, 6, axis=1)
    w_tok = jnp.concatenate([wkc, wvc, wks, wkw, wmk], axis=1).astype(BF16)
    w_tr = jnp.concatenate([wq * scale, wvs, wvw, wmq * scale, wmv, wnz, wmz,
                            _pad_to(wgate, GATE_ROWS, 1)], axis=1).T.astype(BF16)
    (kcmp, vcmp, ksel, kwin, mk, nq_t, vsel_t, vwin_t, mq_t, mv_t, nz_t, mz_t, gate_t) = _proj(
        x2d, w_tok, (NSA_KV_W,) * 4 + (MOBA_W,), (BF16,) * 5,
        w_tr, (NSA_W, NSA_KV_W, NSA_KV_W, MOBA_W, MOBA_W, NSA_W, MOBA_W, GATE_ROWS),
        (BF16,) * 7 + (F32,))

    L, d, HID = NSA_CMP_BLOCK, NSA_CMP_STRIDE, NSA_CMP_HIDDEN
    ncmp = (S - L) // d + 1
    NC = S // d
    eye = jnp.eye(G, dtype=F32)
    w1r = cmp_w1.reshape(2, 2, d, hd, HID)
    w1p = jnp.einsum('ialdj,gh->ialgdhj', w1r, eye).reshape(2, 2, d * G * hd, G * HID).astype(BF16)
    posp = jnp.broadcast_to(cmp_pos.reshape(2, 2, d, 1, hd), (2, 2, d, G, hd)).reshape(2, 2, 1, d * G * hd)
    b1p = jnp.tile(cmp_b1, (1, G)).reshape(2, 1, G * HID)
    w2p = jnp.einsum('ijd,gh->igjhd', _pad_to(cmp_w2, LANES, 2), eye).reshape(2, G * HID, G * LANES).astype(BF16)
    cfeat = np.zeros((NC, G, LANES), np.float32)
    cfeat[:, :, hd + 0] = 1.0
    cfeat[:, :, hd + 1] = 1.0
    cfeat[:, :, hd + 2] = (np.arange(NC) * d)[:, None]
    cfeat[:, :, hd + 3] = L - 1
    kvc = _cmp_mlp(kcmp.reshape(B, NC, d * NSA_KV_W), vcmp.reshape(B, NC, d * NSA_KV_W),
                   w1p, posp, b1p, w2p, jnp.asarray(cfeat.reshape(NC, G * LANES)), ncmp)

    ns = S // NSA_SEL_BLOCK
    cmp_start = np.arange(NC) * d
    sel_start = np.arange(FEAT) * NSA_SEL_BLOCK
    ovt = ((cmp_start[None, :] <= sel_start[:, None] + NSA_SEL_BLOCK - 1)
           & (cmp_start[None, :] + L - 1 >= sel_start[:, None])
           & (np.arange(NC)[None, :] < ncmp) & (np.arange(FEAT)[:, None] < ns))
    o_cmp, bias_sel = _cmp_attn(nq_t, kvc, jnp.asarray(ovt, BF16), B=B, S=S)

    feats_sel = _key_features(S, NSA_SEL_BLOCK)
    o_sel = _flash(nq_t, bias_sel, ksel, feats_sel, vsel_t,
                   B=B, S=S, units=G, tq=TQ, nh=NSA_HG, pair=False, window=None)
    o_win = _flash(nq_t, jnp.zeros_like(bias_sel), kwin, feats_sel, vwin_t,
                   B=B, S=S, units=G, tq=TQ, nh=NSA_HG, pair=False, window=NSA_WINDOW)

    avg = np.zeros((MOBA_NBP, S), np.float32)
    avg[np.arange(S) // MOBA_BLOCK, np.arange(S)] = 1.0 / MOBA_BLOCK
    bias_m = _moba_gate(mq_t, mk, jnp.asarray(avg, BF16), B=B, S=S)
    o_m = _flash(mq_t, bias_m, mk, _key_features(S, MOBA_BLOCK), mv_t,
                 B=B, S=S, units=MOBA_HEADS // 2, tq=TQM, nh=2, pair=True, window=None)

    n = np.arange(NSA_W)[:, None]
    c = np.arange(GATE_ROWS)[None, :]
    expand = jnp.asarray(np.stack([(c == 3 * (n // hd) + br) for br in range(3)]), BF16)
    return _even_out(x2d, o_cmp, o_sel, o_win, o_m, gate_t, nz_t, mz_t, expand, w_out.astype(BF16),
                     ln_g.reshape(1, D), ln_b.reshape(1, D), alpha, B=B, S=S)


def _odd_layer(x2d, B, S, w_in, gate_w2, gate_b, gn_g, w_out, ln_g, ln_b, alpha):
    T, D = x2d.shape
    H, dk, dv, C = GLA_HEADS, GLA_DK, GLA_DV, GLA_CHUNK
    wq, wk, wv, wg, wz = jnp.split(w_in, np.cumsum(ODD_SPLITS)[:-1].tolist(), axis=1)
    w_perm = jnp.concatenate([wq, wk, wv, wz, _pad_to(wg, LANES, 1)], axis=1).astype(BF16)
    q, k, v, z, glr = _proj(x2d, w_perm, (H * dk, H * dk, H * dv, H * dv, LANES),
                            (BF16, BF16, BF16, BF16, F32))
    w2p = _pad_to(gate_w2, LANES, 0)
    nch = min(GLA_STEP_CHUNKS, S // C)
    tril = jnp.asarray(np.kron(np.eye(nch), np.tril(np.ones((C, C)))), BF16)
    o = _gla(q, k, v, glr, w2p, gate_b.reshape(1, H * dk), tril, B=B, S=S)
    return _odd_out(x2d, o, z, gn_g.reshape(1, H * dv), w_out.astype(BF16),
                    ln_g.reshape(1, D), ln_b.reshape(1, D), alpha)


def kernel(x, ev_w_in, ev_cmp_pos, ev_cmp_w1, ev_cmp_b1, ev_cmp_w2, ev_w_out, ev_ln_g, ev_ln_b,
           od_w_in, od_gate_w2, od_gate_b, od_gn_g, od_w_out, od_ln_g, od_ln_b):
    B, S, D = x.shape
    depth = ev_w_in.shape[0] + od_w_in.shape[0]
    alpha = (2.0 * depth) ** 0.25
    h = x.reshape(B * S, D)
    for layer in range(depth):
        i = layer // 2
        if layer % 2 == 0:
            h = _even_layer(h, B, S, ev_w_in[i], ev_cmp_pos[i], ev_cmp_w1[i], ev_cmp_b1[i], ev_cmp_w2[i],
                            ev_w_out[i], ev_ln_g[i], ev_ln_b[i], alpha)
        else:
            h = _odd_layer(h, B, S, od_w_in[i], od_gate_w2[i], od_gate_b[i], od_gn_g[i],
                           od_w_out[i], od_ln_g[i], od_ln_b[i], alpha)
    return h.reshape(B, S, D)
```

```python
import functools

import numpy as np
import jax
import jax.numpy as jnp
from jax import lax
from jax.experimental import pallas as pl
from jax.experimental.pallas import tpu as pltpu

F32 = jnp.float32
BF16 = jnp.bfloat16

HEAD_DIM = 64
NSA_HEADS = 8
NSA_KV_GROUPS = 2
NSA_HG = NSA_HEADS // NSA_KV_GROUPS
NSA_CMP_BLOCK = 32
NSA_CMP_STRIDE = 16
NSA_CMP_HIDDEN = 128
NSA_SEL_BLOCK = 64
NSA_TOP_N = 16
NSA_WINDOW = 512
MOBA_HEADS = 8
MOBA_BLOCK = 256
MOBA_TOP_K = 3
GLA_HEADS = 4
GLA_DK = 128
GLA_DV = 256
GLA_GATE_RANK = 16
GLA_TAU = 16.0
GLA_CHUNK = 64
GLA_SUB = 16
GLA_STEP_CHUNKS = 4
GLA_FACTOR_MAX_DECAY = 60.0
LN_EPS = 1e-5
NEG = -1e30
FORCE_BONUS = 1e4

NSA_W = NSA_HEADS * HEAD_DIM
NSA_KV_W = NSA_KV_GROUPS * HEAD_DIM
MOBA_W = MOBA_HEADS * HEAD_DIM
EVEN_SPLITS = (NSA_W, 6 * NSA_KV_W, 3 * NSA_HEADS, NSA_W, MOBA_W, MOBA_W, MOBA_W, MOBA_W)
ODD_SPLITS = (GLA_HEADS * GLA_DK, GLA_HEADS * GLA_DK, GLA_HEADS * GLA_DV, GLA_GATE_RANK, GLA_HEADS * GLA_DV)

LANES = 128
SUBLANES = 8
FEAT = 64
GATE_ROWS = 32
MOBA_NBP = 16
TQ = 256
TQM = 512
TK = 512
CG = 256
MIN_WEIGHT_SUM = 1e-25
VROWS = HEAD_DIM + 16
VMEM_LIMIT = 48 * 1024 * 1024


def _dot(a, b):
    return jnp.dot(a, b, preferred_element_type=F32)


def _dot_nt(a, b):
    return lax.dot_general(a, b, (((1,), (1,)), ((), ())), preferred_element_type=F32)


def _dot_tn(a, b):
    return lax.dot_general(a, b, (((0,), (0,)), ((), ())), preferred_element_type=F32)


def _hilo(a):
    hi = a.astype(BF16)
    lo = (a - hi.astype(F32)).astype(BF16)
    return hi, lo


def _pow2_neg(n):
    return lax.bitcast_convert_type(lax.shift_left(127 - n, 23), F32)


def _sigmoid(x):
    return 1.0 / (1.0 + jnp.exp(-x))


def _layer_norm(r, g, b):
    mu = jnp.mean(r, axis=-1, keepdims=True)
    d = r - mu
    var = jnp.mean(d * d, axis=-1, keepdims=True)
    return d * lax.rsqrt(var + LN_EPS) * g + b


def _ranks(sc_ref, n_real):
    n_rows, width = sc_ref.shape
    out = []
    for v in range(n_rows // SUBLANES):
        lo = v * SUBLANES
        sc = sc_ref[lo:lo + SUBLANES, :]
        idx = lo + lax.broadcasted_iota(jnp.int32, (SUBLANES, width), 0)
        rank = jnp.zeros((SUBLANES, width), jnp.int32)
        for jp in range(n_real):
            r = sc_ref[pl.ds(jp, 1), :]
            if jp < lo:
                beats = r >= sc
            elif jp >= lo + SUBLANES - 1:
                beats = r > sc
            else:
                beats = (r > sc) | ((r == sc) & (idx > jp))
            rank = rank + jnp.where(beats, 1, 0)
        out.append(rank)
    return jnp.concatenate(out, axis=0)


def _proj_kernel(*refs, tok_splits, tr_splits):
    x_ref, wt_ref = refs[0], refs[1]
    outs = list(refs[3:] if tr_splits else refs[2:])
    x = x_ref[...]
    xb = x.astype(BF16)
    off = 0
    for n in tok_splits:
        o_ref = outs.pop(0)
        o_ref[...] = _dot(xb, wt_ref[:, off:off + n]).astype(o_ref.dtype)
        off += n
    if tr_splits:
        wtr_ref = refs[2]
        xt = x.T.astype(BF16)
        off = 0
        for n in tr_splits:
            o_ref = outs.pop(0)
            o_ref[...] = _dot(wtr_ref[off:off + n, :], xt).astype(o_ref.dtype)
            off += n


def _proj(x2d, w_tok, tok_splits, tok_dtypes, w_tr=None, tr_splits=(), tr_dtypes=(), tm=512):
    T, D = x2d.shape
    assert T % tm == 0 and sum(tok_splits) == w_tok.shape[1]
    in_specs = [pl.BlockSpec((tm, D), lambda i: (i, 0)),
                pl.BlockSpec(w_tok.shape, lambda i: (0, 0))]
    args = [x2d, w_tok]
    if tr_splits:
        assert sum(tr_splits) == w_tr.shape[0]
        in_specs.append(pl.BlockSpec(w_tr.shape, lambda i: (0, 0)))
        args.append(w_tr)
    out_specs = ([pl.BlockSpec((tm, n), lambda i: (i, 0)) for n in tok_splits]
                 + [pl.BlockSpec((n, tm), lambda i: (0, i)) for n in tr_splits])
    out_shape = ([jax.ShapeDtypeStruct((T, n), dt) for n, dt in zip(tok_splits, tok_dtypes)]
                 + [jax.ShapeDtypeStruct((n, T), dt) for n, dt in zip(tr_splits, tr_dtypes)])
    return pl.pallas_call(
        functools.partial(_proj_kernel, tok_splits=tuple(tok_splits), tr_splits=tuple(tr_splits)),
        grid=(T // tm,), in_specs=in_specs, out_specs=out_specs, out_shape=out_shape,
        compiler_params=pltpu.CompilerParams(dimension_semantics=("parallel",),
                                             vmem_limit_bytes=VMEM_LIMIT),
        name="proj",
    )(*args)


def _cmp_mlp_kernel(kx_ref, vx_ref, w1_ref, pos_ref, b1_ref, w2_ref, cf_ref, o_ref, *, ncmp):
    for i, x_ref in enumerate((kx_ref, vx_ref)):
        x = x_ref[...]
        nc, cw = x.shape
        a = _dot(x, w1_ref[i, 0])
        bm = _dot(x, w1_ref[i, 1])
        c = b1_ref[i]
        for half in range(2):
            phi, plo = _hilo(jnp.broadcast_to(pos_ref[i, half], (8, cw)))
            c = c + (_dot(phi, w1_ref[i, half]) + _dot(plo, w1_ref[i, half]))[0:1]
        hid = a + pltpu.roll(bm, nc - 1, axis=0) + c
        hid = hid * _sigmoid(hid)
        row = lax.broadcasted_iota(jnp.int32, hid.shape, 0)
        hid = jnp.where(row < ncmp, hid, 0.0)
        out = _dot(hid.astype(BF16), w2_ref[i])
        if i == 0:
            out = out + cf_ref[...]
        o_ref[i] = out.astype(o_ref.dtype)


def _cmp_mlp(kx, vx, w1p, posp, b1p, w2p, cfeat, ncmp):
    B, NC, CW = kx.shape
    OW = w2p.shape[-1]
    full = lambda a: pl.BlockSpec(a.shape, lambda b: (0,) * a.ndim)
    return pl.pallas_call(
        functools.partial(_cmp_mlp_kernel, ncmp=ncmp),
        grid=(B,),
        in_specs=[pl.BlockSpec((None, NC, CW), lambda b: (b, 0, 0)),
                  pl.BlockSpec((None, NC, CW), lambda b: (b, 0, 0)),
                  full(w1p), full(posp), full(b1p), full(w2p), full(cfeat)],
        out_specs=pl.BlockSpec((2, None, NC, OW), lambda b: (0, b, 0, 0)),
        out_shape=jax.ShapeDtypeStruct((2, B, NC, OW), BF16),
        compiler_params=pltpu.CompilerParams(dimension_semantics=("parallel",),
                                             vmem_limit_bytes=VMEM_LIMIT),
        name="cmp_mlp",
    )(kx, vx, w1p, posp, b1p, w2p, cfeat)


def _cmp_attn_kernel(q_ref, kc_ref, vc_ref, ovt_ref, o_ref, bias_ref, sc_ref, *, tq, nh, ns, ntop):
    g = pl.program_id(1)
    qi = pl.program_id(2)
    R = nh * tq
    nc = kc_ref.shape[0]
    qcat = jnp.concatenate([q_ref[hh * HEAD_DIM:(hh + 1) * HEAD_DIM, :] for hh in range(nh)], axis=1)
    r = lax.broadcasted_iota(jnp.int32, (FEAT, R), 0)
    col = lax.broadcasted_iota(jnp.int32, (FEAT, R), 1)
    t = qi * tq + col % tq
    slope = _pow2_neg(g * nh + col // tq + 1)
    feats = jnp.where(r == 0, -FEAT * slope * (t // FEAT).astype(F32),
                      jnp.where(r == 1, -slope * (t % FEAT).astype(F32), jnp.where(r < 4, slope, 0.0)))
    qa = jnp.concatenate([qcat, feats.astype(BF16)], axis=0)
    s = _dot(kc_ref[...], qa)
    n = lax.broadcasted_iota(jnp.int32, (nc, R), 0)
    t_row = qi * tq + lax.broadcasted_iota(jnp.int32, (1, R), 1) % tq
    last = (t_row - (NSA_CMP_BLOCK - 1)) // NSA_CMP_STRIDE
    s = jnp.where(n <= last, s, NEG)
    e = jnp.exp(s - jnp.max(s, axis=0, keepdims=True))
    inv = jnp.where(last >= 0, 1.0 / jnp.sum(e, axis=0, keepdims=True), 0.0)
    p = e * inv
    o_ref[...] = _dot_tn(vc_ref[:, 0:HEAD_DIM], p.astype(BF16)).astype(o_ref.dtype)

    psum = p[:, 0:tq]
    for hh in range(1, nh):
        psum = psum + p[:, hh * tq:(hh + 1) * tq]
    phi, plo = _hilo(psum)
    imp = _dot(ovt_ref[...], phi) + _dot(ovt_ref[...], plo)

    j = lax.broadcasted_iota(jnp.int32, (FEAT, tq), 0)
    cur = (qi * tq + lax.broadcasted_iota(jnp.int32, (FEAT, tq), 1)) // NSA_SEL_BLOCK
    forced = jnp.where((j == 0) | (j == cur) | (j == cur - 1), FORCE_BONUS, 0.0)
    sc = jnp.where(j <= cur, imp + forced, NEG)
    sc_ref[...] = sc
    sel = (_ranks(sc_ref, ns) < ntop) & (j <= cur)
    bias_ref[...] = jnp.where(sel, 0.0, NEG).astype(bias_ref.dtype)


def _cmp_attn(nq_t, kvc, ovt, *, B, S):
    tq, nh, G = TQ, NSA_HG, NSA_KV_GROUPS
    R = nh * tq
    nq = S // tq
    NC = kvc.shape[2]
    ns = S // NSA_SEL_BLOCK
    assert ns <= FEAT
    ntop = min(NSA_TOP_N, ns)
    return pl.pallas_call(
        functools.partial(_cmp_attn_kernel, tq=tq, nh=nh, ns=ns, ntop=ntop),
        grid=(B, G, nq),
        in_specs=[pl.BlockSpec((nh * HEAD_DIM, tq), lambda b, g, i: (g, b * nq + i)),
                  pl.BlockSpec((None, None, NC, LANES), lambda b, g, i: (0, b, 0, g)),
                  pl.BlockSpec((None, None, NC, LANES), lambda b, g, i: (1, b, 0, g)),
                  pl.BlockSpec(ovt.shape, lambda b, g, i: (0, 0))],
        out_specs=[pl.BlockSpec((None, None, HEAD_DIM, R), lambda b, g, i: (b, g, 0, i)),
                   pl.BlockSpec((None, None, FEAT, tq), lambda b, g, i: (b, g, 0, i))],
        out_shape=[jax.ShapeDtypeStruct((B, G, HEAD_DIM, S * nh), BF16),
                   jax.ShapeDtypeStruct((B, G, FEAT, S), BF16)],
        scratch_shapes=[pltpu.VMEM((FEAT, tq), F32)],
        compiler_params=pltpu.CompilerParams(dimension_semantics=("parallel", "parallel", "parallel"),
                                             vmem_limit_bytes=VMEM_LIMIT),
        name="cmp_attn",
    )(nq_t, kvc, kvc, ovt)


def _moba_gate_kernel(q_ref, k_ref, a_ref, bias_ref, km_ref, sc_ref, *, tq, nb, ntop):
    h = pl.program_id(1)
    qi = pl.program_id(2)

    @pl.when(qi == 0)
    def _means():
        km_ref[...] = _dot(a_ref[...], k_ref[...])

    nbp = km_ref.shape[0]
    lane = lax.broadcasted_iota(jnp.int32, (nbp, LANES), 1)
    km = jnp.where(lane // HEAD_DIM == h % 2, km_ref[...], 0.0)
    khi, klo = _hilo(km)
    q = q_ref[...]
    qq = jnp.concatenate([q, q], axis=0)
    g_t = _dot(khi, qq) + _dot(klo, qq)
    n = lax.broadcasted_iota(jnp.int32, (nbp, tq), 0)
    cur = (qi * tq + lax.broadcasted_iota(jnp.int32, (nbp, tq), 1)) // MOBA_BLOCK
    sc = jnp.where(n < cur, g_t, NEG)
    sc_ref[...] = sc
    sel = ((_ranks(sc_ref, nb) < ntop) & (n < cur)) | (n == cur)
    bias_ref[0:nbp, :] = jnp.where(sel, 0.0, NEG).astype(bias_ref.dtype)
    bias_ref[nbp:FEAT, :] = jnp.zeros((FEAT - nbp, tq), bias_ref.dtype)


def _moba_gate(mq_t, mk, avg, *, B, S):
    tq, H = S, MOBA_HEADS
    nq = S // tq
    nb = S // MOBA_BLOCK
    nbp = avg.shape[0]
    assert nb <= nbp
    ntop = min(MOBA_TOP_K, nb)
    return pl.pallas_call(
        functools.partial(_moba_gate_kernel, tq=tq, nb=nb, ntop=ntop),
        grid=(B, H, nq),
        in_specs=[pl.BlockSpec((HEAD_DIM, tq), lambda b, h, i: (h, b * nq + i)),
                  pl.BlockSpec((S, LANES), lambda b, h, i: (b, h // 2)),
                  pl.BlockSpec(avg.shape, lambda b, h, i: (0, 0))],
        out_specs=pl.BlockSpec((None, None, FEAT, tq), lambda b, h, i: (b, h, 0, i)),
        out_shape=jax.ShapeDtypeStruct((B, H, FEAT, S), BF16),
        scratch_shapes=[pltpu.VMEM((nbp, LANES), F32), pltpu.VMEM((nbp, tq), F32)],
        compiler_params=pltpu.CompilerParams(dimension_semantics=("parallel", "parallel", "arbitrary"),
                                             vmem_limit_bytes=VMEM_LIMIT),
        name="moba_gate",
    )(mq_t, mk, avg)


def _flash_kernel(q_ref, bias_ref, k_ref, kf_ref, vt_ref, mask_ref, o_ref, ka_ref, vta_ref, kn_ref, qa_ref,
                  bnd_ref, m_ref, acc_ref, alpha_ref, p_ref, *, tq, tk, nh, pair, window):
    u = pl.program_id(1)
    qi = pl.program_id(2)
    R = nh * tq
    nv = nh if pair else 1
    q0 = qi * tq
    S = k_ref.shape[0]

    @pl.when(qi == 0)
    def _stage_keys_values():
        k = k_ref[...]
        ka_ref[:, 0:LANES] = k
        ka_ref[:, LANES:2 * LANES] = kf_ref[...]
        k2 = jnp.square(k.astype(F32))
        lane = lax.broadcasted_iota(jnp.int32, k2.shape, 1)
        for hf in range(2):
            n2 = jnp.sum(jnp.where(lane // HEAD_DIM == hf, k2, 0.0), axis=1, keepdims=True)
            kn_ref[hf:hf + 1, :] = jnp.broadcast_to(jnp.sqrt(jnp.max(n2, axis=0, keepdims=True)), (1, LANES))
        ones_row = jnp.where(lax.broadcasted_iota(jnp.int32, (VROWS - HEAD_DIM, tk), 0) == 0, 1.0, 0.0)
        for v in range(nv):
            for jt in range(S // tk):
                vta_ref[v, jt, 0:HEAD_DIM, :] = vt_ref[v * HEAD_DIM:(v + 1) * HEAD_DIM, jt * tk:(jt + 1) * tk]
                vta_ref[v, jt, HEAD_DIM:VROWS, :] = ones_row.astype(BF16)

    qcat = jnp.concatenate([q_ref[hh * HEAD_DIM:(hh + 1) * HEAD_DIM, :] for hh in range(nh)], axis=1)
    qf = qcat.astype(F32)
    qq = jnp.concatenate([qf, qf], axis=0)
    rowi = lax.broadcasted_iota(jnp.int32, (2 * HEAD_DIM, R), 0)
    coli = lax.broadcasted_iota(jnp.int32, (2 * HEAD_DIM, R), 1)
    half = coli // tq if pair else u % 2
    qa_ref[0:2 * HEAD_DIM, :] = jnp.where(rowi // HEAD_DIM == half, qq, 0.0).astype(BF16)
    qa_ref[2 * HEAD_DIM:2 * HEAD_DIM + FEAT, :] = jnp.concatenate(
        [bias_ref[hh if pair else 0] for hh in range(nh)], axis=1)
    r = lax.broadcasted_iota(jnp.int32, (FEAT, R), 0)
    col = lax.broadcasted_iota(jnp.int32, (FEAT, R), 1)
    t = q0 + col % tq
    slope = _pow2_neg(u * nh + col // tq + 1)
    tb = (t // FEAT).astype(F32)
    tr = (t % FEAT).astype(F32)
    feats = jnp.where(r == 0, -FEAT * slope * tb,
                      jnp.where(r == 1, -slope * tr, jnp.where(r < 4, slope, 0.0)))
    qa_ref[2 * HEAD_DIM + FEAT:2 * LANES, :] = feats.astype(BF16)
    kn = [jnp.concatenate([kn_ref[hf:hf + 1, :]] * (R // LANES), axis=1) for hf in range(2)]
    half_row = lax.broadcasted_iota(jnp.int32, (1, R), 1) // tq if pair else u % 2
    qn = jnp.sqrt(jnp.sum(qf * qf, axis=0, keepdims=True))
    bnd_ref[...] = jnp.broadcast_to(qn * jnp.where(half_row == 0, kn[0], kn[1]), bnd_ref.shape)

    ncg = R // CG
    kd = (q0 + tq - 1) // tk

    def sweep(running_max):
        acc_ref[...] = jnp.zeros(acc_ref.shape, F32)
        if running_max:
            m_ref[...] = jnp.full(m_ref.shape, NEG, F32)

        def values(ki):
            for c in range(ncg):
                cols = pl.ds(c * CG, CG)
                v = (c * CG) // tq if pair else 0
                pv = _dot(vta_ref[v, ki], p_ref[:, cols])
                if running_max:
                    acc_ref[:, cols] = alpha_ref[0:1, cols] * acc_ref[:, cols] + pv
                else:
                    acc_ref[:, cols] = acc_ref[:, cols] + pv

        def scores(ki, masked, prev=None):
            keys = pl.ds(pl.multiple_of(ki * tk, tk), tk)
            ka = ka_ref[keys, :]
            s_groups = [_dot(ka, qa_ref[:, pl.ds(c * CG, CG)]) for c in range(ncg)]
            if prev is not None:
                values(prev)
            for c, s in enumerate(s_groups):
                cols = pl.ds(c * CG, CG)
                if masked:
                    s = s + mask_ref[(q0 + (c * CG) % tq - ki * tk) // CG]
                if running_max:
                    m_prev = m_ref[0:1, cols]
                    m_new = jnp.maximum(m_prev, jnp.max(s, axis=0, keepdims=True))
                    alpha_ref[:, cols] = jnp.broadcast_to(jnp.exp(m_prev - m_new), (alpha_ref.shape[0], CG))
                    m_ref[:, cols] = jnp.broadcast_to(m_new, (m_ref.shape[0], CG))
                    p_ref[:, cols] = jnp.exp(s - m_new).astype(BF16)
                else:
                    p_ref[:, cols] = jnp.exp(s - bnd_ref[0:1, cols]).astype(BF16)

        def steady(lo, hi, masked):
            def pair_body(j, carry):
                ki = lo + 2 * j
                scores(ki, masked, prev=ki - 1)
                scores(ki + 1, masked, prev=ki)
                return carry
            lax.fori_loop(0, (hi - lo) // 2, pair_body, 0)

            @pl.when((hi - lo) % 2 == 1)
            def _odd_one():
                scores(hi - 1, masked, prev=hi - 2)

        if window is None:
            @pl.when(kd == 0)
            def _only_diagonal():
                scores(0, True)
                values(0)

            @pl.when(kd > 0)
            def _past_then_diagonal():
                scores(0, False)
                steady(1, kd, False)
                scores(kd, True, prev=kd - 1)
                values(kd)
        else:
            k_lo = jnp.maximum(q0 - window + 1, 0) // tk
            scores(k_lo, True)
            steady(k_lo + 1, kd + 1, True)
            values(kd)

    def emit():
        acc = acc_ref[...]
        out = (acc[0:HEAD_DIM] / acc[HEAD_DIM:HEAD_DIM + 1]).astype(o_ref.dtype)
        if pair:
            for hh in range(nh):
                o_ref[hh] = out[:, hh * tq:(hh + 1) * tq]
        else:
            o_ref[0] = out

    sweep(False)
    trusted = jnp.min(acc_ref[HEAD_DIM:HEAD_DIM + 1, :]) >= MIN_WEIGHT_SUM

    @pl.when(trusted)
    def _done():
        emit()

    @pl.when(jnp.logical_not(trusted))
    def _redo_with_running_max():
        sweep(True)
        emit()


def _mask_table(tk, window):
    n_delta = (tk + (window or 0)) // CG
    d = np.arange(n_delta)[:, None, None] * CG + np.arange(CG)[None, None, :] - np.arange(tk)[None, :, None]
    ok = (d >= 0) & ((d < window) if window else True)
    return jnp.asarray(np.where(ok, 0.0, NEG), F32)


def _flash(q_t, bias_t, k, kfeat, v_t, *, B, S, units, tq, nh, pair, window):
    tk = TK
    R = nh * tq
    nq = S // tq
    nv = nh if pair else 1
    nb = nh if pair else 1
    assert tk % tq == 0 and S % tk == 0 and tq % CG == 0 and tk % CG == 0 and (window or 0) % CG == 0
    mask = _mask_table(tk, window)
    kcol = (lambda u: u) if pair else (lambda u: u // 2)
    out_w = tq if pair else R
    out_shape = (B, units * nb, HEAD_DIM, S if pair else S * nh)
    return pl.pallas_call(
        functools.partial(_flash_kernel, tq=tq, tk=tk, nh=nh, pair=pair, window=window),
        grid=(B, units, nq),
        in_specs=[pl.BlockSpec((nh * HEAD_DIM, tq), lambda b, u, i: (u, b * nq + i)),
                  pl.BlockSpec((None, nb, FEAT, tq), lambda b, u, i: (b, u, 0, i)),
                  pl.BlockSpec((S, LANES), lambda b, u, i: (b, kcol(u))),
                  pl.BlockSpec((S, LANES), lambda b, u, i: (0, 0)),
                  pl.BlockSpec((nv * HEAD_DIM, S), lambda b, u, i: (u, b)),
                  pl.BlockSpec(mask.shape, lambda b, u, i: (0, 0, 0))],
        out_specs=pl.BlockSpec((None, nb, HEAD_DIM, out_w), lambda b, u, i: (b, u, 0, i)),
        out_shape=jax.ShapeDtypeStruct(out_shape, BF16),
        scratch_shapes=[pltpu.VMEM((S, 2 * LANES), BF16),
                        pltpu.VMEM((nv, S // tk, VROWS, tk), BF16),
                        pltpu.VMEM((8, LANES), F32),
                        pltpu.VMEM((2 * LANES, R), BF16),
                        pltpu.VMEM((8, R), F32),
                        pltpu.VMEM((8, R), F32),
                        pltpu.VMEM((VROWS, R), F32),
                        pltpu.VMEM((8, R), F32),
                        pltpu.VMEM((tk, R), BF16)],
        compiler_params=pltpu.CompilerParams(dimension_semantics=("parallel", "parallel", "arbitrary"),
                                             vmem_limit_bytes=VMEM_LIMIT),
        name="flash",
    )(q_t, bias_t, k, kfeat, v_t, mask)


def _even_out_kernel(x_ref, ocmp_ref, osel_ref, owin_ref, om_ref, gate_ref, nz_ref, mz_ref,
                     e_ref, w_ref, g_ref, b_ref, o_ref, *, alpha, tm):
    G, nh = NSA_KV_GROUPS, NSA_HG
    sig = _sigmoid(gate_ref[...])
    shi, slo = _hilo(sig)

    def heads_t(ref):
        return jnp.concatenate([ref[g, :, hh * tm:(hh + 1) * tm] for g in range(G) for hh in range(nh)],
                               axis=0).astype(F32)

    o_nsa = None
    for br, ref in enumerate((ocmp_ref, osel_ref, owin_ref)):
        gb = _dot(e_ref[br], shi) + _dot(e_ref[br], slo)
        term = gb * heads_t(ref)
        o_nsa = term if o_nsa is None else o_nsa + term
    nz = nz_ref[...].astype(F32)
    mz = mz_ref[...].astype(F32)
    o_m = jnp.concatenate([om_ref[h] for h in range(MOBA_HEADS)], axis=0).astype(F32)
    a_t = jnp.concatenate([o_nsa * (nz * _sigmoid(nz)), o_m * (mz * _sigmoid(mz))], axis=0)
    y = _dot_tn(a_t.astype(BF16), w_ref[...])
    o_ref[...] = _layer_norm(alpha * x_ref[...] + y, g_ref[...], b_ref[...])


def _even_out(x2d, ocmp, osel, owin, om, gate_t, nz_t, mz_t, expand, w_out, ln_g, ln_b, alpha, *, B, S):
    T, D = x2d.shape
    tm = TQ
    nq = S // tm
    G, nh, H = NSA_KV_GROUPS, NSA_HG, MOBA_HEADS
    nsa = pl.BlockSpec((None, G, HEAD_DIM, nh * tm), lambda i: (i // nq, 0, 0, i % nq))
    feat = lambda n: pl.BlockSpec((n, tm), lambda i: (0, i))
    full = lambda a: pl.BlockSpec(a.shape, lambda i: (0,) * a.ndim)
    return pl.pallas_call(
        functools.partial(_even_out_kernel, alpha=alpha, tm=tm),
        grid=(T // tm,),
        in_specs=[pl.BlockSpec((tm, D), lambda i: (i, 0)), nsa, nsa, nsa,
                  pl.BlockSpec((None, H, HEAD_DIM, tm), lambda i: (i // nq, 0, 0, i % nq)),
                  feat(GATE_ROWS), feat(NSA_W), feat(MOBA_W),
                  full(expand), full(w_out), full(ln_g), full(ln_b)],
        out_specs=pl.BlockSpec((tm, D), lambda i: (i, 0)),
        out_shape=jax.ShapeDtypeStruct((T, D), F32),
        compiler_params=pltpu.CompilerParams(dimension_semantics=("parallel",),
                                             vmem_limit_bytes=VMEM_LIMIT),
        name="even_out",
    )(x2d, ocmp, osel, owin, om, gate_t, nz_t, mz_t, expand, w_out, ln_g, ln_b)


def _gla_head_exact(row0, h, q_ref, v_ref, o_ref, st_ref, b_ref, kf_ref, scale):
    C, dk, dv, SUB = GLA_CHUNK, GLA_DK, GLA_DV, GLA_SUB
    rows = lax.broadcasted_iota(jnp.int32, (C, dk), 0)
    sub_i = lax.broadcasted_iota(jnp.int32, (SUB, LANES), 0)
    lane = lax.broadcasted_iota(jnp.int32, (SUB, LANES), 1)
    ones = jnp.ones((dk, LANES), BF16)
    sl = pl.ds(h * dk, dk)
    rs_c = pl.ds(row0, C)
    bh = b_ref[:, sl]
    kf = kf_ref[:, sl]
    qf = q_ref[rs_c, sl].astype(F32) * scale
    vh = v_ref[rs_c, pl.ds(h * dv, dv)]
    st = st_ref[h]
    inter = _dot_nt((qf * jnp.exp(bh)).astype(BF16), st.astype(BF16))
    bl = b_ref[pl.ds(C - 1, 1), sl]
    kd = kf * jnp.exp(bl - bh)

    def off_block(r0, r1, c0, c1):
        ref_row = b_ref[pl.ds(r0, 1), sl]
        qt = qf[r0:r1] * jnp.exp(bh[r0:r1] - ref_row)
        ek = jnp.where((rows >= c0) & (rows < c1), ref_row - bh, NEG)
        kt = kf * jnp.exp(ek)
        return _dot_nt(qt.astype(BF16), kt.astype(BF16))

    lower = off_block(2 * SUB, C, 0, 2 * SUB)
    blocks = [None, off_block(SUB, 2 * SUB, 0, SUB), lower[0:SUB],
              lower[SUB:2 * SUB] + off_block(3 * SUB, C, 2 * SUB, 3 * SUB)]
    pieces = []
    for blk in range(C // SUB):
        r0 = blk * SUB
        q_i = qf[r0:r0 + SUB]
        b_i = bh[r0:r0 + SUB]
        for jj in range(SUB):
            b_j = b_ref[pl.ds(r0 + jj, 1), sl]
            k_j = kf_ref[pl.ds(r0 + jj, 1), sl]
            pieces.append((q_i * k_j * jnp.exp(jnp.minimum(b_i - b_j, 0.0))).astype(BF16))
    sums = _dot(jnp.concatenate(pieces, axis=0), ones)
    for blk in range(C // SUB):
        r0 = blk * SUB
        diag = jnp.zeros((SUB, LANES), F32)
        for jj in range(SUB):
            rs = sums[(r0 + jj) * SUB:(r0 + jj + 1) * SUB]
            diag = jnp.where((lane == r0 + jj) & (sub_i >= jj), rs, diag)
        diag = diag[:, 0:C]
        blocks[blk] = diag if blocks[blk] is None else blocks[blk] + diag
    att = jnp.concatenate(blocks, axis=0)
    o_ref[rs_c, pl.ds(h * dv, dv)] = (inter + _dot(att.astype(BF16), vh)).astype(o_ref.dtype)
    st_ref[h] = st * jnp.exp(bl) + _dot_tn(vh, kd.astype(BF16))


def _gla_head_bounded(row0, h, q_ref, v_ref, o_ref, st_ref, b_ref, kf_ref, scale):
    C, dk, dv = GLA_CHUNK, GLA_DK, GLA_DV
    sl = pl.ds(h * dk, dk)
    rs_c = pl.ds(row0, C)
    bh = b_ref[rs_c, sl]
    kf = kf_ref[rs_c, sl]
    qf = q_ref[rs_c, sl].astype(F32) * scale
    vh = v_ref[rs_c, pl.ds(h * dv, dv)]
    st = st_ref[h]
    qs = (qf * jnp.exp(bh)).astype(BF16)
    kt = (kf * jnp.exp(-bh)).astype(BF16)
    res = _dot_nt(qs, jnp.concatenate([st.astype(BF16), kt], axis=0))
    i = lax.broadcasted_iota(jnp.int32, (C, C), 0)
    j = lax.broadcasted_iota(jnp.int32, (C, C), 1)
    att = jnp.where(i >= j, res[:, dv:dv + C], 0.0)
    o_ref[rs_c, pl.ds(h * dv, dv)] = (res[:, 0:dv] + _dot(att.astype(BF16), vh)).astype(o_ref.dtype)
    bl = b_ref[pl.ds(row0 + C - 1, 1), sl]
    kd = kf * jnp.exp(bl - bh)
    st_ref[h] = st * jnp.exp(bl) + _dot_tn(vh, kd.astype(BF16))


def _gla_kernel(q_ref, k_ref, v_ref, glr_ref, w2_ref, gb_ref, tril_ref, o_ref,
                st_ref, b_ref, kf_ref, bc_ref, kc_ref, *, scale):
    C, H = GLA_CHUNK, GLA_HEADS
    nch = q_ref.shape[0] // C

    @pl.when(pl.program_id(1) == 0)
    def _init():
        st_ref[...] = jnp.zeros(st_ref.shape, F32)

    ghi, glo = _hilo(glr_ref[...])
    whi, wlo = _hilo(w2_ref[...])
    pre = _dot(ghi, whi) + _dot(glo, whi) + _dot(ghi, wlo) + gb_ref[...]
    la = (jnp.minimum(pre, 0.0) - jnp.log(1.0 + jnp.exp(-jnp.abs(pre)))) * (1.0 / GLA_TAU)
    lhi, llo = _hilo(la)
    b_all = _dot(tril_ref[...], lhi) + _dot(tril_ref[...], llo)
    b_ref[...] = b_all
    kf_ref[...] = k_ref[...].astype(F32)
    decay = -b_all[C - 1:C, :]
    for c in range(1, nch):
        decay = jnp.maximum(decay, -b_all[c * C + C - 1:(c + 1) * C, :])
    bounded = jnp.max(decay) <= GLA_FACTOR_MAX_DECAY

    @pl.when(bounded)
    def _factored():
        for c in range(nch):
            for h in range(H):
                _gla_head_bounded(c * C, h, q_ref, v_ref, o_ref, st_ref, b_ref, kf_ref, scale)

    @pl.when(jnp.logical_not(bounded))
    def _exact():
        def chunk(c, carry):
            row0 = pl.multiple_of(c * C, C)
            bc_ref[...] = b_ref[pl.ds(row0, C), :]
            kc_ref[...] = kf_ref[pl.ds(row0, C), :]
            for h in range(H):
                _gla_head_exact(row0, h, q_ref, v_ref, o_ref, st_ref, bc_ref, kc_ref, scale)
            return carry
        lax.fori_loop(0, nch, chunk, 0)


def _gla(q, k, v, glr, w2p, gb, tril, *, B, S):
    C, H, dk, dv = GLA_CHUNK, GLA_HEADS, GLA_DK, GLA_DV
    rows = tril.shape[0]
    nchunk = S // rows
    tok = lambda n: pl.BlockSpec((rows, n), lambda b, c: (b * nchunk + c, 0))
    const = lambda shape: pl.BlockSpec(shape, lambda b, c: (0,) * len(shape))
    return pl.pallas_call(
        functools.partial(_gla_kernel, scale=float(dk) ** -0.5),
        grid=(B, nchunk),
        in_specs=[tok(H * dk), tok(H * dk), tok(H * dv), tok(LANES),
                  const(w2p.shape), const(gb.shape), const(tril.shape)],
        out_specs=tok(H * dv),
        out_shape=jax.ShapeDtypeStruct((B * S, H * dv), BF16),
        scratch_shapes=[pltpu.VMEM((H, dv, dk), F32),
                        pltpu.VMEM((rows, H * dk), F32),
                        pltpu.VMEM((rows, H * dk), F32),
                        pltpu.VMEM((C, H * dk), F32),
                        pltpu.VMEM((C, H * dk), F32)],
        compiler_params=pltpu.CompilerParams(dimension_semantics=("parallel", "arbitrary"),
                                             vmem_limit_bytes=VMEM_LIMIT),
        name="gla",
    )(q, k, v, glr, w2p, gb, tril)


def _odd_out_kernel(x_ref, o_ref_in, z_ref, gn_ref, w_ref, g_ref, b_ref, out_ref, *, alpha):
    H, dv = GLA_HEADS, GLA_DV
    parts = []
    for h in range(H):
        sl = pl.ds(h * dv, dv)
        oh = o_ref_in[:, sl].astype(F32)
        r = lax.rsqrt(jnp.mean(oh * oh, axis=-1, keepdims=True) + LN_EPS)
        z = z_ref[:, sl].astype(F32)
        parts.append((oh * r * gn_ref[:, sl] * (z * _sigmoid(z))).astype(BF16))
    y = _dot(jnp.concatenate(parts, axis=1), w_ref[...])
    out_ref[...] = _layer_norm(alpha * x_ref[...] + y, g_ref[...], b_ref[...])


def _odd_out(x2d, o, z, gn_g, w_out, ln_g, ln_b, alpha, tm=256):
    T, D = x2d.shape
    W = o.shape[1]
    row = lambda n: pl.BlockSpec((tm, n), lambda i: (i, 0))
    return pl.pallas_call(
        functools.partial(_odd_out_kernel, alpha=alpha),
        grid=(T // tm,),
        in_specs=[row(D), row(W), row(W),
                  pl.BlockSpec((1, W), lambda i: (0, 0)),
                  pl.BlockSpec((W, D), lambda i: (0, 0)),
                  pl.BlockSpec((1, D), lambda i: (0, 0)),
                  pl.BlockSpec((1, D), lambda i: (0, 0))],
        out_specs=row(D),
        out_shape=jax.ShapeDtypeStruct((T, D), F32),
        compiler_params=pltpu.CompilerParams(dimension_semantics=("parallel",),
                                             vmem_limit_bytes=VMEM_LIMIT),
        name="odd_out",
    )(x2d, o, z, gn_g, w_out, ln_g, ln_b)


def _key_features(S, block):
    s = np.arange(S)
    f = np.zeros((S, LANES), np.float32)
    f[s, s // block] = 1.0
    f[:, FEAT + 0] = 1.0
    f[:, FEAT + 1] = 1.0
    f[:, FEAT + 2] = FEAT * (s // FEAT)
    f[:, FEAT + 3] = s % FEAT
    return jnp.asarray(f, BF16)


def _pad_to(a, n, axis):
    pad = [(0, 0)] * a.ndim
    pad[axis] = (0, n - a.shape[axis])
    return jnp.pad(a, pad)


def _even_layer(x2d, B, S, w_in, cmp_pos, cmp_w1, cmp_b1, cmp_w2, w_out, ln_g, ln_b, alpha):
    T, D = x2d.shape
    G, hd = NSA_KV_GROUPS, HEAD_DIM
    scale = hd ** -0.5
    wq, wkv, wgate, wnz, wmq, wmk, wmv, wmz = jnp.split(w_in, np.cumsum(EVEN_SPLITS)[:-1].tolist(), axis=1)
    wkc, wvc, wks, wvs, wkw, wvw = jnp.split(wkv, 6, axis=1)
    w_tok = jnp.concatenate([wkc, wvc, wks, wkw, wmk], axis=1).astype(BF16)
    w_tr = jnp.concatenate([wq * scale, wvs, wvw, wmq * scale, wmv, wnz, wmz,
                            _pad_to(wgate, GATE_ROWS, 1)], axis=1).T.astype(BF16)
    (kcmp, vcmp, ksel, kwin, mk, nq_t, vsel_t, vwin_t, mq_t, mv_t, nz_t, mz_t, gate_t) = _proj(
        x2d, w_tok, (NSA_KV_W,) * 4 + (MOBA_W,), (BF16,) * 5,
        w_tr, (NSA_W, NSA_KV_W, NSA_KV_W, MOBA_W, MOBA_W, NSA_W, MOBA_W, GATE_ROWS),
        (BF16,) * 7 + (F32,))

    L, d, HID = NSA_CMP_BLOCK, NSA_CMP_STRIDE, NSA_CMP_HIDDEN
    ncmp = (S - L) // d + 1
    NC = S // d
    eye = jnp.eye(G, dtype=F32)
    w1r = cmp_w1.reshape(2, 2, d, hd, HID)
    w1p = jnp.einsum('ialdj,gh->ialgdhj', w1r, eye).reshape(2, 2, d * G * hd, G * HID).astype(BF16)
    posp = jnp.broadcast_to(cmp_pos.reshape(2, 2, d, 1, hd), (2, 2, d, G, hd)).reshape(2, 2, 1, d * G * hd)
    b1p = jnp.tile(cmp_b1, (1, G)).reshape(2, 1, G * HID)
    w2p = jnp.einsum('ijd,gh->igjhd', _pad_to(cmp_w2, LANES, 2), eye).reshape(2, G * HID, G * LANES).astype(BF16)
    cfeat = np.zeros((NC, G, LANES), np.float32)
    cfeat[:, :, hd + 0] = 1.0
    cfeat[:, :, hd + 1] = 1.0
    cfeat[:, :, hd + 2] = (np.arange(NC) * d)[:, None]
    cfeat[:, :, hd + 3] = L - 1
    kvc = _cmp_mlp(kcmp.reshape(B, NC, d * NSA_KV_W), vcmp.reshape(B, NC, d * NSA_KV_W),
                   w1p, posp, b1p, w2p, jnp.asarray(cfeat.reshape(NC, G * LANES)), ncmp)

    ns = S // NSA_SEL_BLOCK
    cmp_start = np.arange(NC) * d
    sel_start = np.arange(FEAT) * NSA_SEL_BLOCK
    ovt = ((cmp_start[None, :] <= sel_start[:, None] + NSA_SEL_BLOCK - 1)
           & (cmp_start[None, :] + L - 1 >= sel_start[:, None])
           & (np.arange(NC)[None, :] < ncmp) & (np.arange(FEAT)[:, None] < ns))
    o_cmp, bias_sel = _cmp_attn(nq_t, kvc, jnp.asarray(ovt, BF16), B=B, S=S)

    feats_sel = _key_features(S, NSA_SEL_BLOCK)
    o_sel = _flash(nq_t, bias_sel, ksel, feats_sel, vsel_t,
                   B=B, S=S, units=G, tq=TQ, nh=NSA_HG, pair=False, window=None)
    o_win = _flash(nq_t, jnp.zeros_like(bias_sel), kwin, feats_sel, vwin_t,
                   B=B, S=S, units=G, tq=TQ, nh=NSA_HG, pair=False, window=NSA_WINDOW)

    avg = np.zeros((MOBA_NBP, S), np.float32)
    avg[np.arange(S) // MOBA_BLOCK, np.arange(S)] = 1.0 / MOBA_BLOCK
    bias_m = _moba_gate(mq_t, mk, jnp.asarray(avg, BF16), B=B, S=S)
    o_m = _flash(mq_t, bias_m, mk, _key_features(S, MOBA_BLOCK), mv_t,
                 B=B, S=S, units=MOBA_HEADS // 2, tq=TQM, nh=2, pair=True, window=None)

    n = np.arange(NSA_W)[:, None]
    c = np.arange(GATE_ROWS)[None, :]
    expand = jnp.asarray(np.stack([(c == 3 * (n // hd) + br) for br in range(3)]), BF16)
    return _even_out(x2d, o_cmp, o_sel, o_win, o_m, gate_t, nz_t, mz_t, expand, w_out.astype(BF16),
                     ln_g.reshape(1, D), ln_b.reshape(1, D), alpha, B=B, S=S)


def _odd_layer(x2d, B, S, w_in, gate_w2, gate_b, gn_g, w_out, ln_g, ln_b, alpha):
    T, D = x2d.shape
    H, dk, dv, C = GLA_HEADS, GLA_DK, GLA_DV, GLA_CHUNK
    wq, wk, wv, wg, wz = jnp.split(w_in, np.cumsum(ODD_SPLITS)[:-1].tolist(), axis=1)
    w_perm = jnp.concatenate([wq, wk, wv, wz, _pad_to(wg, LANES, 1)], axis=1).astype(BF16)
    q, k, v, z, glr = _proj(x2d, w_perm, (H * dk, H * dk, H * dv, H * dv, LANES),
                            (BF16, BF16, BF16, BF16, F32))
    w2p = _pad_to(gate_w2, LANES, 0)
    nch = min(GLA_STEP_CHUNKS, S // C)
    tril = jnp.asarray(np.kron(np.eye(nch), np.tril(np.ones((C, C)))), BF16)
    o = _gla(q, k, v, glr, w2p, gate_b.reshape(1, H * dk), tril, B=B, S=S)
    return _odd_out(x2d, o, z, gn_g.reshape(1, H * dv), w_out.astype(BF16),
                    ln_g.reshape(1, D), ln_b.reshape(1, D), alpha)


def kernel(x, ev_w_in, ev_cmp_pos, ev_cmp_w1, ev_cmp_b1, ev_cmp_w2, ev_w_out, ev_ln_g, ev_ln_b,
           od_w_in, od_gate_w2, od_gate_b, od_gn_g, od_w_out, od_ln_g, od_ln_b):
    B, S, D = x.shape
    depth = ev_w_in.shape[0] + od_w_in.shape[0]
    alpha = (2.0 * depth) ** 0.25
    h = x.reshape(B * S, D)
    for layer in range(depth):
        i = layer // 2
        if layer % 2 == 0:
            h = _even_layer(h, B, S, ev_w_in[i], ev_cmp_pos[i], ev_cmp_w1[i], ev_cmp_b1[i], ev_cmp_w2[i],
                            ev_w_out[i], ev_ln_g[i], ev_ln_b[i], alpha)
        else:
            h = _odd_layer(h, B, S, od_w_in[i], od_gate_w2[i], od_gate_b[i], od_gn_g[i],
                           od_w_out[i], od_ln_g[i], od_ln_b[i], alpha)
    return h.reshape(B, S, D)
```

```python
import functools

import numpy as np
import jax
import jax.numpy as jnp
from jax import lax
from jax.experimental import pallas as pl
from jax.experimental.pallas import tpu as pltpu

F32 = jnp.float32
BF16 = jnp.bfloat16

HEAD_DIM = 64
NSA_HEADS = 8
NSA_KV_GROUPS = 2
NSA_HG = NSA_HEADS // NSA_KV_GROUPS
NSA_CMP_BLOCK = 32
NSA_CMP_STRIDE = 16
NSA_CMP_HIDDEN = 128
NSA_SEL_BLOCK = 64
NSA_TOP_N = 16
NSA_WINDOW = 512
MOBA_HEADS = 8
MOBA_BLOCK = 256
MOBA_TOP_K = 3
GLA_HEADS = 4
GLA_DK = 128
GLA_DV = 256
GLA_GATE_RANK = 16
GLA_TAU = 16.0
GLA_CHUNK = 64
GLA_SUB = 16
GLA_STEP_CHUNKS = 8
GLA_FACTOR_MAX_DECAY = 60.0
LN_EPS = 1e-5
NEG = -1e30
FORCE_BONUS = 1e4

NSA_W = NSA_HEADS * HEAD_DIM
NSA_KV_W = NSA_KV_GROUPS * HEAD_DIM
MOBA_W = MOBA_HEADS * HEAD_DIM
EVEN_SPLITS = (NSA_W, 6 * NSA_KV_W, 3 * NSA_HEADS, NSA_W, MOBA_W, MOBA_W, MOBA_W, MOBA_W)
ODD_SPLITS = (GLA_HEADS * GLA_DK, GLA_HEADS * GLA_DK, GLA_HEADS * GLA_DV, GLA_GATE_RANK, GLA_HEADS * GLA_DV)

LANES = 128
SUBLANES = 8
FEAT = 64
ALIBI_ROWS = 16
GATE_ROWS = 32
MOBA_NBP = 16
TQ = 512
TQM = 512
TK = 512
CG = 256
MIN_WEIGHT_SUM = 1e-25
VROWS = HEAD_DIM + 16
VMEM_LIMIT = 48 * 1024 * 1024


def _dot(a, b):
    return jnp.dot(a, b, preferred_element_type=F32)


def _dot_nt(a, b):
    return lax.dot_general(a, b, (((1,), (1,)), ((), ())), preferred_element_type=F32)


def _dot_tn(a, b):
    return lax.dot_general(a, b, (((0,), (0,)), ((), ())), preferred_element_type=F32)


def _hilo(a):
    hi = a.astype(BF16)
    lo = (a - hi.astype(F32)).astype(BF16)
    return hi, lo


def _pow2_neg(n):
    return lax.bitcast_convert_type(lax.shift_left(127 - n, 23), F32)


def _sigmoid(x):
    return 1.0 / (1.0 + jnp.exp(-x))


def _layer_norm(r, g, b):
    mu = jnp.mean(r, axis=-1, keepdims=True)
    d = r - mu
    var = jnp.mean(d * d, axis=-1, keepdims=True)
    return d * lax.rsqrt(var + LN_EPS) * g + b


def _alibi_query_rows(t0, tq, head0, R):
    r = lax.broadcasted_iota(jnp.int32, (ALIBI_ROWS, R), 0)
    col = lax.broadcasted_iota(jnp.int32, (ALIBI_ROWS, R), 1)
    t = t0 + col % tq
    slope = _pow2_neg(head0 + col // tq + 1)
    feats = jnp.where(r == 0, -FEAT * slope * (t // FEAT).astype(F32),
                      jnp.where(r == 1, -slope * (t % FEAT).astype(F32), jnp.where(r < 4, slope, 0.0)))
    return jnp.concatenate([feats.astype(BF16), jnp.zeros((FEAT - ALIBI_ROWS, R), BF16)], axis=0)


def _ranks(sc_ref, n_real):
    n_rows, width = sc_ref.shape
    out = []
    for v in range(n_rows // SUBLANES):
        lo = v * SUBLANES
        sc = sc_ref[lo:lo + SUBLANES, :]
        idx = lo + lax.broadcasted_iota(jnp.int32, (SUBLANES, width), 0)
        rank = jnp.zeros((SUBLANES, width), jnp.int32)
        for jp in range(n_real):
            r = sc_ref[pl.ds(jp, 1), :]
            if jp < lo:
                beats = r >= sc
            elif jp >= lo + SUBLANES - 1:
                beats = r > sc
            else:
                beats = (r > sc) | ((r == sc) & (idx > jp))
            rank = rank + jnp.where(beats, 1, 0)
        out.append(rank)
    return jnp.concatenate(out, axis=0)


def _proj_kernel(*refs, tok_splits, tr_splits):
    x_ref, wt_ref = refs[0], refs[1]
    outs = list(refs[3:] if tr_splits else refs[2:])
    x = x_ref[...]
    xb = x.astype(BF16)
    off = 0
    for n in tok_splits:
        o_ref = outs.pop(0)
        o_ref[...] = _dot(xb, wt_ref[:, off:off + n]).astype(o_ref.dtype)
        off += n
    if tr_splits:
        wtr_ref = refs[2]
        xt = x.T.astype(BF16)
        off = 0
        for n in tr_splits:
            o_ref = outs.pop(0)
            o_ref[...] = _dot(wtr_ref[off:off + n, :], xt).astype(o_ref.dtype)
            off += n


def _proj(x2d, w_tok, tok_splits, tok_dtypes, w_tr=None, tr_splits=(), tr_dtypes=(), tm=512):
    T, D = x2d.shape
    assert T % tm == 0 and sum(tok_splits) == w_tok.shape[1]
    in_specs = [pl.BlockSpec((tm, D), lambda i: (i, 0)),
                pl.BlockSpec(w_tok.shape, lambda i: (0, 0))]
    args = [x2d, w_tok]
    if tr_splits:
        assert sum(tr_splits) == w_tr.shape[0]
        in_specs.append(pl.BlockSpec(w_tr.shape, lambda i: (0, 0)))
        args.append(w_tr)
    out_specs = ([pl.BlockSpec((tm, n), lambda i: (i, 0)) for n in tok_splits]
                 + [pl.BlockSpec((n, tm), lambda i: (0, i)) for n in tr_splits])
    out_shape = ([jax.ShapeDtypeStruct((T, n), dt) for n, dt in zip(tok_splits, tok_dtypes)]
                 + [jax.ShapeDtypeStruct((n, T), dt) for n, dt in zip(tr_splits, tr_dtypes)])
    return pl.pallas_call(
        functools.partial(_proj_kernel, tok_splits=tuple(tok_splits), tr_splits=tuple(tr_splits)),
        grid=(T // tm,), in_specs=in_specs, out_specs=out_specs, out_shape=out_shape,
        compiler_params=pltpu.CompilerParams(dimension_semantics=("parallel",),
                                             vmem_limit_bytes=VMEM_LIMIT),
        name="proj",
    )(*args)


def _cmp_mlp_kernel(kx_ref, vx_ref, w1_ref, pos_ref, b1_ref, w2_ref, cf_ref, o_ref, *, ncmp):
    for i, x_ref in enumerate((kx_ref, vx_ref)):
        x = x_ref[...]
        nc, cw = x.shape
        a = _dot(x, w1_ref[i, 0])
        bm = _dot(x, w1_ref[i, 1])
        c = b1_ref[i]
        for half in range(2):
            phi, plo = _hilo(jnp.broadcast_to(pos_ref[i, half], (8, cw)))
            c = c + (_dot(phi, w1_ref[i, half]) + _dot(plo, w1_ref[i, half]))[0:1]
        hid = a + pltpu.roll(bm, nc - 1, axis=0) + c
        hid = hid * _sigmoid(hid)
        row = lax.broadcasted_iota(jnp.int32, hid.shape, 0)
        hid = jnp.where(row < ncmp, hid, 0.0)
        out = _dot(hid.astype(BF16), w2_ref[i])
        if i == 0:
            out = out + cf_ref[...]
        o_ref[i] = out.astype(o_ref.dtype)


def _cmp_mlp(kx, vx, w1p, posp, b1p, w2p, cfeat, ncmp):
    B, NC, CW = kx.shape
    OW = w2p.shape[-1]
    full = lambda a: pl.BlockSpec(a.shape, lambda b: (0,) * a.ndim)
    return pl.pallas_call(
        functools.partial(_cmp_mlp_kernel, ncmp=ncmp),
        grid=(B,),
        in_specs=[pl.BlockSpec((None, NC, CW), lambda b: (b, 0, 0)),
                  pl.BlockSpec((None, NC, CW), lambda b: (b, 0, 0)),
                  full(w1p), full(posp), full(b1p), full(w2p), full(cfeat)],
        out_specs=pl.BlockSpec((2, None, NC, OW), lambda b: (0, b, 0, 0)),
        out_shape=jax.ShapeDtypeStruct((2, B, NC, OW), BF16),
        compiler_params=pltpu.CompilerParams(dimension_semantics=("parallel",),
                                             vmem_limit_bytes=VMEM_LIMIT),
        name="cmp_mlp",
    )(kx, vx, w1p, posp, b1p, w2p, cfeat)


def _cmp_attn_kernel(q_ref, kc_ref, vc_ref, ovt_ref, o_ref, bias_ref, sc_ref, *, tq, nh, ns, ntop):
    g = pl.program_id(1)
    qi = pl.program_id(2)
    R = nh * tq
    nc = kc_ref.shape[0]
    qcat = jnp.concatenate([q_ref[hh * HEAD_DIM:(hh + 1) * HEAD_DIM, :] for hh in range(nh)], axis=1)
    qa = jnp.concatenate([qcat, _alibi_query_rows(qi * tq, tq, g * nh, R)], axis=0)
    s = _dot(kc_ref[...], qa)
    n = lax.broadcasted_iota(jnp.int32, (nc, R), 0)
    t_row = qi * tq + lax.broadcasted_iota(jnp.int32, (1, R), 1) % tq
    last = (t_row - (NSA_CMP_BLOCK - 1)) // NSA_CMP_STRIDE
    s = jnp.where(n <= last, s, NEG)
    e = jnp.exp(s - jnp.max(s, axis=0, keepdims=True))
    inv = jnp.where(last >= 0, 1.0 / jnp.sum(e, axis=0, keepdims=True), 0.0)
    p = e * inv
    o_ref[...] = _dot_tn(vc_ref[:, 0:HEAD_DIM], p.astype(BF16)).astype(o_ref.dtype)

    psum = p[:, 0:tq]
    for hh in range(1, nh):
        psum = psum + p[:, hh * tq:(hh + 1) * tq]
    phi, plo = _hilo(psum)
    imp = _dot(ovt_ref[...], phi) + _dot(ovt_ref[...], plo)

    j = lax.broadcasted_iota(jnp.int32, (FEAT, tq), 0)
    cur = (qi * tq + lax.broadcasted_iota(jnp.int32, (FEAT, tq), 1)) // NSA_SEL_BLOCK
    forced = jnp.where((j == 0) | (j == cur) | (j == cur - 1), FORCE_BONUS, 0.0)
    sc = jnp.where(j <= cur, imp + forced, NEG)
    sc_ref[...] = sc
    sel = (_ranks(sc_ref, ns) < ntop) & (j <= cur)
    bias_ref[...] = jnp.where(sel, 0.0, NEG).astype(bias_ref.dtype)


def _cmp_attn(nq_t, kvc, ovt, *, B, S):
    tq, nh, G = TQ, NSA_HG, NSA_KV_GROUPS
    R = nh * tq
    nq = S // tq
    NC = kvc.shape[2]
    ns = S // NSA_SEL_BLOCK
    assert ns <= FEAT
    ntop = min(NSA_TOP_N, ns)
    return pl.pallas_call(
        functools.partial(_cmp_attn_kernel, tq=tq, nh=nh, ns=ns, ntop=ntop),
        grid=(B, G, nq),
        in_specs=[pl.BlockSpec((nh * HEAD_DIM, tq), lambda b, g, i: (g, b * nq + i)),
                  pl.BlockSpec((None, None, NC, LANES), lambda b, g, i: (0, b, 0, g)),
                  pl.BlockSpec((None, None, NC, LANES), lambda b, g, i: (1, b, 0, g)),
                  pl.BlockSpec(ovt.shape, lambda b, g, i: (0, 0))],
        out_specs=[pl.BlockSpec((None, None, HEAD_DIM, R), lambda b, g, i: (b, g, 0, i)),
                   pl.BlockSpec((None, None, FEAT, tq), lambda b, g, i: (b, g, 0, i))],
        out_shape=[jax.ShapeDtypeStruct((B, G, HEAD_DIM, S * nh), BF16),
                   jax.ShapeDtypeStruct((B, G, FEAT, S), BF16)],
        scratch_shapes=[pltpu.VMEM((FEAT, tq), F32)],
        compiler_params=pltpu.CompilerParams(dimension_semantics=("parallel", "parallel", "parallel"),
                                             vmem_limit_bytes=VMEM_LIMIT),
        name="cmp_attn",
    )(nq_t, kvc, kvc, ovt)


def _moba_gate_kernel(q_ref, k_ref, a_ref, bias_ref, km_ref, sc_ref, *, tq, nb, ntop):
    h = pl.program_id(1)
    qi = pl.program_id(2)

    @pl.when(qi == 0)
    def _means():
        km_ref[...] = _dot(a_ref[...], k_ref[...])

    nbp = km_ref.shape[0]
    lane = lax.broadcasted_iota(jnp.int32, (nbp, LANES), 1)
    km = jnp.where(lane // HEAD_DIM == h % 2, km_ref[...], 0.0)
    khi, klo = _hilo(km)
    q = q_ref[...]
    qq = jnp.concatenate([q, q], axis=0)
    g_t = _dot(khi, qq) + _dot(klo, qq)
    n = lax.broadcasted_iota(jnp.int32, (nbp, tq), 0)
    cur = (qi * tq + lax.broadcasted_iota(jnp.int32, (nbp, tq), 1)) // MOBA_BLOCK
    sc = jnp.where(n < cur, g_t, NEG)
    sc_ref[...] = sc
    sel = ((_ranks(sc_ref, nb) < ntop) & (n < cur)) | (n == cur)
    bias_ref[0:nbp, :] = jnp.where(sel, 0.0, NEG).astype(bias_ref.dtype)
    bias_ref[nbp:FEAT, :] = jnp.zeros((FEAT - nbp, tq), bias_ref.dtype)


def _moba_gate(mq_t, mk, avg, *, B, S):
    tq, H = S, MOBA_HEADS
    nq = S // tq
    nb = S // MOBA_BLOCK
    nbp = avg.shape[0]
    assert nb <= nbp
    ntop = min(MOBA_TOP_K, nb)
    return pl.pallas_call(
        functools.partial(_moba_gate_kernel, tq=tq, nb=nb, ntop=ntop),
        grid=(B, H, nq),
        in_specs=[pl.BlockSpec((HEAD_DIM, tq), lambda b, h, i: (h, b * nq + i)),
                  pl.BlockSpec((S, LANES), lambda b, h, i: (b, h // 2)),
                  pl.BlockSpec(avg.shape, lambda b, h, i: (0, 0))],
        out_specs=pl.BlockSpec((None, None, FEAT, tq), lambda b, h, i: (b, h, 0, i)),
        out_shape=jax.ShapeDtypeStruct((B, H, FEAT, S), BF16),
        scratch_shapes=[pltpu.VMEM((nbp, LANES), F32), pltpu.VMEM((nbp, tq), F32)],
        compiler_params=pltpu.CompilerParams(dimension_semantics=("parallel", "parallel", "arbitrary"),
                                             vmem_limit_bytes=VMEM_LIMIT),
        name="moba_gate",
    )(mq_t, mk, avg)


def _flash_kernel(q_ref, bias_ref, k_ref, kf_ref, vt_ref, mask_ref, o_ref, ka_ref, vta_ref, kn_ref, qa_ref,
                  bnd_ref, m_ref, acc_ref, alpha_ref, p_ref, *, tq, tk, nh, pair, window):
    u = pl.program_id(1)
    qi = pl.program_id(2)
    R = nh * tq
    nv = nh if pair else 1
    q0 = qi * tq
    S = k_ref.shape[0]

    @pl.when(qi == 0)
    def _stage_keys_values():
        k = k_ref[...]
        ka_ref[:, 0:LANES] = k
        ka_ref[:, LANES:2 * LANES] = kf_ref[...]
        k2 = jnp.square(k.astype(F32))
        lane = lax.broadcasted_iota(jnp.int32, k2.shape, 1)
        for hf in range(2):
            n2 = jnp.sum(jnp.where(lane // HEAD_DIM == hf, k2, 0.0), axis=1, keepdims=True)
            kn_ref[hf:hf + 1, :] = jnp.broadcast_to(jnp.sqrt(jnp.max(n2, axis=0, keepdims=True)), (1, LANES))
        ones_row = jnp.where(lax.broadcasted_iota(jnp.int32, (VROWS - HEAD_DIM, tk), 0) == 0, 1.0, 0.0)
        for v in range(nv):
            for jt in range(S // tk):
                vta_ref[v, jt, 0:HEAD_DIM, :] = vt_ref[v * HEAD_DIM:(v + 1) * HEAD_DIM, jt * tk:(jt + 1) * tk]
                vta_ref[v, jt, HEAD_DIM:VROWS, :] = ones_row.astype(BF16)

    qcat = jnp.concatenate([q_ref[hh * HEAD_DIM:(hh + 1) * HEAD_DIM, :] for hh in range(nh)], axis=1)
    qf = qcat.astype(F32)
    qq = jnp.concatenate([qf, qf], axis=0)
    rowi = lax.broadcasted_iota(jnp.int32, (2 * HEAD_DIM, R), 0)
    coli = lax.broadcasted_iota(jnp.int32, (2 * HEAD_DIM, R), 1)
    half = coli // tq if pair else u % 2
    qa_ref[0:2 * HEAD_DIM, :] = jnp.where(rowi // HEAD_DIM == half, qq, 0.0).astype(BF16)
    qa_ref[2 * HEAD_DIM:2 * HEAD_DIM + FEAT, :] = jnp.concatenate(
        [bias_ref[hh if pair else 0] for hh in range(nh)], axis=1)
    qa_ref[2 * HEAD_DIM + FEAT:2 * LANES, :] = _alibi_query_rows(q0, tq, u * nh, R)
    kn = [jnp.concatenate([kn_ref[hf:hf + 1, :]] * (R // LANES), axis=1) for hf in range(2)]
    half_row = lax.broadcasted_iota(jnp.int32, (1, R), 1) // tq if pair else u % 2
    qn = jnp.sqrt(jnp.sum(qf * qf, axis=0, keepdims=True))
    bnd_ref[...] = jnp.broadcast_to(qn * jnp.where(half_row == 0, kn[0], kn[1]), bnd_ref.shape)

    ncg = R // CG
    kd = (q0 + tq - 1) // tk

    def sweep(running_max):
        acc_ref[...] = jnp.zeros(acc_ref.shape, F32)
        if running_max:
            m_ref[...] = jnp.full(m_ref.shape, NEG, F32)

        def values(ki):
            for c in range(ncg):
                cols = pl.ds(c * CG, CG)
                v = (c * CG) // tq if pair else 0
                pv = _dot(vta_ref[v, ki], p_ref[:, cols])
                if running_max:
                    acc_ref[:, cols] = alpha_ref[0:1, cols] * acc_ref[:, cols] + pv
                else:
                    acc_ref[:, cols] = acc_ref[:, cols] + pv

        def scores(ki, masked, prev=None):
            keys = pl.ds(pl.multiple_of(ki * tk, tk), tk)
            ka = ka_ref[keys, :]
            s_groups = [_dot(ka, qa_ref[:, pl.ds(c * CG, CG)]) for c in range(ncg)]
            if prev is not None:
                values(prev)
            for c, s in enumerate(s_groups):
                cols = pl.ds(c * CG, CG)
                if masked:
                    s = s + mask_ref[(q0 + (c * CG) % tq - ki * tk) // CG]
                if running_max:
                    m_prev = m_ref[0:1, cols]
                    m_new = jnp.maximum(m_prev, jnp.max(s, axis=0, keepdims=True))
                    alpha_ref[:, cols] = jnp.broadcast_to(jnp.exp(m_prev - m_new), (alpha_ref.shape[0], CG))
                    m_ref[:, cols] = jnp.broadcast_to(m_new, (m_ref.shape[0], CG))
                    p_ref[:, cols] = jnp.exp(s - m_new).astype(BF16)
                else:
                    p_ref[:, cols] = jnp.exp(s - bnd_ref[0:1, cols]).astype(BF16)

        def steady(lo, hi, masked):
            def pair_body(j, carry):
                ki = lo + 2 * j
                scores(ki, masked, prev=ki - 1)
                scores(ki + 1, masked, prev=ki)
                return carry
            lax.fori_loop(0, (hi - lo) // 2, pair_body, 0)

            @pl.when((hi - lo) % 2 == 1)
            def _odd_one():
                scores(hi - 1, masked, prev=hi - 2)

        if window is None:
            @pl.when(kd == 0)
            def _only_diagonal():
                scores(0, True)
                values(0)

            @pl.when(kd > 0)
            def _past_then_diagonal():
                scores(0, False)
                steady(1, kd, False)
                scores(kd, True, prev=kd - 1)
                values(kd)
        else:
            k_lo = jnp.maximum(q0 - window + 1, 0) // tk
            scores(k_lo, True)
            steady(k_lo + 1, kd + 1, True)
            values(kd)

    def emit():
        acc = acc_ref[...]
        out = (acc[0:HEAD_DIM] / acc[HEAD_DIM:HEAD_DIM + 1]).astype(o_ref.dtype)
        if pair:
            for hh in range(nh):
                o_ref[hh] = out[:, hh * tq:(hh + 1) * tq]
        else:
            o_ref[0] = out

    sweep(False)
    trusted = jnp.min(acc_ref[HEAD_DIM:HEAD_DIM + 1, :]) >= MIN_WEIGHT_SUM

    @pl.when(trusted)
    def _done():
        emit()

    @pl.when(jnp.logical_not(trusted))
    def _redo_with_running_max():
        sweep(True)
        emit()


def _mask_table(tk, window):
    n_delta = (tk + (window or 0)) // CG
    d = np.arange(n_delta)[:, None, None] * CG + np.arange(CG)[None, None, :] - np.arange(tk)[None, :, None]
    ok = (d >= 0) & ((d < window) if window else True)
    return jnp.asarray(np.where(ok, 0.0, NEG), F32)


def _flash(q_t, bias_t, k, kfeat, v_t, *, B, S, units, tq, nh, pair, window):
    tk = TK
    R = nh * tq
    nq = S // tq
    nv = nh if pair else 1
    nb = nh if pair else 1
    assert tk % tq == 0 and S % tk == 0 and tq % CG == 0 and tk % CG == 0 and (window or 0) % CG == 0
    mask = _mask_table(tk, window)
    kcol = (lambda u: u) if pair else (lambda u: u // 2)
    out_w = tq if pair else R
    out_shape = (B, units * nb, HEAD_DIM, S if pair else S * nh)
    return pl.pallas_call(
        functools.partial(_flash_kernel, tq=tq, tk=tk, nh=nh, pair=pair, window=window),
        grid=(B, units, nq),
        in_specs=[pl.BlockSpec((nh * HEAD_DIM, tq), lambda b, u, i: (u, b * nq + i)),
                  pl.BlockSpec((None, nb, FEAT, tq), lambda b, u, i: (b, u, 0, i)),
                  pl.BlockSpec((S, LANES), lambda b, u, i: (b, kcol(u))),
                  pl.BlockSpec((S, LANES), lambda b, u, i: (0, 0)),
                  pl.BlockSpec((nv * HEAD_DIM, S), lambda b, u, i: (u, b)),
                  pl.BlockSpec(mask.shape, lambda b, u, i: (0, 0, 0))],
        out_specs=pl.BlockSpec((None, nb, HEAD_DIM, out_w), lambda b, u, i: (b, u, 0, i)),
        out_shape=jax.ShapeDtypeStruct(out_shape, BF16),
        scratch_shapes=[pltpu.VMEM((S, 2 * LANES), BF16),
                        pltpu.VMEM((nv, S // tk, VROWS, tk), BF16),
                        pltpu.VMEM((8, LANES), F32),
                        pltpu.VMEM((2 * LANES, R), BF16),
                        pltpu.VMEM((8, R), F32),
                        pltpu.VMEM((8, R), F32),
                        pltpu.VMEM((VROWS, R), F32),
                        pltpu.VMEM((8, R), F32),
                        pltpu.VMEM((tk, R), BF16)],
        compiler_params=pltpu.CompilerParams(dimension_semantics=("parallel", "parallel", "arbitrary"),
                                             vmem_limit_bytes=VMEM_LIMIT),
        name="flash",
    )(q_t, bias_t, k, kfeat, v_t, mask)


def _even_out_kernel(x_ref, ocmp_ref, osel_ref, owin_ref, om_ref, gate_ref, nz_ref, mz_ref,
                     e_ref, w_ref, g_ref, b_ref, o_ref, *, alpha, tm):
    G, nh = NSA_KV_GROUPS, NSA_HG
    sig = _sigmoid(gate_ref[...])
    shi, slo = _hilo(sig)

    def heads_t(ref):
        return jnp.concatenate([ref[g, :, hh * tm:(hh + 1) * tm] for g in range(G) for hh in range(nh)],
                               axis=0).astype(F32)

    o_nsa = None
    for br, ref in enumerate((ocmp_ref, osel_ref, owin_ref)):
        gb = _dot(e_ref[br], shi) + _dot(e_ref[br], slo)
        term = gb * heads_t(ref)
        o_nsa = term if o_nsa is None else o_nsa + term
    nz = nz_ref[...].astype(F32)
    mz = mz_ref[...].astype(F32)
    o_m = jnp.concatenate([om_ref[h] for h in range(MOBA_HEADS)], axis=0).astype(F32)
    a_t = jnp.concatenate([o_nsa * (nz * _sigmoid(nz)), o_m * (mz * _sigmoid(mz))], axis=0)
    y = _dot_tn(a_t.astype(BF16), w_ref[...])
    o_ref[...] = _layer_norm(alpha * x_ref[...] + y, g_ref[...], b_ref[...])


def _even_out(x2d, ocmp, osel, owin, om, gate_t, nz_t, mz_t, expand, w_out, ln_g, ln_b, alpha, *, B, S):
    T, D = x2d.shape
    tm = TQ
    nq = S // tm
    G, nh, H = NSA_KV_GROUPS, NSA_HG, MOBA_HEADS
    nsa = pl.BlockSpec((None, G, HEAD_DIM, nh * tm), lambda i: (i // nq, 0, 0, i % nq))
    feat = lambda n: pl.BlockSpec((n, tm), lambda i: (0, i))
    full = lambda a: pl.BlockSpec(a.shape, lambda i: (0,) * a.ndim)
    return pl.pallas_call(
        functools.partial(_even_out_kernel, alpha=alpha, tm=tm),
        grid=(T // tm,),
        in_specs=[pl.BlockSpec((tm, D), lambda i: (i, 0)), nsa, nsa, nsa,
                  pl.BlockSpec((None, H, HEAD_DIM, tm), lambda i: (i // nq, 0, 0, i % nq)),
                  feat(GATE_ROWS), feat(NSA_W), feat(MOBA_W),
                  full(expand), full(w_out), full(ln_g), full(ln_b)],
        out_specs=pl.BlockSpec((tm, D), lambda i: (i, 0)),
        out_shape=jax.ShapeDtypeStruct((T, D), F32),
        compiler_params=pltpu.CompilerParams(dimension_semantics=("parallel",),
                                             vmem_limit_bytes=VMEM_LIMIT),
        name="even_out",
    )(x2d, ocmp, osel, owin, om, gate_t, nz_t, mz_t, expand, w_out, ln_g, ln_b)


def _gla_head_exact(row0, h, q_ref, v_ref, o_ref, st_ref, b_ref, kf_ref, scale):
    C, dk, dv, SUB = GLA_CHUNK, GLA_DK, GLA_DV, GLA_SUB
    rows = lax.broadcasted_iota(jnp.int32, (C, dk), 0)
    sub_i = lax.broadcasted_iota(jnp.int32, (SUB, LANES), 0)
    lane = lax.broadcasted_iota(jnp.int32, (SUB, LANES), 1)
    ones = jnp.ones((dk, LANES), BF16)
    sl = pl.ds(h * dk, dk)
    rs_c = pl.ds(row0, C)
    bh = b_ref[:, sl]
    kf = kf_ref[:, sl]
    qf = q_ref[rs_c, sl].astype(F32) * scale
    vh = v_ref[rs_c, pl.ds(h * dv, dv)]
    st = st_ref[h]
    inter = _dot_nt((qf * jnp.exp(bh)).astype(BF16), st.astype(BF16))
    bl = b_ref[pl.ds(C - 1, 1), sl]
    kd = kf * jnp.exp(bl - bh)

    def off_block(r0, r1, c0, c1):
        ref_row = b_ref[pl.ds(r0, 1), sl]
        qt = qf[r0:r1] * jnp.exp(bh[r0:r1] - ref_row)
        ek = jnp.where((rows >= c0) & (rows < c1), ref_row - bh, NEG)
        kt = kf * jnp.exp(ek)
        return _dot_nt(qt.astype(BF16), kt.astype(BF16))

    lower = off_block(2 * SUB, C, 0, 2 * SUB)
    blocks = [None, off_block(SUB, 2 * SUB, 0, SUB), lower[0:SUB],
              lower[SUB:2 * SUB] + off_block(3 * SUB, C, 2 * SUB, 3 * SUB)]
    pieces = []
    for blk in range(C // SUB):
        r0 = blk * SUB
        q_i = qf[r0:r0 + SUB]
        b_i = bh[r0:r0 + SUB]
        for jj in range(SUB):
            b_j = b_ref[pl.ds(r0 + jj, 1), sl]
            k_j = kf_ref[pl.ds(r0 + jj, 1), sl]
            pieces.append((q_i * k_j * jnp.exp(jnp.minimum(b_i - b_j, 0.0))).astype(BF16))
    sums = _dot(jnp.concatenate(pieces, axis=0), ones)
    for blk in range(C // SUB):
        r0 = blk * SUB
        diag = jnp.zeros((SUB, LANES), F32)
        for jj in range(SUB):
            rs = sums[(r0 + jj) * SUB:(r0 + jj + 1) * SUB]
            diag = jnp.where((lane == r0 + jj) & (sub_i >= jj), rs, diag)
        diag = diag[:, 0:C]
        blocks[blk] = diag if blocks[blk] is None else blocks[blk] + diag
    att = jnp.concatenate(blocks, axis=0)
    o_ref[rs_c, pl.ds(h * dv, dv)] = (inter + _dot(att.astype(BF16), vh)).astype(o_ref.dtype)
    st_ref[h] = st * jnp.exp(bl) + _dot_tn(vh, kd.astype(BF16))


def _gla_head_bounded(row0, h, q_ref, v_ref, o_ref, st_ref, b_ref, kf_ref, scale):
    C, dk, dv = GLA_CHUNK, GLA_DK, GLA_DV
    sl = pl.ds(h * dk, dk)
    rs_c = pl.ds(row0, C)
    bh = b_ref[rs_c, sl]
    kf = kf_ref[rs_c, sl]
    qf = q_ref[rs_c, sl].astype(F32) * scale
    vh = v_ref[rs_c, pl.ds(h * dv, dv)]
    st = st_ref[h]
    qs = (qf * jnp.exp(bh)).astype(BF16)
    kt = (kf * jnp.exp(-bh)).astype(BF16)
    res = _dot_nt(qs, jnp.concatenate([st.astype(BF16), kt], axis=0))
    i = lax.broadcasted_iota(jnp.int32, (C, C), 0)
    j = lax.broadcasted_iota(jnp.int32, (C, C), 1)
    att = jnp.where(i >= j, res[:, dv:dv + C], 0.0)
    o_ref[rs_c, pl.ds(h * dv, dv)] = (res[:, 0:dv] + _dot(att.astype(BF16), vh)).astype(o_ref.dtype)
    bl = b_ref[pl.ds(row0 + C - 1, 1), sl]
    kd = kf * jnp.exp(bl - bh)
    st_ref[h] = st * jnp.exp(bl) + _dot_tn(vh, kd.astype(BF16))


def _gla_kernel(q_ref, k_ref, v_ref, glr_ref, w2_ref, gb_ref, tril_ref, o_ref,
                st_ref, b_ref, kf_ref, bc_ref, kc_ref, *, scale):
    C, H = GLA_CHUNK, GLA_HEADS
    nch = q_ref.shape[0] // C

    @pl.when(pl.program_id(1) == 0)
    def _init():
        st_ref[...] = jnp.zeros(st_ref.shape, F32)

    ghi, glo = _hilo(glr_ref[...])
    whi, wlo = _hilo(w2_ref[...])
    pre = _dot(ghi, whi) + _dot(glo, whi) + _dot(ghi, wlo) + gb_ref[...]
    la = (jnp.minimum(pre, 0.0) - jnp.log(1.0 + jnp.exp(-jnp.abs(pre)))) * (1.0 / GLA_TAU)
    lhi, llo = _hilo(la)
    b_all = _dot(tril_ref[...], lhi) + _dot(tril_ref[...], llo)
    b_ref[...] = b_all
    kf_ref[...] = k_ref[...].astype(F32)
    decay = -b_all[C - 1:C, :]
    for c in range(1, nch):
        decay = jnp.maximum(decay, -b_all[c * C + C - 1:(c + 1) * C, :])
    bounded = jnp.max(decay) <= GLA_FACTOR_MAX_DECAY

    @pl.when(bounded)
    def _factored():
        for c in range(nch):
            for h in range(H):
                _gla_head_bounded(c * C, h, q_ref, v_ref, o_ref, st_ref, b_ref, kf_ref, scale)

    @pl.when(jnp.logical_not(bounded))
    def _exact():
        def chunk(c, carry):
            row0 = pl.multiple_of(c * C, C)
            bc_ref[...] = b_ref[pl.ds(row0, C), :]
            kc_ref[...] = kf_ref[pl.ds(row0, C), :]
            for h in range(H):
                _gla_head_exact(row0, h, q_ref, v_ref, o_ref, st_ref, bc_ref, kc_ref, scale)
            return carry
        lax.fori_loop(0, nch, chunk, 0)


def _gla(q, k, v, glr, w2p, gb, tril, *, B, S):
    C, H, dk, dv = GLA_CHUNK, GLA_HEADS, GLA_DK, GLA_DV
    rows = tril.shape[0]
    nchunk = S // rows
    tok = lambda n: pl.BlockSpec((rows, n), lambda b, c: (b * nchunk + c, 0))
    const = lambda shape: pl.BlockSpec(shape, lambda b, c: (0,) * len(shape))
    return pl.pallas_call(
        functools.partial(_gla_kernel, scale=float(dk) ** -0.5),
        grid=(B, nchunk),
        in_specs=[tok(H * dk), tok(H * dk), tok(H * dv), tok(LANES),
                  const(w2p.shape), const(gb.shape), const(tril.shape)],
        out_specs=tok(H * dv),
        out_shape=jax.ShapeDtypeStruct((B * S, H * dv), BF16),
        scratch_shapes=[pltpu.VMEM((H, dv, dk), F32),
                        pltpu.VMEM((rows, H * dk), F32),
                        pltpu.VMEM((rows, H * dk), F32),
                        pltpu.VMEM((C, H * dk), F32),
                        pltpu.VMEM((C, H * dk), F32)],
        compiler_params=pltpu.CompilerParams(dimension_semantics=("parallel", "arbitrary"),
                                             vmem_limit_bytes=VMEM_LIMIT),
        name="gla",
    )(q, k, v, glr, w2p, gb, tril)


def _odd_out_kernel(x_ref, o_ref_in, z_ref, gn_ref, w_ref, g_ref, b_ref, out_ref, *, alpha):
    H, dv = GLA_HEADS, GLA_DV
    parts = []
    for h in range(H):
        sl = pl.ds(h * dv, dv)
        oh = o_ref_in[:, sl].astype(F32)
        r = lax.rsqrt(jnp.mean(oh * oh, axis=-1, keepdims=True) + LN_EPS)
        z = z_ref[:, sl].astype(F32)
        parts.append((oh * r * gn_ref[:, sl] * (z * _sigmoid(z))).astype(BF16))
    y = _dot(jnp.concatenate(parts, axis=1), w_ref[...])
    out_ref[...] = _layer_norm(alpha * x_ref[...] + y, g_ref[...], b_ref[...])


def _odd_out(x2d, o, z, gn_g, w_out, ln_g, ln_b, alpha, tm=512):
    T, D = x2d.shape
    W = o.shape[1]
    row = lambda n: pl.BlockSpec((tm, n), lambda i: (i, 0))
    return pl.pallas_call(
        functools.partial(_odd_out_kernel, alpha=alpha),
        grid=(T // tm,),
        in_specs=[row(D), row(W), row(W),
                  pl.BlockSpec((1, W), lambda i: (0, 0)),
                  pl.BlockSpec((W, D), lambda i: (0, 0)),
                  pl.BlockSpec((1, D), lambda i: (0, 0)),
                  pl.BlockSpec((1, D), lambda i: (0, 0))],
        out_specs=row(D),
        out_shape=jax.ShapeDtypeStruct((T, D), F32),
        compiler_params=pltpu.CompilerParams(dimension_semantics=("parallel",),
                                             vmem_limit_bytes=VMEM_LIMIT),
        name="odd_out",
    )(x2d, o, z, gn_g, w_out, ln_g, ln_b)


def _key_features(S, block):
    s = np.arange(S)
    f = np.zeros((S, LANES), np.float32)
    f[s, s // block] = 1.0
    f[:, FEAT + 0] = 1.0
    f[:, FEAT + 1] = 1.0
    f[:, FEAT + 2] = FEAT * (s // FEAT)
    f[:, FEAT + 3] = s % FEAT
    return jnp.asarray(f, BF16)


def _pad_to(a, n, axis):
    pad = [(0, 0)] * a.ndim
    pad[axis] = (0, n - a.shape[axis])
    return jnp.pad(a, pad)


def _even_layer(x2d, B, S, w_in, cmp_pos, cmp_w1, cmp_b1, cmp_w2, w_out, ln_g, ln_b, alpha):
    T, D = x2d.shape
    G, hd = NSA_KV_GROUPS, HEAD_DIM
    scale = hd ** -0.5
    wq, wkv, wgate, wnz, wmq, wmk, wmv, wmz = jnp.split(w_in, np.cumsum(EVEN_SPLITS)[:-1].tolist(), axis=1)
    wkc, wvc, wks, wvs, wkw, wvw = jnp.split(wkv, 6, axis=1)
    w_tok = jnp.concatenate([wkc, wvc, wks, wkw, wmk], axis=1).astype(BF16)
    w_tr = jnp.concatenate([wq * scale, wvs, wvw, wmq * scale, wmv, wnz, wmz,
                            _pad_to(wgate, GATE_ROWS, 1)], axis=1).T.astype(BF16)
    (kcmp, vcmp, ksel, kwin, mk, nq_t, vsel_t, vwin_t, mq_t, mv_t, nz_t, mz_t, gate_t) = _proj(
        x2d, w_tok, (NSA_KV_W,) * 4 + (MOBA_W,), (BF16,) * 5,
        w_tr, (NSA_W, NSA_KV_W, NSA_KV_W, MOBA_W, MOBA_W, NSA_W, MOBA_W, GATE_ROWS),
        (BF16,) * 7 + (F32,))

    L, d, HID = NSA_CMP_BLOCK, NSA_CMP_STRIDE, NSA_CMP_HIDDEN
    ncmp = (S - L) // d + 1
    NC = S // d
    eye = jnp.eye(G, dtype=F32)
    w1r = cmp_w1.reshape(2, 2, d, hd, HID)
    w1p = jnp.einsum('ialdj,gh->ialgdhj', w1r, eye).reshape(2, 2, d * G * hd, G * HID).astype(BF16)
    posp = jnp.broadcast_to(cmp_pos.reshape(2, 2, d, 1, hd), (2, 2, d, G, hd)).reshape(2, 2, 1, d * G * hd)
    b1p = jnp.tile(cmp_b1, (1, G)).reshape(2, 1, G * HID)
    w2p = jnp.einsum('ijd,gh->igjhd', _pad_to(cmp_w2, LANES, 2), eye).reshape(2, G * HID, G * LANES).astype(BF16)
    cfeat = np.zeros((NC, G, LANES), np.float32)
    cfeat[:, :, hd + 0] = 1.0
    cfeat[:, :, hd + 1] = 1.0
    cfeat[:, :, hd + 2] = (np.arange(NC) * d)[:, None]
    cfeat[:, :, hd + 3] = L - 1
    kvc = _cmp_mlp(kcmp.reshape(B, NC, d * NSA_KV_W), vcmp.reshape(B, NC, d * NSA_KV_W),
                   w1p, posp, b1p, w2p, jnp.asarray(cfeat.reshape(NC, G * LANES)), ncmp)

    ns = S // NSA_SEL_BLOCK
    cmp_start = np.arange(NC) * d
    sel_start = np.arange(FEAT) * NSA_SEL_BLOCK
    ovt = ((cmp_start[None, :] <= sel_start[:, None] + NSA_SEL_BLOCK - 1)
           & (cmp_start[None, :] + L - 1 >= sel_start[:, None])
           & (np.arange(NC)[None, :] < ncmp) & (np.arange(FEAT)[:, None] < ns))
    o_cmp, bias_sel = _cmp_attn(nq_t, kvc, jnp.asarray(ovt, BF16), B=B, S=S)

    feats_sel = _key_features(S, NSA_SEL_BLOCK)
    o_sel = _flash(nq_t, bias_sel, ksel, feats_sel, vsel_t,
                   B=B, S=S, units=G, tq=TQ, nh=NSA_HG, pair=False, window=None)
    o_win = _flash(nq_t, jnp.zeros_like(bias_sel), kwin, feats_sel, vwin_t,
                   B=B, S=S, units=G, tq=TQ, nh=NSA_HG, pair=False, window=NSA_WINDOW)

    avg = np.zeros((MOBA_NBP, S), np.float32)
    avg[np.arange(S) // MOBA_BLOCK, np.arange(S)] = 1.0 / MOBA_BLOCK
    bias_m = _moba_gate(mq_t, mk, jnp.asarray(avg, BF16), B=B, S=S)
    o_m = _flash(mq_t, bias_m, mk, _key_features(S, MOBA_BLOCK), mv_t,
                 B=B, S=S, units=MOBA_HEADS // 2, tq=TQM, nh=2, pair=True, window=None)

    n = np.arange(NSA_W)[:, None]
    c = np.arange(GATE_ROWS)[None, :]
    expand = jnp.asarray(np.stack([(c == 3 * (n // hd) + br) for br in range(3)]), BF16)
    return _even_out(x2d, o_cmp, o_sel, o_win, o_m, gate_t, nz_t, mz_t, expand, w_out.astype(BF16),
                     ln_g.reshape(1, D), ln_b.reshape(1, D), alpha, B=B, S=S)


def _odd_layer(x2d, B, S, w_in, gate_w2, gate_b, gn_g, w_out, ln_g, ln_b, alpha):
    T, D = x2d.shape
    H, dk, dv, C = GLA_HEADS, GLA_DK, GLA_DV, GLA_CHUNK
    wq, wk, wv, wg, wz = jnp.split(w_in, np.cumsum(ODD_SPLITS)[:-1].tolist(), axis=1)
    w_perm = jnp.concatenate([wq, wk, wv, wz, _pad_to(wg, LANES, 1)], axis=1).astype(BF16)
    q, k, v, z, glr = _proj(x2d, w_perm, (H * dk, H * dk, H * dv, H * dv, LANES),
                            (BF16, BF16, BF16, BF16, F32))
    w2p = _pad_to(gate_w2, LANES, 0)
    nch = min(GLA_STEP_CHUNKS, S // C)
    tril = jnp.asarray(np.kron(np.eye(nch), np.tril(np.ones((C, C)))), BF16)
    o = _gla(q, k, v, glr, w2p, gate_b.reshape(1, H * dk), tril, B=B, S=S)
    return _odd_out(x2d, o, z, gn_g.reshape(1, H * dv), w_out.astype(BF16),
                    ln_g.reshape(1, D), ln_b.reshape(1, D), alpha)


def kernel(x, ev_w_in, ev_cmp_pos, ev_cmp_w1, ev_cmp_b1, ev_cmp_w2, ev_w_out, ev_ln_g, ev_ln_b,
           od_w_in, od_gate_w2, od_gate_b, od_gn_g, od_w_out, od_ln_g, od_ln_b):
    B, S, D = x.shape
    depth = ev_w_in.shape[0] + od_w_in.shape[0]
    alpha = (2.0 * depth) ** 0.25
    h = x.reshape(B * S, D)
    for layer in range(depth):
        i = layer // 2
        if layer % 2 == 0:
            h = _even_layer(h, B, S, ev_w_in[i], ev_cmp_pos[i], ev_cmp_w1[i], ev_cmp_b1[i], ev_cmp_w2[i],
                            ev_w_out[i], ev_ln_g[i], ev_ln_b[i], alpha)
        else:
            h = _odd_layer(h, B, S, od_w_in[i], od_gate_w2[i], od_gate_b[i], od_gn_g[i],
                           od_w_out[i], od_ln_g[i], od_ln_b[i], alpha)
    return h.reshape(B, S, D)
```

```python
import functools

import numpy as np
import jax
import jax.numpy as jnp
from jax import lax
from jax.experimental import pallas as pl
from jax.experimental.pallas import tpu as pltpu

F32 = jnp.float32
BF16 = jnp.bfloat16

HEAD_DIM = 64
NSA_HEADS = 8
NSA_KV_GROUPS = 2
NSA_HG = NSA_HEADS // NSA_KV_GROUPS
NSA_CMP_BLOCK = 32
NSA_CMP_STRIDE = 16
NSA_CMP_HIDDEN = 128
NSA_SEL_BLOCK = 64
NSA_TOP_N = 16
NSA_WINDOW = 512
MOBA_HEADS = 8
MOBA_BLOCK = 256
MOBA_TOP_K = 3
GLA_HEADS = 4
GLA_DK = 128
GLA_DV = 256
GLA_GATE_RANK = 16
GLA_TAU = 16.0
GLA_CHUNK = 64
GLA_SUB = 16
GLA_STEP_CHUNKS = 8
GLA_FACTOR_MAX_DECAY = 60.0
LN_EPS = 1e-5
NEG = -1e30
FORCE_BONUS = 1e4

NSA_W = NSA_HEADS * HEAD_DIM
NSA_KV_W = NSA_KV_GROUPS * HEAD_DIM
MOBA_W = MOBA_HEADS * HEAD_DIM
EVEN_SPLITS = (NSA_W, 6 * NSA_KV_W, 3 * NSA_HEADS, NSA_W, MOBA_W, MOBA_W, MOBA_W, MOBA_W)
ODD_SPLITS = (GLA_HEADS * GLA_DK, GLA_HEADS * GLA_DK, GLA_HEADS * GLA_DV, GLA_GATE_RANK, GLA_HEADS * GLA_DV)

LANES = 128
SUBLANES = 8
FEAT = 64
ALIBI_ROWS = 16
GATE_ROWS = 32
MOBA_NBP = 16
TQ = 512
TQM = 512
TK = 512
CG = 256
MIN_WEIGHT_SUM = 1e-25
VROWS = HEAD_DIM + 16
VMEM_LIMIT = 48 * 1024 * 1024


def _dot(a, b):
    return jnp.dot(a, b, preferred_element_type=F32)


def _dot_nt(a, b):
    return lax.dot_general(a, b, (((1,), (1,)), ((), ())), preferred_element_type=F32)


def _dot_tn(a, b):
    return lax.dot_general(a, b, (((0,), (0,)), ((), ())), preferred_element_type=F32)


def _hilo(a):
    hi = a.astype(BF16)
    lo = (a - hi.astype(F32)).astype(BF16)
    return hi, lo


def _pow2_neg(n):
    return lax.bitcast_convert_type(lax.shift_left(127 - n, 23), F32)


def _sigmoid(x):
    return 1.0 / (1.0 + jnp.exp(-x))


def _layer_norm(r, g, b):
    mu = jnp.mean(r, axis=-1, keepdims=True)
    d = r - mu
    var = jnp.mean(d * d, axis=-1, keepdims=True)
    return d * lax.rsqrt(var + LN_EPS) * g + b


def _alibi_query_rows(t0, tq, head0, R):
    r = lax.broadcasted_iota(jnp.int32, (ALIBI_ROWS, R), 0)
    col = lax.broadcasted_iota(jnp.int32, (ALIBI_ROWS, R), 1)
    t = t0 + col % tq
    slope = _pow2_neg(head0 + col // tq + 1)
    feats = jnp.where(r == 0, -FEAT * slope * (t // FEAT).astype(F32),
                      jnp.where(r == 1, -slope * (t % FEAT).astype(F32), jnp.where(r < 4, slope, 0.0)))
    return jnp.concatenate([feats.astype(BF16), jnp.zeros((FEAT - ALIBI_ROWS, R), BF16)], axis=0)


def _ranks(sc_ref, n_real):
    n_rows, width = sc_ref.shape
    out = []
    for v in range(n_rows // SUBLANES):
        lo = v * SUBLANES
        sc = sc_ref[lo:lo + SUBLANES, :]
        idx = lo + lax.broadcasted_iota(jnp.int32, (SUBLANES, width), 0)
        rank = jnp.zeros((SUBLANES, width), jnp.int32)
        for jp in range(n_real):
            r = sc_ref[pl.ds(jp, 1), :]
            if jp < lo:
                beats = r >= sc
            elif jp >= lo + SUBLANES - 1:
                beats = r > sc
            else:
                beats = (r > sc) | ((r == sc) & (idx > jp))
            rank = rank + jnp.where(beats, 1, 0)
        out.append(rank)
    return jnp.concatenate(out, axis=0)


def _proj_kernel(*refs, tok_splits, tr_splits):
    x_ref, wt_ref = refs[0], refs[1]
    outs = list(refs[3:] if tr_splits else refs[2:])
    x = x_ref[...]
    xb = x.astype(BF16)
    off = 0
    for n in tok_splits:
        o_ref = outs.pop(0)
        o_ref[...] = _dot(xb, wt_ref[:, off:off + n]).astype(o_ref.dtype)
        off += n
    if tr_splits:
        wtr_ref = refs[2]
        xt = x.T.astype(BF16)
        off = 0
        for n in tr_splits:
            o_ref = outs.pop(0)
            o_ref[...] = _dot(wtr_ref[off:off + n, :], xt).astype(o_ref.dtype)
            off += n


def _proj(x2d, w_tok, tok_splits, tok_dtypes, w_tr=None, tr_splits=(), tr_dtypes=(), tm=512):
    T, D = x2d.shape
    assert T % tm == 0 and sum(tok_splits) == w_tok.shape[1]
    in_specs = [pl.BlockSpec((tm, D), lambda i: (i, 0)),
                pl.BlockSpec(w_tok.shape, lambda i: (0, 0))]
    args = [x2d, w_tok]
    if tr_splits:
        assert sum(tr_splits) == w_tr.shape[0]
        in_specs.append(pl.BlockSpec(w_tr.shape, lambda i: (0, 0)))
        args.append(w_tr)
    out_specs = ([pl.BlockSpec((tm, n), lambda i: (i, 0)) for n in tok_splits]
                 + [pl.BlockSpec((n, tm), lambda i: (0, i)) for n in tr_splits])
    out_shape = ([jax.ShapeDtypeStruct((T, n), dt) for n, dt in zip(tok_splits, tok_dtypes)]
                 + [jax.ShapeDtypeStruct((n, T), dt) for n, dt in zip(tr_splits, tr_dtypes)])
    return pl.pallas_call(
        functools.partial(_proj_kernel, tok_splits=tuple(tok_splits), tr_splits=tuple(tr_splits)),
        grid=(T // tm,), in_specs=in_specs, out_specs=out_specs, out_shape=out_shape,
        compiler_params=pltpu.CompilerParams(dimension_semantics=("parallel",),
                                             vmem_limit_bytes=VMEM_LIMIT),
        name="proj",
    )(*args)


def _cmp_mlp_kernel(kx_ref, vx_ref, w1_ref, pos_ref, b1_ref, w2_ref, cf_ref, o_ref, *, ncmp):
    for i, x_ref in enumerate((kx_ref, vx_ref)):
        x = x_ref[...]
        nc, cw = x.shape
        a = _dot(x, w1_ref[i, 0])
        bm = _dot(x, w1_ref[i, 1])
        c = b1_ref[i]
        for half in range(2):
            phi, plo = _hilo(jnp.broadcast_to(pos_ref[i, half], (8, cw)))
            c = c + (_dot(phi, w1_ref[i, half]) + _dot(plo, w1_ref[i, half]))[0:1]
        hid = a + pltpu.roll(bm, nc - 1, axis=0) + c
        hid = hid * _sigmoid(hid)
        row = lax.broadcasted_iota(jnp.int32, hid.shape, 0)
        hid = jnp.where(row < ncmp, hid, 0.0)
        out = _dot(hid.astype(BF16), w2_ref[i])
        if i == 0:
            out = out + cf_ref[...]
        o_ref[i] = out.astype(o_ref.dtype)


def _cmp_mlp(kx, vx, w1p, posp, b1p, w2p, cfeat, ncmp):
    B, NC, CW = kx.shape
    OW = w2p.shape[-1]
    full = lambda a: pl.BlockSpec(a.shape, lambda b: (0,) * a.ndim)
    return pl.pallas_call(
        functools.partial(_cmp_mlp_kernel, ncmp=ncmp),
        grid=(B,),
        in_specs=[pl.BlockSpec((None, NC, CW), lambda b: (b, 0, 0)),
                  pl.BlockSpec((None, NC, CW), lambda b: (b, 0, 0)),
                  full(w1p), full(posp), full(b1p), full(w2p), full(cfeat)],
        out_specs=pl.BlockSpec((2, None, NC, OW), lambda b: (0, b, 0, 0)),
        out_shape=jax.ShapeDtypeStruct((2, B, NC, OW), BF16),
        compiler_params=pltpu.CompilerParams(dimension_semantics=("parallel",),
                                             vmem_limit_bytes=VMEM_LIMIT),
        name="cmp_mlp",
    )(kx, vx, w1p, posp, b1p, w2p, cfeat)


def _cmp_attn_kernel(q_ref, kc_ref, vc_ref, ovt_ref, o_ref, bias_ref, sc_ref, *, tq, nh, ns, ntop):
    g = pl.program_id(1)
    R = nh * tq
    nc = kc_ref.shape[0]
    nq = q_ref.shape[1] // tq

    def q_tile(qi, carry):
        toks = pl.ds(pl.multiple_of(qi * tq, tq), tq)
        qcat = jnp.concatenate([q_ref[hh * HEAD_DIM:(hh + 1) * HEAD_DIM, toks] for hh in range(nh)], axis=1)
        qa = jnp.concatenate([qcat, _alibi_query_rows(qi * tq, tq, g * nh, R)], axis=0)
        s = _dot(kc_ref[...], qa)
        n = lax.broadcasted_iota(jnp.int32, (nc, R), 0)
        t_row = qi * tq + lax.broadcasted_iota(jnp.int32, (1, R), 1) % tq
        last = (t_row - (NSA_CMP_BLOCK - 1)) // NSA_CMP_STRIDE
        s = jnp.where(n <= last, s, NEG)
        e = jnp.exp(s - jnp.max(s, axis=0, keepdims=True))
        inv = jnp.where(last >= 0, 1.0 / jnp.sum(e, axis=0, keepdims=True), 0.0)
        p = e * inv
        o_ref[:, pl.ds(pl.multiple_of(qi * R, R), R)] = _dot_tn(vc_ref[:, 0:HEAD_DIM],
                                                                 p.astype(BF16)).astype(o_ref.dtype)

        psum = p[:, 0:tq]
        for hh in range(1, nh):
            psum = psum + p[:, hh * tq:(hh + 1) * tq]
        phi, plo = _hilo(psum)
        imp = _dot(ovt_ref[...], phi) + _dot(ovt_ref[...], plo)

        j = lax.broadcasted_iota(jnp.int32, (FEAT, tq), 0)
        cur = (qi * tq + lax.broadcasted_iota(jnp.int32, (FEAT, tq), 1)) // NSA_SEL_BLOCK
        forced = jnp.where((j == 0) | (j == cur) | (j == cur - 1), FORCE_BONUS, 0.0)
        sc = jnp.where(j <= cur, imp + forced, NEG)
        sc_ref[...] = sc
        sel = (_ranks(sc_ref, ns) < ntop) & (j <= cur)
        bias_ref[:, toks] = jnp.where(sel, 0.0, NEG).astype(bias_ref.dtype)
        return carry

    lax.fori_loop(0, nq, q_tile, 0)


def _cmp_attn(nq_t, kvc, ovt, *, B, S):
    tq, nh, G = TQ, NSA_HG, NSA_KV_GROUPS
    R = nh * tq
    nq = S // tq
    NC = kvc.shape[2]
    ns = S // NSA_SEL_BLOCK
    assert ns <= FEAT
    ntop = min(NSA_TOP_N, ns)
    return pl.pallas_call(
        functools.partial(_cmp_attn_kernel, tq=tq, nh=nh, ns=ns, ntop=ntop),
        grid=(B, G),
        in_specs=[pl.BlockSpec((nh * HEAD_DIM, S), lambda b, g: (g, b)),
                  pl.BlockSpec((None, None, NC, LANES), lambda b, g: (0, b, 0, g)),
                  pl.BlockSpec((None, None, NC, LANES), lambda b, g: (1, b, 0, g)),
                  pl.BlockSpec(ovt.shape, lambda b, g: (0, 0))],
        out_specs=[pl.BlockSpec((None, None, HEAD_DIM, S * nh), lambda b, g: (b, g, 0, 0)),
                   pl.BlockSpec((None, None, FEAT, S), lambda b, g: (b, g, 0, 0))],
        out_shape=[jax.ShapeDtypeStruct((B, G, HEAD_DIM, S * nh), BF16),
                   jax.ShapeDtypeStruct((B, G, FEAT, S), BF16)],
        scratch_shapes=[pltpu.VMEM((FEAT, tq), F32)],
        compiler_params=pltpu.CompilerParams(dimension_semantics=("parallel", "parallel"),
                                             vmem_limit_bytes=VMEM_LIMIT),
        name="cmp_attn",
    )(nq_t, kvc, kvc, ovt)


def _moba_gate_kernel(q_ref, k_ref, a_ref, bias_ref, km_ref, sc_ref, *, tq, nb, ntop):
    h = pl.program_id(1)
    qi = pl.program_id(2)

    @pl.when(qi == 0)
    def _means():
        km_ref[...] = _dot(a_ref[...], k_ref[...])

    nbp = km_ref.shape[0]
    lane = lax.broadcasted_iota(jnp.int32, (nbp, LANES), 1)
    km = jnp.where(lane // HEAD_DIM == h % 2, km_ref[...], 0.0)
    khi, klo = _hilo(km)
    q = q_ref[...]
    qq = jnp.concatenate([q, q], axis=0)
    g_t = _dot(khi, qq) + _dot(klo, qq)
    n = lax.broadcasted_iota(jnp.int32, (nbp, tq), 0)
    cur = (qi * tq + lax.broadcasted_iota(jnp.int32, (nbp, tq), 1)) // MOBA_BLOCK
    sc = jnp.where(n < cur, g_t, NEG)
    sc_ref[...] = sc
    sel = ((_ranks(sc_ref, nb) < ntop) & (n < cur)) | (n == cur)
    bias_ref[0:nbp, :] = jnp.where(sel, 0.0, NEG).astype(bias_ref.dtype)
    bias_ref[nbp:FEAT, :] = jnp.zeros((FEAT - nbp, tq), bias_ref.dtype)


def _moba_gate(mq_t, mk, avg, *, B, S):
    tq, H = S, MOBA_HEADS
    nq = S // tq
    nb = S // MOBA_BLOCK
    nbp = avg.shape[0]
    assert nb <= nbp
    ntop = min(MOBA_TOP_K, nb)
    return pl.pallas_call(
        functools.partial(_moba_gate_kernel, tq=tq, nb=nb, ntop=ntop),
        grid=(B, H, nq),
        in_specs=[pl.BlockSpec((HEAD_DIM, tq), lambda b, h, i: (h, b * nq + i)),
                  pl.BlockSpec((S, LANES), lambda b, h, i: (b, h // 2)),
                  pl.BlockSpec(avg.shape, lambda b, h, i: (0, 0))],
        out_specs=pl.BlockSpec((None, None, FEAT, tq), lambda b, h, i: (b, h, 0, i)),
        out_shape=jax.ShapeDtypeStruct((B, H, FEAT, S), BF16),
        scratch_shapes=[pltpu.VMEM((nbp, LANES), F32), pltpu.VMEM((nbp, tq), F32)],
        compiler_params=pltpu.CompilerParams(dimension_semantics=("parallel", "parallel", "arbitrary"),
                                             vmem_limit_bytes=VMEM_LIMIT),
        name="moba_gate",
    )(mq_t, mk, avg)


def _flash_q_tile(qi, q_ref, bias_ref, mask_ref, o_ref, ka_ref, vta_ref, kn_ref, qa_ref, bnd_ref, m_ref,
                  acc_ref, alpha_ref, p_ref, *, tq, tk, nh, pair, window):
    u = pl.program_id(1)
    R = nh * tq
    q0 = pl.multiple_of(qi * tq, tq)
    toks = pl.ds(q0, tq)

    qcat = jnp.concatenate([q_ref[hh * HEAD_DIM:(hh + 1) * HEAD_DIM, toks] for hh in range(nh)], axis=1)
    qf = qcat.astype(F32)
    qq = jnp.concatenate([qf, qf], axis=0)
    rowi = lax.broadcasted_iota(jnp.int32, (2 * HEAD_DIM, R), 0)
    coli = lax.broadcasted_iota(jnp.int32, (2 * HEAD_DIM, R), 1)
    half = coli // tq if pair else u % 2
    qa_ref[0:2 * HEAD_DIM, :] = jnp.where(rowi // HEAD_DIM == half, qq, 0.0).astype(BF16)
    qa_ref[2 * HEAD_DIM:2 * HEAD_DIM + FEAT, :] = jnp.concatenate(
        [bias_ref[hh if pair else 0, :, toks] for hh in range(nh)], axis=1)
    qa_ref[2 * HEAD_DIM + FEAT:2 * LANES, :] = _alibi_query_rows(q0, tq, u * nh, R)
    kn = [jnp.concatenate([kn_ref[hf:hf + 1, :]] * (R // LANES), axis=1) for hf in range(2)]
    half_row = lax.broadcasted_iota(jnp.int32, (1, R), 1) // tq if pair else u % 2
    qn = jnp.sqrt(jnp.sum(qf * qf, axis=0, keepdims=True))
    bnd_ref[...] = jnp.broadcast_to(qn * jnp.where(half_row == 0, kn[0], kn[1]), bnd_ref.shape)

    ncg = R // CG
    kd = (q0 + tq - 1) // tk

    def sweep(running_max):
        acc_ref[...] = jnp.zeros(acc_ref.shape, F32)
        if running_max:
            m_ref[...] = jnp.full(m_ref.shape, NEG, F32)

        def values(ki):
            for c in range(ncg):
                cols = pl.ds(c * CG, CG)
                v = (c * CG) // tq if pair else 0
                pv = _dot(vta_ref[v, ki], p_ref[:, cols])
                if running_max:
                    acc_ref[:, cols] = alpha_ref[0:1, cols] * acc_ref[:, cols] + pv
                else:
                    acc_ref[:, cols] = acc_ref[:, cols] + pv

        def scores(ki, masked, prev=None):
            keys = pl.ds(pl.multiple_of(ki * tk, tk), tk)
            ka = ka_ref[keys, :]
            s_groups = [_dot(ka, qa_ref[:, pl.ds(c * CG, CG)]) for c in range(ncg)]
            if prev is not None:
                values(prev)
            for c, s in enumerate(s_groups):
                cols = pl.ds(c * CG, CG)
                if masked:
                    s = s + mask_ref[(q0 + (c * CG) % tq - ki * tk) // CG]
                if running_max:
                    m_prev = m_ref[0:1, cols]
                    m_new = jnp.maximum(m_prev, jnp.max(s, axis=0, keepdims=True))
                    alpha_ref[:, cols] = jnp.broadcast_to(jnp.exp(m_prev - m_new), (alpha_ref.shape[0], CG))
                    m_ref[:, cols] = jnp.broadcast_to(m_new, (m_ref.shape[0], CG))
                    p_ref[:, cols] = jnp.exp(s - m_new).astype(BF16)
                else:
                    p_ref[:, cols] = jnp.exp(s - bnd_ref[0:1, cols]).astype(BF16)

        def steady(lo, hi, masked):
            def pair_body(j, carry):
                ki = lo + 2 * j
                scores(ki, masked, prev=ki - 1)
                scores(ki + 1, masked, prev=ki)
                return carry
            lax.fori_loop(0, (hi - lo) // 2, pair_body, 0)

            @pl.when((hi - lo) % 2 == 1)
            def _odd_one():
                scores(hi - 1, masked, prev=hi - 2)

        if window is None:
            @pl.when(kd == 0)
            def _only_diagonal():
                scores(0, True)
                values(0)

            @pl.when(kd > 0)
            def _past_then_diagonal():
                scores(0, False)
                steady(1, kd, False)
                scores(kd, True, prev=kd - 1)
                values(kd)
        else:
            k_lo = jnp.maximum(q0 - window + 1, 0) // tk
            scores(k_lo, True)
            steady(k_lo + 1, kd + 1, True)
            values(kd)

    def emit():
        acc = acc_ref[...]
        out = (acc[0:HEAD_DIM] / acc[HEAD_DIM:HEAD_DIM + 1]).astype(o_ref.dtype)
        if pair:
            for hh in range(nh):
                o_ref[hh, :, toks] = out[:, hh * tq:(hh + 1) * tq]
        else:
            o_ref[0, :, pl.ds(pl.multiple_of(qi * R, R), R)] = out

    sweep(False)
    trusted = jnp.min(acc_ref[HEAD_DIM:HEAD_DIM + 1, :]) >= MIN_WEIGHT_SUM

    @pl.when(trusted)
    def _done():
        emit()

    @pl.when(jnp.logical_not(trusted))
    def _redo_with_running_max():
        sweep(True)
        emit()


def _flash_kernel(q_ref, bias_ref, k_ref, kf_ref, vt_ref, mask_ref, o_ref, ka_ref, vta_ref, kn_ref, qa_ref,
                  bnd_ref, m_ref, acc_ref, alpha_ref, p_ref, *, tq, tk, nh, pair, window):
    S = k_ref.shape[0]
    k = k_ref[...]
    ka_ref[:, 0:LANES] = k
    ka_ref[:, LANES:2 * LANES] = kf_ref[...]
    k2 = jnp.square(k.astype(F32))
    lane = lax.broadcasted_iota(jnp.int32, k2.shape, 1)
    for hf in range(2):
        n2 = jnp.sum(jnp.where(lane // HEAD_DIM == hf, k2, 0.0), axis=1, keepdims=True)
        kn_ref[hf:hf + 1, :] = jnp.broadcast_to(jnp.sqrt(jnp.max(n2, axis=0, keepdims=True)), (1, LANES))
    ones_row = jnp.where(lax.broadcasted_iota(jnp.int32, (VROWS - HEAD_DIM, tk), 0) == 0, 1.0, 0.0)
    for v in range(nh if pair else 1):
        for jt in range(S // tk):
            vta_ref[v, jt, 0:HEAD_DIM, :] = vt_ref[v * HEAD_DIM:(v + 1) * HEAD_DIM, jt * tk:(jt + 1) * tk]
            vta_ref[v, jt, HEAD_DIM:VROWS, :] = ones_row.astype(BF16)

    def q_tile(qi, carry):
        _flash_q_tile(qi, q_ref, bias_ref, mask_ref, o_ref, ka_ref, vta_ref, kn_ref, qa_ref, bnd_ref, m_ref,
                      acc_ref, alpha_ref, p_ref, tq=tq, tk=tk, nh=nh, pair=pair, window=window)
        return carry

    lax.fori_loop(0, S // tq, q_tile, 0)


def _mask_table(tk, window):
    n_delta = (tk + (window or 0)) // CG
    d = np.arange(n_delta)[:, None, None] * CG + np.arange(CG)[None, None, :] - np.arange(tk)[None, :, None]
    ok = (d >= 0) & ((d < window) if window else True)
    return jnp.asarray(np.where(ok, 0.0, NEG), F32)


def _flash(q_t, bias_t, k, kfeat, v_t, *, B, S, units, tq, nh, pair, window):
    tk = TK
    R = nh * tq
    nq = S // tq
    nv = nh if pair else 1
    nb = nh if pair else 1
    assert tk % tq == 0 and S % tk == 0 and tq % CG == 0 and tk % CG == 0 and (window or 0) % CG == 0
    mask = _mask_table(tk, window)
    kcol = (lambda u: u) if pair else (lambda u: u // 2)
    out_w = tq if pair else R
    out_shape = (B, units * nb, HEAD_DIM, S if pair else S * nh)
    return pl.pallas_call(
        functools.partial(_flash_kernel, tq=tq, tk=tk, nh=nh, pair=pair, window=window),
        grid=(B, units),
        in_specs=[pl.BlockSpec((nh * HEAD_DIM, S), lambda b, u: (u, b)),
                  pl.BlockSpec((None, nb, FEAT, S), lambda b, u: (b, u, 0, 0)),
                  pl.BlockSpec((S, LANES), lambda b, u: (b, kcol(u))),
                  pl.BlockSpec((S, LANES), lambda b, u: (0, 0)),
                  pl.BlockSpec((nv * HEAD_DIM, S), lambda b, u: (u, b)),
                  pl.BlockSpec(mask.shape, lambda b, u: (0, 0, 0))],
        out_specs=pl.BlockSpec((None, nb, HEAD_DIM, out_w * nq), lambda b, u: (b, u, 0, 0)),
        out_shape=jax.ShapeDtypeStruct(out_shape, BF16),
        scratch_shapes=[pltpu.VMEM((S, 2 * LANES), BF16),
                        pltpu.VMEM((nv, S // tk, VROWS, tk), BF16),
                        pltpu.VMEM((8, LANES), F32),
                        pltpu.VMEM((2 * LANES, R), BF16),
                        pltpu.VMEM((8, R), F32),
                        pltpu.VMEM((8, R), F32),
                        pltpu.VMEM((VROWS, R), F32),
                        pltpu.VMEM((8, R), F32),
                        pltpu.VMEM((tk, R), BF16)],
        compiler_params=pltpu.CompilerParams(dimension_semantics=("parallel", "parallel"),
                                             vmem_limit_bytes=VMEM_LIMIT),
        name="flash",
    )(q_t, bias_t, k, kfeat, v_t, mask)


def _even_out_kernel(x_ref, ocmp_ref, osel_ref, owin_ref, om_ref, gate_ref, nz_ref, mz_ref,
                     e_ref, w_ref, g_ref, b_ref, o_ref, *, alpha, tm):
    G, nh = NSA_KV_GROUPS, NSA_HG
    sig = _sigmoid(gate_ref[...])
    shi, slo = _hilo(sig)

    def heads_t(ref):
        return jnp.concatenate([ref[g, :, hh * tm:(hh + 1) * tm] for g in range(G) for hh in range(nh)],
                               axis=0).astype(F32)

    o_nsa = None
    for br, ref in enumerate((ocmp_ref, osel_ref, owin_ref)):
        gb = _dot(e_ref[br], shi) + _dot(e_ref[br], slo)
        term = gb * heads_t(ref)
        o_nsa = term if o_nsa is None else o_nsa + term
    nz = nz_ref[...].astype(F32)
    mz = mz_ref[...].astype(F32)
    o_m = jnp.concatenate([om_ref[h] for h in range(MOBA_HEADS)], axis=0).astype(F32)
    a_t = jnp.concatenate([o_nsa * (nz * _sigmoid(nz)), o_m * (mz * _sigmoid(mz))], axis=0)
    y = _dot_tn(a_t.astype(BF16), w_ref[...])
    o_ref[...] = _layer_norm(alpha * x_ref[...] + y, g_ref[...], b_ref[...])


def _even_out(x2d, ocmp, osel, owin, om, gate_t, nz_t, mz_t, expand, w_out, ln_g, ln_b, alpha, *, B, S):
    T, D = x2d.shape
    tm = TQ
    nq = S // tm
    G, nh, H = NSA_KV_GROUPS, NSA_HG, MOBA_HEADS
    nsa = pl.BlockSpec((None, G, HEAD_DIM, nh * tm), lambda i: (i // nq, 0, 0, i % nq))
    feat = lambda n: pl.BlockSpec((n, tm), lambda i: (0, i))
    full = lambda a: pl.BlockSpec(a.shape, lambda i: (0,) * a.ndim)
    return pl.pallas_call(
        functools.partial(_even_out_kernel, alpha=alpha, tm=tm),
        grid=(T // tm,),
        in_specs=[pl.BlockSpec((tm, D), lambda i: (i, 0)), nsa, nsa, nsa,
                  pl.BlockSpec((None, H, HEAD_DIM, tm), lambda i: (i // nq, 0, 0, i % nq)),
                  feat(GATE_ROWS), feat(NSA_W), feat(MOBA_W),
                  full(expand), full(w_out), full(ln_g), full(ln_b)],
        out_specs=pl.BlockSpec((tm, D), lambda i: (i, 0)),
        out_shape=jax.ShapeDtypeStruct((T, D), F32),
        compiler_params=pltpu.CompilerParams(dimension_semantics=("parallel",),
                                             vmem_limit_bytes=VMEM_LIMIT),
        name="even_out",
    )(x2d, ocmp, osel, owin, om, gate_t, nz_t, mz_t, expand, w_out, ln_g, ln_b)


def _gla_head_exact(row0, h, q_ref, v_ref, o_ref, st_ref, b_ref, kf_ref, scale):
    C, dk, dv, SUB = GLA_CHUNK, GLA_DK, GLA_DV, GLA_SUB
    rows = lax.broadcasted_iota(jnp.int32, (C, dk), 0)
    sub_i = lax.broadcasted_iota(jnp.int32, (SUB, LANES), 0)
    lane = lax.broadcasted_iota(jnp.int32, (SUB, LANES), 1)
    ones = jnp.ones((dk, LANES), BF16)
    sl = pl.ds(h * dk, dk)
    rs_c = pl.ds(row0, C)
    bh = b_ref[:, sl]
    kf = kf_ref[:, sl]
    qf = q_ref[rs_c, sl].astype(F32) * scale
    vh = v_ref[rs_c, pl.ds(h * dv, dv)]
    st = st_ref[h]
    inter = _dot_nt((qf * jnp.exp(bh)).astype(BF16), st.astype(BF16))
    bl = b_ref[pl.ds(C - 1, 1), sl]
    kd = kf * jnp.exp(bl - bh)

    def off_block(r0, r1, c0, c1):
        ref_row = b_ref[pl.ds(r0, 1), sl]
        qt = qf[r0:r1] * jnp.exp(bh[r0:r1] - ref_row)
        ek = jnp.where((rows >= c0) & (rows < c1), ref_row - bh, NEG)
        kt = kf * jnp.exp(ek)
        return _dot_nt(qt.astype(BF16), kt.astype(BF16))

    lower = off_block(2 * SUB, C, 0, 2 * SUB)
    blocks = [None, off_block(SUB, 2 * SUB, 0, SUB), lower[0:SUB],
              lower[SUB:2 * SUB] + off_block(3 * SUB, C, 2 * SUB, 3 * SUB)]
    pieces = []
    for blk in range(C // SUB):
        r0 = blk * SUB
        q_i = qf[r0:r0 + SUB]
        b_i = bh[r0:r0 + SUB]
        for jj in range(SUB):
            b_j = b_ref[pl.ds(r0 + jj, 1), sl]
            k_j = kf_ref[pl.ds(r0 + jj, 1), sl]
            pieces.append((q_i * k_j * jnp.exp(jnp.minimum(b_i - b_j, 0.0))).astype(BF16))
    sums = _dot(jnp.concatenate(pieces, axis=0), ones)
    for blk in range(C // SUB):
        r0 = blk * SUB
        diag = jnp.zeros((SUB, LANES), F32)
        for jj in range(SUB):
            rs = sums[(r0 + jj) * SUB:(r0 + jj + 1) * SUB]
            diag = jnp.where((lane == r0 + jj) & (sub_i >= jj), rs, diag)
        diag = diag[:, 0:C]
        blocks[blk] = diag if blocks[blk] is None else blocks[blk] + diag
    att = jnp.concatenate(blocks, axis=0)
    o_ref[rs_c, pl.ds(h * dv, dv)] = (inter + _dot(att.astype(BF16), vh)).astype(o_ref.dtype)
    st_ref[h] = st * jnp.exp(bl) + _dot_tn(vh, kd.astype(BF16))


def _gla_head_bounded(row0, h, q_ref, v_ref, o_ref, st_ref, b_ref, kf_ref, scale):
    C, dk, dv = GLA_CHUNK, GLA_DK, GLA_DV
    sl = pl.ds(h * dk, dk)
    rs_c = pl.ds(row0, C)
    bh = b_ref[rs_c, sl]
    kf = kf_ref[rs_c, sl]
    qf = q_ref[rs_c, sl].astype(F32) * scale
    vh = v_ref[rs_c, pl.ds(h * dv, dv)]
    st = st_ref[h]
    qs = (qf * jnp.exp(bh)).astype(BF16)
    kt = (kf * jnp.exp(-bh)).astype(BF16)
    res = _dot_nt(qs, jnp.concatenate([st.astype(BF16), kt], axis=0))
    i = lax.broadcasted_iota(jnp.int32, (C, C), 0)
    j = lax.broadcasted_iota(jnp.int32, (C, C), 1)
    att = jnp.where(i >= j, res[:, dv:dv + C], 0.0)
    o_ref[rs_c, pl.ds(h * dv, dv)] = (res[:, 0:dv] + _dot(att.astype(BF16), vh)).astype(o_ref.dtype)
    bl = b_ref[pl.ds(row0 + C - 1, 1), sl]
    kd = kf * jnp.exp(bl - bh)
    st_ref[h] = st * jnp.exp(bl) + _dot_tn(vh, kd.astype(BF16))


def _gla_kernel(q_ref, k_ref, v_ref, glr_ref, w2_ref, gb_ref, tril_ref, o_ref,
                st_ref, b_ref, kf_ref, bc_ref, kc_ref, *, scale):
    C, H = GLA_CHUNK, GLA_HEADS
    nch = q_ref.shape[0] // C

    @pl.when(pl.program_id(1) == 0)
    def _init():
        st_ref[...] = jnp.zeros(st_ref.shape, F32)

    ghi, glo = _hilo(glr_ref[...])
    whi, wlo = _hilo(w2_ref[...])
    pre = _dot(ghi, whi) + _dot(glo, whi) + _dot(ghi, wlo) + gb_ref[...]
    la = (jnp.minimum(pre, 0.0) - jnp.log(1.0 + jnp.exp(-jnp.abs(pre)))) * (1.0 / GLA_TAU)
    lhi, llo = _hilo(la)
    b_all = _dot(tril_ref[...], lhi) + _dot(tril_ref[...], llo)
    b_ref[...] = b_all
    kf_ref[...] = k_ref[...].astype(F32)
    decay = -b_all[C - 1:C, :]
    for c in range(1, nch):
        decay = jnp.maximum(decay, -b_all[c * C + C - 1:(c + 1) * C, :])
    bounded = jnp.max(decay) <= GLA_FACTOR_MAX_DECAY

    @pl.when(bounded)
    def _factored():
        for c in range(nch):
            for h in range(H):
                _gla_head_bounded(c * C, h, q_ref, v_ref, o_ref, st_ref, b_ref, kf_ref, scale)

    @pl.when(jnp.logical_not(bounded))
    def _exact():
        def chunk(c, carry):
            row0 = pl.multiple_of(c * C, C)
            bc_ref[...] = b_ref[pl.ds(row0, C), :]
            kc_ref[...] = kf_ref[pl.ds(row0, C), :]
            for h in range(H):
                _gla_head_exact(row0, h, q_ref, v_ref, o_ref, st_ref, bc_ref, kc_ref, scale)
            return carry
        lax.fori_loop(0, nch, chunk, 0)


def _gla(q, k, v, glr, w2p, gb, tril, *, B, S):
    C, H, dk, dv = GLA_CHUNK, GLA_HEADS, GLA_DK, GLA_DV
    rows = tril.shape[0]
    nchunk = S // rows
    tok = lambda n: pl.BlockSpec((rows, n), lambda b, c: (b * nchunk + c, 0))
    const = lambda shape: pl.BlockSpec(shape, lambda b, c: (0,) * len(shape))
    return pl.pallas_call(
        functools.partial(_gla_kernel, scale=float(dk) ** -0.5),
        grid=(B, nchunk),
        in_specs=[tok(H * dk), tok(H * dk), tok(H * dv), tok(LANES),
                  const(w2p.shape), const(gb.shape), const(tril.shape)],
        out_specs=tok(H * dv),
        out_shape=jax.ShapeDtypeStruct((B * S, H * dv), BF16),
        scratch_shapes=[pltpu.VMEM((H, dv, dk), F32),
                        pltpu.VMEM((rows, H * dk), F32),
                        pltpu.VMEM((rows, H * dk), F32),
                        pltpu.VMEM((C, H * dk), F32),
                        pltpu.VMEM((C, H * dk), F32)],
        compiler_params=pltpu.CompilerParams(dimension_semantics=("parallel", "arbitrary"),
                                             vmem_limit_bytes=VMEM_LIMIT),
        name="gla",
    )(q, k, v, glr, w2p, gb, tril)


def _odd_out_kernel(x_ref, o_ref_in, z_ref, gn_ref, w_ref, g_ref, b_ref, out_ref, *, alpha):
    H, dv = GLA_HEADS, GLA_DV
    parts = []
    for h in range(H):
        sl = pl.ds(h * dv, dv)
        oh = o_ref_in[:, sl].astype(F32)
        r = lax.rsqrt(jnp.mean(oh * oh, axis=-1, keepdims=True) + LN_EPS)
        z = z_ref[:, sl].astype(F32)
        parts.append((oh * r * gn_ref[:, sl] * (z * _sigmoid(z))).astype(BF16))
    y = _dot(jnp.concatenate(parts, axis=1), w_ref[...])
    out_ref[...] = _layer_norm(alpha * x_ref[...] + y, g_ref[...], b_ref[...])


def _odd_out(x2d, o, z, gn_g, w_out, ln_g, ln_b, alpha, tm=512):
    T, D = x2d.shape
    W = o.shape[1]
    row = lambda n: pl.BlockSpec((tm, n), lambda i: (i, 0))
    return pl.pallas_call(
        functools.partial(_odd_out_kernel, alpha=alpha),
        grid=(T // tm,),
        in_specs=[row(D), row(W), row(W),
                  pl.BlockSpec((1, W), lambda i: (0, 0)),
                  pl.BlockSpec((W, D), lambda i: (0, 0)),
                  pl.BlockSpec((1, D), lambda i: (0, 0)),
                  pl.BlockSpec((1, D), lambda i: (0, 0))],
        out_specs=row(D),
        out_shape=jax.ShapeDtypeStruct((T, D), F32),
        compiler_params=pltpu.CompilerParams(dimension_semantics=("parallel",),
                                             vmem_limit_bytes=VMEM_LIMIT),
        name="odd_out",
    )(x2d, o, z, gn_g, w_out, ln_g, ln_b)


def _key_features(S, block):
    s = np.arange(S)
    f = np.zeros((S, LANES), np.float32)
    f[s, s // block] = 1.0
    f[:, FEAT + 0] = 1.0
    f[:, FEAT + 1] = 1.0
    f[:, FEAT + 2] = FEAT * (s // FEAT)
    f[:, FEAT + 3] = s % FEAT
    return jnp.asarray(f, BF16)


def _pad_to(a, n, axis):
    pad = [(0, 0)] * a.ndim
    pad[axis] = (0, n - a.shape[axis])
    return jnp.pad(a, pad)


def _even_layer(x2d, B, S, w_in, cmp_pos, cmp_w1, cmp_b1, cmp_w2, w_out, ln_g, ln_b, alpha):
    T, D = x2d.shape
    G, hd = NSA_KV_GROUPS, HEAD_DIM
    scale = hd ** -0.5
    wq, wkv, wgate, wnz, wmq, wmk, wmv, wmz = jnp.split(w_in, np.cumsum(EVEN_SPLITS)[:-1].tolist(), axis=1)
    wkc, wvc, wks, wvs, wkw, wvw = jnp.split(wkv, 6, axis=1)
    w_tok = jnp.concatenate([wkc, wvc, wks, wkw, wmk], axis=1).astype(BF16)
    w_tr = jnp.concatenate([wq * scale, wvs, wvw, wmq * scale, wmv, wnz, wmz,
                            _pad_to(wgate, GATE_ROWS, 1)], axis=1).T.astype(BF16)
    (kcmp, vcmp, ksel, kwin, mk, nq_t, vsel_t, vwin_t, mq_t, mv_t, nz_t, mz_t, gate_t) = _proj(
        x2d, w_tok, (NSA_KV_W,) * 4 + (MOBA_W,), (BF16,) * 5,
        w_tr, (NSA_W, NSA_KV_W, NSA_KV_W, MOBA_W, MOBA_W, NSA_W, MOBA_W, GATE_ROWS),
        (BF16,) * 7 + (F32,))

    L, d, HID = NSA_CMP_BLOCK, NSA_CMP_STRIDE, NSA_CMP_HIDDEN
    ncmp = (S - L) // d + 1
    NC = S // d
    eye = jnp.eye(G, dtype=F32)
    w1r = cmp_w1.reshape(2, 2, d, hd, HID)
    w1p = jnp.einsum('ialdj,gh->ialgdhj', w1r, eye).reshape(2, 2, d * G * hd, G * HID).astype(BF16)
    posp = jnp.broadcast_to(cmp_pos.reshape(2, 2, d, 1, hd), (2, 2, d, G, hd)).reshape(2, 2, 1, d * G * hd)
    b1p = jnp.tile(cmp_b1, (1, G)).reshape(2, 1, G * HID)
    w2p = jnp.einsum('ijd,gh->igjhd', _pad_to(cmp_w2, LANES, 2), eye).reshape(2, G * HID, G * LANES).astype(BF16)
    cfeat = np.zeros((NC, G, LANES), np.float32)
    cfeat[:, :, hd + 0] = 1.0
    cfeat[:, :, hd + 1] = 1.0
    cfeat[:, :, hd + 2] = (np.arange(NC) * d)[:, None]
    cfeat[:, :, hd + 3] = L - 1
    kvc = _cmp_mlp(kcmp.reshape(B, NC, d * NSA_KV_W), vcmp.reshape(B, NC, d * NSA_KV_W),
                   w1p, posp, b1p, w2p, jnp.asarray(cfeat.reshape(NC, G * LANES)), ncmp)

    ns = S // NSA_SEL_BLOCK
    cmp_start = np.arange(NC) * d
    sel_start = np.arange(FEAT) * NSA_SEL_BLOCK
    ovt = ((cmp_start[None, :] <= sel_start[:, None] + NSA_SEL_BLOCK - 1)
           & (cmp_start[None, :] + L - 1 >= sel_start[:, None])
           & (np.arange(NC)[None, :] < ncmp) & (np.arange(FEAT)[:, None] < ns))
    o_cmp, bias_sel = _cmp_attn(nq_t, kvc, jnp.asarray(ovt, BF16), B=B, S=S)

    feats_sel = _key_features(S, NSA_SEL_BLOCK)
    o_sel = _flash(nq_t, bias_sel, ksel, feats_sel, vsel_t,
                   B=B, S=S, units=G, tq=TQ, nh=NSA_HG, pair=False, window=None)
    o_win = _flash(nq_t, jnp.zeros_like(bias_sel), kwin, feats_sel, vwin_t,
                   B=B, S=S, units=G, tq=TQ, nh=NSA_HG, pair=False, window=NSA_WINDOW)

    avg = np.zeros((MOBA_NBP, S), np.float32)
    avg[np.arange(S) // MOBA_BLOCK, np.arange(S)] = 1.0 / MOBA_BLOCK
    bias_m = _moba_gate(mq_t, mk, jnp.asarray(avg, BF16), B=B, S=S)
    o_m = _flash(mq_t, bias_m, mk, _key_features(S, MOBA_BLOCK), mv_t,
                 B=B, S=S, units=MOBA_HEADS // 2, tq=TQM, nh=2, pair=True, window=None)

    n = np.arange(NSA_W)[:, None]
    c = np.arange(GATE_ROWS)[None, :]
    expand = jnp.asarray(np.stack([(c == 3 * (n // hd) + br) for br in range(3)]), BF16)
    return _even_out(x2d, o_cmp, o_sel, o_win, o_m, gate_t, nz_t, mz_t, expand, w_out.astype(BF16),
                     ln_g.reshape(1, D), ln_b.reshape(1, D), alpha, B=B, S=S)


def _odd_layer(x2d, B, S, w_in, gate_w2, gate_b, gn_g, w_out, ln_g, ln_b, alpha):
    T, D = x2d.shape
    H, dk, dv, C = GLA_HEADS, GLA_DK, GLA_DV, GLA_CHUNK
    wq, wk, wv, wg, wz = jnp.split(w_in, np.cumsum(ODD_SPLITS)[:-1].tolist(), axis=1)
    w_perm = jnp.concatenate([wq, wk, wv, wz, _pad_to(wg, LANES, 1)], axis=1).astype(BF16)
    q, k, v, z, glr = _proj(x2d, w_perm, (H * dk, H * dk, H * dv, H * dv, LANES),
                            (BF16, BF16, BF16, BF16, F32))
    w2p = _pad_to(gate_w2, LANES, 0)
    nch = min(GLA_STEP_CHUNKS, S // C)
    tril = jnp.asarray(np.kron(np.eye(nch), np.tril(np.ones((C, C)))), BF16)
    o = _gla(q, k, v, glr, w2p, gate_b.reshape(1, H * dk), tril, B=B, S=S)
    return _odd_out(x2d, o, z, gn_g.reshape(1, H * dv), w_out.astype(BF16),
                    ln_g.reshape(1, D), ln_b.reshape(1, D), alpha)


def kernel(x, ev_w_in, ev_cmp_pos, ev_cmp_w1, ev_cmp_b1, ev_cmp_w2, ev_w_out, ev_ln_g, ev_ln_b,
           od_w_in, od_gate_w2, od_gate_b, od_gn_g, od_w_out, od_ln_g, od_ln_b):
    B, S, D = x.shape
    depth = ev_w_in.shape[0] + od_w_in.shape[0]
    alpha = (2.0 * depth) ** 0.25
    h = x.reshape(B * S, D)
    for layer in range(depth):
        i = layer // 2
        if layer % 2 == 0:
            h = _even_layer(h, B, S, ev_w_in[i], ev_cmp_pos[i], ev_cmp_w1[i], ev_cmp_b1[i], ev_cmp_w2[i],
                            ev_w_out[i], ev_ln_g[i], ev_ln_b[i], alpha)
        else:
            h = _odd_layer(h, B, S, od_w_in[i], od_gate_w2[i], od_gate_b[i], od_gn_g[i],
                           od_w_out[i], od_ln_g[i], od_ln_b[i], alpha)
    return h.reshape(B, S, D)
```

```python
import functools

import numpy as np
import jax
import jax.numpy as jnp
from jax import lax
from jax.experimental import pallas as pl
from jax.experimental.pallas import tpu as pltpu

F32 = jnp.float32
BF16 = jnp.bfloat16

HEAD_DIM = 64
NSA_HEADS = 8
NSA_KV_GROUPS = 2
NSA_HG = NSA_HEADS // NSA_KV_GROUPS
NSA_CMP_BLOCK = 32
NSA_CMP_STRIDE = 16
NSA_CMP_HIDDEN = 128
NSA_SEL_BLOCK = 64
NSA_TOP_N = 16
NSA_WINDOW = 512
MOBA_HEADS = 8
MOBA_BLOCK = 256
MOBA_TOP_K = 3
GLA_HEADS = 4
GLA_DK = 128
GLA_DV = 256
GLA_GATE_RANK = 16
GLA_TAU = 16.0
GLA_CHUNK = 64
GLA_SUB = 16
GLA_STEP_CHUNKS = 8
GLA_FACTOR_MAX_DECAY = 60.0
LN_EPS = 1e-5
NEG = -1e30
FORCE_BONUS = 1e4

NSA_W = NSA_HEADS * HEAD_DIM
NSA_KV_W = NSA_KV_GROUPS * HEAD_DIM
MOBA_W = MOBA_HEADS * HEAD_DIM
EVEN_SPLITS = (NSA_W, 6 * NSA_KV_W, 3 * NSA_HEADS, NSA_W, MOBA_W, MOBA_W, MOBA_W, MOBA_W)
ODD_SPLITS = (GLA_HEADS * GLA_DK, GLA_HEADS * GLA_DK, GLA_HEADS * GLA_DV, GLA_GATE_RANK, GLA_HEADS * GLA_DV)

LANES = 128
SUBLANES = 8
FEAT = 64
ALIBI_ROWS = 16
GATE_ROWS = 32
MOBA_NBP = 16
TQ = 512
TQM = 512
TK = 512
CG = 256
MIN_WEIGHT_SUM = 1e-25
VROWS = HEAD_DIM + 16
VMEM_LIMIT = 48 * 1024 * 1024


def _dot(a, b):
    return jnp.dot(a, b, preferred_element_type=F32)


def _dot_nt(a, b):
    return lax.dot_general(a, b, (((1,), (1,)), ((), ())), preferred_element_type=F32)


def _dot_tn(a, b):
    return lax.dot_general(a, b, (((0,), (0,)), ((), ())), preferred_element_type=F32)


def _hilo(a):
    hi = a.astype(BF16)
    lo = (a - hi.astype(F32)).astype(BF16)
    return hi, lo


def _pow2_neg(n):
    return lax.bitcast_convert_type(lax.shift_left(127 - n, 23), F32)


def _sigmoid(x):
    return 1.0 / (1.0 + jnp.exp(-x))


def _layer_norm(r, g, b):
    mu = jnp.mean(r, axis=-1, keepdims=True)
    d = r - mu
    var = jnp.mean(d * d, axis=-1, keepdims=True)
    return d * lax.rsqrt(var + LN_EPS) * g + b


def _alibi_query_rows(t0, tq, head0, R):
    r = lax.broadcasted_iota(jnp.int32, (ALIBI_ROWS, R), 0)
    col = lax.broadcasted_iota(jnp.int32, (ALIBI_ROWS, R), 1)
    t = t0 + col % tq
    slope = _pow2_neg(head0 + col // tq + 1)
    feats = jnp.where(r == 0, -FEAT * slope * (t // FEAT).astype(F32),
                      jnp.where(r == 1, -slope * (t % FEAT).astype(F32), jnp.where(r < 4, slope, 0.0)))
    return jnp.concatenate([feats.astype(BF16), jnp.zeros((FEAT - ALIBI_ROWS, R), BF16)], axis=0)


def _ranks(sc_ref, n_real):
    n_rows, width = sc_ref.shape
    out = []
    for v in range(n_rows // SUBLANES):
        lo = v * SUBLANES
        sc = sc_ref[lo:lo + SUBLANES, :]
        idx = lo + lax.broadcasted_iota(jnp.int32, (SUBLANES, width), 0)
        rank = jnp.zeros((SUBLANES, width), jnp.int32)
        for jp in range(n_real):
            r = sc_ref[pl.ds(jp, 1), :]
            if jp < lo:
                beats = r >= sc
            elif jp >= lo + SUBLANES - 1:
                beats = r > sc
            else:
                beats = (r > sc) | ((r == sc) & (idx > jp))
            rank = rank + jnp.where(beats, 1, 0)
        out.append(rank)
    return jnp.concatenate(out, axis=0)


def _proj_kernel(*refs, tok_splits, tr_splits):
    x_ref, wt_ref = refs[0], refs[1]
    outs = list(refs[3:] if tr_splits else refs[2:])
    x = x_ref[...]
    xb = x.astype(BF16)
    off = 0
    for n in tok_splits:
        o_ref = outs.pop(0)
        o_ref[...] = _dot(xb, wt_ref[:, off:off + n]).astype(o_ref.dtype)
        off += n
    if tr_splits:
        wtr_ref = refs[2]
        xt = x.T.astype(BF16)
        off = 0
        for n in tr_splits:
            o_ref = outs.pop(0)
            o_ref[...] = _dot(wtr_ref[off:off + n, :], xt).astype(o_ref.dtype)
            off += n


def _proj(x2d, w_tok, tok_splits, tok_dtypes, w_tr=None, tr_splits=(), tr_dtypes=(), tm=512):
    T, D = x2d.shape
    assert T % tm == 0 and sum(tok_splits) == w_tok.shape[1]
    in_specs = [pl.BlockSpec((tm, D), lambda i: (i, 0)),
                pl.BlockSpec(w_tok.shape, lambda i: (0, 0))]
    args = [x2d, w_tok]
    if tr_splits:
        assert sum(tr_splits) == w_tr.shape[0]
        in_specs.append(pl.BlockSpec(w_tr.shape, lambda i: (0, 0)))
        args.append(w_tr)
    out_specs = ([pl.BlockSpec((tm, n), lambda i: (i, 0)) for n in tok_splits]
                 + [pl.BlockSpec((n, tm), lambda i: (0, i)) for n in tr_splits])
    out_shape = ([jax.ShapeDtypeStruct((T, n), dt) for n, dt in zip(tok_splits, tok_dtypes)]
                 + [jax.ShapeDtypeStruct((n, T), dt) for n, dt in zip(tr_splits, tr_dtypes)])
    return pl.pallas_call(
        functools.partial(_proj_kernel, tok_splits=tuple(tok_splits), tr_splits=tuple(tr_splits)),
        grid=(T // tm,), in_specs=in_specs, out_specs=out_specs, out_shape=out_shape,
        compiler_params=pltpu.CompilerParams(dimension_semantics=("parallel",),
                                             vmem_limit_bytes=VMEM_LIMIT),
        name="proj",
    )(*args)


def _cmp_mlp_kernel(kx_ref, vx_ref, w1_ref, pos_ref, b1_ref, w2_ref, cf_ref, o_ref, *, ncmp):
    for i, x_ref in enumerate((kx_ref, vx_ref)):
        x = x_ref[...]
        nc, cw = x.shape
        a = _dot(x, w1_ref[i, 0])
        bm = _dot(x, w1_ref[i, 1])
        c = b1_ref[i]
        for half in range(2):
            phi, plo = _hilo(jnp.broadcast_to(pos_ref[i, half], (8, cw)))
            c = c + (_dot(phi, w1_ref[i, half]) + _dot(plo, w1_ref[i, half]))[0:1]
        hid = a + pltpu.roll(bm, nc - 1, axis=0) + c
        hid = hid * _sigmoid(hid)
        row = lax.broadcasted_iota(jnp.int32, hid.shape, 0)
        hid = jnp.where(row < ncmp, hid, 0.0)
        out = _dot(hid.astype(BF16), w2_ref[i])
        if i == 0:
            out = out + cf_ref[...]
        o_ref[i] = out.astype(o_ref.dtype)


def _cmp_mlp(kx, vx, w1p, posp, b1p, w2p, cfeat, ncmp):
    B, NC, CW = kx.shape
    OW = w2p.shape[-1]
    full = lambda a: pl.BlockSpec(a.shape, lambda b: (0,) * a.ndim)
    return pl.pallas_call(
        functools.partial(_cmp_mlp_kernel, ncmp=ncmp),
        grid=(B,),
        in_specs=[pl.BlockSpec((None, NC, CW), lambda b: (b, 0, 0)),
                  pl.BlockSpec((None, NC, CW), lambda b: (b, 0, 0)),
                  full(w1p), full(posp), full(b1p), full(w2p), full(cfeat)],
        out_specs=pl.BlockSpec((2, None, NC, OW), lambda b: (0, b, 0, 0)),
        out_shape=jax.ShapeDtypeStruct((2, B, NC, OW), BF16),
        compiler_params=pltpu.CompilerParams(dimension_semantics=("parallel",),
                                             vmem_limit_bytes=VMEM_LIMIT),
        name="cmp_mlp",
    )(kx, vx, w1p, posp, b1p, w2p, cfeat)


def _cmp_attn_kernel(q_ref, kc_ref, vc_ref, ovt_ref, o_ref, bias_ref, sc_ref, *, tq, nh, ns, ntop):
    g = pl.program_id(1)
    R = nh * tq
    nc = kc_ref.shape[0]
    nq = q_ref.shape[1] // tq

    def q_tile(qi, carry):
        toks = pl.ds(pl.multiple_of(qi * tq, tq), tq)
        qcat = jnp.concatenate([q_ref[hh * HEAD_DIM:(hh + 1) * HEAD_DIM, toks] for hh in range(nh)], axis=1)
        qa = jnp.concatenate([qcat, _alibi_query_rows(qi * tq, tq, g * nh, R)], axis=0)
        s = _dot(kc_ref[...], qa)
        n = lax.broadcasted_iota(jnp.int32, (nc, R), 0)
        t_row = qi * tq + lax.broadcasted_iota(jnp.int32, (1, R), 1) % tq
        last = (t_row - (NSA_CMP_BLOCK - 1)) // NSA_CMP_STRIDE
        s = jnp.where(n <= last, s, NEG)
        e = jnp.exp(s - jnp.max(s, axis=0, keepdims=True))
        inv = jnp.where(last >= 0, 1.0 / jnp.sum(e, axis=0, keepdims=True), 0.0)
        p = e * inv
        o_ref[:, pl.ds(pl.multiple_of(qi * R, R), R)] = _dot_tn(vc_ref[:, 0:HEAD_DIM],
                                                                 p.astype(BF16)).astype(o_ref.dtype)

        psum = p[:, 0:tq]
        for hh in range(1, nh):
            psum = psum + p[:, hh * tq:(hh + 1) * tq]
        phi, plo = _hilo(psum)
        imp = _dot(ovt_ref[...], phi) + _dot(ovt_ref[...], plo)

        j = lax.broadcasted_iota(jnp.int32, (FEAT, tq), 0)
        cur = (qi * tq + lax.broadcasted_iota(jnp.int32, (FEAT, tq), 1)) // NSA_SEL_BLOCK
        forced = jnp.where((j == 0) | (j == cur) | (j == cur - 1), FORCE_BONUS, 0.0)
        sc = jnp.where(j <= cur, imp + forced, NEG)
        sc_ref[...] = sc
        sel = (_ranks(sc_ref, ns) < ntop) & (j <= cur)
        bias_ref[:, toks] = jnp.where(sel, 0.0, NEG).astype(bias_ref.dtype)
        return carry

    lax.fori_loop(0, nq, q_tile, 0)


def _cmp_attn(nq_t, kvc, ovt, *, B, S):
    tq, nh, G = TQ, NSA_HG, NSA_KV_GROUPS
    R = nh * tq
    nq = S // tq
    NC = kvc.shape[2]
    ns = S // NSA_SEL_BLOCK
    assert ns <= FEAT
    ntop = min(NSA_TOP_N, ns)
    return pl.pallas_call(
        functools.partial(_cmp_attn_kernel, tq=tq, nh=nh, ns=ns, ntop=ntop),
        grid=(B, G),
        in_specs=[pl.BlockSpec((nh * HEAD_DIM, S), lambda b, g: (g, b)),
                  pl.BlockSpec((None, None, NC, LANES), lambda b, g: (0, b, 0, g)),
                  pl.BlockSpec((None, None, NC, LANES), lambda b, g: (1, b, 0, g)),
                  pl.BlockSpec(ovt.shape, lambda b, g: (0, 0))],
        out_specs=[pl.BlockSpec((None, None, HEAD_DIM, S * nh), lambda b, g: (b, g, 0, 0)),
                   pl.BlockSpec((None, None, FEAT, S), lambda b, g: (b, g, 0, 0))],
        out_shape=[jax.ShapeDtypeStruct((B, G, HEAD_DIM, S * nh), BF16),
                   jax.ShapeDtypeStruct((B, G, FEAT, S), BF16)],
        scratch_shapes=[pltpu.VMEM((FEAT, tq), F32)],
        compiler_params=pltpu.CompilerParams(dimension_semantics=("parallel", "parallel"),
                                             vmem_limit_bytes=VMEM_LIMIT),
        name="cmp_attn",
    )(nq_t, kvc, kvc, ovt)


def _moba_gate_kernel(q_ref, k_ref, a_ref, bias_ref, km_ref, sc_ref, *, tq, nb, ntop):
    h = pl.program_id(1)
    qi = pl.program_id(2)

    @pl.when(qi == 0)
    def _means():
        km_ref[...] = _dot(a_ref[...], k_ref[...])

    nbp = km_ref.shape[0]
    lane = lax.broadcasted_iota(jnp.int32, (nbp, LANES), 1)
    km = jnp.where(lane // HEAD_DIM == h % 2, km_ref[...], 0.0)
    khi, klo = _hilo(km)
    q = q_ref[...]
    qq = jnp.concatenate([q, q], axis=0)
    g_t = _dot(khi, qq) + _dot(klo, qq)
    n = lax.broadcasted_iota(jnp.int32, (nbp, tq), 0)
    cur = (qi * tq + lax.broadcasted_iota(jnp.int32, (nbp, tq), 1)) // MOBA_BLOCK
    sc = jnp.where(n < cur, g_t, NEG)
    sc_ref[...] = sc
    sel = ((_ranks(sc_ref, nb) < ntop) & (n < cur)) | (n == cur)
    bias_ref[0:nbp, :] = jnp.where(sel, 0.0, NEG).astype(bias_ref.dtype)
    bias_ref[nbp:FEAT, :] = jnp.zeros((FEAT - nbp, tq), bias_ref.dtype)


def _moba_gate(mq_t, mk, avg, *, B, S):
    tq, H = S, MOBA_HEADS
    nq = S // tq
    nb = S // MOBA_BLOCK
    nbp = avg.shape[0]
    assert nb <= nbp
    ntop = min(MOBA_TOP_K, nb)
    return pl.pallas_call(
        functools.partial(_moba_gate_kernel, tq=tq, nb=nb, ntop=ntop),
        grid=(B, H, nq),
        in_specs=[pl.BlockSpec((HEAD_DIM, tq), lambda b, h, i: (h, b * nq + i)),
                  pl.BlockSpec((S, LANES), lambda b, h, i: (b, h // 2)),
                  pl.BlockSpec(avg.shape, lambda b, h, i: (0, 0))],
        out_specs=pl.BlockSpec((None, None, FEAT, tq), lambda b, h, i: (b, h, 0, i)),
        out_shape=jax.ShapeDtypeStruct((B, H, FEAT, S), BF16),
        scratch_shapes=[pltpu.VMEM((nbp, LANES), F32), pltpu.VMEM((nbp, tq), F32)],
        compiler_params=pltpu.CompilerParams(dimension_semantics=("parallel", "parallel", "arbitrary"),
                                             vmem_limit_bytes=VMEM_LIMIT),
        name="moba_gate",
    )(mq_t, mk, avg)


def _flash_q_tile(qi, q_ref, bias_ref, mask_ref, o_ref, ka_ref, vta_ref, kn_ref, qa_ref, bnd_ref, m_ref,
                  acc_ref, alpha_ref, p_ref, *, tq, tk, nh, pair, window):
    u = pl.program_id(1)
    R = nh * tq
    q0 = pl.multiple_of(qi * tq, tq)
    toks = pl.ds(q0, tq)

    qcat = jnp.concatenate([q_ref[hh * HEAD_DIM:(hh + 1) * HEAD_DIM, toks] for hh in range(nh)], axis=1)
    qf = qcat.astype(F32)
    qq = jnp.concatenate([qf, qf], axis=0)
    rowi = lax.broadcasted_iota(jnp.int32, (2 * HEAD_DIM, R), 0)
    coli = lax.broadcasted_iota(jnp.int32, (2 * HEAD_DIM, R), 1)
    half = coli // tq if pair else u % 2
    qa_ref[0:2 * HEAD_DIM, :] = jnp.where(rowi // HEAD_DIM == half, qq, 0.0).astype(BF16)
    qa_ref[2 * HEAD_DIM:2 * HEAD_DIM + FEAT, :] = jnp.concatenate(
        [bias_ref[hh if pair else 0, :, toks] for hh in range(nh)], axis=1)
    qa_ref[2 * HEAD_DIM + FEAT:2 * LANES, :] = _alibi_query_rows(q0, tq, u * nh, R)
    kn = [jnp.concatenate([kn_ref[hf:hf + 1, :]] * (R // LANES), axis=1) for hf in range(2)]
    half_row = lax.broadcasted_iota(jnp.int32, (1, R), 1) // tq if pair else u % 2
    qn = jnp.sqrt(jnp.sum(qf * qf, axis=0, keepdims=True))
    bnd_ref[...] = jnp.broadcast_to(qn * jnp.where(half_row == 0, kn[0], kn[1]), bnd_ref.shape)

    ncg = R // CG
    kd = (q0 + tq - 1) // tk

    def key_rows(kind, c):
        off = (c * CG) % tq
        if kind is None:
            return 0, tk, None
        if kind == "diag":
            return 0, off + CG, off // CG
        return off, tk, (tk + off) // CG

    def sweep(running_max):
        acc_ref[...] = jnp.zeros(acc_ref.shape, F32)
        if running_max:
            m_ref[...] = jnp.full(m_ref.shape, NEG, F32)

        def values(ki, kind=None):
            for c in range(ncg):
                cols = pl.ds(c * CG, CG)
                r0, r1, _ = key_rows(kind, c)
                v = (c * CG) // tq if pair else 0
                pv = _dot(vta_ref[v, ki, :, r0:r1], p_ref[r0:r1, cols])
                if running_max:
                    acc_ref[:, cols] = alpha_ref[0:1, cols] * acc_ref[:, cols] + pv
                else:
                    acc_ref[:, cols] = acc_ref[:, cols] + pv

        def scores(ki, kind=None, prev=None, prev_kind=None):
            k0 = pl.multiple_of(ki * tk, tk)
            s_groups = []
            for c in range(ncg):
                r0, r1, _ = key_rows(kind, c)
                s_groups.append(_dot(ka_ref[pl.ds(k0 + r0, r1 - r0), :], qa_ref[:, pl.ds(c * CG, CG)]))
            if prev is not None:
                values(prev, prev_kind)
            for c, s in enumerate(s_groups):
                cols = pl.ds(c * CG, CG)
                r0, r1, entry = key_rows(kind, c)
                if entry is not None:
                    s = s + mask_ref[entry, r0:r1, :]
                if running_max:
                    m_prev = m_ref[0:1, cols]
                    m_new = jnp.maximum(m_prev, jnp.max(s, axis=0, keepdims=True))
                    alpha_ref[:, cols] = jnp.broadcast_to(jnp.exp(m_prev - m_new), (alpha_ref.shape[0], CG))
                    m_ref[:, cols] = jnp.broadcast_to(m_new, (m_ref.shape[0], CG))
                    p_ref[r0:r1, cols] = jnp.exp(s - m_new).astype(BF16)
                else:
                    p_ref[r0:r1, cols] = jnp.exp(s - bnd_ref[0:1, cols]).astype(BF16)

        def steady(lo, hi):
            def pair_body(j, carry):
                ki = lo + 2 * j
                scores(ki, prev=ki - 1)
                scores(ki + 1, prev=ki)
                return carry
            lax.fori_loop(0, (hi - lo) // 2, pair_body, 0)

            @pl.when((hi - lo) % 2 == 1)
            def _odd_one():
                scores(hi - 1, prev=hi - 2)

        @pl.when(kd == 0)
        def _only_diagonal():
            scores(0, "diag")
            values(0, "diag")

        @pl.when(kd > 0)
        def _past_then_diagonal():
            if window is None:
                scores(0)
                steady(1, kd)
                scores(kd, "diag", prev=kd - 1)
            else:
                scores(kd - 1, "wedge")
                scores(kd, "diag", prev=kd - 1, prev_kind="wedge")
            values(kd, "diag")

    def emit():
        acc = acc_ref[...]
        out = (acc[0:HEAD_DIM] / acc[HEAD_DIM:HEAD_DIM + 1]).astype(o_ref.dtype)
        if pair:
            for hh in range(nh):
                o_ref[hh, :, toks] = out[:, hh * tq:(hh + 1) * tq]
        else:
            o_ref[0, :, pl.ds(pl.multiple_of(qi * R, R), R)] = out

    sweep(False)
    trusted = jnp.min(acc_ref[HEAD_DIM:HEAD_DIM + 1, :]) >= MIN_WEIGHT_SUM

    @pl.when(trusted)
    def _done():
        emit()

    @pl.when(jnp.logical_not(trusted))
    def _redo_with_running_max():
        sweep(True)
        emit()


def _flash_kernel(q_ref, bias_ref, k_ref, kf_ref, vt_ref, mask_ref, o_ref, ka_ref, vta_ref, kn_ref, qa_ref,
                  bnd_ref, m_ref, acc_ref, alpha_ref, p_ref, *, tq, tk, nh, pair, window):
    S = k_ref.shape[0]
    k = k_ref[...]
    ka_ref[:, 0:LANES] = k
    ka_ref[:, LANES:2 * LANES] = kf_ref[...]
    k2 = jnp.square(k.astype(F32))
    lane = lax.broadcasted_iota(jnp.int32, k2.shape, 1)
    for hf in range(2):
        n2 = jnp.sum(jnp.where(lane // HEAD_DIM == hf, k2, 0.0), axis=1, keepdims=True)
        kn_ref[hf:hf + 1, :] = jnp.broadcast_to(jnp.sqrt(jnp.max(n2, axis=0, keepdims=True)), (1, LANES))
    ones_row = jnp.where(lax.broadcasted_iota(jnp.int32, (VROWS - HEAD_DIM, tk), 0) == 0, 1.0, 0.0)
    for v in range(nh if pair else 1):
        for jt in range(S // tk):
            vta_ref[v, jt, 0:HEAD_DIM, :] = vt_ref[v * HEAD_DIM:(v + 1) * HEAD_DIM, jt * tk:(jt + 1) * tk]
            vta_ref[v, jt, HEAD_DIM:VROWS, :] = ones_row.astype(BF16)

    def q_tile(qi, carry):
        _flash_q_tile(qi, q_ref, bias_ref, mask_ref, o_ref, ka_ref, vta_ref, kn_ref, qa_ref, bnd_ref, m_ref,
                      acc_ref, alpha_ref, p_ref, tq=tq, tk=tk, nh=nh, pair=pair, window=window)
        return carry

    lax.fori_loop(0, S // tq, q_tile, 0)


def _mask_table(tk, window):
    n_delta = (tk + (window or 0)) // CG
    d = np.arange(n_delta)[:, None, None] * CG + np.arange(CG)[None, None, :] - np.arange(tk)[None, :, None]
    ok = (d >= 0) & ((d < window) if window else True)
    return jnp.asarray(np.where(ok, 0.0, NEG), F32)


def _flash(q_t, bias_t, k, kfeat, v_t, *, B, S, units, tq, nh, pair, window):
    tk = TK
    R = nh * tq
    nq = S // tq
    nv = nh if pair else 1
    nb = nh if pair else 1
    assert tq == tk and S % tk == 0 and tk % CG == 0 and window in (None, tk)
    mask = _mask_table(tk, window)
    kcol = (lambda u: u) if pair else (lambda u: u // 2)
    out_w = tq if pair else R
    out_shape = (B, units * nb, HEAD_DIM, S if pair else S * nh)
    return pl.pallas_call(
        functools.partial(_flash_kernel, tq=tq, tk=tk, nh=nh, pair=pair, window=window),
        grid=(B, units),
        in_specs=[pl.BlockSpec((nh * HEAD_DIM, S), lambda b, u: (u, b)),
                  pl.BlockSpec((None, nb, FEAT, S), lambda b, u: (b, u, 0, 0)),
                  pl.BlockSpec((S, LANES), lambda b, u: (b, kcol(u))),
                  pl.BlockSpec((S, LANES), lambda b, u: (0, 0)),
                  pl.BlockSpec((nv * HEAD_DIM, S), lambda b, u: (u, b)),
                  pl.BlockSpec(mask.shape, lambda b, u: (0, 0, 0))],
        out_specs=pl.BlockSpec((None, nb, HEAD_DIM, out_w * nq), lambda b, u: (b, u, 0, 0)),
        out_shape=jax.ShapeDtypeStruct(out_shape, BF16),
        scratch_shapes=[pltpu.VMEM((S, 2 * LANES), BF16),
                        pltpu.VMEM((nv, S // tk, VROWS, tk), BF16),
                        pltpu.VMEM((8, LANES), F32),
                        pltpu.VMEM((2 * LANES, R), BF16),
                        pltpu.VMEM((8, R), F32),
                        pltpu.VMEM((8, R), F32),
                        pltpu.VMEM((VROWS, R), F32),
                        pltpu.VMEM((8, R), F32),
                        pltpu.VMEM((tk, R), BF16)],
        compiler_params=pltpu.CompilerParams(dimension_semantics=("parallel", "parallel"),
                                             vmem_limit_bytes=VMEM_LIMIT),
        name="flash",
    )(q_t, bias_t, k, kfeat, v_t, mask)


def _even_out_kernel(x_ref, ocmp_ref, osel_ref, owin_ref, om_ref, gate_ref, nz_ref, mz_ref,
                     e_ref, w_ref, g_ref, b_ref, o_ref, *, alpha, tm):
    G, nh = NSA_KV_GROUPS, NSA_HG
    sig = _sigmoid(gate_ref[...])
    shi, slo = _hilo(sig)

    def heads_t(ref):
        return jnp.concatenate([ref[g, :, hh * tm:(hh + 1) * tm] for g in range(G) for hh in range(nh)],
                               axis=0).astype(F32)

    o_nsa = None
    for br, ref in enumerate((ocmp_ref, osel_ref, owin_ref)):
        gb = _dot(e_ref[br], shi) + _dot(e_ref[br], slo)
        term = gb * heads_t(ref)
        o_nsa = term if o_nsa is None else o_nsa + term
    nz = nz_ref[...].astype(F32)
    mz = mz_ref[...].astype(F32)
    o_m = jnp.concatenate([om_ref[h] for h in range(MOBA_HEADS)], axis=0).astype(F32)
    a_t = jnp.concatenate([o_nsa * (nz * _sigmoid(nz)), o_m * (mz * _sigmoid(mz))], axis=0)
    y = _dot_tn(a_t.astype(BF16), w_ref[...])
    o_ref[...] = _layer_norm(alpha * x_ref[...] + y, g_ref[...], b_ref[...])


def _even_out(x2d, ocmp, osel, owin, om, gate_t, nz_t, mz_t, expand, w_out, ln_g, ln_b, alpha, *, B, S):
    T, D = x2d.shape
    tm = TQ
    nq = S // tm
    G, nh, H = NSA_KV_GROUPS, NSA_HG, MOBA_HEADS
    nsa = pl.BlockSpec((None, G, HEAD_DIM, nh * tm), lambda i: (i // nq, 0, 0, i % nq))
    feat = lambda n: pl.BlockSpec((n, tm), lambda i: (0, i))
    full = lambda a: pl.BlockSpec(a.shape, lambda i: (0,) * a.ndim)
    return pl.pallas_call(
        functools.partial(_even_out_kernel, alpha=alpha, tm=tm),
        grid=(T // tm,),
        in_specs=[pl.BlockSpec((tm, D), lambda i: (i, 0)), nsa, nsa, nsa,
                  pl.BlockSpec((None, H, HEAD_DIM, tm), lambda i: (i // nq, 0, 0, i % nq)),
                  feat(GATE_ROWS), feat(NSA_W), feat(MOBA_W),
                  full(expand), full(w_out), full(ln_g), full(ln_b)],
        out_specs=pl.BlockSpec((tm, D), lambda i: (i, 0)),
        out_shape=jax.ShapeDtypeStruct((T, D), F32),
        compiler_params=pltpu.CompilerParams(dimension_semantics=("parallel",),
                                             vmem_limit_bytes=VMEM_LIMIT),
        name="even_out",
    )(x2d, ocmp, osel, owin, om, gate_t, nz_t, mz_t, expand, w_out, ln_g, ln_b)


def _gla_head_exact(row0, h, q_ref, v_ref, o_ref, st_ref, b_ref, kf_ref, scale):
    C, dk, dv, SUB = GLA_CHUNK, GLA_DK, GLA_DV, GLA_SUB
    rows = lax.broadcasted_iota(jnp.int32, (C, dk), 0)
    sub_i = lax.broadcasted_iota(jnp.int32, (SUB, LANES), 0)
    lane = lax.broadcasted_iota(jnp.int32, (SUB, LANES), 1)
    ones = jnp.ones((dk, LANES), BF16)
    sl = pl.ds(h * dk, dk)
    rs_c = pl.ds(row0, C)
    bh = b_ref[:, sl]
    kf = kf_ref[:, sl]
    qf = q_ref[rs_c, sl].astype(F32) * scale
    vh = v_ref[rs_c, pl.ds(h * dv, dv)]
    st = st_ref[h]
    inter = _dot_nt((qf * jnp.exp(bh)).astype(BF16), st.astype(BF16))
    bl = b_ref[pl.ds(C - 1, 1), sl]
    kd = kf * jnp.exp(bl - bh)

    def off_block(r0, r1, c0, c1):
        ref_row = b_ref[pl.ds(r0, 1), sl]
        qt = qf[r0:r1] * jnp.exp(bh[r0:r1] - ref_row)
        ek = jnp.where((rows >= c0) & (rows < c1), ref_row - bh, NEG)
        kt = kf * jnp.exp(ek)
        return _dot_nt(qt.astype(BF16), kt.astype(BF16))

    lower = off_block(2 * SUB, C, 0, 2 * SUB)
    blocks = [None, off_block(SUB, 2 * SUB, 0, SUB), lower[0:SUB],
              lower[SUB:2 * SUB] + off_block(3 * SUB, C, 2 * SUB, 3 * SUB)]
    pieces = []
    for blk in range(C // SUB):
        r0 = blk * SUB
        q_i = qf[r0:r0 + SUB]
        b_i = bh[r0:r0 + SUB]
        for jj in range(SUB):
            b_j = b_ref[pl.ds(r0 + jj, 1), sl]
            k_j = kf_ref[pl.ds(r0 + jj, 1), sl]
            pieces.append((q_i * k_j * jnp.exp(jnp.minimum(b_i - b_j, 0.0))).astype(BF16))
    sums = _dot(jnp.concatenate(pieces, axis=0), ones)
    for blk in range(C // SUB):
        r0 = blk * SUB
        diag = jnp.zeros((SUB, LANES), F32)
        for jj in range(SUB):
            rs = sums[(r0 + jj) * SUB:(r0 + jj + 1) * SUB]
            diag = jnp.where((lane == r0 + jj) & (sub_i >= jj), rs, diag)
        diag = diag[:, 0:C]
        blocks[blk] = diag if blocks[blk] is None else blocks[blk] + diag
    att = jnp.concatenate(blocks, axis=0)
    o_ref[rs_c, pl.ds(h * dv, dv)] = (inter + _dot(att.astype(BF16), vh)).astype(o_ref.dtype)
    st_ref[h] = st * jnp.exp(bl) + _dot_tn(vh, kd.astype(BF16))


def _gla_head_bounded(row0, h, q_ref, v_ref, o_ref, st_ref, b_ref, kf_ref, scale):
    C, dk, dv = GLA_CHUNK, GLA_DK, GLA_DV
    sl = pl.ds(h * dk, dk)
    rs_c = pl.ds(row0, C)
    bh = b_ref[rs_c, sl]
    kf = kf_ref[rs_c, sl]
    qf = q_ref[rs_c, sl].astype(F32) * scale
    vh = v_ref[rs_c, pl.ds(h * dv, dv)]
    st = st_ref[h]
    qs = (qf * jnp.exp(bh)).astype(BF16)
    kt = (kf * jnp.exp(-bh)).astype(BF16)
    res = _dot_nt(qs, jnp.concatenate([st.astype(BF16), kt], axis=0))
    i = lax.broadcasted_iota(jnp.int32, (C, C), 0)
    j = lax.broadcasted_iota(jnp.int32, (C, C), 1)
    att = jnp.where(i >= j, res[:, dv:dv + C], 0.0)
    o_ref[rs_c, pl.ds(h * dv, dv)] = (res[:, 0:dv] + _dot(att.astype(BF16), vh)).astype(o_ref.dtype)
    bl = b_ref[pl.ds(row0 + C - 1, 1), sl]
    kd = kf * jnp.exp(bl - bh)
    st_ref[h] = st * jnp.exp(bl) + _dot_tn(vh, kd.astype(BF16))


def _gla_kernel(q_ref, k_ref, v_ref, glr_ref, w2_ref, gb_ref, tril_ref, o_ref,
                st_ref, b_ref, kf_ref, bc_ref, kc_ref, *, scale):
    C, H = GLA_CHUNK, GLA_HEADS
    nch = q_ref.shape[0] // C

    @pl.when(pl.program_id(1) == 0)
    def _init():
        st_ref[...] = jnp.zeros(st_ref.shape, F32)

    ghi, glo = _hilo(glr_ref[...])
    whi, wlo = _hilo(w2_ref[...])
    pre = _dot(ghi, whi) + _dot(glo, whi) + _dot(ghi, wlo) + gb_ref[...]
    la = (jnp.minimum(pre, 0.0) - jnp.log(1.0 + jnp.exp(-jnp.abs(pre)))) * (1.0 / GLA_TAU)
    lhi, llo = _hilo(la)
    b_all = _dot(tril_ref[...], lhi) + _dot(tril_ref[...], llo)
    b_ref[...] = b_all
    kf_ref[...] = k_ref[...].astype(F32)
    decay = -b_all[C - 1:C, :]
    for c in range(1, nch):
        decay = jnp.maximum(decay, -b_all[c * C + C - 1:(c + 1) * C, :])
    bounded = jnp.max(decay) <= GLA_FACTOR_MAX_DECAY

    @pl.when(bounded)
    def _factored():
        for c in range(nch):
            for h in range(H):
                _gla_head_bounded(c * C, h, q_ref, v_ref, o_ref, st_ref, b_ref, kf_ref, scale)

    @pl.when(jnp.logical_not(bounded))
    def _exact():
        def chunk(c, carry):
            row0 = pl.multiple_of(c * C, C)
            bc_ref[...] = b_ref[pl.ds(row0, C), :]
            kc_ref[...] = kf_ref[pl.ds(row0, C), :]
            for h in range(H):
                _gla_head_exact(row0, h, q_ref, v_ref, o_ref, st_ref, bc_ref, kc_ref, scale)
            return carry
        lax.fori_loop(0, nch, chunk, 0)


def _gla(q, k, v, glr, w2p, gb, tril, *, B, S):
    C, H, dk, dv = GLA_CHUNK, GLA_HEADS, GLA_DK, GLA_DV
    rows = tril.shape[0]
    nchunk = S // rows
    tok = lambda n: pl.BlockSpec((rows, n), lambda b, c: (b * nchunk + c, 0))
    const = lambda shape: pl.BlockSpec(shape, lambda b, c: (0,) * len(shape))
    return pl.pallas_call(
        functools.partial(_gla_kernel, scale=float(dk) ** -0.5),
        grid=(B, nchunk),
        in_specs=[tok(H * dk), tok(H * dk), tok(H * dv), tok(LANES),
                  const(w2p.shape), const(gb.shape), const(tril.shape)],
        out_specs=tok(H * dv),
        out_shape=jax.ShapeDtypeStruct((B * S, H * dv), BF16),
        scratch_shapes=[pltpu.VMEM((H, dv, dk), F32),
                        pltpu.VMEM((rows, H * dk), F32),
                        pltpu.VMEM((rows, H * dk), F32),
                        pltpu.VMEM((C, H * dk), F32),
                        pltpu.VMEM((C, H * dk), F32)],
        compiler_params=pltpu.CompilerParams(dimension_semantics=("parallel", "arbitrary"),
                                             vmem_limit_bytes=VMEM_LIMIT),
        name="gla",
    )(q, k, v, glr, w2p, gb, tril)


def _odd_out_kernel(x_ref, o_ref_in, z_ref, gn_ref, w_ref, g_ref, b_ref, out_ref, *, alpha):
    H, dv = GLA_HEADS, GLA_DV
    parts = []
    for h in range(H):
        sl = pl.ds(h * dv, dv)
        oh = o_ref_in[:, sl].astype(F32)
        r = lax.rsqrt(jnp.mean(oh * oh, axis=-1, keepdims=True) + LN_EPS)
        z = z_ref[:, sl].astype(F32)
        parts.append((oh * r * gn_ref[:, sl] * (z * _sigmoid(z))).astype(BF16))
    y = _dot(jnp.concatenate(parts, axis=1), w_ref[...])
    out_ref[...] = _layer_norm(alpha * x_ref[...] + y, g_ref[...], b_ref[...])


def _odd_out(x2d, o, z, gn_g, w_out, ln_g, ln_b, alpha, tm=512):
    T, D = x2d.shape
    W = o.shape[1]
    row = lambda n: pl.BlockSpec((tm, n), lambda i: (i, 0))
    return pl.pallas_call(
        functools.partial(_odd_out_kernel, alpha=alpha),
        grid=(T // tm,),
        in_specs=[row(D), row(W), row(W),
                  pl.BlockSpec((1, W), lambda i: (0, 0)),
                  pl.BlockSpec((W, D), lambda i: (0, 0)),
                  pl.BlockSpec((1, D), lambda i: (0, 0)),
                  pl.BlockSpec((1, D), lambda i: (0, 0))],
        out_specs=row(D),
        out_shape=jax.ShapeDtypeStruct((T, D), F32),
        compiler_params=pltpu.CompilerParams(dimension_semantics=("parallel",),
                                             vmem_limit_bytes=VMEM_LIMIT),
        name="odd_out",
    )(x2d, o, z, gn_g, w_out, ln_g, ln_b)


def _key_features(S, block):
    s = np.arange(S)
    f = np.zeros((S, LANES), np.float32)
    f[s, s // block] = 1.0
    f[:, FEAT + 0] = 1.0
    f[:, FEAT + 1] = 1.0
    f[:, FEAT + 2] = FEAT * (s // FEAT)
    f[:, FEAT + 3] = s % FEAT
    return jnp.asarray(f, BF16)


def _pad_to(a, n, axis):
    pad = [(0, 0)] * a.ndim
    pad[axis] = (0, n - a.shape[axis])
    return jnp.pad(a, pad)


def _even_layer(x2d, B, S, w_in, cmp_pos, cmp_w1, cmp_b1, cmp_w2, w_out, ln_g, ln_b, alpha):
    T, D = x2d.shape
    G, hd = NSA_KV_GROUPS, HEAD_DIM
    scale = hd ** -0.5
    wq, wkv, wgate, wnz, wmq, wmk, wmv, wmz = jnp.split(w_in, np.cumsum(EVEN_SPLITS)[:-1].tolist(), axis=1)
    wkc, wvc, wks, wvs, wkw, wvw = jnp.split(wkv, 6, axis=1)
    w_tok = jnp.concatenate([wkc, wvc, wks, wkw, wmk], axis=1).astype(BF16)
    w_tr = jnp.concatenate([wq * scale, wvs, wvw, wmq * scale, wmv, wnz, wmz,
                            _pad_to(wgate, GATE_ROWS, 1)], axis=1).T.astype(BF16)
    (kcmp, vcmp, ksel, kwin, mk, nq_t, vsel_t, vwin_t, mq_t, mv_t, nz_t, mz_t, gate_t) = _proj(
        x2d, w_tok, (NSA_KV_W,) * 4 + (MOBA_W,), (BF16,) * 5,
        w_tr, (NSA_W, NSA_KV_W, NSA_KV_W, MOBA_W, MOBA_W, NSA_W, MOBA_W, GATE_ROWS),
        (BF16,) * 7 + (F32,))

    L, d, HID = NSA_CMP_BLOCK, NSA_CMP_STRIDE, NSA_CMP_HIDDEN
    ncmp = (S - L) // d + 1
    NC = S // d
    eye = jnp.eye(G, dtype=F32)
    w1r = cmp_w1.reshape(2, 2, d, hd, HID)
    w1p = jnp.einsum('ialdj,gh->ialgdhj', w1r, eye).reshape(2, 2, d * G * hd, G * HID).astype(BF16)
    posp = jnp.broadcast_to(cmp_pos.reshape(2, 2, d, 1, hd), (2, 2, d, G, hd)).reshape(2, 2, 1, d * G * hd)
    b1p = jnp.tile(cmp_b1, (1, G)).reshape(2, 1, G * HID)
    w2p = jnp.einsum('ijd,gh->igjhd', _pad_to(cmp_w2, LANES, 2), eye).reshape(2, G * HID, G * LANES).astype(BF16)
    cfeat = np.zeros((NC, G, LANES), np.float32)
    cfeat[:, :, hd + 0] = 1.0
    cfeat[:, :, hd + 1] = 1.0
    cfeat[:, :, hd + 2] = (np.arange(NC) * d)[:, None]
    cfeat[:, :, hd + 3] = L - 1
    kvc = _cmp_mlp(kcmp.reshape(B, NC, d * NSA_KV_W), vcmp.reshape(B, NC, d * NSA_KV_W),
                   w1p, posp, b1p, w2p, jnp.asarray(cfeat.reshape(NC, G * LANES)), ncmp)

    ns = S // NSA_SEL_BLOCK
    cmp_start = np.arange(NC) * d
    sel_start = np.arange(FEAT) * NSA_SEL_BLOCK
    ovt = ((cmp_start[None, :] <= sel_start[:, None] + NSA_SEL_BLOCK - 1)
           & (cmp_start[None, :] + L - 1 >= sel_start[:, None])
           & (np.arange(NC)[None, :] < ncmp) & (np.arange(FEAT)[:, None] < ns))
    o_cmp, bias_sel = _cmp_attn(nq_t, kvc, jnp.asarray(ovt, BF16), B=B, S=S)

    feats_sel = _key_features(S, NSA_SEL_BLOCK)
    o_sel = _flash(nq_t, bias_sel, ksel, feats_sel, vsel_t,
                   B=B, S=S, units=G, tq=TQ, nh=NSA_HG, pair=False, window=None)
    o_win = _flash(nq_t, jnp.zeros_like(bias_sel), kwin, feats_sel, vwin_t,
                   B=B, S=S, units=G, tq=TQ, nh=NSA_HG, pair=False, window=NSA_WINDOW)

    avg = np.zeros((MOBA_NBP, S), np.float32)
    avg[np.arange(S) // MOBA_BLOCK, np.arange(S)] = 1.0 / MOBA_BLOCK
    bias_m = _moba_gate(mq_t, mk, jnp.asarray(avg, BF16), B=B, S=S)
    o_m = _flash(mq_t, bias_m, mk, _key_features(S, MOBA_BLOCK), mv_t,
                 B=B, S=S, units=MOBA_HEADS // 2, tq=TQM, nh=2, pair=True, window=None)

    n = np.arange(NSA_W)[:, None]
    c = np.arange(GATE_ROWS)[None, :]
    expand = jnp.asarray(np.stack([(c == 3 * (n // hd) + br) for br in range(3)]), BF16)
    return _even_out(x2d, o_cmp, o_sel, o_win, o_m, gate_t, nz_t, mz_t, expand, w_out.astype(BF16),
                     ln_g.reshape(1, D), ln_b.reshape(1, D), alpha, B=B, S=S)


def _odd_layer(x2d, B, S, w_in, gate_w2, gate_b, gn_g, w_out, ln_g, ln_b, alpha):
    T, D = x2d.shape
    H, dk, dv, C = GLA_HEADS, GLA_DK, GLA_DV, GLA_CHUNK
    wq, wk, wv, wg, wz = jnp.split(w_in, np.cumsum(ODD_SPLITS)[:-1].tolist(), axis=1)
    w_perm = jnp.concatenate([wq, wk, wv, wz, _pad_to(wg, LANES, 1)], axis=1).astype(BF16)
    q, k, v, z, glr = _proj(x2d, w_perm, (H * dk, H * dk, H * dv, H * dv, LANES),
                            (BF16, BF16, BF16, BF16, F32))
    w2p = _pad_to(gate_w2, LANES, 0)
    nch = min(GLA_STEP_CHUNKS, S // C)
    tril = jnp.asarray(np.kron(np.eye(nch), np.tril(np.ones((C, C)))), BF16)
    o = _gla(q, k, v, glr, w2p, gate_b.reshape(1, H * dk), tril, B=B, S=S)
    return _odd_out(x2d, o, z, gn_g.reshape(1, H * dv), w_out.astype(BF16),
                    ln_g.reshape(1, D), ln_b.reshape(1, D), alpha)


def kernel(x, ev_w_in, ev_cmp_pos, ev_cmp_w1, ev_cmp_b1, ev_cmp_w2, ev_w_out, ev_ln_g, ev_ln_b,
           od_w_in, od_gate_w2, od_gate_b, od_gn_g, od_w_out, od_ln_g, od_ln_b):
    B, S, D = x.shape
    depth = ev_w_in.shape[0] + od_w_in.shape[0]
    alpha = (2.0 * depth) ** 0.25
    h = x.reshape(B * S, D)
    for layer in range(depth):
        i = layer // 2
        if layer % 2 == 0:
            h = _even_layer(h, B, S, ev_w_in[i], ev_cmp_pos[i], ev_cmp_w1[i], ev_cmp_b1[i], ev_cmp_w2[i],
                            ev_w_out[i], ev_ln_g[i], ev_ln_b[i], alpha)
        else:
            h = _odd_layer(h, B, S, od_w_in[i], od_gate_w2[i], od_gate_b[i], od_gn_g[i],
                           od_w_out[i], od_ln_g[i], od_ln_b[i], alpha)
    return h.reshape(B, S, D)
```

```python
import functools

import numpy as np
import jax
import jax.numpy as jnp
from jax import lax
from jax.experimental import pallas as pl
from jax.experimental.pallas import tpu as pltpu

F32 = jnp.float32
BF16 = jnp.bfloat16

HEAD_DIM = 64
NSA_HEADS = 8
NSA_KV_GROUPS = 2
NSA_HG = NSA_HEADS // NSA_KV_GROUPS
NSA_CMP_BLOCK = 32
NSA_CMP_STRIDE = 16
NSA_CMP_HIDDEN = 128
NSA_SEL_BLOCK = 64
NSA_TOP_N = 16
NSA_WINDOW = 512
MOBA_HEADS = 8
MOBA_BLOCK = 256
MOBA_TOP_K = 3
GLA_HEADS = 4
GLA_DK = 128
GLA_DV = 256
GLA_GATE_RANK = 16
GLA_TAU = 16.0
GLA_CHUNK = 64
GLA_SUB = 16
GLA_STEP_CHUNKS = 8
GLA_CUMSUM_CHUNKS = 4
GLA_FACTOR_MAX_DECAY = 60.0
LN_EPS = 1e-5
NEG = -1e30
FORCE_BONUS = 1e4

NSA_W = NSA_HEADS * HEAD_DIM
NSA_KV_W = NSA_KV_GROUPS * HEAD_DIM
MOBA_W = MOBA_HEADS * HEAD_DIM
EVEN_SPLITS = (NSA_W, 6 * NSA_KV_W, 3 * NSA_HEADS, NSA_W, MOBA_W, MOBA_W, MOBA_W, MOBA_W)
ODD_SPLITS = (GLA_HEADS * GLA_DK, GLA_HEADS * GLA_DK, GLA_HEADS * GLA_DV, GLA_GATE_RANK, GLA_HEADS * GLA_DV)

LANES = 128
SUBLANES = 8
FEAT = 64
ALIBI_ROWS = 16
GATE_ROWS = 32
MOBA_NBP = 16
TQ = 512
TQM = 512
TK = 512
CG = 256
MIN_WEIGHT_SUM = 1e-25
VROWS = HEAD_DIM + 16
VMEM_LIMIT = 48 * 1024 * 1024


def _dot(a, b):
    return jnp.dot(a, b, preferred_element_type=F32)


def _dot_nt(a, b):
    return lax.dot_general(a, b, (((1,), (1,)), ((), ())), preferred_element_type=F32)


def _dot_tn(a, b):
    return lax.dot_general(a, b, (((0,), (0,)), ((), ())), preferred_element_type=F32)


def _hilo(a):
    hi = a.astype(BF16)
    lo = (a - hi.astype(F32)).astype(BF16)
    return hi, lo


def _pow2_neg(n):
    return lax.bitcast_convert_type(lax.shift_left(127 - n, 23), F32)


def _sigmoid(x):
    return 0.5 * jnp.tanh(0.5 * x) + 0.5


def _layer_norm(r, g, b):
    mu = jnp.mean(r, axis=-1, keepdims=True)
    d = r - mu
    var = jnp.mean(d * d, axis=-1, keepdims=True)
    return d * lax.rsqrt(var + LN_EPS) * g + b


def _alibi_query_rows(t0, tq, head0, R):
    r = lax.broadcasted_iota(jnp.int32, (ALIBI_ROWS, R), 0)
    col = lax.broadcasted_iota(jnp.int32, (ALIBI_ROWS, R), 1)
    t = t0 + col % tq
    slope = _pow2_neg(head0 + col // tq + 1)
    feats = jnp.where(r == 0, -FEAT * slope * (t // FEAT).astype(F32),
                      jnp.where(r == 1, -slope * (t % FEAT).astype(F32), jnp.where(r < 4, slope, 0.0)))
    return jnp.concatenate([feats.astype(BF16), jnp.zeros((FEAT - ALIBI_ROWS, R), BF16)], axis=0)


def _ranks(sc_ref, n_real):
    n_rows, width = sc_ref.shape
    out = []
    for v in range(n_rows // SUBLANES):
        lo = v * SUBLANES
        sc = sc_ref[lo:lo + SUBLANES, :]
        idx = lo + lax.broadcasted_iota(jnp.int32, (SUBLANES, width), 0)
        rank = jnp.zeros((SUBLANES, width), jnp.int32)
        for jp in range(n_real):
            r = sc_ref[pl.ds(jp, 1), :]
            if jp < lo:
                beats = r >= sc
            elif jp >= lo + SUBLANES - 1:
                beats = r > sc
            else:
                beats = (r > sc) | ((r == sc) & (idx > jp))
            rank = rank + jnp.where(beats, 1, 0)
        out.append(rank)
    return jnp.concatenate(out, axis=0)


def _proj_kernel(*refs, tok_splits, tr_splits):
    x_ref, wt_ref = refs[0], refs[1]
    outs = list(refs[3:] if tr_splits else refs[2:])
    x = x_ref[...]
    xb = x.astype(BF16)
    off = 0
    for n in tok_splits:
        o_ref = outs.pop(0)
        o_ref[...] = _dot(xb, wt_ref[:, off:off + n]).astype(o_ref.dtype)
        off += n
    if tr_splits:
        wtr_ref = refs[2]
        xt = x.T.astype(BF16)
        off = 0
        for n in tr_splits:
            o_ref = outs.pop(0)
            o_ref[...] = _dot(wtr_ref[off:off + n, :], xt).astype(o_ref.dtype)
            off += n


def _proj(x2d, w_tok, tok_splits, tok_dtypes, w_tr=None, tr_splits=(), tr_dtypes=(), tm=512):
    T, D = x2d.shape
    assert T % tm == 0 and sum(tok_splits) == w_tok.shape[1]
    in_specs = [pl.BlockSpec((tm, D), lambda i: (i, 0)),
                pl.BlockSpec(w_tok.shape, lambda i: (0, 0))]
    args = [x2d, w_tok]
    if tr_splits:
        assert sum(tr_splits) == w_tr.shape[0]
        in_specs.append(pl.BlockSpec(w_tr.shape, lambda i: (0, 0)))
        args.append(w_tr)
    out_specs = ([pl.BlockSpec((tm, n), lambda i: (i, 0)) for n in tok_splits]
                 + [pl.BlockSpec((n, tm), lambda i: (0, i)) for n in tr_splits])
    out_shape = ([jax.ShapeDtypeStruct((T, n), dt) for n, dt in zip(tok_splits, tok_dtypes)]
                 + [jax.ShapeDtypeStruct((n, T), dt) for n, dt in zip(tr_splits, tr_dtypes)])
    return pl.pallas_call(
        functools.partial(_proj_kernel, tok_splits=tuple(tok_splits), tr_splits=tuple(tr_splits)),
        grid=(T // tm,), in_specs=in_specs, out_specs=out_specs, out_shape=out_shape,
        compiler_params=pltpu.CompilerParams(dimension_semantics=("parallel",),
                                             vmem_limit_bytes=VMEM_LIMIT),
        name="proj",
    )(*args)


def _cmp_mlp_kernel(kx_ref, vx_ref, w1_ref, pos_ref, b1_ref, w2_ref, cf_ref, o_ref, *, ncmp):
    for i, x_ref in enumerate((kx_ref, vx_ref)):
        x = x_ref[...]
        nc, cw = x.shape
        a = _dot(x, w1_ref[i, 0])
        bm = _dot(x, w1_ref[i, 1])
        c = b1_ref[i]
        for half in range(2):
            phi, plo = _hilo(jnp.broadcast_to(pos_ref[i, half], (8, cw)))
            c = c + (_dot(phi, w1_ref[i, half]) + _dot(plo, w1_ref[i, half]))[0:1]
        hid = a + pltpu.roll(bm, nc - 1, axis=0) + c
        hid = hid * _sigmoid(hid)
        row = lax.broadcasted_iota(jnp.int32, hid.shape, 0)
        hid = jnp.where(row < ncmp, hid, 0.0)
        out = _dot(hid.astype(BF16), w2_ref[i])
        if i == 0:
            out = out + cf_ref[...]
        o_ref[i] = out.astype(o_ref.dtype)


def _cmp_mlp(kx, vx, w1p, posp, b1p, w2p, cfeat, ncmp):
    B, NC, CW = kx.shape
    OW = w2p.shape[-1]
    full = lambda a: pl.BlockSpec(a.shape, lambda b: (0,) * a.ndim)
    return pl.pallas_call(
        functools.partial(_cmp_mlp_kernel, ncmp=ncmp),
        grid=(B,),
        in_specs=[pl.BlockSpec((None, NC, CW), lambda b: (b, 0, 0)),
                  pl.BlockSpec((None, NC, CW), lambda b: (b, 0, 0)),
                  full(w1p), full(posp), full(b1p), full(w2p), full(cfeat)],
        out_specs=pl.BlockSpec((2, None, NC, OW), lambda b: (0, b, 0, 0)),
        out_shape=jax.ShapeDtypeStruct((2, B, NC, OW), BF16),
        compiler_params=pltpu.CompilerParams(dimension_semantics=("parallel",),
                                             vmem_limit_bytes=VMEM_LIMIT),
        name="cmp_mlp",
    )(kx, vx, w1p, posp, b1p, w2p, cfeat)


def _cmp_attn_kernel(q_ref, kc_ref, vc_ref, ovt_ref, o_ref, bias_ref, sc_ref, *, tq, nh, ns, ntop):
    g = pl.program_id(1)
    R = nh * tq
    nc = kc_ref.shape[0]
    nq = q_ref.shape[1] // tq

    def q_tile(qi, carry):
        toks = pl.ds(pl.multiple_of(qi * tq, tq), tq)
        qcat = jnp.concatenate([q_ref[hh * HEAD_DIM:(hh + 1) * HEAD_DIM, toks] for hh in range(nh)], axis=1)
        qa = jnp.concatenate([qcat, _alibi_query_rows(qi * tq, tq, g * nh, R)], axis=0)
        s = _dot(kc_ref[...], qa)
        n = lax.broadcasted_iota(jnp.int32, (nc, R), 0)
        t_row = qi * tq + lax.broadcasted_iota(jnp.int32, (1, R), 1) % tq
        last = (t_row - (NSA_CMP_BLOCK - 1)) // NSA_CMP_STRIDE
        s = jnp.where(n <= last, s, NEG)
        e = jnp.exp(s - jnp.max(s, axis=0, keepdims=True))
        inv = jnp.where(last >= 0, 1.0 / jnp.sum(e, axis=0, keepdims=True), 0.0)
        p = e * inv
        o_ref[:, pl.ds(pl.multiple_of(qi * R, R), R)] = _dot_tn(vc_ref[:, 0:HEAD_DIM],
                                                                 p.astype(BF16)).astype(o_ref.dtype)

        psum = p[:, 0:tq]
        for hh in range(1, nh):
            psum = psum + p[:, hh * tq:(hh + 1) * tq]
        phi, plo = _hilo(psum)
        imp = _dot(ovt_ref[...], phi) + _dot(ovt_ref[...], plo)

        j = lax.broadcasted_iota(jnp.int32, (FEAT, tq), 0)
        cur = (qi * tq + lax.broadcasted_iota(jnp.int32, (FEAT, tq), 1)) // NSA_SEL_BLOCK
        forced = jnp.where((j == 0) | (j == cur) | (j == cur - 1), FORCE_BONUS, 0.0)
        sc = jnp.where(j <= cur, imp + forced, NEG)
        sc_ref[...] = sc
        sel = (_ranks(sc_ref, ns) < ntop) & (j <= cur)
        bias_ref[:, toks] = jnp.where(sel, 0.0, NEG).astype(bias_ref.dtype)
        return carry

    lax.fori_loop(0, nq, q_tile, 0)


def _cmp_attn(nq_t, kvc, ovt, *, B, S):
    tq, nh, G = TQ, NSA_HG, NSA_KV_GROUPS
    R = nh * tq
    nq = S // tq
    NC = kvc.shape[2]
    ns = S // NSA_SEL_BLOCK
    assert ns <= FEAT
    ntop = min(NSA_TOP_N, ns)
    return pl.pallas_call(
        functools.partial(_cmp_attn_kernel, tq=tq, nh=nh, ns=ns, ntop=ntop),
        grid=(B, G),
        in_specs=[pl.BlockSpec((nh * HEAD_DIM, S), lambda b, g: (g, b)),
                  pl.BlockSpec((None, None, NC, LANES), lambda b, g: (0, b, 0, g)),
                  pl.BlockSpec((None, None, NC, LANES), lambda b, g: (1, b, 0, g)),
                  pl.BlockSpec(ovt.shape, lambda b, g: (0, 0))],
        out_specs=[pl.BlockSpec((None, None, HEAD_DIM, S * nh), lambda b, g: (b, g, 0, 0)),
                   pl.BlockSpec((None, None, FEAT, S), lambda b, g: (b, g, 0, 0))],
        out_shape=[jax.ShapeDtypeStruct((B, G, HEAD_DIM, S * nh), BF16),
                   jax.ShapeDtypeStruct((B, G, FEAT, S), BF16)],
        scratch_shapes=[pltpu.VMEM((FEAT, tq), F32)],
        compiler_params=pltpu.CompilerParams(dimension_semantics=("parallel", "parallel"),
                                             vmem_limit_bytes=VMEM_LIMIT),
        name="cmp_attn",
    )(nq_t, kvc, kvc, ovt)


def _moba_gate_kernel(q_ref, k_ref, a_ref, bias_ref, km_ref, sc_ref, *, tq, nb, ntop):
    h = pl.program_id(1)
    qi = pl.program_id(2)

    @pl.when(qi == 0)
    def _means():
        km_ref[...] = _dot(a_ref[...], k_ref[...])

    nbp = km_ref.shape[0]
    lane = lax.broadcasted_iota(jnp.int32, (nbp, LANES), 1)
    km = jnp.where(lane // HEAD_DIM == h % 2, km_ref[...], 0.0)
    khi, klo = _hilo(km)
    q = q_ref[...]
    qq = jnp.concatenate([q, q], axis=0)
    g_t = _dot(khi, qq) + _dot(klo, qq)
    n = lax.broadcasted_iota(jnp.int32, (nbp, tq), 0)
    cur = (qi * tq + lax.broadcasted_iota(jnp.int32, (nbp, tq), 1)) // MOBA_BLOCK
    sc = jnp.where(n < cur, g_t, NEG)
    sc_ref[...] = sc
    sel = ((_ranks(sc_ref, nb) < ntop) & (n < cur)) | (n == cur)
    bias_ref[0:nbp, :] = jnp.where(sel, 0.0, NEG).astype(bias_ref.dtype)
    bias_ref[nbp:FEAT, :] = jnp.zeros((FEAT - nbp, tq), bias_ref.dtype)


def _moba_gate(mq_t, mk, avg, *, B, S):
    tq, H = S, MOBA_HEADS
    nq = S // tq
    nb = S // MOBA_BLOCK
    nbp = avg.shape[0]
    assert nb <= nbp
    ntop = min(MOBA_TOP_K, nb)
    return pl.pallas_call(
        functools.partial(_moba_gate_kernel, tq=tq, nb=nb, ntop=ntop),
        grid=(B, H, nq),
        in_specs=[pl.BlockSpec((HEAD_DIM, tq), lambda b, h, i: (h, b * nq + i)),
                  pl.BlockSpec((S, LANES), lambda b, h, i: (b, h // 2)),
                  pl.BlockSpec(avg.shape, lambda b, h, i: (0, 0))],
        out_specs=pl.BlockSpec((None, None, FEAT, tq), lambda b, h, i: (b, h, 0, i)),
        out_shape=jax.ShapeDtypeStruct((B, H, FEAT, S), BF16),
        scratch_shapes=[pltpu.VMEM((nbp, LANES), F32), pltpu.VMEM((nbp, tq), F32)],
        compiler_params=pltpu.CompilerParams(dimension_semantics=("parallel", "parallel", "arbitrary"),
                                             vmem_limit_bytes=VMEM_LIMIT),
        name="moba_gate",
    )(mq_t, mk, avg)


def _flash_q_tile(qi, q_ref, bias_ref, mask_ref, o_ref, ka_ref, vta_ref, kn_ref, qa_ref, bnd_ref, m_ref,
                  acc_ref, alpha_ref, p_ref, *, tq, tk, nh, pair, window):
    u = pl.program_id(1)
    R = nh * tq
    q0 = pl.multiple_of(qi * tq, tq)
    toks = pl.ds(q0, tq)

    qcat = jnp.concatenate([q_ref[hh * HEAD_DIM:(hh + 1) * HEAD_DIM, toks] for hh in range(nh)], axis=1)
    qf = qcat.astype(F32)
    qq = jnp.concatenate([qf, qf], axis=0)
    rowi = lax.broadcasted_iota(jnp.int32, (2 * HEAD_DIM, R), 0)
    coli = lax.broadcasted_iota(jnp.int32, (2 * HEAD_DIM, R), 1)
    half = coli // tq if pair else u % 2
    qa_ref[0:2 * HEAD_DIM, :] = jnp.where(rowi // HEAD_DIM == half, qq, 0.0).astype(BF16)
    qa_ref[2 * HEAD_DIM:2 * HEAD_DIM + FEAT, :] = jnp.concatenate(
        [bias_ref[hh if pair else 0, :, toks] for hh in range(nh)], axis=1)
    qa_ref[2 * HEAD_DIM + FEAT:2 * LANES, :] = _alibi_query_rows(q0, tq, u * nh, R)
    kn = [jnp.concatenate([kn_ref[hf:hf + 1, :]] * (R // LANES), axis=1) for hf in range(2)]
    half_row = lax.broadcasted_iota(jnp.int32, (1, R), 1) // tq if pair else u % 2
    qn = jnp.sqrt(jnp.sum(qf * qf, axis=0, keepdims=True))
    bnd_ref[...] = jnp.broadcast_to(qn * jnp.where(half_row == 0, kn[0], kn[1]), bnd_ref.shape)

    ncg = R // CG
    kd = (q0 + tq - 1) // tk

    def key_rows(kind, c):
        off = (c * CG) % tq
        if kind is None:
            return 0, tk, None
        if kind == "diag":
            return 0, off + CG, off // CG
        return off, tk, (tk + off) // CG

    def sweep(running_max):
        acc_ref[...] = jnp.zeros(acc_ref.shape, F32)
        if running_max:
            m_ref[...] = jnp.full(m_ref.shape, NEG, F32)

        def values(ki, kind=None):
            for c in range(ncg):
                cols = pl.ds(c * CG, CG)
                r0, r1, _ = key_rows(kind, c)
                v = (c * CG) // tq if pair else 0
                pv = _dot(vta_ref[v, ki, :, r0:r1], p_ref[r0:r1, cols])
                if running_max:
                    acc_ref[:, cols] = alpha_ref[0:1, cols] * acc_ref[:, cols] + pv
                else:
                    acc_ref[:, cols] = acc_ref[:, cols] + pv

        def scores(ki, kind=None, prev=None, prev_kind=None):
            k0 = pl.multiple_of(ki * tk, tk)
            s_groups = []
            for c in range(ncg):
                r0, r1, _ = key_rows(kind, c)
                s_groups.append(_dot(ka_ref[pl.ds(k0 + r0, r1 - r0), :], qa_ref[:, pl.ds(c * CG, CG)]))
            if prev is not None:
                values(prev, prev_kind)
            for c, s in enumerate(s_groups):
                cols = pl.ds(c * CG, CG)
                r0, r1, entry = key_rows(kind, c)
                if entry is not None:
                    s = s + mask_ref[entry, r0:r1, :]
                if running_max:
                    m_prev = m_ref[0:1, cols]
                    m_new = jnp.maximum(m_prev, jnp.max(s, axis=0, keepdims=True))
                    alpha_ref[:, cols] = jnp.broadcast_to(jnp.exp(m_prev - m_new), (alpha_ref.shape[0], CG))
                    m_ref[:, cols] = jnp.broadcast_to(m_new, (m_ref.shape[0], CG))
                    p_ref[r0:r1, cols] = jnp.exp(s - m_new).astype(BF16)
                else:
                    p_ref[r0:r1, cols] = jnp.exp(s - bnd_ref[0:1, cols]).astype(BF16)

        def steady(lo, hi):
            def pair_body(j, carry):
                ki = lo + 2 * j
                scores(ki, prev=ki - 1)
                scores(ki + 1, prev=ki)
                return carry
            lax.fori_loop(0, (hi - lo) // 2, pair_body, 0)

            @pl.when((hi - lo) % 2 == 1)
            def _odd_one():
                scores(hi - 1, prev=hi - 2)

        @pl.when(kd == 0)
        def _only_diagonal():
            scores(0, "diag")
            values(0, "diag")

        @pl.when(kd > 0)
        def _past_then_diagonal():
            if window is None:
                scores(0)
                steady(1, kd)
                scores(kd, "diag", prev=kd - 1)
            else:
                scores(kd - 1, "wedge")
                scores(kd, "diag", prev=kd - 1, prev_kind="wedge")
            values(kd, "diag")

    def emit():
        acc = acc_ref[...]
        out = (acc[0:HEAD_DIM] / acc[HEAD_DIM:HEAD_DIM + 1]).astype(o_ref.dtype)
        if pair:
            for hh in range(nh):
                o_ref[hh, :, toks] = out[:, hh * tq:(hh + 1) * tq]
        else:
            o_ref[0, :, pl.ds(pl.multiple_of(qi * R, R), R)] = out

    sweep(False)
    trusted = jnp.min(acc_ref[HEAD_DIM:HEAD_DIM + 1, :]) >= MIN_WEIGHT_SUM

    @pl.when(trusted)
    def _done():
        emit()

    @pl.when(jnp.logical_not(trusted))
    def _redo_with_running_max():
        sweep(True)
        emit()


def _flash_kernel(q_ref, bias_ref, k_ref, kf_ref, vt_ref, mask_ref, o_ref, ka_ref, vta_ref, kn_ref, qa_ref,
                  bnd_ref, m_ref, acc_ref, alpha_ref, p_ref, *, tq, tk, nh, pair, window):
    S = k_ref.shape[0]
    k = k_ref[...]
    ka_ref[:, 0:LANES] = k
    ka_ref[:, LANES:2 * LANES] = kf_ref[...]
    k2 = jnp.square(k.astype(F32))
    lane = lax.broadcasted_iota(jnp.int32, k2.shape, 1)
    for hf in range(2):
        n2 = jnp.sum(jnp.where(lane // HEAD_DIM == hf, k2, 0.0), axis=1, keepdims=True)
        kn_ref[hf:hf + 1, :] = jnp.broadcast_to(jnp.sqrt(jnp.max(n2, axis=0, keepdims=True)), (1, LANES))
    ones_row = jnp.where(lax.broadcasted_iota(jnp.int32, (VROWS - HEAD_DIM, tk), 0) == 0, 1.0, 0.0)
    for v in range(nh if pair else 1):
        for jt in range(S // tk):
            vta_ref[v, jt, 0:HEAD_DIM, :] = vt_ref[v * HEAD_DIM:(v + 1) * HEAD_DIM, jt * tk:(jt + 1) * tk]
            vta_ref[v, jt, HEAD_DIM:VROWS, :] = ones_row.astype(BF16)

    def q_tile(qi, carry):
        _flash_q_tile(qi, q_ref, bias_ref, mask_ref, o_ref, ka_ref, vta_ref, kn_ref, qa_ref, bnd_ref, m_ref,
                      acc_ref, alpha_ref, p_ref, tq=tq, tk=tk, nh=nh, pair=pair, window=window)
        return carry

    lax.fori_loop(0, S // tq, q_tile, 0)


def _mask_table(tk, window):
    n_delta = (tk + (window or 0)) // CG
    d = np.arange(n_delta)[:, None, None] * CG + np.arange(CG)[None, None, :] - np.arange(tk)[None, :, None]
    ok = (d >= 0) & ((d < window) if window else True)
    return jnp.asarray(np.where(ok, 0.0, NEG), F32)


def _flash(q_t, bias_t, k, kfeat, v_t, *, B, S, units, tq, nh, pair, window):
    tk = TK
    R = nh * tq
    nq = S // tq
    nv = nh if pair else 1
    nb = nh if pair else 1
    assert tq == tk and S % tk == 0 and tk % CG == 0 and window in (None, tk)
    mask = _mask_table(tk, window)
    kcol = (lambda u: u) if pair else (lambda u: u // 2)
    out_w = tq if pair else R
    out_shape = (B, units * nb, HEAD_DIM, S if pair else S * nh)
    return pl.pallas_call(
        functools.partial(_flash_kernel, tq=tq, tk=tk, nh=nh, pair=pair, window=window),
        grid=(B, units),
        in_specs=[pl.BlockSpec((nh * HEAD_DIM, S), lambda b, u: (u, b)),
                  pl.BlockSpec((None, nb, FEAT, S), lambda b, u: (b, u, 0, 0)),
                  pl.BlockSpec((S, LANES), lambda b, u: (b, kcol(u))),
                  pl.BlockSpec((S, LANES), lambda b, u: (0, 0)),
                  pl.BlockSpec((nv * HEAD_DIM, S), lambda b, u: (u, b)),
                  pl.BlockSpec(mask.shape, lambda b, u: (0, 0, 0))],
        out_specs=pl.BlockSpec((None, nb, HEAD_DIM, out_w * nq), lambda b, u: (b, u, 0, 0)),
        out_shape=jax.ShapeDtypeStruct(out_shape, BF16),
        scratch_shapes=[pltpu.VMEM((S, 2 * LANES), BF16),
                        pltpu.VMEM((nv, S // tk, VROWS, tk), BF16),
                        pltpu.VMEM((8, LANES), F32),
                        pltpu.VMEM((2 * LANES, R), BF16),
                        pltpu.VMEM((8, R), F32),
                        pltpu.VMEM((8, R), F32),
                        pltpu.VMEM((VROWS, R), F32),
                        pltpu.VMEM((8, R), F32),
                        pltpu.VMEM((tk, R), BF16)],
        compiler_params=pltpu.CompilerParams(dimension_semantics=("parallel", "parallel"),
                                             vmem_limit_bytes=VMEM_LIMIT),
        name="flash",
    )(q_t, bias_t, k, kfeat, v_t, mask)


def _even_out_kernel(x_ref, ocmp_ref, osel_ref, owin_ref, om_ref, gate_ref, nz_ref, mz_ref,
                     e_ref, w_ref, g_ref, b_ref, o_ref, *, alpha, tm):
    G, nh = NSA_KV_GROUPS, NSA_HG
    sig = _sigmoid(gate_ref[...])
    shi, slo = _hilo(sig)

    def heads_t(ref):
        return jnp.concatenate([ref[g, :, hh * tm:(hh + 1) * tm] for g in range(G) for hh in range(nh)],
                               axis=0).astype(F32)

    o_nsa = None
    for br, ref in enumerate((ocmp_ref, osel_ref, owin_ref)):
        gb = _dot(e_ref[br], shi) + _dot(e_ref[br], slo)
        term = gb * heads_t(ref)
        o_nsa = term if o_nsa is None else o_nsa + term
    nz = nz_ref[...].astype(F32)
    mz = mz_ref[...].astype(F32)
    o_m = jnp.concatenate([om_ref[h] for h in range(MOBA_HEADS)], axis=0).astype(F32)
    a_t = jnp.concatenate([o_nsa * (nz * _sigmoid(nz)), o_m * (mz * _sigmoid(mz))], axis=0)
    y = _dot_tn(a_t.astype(BF16), w_ref[...])
    o_ref[...] = _layer_norm(alpha * x_ref[...] + y, g_ref[...], b_ref[...])


def _even_out(x2d, ocmp, osel, owin, om, gate_t, nz_t, mz_t, expand, w_out, ln_g, ln_b, alpha, *, B, S):
    T, D = x2d.shape
    tm = TQ
    nq = S // tm
    G, nh, H = NSA_KV_GROUPS, NSA_HG, MOBA_HEADS
    nsa = pl.BlockSpec((None, G, HEAD_DIM, nh * tm), lambda i: (i // nq, 0, 0, i % nq))
    feat = lambda n: pl.BlockSpec((n, tm), lambda i: (0, i))
    full = lambda a: pl.BlockSpec(a.shape, lambda i: (0,) * a.ndim)
    return pl.pallas_call(
        functools.partial(_even_out_kernel, alpha=alpha, tm=tm),
        grid=(T // tm,),
        in_specs=[pl.BlockSpec((tm, D), lambda i: (i, 0)), nsa, nsa, nsa,
                  pl.BlockSpec((None, H, HEAD_DIM, tm), lambda i: (i // nq, 0, 0, i % nq)),
                  feat(GATE_ROWS), feat(NSA_W), feat(MOBA_W),
                  full(expand), full(w_out), full(ln_g), full(ln_b)],
        out_specs=pl.BlockSpec((tm, D), lambda i: (i, 0)),
        out_shape=jax.ShapeDtypeStruct((T, D), F32),
        compiler_params=pltpu.CompilerParams(dimension_semantics=("parallel",),
                                             vmem_limit_bytes=VMEM_LIMIT),
        name="even_out",
    )(x2d, ocmp, osel, owin, om, gate_t, nz_t, mz_t, expand, w_out, ln_g, ln_b)


def _gla_head_exact(row0, h, q_ref, v_ref, o_ref, st_ref, b_ref, kf_ref, scale):
    C, dk, dv, SUB = GLA_CHUNK, GLA_DK, GLA_DV, GLA_SUB
    rows = lax.broadcasted_iota(jnp.int32, (C, dk), 0)
    sub_i = lax.broadcasted_iota(jnp.int32, (SUB, LANES), 0)
    lane = lax.broadcasted_iota(jnp.int32, (SUB, LANES), 1)
    ones = jnp.ones((dk, LANES), BF16)
    sl = pl.ds(h * dk, dk)
    rs_c = pl.ds(row0, C)
    bh = b_ref[:, sl]
    kf = kf_ref[:, sl]
    qf = q_ref[rs_c, sl].astype(F32) * scale
    vh = v_ref[rs_c, pl.ds(h * dv, dv)]
    st = st_ref[h]
    inter = _dot_nt((qf * jnp.exp(bh)).astype(BF16), st.astype(BF16))
    bl = b_ref[pl.ds(C - 1, 1), sl]
    kd = kf * jnp.exp(bl - bh)

    def off_block(r0, r1, c0, c1):
        ref_row = b_ref[pl.ds(r0, 1), sl]
        qt = qf[r0:r1] * jnp.exp(bh[r0:r1] - ref_row)
        ek = jnp.where((rows >= c0) & (rows < c1), ref_row - bh, NEG)
        kt = kf * jnp.exp(ek)
        return _dot_nt(qt.astype(BF16), kt.astype(BF16))

    lower = off_block(2 * SUB, C, 0, 2 * SUB)
    blocks = [None, off_block(SUB, 2 * SUB, 0, SUB), lower[0:SUB],
              lower[SUB:2 * SUB] + off_block(3 * SUB, C, 2 * SUB, 3 * SUB)]
    pieces = []
    for blk in range(C // SUB):
        r0 = blk * SUB
        q_i = qf[r0:r0 + SUB]
        b_i = bh[r0:r0 + SUB]
        for jj in range(SUB):
            b_j = b_ref[pl.ds(r0 + jj, 1), sl]
            k_j = kf_ref[pl.ds(r0 + jj, 1), sl]
            pieces.append((q_i * k_j * jnp.exp(jnp.minimum(b_i - b_j, 0.0))).astype(BF16))
    sums = _dot(jnp.concatenate(pieces, axis=0), ones)
    for blk in range(C // SUB):
        r0 = blk * SUB
        diag = jnp.zeros((SUB, LANES), F32)
        for jj in range(SUB):
            rs = sums[(r0 + jj) * SUB:(r0 + jj + 1) * SUB]
            diag = jnp.where((lane == r0 + jj) & (sub_i >= jj), rs, diag)
        diag = diag[:, 0:C]
        blocks[blk] = diag if blocks[blk] is None else blocks[blk] + diag
    att = jnp.concatenate(blocks, axis=0)
    o_ref[rs_c, pl.ds(h * dv, dv)] = (inter + _dot(att.astype(BF16), vh)).astype(o_ref.dtype)
    st_ref[h] = st * jnp.exp(bl) + _dot_tn(vh, kd.astype(BF16))


def _gla_head_bounded(row0, h, q_ref, v_ref, o_ref, st_ref, b_ref, kf_ref, scale):
    C, dk, dv = GLA_CHUNK, GLA_DK, GLA_DV
    sl = pl.ds(h * dk, dk)
    rs_c = pl.ds(row0, C)
    bh = b_ref[rs_c, sl]
    kf = kf_ref[rs_c, sl]
    qf = q_ref[rs_c, sl].astype(F32) * scale
    vh = v_ref[rs_c, pl.ds(h * dv, dv)]
    st = st_ref[h]
    qs = (qf * jnp.exp(bh)).astype(BF16)
    kt = (kf * jnp.exp(-bh)).astype(BF16)
    i = lax.broadcasted_iota(jnp.int32, (C, C), 0)
    j = lax.broadcasted_iota(jnp.int32, (C, C), 1)
    att = jnp.where(i >= j, _dot_nt(qs, kt), 0.0)
    inter = _dot_nt(qs, st.astype(BF16))
    o_ref[rs_c, pl.ds(h * dv, dv)] = (inter + _dot(att.astype(BF16), vh)).astype(o_ref.dtype)
    bl = b_ref[pl.ds(row0 + C - 1, 1), sl]
    kd = kf * jnp.exp(bl - bh)
    st_ref[h] = st * jnp.exp(bl) + _dot_tn(vh, kd.astype(BF16))


def _gla_kernel(q_ref, k_ref, v_ref, glr_ref, w2_ref, gb_ref, tril_ref, o_ref,
                st_ref, b_ref, kf_ref, bc_ref, kc_ref, *, scale):
    C, H = GLA_CHUNK, GLA_HEADS
    nch = q_ref.shape[0] // C

    @pl.when(pl.program_id(1) == 0)
    def _init():
        st_ref[...] = jnp.zeros(st_ref.shape, F32)

    whi, wlo = _hilo(w2_ref[...])
    part = tril_ref.shape[0]
    decay = None
    for r0 in range(0, nch * C, part):
        rs = pl.ds(r0, part)
        ghi, glo = _hilo(glr_ref[rs, :])
        pre = _dot(ghi, whi) + _dot(glo, whi) + _dot(ghi, wlo) + gb_ref[...]
        la = (jnp.minimum(pre, 0.0) - jnp.log(1.0 + jnp.exp(-jnp.abs(pre)))) * (1.0 / GLA_TAU)
        lhi, llo = _hilo(la)
        b_part = _dot(tril_ref[...], lhi) + _dot(tril_ref[...], llo)
        b_ref[rs, :] = b_part
        for c in range(part // C):
            d_c = -b_part[c * C + C - 1:(c + 1) * C, :]
            decay = d_c if decay is None else jnp.maximum(decay, d_c)
    kf_ref[...] = k_ref[...].astype(F32)
    bounded = jnp.max(decay) <= GLA_FACTOR_MAX_DECAY

    @pl.when(bounded)
    def _factored():
        for c in range(nch):
            for h in range(H):
                _gla_head_bounded(c * C, h, q_ref, v_ref, o_ref, st_ref, b_ref, kf_ref, scale)

    @pl.when(jnp.logical_not(bounded))
    def _exact():
        def chunk(c, carry):
            row0 = pl.multiple_of(c * C, C)
            bc_ref[...] = b_ref[pl.ds(row0, C), :]
            kc_ref[...] = kf_ref[pl.ds(row0, C), :]
            for h in range(H):
                _gla_head_exact(row0, h, q_ref, v_ref, o_ref, st_ref, bc_ref, kc_ref, scale)
            return carry
        lax.fori_loop(0, nch, chunk, 0)


def _gla(q, k, v, glr, w2p, gb, tril, *, B, S):
    C, H, dk, dv = GLA_CHUNK, GLA_HEADS, GLA_DK, GLA_DV
    rows = C * min(GLA_STEP_CHUNKS, S // C)
    nchunk = S // rows
    tok = lambda n: pl.BlockSpec((rows, n), lambda b, c: (b * nchunk + c, 0))
    const = lambda shape: pl.BlockSpec(shape, lambda b, c: (0,) * len(shape))
    return pl.pallas_call(
        functools.partial(_gla_kernel, scale=float(dk) ** -0.5),
        grid=(B, nchunk),
        in_specs=[tok(H * dk), tok(H * dk), tok(H * dv), tok(LANES),
                  const(w2p.shape), const(gb.shape), const(tril.shape)],
        out_specs=tok(H * dv),
        out_shape=jax.ShapeDtypeStruct((B * S, H * dv), BF16),
        scratch_shapes=[pltpu.VMEM((H, dv, dk), F32),
                        pltpu.VMEM((rows, H * dk), F32),
                        pltpu.VMEM((rows, H * dk), F32),
                        pltpu.VMEM((C, H * dk), F32),
                        pltpu.VMEM((C, H * dk), F32)],
        compiler_params=pltpu.CompilerParams(dimension_semantics=("parallel", "arbitrary"),
                                             vmem_limit_bytes=VMEM_LIMIT),
        name="gla",
    )(q, k, v, glr, w2p, gb, tril)


def _odd_out_kernel(x_ref, o_ref_in, z_ref, gn_ref, w_ref, g_ref, b_ref, out_ref, *, alpha):
    H, dv = GLA_HEADS, GLA_DV
    parts = []
    for h in range(H):
        sl = pl.ds(h * dv, dv)
        oh = o_ref_in[:, sl].astype(F32)
        r = lax.rsqrt(jnp.mean(oh * oh, axis=-1, keepdims=True) + LN_EPS)
        z = z_ref[:, sl].astype(F32)
        parts.append((oh * r * gn_ref[:, sl] * (z * _sigmoid(z))).astype(BF16))
    y = _dot(jnp.concatenate(parts, axis=1), w_ref[...])
    out_ref[...] = _layer_norm(alpha * x_ref[...] + y, g_ref[...], b_ref[...])


def _odd_out(x2d, o, z, gn_g, w_out, ln_g, ln_b, alpha, tm=512):
    T, D = x2d.shape
    W = o.shape[1]
    row = lambda n: pl.BlockSpec((tm, n), lambda i: (i, 0))
    return pl.pallas_call(
        functools.partial(_odd_out_kernel, alpha=alpha),
        grid=(T // tm,),
        in_specs=[row(D), row(W), row(W),
                  pl.BlockSpec((1, W), lambda i: (0, 0)),
                  pl.BlockSpec((W, D), lambda i: (0, 0)),
                  pl.BlockSpec((1, D), lambda i: (0, 0)),
                  pl.BlockSpec((1, D), lambda i: (0, 0))],
        out_specs=row(D),
        out_shape=jax.ShapeDtypeStruct((T, D), F32),
        compiler_params=pltpu.CompilerParams(dimension_semantics=("parallel",),
                                             vmem_limit_bytes=VMEM_LIMIT),
        name="odd_out",
    )(x2d, o, z, gn_g, w_out, ln_g, ln_b)


def _key_features(S, block):
    s = np.arange(S)
    f = np.zeros((S, LANES), np.float32)
    f[s, s // block] = 1.0
    f[:, FEAT + 0] = 1.0
    f[:, FEAT + 1] = 1.0
    f[:, FEAT + 2] = FEAT * (s // FEAT)
    f[:, FEAT + 3] = s % FEAT
    return jnp.asarray(f, BF16)


def _pad_to(a, n, axis):
    pad = [(0, 0)] * a.ndim
    pad[axis] = (0, n - a.shape[axis])
    return jnp.pad(a, pad)


def _even_layer(x2d, B, S, w_in, cmp_pos, cmp_w1, cmp_b1, cmp_w2, w_out, ln_g, ln_b, alpha):
    T, D = x2d.shape
    G, hd = NSA_KV_GROUPS, HEAD_DIM
    scale = hd ** -0.5
    wq, wkv, wgate, wnz, wmq, wmk, wmv, wmz = jnp.split(w_in, np.cumsum(EVEN_SPLITS)[:-1].tolist(), axis=1)
    wkc, wvc, wks, wvs, wkw, wvw = jnp.split(wkv, 6, axis=1)
    w_tok = jnp.concatenate([wkc, wvc, wks, wkw, wmk], axis=1).astype(BF16)
    w_tr = jnp.concatenate([wq * scale, wvs, wvw, wmq * scale, wmv, wnz, wmz,
                            _pad_to(wgate, GATE_ROWS, 1)], axis=1).T.astype(BF16)
    (kcmp, vcmp, ksel, kwin, mk, nq_t, vsel_t, vwin_t, mq_t, mv_t, nz_t, mz_t, gate_t) = _proj(
        x2d, w_tok, (NSA_KV_W,) * 4 + (MOBA_W,), (BF16,) * 5,
        w_tr, (NSA_W, NSA_KV_W, NSA_KV_W, MOBA_W, MOBA_W, NSA_W, MOBA_W, GATE_ROWS),
        (BF16,) * 7 + (F32,))

    L, d, HID = NSA_CMP_BLOCK, NSA_CMP_STRIDE, NSA_CMP_HIDDEN
    ncmp = (S - L) // d + 1
    NC = S // d
    eye = jnp.eye(G, dtype=F32)
    w1r = cmp_w1.reshape(2, 2, d, hd, HID)
    w1p = jnp.einsum('ialdj,gh->ialgdhj', w1r, eye).reshape(2, 2, d * G * hd, G * HID).astype(BF16)
    posp = jnp.broadcast_to(cmp_pos.reshape(2, 2, d, 1, hd), (2, 2, d, G, hd)).reshape(2, 2, 1, d * G * hd)
    b1p = jnp.tile(cmp_b1, (1, G)).reshape(2, 1, G * HID)
    w2p = jnp.einsum('ijd,gh->igjhd', _pad_to(cmp_w2, LANES, 2), eye).reshape(2, G * HID, G * LANES).astype(BF16)
    cfeat = np.zeros((NC, G, LANES), np.float32)
    cfeat[:, :, hd + 0] = 1.0
    cfeat[:, :, hd + 1] = 1.0
    cfeat[:, :, hd + 2] = (np.arange(NC) * d)[:, None]
    cfeat[:, :, hd + 3] = L - 1
    kvc = _cmp_mlp(kcmp.reshape(B, NC, d * NSA_KV_W), vcmp.reshape(B, NC, d * NSA_KV_W),
                   w1p, posp, b1p, w2p, jnp.asarray(cfeat.reshape(NC, G * LANES)), ncmp)

    ns = S // NSA_SEL_BLOCK
    cmp_start = np.arange(NC) * d
    sel_start = np.arange(FEAT) * NSA_SEL_BLOCK
    ovt = ((cmp_start[None, :] <= sel_start[:, None] + NSA_SEL_BLOCK - 1)
           & (cmp_start[None, :] + L - 1 >= sel_start[:, None])
           & (np.arange(NC)[None, :] < ncmp) & (np.arange(FEAT)[:, None] < ns))
    o_cmp, bias_sel = _cmp_attn(nq_t, kvc, jnp.asarray(ovt, BF16), B=B, S=S)

    feats_sel = _key_features(S, NSA_SEL_BLOCK)
    o_sel = _flash(nq_t, bias_sel, ksel, feats_sel, vsel_t,
                   B=B, S=S, units=G, tq=TQ, nh=NSA_HG, pair=False, window=None)
    o_win = _flash(nq_t, jnp.zeros_like(bias_sel), kwin, feats_sel, vwin_t,
                   B=B, S=S, units=G, tq=TQ, nh=NSA_HG, pair=False, window=NSA_WINDOW)

    avg = np.zeros((MOBA_NBP, S), np.float32)
    avg[np.arange(S) // MOBA_BLOCK, np.arange(S)] = 1.0 / MOBA_BLOCK
    bias_m = _moba_gate(mq_t, mk, jnp.asarray(avg, BF16), B=B, S=S)
    o_m = _flash(mq_t, bias_m, mk, _key_features(S, MOBA_BLOCK), mv_t,
                 B=B, S=S, units=MOBA_HEADS // 2, tq=TQM, nh=2, pair=True, window=None)

    n = np.arange(NSA_W)[:, None]
    c = np.arange(GATE_ROWS)[None, :]
    expand = jnp.asarray(np.stack([(c == 3 * (n // hd) + br) for br in range(3)]), BF16)
    return _even_out(x2d, o_cmp, o_sel, o_win, o_m, gate_t, nz_t, mz_t, expand, w_out.astype(BF16),
                     ln_g.reshape(1, D), ln_b.reshape(1, D), alpha, B=B, S=S)


def _odd_layer(x2d, B, S, w_in, gate_w2, gate_b, gn_g, w_out, ln_g, ln_b, alpha):
    T, D = x2d.shape
    H, dk, dv, C = GLA_HEADS, GLA_DK, GLA_DV, GLA_CHUNK
    wq, wk, wv, wg, wz = jnp.split(w_in, np.cumsum(ODD_SPLITS)[:-1].tolist(), axis=1)
    w_perm = jnp.concatenate([wq, wk, wv, wz, _pad_to(wg, LANES, 1)], axis=1).astype(BF16)
    q, k, v, z, glr = _proj(x2d, w_perm, (H * dk, H * dk, H * dv, H * dv, LANES),
                            (BF16, BF16, BF16, BF16, F32))
    w2p = _pad_to(gate_w2, LANES, 0)
    ncs = min(GLA_CUMSUM_CHUNKS, GLA_STEP_CHUNKS, S // C)
    tril = jnp.asarray(np.kron(np.eye(ncs), np.tril(np.ones((C, C)))), BF16)
    o = _gla(q, k, v, glr, w2p, gate_b.reshape(1, H * dk), tril, B=B, S=S)
    return _odd_out(x2d, o, z, gn_g.reshape(1, H * dv), w_out.astype(BF16),
                    ln_g.reshape(1, D), ln_b.reshape(1, D), alpha)


def kernel(x, ev_w_in, ev_cmp_pos, ev_cmp_w1, ev_cmp_b1, ev_cmp_w2, ev_w_out, ev_ln_g, ev_ln_b,
           od_w_in, od_gate_w2, od_gate_b, od_gn_g, od_w_out, od_ln_g, od_ln_b):
    B, S, D = x.shape
    depth = ev_w_in.shape[0] + od_w_in.shape[0]
    alpha = (2.0 * depth) ** 0.25
    h = x.reshape(B * S, D)
    for layer in range(depth):
        i = layer // 2
        if layer % 2 == 0:
            h = _even_layer(h, B, S, ev_w_in[i], ev_cmp_pos[i], ev_cmp_w1[i], ev_cmp_b1[i], ev_cmp_w2[i],
                            ev_w_out[i], ev_ln_g[i], ev_ln_b[i], alpha)
        else:
            h = _odd_layer(h, B, S, od_w_in[i], od_gate_w2[i], od_gate_b[i], od_gn_g[i],
                           od_w_out[i], od_ln_g[i], od_ln_b[i], alpha)
    return h.reshape(B, S, D)
```

```python
import functools

import numpy as np
import jax
import jax.numpy as jnp
from jax import lax
from jax.experimental import pallas as pl
from jax.experimental.pallas import tpu as pltpu

F32 = jnp.float32
BF16 = jnp.bfloat16

HEAD_DIM = 64
NSA_HEADS = 8
NSA_KV_GROUPS = 2
NSA_HG = NSA_HEADS // NSA_KV_GROUPS
NSA_CMP_BLOCK = 32
NSA_CMP_STRIDE = 16
NSA_CMP_HIDDEN = 128
NSA_SEL_BLOCK = 64
NSA_TOP_N = 16
NSA_WINDOW = 512
MOBA_HEADS = 8
MOBA_BLOCK = 256
MOBA_TOP_K = 3
GLA_HEADS = 4
GLA_DK = 128
GLA_DV = 256
GLA_GATE_RANK = 16
GLA_TAU = 16.0
GLA_CHUNK = 64
GLA_SUB = 16
GLA_STEP_CHUNKS = 8
GLA_CUMSUM_CHUNKS = 4
GLA_FACTOR_MAX_DECAY = 60.0
LN_EPS = 1e-5
NEG = -1e30
FORCE_BONUS = 1e4

NSA_W = NSA_HEADS * HEAD_DIM
NSA_KV_W = NSA_KV_GROUPS * HEAD_DIM
MOBA_W = MOBA_HEADS * HEAD_DIM
EVEN_SPLITS = (NSA_W, 6 * NSA_KV_W, 3 * NSA_HEADS, NSA_W, MOBA_W, MOBA_W, MOBA_W, MOBA_W)
ODD_SPLITS = (GLA_HEADS * GLA_DK, GLA_HEADS * GLA_DK, GLA_HEADS * GLA_DV, GLA_GATE_RANK, GLA_HEADS * GLA_DV)

LANES = 128
SUBLANES = 8
FEAT = 64
ALIBI_ROWS = 16
GATE_ROWS = 32
MOBA_NBP = 16
TQ = 512
TQM = 512
MOBA_STEP_HEADS = 4
TK = 512
CG = 256
MIN_WEIGHT_SUM = 1e-25
VROWS = HEAD_DIM + 16
VMEM_LIMIT = 48 * 1024 * 1024


def _dot(a, b):
    return jnp.dot(a, b, preferred_element_type=F32)


def _dot_nt(a, b):
    return lax.dot_general(a, b, (((1,), (1,)), ((), ())), preferred_element_type=F32)


def _dot_tn(a, b):
    return lax.dot_general(a, b, (((0,), (0,)), ((), ())), preferred_element_type=F32)


def _hilo(a):
    hi = a.astype(BF16)
    lo = (a - hi.astype(F32)).astype(BF16)
    return hi, lo


def _pow2_neg(n):
    return lax.bitcast_convert_type(lax.shift_left(127 - n, 23), F32)


def _sigmoid(x):
    return 0.5 * jnp.tanh(0.5 * x) + 0.5


def _layer_norm(r, g, b):
    mu = jnp.mean(r, axis=-1, keepdims=True)
    d = r - mu
    var = jnp.mean(d * d, axis=-1, keepdims=True)
    return d * lax.rsqrt(var + LN_EPS) * g + b


def _alibi_query_rows(t0, tq, head0, R):
    r = lax.broadcasted_iota(jnp.int32, (ALIBI_ROWS, R), 0)
    col = lax.broadcasted_iota(jnp.int32, (ALIBI_ROWS, R), 1)
    t = t0 + col % tq
    slope = _pow2_neg(head0 + col // tq + 1)
    feats = jnp.where(r == 0, -FEAT * slope * (t // FEAT).astype(F32),
                      jnp.where(r == 1, -slope * (t % FEAT).astype(F32), jnp.where(r < 4, slope, 0.0)))
    return jnp.concatenate([feats.astype(BF16), jnp.zeros((FEAT - ALIBI_ROWS, R), BF16)], axis=0)


def _ranks(sc_ref, n_real):
    n_rows, width = sc_ref.shape
    out = []
    for v in range(n_rows // SUBLANES):
        lo = v * SUBLANES
        sc = sc_ref[lo:lo + SUBLANES, :]
        idx = lo + lax.broadcasted_iota(jnp.int32, (SUBLANES, width), 0)
        rank = jnp.zeros((SUBLANES, width), jnp.int32)
        for jp in range(n_real):
            r = sc_ref[pl.ds(jp, 1), :]
            if jp < lo:
                beats = r >= sc
            elif jp >= lo + SUBLANES - 1:
                beats = r > sc
            else:
                beats = (r > sc) | ((r == sc) & (idx > jp))
            rank = rank + jnp.where(beats, 1, 0)
        out.append(rank)
    return jnp.concatenate(out, axis=0)


def _proj_kernel(*refs, tok_splits, tr_splits):
    x_ref, wt_ref = refs[0], refs[1]
    outs = list(refs[3:] if tr_splits else refs[2:])
    x = x_ref[...]
    xb = x.astype(BF16)
    off = 0
    for n in tok_splits:
        o_ref = outs.pop(0)
        o_ref[...] = _dot(xb, wt_ref[:, off:off + n]).astype(o_ref.dtype)
        off += n
    if tr_splits:
        wtr_ref = refs[2]
        xt = x.T.astype(BF16)
        off = 0
        for n in tr_splits:
            o_ref = outs.pop(0)
            o_ref[...] = _dot(wtr_ref[off:off + n, :], xt).astype(o_ref.dtype)
            off += n


def _proj(x2d, w_tok, tok_splits, tok_dtypes, w_tr=None, tr_splits=(), tr_dtypes=(), tm=512):
    T, D = x2d.shape
    assert T % tm == 0 and sum(tok_splits) == w_tok.shape[1]
    in_specs = [pl.BlockSpec((tm, D), lambda i: (i, 0)),
                pl.BlockSpec(w_tok.shape, lambda i: (0, 0))]
    args = [x2d, w_tok]
    if tr_splits:
        assert sum(tr_splits) == w_tr.shape[0]
        in_specs.append(pl.BlockSpec(w_tr.shape, lambda i: (0, 0)))
        args.append(w_tr)
    out_specs = ([pl.BlockSpec((tm, n), lambda i: (i, 0)) for n in tok_splits]
                 + [pl.BlockSpec((n, tm), lambda i: (0, i)) for n in tr_splits])
    out_shape = ([jax.ShapeDtypeStruct((T, n), dt) for n, dt in zip(tok_splits, tok_dtypes)]
                 + [jax.ShapeDtypeStruct((n, T), dt) for n, dt in zip(tr_splits, tr_dtypes)])
    return pl.pallas_call(
        functools.partial(_proj_kernel, tok_splits=tuple(tok_splits), tr_splits=tuple(tr_splits)),
        grid=(T // tm,), in_specs=in_specs, out_specs=out_specs, out_shape=out_shape,
        compiler_params=pltpu.CompilerParams(dimension_semantics=("parallel",),
                                             vmem_limit_bytes=VMEM_LIMIT),
        name="proj",
    )(*args)


def _cmp_mlp_kernel(kx_ref, vx_ref, w1_ref, pos_ref, b1_ref, w2_ref, cf_ref, o_ref, *, ncmp):
    for i, x_ref in enumerate((kx_ref, vx_ref)):
        x = x_ref[...]
        nc, cw = x.shape
        a = _dot(x, w1_ref[i, 0])
        bm = _dot(x, w1_ref[i, 1])
        c = b1_ref[i]
        for half in range(2):
            phi, plo = _hilo(jnp.broadcast_to(pos_ref[i, half], (8, cw)))
            c = c + (_dot(phi, w1_ref[i, half]) + _dot(plo, w1_ref[i, half]))[0:1]
        hid = a + pltpu.roll(bm, nc - 1, axis=0) + c
        hid = hid * _sigmoid(hid)
        row = lax.broadcasted_iota(jnp.int32, hid.shape, 0)
        hid = jnp.where(row < ncmp, hid, 0.0)
        out = _dot(hid.astype(BF16), w2_ref[i])
        if i == 0:
            out = out + cf_ref[...]
        o_ref[i] = out.astype(o_ref.dtype)


def _cmp_mlp(kx, vx, w1p, posp, b1p, w2p, cfeat, ncmp):
    B, NC, CW = kx.shape
    OW = w2p.shape[-1]
    full = lambda a: pl.BlockSpec(a.shape, lambda b: (0,) * a.ndim)
    return pl.pallas_call(
        functools.partial(_cmp_mlp_kernel, ncmp=ncmp),
        grid=(B,),
        in_specs=[pl.BlockSpec((None, NC, CW), lambda b: (b, 0, 0)),
                  pl.BlockSpec((None, NC, CW), lambda b: (b, 0, 0)),
                  full(w1p), full(posp), full(b1p), full(w2p), full(cfeat)],
        out_specs=pl.BlockSpec((2, None, NC, OW), lambda b: (0, b, 0, 0)),
        out_shape=jax.ShapeDtypeStruct((2, B, NC, OW), BF16),
        compiler_params=pltpu.CompilerParams(dimension_semantics=("parallel",),
                                             vmem_limit_bytes=VMEM_LIMIT),
        name="cmp_mlp",
    )(kx, vx, w1p, posp, b1p, w2p, cfeat)


def _cmp_attn_kernel(q_ref, kc_ref, vc_ref, ovt_ref, o_ref, bias_ref, sc_ref, *, tq, nh, ns, ntop):
    g = pl.program_id(1)
    R = nh * tq
    nc = kc_ref.shape[0]
    nq = q_ref.shape[1] // tq

    def q_tile(qi, carry):
        toks = pl.ds(pl.multiple_of(qi * tq, tq), tq)
        qcat = jnp.concatenate([q_ref[hh * HEAD_DIM:(hh + 1) * HEAD_DIM, toks] for hh in range(nh)], axis=1)
        qa = jnp.concatenate([qcat, _alibi_query_rows(qi * tq, tq, g * nh, R)], axis=0)
        s = _dot(kc_ref[...], qa)
        n = lax.broadcasted_iota(jnp.int32, (nc, R), 0)
        t_row = qi * tq + lax.broadcasted_iota(jnp.int32, (1, R), 1) % tq
        last = (t_row - (NSA_CMP_BLOCK - 1)) // NSA_CMP_STRIDE
        s = jnp.where(n <= last, s, NEG)
        e = jnp.exp(s - jnp.max(s, axis=0, keepdims=True))
        inv = jnp.where(last >= 0, 1.0 / jnp.sum(e, axis=0, keepdims=True), 0.0)
        p = e * inv
        o_ref[:, pl.ds(pl.multiple_of(qi * R, R), R)] = _dot_tn(vc_ref[:, 0:HEAD_DIM],
                                                                 p.astype(BF16)).astype(o_ref.dtype)

        psum = p[:, 0:tq]
        for hh in range(1, nh):
            psum = psum + p[:, hh * tq:(hh + 1) * tq]
        phi, plo = _hilo(psum)
        imp = _dot(ovt_ref[...], phi) + _dot(ovt_ref[...], plo)

        j = lax.broadcasted_iota(jnp.int32, (FEAT, tq), 0)
        cur = (qi * tq + lax.broadcasted_iota(jnp.int32, (FEAT, tq), 1)) // NSA_SEL_BLOCK
        forced = jnp.where((j == 0) | (j == cur) | (j == cur - 1), FORCE_BONUS, 0.0)
        sc = jnp.where(j <= cur, imp + forced, NEG)
        blocks_so_far = ((qi + 1) * tq - 1) // NSA_SEL_BLOCK + 1

        @pl.when(blocks_so_far <= ntop)
        def _all_valid_blocks_selected():
            bias_ref[:, toks] = jnp.where(j <= cur, 0.0, NEG).astype(bias_ref.dtype)

        @pl.when(blocks_so_far > ntop)
        def _rank():
            sc_ref[...] = sc
            sel = (_ranks(sc_ref, ns) < ntop) & (j <= cur)
            bias_ref[:, toks] = jnp.where(sel, 0.0, NEG).astype(bias_ref.dtype)
        return carry

    lax.fori_loop(0, nq, q_tile, 0)


def _cmp_attn(nq_t, kvc, ovt, *, B, S):
    tq, nh, G = TQ, NSA_HG, NSA_KV_GROUPS
    R = nh * tq
    nq = S // tq
    NC = kvc.shape[2]
    ns = S // NSA_SEL_BLOCK
    assert ns <= FEAT
    ntop = min(NSA_TOP_N, ns)
    return pl.pallas_call(
        functools.partial(_cmp_attn_kernel, tq=tq, nh=nh, ns=ns, ntop=ntop),
        grid=(B, G),
        in_specs=[pl.BlockSpec((nh * HEAD_DIM, S), lambda b, g: (g, b)),
                  pl.BlockSpec((None, None, NC, LANES), lambda b, g: (0, b, 0, g)),
                  pl.BlockSpec((None, None, NC, LANES), lambda b, g: (1, b, 0, g)),
                  pl.BlockSpec(ovt.shape, lambda b, g: (0, 0))],
        out_specs=[pl.BlockSpec((None, None, HEAD_DIM, S * nh), lambda b, g: (b, g, 0, 0)),
                   pl.BlockSpec((None, None, FEAT, S), lambda b, g: (b, g, 0, 0))],
        out_shape=[jax.ShapeDtypeStruct((B, G, HEAD_DIM, S * nh), BF16),
                   jax.ShapeDtypeStruct((B, G, FEAT, S), BF16)],
        scratch_shapes=[pltpu.VMEM((FEAT, tq), F32)],
        compiler_params=pltpu.CompilerParams(dimension_semantics=("parallel", "parallel"),
                                             vmem_limit_bytes=VMEM_LIMIT),
        name="cmp_attn",
    )(nq_t, kvc, kvc, ovt)


def _moba_gate_kernel(q_ref, k_ref, a_ref, bias_ref, km_ref, sc_ref, *, tq, nb, ntop):
    h = pl.program_id(1)
    qi = pl.program_id(2)

    @pl.when(qi == 0)
    def _means():
        km_ref[...] = _dot(a_ref[...], k_ref[...])

    nbp = km_ref.shape[0]
    lane = lax.broadcasted_iota(jnp.int32, (nbp, LANES), 1)
    km = jnp.where(lane // HEAD_DIM == h % 2, km_ref[...], 0.0)
    khi, klo = _hilo(km)
    q = q_ref[...]
    qq = jnp.concatenate([q, q], axis=0)
    g_t = _dot(khi, qq) + _dot(klo, qq)
    n = lax.broadcasted_iota(jnp.int32, (nbp, tq), 0)
    cur = (qi * tq + lax.broadcasted_iota(jnp.int32, (nbp, tq), 1)) // MOBA_BLOCK
    sc = jnp.where(n < cur, g_t, NEG)
    sc_ref[...] = sc
    sel = ((_ranks(sc_ref, nb) < ntop) & (n < cur)) | (n == cur)
    bias_ref[0:nbp, :] = jnp.where(sel, 0.0, NEG).astype(bias_ref.dtype)
    bias_ref[nbp:FEAT, :] = jnp.zeros((FEAT - nbp, tq), bias_ref.dtype)


def _moba_gate(mq_t, mk, avg, *, B, S):
    tq, H = S, MOBA_HEADS
    nq = S // tq
    nb = S // MOBA_BLOCK
    nbp = avg.shape[0]
    assert nb <= nbp
    ntop = min(MOBA_TOP_K, nb)
    return pl.pallas_call(
        functools.partial(_moba_gate_kernel, tq=tq, nb=nb, ntop=ntop),
        grid=(B, H, nq),
        in_specs=[pl.BlockSpec((HEAD_DIM, tq), lambda b, h, i: (h, b * nq + i)),
                  pl.BlockSpec((S, LANES), lambda b, h, i: (b, h // 2)),
                  pl.BlockSpec(avg.shape, lambda b, h, i: (0, 0))],
        out_specs=pl.BlockSpec((None, None, FEAT, tq), lambda b, h, i: (b, h, 0, i)),
        out_shape=jax.ShapeDtypeStruct((B, H, FEAT, S), BF16),
        scratch_shapes=[pltpu.VMEM((nbp, LANES), F32), pltpu.VMEM((nbp, tq), F32)],
        compiler_params=pltpu.CompilerParams(dimension_semantics=("parallel", "parallel", "arbitrary"),
                                             vmem_limit_bytes=VMEM_LIMIT),
        name="moba_gate",
    )(mq_t, mk, avg)


def _flash_q_tile(qi, q_ref, bias_ref, mask_ref, o_ref, ka_ref, vta_ref, kn_ref, qa_ref, bnd_ref, m_ref,
                  acc_ref, alpha_ref, p_ref, *, tq, tk, nh, pair, window):
    u = pl.program_id(1)
    R = nh * tq
    q0 = pl.multiple_of(qi * tq, tq)
    toks = pl.ds(q0, tq)

    qcat = jnp.concatenate([q_ref[hh * HEAD_DIM:(hh + 1) * HEAD_DIM, toks] for hh in range(nh)], axis=1)
    qf = qcat.astype(F32)
    qq = jnp.concatenate([qf, qf], axis=0)
    rowi = lax.broadcasted_iota(jnp.int32, (2 * HEAD_DIM, R), 0)
    coli = lax.broadcasted_iota(jnp.int32, (2 * HEAD_DIM, R), 1)
    half = (coli // tq) % 2 if pair else u % 2
    qa_ref[0:2 * HEAD_DIM, :] = jnp.where(rowi // HEAD_DIM == half, qq, 0.0).astype(BF16)
    qa_ref[2 * HEAD_DIM:2 * HEAD_DIM + FEAT, :] = jnp.concatenate(
        [bias_ref[hh if pair else 0, :, toks] for hh in range(nh)], axis=1)
    qa_ref[2 * HEAD_DIM + FEAT:2 * LANES, :] = _alibi_query_rows(q0, tq, u * nh, R)
    if pair:
        kn = jnp.concatenate([kn_ref[hh:hh + 1, :] for hh in range(nh) for _ in range(tq // LANES)], axis=1)
    else:
        kn = jnp.concatenate([jnp.where(u % 2 == 0, kn_ref[0:1, :], kn_ref[1:2, :])] * (R // LANES), axis=1)
    qn = jnp.sqrt(jnp.sum(qf * qf, axis=0, keepdims=True))
    bnd_ref[...] = jnp.broadcast_to(qn * kn, bnd_ref.shape)

    ncg = R // CG
    kd = (q0 + tq - 1) // tk

    def key_rows(kind, c):
        off = (c * CG) % tq
        if kind is None:
            return 0, tk, None
        if kind == "diag":
            return 0, off + CG, off // CG
        return off, tk, (tk + off) // CG

    def sweep(running_max):
        acc_ref[...] = jnp.zeros(acc_ref.shape, F32)
        if running_max:
            m_ref[...] = jnp.full(m_ref.shape, NEG, F32)

        def values(ki, kind=None):
            for c in range(ncg):
                cols = pl.ds(c * CG, CG)
                r0, r1, _ = key_rows(kind, c)
                v = (c * CG) // tq if pair else 0
                pv = _dot(vta_ref[v, ki, :, r0:r1], p_ref[r0:r1, cols])
                if running_max:
                    acc_ref[:, cols] = alpha_ref[0:1, cols] * acc_ref[:, cols] + pv
                else:
                    acc_ref[:, cols] = acc_ref[:, cols] + pv

        def scores(ki, kind=None, prev=None, prev_kind=None):
            k0 = pl.multiple_of(ki * tk, tk)
            s_groups = []
            for c in range(ncg):
                r0, r1, _ = key_rows(kind, c)
                kb = (c * CG) // (2 * tq) if pair else 0
                s_groups.append(_dot(ka_ref[kb, pl.ds(k0 + r0, r1 - r0), :], qa_ref[:, pl.ds(c * CG, CG)]))
            if prev is not None:
                values(prev, prev_kind)
            for c, s in enumerate(s_groups):
                cols = pl.ds(c * CG, CG)
                r0, r1, entry = key_rows(kind, c)
                if entry is not None:
                    s = s + mask_ref[entry, r0:r1, :]
                if running_max:
                    m_prev = m_ref[0:1, cols]
                    m_new = jnp.maximum(m_prev, jnp.max(s, axis=0, keepdims=True))
                    alpha_ref[:, cols] = jnp.broadcast_to(jnp.exp(m_prev - m_new), (alpha_ref.shape[0], CG))
                    m_ref[:, cols] = jnp.broadcast_to(m_new, (m_ref.shape[0], CG))
                    p_ref[r0:r1, cols] = jnp.exp(s - m_new).astype(BF16)
                else:
                    p_ref[r0:r1, cols] = jnp.exp(s - bnd_ref[0:1, cols]).astype(BF16)

        def steady(lo, hi):
            def pair_body(j, carry):
                ki = lo + 2 * j
                scores(ki, prev=ki - 1)
                scores(ki + 1, prev=ki)
                return carry
            lax.fori_loop(0, (hi - lo) // 2, pair_body, 0)

            @pl.when((hi - lo) % 2 == 1)
            def _odd_one():
                scores(hi - 1, prev=hi - 2)

        @pl.when(kd == 0)
        def _only_diagonal():
            scores(0, "diag")
            values(0, "diag")

        @pl.when(kd > 0)
        def _past_then_diagonal():
            if window is None:
                scores(0)
                steady(1, kd)
                scores(kd, "diag", prev=kd - 1)
            else:
                scores(kd - 1, "wedge")
                scores(kd, "diag", prev=kd - 1, prev_kind="wedge")
            values(kd, "diag")

    def emit():
        acc = acc_ref[...]
        out = (acc[0:HEAD_DIM] / acc[HEAD_DIM:HEAD_DIM + 1]).astype(o_ref.dtype)
        if pair:
            for hh in range(nh):
                o_ref[hh, :, toks] = out[:, hh * tq:(hh + 1) * tq]
        else:
            o_ref[0, :, pl.ds(pl.multiple_of(qi * R, R), R)] = out

    sweep(False)
    trusted = jnp.min(acc_ref[HEAD_DIM:HEAD_DIM + 1, :]) >= MIN_WEIGHT_SUM

    @pl.when(trusted)
    def _done():
        emit()

    @pl.when(jnp.logical_not(trusted))
    def _redo_with_running_max():
        sweep(True)
        emit()


def _flash_kernel(q_ref, bias_ref, k_ref, kf_ref, vt_ref, mask_ref, o_ref, ka_ref, vta_ref, kn_ref, qa_ref,
                  bnd_ref, m_ref, acc_ref, alpha_ref, p_ref, *, tq, tk, nh, pair, window):
    S = k_ref.shape[0]
    for kb in range(ka_ref.shape[0]):
        k = k_ref[:, kb * LANES:(kb + 1) * LANES]
        ka_ref[kb, :, 0:LANES] = k
        ka_ref[kb, :, LANES:2 * LANES] = kf_ref[...]
        k2 = jnp.square(k.astype(F32))
        lane = lax.broadcasted_iota(jnp.int32, k2.shape, 1)
        for hf in range(2):
            n2 = jnp.sum(jnp.where(lane // HEAD_DIM == hf, k2, 0.0), axis=1, keepdims=True)
            kn_ref[2 * kb + hf:2 * kb + hf + 1, :] = jnp.broadcast_to(
                jnp.sqrt(jnp.max(n2, axis=0, keepdims=True)), (1, LANES))
    ones_row = jnp.where(lax.broadcasted_iota(jnp.int32, (VROWS - HEAD_DIM, tk), 0) == 0, 1.0, 0.0)
    for v in range(nh if pair else 1):
        for jt in range(S // tk):
            vta_ref[v, jt, 0:HEAD_DIM, :] = vt_ref[v * HEAD_DIM:(v + 1) * HEAD_DIM, jt * tk:(jt + 1) * tk]
            vta_ref[v, jt, HEAD_DIM:VROWS, :] = ones_row.astype(BF16)

    def q_tile(qi, carry):
        _flash_q_tile(qi, q_ref, bias_ref, mask_ref, o_ref, ka_ref, vta_ref, kn_ref, qa_ref, bnd_ref, m_ref,
                      acc_ref, alpha_ref, p_ref, tq=tq, tk=tk, nh=nh, pair=pair, window=window)
        return carry

    lax.fori_loop(0, S // tq, q_tile, 0)


def _mask_table(tk, window):
    n_delta = (tk + (window or 0)) // CG
    d = np.arange(n_delta)[:, None, None] * CG + np.arange(CG)[None, None, :] - np.arange(tk)[None, :, None]
    ok = (d >= 0) & ((d < window) if window else True)
    return jnp.asarray(np.where(ok, 0.0, NEG), F32)


def _flash(q_t, bias_t, k, kfeat, v_t, *, B, S, units, tq, nh, pair, window):
    tk = TK
    R = nh * tq
    nq = S // tq
    nv = nh if pair else 1
    nb = nh if pair else 1
    nkb = nh // 2 if pair else 1
    assert tq == tk and S % tk == 0 and tk % CG == 0 and window in (None, tk)
    mask = _mask_table(tk, window)
    kcol = (lambda u: u) if pair else (lambda u: u // 2)
    out_w = tq if pair else R
    out_shape = (B, units * nb, HEAD_DIM, S if pair else S * nh)
    return pl.pallas_call(
        functools.partial(_flash_kernel, tq=tq, tk=tk, nh=nh, pair=pair, window=window),
        grid=(B, units),
        in_specs=[pl.BlockSpec((nh * HEAD_DIM, S), lambda b, u: (u, b)),
                  pl.BlockSpec((None, nb, FEAT, S), lambda b, u: (b, u, 0, 0)),
                  pl.BlockSpec((S, nkb * LANES), lambda b, u: (b, kcol(u))),
                  pl.BlockSpec((S, LANES), lambda b, u: (0, 0)),
                  pl.BlockSpec((nv * HEAD_DIM, S), lambda b, u: (u, b)),
                  pl.BlockSpec(mask.shape, lambda b, u: (0, 0, 0))],
        out_specs=pl.BlockSpec((None, nb, HEAD_DIM, out_w * nq), lambda b, u: (b, u, 0, 0)),
        out_shape=jax.ShapeDtypeStruct(out_shape, BF16),
        scratch_shapes=[pltpu.VMEM((nkb, S, 2 * LANES), BF16),
                        pltpu.VMEM((nv, S // tk, VROWS, tk), BF16),
                        pltpu.VMEM((8, LANES), F32),
                        pltpu.VMEM((2 * LANES, R), BF16),
                        pltpu.VMEM((8, R), F32),
                        pltpu.VMEM((8, R), F32),
                        pltpu.VMEM((VROWS, R), F32),
                        pltpu.VMEM((8, R), F32),
                        pltpu.VMEM((tk, R), BF16)],
        compiler_params=pltpu.CompilerParams(dimension_semantics=("parallel", "parallel"),
                                             vmem_limit_bytes=VMEM_LIMIT),
        name="flash",
    )(q_t, bias_t, k, kfeat, v_t, mask)


def _even_out_kernel(x_ref, ocmp_ref, osel_ref, owin_ref, om_ref, gate_ref, nz_ref, mz_ref,
                     e_ref, w_ref, g_ref, b_ref, o_ref, *, alpha, tm):
    G, nh = NSA_KV_GROUPS, NSA_HG
    sig = _sigmoid(gate_ref[...])
    shi, slo = _hilo(sig)

    def heads_t(ref):
        return jnp.concatenate([ref[g, :, hh * tm:(hh + 1) * tm] for g in range(G) for hh in range(nh)],
                               axis=0).astype(F32)

    o_nsa = None
    for br, ref in enumerate((ocmp_ref, osel_ref, owin_ref)):
        gb = _dot(e_ref[br], shi) + _dot(e_ref[br], slo)
        term = gb * heads_t(ref)
        o_nsa = term if o_nsa is None else o_nsa + term
    nz = nz_ref[...].astype(F32)
    mz = mz_ref[...].astype(F32)
    o_m = jnp.concatenate([om_ref[h] for h in range(MOBA_HEADS)], axis=0).astype(F32)
    a_t = jnp.concatenate([o_nsa * (nz * _sigmoid(nz)), o_m * (mz * _sigmoid(mz))], axis=0)
    y = _dot_tn(a_t.astype(BF16), w_ref[...])
    o_ref[...] = _layer_norm(alpha * x_ref[...] + y, g_ref[...], b_ref[...])


def _even_out(x2d, ocmp, osel, owin, om, gate_t, nz_t, mz_t, expand, w_out, ln_g, ln_b, alpha, *, B, S):
    T, D = x2d.shape
    tm = TQ
    nq = S // tm
    G, nh, H = NSA_KV_GROUPS, NSA_HG, MOBA_HEADS
    nsa = pl.BlockSpec((None, G, HEAD_DIM, nh * tm), lambda i: (i // nq, 0, 0, i % nq))
    feat = lambda n: pl.BlockSpec((n, tm), lambda i: (0, i))
    full = lambda a: pl.BlockSpec(a.shape, lambda i: (0,) * a.ndim)
    return pl.pallas_call(
        functools.partial(_even_out_kernel, alpha=alpha, tm=tm),
        grid=(T // tm,),
        in_specs=[pl.BlockSpec((tm, D), lambda i: (i, 0)), nsa, nsa, nsa,
                  pl.BlockSpec((None, H, HEAD_DIM, tm), lambda i: (i // nq, 0, 0, i % nq)),
                  feat(GATE_ROWS), feat(NSA_W), feat(MOBA_W),
                  full(expand), full(w_out), full(ln_g), full(ln_b)],
        out_specs=pl.BlockSpec((tm, D), lambda i: (i, 0)),
        out_shape=jax.ShapeDtypeStruct((T, D), F32),
        compiler_params=pltpu.CompilerParams(dimension_semantics=("parallel",),
                                             vmem_limit_bytes=VMEM_LIMIT),
        name="even_out",
    )(x2d, ocmp, osel, owin, om, gate_t, nz_t, mz_t, expand, w_out, ln_g, ln_b)


def _gla_head_exact(row0, h, q_ref, v_ref, o_ref, st_ref, b_ref, kf_ref, scale):
    C, dk, dv, SUB = GLA_CHUNK, GLA_DK, GLA_DV, GLA_SUB
    rows = lax.broadcasted_iota(jnp.int32, (C, dk), 0)
    sub_i = lax.broadcasted_iota(jnp.int32, (SUB, LANES), 0)
    lane = lax.broadcasted_iota(jnp.int32, (SUB, LANES), 1)
    ones = jnp.ones((dk, LANES), BF16)
    sl = pl.ds(h * dk, dk)
    rs_c = pl.ds(row0, C)
    bh = b_ref[:, sl]
    kf = kf_ref[:, sl]
    qf = q_ref[rs_c, sl].astype(F32) * scale
    vh = v_ref[rs_c, pl.ds(h * dv, dv)]
    st = st_ref[h]
    inter = _dot_nt((qf * jnp.exp(bh)).astype(BF16), st.astype(BF16))
    bl = b_ref[pl.ds(C - 1, 1), sl]
    kd = kf * jnp.exp(bl - bh)

    def off_block(r0, r1, c0, c1):
        ref_row = b_ref[pl.ds(r0, 1), sl]
        qt = qf[r0:r1] * jnp.exp(bh[r0:r1] - ref_row)
        ek = jnp.where((rows >= c0) & (rows < c1), ref_row - bh, NEG)
        kt = kf * jnp.exp(ek)
        return _dot_nt(qt.astype(BF16), kt.astype(BF16))

    lower = off_block(2 * SUB, C, 0, 2 * SUB)
    blocks = [None, off_block(SUB, 2 * SUB, 0, SUB), lower[0:SUB],
              lower[SUB:2 * SUB] + off_block(3 * SUB, C, 2 * SUB, 3 * SUB)]
    pieces = []
    for blk in range(C // SUB):
        r0 = blk * SUB
        q_i = qf[r0:r0 + SUB]
        b_i = bh[r0:r0 + SUB]
        for jj in range(SUB):
            b_j = b_ref[pl.ds(r0 + jj, 1), sl]
            k_j = kf_ref[pl.ds(r0 + jj, 1), sl]
            pieces.append((q_i * k_j * jnp.exp(jnp.minimum(b_i - b_j, 0.0))).astype(BF16))
    sums = _dot(jnp.concatenate(pieces, axis=0), ones)
    for blk in range(C // SUB):
        r0 = blk * SUB
        diag = jnp.zeros((SUB, LANES), F32)
        for jj in range(SUB):
            rs = sums[(r0 + jj) * SUB:(r0 + jj + 1) * SUB]
            diag = jnp.where((lane == r0 + jj) & (sub_i >= jj), rs, diag)
        diag = diag[:, 0:C]
        blocks[blk] = diag if blocks[blk] is None else blocks[blk] + diag
    att = jnp.concatenate(blocks, axis=0)
    o_ref[rs_c, pl.ds(h * dv, dv)] = (inter + _dot(att.astype(BF16), vh)).astype(o_ref.dtype)
    st_ref[h] = st * jnp.exp(bl) + _dot_tn(vh, kd.astype(BF16))


def _gla_head_bounded(row0, h, q_ref, v_ref, o_ref, st_ref, b_ref, kf_ref, scale):
    C, dk, dv = GLA_CHUNK, GLA_DK, GLA_DV
    sl = pl.ds(h * dk, dk)
    rs_c = pl.ds(row0, C)
    bh = b_ref[rs_c, sl]
    kf = kf_ref[rs_c, sl]
    qf = q_ref[rs_c, sl].astype(F32) * scale
    vh = v_ref[rs_c, pl.ds(h * dv, dv)]
    st = st_ref[h]
    qs = (qf * jnp.exp(bh)).astype(BF16)
    kt = (kf * jnp.exp(-bh)).astype(BF16)
    i = lax.broadcasted_iota(jnp.int32, (C, C), 0)
    j = lax.broadcasted_iota(jnp.int32, (C, C), 1)
    att = jnp.where(i >= j, _dot_nt(qs, kt), 0.0)
    inter = _dot_nt(qs, st.astype(BF16))
    o_ref[rs_c, pl.ds(h * dv, dv)] = (inter + _dot(att.astype(BF16), vh)).astype(o_ref.dtype)
    bl = b_ref[pl.ds(row0 + C - 1, 1), sl]
    kd = kf * jnp.exp(bl - bh)
    st_ref[h] = st * jnp.exp(bl) + _dot_tn(vh, kd.astype(BF16))


def _gla_kernel(q_ref, k_ref, v_ref, glr_ref, w2_ref, gb_ref, tril_ref, o_ref,
                st_ref, b_ref, kf_ref, bc_ref, kc_ref, *, scale):
    C, H = GLA_CHUNK, GLA_HEADS
    nch = q_ref.shape[0] // C

    @pl.when(pl.program_id(1) == 0)
    def _init():
        st_ref[...] = jnp.zeros(st_ref.shape, F32)

    whi, wlo = _hilo(w2_ref[...])
    part = tril_ref.shape[0]
    decay = None
    for r0 in range(0, nch * C, part):
        rs = pl.ds(r0, part)
        ghi, glo = _hilo(glr_ref[rs, :])
        pre = _dot(ghi, whi) + _dot(glo, whi) + _dot(ghi, wlo) + gb_ref[...]
        la = (jnp.minimum(pre, 0.0) - jnp.log(1.0 + jnp.exp(-jnp.abs(pre)))) * (1.0 / GLA_TAU)
        lhi, llo = _hilo(la)
        b_part = _dot(tril_ref[...], lhi) + _dot(tril_ref[...], llo)
        b_ref[rs, :] = b_part
        for c in range(part // C):
            d_c = -b_part[c * C + C - 1:(c + 1) * C, :]
            decay = d_c if decay is None else jnp.maximum(decay, d_c)
    kf_ref[...] = k_ref[...].astype(F32)
    bounded = jnp.max(decay) <= GLA_FACTOR_MAX_DECAY

    @pl.when(bounded)
    def _factored():
        for c in range(nch):
            for h in range(H):
                _gla_head_bounded(c * C, h, q_ref, v_ref, o_ref, st_ref, b_ref, kf_ref, scale)

    @pl.when(jnp.logical_not(bounded))
    def _exact():
        def chunk(c, carry):
            row0 = pl.multiple_of(c * C, C)
            bc_ref[...] = b_ref[pl.ds(row0, C), :]
            kc_ref[...] = kf_ref[pl.ds(row0, C), :]
            for h in range(H):
                _gla_head_exact(row0, h, q_ref, v_ref, o_ref, st_ref, bc_ref, kc_ref, scale)
            return carry
        lax.fori_loop(0, nch, chunk, 0)


def _gla(q, k, v, glr, w2p, gb, tril, *, B, S):
    C, H, dk, dv = GLA_CHUNK, GLA_HEADS, GLA_DK, GLA_DV
    rows = C * min(GLA_STEP_CHUNKS, S // C)
    nchunk = S // rows
    tok = lambda n: pl.BlockSpec((rows, n), lambda b, c: (b * nchunk + c, 0))
    const = lambda shape: pl.BlockSpec(shape, lambda b, c: (0,) * len(shape))
    return pl.pallas_call(
        functools.partial(_gla_kernel, scale=float(dk) ** -0.5),
        grid=(B, nchunk),
        in_specs=[tok(H * dk), tok(H * dk), tok(H * dv), tok(LANES),
                  const(w2p.shape), const(gb.shape), const(tril.shape)],
        out_specs=tok(H * dv),
        out_shape=jax.ShapeDtypeStruct((B * S, H * dv), BF16),
        scratch_shapes=[pltpu.VMEM((H, dv, dk), F32),
                        pltpu.VMEM((rows, H * dk), F32),
                        pltpu.VMEM((rows, H * dk), F32),
                        pltpu.VMEM((C, H * dk), F32),
                        pltpu.VMEM((C, H * dk), F32)],
        compiler_params=pltpu.CompilerParams(dimension_semantics=("parallel", "arbitrary"),
                                             vmem_limit_bytes=VMEM_LIMIT),
        name="gla",
    )(q, k, v, glr, w2p, gb, tril)


def _odd_out_kernel(x_ref, o_ref_in, z_ref, gn_ref, w_ref, g_ref, b_ref, out_ref, *, alpha):
    H, dv = GLA_HEADS, GLA_DV
    parts = []
    for h in range(H):
        sl = pl.ds(h * dv, dv)
        oh = o_ref_in[:, sl].astype(F32)
        r = lax.rsqrt(jnp.mean(oh * oh, axis=-1, keepdims=True) + LN_EPS)
        z = z_ref[:, sl].astype(F32)
        parts.append((oh * r * gn_ref[:, sl] * (z * _sigmoid(z))).astype(BF16))
    y = _dot(jnp.concatenate(parts, axis=1), w_ref[...])
    out_ref[...] = _layer_norm(alpha * x_ref[...] + y, g_ref[...], b_ref[...])


def _odd_out(x2d, o, z, gn_g, w_out, ln_g, ln_b, alpha, tm=512):
    T, D = x2d.shape
    W = o.shape[1]
    row = lambda n: pl.BlockSpec((tm, n), lambda i: (i, 0))
    return pl.pallas_call(
        functools.partial(_odd_out_kernel, alpha=alpha),
        grid=(T // tm,),
        in_specs=[row(D), row(W), row(W),
                  pl.BlockSpec((1, W), lambda i: (0, 0)),
                  pl.BlockSpec((W, D), lambda i: (0, 0)),
                  pl.BlockSpec((1, D), lambda i: (0, 0)),
                  pl.BlockSpec((1, D), lambda i: (0, 0))],
        out_specs=row(D),
        out_shape=jax.ShapeDtypeStruct((T, D), F32),
        compiler_params=pltpu.CompilerParams(dimension_semantics=("parallel",),
                                             vmem_limit_bytes=VMEM_LIMIT),
        name="odd_out",
    )(x2d, o, z, gn_g, w_out, ln_g, ln_b)


def _key_features(S, block):
    s = np.arange(S)
    f = np.zeros((S, LANES), np.float32)
    f[s, s // block] = 1.0
    f[:, FEAT + 0] = 1.0
    f[:, FEAT + 1] = 1.0
    f[:, FEAT + 2] = FEAT * (s // FEAT)
    f[:, FEAT + 3] = s % FEAT
    return jnp.asarray(f, BF16)


def _pad_to(a, n, axis):
    pad = [(0, 0)] * a.ndim
    pad[axis] = (0, n - a.shape[axis])
    return jnp.pad(a, pad)


def _even_layer(x2d, B, S, w_in, cmp_pos, cmp_w1, cmp_b1, cmp_w2, w_out, ln_g, ln_b, alpha):
    T, D = x2d.shape
    G, hd = NSA_KV_GROUPS, HEAD_DIM
    scale = hd ** -0.5
    wq, wkv, wgate, wnz, wmq, wmk, wmv, wmz = jnp.split(w_in, np.cumsum(EVEN_SPLITS)[:-1].tolist(), axis=1)
    wkc, wvc, wks, wvs, wkw, wvw = jnp.split(wkv, 6, axis=1)
    w_tok = jnp.concatenate([wkc, wvc, wks, wkw, wmk], axis=1).astype(BF16)
    w_tr = jnp.concatenate([wq * scale, wvs, wvw, wmq * scale, wmv, wnz, wmz,
                            _pad_to(wgate, GATE_ROWS, 1)], axis=1).T.astype(BF16)
    (kcmp, vcmp, ksel, kwin, mk, nq_t, vsel_t, vwin_t, mq_t, mv_t, nz_t, mz_t, gate_t) = _proj(
        x2d, w_tok, (NSA_KV_W,) * 4 + (MOBA_W,), (BF16,) * 5,
        w_tr, (NSA_W, NSA_KV_W, NSA_KV_W, MOBA_W, MOBA_W, NSA_W, MOBA_W, GATE_ROWS),
        (BF16,) * 7 + (F32,))

    L, d, HID = NSA_CMP_BLOCK, NSA_CMP_STRIDE, NSA_CMP_HIDDEN
    ncmp = (S - L) // d + 1
    NC = S // d
    eye = jnp.eye(G, dtype=F32)
    w1r = cmp_w1.reshape(2, 2, d, hd, HID)
    w1p = jnp.einsum('ialdj,gh->ialgdhj', w1r, eye).reshape(2, 2, d * G * hd, G * HID).astype(BF16)
    posp = jnp.broadcast_to(cmp_pos.reshape(2, 2, d, 1, hd), (2, 2, d, G, hd)).reshape(2, 2, 1, d * G * hd)
    b1p = jnp.tile(cmp_b1, (1, G)).reshape(2, 1, G * HID)
    w2p = jnp.einsum('ijd,gh->igjhd', _pad_to(cmp_w2, LANES, 2), eye).reshape(2, G * HID, G * LANES).astype(BF16)
    cfeat = np.zeros((NC, G, LANES), np.float32)
    cfeat[:, :, hd + 0] = 1.0
    cfeat[:, :, hd + 1] = 1.0
    cfeat[:, :, hd + 2] = (np.arange(NC) * d)[:, None]
    cfeat[:, :, hd + 3] = L - 1
    kvc = _cmp_mlp(kcmp.reshape(B, NC, d * NSA_KV_W), vcmp.reshape(B, NC, d * NSA_KV_W),
                   w1p, posp, b1p, w2p, jnp.asarray(cfeat.reshape(NC, G * LANES)), ncmp)

    ns = S // NSA_SEL_BLOCK
    cmp_start = np.arange(NC) * d
    sel_start = np.arange(FEAT) * NSA_SEL_BLOCK
    ovt = ((cmp_start[None, :] <= sel_start[:, None] + NSA_SEL_BLOCK - 1)
           & (cmp_start[None, :] + L - 1 >= sel_start[:, None])
           & (np.arange(NC)[None, :] < ncmp) & (np.arange(FEAT)[:, None] < ns))
    o_cmp, bias_sel = _cmp_attn(nq_t, kvc, jnp.asarray(ovt, BF16), B=B, S=S)

    feats_sel = _key_features(S, NSA_SEL_BLOCK)
    o_sel = _flash(nq_t, bias_sel, ksel, feats_sel, vsel_t,
                   B=B, S=S, units=G, tq=TQ, nh=NSA_HG, pair=False, window=None)
    o_win = _flash(nq_t, jnp.zeros_like(bias_sel), kwin, feats_sel, vwin_t,
                   B=B, S=S, units=G, tq=TQ, nh=NSA_HG, pair=False, window=NSA_WINDOW)

    avg = np.zeros((MOBA_NBP, S), np.float32)
    avg[np.arange(S) // MOBA_BLOCK, np.arange(S)] = 1.0 / MOBA_BLOCK
    bias_m = _moba_gate(mq_t, mk, jnp.asarray(avg, BF16), B=B, S=S)
    o_m = _flash(mq_t, bias_m, mk, _key_features(S, MOBA_BLOCK), mv_t,
                 B=B, S=S, units=MOBA_HEADS // MOBA_STEP_HEADS, tq=TQM, nh=MOBA_STEP_HEADS, pair=True, window=None)

    n = np.arange(NSA_W)[:, None]
    c = np.arange(GATE_ROWS)[None, :]
    expand = jnp.asarray(np.stack([(c == 3 * (n // hd) + br) for br in range(3)]), BF16)
    return _even_out(x2d, o_cmp, o_sel, o_win, o_m, gate_t, nz_t, mz_t, expand, w_out.astype(BF16),
                     ln_g.reshape(1, D), ln_b.reshape(1, D), alpha, B=B, S=S)


def _odd_layer(x2d, B, S, w_in, gate_w2, gate_b, gn_g, w_out, ln_g, ln_b, alpha):
    T, D = x2d.shape
    H, dk, dv, C = GLA_HEADS, GLA_DK, GLA_DV, GLA_CHUNK
    wq, wk, wv, wg, wz = jnp.split(w_in, np.cumsum(ODD_SPLITS)[:-1].tolist(), axis=1)
    w_perm = jnp.concatenate([wq, wk, wv, wz, _pad_to(wg, LANES, 1)], axis=1).astype(BF16)
    q, k, v, z, glr = _proj(x2d, w_perm, (H * dk, H * dk, H * dv, H * dv, LANES),
                            (BF16, BF16, BF16, BF16, F32))
    w2p = _pad_to(gate_w2, LANES, 0)
    ncs = min(GLA_CUMSUM_CHUNKS, GLA_STEP_CHUNKS, S // C)
    tril = jnp.asarray(np.kron(np.eye(ncs), np.tril(np.ones((C, C)))), BF16)
    o = _gla(q, k, v, glr, w2p, gate_b.reshape(1, H * dk), tril, B=B, S=S)
    return _odd_out(x2d, o, z, gn_g.reshape(1, H * dv), w_out.astype(BF16),
                    ln_g.reshape(1, D), ln_b.reshape(1, D), alpha)


def kernel(x, ev_w_in, ev_cmp_pos, ev_cmp_w1, ev_cmp_b1, ev_cmp_w2, ev_w_out, ev_ln_g, ev_ln_b,
           od_w_in, od_gate_w2, od_gate_b, od_gn_g, od_w_out, od_ln_g, od_ln_b):
    B, S, D = x.shape
    depth = ev_w_in.shape[0] + od_w_in.shape[0]
    alpha = (2.0 * depth) ** 0.25
    h = x.reshape(B * S, D)
    for layer in range(depth):
        i = layer // 2
        if layer % 2 == 0:
            h = _even_layer(h, B, S, ev_w_in[i], ev_cmp_pos[i], ev_cmp_w1[i], ev_cmp_b1[i], ev_cmp_w2[i],
                            ev_w_out[i], ev_ln_g[i], ev_ln_b[i], alpha)
        else:
            h = _odd_layer(h, B, S, od_w_in[i], od_gate_w2[i], od_gate_b[i], od_gn_g[i],
                           od_w_out[i], od_ln_g[i], od_ln_b[i], alpha)
    return h.reshape(B, S, D)
```

```python
import functools

import numpy as np
import jax
import jax.numpy as jnp
from jax import lax
from jax.experimental import pallas as pl
from jax.experimental.pallas import tpu as pltpu

F32 = jnp.float32
BF16 = jnp.bfloat16

HEAD_DIM = 64
NSA_HEADS = 8
NSA_KV_GROUPS = 2
NSA_HG = NSA_HEADS // NSA_KV_GROUPS
NSA_CMP_BLOCK = 32
NSA_CMP_STRIDE = 16
NSA_CMP_HIDDEN = 128
NSA_SEL_BLOCK = 64
NSA_TOP_N = 16
NSA_WINDOW = 512
MOBA_HEADS = 8
MOBA_BLOCK = 256
MOBA_TOP_K = 3
GLA_HEADS = 4
GLA_DK = 128
GLA_DV = 256
GLA_GATE_RANK = 16
GLA_TAU = 16.0
GLA_CHUNK = 64
GLA_SUB = 16
GLA_STEP_CHUNKS = 8
GLA_CUMSUM_CHUNKS = 4
GLA_FACTOR_MAX_DECAY = 60.0
LN_EPS = 1e-5
NEG = -1e30
FORCE_BONUS = 1e4

NSA_W = NSA_HEADS * HEAD_DIM
NSA_KV_W = NSA_KV_GROUPS * HEAD_DIM
MOBA_W = MOBA_HEADS * HEAD_DIM
EVEN_SPLITS = (NSA_W, 6 * NSA_KV_W, 3 * NSA_HEADS, NSA_W, MOBA_W, MOBA_W, MOBA_W, MOBA_W)
ODD_SPLITS = (GLA_HEADS * GLA_DK, GLA_HEADS * GLA_DK, GLA_HEADS * GLA_DV, GLA_GATE_RANK, GLA_HEADS * GLA_DV)

LANES = 128
SUBLANES = 8
FEAT = 64
ALIBI_ROWS = 16
GATE_ROWS = 32
MOBA_NBP = 16
TQ = 512
TQM = 512
MOBA_STEP_HEADS = 4
TK = 512
CG = 256
MIN_WEIGHT_SUM = 1e-25
VROWS = HEAD_DIM + 16
VMEM_LIMIT = 48 * 1024 * 1024


def _dot(a, b):
    return jnp.dot(a, b, preferred_element_type=F32)


def _dot_nt(a, b):
    return lax.dot_general(a, b, (((1,), (1,)), ((), ())), preferred_element_type=F32)


def _dot_tn(a, b):
    return lax.dot_general(a, b, (((0,), (0,)), ((), ())), preferred_element_type=F32)


def _hilo(a):
    hi = a.astype(BF16)
    lo = (a - hi.astype(F32)).astype(BF16)
    return hi, lo


def _pow2_neg(n):
    return lax.bitcast_convert_type(lax.shift_left(127 - n, 23), F32)


def _sigmoid(x):
    return 0.5 * jnp.tanh(0.5 * x) + 0.5


def _layer_norm(r, g, b):
    mu = jnp.mean(r, axis=-1, keepdims=True)
    d = r - mu
    var = jnp.mean(d * d, axis=-1, keepdims=True)
    return d * lax.rsqrt(var + LN_EPS) * g + b


def _alibi_query_rows(t0, tq, head0, R):
    r = lax.broadcasted_iota(jnp.int32, (ALIBI_ROWS, R), 0)
    col = lax.broadcasted_iota(jnp.int32, (ALIBI_ROWS, R), 1)
    t = t0 + col % tq
    slope = _pow2_neg(head0 + col // tq + 1)
    feats = jnp.where(r == 0, -FEAT * slope * (t // FEAT).astype(F32),
                      jnp.where(r == 1, -slope * (t % FEAT).astype(F32), jnp.where(r < 4, slope, 0.0)))
    return jnp.concatenate([feats.astype(BF16), jnp.zeros((FEAT - ALIBI_ROWS, R), BF16)], axis=0)


def _ranks(sc_ref, n_real):
    n_rows, width = sc_ref.shape
    out = []
    for v in range(n_rows // SUBLANES):
        lo = v * SUBLANES
        sc = sc_ref[lo:lo + SUBLANES, :]
        idx = lo + lax.broadcasted_iota(jnp.int32, (SUBLANES, width), 0)
        rank = jnp.zeros((SUBLANES, width), jnp.int32)
        for jp in range(n_real):
            r = sc_ref[pl.ds(jp, 1), :]
            if jp < lo:
                beats = r >= sc
            elif jp >= lo + SUBLANES - 1:
                beats = r > sc
            else:
                beats = (r > sc) | ((r == sc) & (idx > jp))
            rank = rank + jnp.where(beats, 1, 0)
        out.append(rank)
    return jnp.concatenate(out, axis=0)


def _proj_kernel(*refs, tok_splits, tr_splits):
    x_ref, wt_ref = refs[0], refs[1]
    outs = list(refs[3:] if tr_splits else refs[2:])
    x = x_ref[...]
    xb = x.astype(BF16)
    off = 0
    for n in tok_splits:
        o_ref = outs.pop(0)
        o_ref[...] = _dot(xb, wt_ref[:, off:off + n]).astype(o_ref.dtype)
        off += n
    if tr_splits:
        wtr_ref = refs[2]
        xt = x.T.astype(BF16)
        off = 0
        for n in tr_splits:
            o_ref = outs.pop(0)
            o_ref[...] = _dot(wtr_ref[off:off + n, :], xt).astype(o_ref.dtype)
            off += n


def _proj(x2d, w_tok, tok_splits, tok_dtypes, w_tr=None, tr_splits=(), tr_dtypes=(), tm=512):
    T, D = x2d.shape
    assert T % tm == 0 and sum(tok_splits) == w_tok.shape[1]
    in_specs = [pl.BlockSpec((tm, D), lambda i: (i, 0)),
                pl.BlockSpec(w_tok.shape, lambda i: (0, 0))]
    args = [x2d, w_tok]
    if tr_splits:
        assert sum(tr_splits) == w_tr.shape[0]
        in_specs.append(pl.BlockSpec(w_tr.shape, lambda i: (0, 0)))
        args.append(w_tr)
    out_specs = ([pl.BlockSpec((tm, n), lambda i: (i, 0)) for n in tok_splits]
                 + [pl.BlockSpec((n, tm), lambda i: (0, i)) for n in tr_splits])
    out_shape = ([jax.ShapeDtypeStruct((T, n), dt) for n, dt in zip(tok_splits, tok_dtypes)]
                 + [jax.ShapeDtypeStruct((n, T), dt) for n, dt in zip(tr_splits, tr_dtypes)])
    return pl.pallas_call(
        functools.partial(_proj_kernel, tok_splits=tuple(tok_splits), tr_splits=tuple(tr_splits)),
        grid=(T // tm,), in_specs=in_specs, out_specs=out_specs, out_shape=out_shape,
        compiler_params=pltpu.CompilerParams(dimension_semantics=("parallel",),
                                             vmem_limit_bytes=VMEM_LIMIT),
        name="proj",
    )(*args)


def _cmp_mlp_kernel(kx_ref, vx_ref, w1_ref, pos_ref, b1_ref, w2_ref, cf_ref, o_ref, *, ncmp):
    for i, x_ref in enumerate((kx_ref, vx_ref)):
        x = x_ref[...]
        nc, cw = x.shape
        a = _dot(x, w1_ref[i, 0])
        bm = _dot(x, w1_ref[i, 1])
        c = b1_ref[i]
        for half in range(2):
            phi, plo = _hilo(jnp.broadcast_to(pos_ref[i, half], (8, cw)))
            c = c + (_dot(phi, w1_ref[i, half]) + _dot(plo, w1_ref[i, half]))[0:1]
        hid = a + pltpu.roll(bm, nc - 1, axis=0) + c
        hid = hid * _sigmoid(hid)
        row = lax.broadcasted_iota(jnp.int32, hid.shape, 0)
        hid = jnp.where(row < ncmp, hid, 0.0)
        out = _dot(hid.astype(BF16), w2_ref[i])
        if i == 0:
            out = out + cf_ref[...]
        o_ref[i] = out.astype(o_ref.dtype)


def _cmp_mlp(kx, vx, w1p, posp, b1p, w2p, cfeat, ncmp):
    B, NC, CW = kx.shape
    OW = w2p.shape[-1]
    full = lambda a: pl.BlockSpec(a.shape, lambda b: (0,) * a.ndim)
    return pl.pallas_call(
        functools.partial(_cmp_mlp_kernel, ncmp=ncmp),
        grid=(B,),
        in_specs=[pl.BlockSpec((None, NC, CW), lambda b: (b, 0, 0)),
                  pl.BlockSpec((None, NC, CW), lambda b: (b, 0, 0)),
                  full(w1p), full(posp), full(b1p), full(w2p), full(cfeat)],
        out_specs=pl.BlockSpec((2, None, NC, OW), lambda b: (0, b, 0, 0)),
        out_shape=jax.ShapeDtypeStruct((2, B, NC, OW), BF16),
        compiler_params=pltpu.CompilerParams(dimension_semantics=("parallel",),
                                             vmem_limit_bytes=VMEM_LIMIT),
        name="cmp_mlp",
    )(kx, vx, w1p, posp, b1p, w2p, cfeat)


def _cmp_attn_kernel(q_ref, kc_ref, vc_ref, ovt_ref, o_ref, bias_ref, sc_ref, *, tq, nh, ns, ntop):
    g = pl.program_id(1)
    R = nh * tq
    nc = kc_ref.shape[0]
    nq = q_ref.shape[1] // tq

    def q_tile(qi, carry):
        toks = pl.ds(pl.multiple_of(qi * tq, tq), tq)
        qcat = jnp.concatenate([q_ref[hh * HEAD_DIM:(hh + 1) * HEAD_DIM, toks] for hh in range(nh)], axis=1)
        qa = jnp.concatenate([qcat, _alibi_query_rows(qi * tq, tq, g * nh, R)], axis=0)
        s = _dot(kc_ref[...], qa)
        n = lax.broadcasted_iota(jnp.int32, (nc, R), 0)
        t_row = qi * tq + lax.broadcasted_iota(jnp.int32, (1, R), 1) % tq
        last = (t_row - (NSA_CMP_BLOCK - 1)) // NSA_CMP_STRIDE
        s = jnp.where(n <= last, s, NEG)
        e = jnp.exp(s - jnp.max(s, axis=0, keepdims=True))
        inv = jnp.where(last >= 0, 1.0 / jnp.sum(e, axis=0, keepdims=True), 0.0)
        p = e * inv
        o_ref[:, pl.ds(pl.multiple_of(qi * R, R), R)] = _dot_tn(vc_ref[:, 0:HEAD_DIM],
                                                                 p.astype(BF16)).astype(o_ref.dtype)

        psum = p[:, 0:tq]
        for hh in range(1, nh):
            psum = psum + p[:, hh * tq:(hh + 1) * tq]
        phi, plo = _hilo(psum)
        imp = _dot(ovt_ref[...], phi) + _dot(ovt_ref[...], plo)

        j = lax.broadcasted_iota(jnp.int32, (FEAT, tq), 0)
        cur = (qi * tq + lax.broadcasted_iota(jnp.int32, (FEAT, tq), 1)) // NSA_SEL_BLOCK
        forced = jnp.where((j == 0) | (j == cur) | (j == cur - 1), FORCE_BONUS, 0.0)
        sc = jnp.where(j <= cur, imp + forced, NEG)
        blocks_so_far = ((qi + 1) * tq - 1) // NSA_SEL_BLOCK + 1

        @pl.when(blocks_so_far <= ntop)
        def _all_valid_blocks_selected():
            bias_ref[:, toks] = jnp.where(j <= cur, 0.0, NEG).astype(bias_ref.dtype)

        @pl.when(blocks_so_far > ntop)
        def _rank():
            sc_ref[...] = sc
            sel = (_ranks(sc_ref, ns) < ntop) & (j <= cur)
            bias_ref[:, toks] = jnp.where(sel, 0.0, NEG).astype(bias_ref.dtype)
        return carry

    lax.fori_loop(0, nq, q_tile, 0)


def _cmp_attn(nq_t, kvc, ovt, *, B, S):
    tq, nh, G = TQ, NSA_HG, NSA_KV_GROUPS
    R = nh * tq
    nq = S // tq
    NC = kvc.shape[2]
    ns = S // NSA_SEL_BLOCK
    assert ns <= FEAT
    ntop = min(NSA_TOP_N, ns)
    return pl.pallas_call(
        functools.partial(_cmp_attn_kernel, tq=tq, nh=nh, ns=ns, ntop=ntop),
        grid=(B, G),
        in_specs=[pl.BlockSpec((nh * HEAD_DIM, S), lambda b, g: (g, b)),
                  pl.BlockSpec((None, None, NC, LANES), lambda b, g: (0, b, 0, g)),
                  pl.BlockSpec((None, None, NC, LANES), lambda b, g: (1, b, 0, g)),
                  pl.BlockSpec(ovt.shape, lambda b, g: (0, 0))],
        out_specs=[pl.BlockSpec((None, None, HEAD_DIM, S * nh), lambda b, g: (b, g, 0, 0)),
                   pl.BlockSpec((None, None, FEAT, S), lambda b, g: (b, g, 0, 0))],
        out_shape=[jax.ShapeDtypeStruct((B, G, HEAD_DIM, S * nh), BF16),
                   jax.ShapeDtypeStruct((B, G, FEAT, S), BF16)],
        scratch_shapes=[pltpu.VMEM((FEAT, tq), F32)],
        compiler_params=pltpu.CompilerParams(dimension_semantics=("parallel", "parallel"),
                                             vmem_limit_bytes=VMEM_LIMIT),
        name="cmp_attn",
    )(nq_t, kvc, kvc, ovt)


def _moba_gate_kernel(q_ref, k_ref, a_ref, bias_ref, km_ref, sc_ref, *, tq, nb, ntop):
    h = pl.program_id(1)
    qi = pl.program_id(2)

    @pl.when(qi == 0)
    def _means():
        km_ref[...] = _dot(a_ref[...], k_ref[...])

    nbp = km_ref.shape[0]
    lane = lax.broadcasted_iota(jnp.int32, (nbp, LANES), 1)
    km = jnp.where(lane // HEAD_DIM == h % 2, km_ref[...], 0.0)
    khi, klo = _hilo(km)
    q = q_ref[...]
    qq = jnp.concatenate([q, q], axis=0)
    g_t = _dot(khi, qq) + _dot(klo, qq)
    n = lax.broadcasted_iota(jnp.int32, (nbp, tq), 0)
    cur = (qi * tq + lax.broadcasted_iota(jnp.int32, (nbp, tq), 1)) // MOBA_BLOCK
    sc = jnp.where(n < cur, g_t, NEG)
    sc_ref[...] = sc
    sel = ((_ranks(sc_ref, nb) < ntop) & (n < cur)) | (n == cur)
    bias_ref[0:nbp, :] = jnp.where(sel, 0.0, NEG).astype(bias_ref.dtype)
    bias_ref[nbp:FEAT, :] = jnp.zeros((FEAT - nbp, tq), bias_ref.dtype)


def _moba_gate(mq_t, mk, avg, *, B, S):
    tq, H = S, MOBA_HEADS
    nq = S // tq
    nb = S // MOBA_BLOCK
    nbp = avg.shape[0]
    assert nb <= nbp
    ntop = min(MOBA_TOP_K, nb)
    return pl.pallas_call(
        functools.partial(_moba_gate_kernel, tq=tq, nb=nb, ntop=ntop),
        grid=(B, H, nq),
        in_specs=[pl.BlockSpec((HEAD_DIM, tq), lambda b, h, i: (h, b * nq + i)),
                  pl.BlockSpec((S, LANES), lambda b, h, i: (b, h // 2)),
                  pl.BlockSpec(avg.shape, lambda b, h, i: (0, 0))],
        out_specs=pl.BlockSpec((None, None, FEAT, tq), lambda b, h, i: (b, h, 0, i)),
        out_shape=jax.ShapeDtypeStruct((B, H, FEAT, S), BF16),
        scratch_shapes=[pltpu.VMEM((nbp, LANES), F32), pltpu.VMEM((nbp, tq), F32)],
        compiler_params=pltpu.CompilerParams(dimension_semantics=("parallel", "parallel", "arbitrary"),
                                             vmem_limit_bytes=VMEM_LIMIT),
        name="moba_gate",
    )(mq_t, mk, avg)


def _flash_q_tile(qi, q_ref, bias_ref, mask_ref, o_ref, ka_ref, vta_ref, kn_ref, qa_ref, bnd_ref, m_ref,
                  acc_ref, alpha_ref, p_ref, *, tq, tk, nh, pair, window):
    u = pl.program_id(1)
    R = nh * tq
    q0 = pl.multiple_of(qi * tq, tq)
    toks = pl.ds(q0, tq)

    qcat = jnp.concatenate([q_ref[hh * HEAD_DIM:(hh + 1) * HEAD_DIM, toks] for hh in range(nh)], axis=1)
    qf = qcat.astype(F32)
    qq = jnp.concatenate([qf, qf], axis=0)
    rowi = lax.broadcasted_iota(jnp.int32, (2 * HEAD_DIM, R), 0)
    coli = lax.broadcasted_iota(jnp.int32, (2 * HEAD_DIM, R), 1)
    half = (coli // tq) % 2 if pair else u % 2
    qa_ref[0:2 * HEAD_DIM, :] = jnp.where(rowi // HEAD_DIM == half, qq, 0.0).astype(BF16)
    qa_ref[2 * HEAD_DIM:2 * HEAD_DIM + FEAT, :] = jnp.concatenate(
        [bias_ref[hh if pair else 0, :, toks] for hh in range(nh)], axis=1)
    qa_ref[2 * HEAD_DIM + FEAT:2 * LANES, :] = _alibi_query_rows(q0, tq, u * nh, R)
    if pair:
        kn = jnp.concatenate([kn_ref[hh:hh + 1, :] for hh in range(nh) for _ in range(tq // LANES)], axis=1)
    else:
        kn = jnp.concatenate([jnp.where(u % 2 == 0, kn_ref[0:1, :], kn_ref[1:2, :])] * (R // LANES), axis=1)
    qn = jnp.sqrt(jnp.sum(qf * qf, axis=0, keepdims=True))
    bnd_ref[...] = jnp.broadcast_to(qn * kn, bnd_ref.shape)

    ncg = R // CG
    kd = (q0 + tq - 1) // tk

    def key_rows(kind, c):
        off = (c * CG) % tq
        if kind is None:
            return 0, tk, None
        if kind == "diag":
            return 0, off + CG, off // CG
        return off, tk, (tk + off) // CG

    def sweep(running_max):
        acc_ref[...] = jnp.zeros(acc_ref.shape, F32)
        if running_max:
            m_ref[...] = jnp.full(m_ref.shape, NEG, F32)

        def values(ki, kind=None):
            for c in range(ncg):
                cols = pl.ds(c * CG, CG)
                r0, r1, _ = key_rows(kind, c)
                v = (c * CG) // tq if pair else 0
                pv = _dot(vta_ref[v, ki, :, r0:r1], p_ref[r0:r1, cols])
                if running_max:
                    acc_ref[:, cols] = alpha_ref[0:1, cols] * acc_ref[:, cols] + pv
                else:
                    acc_ref[:, cols] = acc_ref[:, cols] + pv

        def scores(ki, kind=None, prev=None, prev_kind=None):
            k0 = pl.multiple_of(ki * tk, tk)
            s_groups = []
            for c in range(ncg):
                r0, r1, _ = key_rows(kind, c)
                kb = (c * CG) // (2 * tq) if pair else 0
                s_groups.append(_dot(ka_ref[kb, pl.ds(k0 + r0, r1 - r0), :], qa_ref[:, pl.ds(c * CG, CG)]))
            if prev is not None:
                values(prev, prev_kind)
            for c, s in enumerate(s_groups):
                cols = pl.ds(c * CG, CG)
                r0, r1, entry = key_rows(kind, c)
                if entry is not None:
                    s = s + mask_ref[entry, r0:r1, :]
                if running_max:
                    m_prev = m_ref[0:1, cols]
                    m_new = jnp.maximum(m_prev, jnp.max(s, axis=0, keepdims=True))
                    alpha_ref[:, cols] = jnp.broadcast_to(jnp.exp(m_prev - m_new), (alpha_ref.shape[0], CG))
                    m_ref[:, cols] = jnp.broadcast_to(m_new, (m_ref.shape[0], CG))
                    p_ref[r0:r1, cols] = jnp.exp(s - m_new).astype(BF16)
                else:
                    p_ref[r0:r1, cols] = jnp.exp(s - bnd_ref[0:1, cols]).astype(BF16)

        def steady(lo, hi):
            def pair_body(j, carry):
                ki = lo + 2 * j
                scores(ki, prev=ki - 1)
                scores(ki + 1, prev=ki)
                return carry
            lax.fori_loop(0, (hi - lo) // 2, pair_body, 0)

            @pl.when((hi - lo) % 2 == 1)
            def _odd_one():
                scores(hi - 1, prev=hi - 2)

        @pl.when(kd == 0)
        def _only_diagonal():
            scores(0, "diag")
            values(0, "diag")

        @pl.when(kd > 0)
        def _past_then_diagonal():
            if window is None:
                scores(0)
                steady(1, kd)
                scores(kd, "diag", prev=kd - 1)
            else:
                scores(kd - 1, "wedge")
                scores(kd, "diag", prev=kd - 1, prev_kind="wedge")
            values(kd, "diag")

    def emit():
        acc = acc_ref[...]
        out = (acc[0:HEAD_DIM] / acc[HEAD_DIM:HEAD_DIM + 1]).astype(o_ref.dtype)
        if pair:
            for hh in range(nh):
                o_ref[hh, :, toks] = out[:, hh * tq:(hh + 1) * tq]
        else:
            o_ref[0, :, pl.ds(pl.multiple_of(qi * R, R), R)] = out

    sweep(False)
    trusted = jnp.min(acc_ref[HEAD_DIM:HEAD_DIM + 1, :]) >= MIN_WEIGHT_SUM

    @pl.when(trusted)
    def _done():
        emit()

    @pl.when(jnp.logical_not(trusted))
    def _redo_with_running_max():
        sweep(True)
        emit()


def _flash_kernel(q_ref, bias_ref, k_ref, kf_ref, vt_ref, mask_ref, o_ref, ka_ref, vta_ref, kn_ref, qa_ref,
                  bnd_ref, m_ref, acc_ref, alpha_ref, p_ref, *, tq, tk, nh, pair, window):
    S = k_ref.shape[0]
    for kb in range(ka_ref.shape[0]):
        k = k_ref[:, kb * LANES:(kb + 1) * LANES]
        ka_ref[kb, :, 0:LANES] = k
        ka_ref[kb, :, LANES:2 * LANES] = kf_ref[...]
        k2 = jnp.square(k.astype(F32))
        lane = lax.broadcasted_iota(jnp.int32, k2.shape, 1)
        for hf in range(2):
            n2 = jnp.sum(jnp.where(lane // HEAD_DIM == hf, k2, 0.0), axis=1, keepdims=True)
            kn_ref[2 * kb + hf:2 * kb + hf + 1, :] = jnp.broadcast_to(
                jnp.sqrt(jnp.max(n2, axis=0, keepdims=True)), (1, LANES))
    ones_row = jnp.where(lax.broadcasted_iota(jnp.int32, (VROWS - HEAD_DIM, tk), 0) == 0, 1.0, 0.0)
    for v in range(nh if pair else 1):
        for jt in range(S // tk):
            vta_ref[v, jt, 0:HEAD_DIM, :] = vt_ref[v * HEAD_DIM:(v + 1) * HEAD_DIM, jt * tk:(jt + 1) * tk]
            vta_ref[v, jt, HEAD_DIM:VROWS, :] = ones_row.astype(BF16)

    def q_tile(qi, carry):
        _flash_q_tile(qi, q_ref, bias_ref, mask_ref, o_ref, ka_ref, vta_ref, kn_ref, qa_ref, bnd_ref, m_ref,
                      acc_ref, alpha_ref, p_ref, tq=tq, tk=tk, nh=nh, pair=pair, window=window)
        return carry

    lax.fori_loop(0, S // tq, q_tile, 0)


def _mask_table(tk, window):
    n_delta = (tk + (window or 0)) // CG
    d = np.arange(n_delta)[:, None, None] * CG + np.arange(CG)[None, None, :] - np.arange(tk)[None, :, None]
    ok = (d >= 0) & ((d < window) if window else True)
    return jnp.asarray(np.where(ok, 0.0, NEG), F32)


def _flash(q_t, bias_t, k, kfeat, v_t, *, B, S, units, tq, nh, pair, window):
    tk = TK
    R = nh * tq
    nq = S // tq
    nv = nh if pair else 1
    nb = nh if pair else 1
    nkb = nh // 2 if pair else 1
    assert tq == tk and S % tk == 0 and tk % CG == 0 and window in (None, tk)
    mask = _mask_table(tk, window)
    kcol = (lambda u: u) if pair else (lambda u: u // 2)
    out_w = tq if pair else R
    out_shape = (B, units * nb, HEAD_DIM, S if pair else S * nh)
    return pl.pallas_call(
        functools.partial(_flash_kernel, tq=tq, tk=tk, nh=nh, pair=pair, window=window),
        grid=(B, units),
        in_specs=[pl.BlockSpec((nh * HEAD_DIM, S), lambda b, u: (u, b)),
                  pl.BlockSpec((None, nb, FEAT, S), lambda b, u: (b, u, 0, 0)),
                  pl.BlockSpec((S, nkb * LANES), lambda b, u: (b, kcol(u))),
                  pl.BlockSpec((S, LANES), lambda b, u: (0, 0)),
                  pl.BlockSpec((nv * HEAD_DIM, S), lambda b, u: (u, b)),
                  pl.BlockSpec(mask.shape, lambda b, u: (0, 0, 0))],
        out_specs=pl.BlockSpec((None, nb, HEAD_DIM, out_w * nq), lambda b, u: (b, u, 0, 0)),
        out_shape=jax.ShapeDtypeStruct(out_shape, BF16),
        scratch_shapes=[pltpu.VMEM((nkb, S, 2 * LANES), BF16),
                        pltpu.VMEM((nv, S // tk, VROWS, tk), BF16),
                        pltpu.VMEM((8, LANES), F32),
                        pltpu.VMEM((2 * LANES, R), BF16),
                        pltpu.VMEM((8, R), F32),
                        pltpu.VMEM((8, R), F32),
                        pltpu.VMEM((VROWS, R), F32),
                        pltpu.VMEM((8, R), F32),
                        pltpu.VMEM((tk, R), BF16)],
        compiler_params=pltpu.CompilerParams(dimension_semantics=("parallel", "parallel"),
                                             vmem_limit_bytes=VMEM_LIMIT),
        name="flash",
    )(q_t, bias_t, k, kfeat, v_t, mask)


def _even_out_kernel(x_ref, ocmp_ref, osel_ref, owin_ref, om_ref, gate_ref, nz_ref, mz_ref,
                     w_ref, g_ref, b_ref, o_ref, sig_ref, *, alpha, tm):
    G, nh = NSA_KV_GROUPS, NSA_HG
    sig_ref[...] = _sigmoid(gate_ref[...])
    heads = []
    for g in range(G):
        for hh in range(nh):
            acc = None
            for br, ref in enumerate((ocmp_ref, osel_ref, owin_ref)):
                gate = sig_ref[pl.ds(3 * (g * nh + hh) + br, 1), :]
                term = gate * ref[g, :, hh * tm:(hh + 1) * tm].astype(F32)
                acc = term if acc is None else acc + term
            heads.append(acc)
    o_nsa = jnp.concatenate(heads, axis=0)
    nz = nz_ref[...].astype(F32)
    mz = mz_ref[...].astype(F32)
    o_m = jnp.concatenate([om_ref[h] for h in range(MOBA_HEADS)], axis=0).astype(F32)
    a_t = jnp.concatenate([o_nsa * (nz * _sigmoid(nz)), o_m * (mz * _sigmoid(mz))], axis=0)
    y = _dot_tn(a_t.astype(BF16), w_ref[...])
    o_ref[...] = _layer_norm(alpha * x_ref[...] + y, g_ref[...], b_ref[...])


def _even_out(x2d, ocmp, osel, owin, om, gate_t, nz_t, mz_t, w_out, ln_g, ln_b, alpha, *, B, S):
    T, D = x2d.shape
    tm = TQ
    nq = S // tm
    G, nh, H = NSA_KV_GROUPS, NSA_HG, MOBA_HEADS
    nsa = pl.BlockSpec((None, G, HEAD_DIM, nh * tm), lambda i: (i // nq, 0, 0, i % nq))
    feat = lambda n: pl.BlockSpec((n, tm), lambda i: (0, i))
    full = lambda a: pl.BlockSpec(a.shape, lambda i: (0,) * a.ndim)
    return pl.pallas_call(
        functools.partial(_even_out_kernel, alpha=alpha, tm=tm),
        grid=(T // tm,),
        in_specs=[pl.BlockSpec((tm, D), lambda i: (i, 0)), nsa, nsa, nsa,
                  pl.BlockSpec((None, H, HEAD_DIM, tm), lambda i: (i // nq, 0, 0, i % nq)),
                  feat(GATE_ROWS), feat(NSA_W), feat(MOBA_W),
                  full(w_out), full(ln_g), full(ln_b)],
        out_specs=pl.BlockSpec((tm, D), lambda i: (i, 0)),
        out_shape=jax.ShapeDtypeStruct((T, D), F32),
        scratch_shapes=[pltpu.VMEM((GATE_ROWS, tm), F32)],
        compiler_params=pltpu.CompilerParams(dimension_semantics=("parallel",),
                                             vmem_limit_bytes=VMEM_LIMIT),
        name="even_out",
    )(x2d, ocmp, osel, owin, om, gate_t, nz_t, mz_t, w_out, ln_g, ln_b)


def _gla_head_exact(row0, h, q_ref, v_ref, o_ref, st_ref, b_ref, kf_ref, scale):
    C, dk, dv, SUB = GLA_CHUNK, GLA_DK, GLA_DV, GLA_SUB
    rows = lax.broadcasted_iota(jnp.int32, (C, dk), 0)
    sub_i = lax.broadcasted_iota(jnp.int32, (SUB, LANES), 0)
    lane = lax.broadcasted_iota(jnp.int32, (SUB, LANES), 1)
    ones = jnp.ones((dk, LANES), BF16)
    sl = pl.ds(h * dk, dk)
    rs_c = pl.ds(row0, C)
    bh = b_ref[:, sl]
    kf = kf_ref[:, sl]
    qf = q_ref[rs_c, sl].astype(F32) * scale
    vh = v_ref[rs_c, pl.ds(h * dv, dv)]
    st = st_ref[h]
    inter = _dot_nt((qf * jnp.exp(bh)).astype(BF16), st.astype(BF16))
    bl = b_ref[pl.ds(C - 1, 1), sl]
    kd = kf * jnp.exp(bl - bh)

    def off_block(r0, r1, c0, c1):
        ref_row = b_ref[pl.ds(r0, 1), sl]
        qt = qf[r0:r1] * jnp.exp(bh[r0:r1] - ref_row)
        ek = jnp.where((rows >= c0) & (rows < c1), ref_row - bh, NEG)
        kt = kf * jnp.exp(ek)
        return _dot_nt(qt.astype(BF16), kt.astype(BF16))

    lower = off_block(2 * SUB, C, 0, 2 * SUB)
    blocks = [None, off_block(SUB, 2 * SUB, 0, SUB), lower[0:SUB],
              lower[SUB:2 * SUB] + off_block(3 * SUB, C, 2 * SUB, 3 * SUB)]
    pieces = []
    for blk in range(C // SUB):
        r0 = blk * SUB
        q_i = qf[r0:r0 + SUB]
        b_i = bh[r0:r0 + SUB]
        for jj in range(SUB):
            b_j = b_ref[pl.ds(r0 + jj, 1), sl]
            k_j = kf_ref[pl.ds(r0 + jj, 1), sl]
            pieces.append((q_i * k_j * jnp.exp(jnp.minimum(b_i - b_j, 0.0))).astype(BF16))
    sums = _dot(jnp.concatenate(pieces, axis=0), ones)
    for blk in range(C // SUB):
        r0 = blk * SUB
        diag = jnp.zeros((SUB, LANES), F32)
        for jj in range(SUB):
            rs = sums[(r0 + jj) * SUB:(r0 + jj + 1) * SUB]
            diag = jnp.where((lane == r0 + jj) & (sub_i >= jj), rs, diag)
        diag = diag[:, 0:C]
        blocks[blk] = diag if blocks[blk] is None else blocks[blk] + diag
    att = jnp.concatenate(blocks, axis=0)
    o_ref[rs_c, pl.ds(h * dv, dv)] = (inter + _dot(att.astype(BF16), vh)).astype(o_ref.dtype)
    st_ref[h] = st * jnp.exp(bl) + _dot_tn(vh, kd.astype(BF16))


def _gla_head_bounded(row0, h, q_ref, v_ref, o_ref, st_ref, b_ref, kf_ref, scale):
    C, dk, dv = GLA_CHUNK, GLA_DK, GLA_DV
    sl = pl.ds(h * dk, dk)
    rs_c = pl.ds(row0, C)
    bh = b_ref[rs_c, sl]
    kf = kf_ref[rs_c, sl]
    qf = q_ref[rs_c, sl].astype(F32) * scale
    vh = v_ref[rs_c, pl.ds(h * dv, dv)]
    st = st_ref[h]
    qs = (qf * jnp.exp(bh)).astype(BF16)
    kt = (kf * jnp.exp(-bh)).astype(BF16)
    i = lax.broadcasted_iota(jnp.int32, (C, C), 0)
    j = lax.broadcasted_iota(jnp.int32, (C, C), 1)
    att = jnp.where(i >= j, _dot_nt(qs, kt), 0.0)
    inter = _dot_nt(qs, st.astype(BF16))
    o_ref[rs_c, pl.ds(h * dv, dv)] = (inter + _dot(att.astype(BF16), vh)).astype(o_ref.dtype)
    bl = b_ref[pl.ds(row0 + C - 1, 1), sl]
    kd = kf * jnp.exp(bl - bh)
    st_ref[h] = st * jnp.exp(bl) + _dot_tn(vh, kd.astype(BF16))


def _gla_kernel(q_ref, k_ref, v_ref, glr_ref, w2_ref, gb_ref, tril_ref, o_ref,
                st_ref, b_ref, kf_ref, bc_ref, kc_ref, *, scale):
    C, H = GLA_CHUNK, GLA_HEADS
    nch = q_ref.shape[0] // C

    @pl.when(pl.program_id(1) == 0)
    def _init():
        st_ref[...] = jnp.zeros(st_ref.shape, F32)

    whi, wlo = _hilo(w2_ref[...])
    part = tril_ref.shape[0]
    decay = None
    for r0 in range(0, nch * C, part):
        rs = pl.ds(r0, part)
        ghi, glo = _hilo(glr_ref[rs, :])
        pre = _dot(ghi, whi) + _dot(glo, whi) + _dot(ghi, wlo) + gb_ref[...]
        la = (jnp.minimum(pre, 0.0) - jnp.log(1.0 + jnp.exp(-jnp.abs(pre)))) * (1.0 / GLA_TAU)
        lhi, llo = _hilo(la)
        b_part = _dot(tril_ref[...], lhi) + _dot(tril_ref[...], llo)
        b_ref[rs, :] = b_part
        for c in range(part // C):
            d_c = -b_part[c * C + C - 1:(c + 1) * C, :]
            decay = d_c if decay is None else jnp.maximum(decay, d_c)
    kf_ref[...] = k_ref[...].astype(F32)
    bounded = jnp.max(decay) <= GLA_FACTOR_MAX_DECAY

    @pl.when(bounded)
    def _factored():
        for c in range(nch):
            for h in range(H):
                _gla_head_bounded(c * C, h, q_ref, v_ref, o_ref, st_ref, b_ref, kf_ref, scale)

    @pl.when(jnp.logical_not(bounded))
    def _exact():
        def chunk(c, carry):
            row0 = pl.multiple_of(c * C, C)
            bc_ref[...] = b_ref[pl.ds(row0, C), :]
            kc_ref[...] = kf_ref[pl.ds(row0, C), :]
            for h in range(H):
                _gla_head_exact(row0, h, q_ref, v_ref, o_ref, st_ref, bc_ref, kc_ref, scale)
            return carry
        lax.fori_loop(0, nch, chunk, 0)


def _gla(q, k, v, glr, w2p, gb, tril, *, B, S):
    C, H, dk, dv = GLA_CHUNK, GLA_HEADS, GLA_DK, GLA_DV
    rows = C * min(GLA_STEP_CHUNKS, S // C)
    nchunk = S // rows
    tok = lambda n: pl.BlockSpec((rows, n), lambda b, c: (b * nchunk + c, 0))
    const = lambda shape: pl.BlockSpec(shape, lambda b, c: (0,) * len(shape))
    return pl.pallas_call(
        functools.partial(_gla_kernel, scale=float(dk) ** -0.5),
        grid=(B, nchunk),
        in_specs=[tok(H * dk), tok(H * dk), tok(H * dv), tok(LANES),
                  const(w2p.shape), const(gb.shape), const(tril.shape)],
        out_specs=tok(H * dv),
        out_shape=jax.ShapeDtypeStruct((B * S, H * dv), BF16),
        scratch_shapes=[pltpu.VMEM((H, dv, dk), F32),
                        pltpu.VMEM((rows, H * dk), F32),
                        pltpu.VMEM((rows, H * dk), F32),
                        pltpu.VMEM((C, H * dk), F32),
                        pltpu.VMEM((C, H * dk), F32)],
        compiler_params=pltpu.CompilerParams(dimension_semantics=("parallel", "arbitrary"),
                                             vmem_limit_bytes=VMEM_LIMIT),
        name="gla",
    )(q, k, v, glr, w2p, gb, tril)


def _odd_out_kernel(x_ref, o_ref_in, z_ref, gn_ref, w_ref, g_ref, b_ref, out_ref, *, alpha):
    H, dv = GLA_HEADS, GLA_DV
    parts = []
    for h in range(H):
        sl = pl.ds(h * dv, dv)
        oh = o_ref_in[:, sl].astype(F32)
        r = lax.rsqrt(jnp.mean(oh * oh, axis=-1, keepdims=True) + LN_EPS)
        z = z_ref[:, sl].astype(F32)
        parts.append((oh * r * gn_ref[:, sl] * (z * _sigmoid(z))).astype(BF16))
    y = _dot(jnp.concatenate(parts, axis=1), w_ref[...])
    out_ref[...] = _layer_norm(alpha * x_ref[...] + y, g_ref[...], b_ref[...])


def _odd_out(x2d, o, z, gn_g, w_out, ln_g, ln_b, alpha, tm=512):
    T, D = x2d.shape
    W = o.shape[1]
    row = lambda n: pl.BlockSpec((tm, n), lambda i: (i, 0))
    return pl.pallas_call(
        functools.partial(_odd_out_kernel, alpha=alpha),
        grid=(T // tm,),
        in_specs=[row(D), row(W), row(W),
                  pl.BlockSpec((1, W), lambda i: (0, 0)),
                  pl.BlockSpec((W, D), lambda i: (0, 0)),
                  pl.BlockSpec((1, D), lambda i: (0, 0)),
                  pl.BlockSpec((1, D), lambda i: (0, 0))],
        out_specs=row(D),
        out_shape=jax.ShapeDtypeStruct((T, D), F32),
        compiler_params=pltpu.CompilerParams(dimension_semantics=("parallel",),
                                             vmem_limit_bytes=VMEM_LIMIT),
        name="odd_out",
    )(x2d, o, z, gn_g, w_out, ln_g, ln_b)


def _key_features(S, block):
    s = np.arange(S)
    f = np.zeros((S, LANES), np.float32)
    f[s, s // block] = 1.0
    f[:, FEAT + 0] = 1.0
    f[:, FEAT + 1] = 1.0
    f[:, FEAT + 2] = FEAT * (s // FEAT)
    f[:, FEAT + 3] = s % FEAT
    return jnp.asarray(f, BF16)


def _pad_to(a, n, axis):
    pad = [(0, 0)] * a.ndim
    pad[axis] = (0, n - a.shape[axis])
    return jnp.pad(a, pad)


def _even_layer(x2d, B, S, w_in, cmp_pos, cmp_w1, cmp_b1, cmp_w2, w_out, ln_g, ln_b, alpha):
    T, D = x2d.shape
    G, hd = NSA_KV_GROUPS, HEAD_DIM
    scale = hd ** -0.5
    wq, wkv, wgate, wnz, wmq, wmk, wmv, wmz = jnp.split(w_in, np.cumsum(EVEN_SPLITS)[:-1].tolist(), axis=1)
    wkc, wvc, wks, wvs, wkw, wvw = jnp.split(wkv, 6, axis=1)
    w_tok = jnp.concatenate([wkc, wvc, wks, wkw, wmk], axis=1).astype(BF16)
    w_tr = jnp.concatenate([wq * scale, wvs, wvw, wmq * scale, wmv, wnz, wmz,
                            _pad_to(wgate, GATE_ROWS, 1)], axis=1).T.astype(BF16)
    (kcmp, vcmp, ksel, kwin, mk, nq_t, vsel_t, vwin_t, mq_t, mv_t, nz_t, mz_t, gate_t) = _proj(
        x2d, w_tok, (NSA_KV_W,) * 4 + (MOBA_W,), (BF16,) * 5,
        w_tr, (NSA_W, NSA_KV_W, NSA_KV_W, MOBA_W, MOBA_W, NSA_W, MOBA_W, GATE_ROWS),
        (BF16,) * 7 + (F32,))

    L, d, HID = NSA_CMP_BLOCK, NSA_CMP_STRIDE, NSA_CMP_HIDDEN
    ncmp = (S - L) // d + 1
    NC = S // d
    eye = jnp.eye(G, dtype=F32)
    w1r = cmp_w1.reshape(2, 2, d, hd, HID)
    w1p = jnp.einsum('ialdj,gh->ialgdhj', w1r, eye).reshape(2, 2, d * G * hd, G * HID).astype(BF16)
    posp = jnp.broadcast_to(cmp_pos.reshape(2, 2, d, 1, hd), (2, 2, d, G, hd)).reshape(2, 2, 1, d * G * hd)
    b1p = jnp.tile(cmp_b1, (1, G)).reshape(2, 1, G * HID)
    w2p = jnp.einsum('ijd,gh->igjhd', _pad_to(cmp_w2, LANES, 2), eye).reshape(2, G * HID, G * LANES).astype(BF16)
    cfeat = np.zeros((NC, G, LANES), np.float32)
    cfeat[:, :, hd + 0] = 1.0
    cfeat[:, :, hd + 1] = 1.0
    cfeat[:, :, hd + 2] = (np.arange(NC) * d)[:, None]
    cfeat[:, :, hd + 3] = L - 1
    kvc = _cmp_mlp(kcmp.reshape(B, NC, d * NSA_KV_W), vcmp.reshape(B, NC, d * NSA_KV_W),
                   w1p, posp, b1p, w2p, jnp.asarray(cfeat.reshape(NC, G * LANES)), ncmp)

    ns = S // NSA_SEL_BLOCK
    cmp_start = np.arange(NC) * d
    sel_start = np.arange(FEAT) * NSA_SEL_BLOCK
    ovt = ((cmp_start[None, :] <= sel_start[:, None] + NSA_SEL_BLOCK - 1)
           & (cmp_start[None, :] + L - 1 >= sel_start[:, None])
           & (np.arange(NC)[None, :] < ncmp) & (np.arange(FEAT)[:, None] < ns))
    o_cmp, bias_sel = _cmp_attn(nq_t, kvc, jnp.asarray(ovt, BF16), B=B, S=S)

    feats_sel = _key_features(S, NSA_SEL_BLOCK)
    o_sel = _flash(nq_t, bias_sel, ksel, feats_sel, vsel_t,
                   B=B, S=S, units=G, tq=TQ, nh=NSA_HG, pair=False, window=None)
    o_win = _flash(nq_t, jnp.zeros_like(bias_sel), kwin, feats_sel, vwin_t,
                   B=B, S=S, units=G, tq=TQ, nh=NSA_HG, pair=False, window=NSA_WINDOW)

    avg = np.zeros((MOBA_NBP, S), np.float32)
    avg[np.arange(S) // MOBA_BLOCK, np.arange(S)] = 1.0 / MOBA_BLOCK
    bias_m = _moba_gate(mq_t, mk, jnp.asarray(avg, BF16), B=B, S=S)
    o_m = _flash(mq_t, bias_m, mk, _key_features(S, MOBA_BLOCK), mv_t,
                 B=B, S=S, units=MOBA_HEADS // MOBA_STEP_HEADS, tq=TQM, nh=MOBA_STEP_HEADS, pair=True, window=None)

    return _even_out(x2d, o_cmp, o_sel, o_win, o_m, gate_t, nz_t, mz_t, w_out.astype(BF16),
                     ln_g.reshape(1, D), ln_b.reshape(1, D), alpha, B=B, S=S)


def _odd_layer(x2d, B, S, w_in, gate_w2, gate_b, gn_g, w_out, ln_g, ln_b, alpha):
    T, D = x2d.shape
    H, dk, dv, C = GLA_HEADS, GLA_DK, GLA_DV, GLA_CHUNK
    wq, wk, wv, wg, wz = jnp.split(w_in, np.cumsum(ODD_SPLITS)[:-1].tolist(), axis=1)
    w_perm = jnp.concatenate([wq, wk, wv, wz, _pad_to(wg, LANES, 1)], axis=1).astype(BF16)
    q, k, v, z, glr = _proj(x2d, w_perm, (H * dk, H * dk, H * dv, H * dv, LANES),
                            (BF16, BF16, BF16, BF16, F32))
    w2p = _pad_to(gate_w2, LANES, 0)
    ncs = min(GLA_CUMSUM_CHUNKS, GLA_STEP_CHUNKS, S // C)
    tril = jnp.asarray(np.kron(np.eye(ncs), np.tril(np.ones((C, C)))), BF16)
    o = _gla(q, k, v, glr, w2p, gate_b.reshape(1, H * dk), tril, B=B, S=S)
    return _odd_out(x2d, o, z, gn_g.reshape(1, H * dv), w_out.astype(BF16),
                    ln_g.reshape(1, D), ln_b.reshape(1, D), alpha)


def kernel(x, ev_w_in, ev_cmp_pos, ev_cmp_w1, ev_cmp_b1, ev_cmp_w2, ev_w_out, ev_ln_g, ev_ln_b,
           od_w_in, od_gate_w2, od_gate_b, od_gn_g, od_w_out, od_ln_g, od_ln_b):
    B, S, D = x.shape
    depth = ev_w_in.shape[0] + od_w_in.shape[0]
    alpha = (2.0 * depth) ** 0.25
    h = x.reshape(B * S, D)
    for layer in range(depth):
        i = layer // 2
        if layer % 2 == 0:
            h = _even_layer(h, B, S, ev_w_in[i], ev_cmp_pos[i], ev_cmp_w1[i], ev_cmp_b1[i], ev_cmp_w2[i],
                            ev_w_out[i], ev_ln_g[i], ev_ln_b[i], alpha)
        else:
            h = _odd_layer(h, B, S, od_w_in[i], od_gate_w2[i], od_gate_b[i], od_gn_g[i],
                           od_w_out[i], od_ln_g[i], od_ln_b[i], alpha)
    return h.reshape(B, S, D)
```

```python
import functools

import numpy as np
import jax
import jax.numpy as jnp
from jax import lax
from jax.experimental import pallas as pl
from jax.experimental.pallas import tpu as pltpu

F32 = jnp.float32
BF16 = jnp.bfloat16

HEAD_DIM = 64
NSA_HEADS = 8
NSA_KV_GROUPS = 2
NSA_HG = NSA_HEADS // NSA_KV_GROUPS
NSA_CMP_BLOCK = 32
NSA_CMP_STRIDE = 16
NSA_CMP_HIDDEN = 128
NSA_SEL_BLOCK = 64
NSA_TOP_N = 16
NSA_WINDOW = 512
MOBA_HEADS = 8
MOBA_BLOCK = 256
MOBA_TOP_K = 3
GLA_HEADS = 4
GLA_DK = 128
GLA_DV = 256
GLA_GATE_RANK = 16
GLA_TAU = 16.0
GLA_CHUNK = 64
GLA_SUB = 16
GLA_STEP_CHUNKS = 16
GLA_CUMSUM_CHUNKS = 4
GLA_FACTOR_MAX_DECAY = 60.0
LN_EPS = 1e-5
NEG = -1e30
FORCE_BONUS = 1e4

NSA_W = NSA_HEADS * HEAD_DIM
NSA_KV_W = NSA_KV_GROUPS * HEAD_DIM
MOBA_W = MOBA_HEADS * HEAD_DIM
EVEN_SPLITS = (NSA_W, 6 * NSA_KV_W, 3 * NSA_HEADS, NSA_W, MOBA_W, MOBA_W, MOBA_W, MOBA_W)
ODD_SPLITS = (GLA_HEADS * GLA_DK, GLA_HEADS * GLA_DK, GLA_HEADS * GLA_DV, GLA_GATE_RANK, GLA_HEADS * GLA_DV)

LANES = 128
SUBLANES = 8
FEAT = 64
ALIBI_ROWS = 16
GATE_ROWS = 32
MOBA_NBP = 16
TQ = 512
TQM = 512
MOBA_STEP_HEADS = 4
TK = 512
CG = 256
MIN_WEIGHT_SUM = 1e-25
VROWS = HEAD_DIM + 16
VMEM_LIMIT = 48 * 1024 * 1024


def _dot(a, b):
    return jnp.dot(a, b, preferred_element_type=F32)


def _dot_nt(a, b):
    return lax.dot_general(a, b, (((1,), (1,)), ((), ())), preferred_element_type=F32)


def _dot_tn(a, b):
    return lax.dot_general(a, b, (((0,), (0,)), ((), ())), preferred_element_type=F32)


def _hilo(a):
    hi = a.astype(BF16)
    lo = (a - hi.astype(F32)).astype(BF16)
    return hi, lo


def _pow2_neg(n):
    return lax.bitcast_convert_type(lax.shift_left(127 - n, 23), F32)


def _sigmoid(x):
    return 0.5 * jnp.tanh(0.5 * x) + 0.5


def _layer_norm(r, g, b):
    mu = jnp.mean(r, axis=-1, keepdims=True)
    d = r - mu
    var = jnp.mean(d * d, axis=-1, keepdims=True)
    return d * lax.rsqrt(var + LN_EPS) * g + b


def _alibi_query_rows(t0, tq, head0, R):
    r = lax.broadcasted_iota(jnp.int32, (ALIBI_ROWS, R), 0)
    col = lax.broadcasted_iota(jnp.int32, (ALIBI_ROWS, R), 1)
    t = t0 + col % tq
    slope = _pow2_neg(head0 + col // tq + 1)
    feats = jnp.where(r == 0, -FEAT * slope * (t // FEAT).astype(F32),
                      jnp.where(r == 1, -slope * (t % FEAT).astype(F32), jnp.where(r < 4, slope, 0.0)))
    return jnp.concatenate([feats.astype(BF16), jnp.zeros((FEAT - ALIBI_ROWS, R), BF16)], axis=0)


def _ranks(sc_ref, n_real):
    n_rows, width = sc_ref.shape
    out = []
    for v in range(n_rows // SUBLANES):
        lo = v * SUBLANES
        sc = sc_ref[lo:lo + SUBLANES, :]
        idx = lo + lax.broadcasted_iota(jnp.int32, (SUBLANES, width), 0)
        rank = jnp.zeros((SUBLANES, width), jnp.int32)
        for jp in range(n_real):
            r = sc_ref[pl.ds(jp, 1), :]
            if jp < lo:
                beats = r >= sc
            elif jp >= lo + SUBLANES - 1:
                beats = r > sc
            else:
                beats = (r > sc) | ((r == sc) & (idx > jp))
            rank = rank + jnp.where(beats, 1, 0)
        out.append(rank)
    return jnp.concatenate(out, axis=0)


def _proj_kernel(*refs, tok_splits, tr_splits):
    x_ref, wt_ref = refs[0], refs[1]
    outs = list(refs[3:] if tr_splits else refs[2:])
    x = x_ref[...]
    xb = x.astype(BF16)
    off = 0
    for n in tok_splits:
        o_ref = outs.pop(0)
        o_ref[...] = _dot(xb, wt_ref[:, off:off + n]).astype(o_ref.dtype)
        off += n
    if tr_splits:
        wtr_ref = refs[2]
        xt = x.T.astype(BF16)
        off = 0
        for n in tr_splits:
            o_ref = outs.pop(0)
            o_ref[...] = _dot(wtr_ref[off:off + n, :], xt).astype(o_ref.dtype)
            off += n


def _proj(x2d, w_tok, tok_splits, tok_dtypes, w_tr=None, tr_splits=(), tr_dtypes=(), tm=512):
    T, D = x2d.shape
    assert T % tm == 0 and sum(tok_splits) == w_tok.shape[1]
    in_specs = [pl.BlockSpec((tm, D), lambda i: (i, 0)),
                pl.BlockSpec(w_tok.shape, lambda i: (0, 0))]
    args = [x2d, w_tok]
    if tr_splits:
        assert sum(tr_splits) == w_tr.shape[0]
        in_specs.append(pl.BlockSpec(w_tr.shape, lambda i: (0, 0)))
        args.append(w_tr)
    out_specs = ([pl.BlockSpec((tm, n), lambda i: (i, 0)) for n in tok_splits]
                 + [pl.BlockSpec((n, tm), lambda i: (0, i)) for n in tr_splits])
    out_shape = ([jax.ShapeDtypeStruct((T, n), dt) for n, dt in zip(tok_splits, tok_dtypes)]
                 + [jax.ShapeDtypeStruct((n, T), dt) for n, dt in zip(tr_splits, tr_dtypes)])
    return pl.pallas_call(
        functools.partial(_proj_kernel, tok_splits=tuple(tok_splits), tr_splits=tuple(tr_splits)),
        grid=(T // tm,), in_specs=in_specs, out_specs=out_specs, out_shape=out_shape,
        compiler_params=pltpu.CompilerParams(dimension_semantics=("parallel",),
                                             vmem_limit_bytes=VMEM_LIMIT),
        name="proj",
    )(*args)


def _cmp_mlp_kernel(kx_ref, vx_ref, w1_ref, pos_ref, b1_ref, w2_ref, cf_ref, o_ref, *, ncmp):
    for i, x_ref in enumerate((kx_ref, vx_ref)):
        x = x_ref[...]
        nc, cw = x.shape
        a = _dot(x, w1_ref[i, 0])
        bm = _dot(x, w1_ref[i, 1])
        c = b1_ref[i]
        for half in range(2):
            phi, plo = _hilo(jnp.broadcast_to(pos_ref[i, half], (SUBLANES, cw)))
            c = c + (_dot(phi, w1_ref[i, half]) + _dot(plo, w1_ref[i, half]))[0:1]
        hid = a + pltpu.roll(bm, nc - 1, axis=0) + c
        hid = hid * _sigmoid(hid)
        row = lax.broadcasted_iota(jnp.int32, hid.shape, 0)
        hid = jnp.where(row < ncmp, hid, 0.0)
        out = _dot(hid.astype(BF16), w2_ref[i])
        if i == 0:
            out = out + cf_ref[...]
        o_ref[i] = out.astype(o_ref.dtype)


def _cmp_mlp(kx, vx, w1p, posp, b1p, w2p, cfeat, ncmp):
    B, NC, CW = kx.shape
    OW = w2p.shape[-1]
    full = lambda a: pl.BlockSpec(a.shape, lambda b: (0,) * a.ndim)
    return pl.pallas_call(
        functools.partial(_cmp_mlp_kernel, ncmp=ncmp),
        grid=(B,),
        in_specs=[pl.BlockSpec((None, NC, CW), lambda b: (b, 0, 0)),
                  pl.BlockSpec((None, NC, CW), lambda b: (b, 0, 0)),
                  full(w1p), full(posp), full(b1p), full(w2p), full(cfeat)],
        out_specs=pl.BlockSpec((2, None, NC, OW), lambda b: (0, b, 0, 0)),
        out_shape=jax.ShapeDtypeStruct((2, B, NC, OW), BF16),
        compiler_params=pltpu.CompilerParams(dimension_semantics=("parallel",),
                                             vmem_limit_bytes=VMEM_LIMIT),
        name="cmp_mlp",
    )(kx, vx, w1p, posp, b1p, w2p, cfeat)


def _cmp_attn_kernel(q_ref, kc_ref, vc_ref, ovt_ref, o_ref, bias_ref, sc_ref, *, tq, nh, ns, ntop):
    g = pl.program_id(1)
    R = nh * tq
    nc = kc_ref.shape[0]
    nq = q_ref.shape[1] // tq

    def q_tile(qi, carry):
        toks = pl.ds(pl.multiple_of(qi * tq, tq), tq)
        qcat = jnp.concatenate([q_ref[hh * HEAD_DIM:(hh + 1) * HEAD_DIM, toks] for hh in range(nh)], axis=1)
        qa = jnp.concatenate([qcat, _alibi_query_rows(qi * tq, tq, g * nh, R)], axis=0)
        s = _dot(kc_ref[...], qa)
        n = lax.broadcasted_iota(jnp.int32, (nc, R), 0)
        t_row = qi * tq + lax.broadcasted_iota(jnp.int32, (1, R), 1) % tq
        last = (t_row - (NSA_CMP_BLOCK - 1)) // NSA_CMP_STRIDE
        s = jnp.where(n <= last, s, NEG)
        e = jnp.exp(s - jnp.max(s, axis=0, keepdims=True))
        inv = jnp.where(last >= 0, 1.0 / jnp.sum(e, axis=0, keepdims=True), 0.0)
        p = e * inv
        o_ref[:, pl.ds(pl.multiple_of(qi * R, R), R)] = _dot_tn(vc_ref[:, 0:HEAD_DIM],
                                                                 p.astype(BF16)).astype(o_ref.dtype)

        psum = p[:, 0:tq]
        for hh in range(1, nh):
            psum = psum + p[:, hh * tq:(hh + 1) * tq]
        phi, plo = _hilo(psum)
        imp = _dot(ovt_ref[...], phi) + _dot(ovt_ref[...], plo)

        j = lax.broadcasted_iota(jnp.int32, (FEAT, tq), 0)
        cur = (qi * tq + lax.broadcasted_iota(jnp.int32, (FEAT, tq), 1)) // NSA_SEL_BLOCK
        forced = jnp.where((j == 0) | (j == cur) | (j == cur - 1), FORCE_BONUS, 0.0)
        sc = jnp.where(j <= cur, imp + forced, NEG)
        blocks_so_far = ((qi + 1) * tq - 1) // NSA_SEL_BLOCK + 1

        @pl.when(blocks_so_far <= ntop)
        def _all_valid_blocks_selected():
            bias_ref[:, toks] = jnp.where(j <= cur, 0.0, NEG).astype(bias_ref.dtype)

        @pl.when(blocks_so_far > ntop)
        def _rank():
            sc_ref[...] = sc
            sel = (_ranks(sc_ref, ns) < ntop) & (j <= cur)
            bias_ref[:, toks] = jnp.where(sel, 0.0, NEG).astype(bias_ref.dtype)
        return carry

    lax.fori_loop(0, nq, q_tile, 0)


def _cmp_attn(nq_t, kvc, ovt, *, B, S):
    tq, nh, G = TQ, NSA_HG, NSA_KV_GROUPS
    R = nh * tq
    nq = S // tq
    NC = kvc.shape[2]
    ns = S // NSA_SEL_BLOCK
    assert ns <= FEAT
    ntop = min(NSA_TOP_N, ns)
    return pl.pallas_call(
        functools.partial(_cmp_attn_kernel, tq=tq, nh=nh, ns=ns, ntop=ntop),
        grid=(B, G),
        in_specs=[pl.BlockSpec((nh * HEAD_DIM, S), lambda b, g: (g, b)),
                  pl.BlockSpec((None, None, NC, LANES), lambda b, g: (0, b, 0, g)),
                  pl.BlockSpec((None, None, NC, LANES), lambda b, g: (1, b, 0, g)),
                  pl.BlockSpec(ovt.shape, lambda b, g: (0, 0))],
        out_specs=[pl.BlockSpec((None, None, HEAD_DIM, S * nh), lambda b, g: (b, g, 0, 0)),
                   pl.BlockSpec((None, None, FEAT, S), lambda b, g: (b, g, 0, 0))],
        out_shape=[jax.ShapeDtypeStruct((B, G, HEAD_DIM, S * nh), BF16),
                   jax.ShapeDtypeStruct((B, G, FEAT, S), BF16)],
        scratch_shapes=[pltpu.VMEM((FEAT, tq), F32)],
        compiler_params=pltpu.CompilerParams(dimension_semantics=("parallel", "parallel"),
                                             vmem_limit_bytes=VMEM_LIMIT),
        name="cmp_attn",
    )(nq_t, kvc, kvc, ovt)


def _moba_gate_kernel(q_ref, k_ref, a_ref, bias_ref, km_ref, sc_ref, *, tq, nb, ntop):
    h = pl.program_id(1)
    qi = pl.program_id(2)

    @pl.when(qi == 0)
    def _means():
        km_ref[...] = _dot(a_ref[...], k_ref[...])

    nbp = km_ref.shape[0]
    lane = lax.broadcasted_iota(jnp.int32, (nbp, LANES), 1)
    km = jnp.where(lane // HEAD_DIM == h % 2, km_ref[...], 0.0)
    khi, klo = _hilo(km)
    q = q_ref[...]
    qq = jnp.concatenate([q, q], axis=0)
    g_t = _dot(khi, qq) + _dot(klo, qq)
    n = lax.broadcasted_iota(jnp.int32, (nbp, tq), 0)
    cur = (qi * tq + lax.broadcasted_iota(jnp.int32, (nbp, tq), 1)) // MOBA_BLOCK
    sc = jnp.where(n < cur, g_t, NEG)
    sc_ref[...] = sc
    sel = ((_ranks(sc_ref, nb) < ntop) & (n < cur)) | (n == cur)
    bias_ref[0:nbp, :] = jnp.where(sel, 0.0, NEG).astype(bias_ref.dtype)
    bias_ref[nbp:FEAT, :] = jnp.zeros((FEAT - nbp, tq), bias_ref.dtype)


def _moba_gate(mq_t, mk, avg, *, B, S):
    tq, H = S, MOBA_HEADS
    nq = S // tq
    nb = S // MOBA_BLOCK
    nbp = avg.shape[0]
    assert nb <= nbp
    ntop = min(MOBA_TOP_K, nb)
    return pl.pallas_call(
        functools.partial(_moba_gate_kernel, tq=tq, nb=nb, ntop=ntop),
        grid=(B, H, nq),
        in_specs=[pl.BlockSpec((HEAD_DIM, tq), lambda b, h, i: (h, b * nq + i)),
                  pl.BlockSpec((S, LANES), lambda b, h, i: (b, h // 2)),
                  pl.BlockSpec(avg.shape, lambda b, h, i: (0, 0))],
        out_specs=pl.BlockSpec((None, None, FEAT, tq), lambda b, h, i: (b, h, 0, i)),
        out_shape=jax.ShapeDtypeStruct((B, H, FEAT, S), BF16),
        scratch_shapes=[pltpu.VMEM((nbp, LANES), F32), pltpu.VMEM((nbp, tq), F32)],
        compiler_params=pltpu.CompilerParams(dimension_semantics=("parallel", "parallel", "arbitrary"),
                                             vmem_limit_bytes=VMEM_LIMIT),
        name="moba_gate",
    )(mq_t, mk, avg)


def _flash_q_tile(qi, q_ref, bias_ref, mask_ref, o_ref, ka_ref, vta_ref, kn_ref, qa_ref, bnd_ref, m_ref,
                  acc_ref, alpha_ref, p_ref, *, tq, tk, nh, pair, window):
    u = pl.program_id(1)
    R = nh * tq
    q0 = pl.multiple_of(qi * tq, tq)
    toks = pl.ds(q0, tq)

    qcat = jnp.concatenate([q_ref[hh * HEAD_DIM:(hh + 1) * HEAD_DIM, toks] for hh in range(nh)], axis=1)
    qf = qcat.astype(F32)
    qq = jnp.concatenate([qf, qf], axis=0)
    rowi = lax.broadcasted_iota(jnp.int32, (2 * HEAD_DIM, R), 0)
    coli = lax.broadcasted_iota(jnp.int32, (2 * HEAD_DIM, R), 1)
    half = (coli // tq) % 2 if pair else u % 2
    qa_ref[0:2 * HEAD_DIM, :] = jnp.where(rowi // HEAD_DIM == half, qq, 0.0).astype(BF16)
    qa_ref[2 * HEAD_DIM:2 * HEAD_DIM + FEAT, :] = jnp.concatenate(
        [bias_ref[hh if pair else 0, :, toks] for hh in range(nh)], axis=1)
    qa_ref[2 * HEAD_DIM + FEAT:2 * LANES, :] = _alibi_query_rows(q0, tq, u * nh, R)
    if pair:
        kn = jnp.concatenate([kn_ref[hh:hh + 1, :] for hh in range(nh) for _ in range(tq // LANES)], axis=1)
    else:
        kn = jnp.concatenate([jnp.where(u % 2 == 0, kn_ref[0:1, :], kn_ref[1:2, :])] * (R // LANES), axis=1)
    qn = jnp.sqrt(jnp.sum(qf * qf, axis=0, keepdims=True))
    bnd_ref[...] = jnp.broadcast_to(qn * kn, bnd_ref.shape)

    ncg = R // CG
    kd = (q0 + tq - 1) // tk

    def key_rows(kind, c):
        off = (c * CG) % tq
        if kind is None:
            return 0, tk, None
        if kind == "diag":
            return 0, off + CG, off // CG
        return off, tk, (tk + off) // CG

    def sweep(running_max):
        acc_ref[...] = jnp.zeros(acc_ref.shape, F32)
        if running_max:
            m_ref[...] = jnp.full(m_ref.shape, NEG, F32)

        def values(ki, kind=None):
            for c in range(ncg):
                cols = pl.ds(c * CG, CG)
                r0, r1, _ = key_rows(kind, c)
                v = (c * CG) // tq if pair else 0
                pv = _dot(vta_ref[v, ki, :, r0:r1], p_ref[r0:r1, cols])
                if running_max:
                    acc_ref[:, cols] = alpha_ref[0:1, cols] * acc_ref[:, cols] + pv
                else:
                    acc_ref[:, cols] = acc_ref[:, cols] + pv

        def scores(ki, kind=None, prev=None, prev_kind=None):
            k0 = pl.multiple_of(ki * tk, tk)
            s_groups = []
            for c in range(ncg):
                r0, r1, _ = key_rows(kind, c)
                kb = (c * CG) // (2 * tq) if pair else 0
                s_groups.append(_dot(ka_ref[kb, pl.ds(k0 + r0, r1 - r0), :], qa_ref[:, pl.ds(c * CG, CG)]))
            if prev is not None:
                values(prev, prev_kind)
            for c, s in enumerate(s_groups):
                cols = pl.ds(c * CG, CG)
                r0, r1, entry = key_rows(kind, c)
                if entry is not None:
                    s = s + mask_ref[entry, r0:r1, :]
                if running_max:
                    m_prev = m_ref[0:1, cols]
                    m_new = jnp.maximum(m_prev, jnp.max(s, axis=0, keepdims=True))
                    alpha_ref[:, cols] = jnp.broadcast_to(jnp.exp(m_prev - m_new), (alpha_ref.shape[0], CG))
                    m_ref[:, cols] = jnp.broadcast_to(m_new, (m_ref.shape[0], CG))
                    p_ref[r0:r1, cols] = jnp.exp(s - m_new).astype(BF16)
                else:
                    p_ref[r0:r1, cols] = jnp.exp(s - bnd_ref[0:1, cols]).astype(BF16)

        def steady(lo, hi):
            def pair_body(j, carry):
                ki = lo + 2 * j
                scores(ki, prev=ki - 1)
                scores(ki + 1, prev=ki)
                return carry
            lax.fori_loop(0, (hi - lo) // 2, pair_body, 0)

            @pl.when((hi - lo) % 2 == 1)
            def _odd_one():
                scores(hi - 1, prev=hi - 2)

        @pl.when(kd == 0)
        def _only_diagonal():
            scores(0, "diag")
            values(0, "diag")

        @pl.when(kd > 0)
        def _past_then_diagonal():
            if window is None:
                scores(0)
                steady(1, kd)
                scores(kd, "diag", prev=kd - 1)
            else:
                scores(kd - 1, "wedge")
                scores(kd, "diag", prev=kd - 1, prev_kind="wedge")
            values(kd, "diag")

    def emit():
        acc = acc_ref[...]
        out = (acc[0:HEAD_DIM] / acc[HEAD_DIM:HEAD_DIM + 1]).astype(o_ref.dtype)
        if pair:
            for hh in range(nh):
                o_ref[hh, :, toks] = out[:, hh * tq:(hh + 1) * tq]
        else:
            o_ref[0, :, pl.ds(pl.multiple_of(qi * R, R), R)] = out

    sweep(False)
    trusted = jnp.min(acc_ref[HEAD_DIM:HEAD_DIM + 1, :]) >= MIN_WEIGHT_SUM

    @pl.when(trusted)
    def _done():
        emit()

    @pl.when(jnp.logical_not(trusted))
    def _redo_with_running_max():
        sweep(True)
        emit()


def _flash_kernel(q_ref, bias_ref, k_ref, kf_ref, vt_ref, mask_ref, o_ref, ka_ref, vta_ref, kn_ref, qa_ref,
                  bnd_ref, m_ref, acc_ref, alpha_ref, p_ref, *, tq, tk, nh, pair, window):
    S = k_ref.shape[0]
    for kb in range(ka_ref.shape[0]):
        k = k_ref[:, kb * LANES:(kb + 1) * LANES]
        ka_ref[kb, :, 0:LANES] = k
        ka_ref[kb, :, LANES:2 * LANES] = kf_ref[...]
        k2 = jnp.square(k.astype(F32))
        lane = lax.broadcasted_iota(jnp.int32, k2.shape, 1)
        for hf in range(2):
            n2 = jnp.sum(jnp.where(lane // HEAD_DIM == hf, k2, 0.0), axis=1, keepdims=True)
            kn_ref[2 * kb + hf:2 * kb + hf + 1, :] = jnp.broadcast_to(
                jnp.sqrt(jnp.max(n2, axis=0, keepdims=True)), (1, LANES))
    ones_row = jnp.where(lax.broadcasted_iota(jnp.int32, (VROWS - HEAD_DIM, tk), 0) == 0, 1.0, 0.0)
    for v in range(nh if pair else 1):
        for jt in range(S // tk):
            vta_ref[v, jt, 0:HEAD_DIM, :] = vt_ref[v * HEAD_DIM:(v + 1) * HEAD_DIM, jt * tk:(jt + 1) * tk]
            vta_ref[v, jt, HEAD_DIM:VROWS, :] = ones_row.astype(BF16)

    def q_tile(qi, carry):
        _flash_q_tile(qi, q_ref, bias_ref, mask_ref, o_ref, ka_ref, vta_ref, kn_ref, qa_ref, bnd_ref, m_ref,
                      acc_ref, alpha_ref, p_ref, tq=tq, tk=tk, nh=nh, pair=pair, window=window)
        return carry

    lax.fori_loop(0, S // tq, q_tile, 0)


def _mask_table(tk, window):
    n_delta = (tk + (window or 0)) // CG
    d = np.arange(n_delta)[:, None, None] * CG + np.arange(CG)[None, None, :] - np.arange(tk)[None, :, None]
    ok = (d >= 0) & ((d < window) if window else True)
    return jnp.asarray(np.where(ok, 0.0, NEG), F32)


def _flash(q_t, bias_t, k, kfeat, v_t, *, B, S, units, tq, nh, pair, window):
    tk = TK
    R = nh * tq
    nq = S // tq
    nv = nh if pair else 1
    nb = nh if pair else 1
    nkb = nh // 2 if pair else 1
    assert tq == tk and S % tk == 0 and tk % CG == 0 and window in (None, tk)
    mask = _mask_table(tk, window)
    kcol = (lambda u: u) if pair else (lambda u: u // 2)
    out_w = tq if pair else R
    out_shape = (B, units * nb, HEAD_DIM, S if pair else S * nh)
    return pl.pallas_call(
        functools.partial(_flash_kernel, tq=tq, tk=tk, nh=nh, pair=pair, window=window),
        grid=(B, units),
        in_specs=[pl.BlockSpec((nh * HEAD_DIM, S), lambda b, u: (u, b)),
                  pl.BlockSpec((None, nb, FEAT, S), lambda b, u: (b, u, 0, 0)),
                  pl.BlockSpec((S, nkb * LANES), lambda b, u: (b, kcol(u))),
                  pl.BlockSpec((S, LANES), lambda b, u: (0, 0)),
                  pl.BlockSpec((nv * HEAD_DIM, S), lambda b, u: (u, b)),
                  pl.BlockSpec(mask.shape, lambda b, u: (0, 0, 0))],
        out_specs=pl.BlockSpec((None, nb, HEAD_DIM, out_w * nq), lambda b, u: (b, u, 0, 0)),
        out_shape=jax.ShapeDtypeStruct(out_shape, BF16),
        scratch_shapes=[pltpu.VMEM((nkb, S, 2 * LANES), BF16),
                        pltpu.VMEM((nv, S // tk, VROWS, tk), BF16),
                        pltpu.VMEM((SUBLANES, LANES), F32),
                        pltpu.VMEM((2 * LANES, R), BF16),
                        pltpu.VMEM((SUBLANES, R), F32),
                        pltpu.VMEM((SUBLANES, R), F32),
                        pltpu.VMEM((VROWS, R), F32),
                        pltpu.VMEM((SUBLANES, R), F32),
                        pltpu.VMEM((tk, R), BF16)],
        compiler_params=pltpu.CompilerParams(dimension_semantics=("parallel", "parallel"),
                                             vmem_limit_bytes=VMEM_LIMIT),
        name="flash",
    )(q_t, bias_t, k, kfeat, v_t, mask)


def _even_out_kernel(x_ref, ocmp_ref, osel_ref, owin_ref, om_ref, gate_ref, nz_ref, mz_ref,
                     w_ref, g_ref, b_ref, o_ref, sig_ref, *, alpha, tm):
    G, nh = NSA_KV_GROUPS, NSA_HG
    sig_ref[...] = _sigmoid(gate_ref[...])
    heads = []
    for g in range(G):
        for hh in range(nh):
            acc = None
            for br, ref in enumerate((ocmp_ref, osel_ref, owin_ref)):
                gate = sig_ref[pl.ds(3 * (g * nh + hh) + br, 1), :]
                term = gate * ref[g, :, hh * tm:(hh + 1) * tm].astype(F32)
                acc = term if acc is None else acc + term
            heads.append(acc)
    o_nsa = jnp.concatenate(heads, axis=0)
    nz = nz_ref[...].astype(F32)
    mz = mz_ref[...].astype(F32)
    o_m = jnp.concatenate([om_ref[h] for h in range(MOBA_HEADS)], axis=0).astype(F32)
    a_t = jnp.concatenate([o_nsa * (nz * _sigmoid(nz)), o_m * (mz * _sigmoid(mz))], axis=0)
    y = _dot_tn(a_t.astype(BF16), w_ref[...])
    o_ref[...] = _layer_norm(alpha * x_ref[...] + y, g_ref[...], b_ref[...])


def _even_out(x2d, ocmp, osel, owin, om, gate_t, nz_t, mz_t, w_out, ln_g, ln_b, alpha, *, B, S):
    T, D = x2d.shape
    tm = TQ
    nq = S // tm
    G, nh, H = NSA_KV_GROUPS, NSA_HG, MOBA_HEADS
    nsa = pl.BlockSpec((None, G, HEAD_DIM, nh * tm), lambda i: (i // nq, 0, 0, i % nq))
    feat = lambda n: pl.BlockSpec((n, tm), lambda i: (0, i))
    full = lambda a: pl.BlockSpec(a.shape, lambda i: (0,) * a.ndim)
    return pl.pallas_call(
        functools.partial(_even_out_kernel, alpha=alpha, tm=tm),
        grid=(T // tm,),
        in_specs=[pl.BlockSpec((tm, D), lambda i: (i, 0)), nsa, nsa, nsa,
                  pl.BlockSpec((None, H, HEAD_DIM, tm), lambda i: (i // nq, 0, 0, i % nq)),
                  feat(GATE_ROWS), feat(NSA_W), feat(MOBA_W),
                  full(w_out), full(ln_g), full(ln_b)],
        out_specs=pl.BlockSpec((tm, D), lambda i: (i, 0)),
        out_shape=jax.ShapeDtypeStruct((T, D), F32),
        scratch_shapes=[pltpu.VMEM((GATE_ROWS, tm), F32)],
        compiler_params=pltpu.CompilerParams(dimension_semantics=("parallel",),
                                             vmem_limit_bytes=VMEM_LIMIT),
        name="even_out",
    )(x2d, ocmp, osel, owin, om, gate_t, nz_t, mz_t, w_out, ln_g, ln_b)


def _gla_head_exact(row0, h, q_ref, v_ref, o_ref, st_ref, b_ref, kf_ref, scale):
    C, dk, dv, SUB = GLA_CHUNK, GLA_DK, GLA_DV, GLA_SUB
    rows = lax.broadcasted_iota(jnp.int32, (C, dk), 0)
    sub_i = lax.broadcasted_iota(jnp.int32, (SUB, LANES), 0)
    lane = lax.broadcasted_iota(jnp.int32, (SUB, LANES), 1)
    ones = jnp.ones((dk, LANES), BF16)
    sl = pl.ds(h * dk, dk)
    rs_c = pl.ds(row0, C)
    bh = b_ref[:, sl]
    kf = kf_ref[:, sl]
    qf = q_ref[rs_c, sl].astype(F32) * scale
    vh = v_ref[rs_c, pl.ds(h * dv, dv)]
    st = st_ref[h]
    inter = _dot_nt((qf * jnp.exp(bh)).astype(BF16), st.astype(BF16))
    bl = b_ref[pl.ds(C - 1, 1), sl]
    kd = kf * jnp.exp(bl - bh)

    def off_block(r0, r1, c0, c1):
        ref_row = b_ref[pl.ds(r0, 1), sl]
        qt = qf[r0:r1] * jnp.exp(bh[r0:r1] - ref_row)
        ek = jnp.where((rows >= c0) & (rows < c1), ref_row - bh, NEG)
        kt = kf * jnp.exp(ek)
        return _dot_nt(qt.astype(BF16), kt.astype(BF16))

    lower = off_block(2 * SUB, C, 0, 2 * SUB)
    blocks = [None, off_block(SUB, 2 * SUB, 0, SUB), lower[0:SUB],
              lower[SUB:2 * SUB] + off_block(3 * SUB, C, 2 * SUB, 3 * SUB)]
    pieces = []
    for blk in range(C // SUB):
        r0 = blk * SUB
        q_i = qf[r0:r0 + SUB]
        b_i = bh[r0:r0 + SUB]
        for jj in range(SUB):
            b_j = b_ref[pl.ds(r0 + jj, 1), sl]
            k_j = kf_ref[pl.ds(r0 + jj, 1), sl]
            pieces.append((q_i * k_j * jnp.exp(jnp.minimum(b_i - b_j, 0.0))).astype(BF16))
    sums = _dot(jnp.concatenate(pieces, axis=0), ones)
    for blk in range(C // SUB):
        r0 = blk * SUB
        diag = jnp.zeros((SUB, LANES), F32)
        for jj in range(SUB):
            rs = sums[(r0 + jj) * SUB:(r0 + jj + 1) * SUB]
            diag = jnp.where((lane == r0 + jj) & (sub_i >= jj), rs, diag)
        diag = diag[:, 0:C]
        blocks[blk] = diag if blocks[blk] is None else blocks[blk] + diag
    att = jnp.concatenate(blocks, axis=0)
    o_ref[rs_c, pl.ds(h * dv, dv)] = (inter + _dot(att.astype(BF16), vh)).astype(o_ref.dtype)
    st_ref[h] = st * jnp.exp(bl) + _dot_tn(vh, kd.astype(BF16))


def _gla_head_bounded(row0, h, q_ref, v_ref, o_ref, st_ref, b_ref, kf_ref, scale):
    C, dk, dv = GLA_CHUNK, GLA_DK, GLA_DV
    sl = pl.ds(h * dk, dk)
    rs_c = pl.ds(row0, C)
    bh = b_ref[rs_c, sl]
    kf = kf_ref[rs_c, sl]
    qf = q_ref[rs_c, sl].astype(F32) * scale
    vh = v_ref[rs_c, pl.ds(h * dv, dv)]
    st = st_ref[h]
    qs = (qf * jnp.exp(bh)).astype(BF16)
    kt = (kf * jnp.exp(-bh)).astype(BF16)
    i = lax.broadcasted_iota(jnp.int32, (C, C), 0)
    j = lax.broadcasted_iota(jnp.int32, (C, C), 1)
    att = jnp.where(i >= j, _dot_nt(qs, kt), 0.0)
    inter = _dot_nt(qs, st.astype(BF16))
    o_ref[rs_c, pl.ds(h * dv, dv)] = (inter + _dot(att.astype(BF16), vh)).astype(o_ref.dtype)
    bl = b_ref[pl.ds(row0 + C - 1, 1), sl]
    kd = kf * jnp.exp(bl - bh)
    st_ref[h] = st * jnp.exp(bl) + _dot_tn(vh, kd.astype(BF16))


def _gla_kernel(q_ref, k_ref, v_ref, glr_ref, w2_ref, gb_ref, tril_ref, o_ref,
                st_ref, b_ref, kf_ref, bc_ref, kc_ref, *, scale):
    C, H = GLA_CHUNK, GLA_HEADS
    nch = q_ref.shape[0] // C

    @pl.when(pl.program_id(1) == 0)
    def _init():
        st_ref[...] = jnp.zeros(st_ref.shape, F32)

    whi, wlo = _hilo(w2_ref[...])
    part = tril_ref.shape[0]
    decay = None
    for r0 in range(0, nch * C, part):
        rs = pl.ds(r0, part)
        ghi, glo = _hilo(glr_ref[rs, :])
        pre = _dot(ghi, whi) + _dot(glo, whi) + _dot(ghi, wlo) + gb_ref[...]
        la = (jnp.minimum(pre, 0.0) - jnp.log(1.0 + jnp.exp(-jnp.abs(pre)))) * (1.0 / GLA_TAU)
        lhi, llo = _hilo(la)
        b_part = _dot(tril_ref[...], lhi) + _dot(tril_ref[...], llo)
        b_ref[rs, :] = b_part
        for c in range(part // C):
            d_c = -b_part[c * C + C - 1:(c + 1) * C, :]
            decay = d_c if decay is None else jnp.maximum(decay, d_c)
    kf_ref[...] = k_ref[...].astype(F32)
    bounded = jnp.max(decay) <= GLA_FACTOR_MAX_DECAY

    @pl.when(bounded)
    def _factored():
        for c in range(nch):
            for h in range(H):
                _gla_head_bounded(c * C, h, q_ref, v_ref, o_ref, st_ref, b_ref, kf_ref, scale)

    @pl.when(jnp.logical_not(bounded))
    def _exact():
        def chunk(c, carry):
            row0 = pl.multiple_of(c * C, C)
            bc_ref[...] = b_ref[pl.ds(row0, C), :]
            kc_ref[...] = kf_ref[pl.ds(row0, C), :]
            for h in range(H):
                _gla_head_exact(row0, h, q_ref, v_ref, o_ref, st_ref, bc_ref, kc_ref, scale)
            return carry
        lax.fori_loop(0, nch, chunk, 0)


def _gla(q, k, v, glr, w2p, gb, tril, *, B, S):
    C, H, dk, dv = GLA_CHUNK, GLA_HEADS, GLA_DK, GLA_DV
    rows = C * min(GLA_STEP_CHUNKS, S // C)
    nchunk = S // rows
    tok = lambda n: pl.BlockSpec((rows, n), lambda b, c: (b * nchunk + c, 0))
    const = lambda shape: pl.BlockSpec(shape, lambda b, c: (0,) * len(shape))
    return pl.pallas_call(
        functools.partial(_gla_kernel, scale=float(dk) ** -0.5),
        grid=(B, nchunk),
        in_specs=[tok(H * dk), tok(H * dk), tok(H * dv), tok(LANES),
                  const(w2p.shape), const(gb.shape), const(tril.shape)],
        out_specs=tok(H * dv),
        out_shape=jax.ShapeDtypeStruct((B * S, H * dv), BF16),
        scratch_shapes=[pltpu.VMEM((H, dv, dk), F32),
                        pltpu.VMEM((rows, H * dk), F32),
                        pltpu.VMEM((rows, H * dk), F32),
                        pltpu.VMEM((C, H * dk), F32),
                        pltpu.VMEM((C, H * dk), F32)],
        compiler_params=pltpu.CompilerParams(dimension_semantics=("parallel", "arbitrary"),
                                             vmem_limit_bytes=VMEM_LIMIT),
        name="gla",
    )(q, k, v, glr, w2p, gb, tril)


def _odd_out_kernel(x_ref, o_ref_in, z_ref, gn_ref, w_ref, g_ref, b_ref, out_ref, *, alpha):
    H, dv = GLA_HEADS, GLA_DV
    parts = []
    for h in range(H):
        sl = pl.ds(h * dv, dv)
        oh = o_ref_in[:, sl].astype(F32)
        r = lax.rsqrt(jnp.mean(oh * oh, axis=-1, keepdims=True) + LN_EPS)
        z = z_ref[:, sl].astype(F32)
        parts.append((oh * r * gn_ref[:, sl] * (z * _sigmoid(z))).astype(BF16))
    y = _dot(jnp.concatenate(parts, axis=1), w_ref[...])
    out_ref[...] = _layer_norm(alpha * x_ref[...] + y, g_ref[...], b_ref[...])


def _odd_out(x2d, o, z, gn_g, w_out, ln_g, ln_b, alpha, tm=1024):
    T, D = x2d.shape
    W = o.shape[1]
    row = lambda n: pl.BlockSpec((tm, n), lambda i: (i, 0))
    return pl.pallas_call(
        functools.partial(_odd_out_kernel, alpha=alpha),
        grid=(T // tm,),
        in_specs=[row(D), row(W), row(W),
                  pl.BlockSpec((1, W), lambda i: (0, 0)),
                  pl.BlockSpec((W, D), lambda i: (0, 0)),
                  pl.BlockSpec((1, D), lambda i: (0, 0)),
                  pl.BlockSpec((1, D), lambda i: (0, 0))],
        out_specs=row(D),
        out_shape=jax.ShapeDtypeStruct((T, D), F32),
        compiler_params=pltpu.CompilerParams(dimension_semantics=("parallel",),
                                             vmem_limit_bytes=VMEM_LIMIT),
        name="odd_out",
    )(x2d, o, z, gn_g, w_out, ln_g, ln_b)


def _key_features(S, block):
    s = np.arange(S)
    f = np.zeros((S, LANES), np.float32)
    f[s, s // block] = 1.0
    f[:, FEAT + 0] = 1.0
    f[:, FEAT + 1] = 1.0
    f[:, FEAT + 2] = FEAT * (s // FEAT)
    f[:, FEAT + 3] = s % FEAT
    return jnp.asarray(f, BF16)


def _pad_to(a, n, axis):
    pad = [(0, 0)] * a.ndim
    pad[axis] = (0, n - a.shape[axis])
    return jnp.pad(a, pad)


def _even_layer(x2d, B, S, w_in, cmp_pos, cmp_w1, cmp_b1, cmp_w2, w_out, ln_g, ln_b, alpha):
    T, D = x2d.shape
    G, hd = NSA_KV_GROUPS, HEAD_DIM
    scale = hd ** -0.5
    wq, wkv, wgate, wnz, wmq, wmk, wmv, wmz = jnp.split(w_in, np.cumsum(EVEN_SPLITS)[:-1].tolist(), axis=1)
    wkc, wvc, wks, wvs, wkw, wvw = jnp.split(wkv, 6, axis=1)
    w_tok = jnp.concatenate([wkc, wvc, wks, wkw, wmk], axis=1).astype(BF16)
    w_tr = jnp.concatenate([wq * scale, wvs, wvw, wmq * scale, wmv, wnz, wmz,
                            _pad_to(wgate, GATE_ROWS, 1)], axis=1).T.astype(BF16)
    (kcmp, vcmp, ksel, kwin, mk, nq_t, vsel_t, vwin_t, mq_t, mv_t, nz_t, mz_t, gate_t) = _proj(
        x2d, w_tok, (NSA_KV_W,) * 4 + (MOBA_W,), (BF16,) * 5,
        w_tr, (NSA_W, NSA_KV_W, NSA_KV_W, MOBA_W, MOBA_W, NSA_W, MOBA_W, GATE_ROWS),
        (BF16,) * 7 + (F32,))

    L, d, HID = NSA_CMP_BLOCK, NSA_CMP_STRIDE, NSA_CMP_HIDDEN
    ncmp = (S - L) // d + 1
    NC = S // d
    eye = jnp.eye(G, dtype=F32)
    w1r = cmp_w1.reshape(2, 2, d, hd, HID)
    w1p = jnp.einsum('ialdj,gh->ialgdhj', w1r, eye).reshape(2, 2, d * G * hd, G * HID).astype(BF16)
    posp = jnp.broadcast_to(cmp_pos.reshape(2, 2, d, 1, hd), (2, 2, d, G, hd)).reshape(2, 2, 1, d * G * hd)
    b1p = jnp.tile(cmp_b1, (1, G)).reshape(2, 1, G * HID)
    w2p = jnp.einsum('ijd,gh->igjhd', _pad_to(cmp_w2, LANES, 2), eye).reshape(2, G * HID, G * LANES).astype(BF16)
    cfeat = np.zeros((NC, G, LANES), np.float32)
    cfeat[:, :, hd + 0] = 1.0
    cfeat[:, :, hd + 1] = 1.0
    cfeat[:, :, hd + 2] = (np.arange(NC) * d)[:, None]
    cfeat[:, :, hd + 3] = L - 1
    kvc = _cmp_mlp(kcmp.reshape(B, NC, d * NSA_KV_W), vcmp.reshape(B, NC, d * NSA_KV_W),
                   w1p, posp, b1p, w2p, jnp.asarray(cfeat.reshape(NC, G * LANES)), ncmp)

    ns = S // NSA_SEL_BLOCK
    cmp_start = np.arange(NC) * d
    sel_start = np.arange(FEAT) * NSA_SEL_BLOCK
    ovt = ((cmp_start[None, :] <= sel_start[:, None] + NSA_SEL_BLOCK - 1)
           & (cmp_start[None, :] + L - 1 >= sel_start[:, None])
           & (np.arange(NC)[None, :] < ncmp) & (np.arange(FEAT)[:, None] < ns))
    o_cmp, bias_sel = _cmp_attn(nq_t, kvc, jnp.asarray(ovt, BF16), B=B, S=S)

    feats_sel = _key_features(S, NSA_SEL_BLOCK)
    o_sel = _flash(nq_t, bias_sel, ksel, feats_sel, vsel_t,
                   B=B, S=S, units=G, tq=TQ, nh=NSA_HG, pair=False, window=None)
    o_win = _flash(nq_t, jnp.zeros_like(bias_sel), kwin, feats_sel, vwin_t,
                   B=B, S=S, units=G, tq=TQ, nh=NSA_HG, pair=False, window=NSA_WINDOW)

    avg = np.zeros((MOBA_NBP, S), np.float32)
    avg[np.arange(S) // MOBA_BLOCK, np.arange(S)] = 1.0 / MOBA_BLOCK
    bias_m = _moba_gate(mq_t, mk, jnp.asarray(avg, BF16), B=B, S=S)
    o_m = _flash(mq_t, bias_m, mk, _key_features(S, MOBA_BLOCK), mv_t,
                 B=B, S=S, units=MOBA_HEADS // MOBA_STEP_HEADS, tq=TQM, nh=MOBA_STEP_HEADS, pair=True, window=None)

    return _even_out(x2d, o_cmp, o_sel, o_win, o_m, gate_t, nz_t, mz_t, w_out.astype(BF16),
                     ln_g.reshape(1, D), ln_b.reshape(1, D), alpha, B=B, S=S)


def _odd_layer(x2d, B, S, w_in, gate_w2, gate_b, gn_g, w_out, ln_g, ln_b, alpha):
    T, D = x2d.shape
    H, dk, dv, C = GLA_HEADS, GLA_DK, GLA_DV, GLA_CHUNK
    wq, wk, wv, wg, wz = jnp.split(w_in, np.cumsum(ODD_SPLITS)[:-1].tolist(), axis=1)
    w_perm = jnp.concatenate([wq, wk, wv, wz, _pad_to(wg, LANES, 1)], axis=1).astype(BF16)
    q, k, v, z, glr = _proj(x2d, w_perm, (H * dk, H * dk, H * dv, H * dv, LANES),
                            (BF16, BF16, BF16, BF16, F32))
    w2p = _pad_to(gate_w2, LANES, 0)
    ncs = min(GLA_CUMSUM_CHUNKS, GLA_STEP_CHUNKS, S // C)
    tril = jnp.asarray(np.kron(np.eye(ncs), np.tril(np.ones((C, C)))), BF16)
    o = _gla(q, k, v, glr, w2p, gate_b.reshape(1, H * dk), tril, B=B, S=S)
    return _odd_out(x2d, o, z, gn_g.reshape(1, H * dv), w_out.astype(BF16),
                    ln_g.reshape(1, D), ln_b.reshape(1, D), alpha)


def kernel(x, ev_w_in, ev_cmp_pos, ev_cmp_w1, ev_cmp_b1, ev_cmp_w2, ev_w_out, ev_ln_g, ev_ln_b,
           od_w_in, od_gate_w2, od_gate_b, od_gn_g, od_w_out, od_ln_g, od_ln_b):
    B, S, D = x.shape
    depth = ev_w_in.shape[0] + od_w_in.shape[0]
    alpha = (2.0 * depth) ** 0.25
    h = x.reshape(B * S, D)
    for layer in range(depth):
        i = layer // 2
        if layer % 2 == 0:
            h = _even_layer(h, B, S, ev_w_in[i], ev_cmp_pos[i], ev_cmp_w1[i], ev_cmp_b1[i], ev_cmp_w2[i],
                            ev_w_out[i], ev_ln_g[i], ev_ln_b[i], alpha)
        else:
            h = _odd_layer(h, B, S, od_w_in[i], od_gate_w2[i], od_gate_b[i], od_gn_g[i],
                           od_w_out[i], od_ln_g[i], od_ln_b[i], alpha)
    return h.reshape(B, S, D)
```

```python
import functools

import numpy as np
import jax
import jax.numpy as jnp
from jax import lax
from jax.experimental import pallas as pl
from jax.experimental.pallas import tpu as pltpu

F32 = jnp.float32
BF16 = jnp.bfloat16

HEAD_DIM = 64
NSA_HEADS = 8
NSA_KV_GROUPS = 2
NSA_HG = NSA_HEADS // NSA_KV_GROUPS
NSA_CMP_BLOCK = 32
NSA_CMP_STRIDE = 16
NSA_CMP_HIDDEN = 128
NSA_SEL_BLOCK = 64
NSA_TOP_N = 16
NSA_WINDOW = 512
MOBA_HEADS = 8
MOBA_BLOCK = 256
MOBA_TOP_K = 3
GLA_HEADS = 4
GLA_DK = 128
GLA_DV = 256
GLA_GATE_RANK = 16
GLA_TAU = 16.0
GLA_CHUNK = 64
GLA_SUB = 16
GLA_STEP_CHUNKS = 16
GLA_CUMSUM_CHUNKS = 4
GLA_FACTOR_MAX_DECAY = 60.0
LN_EPS = 1e-5
NEG = -1e30
FORCE_BONUS = 1e4

NSA_W = NSA_HEADS * HEAD_DIM
NSA_KV_W = NSA_KV_GROUPS * HEAD_DIM
MOBA_W = MOBA_HEADS * HEAD_DIM
EVEN_SPLITS = (NSA_W, 6 * NSA_KV_W, 3 * NSA_HEADS, NSA_W, MOBA_W, MOBA_W, MOBA_W, MOBA_W)
ODD_SPLITS = (GLA_HEADS * GLA_DK, GLA_HEADS * GLA_DK, GLA_HEADS * GLA_DV, GLA_GATE_RANK, GLA_HEADS * GLA_DV)

LANES = 128
SUBLANES = 8
FEAT = 64
ALIBI_ROWS = 16
GATE_ROWS = 32
MOBA_NBP = 16
TQ = 512
TQM = 512
MOBA_STEP_HEADS = 4
TK = 512
CG = 256
MIN_WEIGHT_SUM = 1e-25
VROWS = HEAD_DIM + 16
VMEM_LIMIT = 48 * 1024 * 1024


def _dot(a, b):
    return jnp.dot(a, b, preferred_element_type=F32)


def _dot_nt(a, b):
    return lax.dot_general(a, b, (((1,), (1,)), ((), ())), preferred_element_type=F32)


def _dot_tn(a, b):
    return lax.dot_general(a, b, (((0,), (0,)), ((), ())), preferred_element_type=F32)


def _hilo(a):
    hi = a.astype(BF16)
    lo = (a - hi.astype(F32)).astype(BF16)
    return hi, lo


def _pow2_neg(n):
    return lax.bitcast_convert_type(lax.shift_left(127 - n, 23), F32)


def _sigmoid(x):
    return 0.5 * jnp.tanh(0.5 * x) + 0.5


def _layer_norm(r, g, b):
    mu = jnp.mean(r, axis=-1, keepdims=True)
    d = r - mu
    var = jnp.mean(d * d, axis=-1, keepdims=True)
    return d * lax.rsqrt(var + LN_EPS) * g + b


def _alibi_query_rows(t0, tq, head0, R):
    r = lax.broadcasted_iota(jnp.int32, (ALIBI_ROWS, R), 0)
    col = lax.broadcasted_iota(jnp.int32, (ALIBI_ROWS, R), 1)
    t = t0 + col % tq
    slope = _pow2_neg(head0 + col // tq + 1)
    feats = jnp.where(r == 0, -FEAT * slope * (t // FEAT).astype(F32),
                      jnp.where(r == 1, -slope * (t % FEAT).astype(F32), jnp.where(r < 4, slope, 0.0)))
    return jnp.concatenate([feats.astype(BF16), jnp.zeros((FEAT - ALIBI_ROWS, R), BF16)], axis=0)


def _ranks(sc_ref, n_real):
    n_rows, width = sc_ref.shape
    out = []
    for v in range(n_rows // SUBLANES):
        lo = v * SUBLANES
        sc = sc_ref[lo:lo + SUBLANES, :]
        idx = lo + lax.broadcasted_iota(jnp.int32, (SUBLANES, width), 0)
        rank = jnp.zeros((SUBLANES, width), jnp.int32)
        for jp in range(n_real):
            r = sc_ref[pl.ds(jp, 1), :]
            if jp < lo:
                beats = r >= sc
            elif jp >= lo + SUBLANES - 1:
                beats = r > sc
            else:
                beats = (r > sc) | ((r == sc) & (idx > jp))
            rank = rank + jnp.where(beats, 1, 0)
        out.append(rank)
    return jnp.concatenate(out, axis=0)


def _proj_kernel(*refs, tok_splits, tr_splits):
    x_ref, wt_ref = refs[0], refs[1]
    outs = list(refs[3:] if tr_splits else refs[2:])
    x = x_ref[...]
    xb = x.astype(BF16)
    off = 0
    for n in tok_splits:
        o_ref = outs.pop(0)
        o_ref[...] = _dot(xb, wt_ref[:, off:off + n]).astype(o_ref.dtype)
        off += n
    if tr_splits:
        wtr_ref = refs[2]
        xt = x.T.astype(BF16)
        off = 0
        for n in tr_splits:
            o_ref = outs.pop(0)
            o_ref[...] = _dot(wtr_ref[off:off + n, :], xt).astype(o_ref.dtype)
            off += n


def _proj(x2d, w_tok, tok_splits, tok_dtypes, w_tr=None, tr_splits=(), tr_dtypes=(), tm=512):
    T, D = x2d.shape
    assert T % tm == 0 and sum(tok_splits) == w_tok.shape[1]
    in_specs = [pl.BlockSpec((tm, D), lambda i: (i, 0)),
                pl.BlockSpec(w_tok.shape, lambda i: (0, 0))]
    args = [x2d, w_tok]
    if tr_splits:
        assert sum(tr_splits) == w_tr.shape[0]
        in_specs.append(pl.BlockSpec(w_tr.shape, lambda i: (0, 0)))
        args.append(w_tr)
    out_specs = ([pl.BlockSpec((tm, n), lambda i: (i, 0)) for n in tok_splits]
                 + [pl.BlockSpec((n, tm), lambda i: (0, i)) for n in tr_splits])
    out_shape = ([jax.ShapeDtypeStruct((T, n), dt) for n, dt in zip(tok_splits, tok_dtypes)]
                 + [jax.ShapeDtypeStruct((n, T), dt) for n, dt in zip(tr_splits, tr_dtypes)])
    return pl.pallas_call(
        functools.partial(_proj_kernel, tok_splits=tuple(tok_splits), tr_splits=tuple(tr_splits)),
        grid=(T // tm,), in_specs=in_specs, out_specs=out_specs, out_shape=out_shape,
        compiler_params=pltpu.CompilerParams(dimension_semantics=("parallel",),
                                             vmem_limit_bytes=VMEM_LIMIT),
        name="proj",
    )(*args)


def _cmp_mlp_kernel(kx_ref, vx_ref, w1_ref, pos_ref, b1_ref, w2_ref, cf_ref, o_ref, *, ncmp):
    d = NSA_CMP_STRIDE
    nc = kx_ref.shape[0] // d
    cw = w1_ref.shape[2]
    for i, x_ref in enumerate((kx_ref, vx_ref)):
        a = bm = None
        for l in range(0, d, 2):
            x2 = jnp.concatenate([x_ref[pl.ds(l, nc, stride=d), :], x_ref[pl.ds(l + 1, nc, stride=d), :]],
                                 axis=1).astype(BF16)
            rows = pl.ds(l * x_ref.shape[1], 2 * x_ref.shape[1])
            a_l, b_l = _dot(x2, w1_ref[i, 0, rows, :]), _dot(x2, w1_ref[i, 1, rows, :])
            a, bm = (a_l, b_l) if a is None else (a + a_l, bm + b_l)
        c = b1_ref[i]
        for half in range(2):
            phi, plo = _hilo(jnp.broadcast_to(pos_ref[i, half], (SUBLANES, cw)))
            c = c + (_dot(phi, w1_ref[i, half]) + _dot(plo, w1_ref[i, half]))[0:1]
        hid = a + pltpu.roll(bm, nc - 1, axis=0) + c
        hid = hid * _sigmoid(hid)
        row = lax.broadcasted_iota(jnp.int32, hid.shape, 0)
        hid = jnp.where(row < ncmp, hid, 0.0)
        out = _dot(hid.astype(BF16), w2_ref[i])
        if i == 0:
            out = out + cf_ref[...]
        o_ref[i] = out.astype(o_ref.dtype)


def _cmp_mlp(kx, vx, w1p, posp, b1p, w2p, cfeat, ncmp, *, B, S):
    NC = S // NSA_CMP_STRIDE
    OW = w2p.shape[-1]
    full = lambda a: pl.BlockSpec(a.shape, lambda b: (0,) * a.ndim)
    return pl.pallas_call(
        functools.partial(_cmp_mlp_kernel, ncmp=ncmp),
        grid=(B,),
        in_specs=[pl.BlockSpec((S, kx.shape[1]), lambda b: (b, 0)),
                  pl.BlockSpec((S, vx.shape[1]), lambda b: (b, 0)),
                  full(w1p), full(posp), full(b1p), full(w2p), full(cfeat)],
        out_specs=pl.BlockSpec((2, None, NC, OW), lambda b: (0, b, 0, 0)),
        out_shape=jax.ShapeDtypeStruct((2, B, NC, OW), BF16),
        compiler_params=pltpu.CompilerParams(dimension_semantics=("parallel",),
                                             vmem_limit_bytes=VMEM_LIMIT),
        name="cmp_mlp",
    )(kx, vx, w1p, posp, b1p, w2p, cfeat)


def _cmp_attn_kernel(q_ref, kc_ref, vc_ref, ovt_ref, o_ref, bias_ref, sc_ref, *, tq, nh, ns, ntop):
    g = pl.program_id(1)
    R = nh * tq
    nc = kc_ref.shape[0]
    nq = q_ref.shape[1] // tq

    def q_tile(qi, carry):
        toks = pl.ds(pl.multiple_of(qi * tq, tq), tq)
        qcat = jnp.concatenate([q_ref[hh * HEAD_DIM:(hh + 1) * HEAD_DIM, toks] for hh in range(nh)], axis=1)
        qa = jnp.concatenate([qcat, _alibi_query_rows(qi * tq, tq, g * nh, R)], axis=0)
        s = _dot(kc_ref[...], qa)
        n = lax.broadcasted_iota(jnp.int32, (nc, R), 0)
        t_row = qi * tq + lax.broadcasted_iota(jnp.int32, (1, R), 1) % tq
        last = (t_row - (NSA_CMP_BLOCK - 1)) // NSA_CMP_STRIDE
        s = jnp.where(n <= last, s, NEG)
        e = jnp.exp(s - jnp.max(s, axis=0, keepdims=True))
        inv = jnp.where(last >= 0, 1.0 / jnp.sum(e, axis=0, keepdims=True), 0.0)
        p = e * inv
        o_ref[:, pl.ds(pl.multiple_of(qi * R, R), R)] = _dot_tn(vc_ref[:, 0:HEAD_DIM],
                                                                 p.astype(BF16)).astype(o_ref.dtype)

        psum = p[:, 0:tq]
        for hh in range(1, nh):
            psum = psum + p[:, hh * tq:(hh + 1) * tq]
        phi, plo = _hilo(psum)
        imp = _dot(ovt_ref[...], phi) + _dot(ovt_ref[...], plo)

        j = lax.broadcasted_iota(jnp.int32, (FEAT, tq), 0)
        cur = (qi * tq + lax.broadcasted_iota(jnp.int32, (FEAT, tq), 1)) // NSA_SEL_BLOCK
        forced = jnp.where((j == 0) | (j == cur) | (j == cur - 1), FORCE_BONUS, 0.0)
        sc = jnp.where(j <= cur, imp + forced, NEG)
        blocks_so_far = ((qi + 1) * tq - 1) // NSA_SEL_BLOCK + 1

        @pl.when(blocks_so_far <= ntop)
        def _all_valid_blocks_selected():
            bias_ref[:, toks] = jnp.where(j <= cur, 0.0, NEG).astype(bias_ref.dtype)

        @pl.when(blocks_so_far > ntop)
        def _rank():
            sc_ref[...] = sc
            sel = (_ranks(sc_ref, ns) < ntop) & (j <= cur)
            bias_ref[:, toks] = jnp.where(sel, 0.0, NEG).astype(bias_ref.dtype)
        return carry

    lax.fori_loop(0, nq, q_tile, 0)


def _cmp_attn(nq_t, kvc, ovt, *, B, S):
    tq, nh, G = TQ, NSA_HG, NSA_KV_GROUPS
    R = nh * tq
    nq = S // tq
    NC = kvc.shape[2]
    ns = S // NSA_SEL_BLOCK
    assert ns <= FEAT
    ntop = min(NSA_TOP_N, ns)
    return pl.pallas_call(
        functools.partial(_cmp_attn_kernel, tq=tq, nh=nh, ns=ns, ntop=ntop),
        grid=(B, G),
        in_specs=[pl.BlockSpec((nh * HEAD_DIM, S), lambda b, g: (g, b)),
                  pl.BlockSpec((None, None, NC, LANES), lambda b, g: (0, b, 0, g)),
                  pl.BlockSpec((None, None, NC, LANES), lambda b, g: (1, b, 0, g)),
                  pl.BlockSpec(ovt.shape, lambda b, g: (0, 0))],
        out_specs=[pl.BlockSpec((None, None, HEAD_DIM, S * nh), lambda b, g: (b, g, 0, 0)),
                   pl.BlockSpec((None, None, FEAT, S), lambda b, g: (b, g, 0, 0))],
        out_shape=[jax.ShapeDtypeStruct((B, G, HEAD_DIM, S * nh), BF16),
                   jax.ShapeDtypeStruct((B, G, FEAT, S), BF16)],
        scratch_shapes=[pltpu.VMEM((FEAT, tq), F32)],
        compiler_params=pltpu.CompilerParams(dimension_semantics=("parallel", "parallel"),
                                             vmem_limit_bytes=VMEM_LIMIT),
        name="cmp_attn",
    )(nq_t, kvc, kvc, ovt)


def _moba_gate_kernel(q_ref, k_ref, a_ref, bias_ref, km_ref, sc_ref, *, tq, nb, ntop):
    h = pl.program_id(1)
    qi = pl.program_id(2)

    @pl.when(qi == 0)
    def _means():
        km_ref[...] = _dot(a_ref[...], k_ref[...])

    nbp = km_ref.shape[0]
    lane = lax.broadcasted_iota(jnp.int32, (nbp, LANES), 1)
    km = jnp.where(lane // HEAD_DIM == h % 2, km_ref[...], 0.0)
    khi, klo = _hilo(km)
    q = q_ref[...]
    qq = jnp.concatenate([q, q], axis=0)
    g_t = _dot(khi, qq) + _dot(klo, qq)
    n = lax.broadcasted_iota(jnp.int32, (nbp, tq), 0)
    cur = (qi * tq + lax.broadcasted_iota(jnp.int32, (nbp, tq), 1)) // MOBA_BLOCK
    sc = jnp.where(n < cur, g_t, NEG)
    sc_ref[...] = sc
    sel = ((_ranks(sc_ref, nb) < ntop) & (n < cur)) | (n == cur)
    bias_ref[0:nbp, :] = jnp.where(sel, 0.0, NEG).astype(bias_ref.dtype)
    bias_ref[nbp:FEAT, :] = jnp.zeros((FEAT - nbp, tq), bias_ref.dtype)


def _moba_gate(mq_t, mk, avg, *, B, S):
    tq, H = S, MOBA_HEADS
    nq = S // tq
    nb = S // MOBA_BLOCK
    nbp = avg.shape[0]
    assert nb <= nbp
    ntop = min(MOBA_TOP_K, nb)
    return pl.pallas_call(
        functools.partial(_moba_gate_kernel, tq=tq, nb=nb, ntop=ntop),
        grid=(B, H, nq),
        in_specs=[pl.BlockSpec((HEAD_DIM, tq), lambda b, h, i: (h, b * nq + i)),
                  pl.BlockSpec((S, LANES), lambda b, h, i: (b, h // 2)),
                  pl.BlockSpec(avg.shape, lambda b, h, i: (0, 0))],
        out_specs=pl.BlockSpec((None, None, FEAT, tq), lambda b, h, i: (b, h, 0, i)),
        out_shape=jax.ShapeDtypeStruct((B, H, FEAT, S), BF16),
        scratch_shapes=[pltpu.VMEM((nbp, LANES), F32), pltpu.VMEM((nbp, tq), F32)],
        compiler_params=pltpu.CompilerParams(dimension_semantics=("parallel", "parallel", "arbitrary"),
                                             vmem_limit_bytes=VMEM_LIMIT),
        name="moba_gate",
    )(mq_t, mk, avg)


def _flash_q_tile(qi, q_ref, bias_ref, mask_ref, o_ref, ka_ref, vta_ref, kn_ref, qa_ref, bnd_ref, m_ref,
                  acc_ref, alpha_ref, p_ref, *, tq, tk, nh, pair, window):
    u = pl.program_id(1)
    R = nh * tq
    q0 = pl.multiple_of(qi * tq, tq)
    toks = pl.ds(q0, tq)

    qcat = jnp.concatenate([q_ref[hh * HEAD_DIM:(hh + 1) * HEAD_DIM, toks] for hh in range(nh)], axis=1)
    qf = qcat.astype(F32)
    qq = jnp.concatenate([qf, qf], axis=0)
    rowi = lax.broadcasted_iota(jnp.int32, (2 * HEAD_DIM, R), 0)
    coli = lax.broadcasted_iota(jnp.int32, (2 * HEAD_DIM, R), 1)
    half = (coli // tq) % 2 if pair else u % 2
    qa_ref[0:2 * HEAD_DIM, :] = jnp.where(rowi // HEAD_DIM == half, qq, 0.0).astype(BF16)
    if bias_ref is None:
        qa_ref[2 * HEAD_DIM:2 * HEAD_DIM + FEAT, :] = jnp.zeros((FEAT, R), BF16)
    else:
        qa_ref[2 * HEAD_DIM:2 * HEAD_DIM + FEAT, :] = jnp.concatenate(
            [bias_ref[hh if pair else 0, :, toks] for hh in range(nh)], axis=1)
    qa_ref[2 * HEAD_DIM + FEAT:2 * LANES, :] = _alibi_query_rows(q0, tq, u * nh, R)
    if pair:
        kn = jnp.concatenate([kn_ref[hh:hh + 1, :] for hh in range(nh) for _ in range(tq // LANES)], axis=1)
    else:
        kn = jnp.concatenate([jnp.where(u % 2 == 0, kn_ref[0:1, :], kn_ref[1:2, :])] * (R // LANES), axis=1)
    qn = jnp.sqrt(jnp.sum(qf * qf, axis=0, keepdims=True))
    bnd_ref[...] = jnp.broadcast_to(qn * kn, bnd_ref.shape)

    ncg = R // CG
    kd = (q0 + tq - 1) // tk

    def key_rows(kind, c):
        off = (c * CG) % tq
        if kind is None:
            return 0, tk, None
        if kind == "diag":
            return 0, off + CG, off // CG
        return off, tk, (tk + off) // CG

    def sweep(running_max):
        acc_ref[...] = jnp.zeros(acc_ref.shape, F32)
        if running_max:
            m_ref[...] = jnp.full(m_ref.shape, NEG, F32)

        def values(ki, kind=None):
            for c in range(ncg):
                cols = pl.ds(c * CG, CG)
                r0, r1, _ = key_rows(kind, c)
                v = (c * CG) // tq if pair else 0
                pv = _dot(vta_ref[v, ki, :, r0:r1], p_ref[r0:r1, cols])
                if running_max:
                    acc_ref[:, cols] = alpha_ref[0:1, cols] * acc_ref[:, cols] + pv
                else:
                    acc_ref[:, cols] = acc_ref[:, cols] + pv

        def scores(ki, kind=None, prev=None, prev_kind=None):
            k0 = pl.multiple_of(ki * tk, tk)
            s_groups = []
            for c in range(ncg):
                r0, r1, _ = key_rows(kind, c)
                kb = (c * CG) // (2 * tq) if pair else 0
                s_groups.append(_dot(ka_ref[kb, pl.ds(k0 + r0, r1 - r0), :], qa_ref[:, pl.ds(c * CG, CG)]))
            if prev is not None:
                values(prev, prev_kind)
            for c, s in enumerate(s_groups):
                cols = pl.ds(c * CG, CG)
                r0, r1, entry = key_rows(kind, c)
                if entry is not None:
                    s = s + mask_ref[entry, r0:r1, :]
                if running_max:
                    m_prev = m_ref[0:1, cols]
                    m_new = jnp.maximum(m_prev, jnp.max(s, axis=0, keepdims=True))
                    alpha_ref[:, cols] = jnp.broadcast_to(jnp.exp(m_prev - m_new), (alpha_ref.shape[0], CG))
                    m_ref[:, cols] = jnp.broadcast_to(m_new, (m_ref.shape[0], CG))
                    p_ref[r0:r1, cols] = jnp.exp(s - m_new).astype(BF16)
                else:
                    p_ref[r0:r1, cols] = jnp.exp(s - bnd_ref[0:1, cols]).astype(BF16)

        def steady(lo, hi):
            def pair_body(j, carry):
                ki = lo + 2 * j
                scores(ki, prev=ki - 1)
                scores(ki + 1, prev=ki)
                return carry
            lax.fori_loop(0, (hi - lo) // 2, pair_body, 0)

            @pl.when((hi - lo) % 2 == 1)
            def _odd_one():
                scores(hi - 1, prev=hi - 2)

        @pl.when(kd == 0)
        def _only_diagonal():
            scores(0, "diag")
            values(0, "diag")

        @pl.when(kd > 0)
        def _past_then_diagonal():
            if window is None:
                scores(0)
                steady(1, kd)
                scores(kd, "diag", prev=kd - 1)
            else:
                scores(kd - 1, "wedge")
                scores(kd, "diag", prev=kd - 1, prev_kind="wedge")
            values(kd, "diag")

    def emit():
        acc = acc_ref[...]
        out = (acc[0:HEAD_DIM] / acc[HEAD_DIM:HEAD_DIM + 1]).astype(o_ref.dtype)
        if pair:
            for hh in range(nh):
                o_ref[hh, :, toks] = out[:, hh * tq:(hh + 1) * tq]
        else:
            o_ref[0, :, pl.ds(pl.multiple_of(qi * R, R), R)] = out

    sweep(False)
    trusted = jnp.min(acc_ref[HEAD_DIM:HEAD_DIM + 1, :]) >= MIN_WEIGHT_SUM

    @pl.when(trusted)
    def _done():
        emit()

    @pl.when(jnp.logical_not(trusted))
    def _redo_with_running_max():
        sweep(True)
        emit()


def _flash_kernel(*refs, tq, tk, nh, pair, window, has_bias):
    q_ref = refs[0]
    bias_ref = refs[1] if has_bias else None
    (k_ref, kf_ref, vt_ref, mask_ref, o_ref, ka_ref, vta_ref, kn_ref, qa_ref, bnd_ref, m_ref, acc_ref,
     alpha_ref, p_ref) = refs[2 if has_bias else 1:]
    S = k_ref.shape[0]
    for kb in range(ka_ref.shape[0]):
        k = k_ref[:, kb * LANES:(kb + 1) * LANES]
        ka_ref[kb, :, 0:LANES] = k
        ka_ref[kb, :, LANES:2 * LANES] = kf_ref[...]
        k2 = jnp.square(k.astype(F32))
        lane = lax.broadcasted_iota(jnp.int32, k2.shape, 1)
        for hf in range(2):
            n2 = jnp.sum(jnp.where(lane // HEAD_DIM == hf, k2, 0.0), axis=1, keepdims=True)
            kn_ref[2 * kb + hf:2 * kb + hf + 1, :] = jnp.broadcast_to(
                jnp.sqrt(jnp.max(n2, axis=0, keepdims=True)), (1, LANES))
    ones_row = jnp.where(lax.broadcasted_iota(jnp.int32, (VROWS - HEAD_DIM, tk), 0) == 0, 1.0, 0.0)
    for v in range(nh if pair else 1):
        for jt in range(S // tk):
            vta_ref[v, jt, 0:HEAD_DIM, :] = vt_ref[v * HEAD_DIM:(v + 1) * HEAD_DIM, jt * tk:(jt + 1) * tk]
            vta_ref[v, jt, HEAD_DIM:VROWS, :] = ones_row.astype(BF16)

    def q_tile(qi, carry):
        _flash_q_tile(qi, q_ref, bias_ref, mask_ref, o_ref, ka_ref, vta_ref, kn_ref, qa_ref, bnd_ref, m_ref,
                      acc_ref, alpha_ref, p_ref, tq=tq, tk=tk, nh=nh, pair=pair, window=window)
        return carry

    lax.fori_loop(0, S // tq, q_tile, 0)


def _mask_table(tk, window):
    n_delta = (tk + (window or 0)) // CG
    d = np.arange(n_delta)[:, None, None] * CG + np.arange(CG)[None, None, :] - np.arange(tk)[None, :, None]
    ok = (d >= 0) & ((d < window) if window else True)
    return jnp.asarray(np.where(ok, 0.0, NEG), F32)


def _flash(q_t, bias_t, k, kfeat, v_t, *, B, S, units, tq, nh, pair, window):
    tk = TK
    R = nh * tq
    nq = S // tq
    nv = nh if pair else 1
    nb = nh if pair else 1
    nkb = nh // 2 if pair else 1
    assert tq == tk and S % tk == 0 and tk % CG == 0 and window in (None, tk)
    mask = _mask_table(tk, window)
    kcol = (lambda u: u) if pair else (lambda u: u // 2)
    out_w = tq if pair else R
    out_shape = (B, units * nb, HEAD_DIM, S if pair else S * nh)
    has_bias = bias_t is not None
    in_specs = [pl.BlockSpec((nh * HEAD_DIM, S), lambda b, u: (u, b))]
    if has_bias:
        in_specs.append(pl.BlockSpec((None, nb, FEAT, S), lambda b, u: (b, u, 0, 0)))
    in_specs += [pl.BlockSpec((S, nkb * LANES), lambda b, u: (b, kcol(u))),
                 pl.BlockSpec((S, LANES), lambda b, u: (0, 0)),
                 pl.BlockSpec((nv * HEAD_DIM, S), lambda b, u: (u, b)),
                 pl.BlockSpec(mask.shape, lambda b, u: (0, 0, 0))]
    args = (q_t,) + ((bias_t,) if has_bias else ()) + (k, kfeat, v_t, mask)
    return pl.pallas_call(
        functools.partial(_flash_kernel, tq=tq, tk=tk, nh=nh, pair=pair, window=window, has_bias=has_bias),
        grid=(B, units),
        in_specs=in_specs,
        out_specs=pl.BlockSpec((None, nb, HEAD_DIM, out_w * nq), lambda b, u: (b, u, 0, 0)),
        out_shape=jax.ShapeDtypeStruct(out_shape, BF16),
        scratch_shapes=[pltpu.VMEM((nkb, S, 2 * LANES), BF16),
                        pltpu.VMEM((nv, S // tk, VROWS, tk), BF16),
                        pltpu.VMEM((SUBLANES, LANES), F32),
                        pltpu.VMEM((2 * LANES, R), BF16),
                        pltpu.VMEM((SUBLANES, R), F32),
                        pltpu.VMEM((SUBLANES, R), F32),
                        pltpu.VMEM((VROWS, R), F32),
                        pltpu.VMEM((SUBLANES, R), F32),
                        pltpu.VMEM((tk, R), BF16)],
        compiler_params=pltpu.CompilerParams(dimension_semantics=("parallel", "parallel"),
                                             vmem_limit_bytes=VMEM_LIMIT),
        name="flash",
    )(*args)


def _even_out_kernel(x_ref, ocmp_ref, osel_ref, owin_ref, om_ref, gate_ref, nz_ref, mz_ref,
                     w_ref, g_ref, b_ref, o_ref, sig_ref, *, alpha, tm):
    G, nh = NSA_KV_GROUPS, NSA_HG
    sig_ref[...] = _sigmoid(gate_ref[...])
    heads = []
    for g in range(G):
        for hh in range(nh):
            acc = None
            for br, ref in enumerate((ocmp_ref, osel_ref, owin_ref)):
                gate = sig_ref[pl.ds(3 * (g * nh + hh) + br, 1), :]
                term = gate * ref[g, :, hh * tm:(hh + 1) * tm].astype(F32)
                acc = term if acc is None else acc + term
            heads.append(acc)
    o_nsa = jnp.concatenate(heads, axis=0)
    nz = nz_ref[...].astype(F32)
    mz = mz_ref[...].astype(F32)
    o_m = jnp.concatenate([om_ref[h] for h in range(MOBA_HEADS)], axis=0).astype(F32)
    a_t = jnp.concatenate([o_nsa * (nz * _sigmoid(nz)), o_m * (mz * _sigmoid(mz))], axis=0)
    y = _dot_tn(a_t.astype(BF16), w_ref[...])
    o_ref[...] = _layer_norm(alpha * x_ref[...] + y, g_ref[...], b_ref[...])


def _even_out(x2d, ocmp, osel, owin, om, gate_t, nz_t, mz_t, w_out, ln_g, ln_b, alpha, *, B, S):
    T, D = x2d.shape
    tm = TQ
    nq = S // tm
    G, nh, H = NSA_KV_GROUPS, NSA_HG, MOBA_HEADS
    nsa = pl.BlockSpec((None, G, HEAD_DIM, nh * tm), lambda i: (i // nq, 0, 0, i % nq))
    feat = lambda n: pl.BlockSpec((n, tm), lambda i: (0, i))
    full = lambda a: pl.BlockSpec(a.shape, lambda i: (0,) * a.ndim)
    return pl.pallas_call(
        functools.partial(_even_out_kernel, alpha=alpha, tm=tm),
        grid=(T // tm,),
        in_specs=[pl.BlockSpec((tm, D), lambda i: (i, 0)), nsa, nsa, nsa,
                  pl.BlockSpec((None, H, HEAD_DIM, tm), lambda i: (i // nq, 0, 0, i % nq)),
                  feat(GATE_ROWS), feat(NSA_W), feat(MOBA_W),
                  full(w_out), full(ln_g), full(ln_b)],
        out_specs=pl.BlockSpec((tm, D), lambda i: (i, 0)),
        out_shape=jax.ShapeDtypeStruct((T, D), F32),
        scratch_shapes=[pltpu.VMEM((GATE_ROWS, tm), F32)],
        compiler_params=pltpu.CompilerParams(dimension_semantics=("parallel",),
                                             vmem_limit_bytes=VMEM_LIMIT),
        name="even_out",
    )(x2d, ocmp, osel, owin, om, gate_t, nz_t, mz_t, w_out, ln_g, ln_b)


def _gla_head_exact(row0, h, q_ref, v_ref, o_ref, st_ref, b_ref, kf_ref, scale):
    C, dk, dv, SUB = GLA_CHUNK, GLA_DK, GLA_DV, GLA_SUB
    rows = lax.broadcasted_iota(jnp.int32, (C, dk), 0)
    sub_i = lax.broadcasted_iota(jnp.int32, (SUB, LANES), 0)
    lane = lax.broadcasted_iota(jnp.int32, (SUB, LANES), 1)
    ones = jnp.ones((dk, LANES), BF16)
    sl = pl.ds(h * dk, dk)
    rs_c = pl.ds(row0, C)
    bh = b_ref[:, sl]
    kf = kf_ref[:, sl]
    qf = q_ref[rs_c, sl].astype(F32) * scale
    vh = v_ref[rs_c, pl.ds(h * dv, dv)]
    st = st_ref[h]
    inter = _dot_nt((qf * jnp.exp(bh)).astype(BF16), st.astype(BF16))
    bl = b_ref[pl.ds(C - 1, 1), sl]
    kd = kf * jnp.exp(bl - bh)

    def off_block(r0, r1, c0, c1):
        ref_row = b_ref[pl.ds(r0, 1), sl]
        qt = qf[r0:r1] * jnp.exp(bh[r0:r1] - ref_row)
        ek = jnp.where((rows >= c0) & (rows < c1), ref_row - bh, NEG)
        kt = kf * jnp.exp(ek)
        return _dot_nt(qt.astype(BF16), kt.astype(BF16))

    lower = off_block(2 * SUB, C, 0, 2 * SUB)
    blocks = [None, off_block(SUB, 2 * SUB, 0, SUB), lower[0:SUB],
              lower[SUB:2 * SUB] + off_block(3 * SUB, C, 2 * SUB, 3 * SUB)]
    pieces = []
    for blk in range(C // SUB):
        r0 = blk * SUB
        q_i = qf[r0:r0 + SUB]
        b_i = bh[r0:r0 + SUB]
        for jj in range(SUB):
            b_j = b_ref[pl.ds(r0 + jj, 1), sl]
            k_j = kf_ref[pl.ds(r0 + jj, 1), sl]
            pieces.append((q_i * k_j * jnp.exp(jnp.minimum(b_i - b_j, 0.0))).astype(BF16))
    sums = _dot(jnp.concatenate(pieces, axis=0), ones)
    for blk in range(C // SUB):
        r0 = blk * SUB
        diag = jnp.zeros((SUB, LANES), F32)
        for jj in range(SUB):
            rs = sums[(r0 + jj) * SUB:(r0 + jj + 1) * SUB]
            diag = jnp.where((lane == r0 + jj) & (sub_i >= jj), rs, diag)
        diag = diag[:, 0:C]
        blocks[blk] = diag if blocks[blk] is None else blocks[blk] + diag
    att = jnp.concatenate(blocks, axis=0)
    o_ref[rs_c, pl.ds(h * dv, dv)] = (inter + _dot(att.astype(BF16), vh)).astype(o_ref.dtype)
    st_ref[h] = st * jnp.exp(bl) + _dot_tn(vh, kd.astype(BF16))


def _gla_head_bounded(row0, h, q_ref, v_ref, o_ref, st_ref, b_ref, kf_ref, scale):
    C, dk, dv = GLA_CHUNK, GLA_DK, GLA_DV
    sl = pl.ds(h * dk, dk)
    rs_c = pl.ds(row0, C)
    bh = b_ref[rs_c, sl]
    kf = kf_ref[rs_c, sl]
    qf = q_ref[rs_c, sl].astype(F32) * scale
    vh = v_ref[rs_c, pl.ds(h * dv, dv)]
    st = st_ref[h]
    qs = (qf * jnp.exp(bh)).astype(BF16)
    kt = (kf * jnp.exp(-bh)).astype(BF16)
    i = lax.broadcasted_iota(jnp.int32, (C, C), 0)
    j = lax.broadcasted_iota(jnp.int32, (C, C), 1)
    att = jnp.where(i >= j, _dot_nt(qs, kt), 0.0)
    inter = _dot_nt(qs, st.astype(BF16))
    o_ref[rs_c, pl.ds(h * dv, dv)] = (inter + _dot(att.astype(BF16), vh)).astype(o_ref.dtype)
    bl = b_ref[pl.ds(row0 + C - 1, 1), sl]
    kd = kf * jnp.exp(bl - bh)
    st_ref[h] = st * jnp.exp(bl) + _dot_tn(vh, kd.astype(BF16))


def _gla_kernel(q_ref, k_ref, v_ref, glr_ref, w2_ref, gb_ref, tril_ref, o_ref,
                st_ref, b_ref, kf_ref, bc_ref, kc_ref, *, scale):
    C, H = GLA_CHUNK, GLA_HEADS
    nch = q_ref.shape[0] // C

    @pl.when(pl.program_id(1) == 0)
    def _init():
        st_ref[...] = jnp.zeros(st_ref.shape, F32)

    whi, wlo = _hilo(w2_ref[...])
    part = tril_ref.shape[0]
    decay = None
    for r0 in range(0, nch * C, part):
        rs = pl.ds(r0, part)
        ghi, glo = _hilo(glr_ref[rs, :])
        pre = _dot(ghi, whi) + _dot(glo, whi) + _dot(ghi, wlo) + gb_ref[...]
        la = (jnp.minimum(pre, 0.0) - jnp.log(1.0 + jnp.exp(-jnp.abs(pre)))) * (1.0 / GLA_TAU)
        lhi, llo = _hilo(la)
        b_part = _dot(tril_ref[...], lhi) + _dot(tril_ref[...], llo)
        b_ref[rs, :] = b_part
        for c in range(part // C):
            d_c = -b_part[c * C + C - 1:(c + 1) * C, :]
            decay = d_c if decay is None else jnp.maximum(decay, d_c)
    kf_ref[...] = k_ref[...].astype(F32)
    bounded = jnp.max(decay) <= GLA_FACTOR_MAX_DECAY

    @pl.when(bounded)
    def _factored():
        for c in range(nch):
            for h in range(H):
                _gla_head_bounded(c * C, h, q_ref, v_ref, o_ref, st_ref, b_ref, kf_ref, scale)

    @pl.when(jnp.logical_not(bounded))
    def _exact():
        def chunk(c, carry):
            row0 = pl.multiple_of(c * C, C)
            bc_ref[...] = b_ref[pl.ds(row0, C), :]
            kc_ref[...] = kf_ref[pl.ds(row0, C), :]
            for h in range(H):
                _gla_head_exact(row0, h, q_ref, v_ref, o_ref, st_ref, bc_ref, kc_ref, scale)
            return carry
        lax.fori_loop(0, nch, chunk, 0)


def _gla(q, k, v, glr, w2p, gb, tril, *, B, S):
    C, H, dk, dv = GLA_CHUNK, GLA_HEADS, GLA_DK, GLA_DV
    rows = C * min(GLA_STEP_CHUNKS, S // C)
    nchunk = S // rows
    tok = lambda n: pl.BlockSpec((rows, n), lambda b, c: (b * nchunk + c, 0))
    const = lambda shape: pl.BlockSpec(shape, lambda b, c: (0,) * len(shape))
    return pl.pallas_call(
        functools.partial(_gla_kernel, scale=float(dk) ** -0.5),
        grid=(B, nchunk),
        in_specs=[tok(H * dk), tok(H * dk), tok(H * dv), tok(LANES),
                  const(w2p.shape), const(gb.shape), const(tril.shape)],
        out_specs=tok(H * dv),
        out_shape=jax.ShapeDtypeStruct((B * S, H * dv), BF16),
        scratch_shapes=[pltpu.VMEM((H, dv, dk), F32),
                        pltpu.VMEM((rows, H * dk), F32),
                        pltpu.VMEM((rows, H * dk), F32),
                        pltpu.VMEM((C, H * dk), F32),
                        pltpu.VMEM((C, H * dk), F32)],
        compiler_params=pltpu.CompilerParams(dimension_semantics=("parallel", "arbitrary"),
                                             vmem_limit_bytes=VMEM_LIMIT),
        name="gla",
    )(q, k, v, glr, w2p, gb, tril)


def _odd_out_kernel(x_ref, o_ref_in, z_ref, gn_ref, w_ref, g_ref, b_ref, out_ref, *, alpha):
    H, dv = GLA_HEADS, GLA_DV
    parts = []
    for h in range(H):
        sl = pl.ds(h * dv, dv)
        oh = o_ref_in[:, sl].astype(F32)
        r = lax.rsqrt(jnp.mean(oh * oh, axis=-1, keepdims=True) + LN_EPS)
        z = z_ref[:, sl].astype(F32)
        parts.append((oh * r * gn_ref[:, sl] * (z * _sigmoid(z))).astype(BF16))
    y = _dot(jnp.concatenate(parts, axis=1), w_ref[...])
    out_ref[...] = _layer_norm(alpha * x_ref[...] + y, g_ref[...], b_ref[...])


def _odd_out(x2d, o, z, gn_g, w_out, ln_g, ln_b, alpha, tm=1024):
    T, D = x2d.shape
    W = o.shape[1]
    row = lambda n: pl.BlockSpec((tm, n), lambda i: (i, 0))
    return pl.pallas_call(
        functools.partial(_odd_out_kernel, alpha=alpha),
        grid=(T // tm,),
        in_specs=[row(D), row(W), row(W),
                  pl.BlockSpec((1, W), lambda i: (0, 0)),
                  pl.BlockSpec((W, D), lambda i: (0, 0)),
                  pl.BlockSpec((1, D), lambda i: (0, 0)),
                  pl.BlockSpec((1, D), lambda i: (0, 0))],
        out_specs=row(D),
        out_shape=jax.ShapeDtypeStruct((T, D), F32),
        compiler_params=pltpu.CompilerParams(dimension_semantics=("parallel",),
                                             vmem_limit_bytes=VMEM_LIMIT),
        name="odd_out",
    )(x2d, o, z, gn_g, w_out, ln_g, ln_b)


def _key_features(S, block):
    s = np.arange(S)
    f = np.zeros((S, LANES), np.float32)
    f[s, s // block] = 1.0
    f[:, FEAT + 0] = 1.0
    f[:, FEAT + 1] = 1.0
    f[:, FEAT + 2] = FEAT * (s // FEAT)
    f[:, FEAT + 3] = s % FEAT
    return jnp.asarray(f, BF16)


def _pad_to(a, n, axis):
    pad = [(0, 0)] * a.ndim
    pad[axis] = (0, n - a.shape[axis])
    return jnp.pad(a, pad)


def _even_layer(x2d, B, S, w_in, cmp_pos, cmp_w1, cmp_b1, cmp_w2, w_out, ln_g, ln_b, alpha):
    T, D = x2d.shape
    G, hd = NSA_KV_GROUPS, HEAD_DIM
    scale = hd ** -0.5
    wq, wkv, wgate, wnz, wmq, wmk, wmv, wmz = jnp.split(w_in, np.cumsum(EVEN_SPLITS)[:-1].tolist(), axis=1)
    wkc, wvc, wks, wvs, wkw, wvw = jnp.split(wkv, 6, axis=1)
    w_tok = jnp.concatenate([wkc, wvc, wks, wkw, wmk], axis=1).astype(BF16)
    w_tr = jnp.concatenate([wq * scale, wvs, wvw, wmq * scale, wmv, wnz, wmz,
                            _pad_to(wgate, GATE_ROWS, 1)], axis=1).T.astype(BF16)
    (kcmp, vcmp, ksel, kwin, mk, nq_t, vsel_t, vwin_t, mq_t, mv_t, nz_t, mz_t, gate_t) = _proj(
        x2d, w_tok, (NSA_KV_W,) * 4 + (MOBA_W,), (F32, F32, BF16, BF16, BF16),
        w_tr, (NSA_W, NSA_KV_W, NSA_KV_W, MOBA_W, MOBA_W, NSA_W, MOBA_W, GATE_ROWS),
        (BF16,) * 7 + (F32,))

    L, d, HID = NSA_CMP_BLOCK, NSA_CMP_STRIDE, NSA_CMP_HIDDEN
    ncmp = (S - L) // d + 1
    NC = S // d
    eye = jnp.eye(G, dtype=F32)
    w1r = cmp_w1.reshape(2, 2, d, hd, HID)
    w1p = jnp.einsum('ialdj,gh->ialgdhj', w1r, eye).reshape(2, 2, d * G * hd, G * HID).astype(BF16)
    posp = jnp.broadcast_to(cmp_pos.reshape(2, 2, d, 1, hd), (2, 2, d, G, hd)).reshape(2, 2, 1, d * G * hd)
    b1p = jnp.tile(cmp_b1, (1, G)).reshape(2, 1, G * HID)
    w2p = jnp.einsum('ijd,gh->igjhd', _pad_to(cmp_w2, LANES, 2), eye).reshape(2, G * HID, G * LANES).astype(BF16)
    cfeat = np.zeros((NC, G, LANES), np.float32)
    cfeat[:, :, hd + 0] = 1.0
    cfeat[:, :, hd + 1] = 1.0
    cfeat[:, :, hd + 2] = (np.arange(NC) * d)[:, None]
    cfeat[:, :, hd + 3] = L - 1
    kvc = _cmp_mlp(kcmp, vcmp, w1p, posp, b1p, w2p, jnp.asarray(cfeat.reshape(NC, G * LANES)), ncmp,
                   B=B, S=S)

    ns = S // NSA_SEL_BLOCK
    cmp_start = np.arange(NC) * d
    sel_start = np.arange(FEAT) * NSA_SEL_BLOCK
    ovt = ((cmp_start[None, :] <= sel_start[:, None] + NSA_SEL_BLOCK - 1)
           & (cmp_start[None, :] + L - 1 >= sel_start[:, None])
           & (np.arange(NC)[None, :] < ncmp) & (np.arange(FEAT)[:, None] < ns))
    o_cmp, bias_sel = _cmp_attn(nq_t, kvc, jnp.asarray(ovt, BF16), B=B, S=S)

    feats_sel = _key_features(S, NSA_SEL_BLOCK)
    o_sel = _flash(nq_t, bias_sel, ksel, feats_sel, vsel_t,
                   B=B, S=S, units=G, tq=TQ, nh=NSA_HG, pair=False, window=None)
    o_win = _flash(nq_t, None, kwin, feats_sel, vwin_t,
                   B=B, S=S, units=G, tq=TQ, nh=NSA_HG, pair=False, window=NSA_WINDOW)

    avg = np.zeros((MOBA_NBP, S), np.float32)
    avg[np.arange(S) // MOBA_BLOCK, np.arange(S)] = 1.0 / MOBA_BLOCK
    bias_m = _moba_gate(mq_t, mk, jnp.asarray(avg, BF16), B=B, S=S)
    o_m = _flash(mq_t, bias_m, mk, _key_features(S, MOBA_BLOCK), mv_t,
                 B=B, S=S, units=MOBA_HEADS // MOBA_STEP_HEADS, tq=TQM, nh=MOBA_STEP_HEADS, pair=True, window=None)

    return _even_out(x2d, o_cmp, o_sel, o_win, o_m, gate_t, nz_t, mz_t, w_out.astype(BF16),
                     ln_g.reshape(1, D), ln_b.reshape(1, D), alpha, B=B, S=S)


def _odd_layer(x2d, B, S, w_in, gate_w2, gate_b, gn_g, w_out, ln_g, ln_b, alpha):
    T, D = x2d.shape
    H, dk, dv, C = GLA_HEADS, GLA_DK, GLA_DV, GLA_CHUNK
    wq, wk, wv, wg, wz = jnp.split(w_in, np.cumsum(ODD_SPLITS)[:-1].tolist(), axis=1)
    w_perm = jnp.concatenate([wq, wk, wv, wz, _pad_to(wg, LANES, 1)], axis=1).astype(BF16)
    q, k, v, z, glr = _proj(x2d, w_perm, (H * dk, H * dk, H * dv, H * dv, LANES),
                            (BF16, BF16, BF16, BF16, F32))
    w2p = _pad_to(gate_w2, LANES, 0)
    ncs = min(GLA_CUMSUM_CHUNKS, GLA_STEP_CHUNKS, S // C)
    tril = jnp.asarray(np.kron(np.eye(ncs), np.tril(np.ones((C, C)))), BF16)
    o = _gla(q, k, v, glr, w2p, gate_b.reshape(1, H * dk), tril, B=B, S=S)
    return _odd_out(x2d, o, z, gn_g.reshape(1, H * dv), w_out.astype(BF16),
                    ln_g.reshape(1, D), ln_b.reshape(1, D), alpha)


def kernel(x, ev_w_in, ev_cmp_pos, ev_cmp_w1, ev_cmp_b1, ev_cmp_w2, ev_w_out, ev_ln_g, ev_ln_b,
           od_w_in, od_gate_w2, od_gate_b, od_gn_g, od_w_out, od_ln_g, od_ln_b):
    B, S, D = x.shape
    depth = ev_w_in.shape[0] + od_w_in.shape[0]
    alpha = (2.0 * depth) ** 0.25
    h = x.reshape(B * S, D)
    for layer in range(depth):
        i = layer // 2
        if layer % 2 == 0:
            h = _even_layer(h, B, S, ev_w_in[i], ev_cmp_pos[i], ev_cmp_w1[i], ev_cmp_b1[i], ev_cmp_w2[i],
                            ev_w_out[i], ev_ln_g[i], ev_ln_b[i], alpha)
        else:
            h = _odd_layer(h, B, S, od_w_in[i], od_gate_w2[i], od_gate_b[i], od_gn_g[i],
                           od_w_out[i], od_ln_g[i], od_ln_b[i], alpha)
    return h.reshape(B, S, D)
```

```python
import functools

import numpy as np
import jax
import jax.numpy as jnp
from jax import lax
from jax.experimental import pallas as pl
from jax.experimental.pallas import tpu as pltpu

F32 = jnp.float32
BF16 = jnp.bfloat16

HEAD_DIM = 64
NSA_HEADS = 8
NSA_KV_GROUPS = 2
NSA_HG = NSA_HEADS // NSA_KV_GROUPS
NSA_CMP_BLOCK = 32
NSA_CMP_STRIDE = 16
NSA_CMP_HIDDEN = 128
NSA_SEL_BLOCK = 64
NSA_TOP_N = 16
NSA_WINDOW = 512
MOBA_HEADS = 8
MOBA_BLOCK = 256
MOBA_TOP_K = 3
GLA_HEADS = 4
GLA_DK = 128
GLA_DV = 256
GLA_GATE_RANK = 16
GLA_TAU = 16.0
GLA_CHUNK = 64
GLA_SUB = 16
GLA_STEP_CHUNKS = 16
GLA_CUMSUM_CHUNKS = 4
GLA_FACTOR_MAX_DECAY = 60.0
LN_EPS = 1e-5
NEG = -1e30
FORCE_BONUS = 1e4

NSA_W = NSA_HEADS * HEAD_DIM
NSA_KV_W = NSA_KV_GROUPS * HEAD_DIM
MOBA_W = MOBA_HEADS * HEAD_DIM
EVEN_SPLITS = (NSA_W, 6 * NSA_KV_W, 3 * NSA_HEADS, NSA_W, MOBA_W, MOBA_W, MOBA_W, MOBA_W)
ODD_SPLITS = (GLA_HEADS * GLA_DK, GLA_HEADS * GLA_DK, GLA_HEADS * GLA_DV, GLA_GATE_RANK, GLA_HEADS * GLA_DV)

LANES = 128
SUBLANES = 8
FEAT = 64
ALIBI_ROWS = 16
GATE_ROWS = 32
MOBA_NBP = 16
TQ = 512
TQM = 512
MOBA_STEP_HEADS = 4
TK = 512
CG = 256
MIN_WEIGHT_SUM = 1e-25
VROWS = HEAD_DIM + 16
VMEM_LIMIT = 48 * 1024 * 1024


def _dot(a, b):
    return jnp.dot(a, b, preferred_element_type=F32)


def _dot_nt(a, b):
    return lax.dot_general(a, b, (((1,), (1,)), ((), ())), preferred_element_type=F32)


def _dot_tn(a, b):
    return lax.dot_general(a, b, (((0,), (0,)), ((), ())), preferred_element_type=F32)


def _hilo(a):
    hi = a.astype(BF16)
    lo = (a - hi.astype(F32)).astype(BF16)
    return hi, lo


def _pow2_neg(n):
    return lax.bitcast_convert_type(lax.shift_left(127 - n, 23), F32)


def _sigmoid(x):
    return 0.5 * jnp.tanh(0.5 * x) + 0.5


def _layer_norm(r, g, b):
    mu = jnp.mean(r, axis=-1, keepdims=True)
    d = r - mu
    var = jnp.mean(d * d, axis=-1, keepdims=True)
    return d * lax.rsqrt(var + LN_EPS) * g + b


def _alibi_query_rows(t0, tq, head0, R):
    r = lax.broadcasted_iota(jnp.int32, (ALIBI_ROWS, R), 0)
    col = lax.broadcasted_iota(jnp.int32, (ALIBI_ROWS, R), 1)
    t = t0 + col % tq
    slope = _pow2_neg(head0 + col // tq + 1)
    feats = jnp.where(r == 0, -FEAT * slope * (t // FEAT).astype(F32),
                      jnp.where(r == 1, -slope * (t % FEAT).astype(F32), jnp.where(r < 4, slope, 0.0)))
    return jnp.concatenate([feats.astype(BF16), jnp.zeros((FEAT - ALIBI_ROWS, R), BF16)], axis=0)


def _ranks(sc_ref, n_real, n_rows=None):
    width = sc_ref.shape[1]
    n_rows = sc_ref.shape[0] if n_rows is None else n_rows
    out = []
    for v in range(n_rows // SUBLANES):
        lo = v * SUBLANES
        sc = sc_ref[lo:lo + SUBLANES, :]
        idx = lo + lax.broadcasted_iota(jnp.int32, (SUBLANES, width), 0)
        rank = jnp.zeros((SUBLANES, width), jnp.int32)
        for jp in range(n_real):
            r = sc_ref[pl.ds(jp, 1), :]
            if jp < lo:
                beats = r >= sc
            elif jp >= lo + SUBLANES - 1:
                beats = r > sc
            else:
                beats = (r > sc) | ((r == sc) & (idx > jp))
            rank = rank + jnp.where(beats, 1, 0)
        out.append(rank)
    return jnp.concatenate(out, axis=0)


def _proj_kernel(*refs, tok_splits, tr_splits):
    x_ref, wt_ref = refs[0], refs[1]
    outs = list(refs[3:] if tr_splits else refs[2:])
    x = x_ref[...]
    xb = x.astype(BF16)
    off = 0
    for n in tok_splits:
        o_ref = outs.pop(0)
        o_ref[...] = _dot(xb, wt_ref[:, off:off + n]).astype(o_ref.dtype)
        off += n
    if tr_splits:
        wtr_ref = refs[2]
        xt = x.T.astype(BF16)
        off = 0
        for n in tr_splits:
            o_ref = outs.pop(0)
            o_ref[...] = _dot(wtr_ref[off:off + n, :], xt).astype(o_ref.dtype)
            off += n


def _proj(x2d, w_tok, tok_splits, tok_dtypes, w_tr=None, tr_splits=(), tr_dtypes=(), tm=512):
    T, D = x2d.shape
    assert T % tm == 0 and sum(tok_splits) == w_tok.shape[1]
    in_specs = [pl.BlockSpec((tm, D), lambda i: (i, 0)),
                pl.BlockSpec(w_tok.shape, lambda i: (0, 0))]
    args = [x2d, w_tok]
    if tr_splits:
        assert sum(tr_splits) == w_tr.shape[0]
        in_specs.append(pl.BlockSpec(w_tr.shape, lambda i: (0, 0)))
        args.append(w_tr)
    out_specs = ([pl.BlockSpec((tm, n), lambda i: (i, 0)) for n in tok_splits]
                 + [pl.BlockSpec((n, tm), lambda i: (0, i)) for n in tr_splits])
    out_shape = ([jax.ShapeDtypeStruct((T, n), dt) for n, dt in zip(tok_splits, tok_dtypes)]
                 + [jax.ShapeDtypeStruct((n, T), dt) for n, dt in zip(tr_splits, tr_dtypes)])
    return pl.pallas_call(
        functools.partial(_proj_kernel, tok_splits=tuple(tok_splits), tr_splits=tuple(tr_splits)),
        grid=(T // tm,), in_specs=in_specs, out_specs=out_specs, out_shape=out_shape,
        compiler_params=pltpu.CompilerParams(dimension_semantics=("parallel",),
                                             vmem_limit_bytes=VMEM_LIMIT),
        name="proj",
    )(*args)


def _cmp_mlp_kernel(kx_ref, vx_ref, w1_ref, pos_ref, b1_ref, w2_ref, cf_ref, o_ref, *, ncmp):
    d = NSA_CMP_STRIDE
    nc = kx_ref.shape[0] // d
    cw = w1_ref.shape[2]
    for i, x_ref in enumerate((kx_ref, vx_ref)):
        a = bm = None
        for l in range(0, d, 2):
            x2 = jnp.concatenate([x_ref[pl.ds(l, nc, stride=d), :], x_ref[pl.ds(l + 1, nc, stride=d), :]],
                                 axis=1).astype(BF16)
            rows = pl.ds(l * x_ref.shape[1], 2 * x_ref.shape[1])
            a_l, b_l = _dot(x2, w1_ref[i, 0, rows, :]), _dot(x2, w1_ref[i, 1, rows, :])
            a, bm = (a_l, b_l) if a is None else (a + a_l, bm + b_l)
        c = b1_ref[i]
        for half in range(2):
            phi, plo = _hilo(jnp.broadcast_to(pos_ref[i, half], (SUBLANES, cw)))
            c = c + (_dot(phi, w1_ref[i, half]) + _dot(plo, w1_ref[i, half]))[0:1]
        hid = a + pltpu.roll(bm, nc - 1, axis=0) + c
        hid = hid * _sigmoid(hid)
        row = lax.broadcasted_iota(jnp.int32, hid.shape, 0)
        hid = jnp.where(row < ncmp, hid, 0.0)
        out = _dot(hid.astype(BF16), w2_ref[i])
        if i == 0:
            out = out + cf_ref[...]
        o_ref[i] = out.astype(o_ref.dtype)


def _cmp_mlp(kx, vx, w1p, posp, b1p, w2p, cfeat, ncmp, *, B, S):
    NC = S // NSA_CMP_STRIDE
    OW = w2p.shape[-1]
    full = lambda a: pl.BlockSpec(a.shape, lambda b: (0,) * a.ndim)
    return pl.pallas_call(
        functools.partial(_cmp_mlp_kernel, ncmp=ncmp),
        grid=(B,),
        in_specs=[pl.BlockSpec((S, kx.shape[1]), lambda b: (b, 0)),
                  pl.BlockSpec((S, vx.shape[1]), lambda b: (b, 0)),
                  full(w1p), full(posp), full(b1p), full(w2p), full(cfeat)],
        out_specs=pl.BlockSpec((2, None, NC, OW), lambda b: (0, b, 0, 0)),
        out_shape=jax.ShapeDtypeStruct((2, B, NC, OW), BF16),
        compiler_params=pltpu.CompilerParams(dimension_semantics=("parallel",),
                                             vmem_limit_bytes=VMEM_LIMIT),
        name="cmp_mlp",
    )(kx, vx, w1p, posp, b1p, w2p, cfeat)


def _cmp_attn_kernel(q_ref, kc_ref, vc_ref, ovt_ref, o_ref, bias_ref, sc_ref, *, tq, nh, ns, ntop):
    g = pl.program_id(1)
    R = nh * tq
    nq = q_ref.shape[1] // tq

    def q_tile(qi):
        toks = pl.ds(qi * tq, tq)
        nc = min(kc_ref.shape[0], -(-((qi + 1) * tq) // (NSA_CMP_STRIDE * 16)) * 16)
        nsel = min(ns, -(-((qi + 1) * tq) // (NSA_SEL_BLOCK * SUBLANES)) * SUBLANES)
        qcat = jnp.concatenate([q_ref[hh * HEAD_DIM:(hh + 1) * HEAD_DIM, toks] for hh in range(nh)], axis=1)
        qa = jnp.concatenate([qcat, _alibi_query_rows(qi * tq, tq, g * nh, R)], axis=0)
        s = _dot(kc_ref[0:nc, :], qa)
        n = lax.broadcasted_iota(jnp.int32, (nc, R), 0)
        t_row = qi * tq + lax.broadcasted_iota(jnp.int32, (1, R), 1) % tq
        last = (t_row - (NSA_CMP_BLOCK - 1)) // NSA_CMP_STRIDE
        s = jnp.where(n <= last, s, NEG)
        e = jnp.exp(s - jnp.max(s, axis=0, keepdims=True))
        inv = jnp.where(last >= 0, 1.0 / jnp.sum(e, axis=0, keepdims=True), 0.0)
        p = e * inv
        o_ref[:, pl.ds(qi * R, R)] = _dot_tn(vc_ref[0:nc, 0:HEAD_DIM], p.astype(BF16)).astype(o_ref.dtype)

        psum = p[:, 0:tq]
        for hh in range(1, nh):
            psum = psum + p[:, hh * tq:(hh + 1) * tq]
        phi, plo = _hilo(psum)
        ovt = ovt_ref[0:nsel, 0:nc]
        imp = _dot(ovt, phi) + _dot(ovt, plo)

        j = lax.broadcasted_iota(jnp.int32, (nsel, tq), 0)
        cur = (qi * tq + lax.broadcasted_iota(jnp.int32, (nsel, tq), 1)) // NSA_SEL_BLOCK
        if nsel <= ntop:
            sel = j <= cur
        else:
            forced = jnp.where((j == 0) | (j == cur) | (j == cur - 1), FORCE_BONUS, 0.0)
            sc_ref[0:nsel, :] = jnp.where(j <= cur, imp + forced, NEG)
            sel = (_ranks(sc_ref, nsel, nsel) < ntop) & (j <= cur)
        bias_ref[0:nsel, toks] = jnp.where(sel, 0.0, NEG).astype(bias_ref.dtype)
        if nsel < FEAT:
            bias_ref[nsel:FEAT, toks] = jnp.full((FEAT - nsel, tq), NEG, bias_ref.dtype)

    for qi in range(nq):
        q_tile(qi)


def _cmp_attn(nq_t, kvc, ovt, *, B, S):
    tq, nh, G = TQ, NSA_HG, NSA_KV_GROUPS
    R = nh * tq
    nq = S // tq
    NC = kvc.shape[2]
    ns = S // NSA_SEL_BLOCK
    assert ns <= FEAT
    ntop = min(NSA_TOP_N, ns)
    return pl.pallas_call(
        functools.partial(_cmp_attn_kernel, tq=tq, nh=nh, ns=ns, ntop=ntop),
        grid=(B, G),
        in_specs=[pl.BlockSpec((nh * HEAD_DIM, S), lambda b, g: (g, b)),
                  pl.BlockSpec((None, None, NC, LANES), lambda b, g: (0, b, 0, g)),
                  pl.BlockSpec((None, None, NC, LANES), lambda b, g: (1, b, 0, g)),
                  pl.BlockSpec(ovt.shape, lambda b, g: (0, 0))],
        out_specs=[pl.BlockSpec((None, None, HEAD_DIM, S * nh), lambda b, g: (b, g, 0, 0)),
                   pl.BlockSpec((None, None, FEAT, S), lambda b, g: (b, g, 0, 0))],
        out_shape=[jax.ShapeDtypeStruct((B, G, HEAD_DIM, S * nh), BF16),
                   jax.ShapeDtypeStruct((B, G, FEAT, S), BF16)],
        scratch_shapes=[pltpu.VMEM((FEAT, tq), F32)],
        compiler_params=pltpu.CompilerParams(dimension_semantics=("parallel", "parallel"),
                                             vmem_limit_bytes=VMEM_LIMIT),
        name="cmp_attn",
    )(nq_t, kvc, kvc, ovt)


def _moba_gate_kernel(q_ref, k_ref, a_ref, bias_ref, km_ref, sc_ref, *, tq, nb, ntop):
    h = pl.program_id(1)
    qi = pl.program_id(2)

    @pl.when(qi == 0)
    def _means():
        km_ref[...] = _dot(a_ref[...], k_ref[...])

    nbp = km_ref.shape[0]
    lane = lax.broadcasted_iota(jnp.int32, (nbp, LANES), 1)
    km = jnp.where(lane // HEAD_DIM == h % 2, km_ref[...], 0.0)
    khi, klo = _hilo(km)
    q = q_ref[...]
    qq = jnp.concatenate([q, q], axis=0)
    g_t = _dot(khi, qq) + _dot(klo, qq)
    n = lax.broadcasted_iota(jnp.int32, (nbp, tq), 0)
    cur = (qi * tq + lax.broadcasted_iota(jnp.int32, (nbp, tq), 1)) // MOBA_BLOCK
    sc = jnp.where(n < cur, g_t, NEG)
    sc_ref[...] = sc
    sel = ((_ranks(sc_ref, nb) < ntop) & (n < cur)) | (n == cur)
    bias_ref[0:nbp, :] = jnp.where(sel, 0.0, NEG).astype(bias_ref.dtype)
    bias_ref[nbp:FEAT, :] = jnp.zeros((FEAT - nbp, tq), bias_ref.dtype)


def _moba_gate(mq_t, mk, avg, *, B, S):
    tq, H = S, MOBA_HEADS
    nq = S // tq
    nb = S // MOBA_BLOCK
    nbp = avg.shape[0]
    assert nb <= nbp
    ntop = min(MOBA_TOP_K, nb)
    return pl.pallas_call(
        functools.partial(_moba_gate_kernel, tq=tq, nb=nb, ntop=ntop),
        grid=(B, H, nq),
        in_specs=[pl.BlockSpec((HEAD_DIM, tq), lambda b, h, i: (h, b * nq + i)),
                  pl.BlockSpec((S, LANES), lambda b, h, i: (b, h // 2)),
                  pl.BlockSpec(avg.shape, lambda b, h, i: (0, 0))],
        out_specs=pl.BlockSpec((None, None, FEAT, tq), lambda b, h, i: (b, h, 0, i)),
        out_shape=jax.ShapeDtypeStruct((B, H, FEAT, S), BF16),
        scratch_shapes=[pltpu.VMEM((nbp, LANES), F32), pltpu.VMEM((nbp, tq), F32)],
        compiler_params=pltpu.CompilerParams(dimension_semantics=("parallel", "parallel", "arbitrary"),
                                             vmem_limit_bytes=VMEM_LIMIT),
        name="moba_gate",
    )(mq_t, mk, avg)


def _flash_q_tile(qi, running_max, q_ref, bias_ref, mask_ref, o_ref, ka_ref, vta_ref, kn_ref, qa_ref, bnd_ref,
                  m_ref, acc_ref, alpha_ref, p_ref, *, tq, tk, nh, pair, window):
    u = pl.program_id(1)
    R = nh * tq
    static = isinstance(qi, int)
    q0 = qi * tq if static else pl.multiple_of(qi * tq, tq)
    toks = pl.ds(q0, tq)

    qcat = jnp.concatenate([q_ref[hh * HEAD_DIM:(hh + 1) * HEAD_DIM, toks] for hh in range(nh)], axis=1)
    qf = qcat.astype(F32)
    qq = jnp.concatenate([qf, qf], axis=0)
    rowi = lax.broadcasted_iota(jnp.int32, (2 * HEAD_DIM, R), 0)
    coli = lax.broadcasted_iota(jnp.int32, (2 * HEAD_DIM, R), 1)
    half = (coli // tq) % 2 if pair else u % 2
    qa_ref[0:2 * HEAD_DIM, :] = jnp.where(rowi // HEAD_DIM == half, qq, 0.0).astype(BF16)
    if bias_ref is None:
        qa_ref[2 * HEAD_DIM:2 * HEAD_DIM + FEAT, :] = jnp.zeros((FEAT, R), BF16)
    else:
        qa_ref[2 * HEAD_DIM:2 * HEAD_DIM + FEAT, :] = jnp.concatenate(
            [bias_ref[hh if pair else 0, :, toks] for hh in range(nh)], axis=1)
    qa_ref[2 * HEAD_DIM + FEAT:2 * LANES, :] = _alibi_query_rows(q0, tq, u * nh, R)
    if pair:
        kn = jnp.concatenate([kn_ref[hh:hh + 1, :] for hh in range(nh) for _ in range(tq // LANES)], axis=1)
    else:
        kn = jnp.concatenate([jnp.where(u % 2 == 0, kn_ref[0:1, :], kn_ref[1:2, :])] * (R // LANES), axis=1)
    qn = jnp.sqrt(jnp.sum(qf * qf, axis=0, keepdims=True))
    bnd_ref[...] = jnp.broadcast_to(qn * kn, bnd_ref.shape)

    ncg = R // CG
    kd = (q0 + tq - 1) // tk

    def key_rows(kind, c):
        off = (c * CG) % tq
        if kind is None:
            return 0, tk, None
        if kind == "diag":
            return 0, off + CG, off // CG
        return off, tk, (tk + off) // CG

    def values(ki, kind=None):
        for c in range(ncg):
            cols = pl.ds(c * CG, CG)
            r0, r1, _ = key_rows(kind, c)
            v = (c * CG) // tq if pair else 0
            pv = _dot(vta_ref[v, ki, :, r0:r1], p_ref[r0:r1, cols])
            if running_max:
                acc_ref[:, cols] = alpha_ref[0:1, cols] * acc_ref[:, cols] + pv
            else:
                acc_ref[:, cols] = acc_ref[:, cols] + pv

    def scores(ki, kind=None, prev=None, prev_kind=None):
        k0 = ki * tk if isinstance(ki, int) else pl.multiple_of(ki * tk, tk)
        s_groups = []
        for c in range(ncg):
            r0, r1, _ = key_rows(kind, c)
            kb = (c * CG) // (2 * tq) if pair else 0
            s_groups.append(_dot(ka_ref[kb, pl.ds(k0 + r0, r1 - r0), :], qa_ref[:, pl.ds(c * CG, CG)]))
        if prev is not None:
            values(prev, prev_kind)
        for c, s in enumerate(s_groups):
            cols = pl.ds(c * CG, CG)
            r0, r1, entry = key_rows(kind, c)
            if entry is not None:
                s = s + mask_ref[entry, r0:r1, :]
            if running_max:
                m_prev = m_ref[0:1, cols]
                m_new = jnp.maximum(m_prev, jnp.max(s, axis=0, keepdims=True))
                alpha_ref[:, cols] = jnp.broadcast_to(jnp.exp(m_prev - m_new), (alpha_ref.shape[0], CG))
                m_ref[:, cols] = jnp.broadcast_to(m_new, (m_ref.shape[0], CG))
                p_ref[r0:r1, cols] = jnp.exp(s - m_new).astype(BF16)
            else:
                p_ref[r0:r1, cols] = jnp.exp(s - bnd_ref[0:1, cols]).astype(BF16)

    def steady(lo, hi):
        def pair_body(j, carry):
            ki = lo + 2 * j
            scores(ki, prev=ki - 1)
            scores(ki + 1, prev=ki)
            return carry
        lax.fori_loop(0, (hi - lo) // 2, pair_body, 0)

        @pl.when((hi - lo) % 2 == 1)
        def _odd_one():
            scores(hi - 1, prev=hi - 2)

    def only_diagonal():
        scores(0, "diag")
        values(0, "diag")

    def past_then_diagonal():
        if window is not None:
            scores(kd - 1, "wedge")
            scores(kd, "diag", prev=kd - 1, prev_kind="wedge")
        elif static:
            scores(0)
            for ki in range(1, kd):
                scores(ki, prev=ki - 1)
            scores(kd, "diag", prev=kd - 1)
        else:
            scores(0)
            steady(1, kd)
            scores(kd, "diag", prev=kd - 1)
        values(kd, "diag")

    acc_ref[...] = jnp.zeros(acc_ref.shape, F32)
    if running_max:
        m_ref[...] = jnp.full(m_ref.shape, NEG, F32)
    if static:
        only_diagonal() if kd == 0 else past_then_diagonal()
    else:
        pl.when(kd == 0)(only_diagonal)
        pl.when(kd > 0)(past_then_diagonal)

    acc = acc_ref[...]
    out = (acc[0:HEAD_DIM] / acc[HEAD_DIM:HEAD_DIM + 1]).astype(o_ref.dtype)
    if pair:
        for hh in range(nh):
            o_ref[hh, :, toks] = out[:, hh * tq:(hh + 1) * tq]
    else:
        o_ref[0, :, pl.ds(qi * R if static else pl.multiple_of(qi * R, R), R)] = out
    return jnp.min(acc[HEAD_DIM:HEAD_DIM + 1, :])


def _flash_kernel(*refs, tq, tk, nh, pair, window, has_bias):
    q_ref = refs[0]
    bias_ref = refs[1] if has_bias else None
    (k_ref, kf_ref, vt_ref, mask_ref, o_ref, ka_ref, vta_ref, kn_ref, qa_ref, bnd_ref, m_ref, acc_ref,
     alpha_ref, p_ref, low_ref) = refs[2 if has_bias else 1:]
    S = k_ref.shape[0]
    for kb in range(ka_ref.shape[0]):
        k = k_ref[:, kb * LANES:(kb + 1) * LANES]
        ka_ref[kb, :, 0:LANES] = k
        ka_ref[kb, :, LANES:2 * LANES] = kf_ref[...]
        k2 = jnp.square(k.astype(F32))
        lane = lax.broadcasted_iota(jnp.int32, k2.shape, 1)
        for hf in range(2):
            n2 = jnp.sum(jnp.where(lane // HEAD_DIM == hf, k2, 0.0), axis=1, keepdims=True)
            kn_ref[2 * kb + hf:2 * kb + hf + 1, :] = jnp.broadcast_to(
                jnp.sqrt(jnp.max(n2, axis=0, keepdims=True)), (1, LANES))
    ones_row = jnp.where(lax.broadcasted_iota(jnp.int32, (VROWS - HEAD_DIM, tk), 0) == 0, 1.0, 0.0)
    for v in range(nh if pair else 1):
        for jt in range(S // tk):
            vta_ref[v, jt, 0:HEAD_DIM, :] = vt_ref[v * HEAD_DIM:(v + 1) * HEAD_DIM, jt * tk:(jt + 1) * tk]
            vta_ref[v, jt, HEAD_DIM:VROWS, :] = ones_row.astype(BF16)

    tile = functools.partial(_flash_q_tile, q_ref=q_ref, bias_ref=bias_ref, mask_ref=mask_ref, o_ref=o_ref,
                             ka_ref=ka_ref, vta_ref=vta_ref, kn_ref=kn_ref, qa_ref=qa_ref, bnd_ref=bnd_ref,
                             m_ref=m_ref, acc_ref=acc_ref, alpha_ref=alpha_ref, p_ref=p_ref,
                             tq=tq, tk=tk, nh=nh, pair=pair, window=window)
    nq = S // tq
    for qi in range(nq):
        low_ref[qi] = tile(qi, False)

    def redo(qi, carry):
        @pl.when(low_ref[qi] < MIN_WEIGHT_SUM)
        def _running_max():
            tile(qi, True)
        return carry

    lax.fori_loop(0, nq, redo, 0)


def _mask_table(tk, window):
    n_delta = (tk + (window or 0)) // CG
    d = np.arange(n_delta)[:, None, None] * CG + np.arange(CG)[None, None, :] - np.arange(tk)[None, :, None]
    ok = (d >= 0) & ((d < window) if window else True)
    return jnp.asarray(np.where(ok, 0.0, NEG), F32)


def _flash(q_t, bias_t, k, kfeat, v_t, *, B, S, units, tq, nh, pair, window):
    tk = TK
    R = nh * tq
    nq = S // tq
    nv = nh if pair else 1
    nb = nh if pair else 1
    nkb = nh // 2 if pair else 1
    assert tq == tk and S % tk == 0 and tk % CG == 0 and window in (None, tk)
    mask = _mask_table(tk, window)
    kcol = (lambda u: u) if pair else (lambda u: u // 2)
    out_w = tq if pair else R
    out_shape = (B, units * nb, HEAD_DIM, S if pair else S * nh)
    has_bias = bias_t is not None
    in_specs = [pl.BlockSpec((nh * HEAD_DIM, S), lambda b, u: (u, b))]
    if has_bias:
        in_specs.append(pl.BlockSpec((None, nb, FEAT, S), lambda b, u: (b, u, 0, 0)))
    in_specs += [pl.BlockSpec((S, nkb * LANES), lambda b, u: (b, kcol(u))),
                 pl.BlockSpec((S, LANES), lambda b, u: (0, 0)),
                 pl.BlockSpec((nv * HEAD_DIM, S), lambda b, u: (u, b)),
                 pl.BlockSpec(mask.shape, lambda b, u: (0, 0, 0))]
    args = (q_t,) + ((bias_t,) if has_bias else ()) + (k, kfeat, v_t, mask)
    return pl.pallas_call(
        functools.partial(_flash_kernel, tq=tq, tk=tk, nh=nh, pair=pair, window=window, has_bias=has_bias),
        grid=(B, units),
        in_specs=in_specs,
        out_specs=pl.BlockSpec((None, nb, HEAD_DIM, out_w * nq), lambda b, u: (b, u, 0, 0)),
        out_shape=jax.ShapeDtypeStruct(out_shape, BF16),
        scratch_shapes=[pltpu.VMEM((nkb, S, 2 * LANES), BF16),
                        pltpu.VMEM((nv, S // tk, VROWS, tk), BF16),
                        pltpu.VMEM((SUBLANES, LANES), F32),
                        pltpu.VMEM((2 * LANES, R), BF16),
                        pltpu.VMEM((SUBLANES, R), F32),
                        pltpu.VMEM((SUBLANES, R), F32),
                        pltpu.VMEM((VROWS, R), F32),
                        pltpu.VMEM((SUBLANES, R), F32),
                        pltpu.VMEM((tk, R), BF16),
                        pltpu.SMEM((nq,), F32)],
        compiler_params=pltpu.CompilerParams(dimension_semantics=("parallel", "parallel"),
                                             vmem_limit_bytes=VMEM_LIMIT),
        name="flash",
    )(*args)


def _even_out_kernel(x_ref, ocmp_ref, osel_ref, owin_ref, om_ref, gate_ref, nz_ref, mz_ref,
                     w_ref, g_ref, b_ref, o_ref, sig_ref, *, alpha, tm):
    G, nh = NSA_KV_GROUPS, NSA_HG
    sig_ref[...] = _sigmoid(gate_ref[...])
    heads = []
    for g in range(G):
        for hh in range(nh):
            acc = None
            for br, ref in enumerate((ocmp_ref, osel_ref, owin_ref)):
                gate = sig_ref[pl.ds(3 * (g * nh + hh) + br, 1), :]
                term = gate * ref[g, :, hh * tm:(hh + 1) * tm].astype(F32)
                acc = term if acc is None else acc + term
            heads.append(acc)
    o_nsa = jnp.concatenate(heads, axis=0)
    nz = nz_ref[...].astype(F32)
    mz = mz_ref[...].astype(F32)
    o_m = jnp.concatenate([om_ref[h] for h in range(MOBA_HEADS)], axis=0).astype(F32)
    a_t = jnp.concatenate([o_nsa * (nz * _sigmoid(nz)), o_m * (mz * _sigmoid(mz))], axis=0)
    y = _dot_tn(a_t.astype(BF16), w_ref[...])
    o_ref[...] = _layer_norm(alpha * x_ref[...] + y, g_ref[...], b_ref[...])


def _even_out(x2d, ocmp, osel, owin, om, gate_t, nz_t, mz_t, w_out, ln_g, ln_b, alpha, *, B, S):
    T, D = x2d.shape
    tm = TQ
    nq = S // tm
    G, nh, H = NSA_KV_GROUPS, NSA_HG, MOBA_HEADS
    nsa = pl.BlockSpec((None, G, HEAD_DIM, nh * tm), lambda i: (i // nq, 0, 0, i % nq))
    feat = lambda n: pl.BlockSpec((n, tm), lambda i: (0, i))
    full = lambda a: pl.BlockSpec(a.shape, lambda i: (0,) * a.ndim)
    return pl.pallas_call(
        functools.partial(_even_out_kernel, alpha=alpha, tm=tm),
        grid=(T // tm,),
        in_specs=[pl.BlockSpec((tm, D), lambda i: (i, 0)), nsa, nsa, nsa,
                  pl.BlockSpec((None, H, HEAD_DIM, tm), lambda i: (i // nq, 0, 0, i % nq)),
                  feat(GATE_ROWS), feat(NSA_W), feat(MOBA_W),
                  full(w_out), full(ln_g), full(ln_b)],
        out_specs=pl.BlockSpec((tm, D), lambda i: (i, 0)),
        out_shape=jax.ShapeDtypeStruct((T, D), F32),
        scratch_shapes=[pltpu.VMEM((GATE_ROWS, tm), F32)],
        compiler_params=pltpu.CompilerParams(dimension_semantics=("parallel",),
                                             vmem_limit_bytes=VMEM_LIMIT),
        name="even_out",
    )(x2d, ocmp, osel, owin, om, gate_t, nz_t, mz_t, w_out, ln_g, ln_b)


def _gla_head_exact(row0, h, q_ref, v_ref, o_ref, st_ref, b_ref, kf_ref, scale):
    C, dk, dv, SUB = GLA_CHUNK, GLA_DK, GLA_DV, GLA_SUB
    rows = lax.broadcasted_iota(jnp.int32, (C, dk), 0)
    sub_i = lax.broadcasted_iota(jnp.int32, (SUB, LANES), 0)
    lane = lax.broadcasted_iota(jnp.int32, (SUB, LANES), 1)
    ones = jnp.ones((dk, LANES), BF16)
    sl = pl.ds(h * dk, dk)
    rs_c = pl.ds(row0, C)
    bh = b_ref[:, sl]
    kf = kf_ref[:, sl]
    qf = q_ref[rs_c, sl].astype(F32) * scale
    vh = v_ref[rs_c, pl.ds(h * dv, dv)]
    st = st_ref[h]
    inter = _dot_nt((qf * jnp.exp(bh)).astype(BF16), st.astype(BF16))
    bl = b_ref[pl.ds(C - 1, 1), sl]
    kd = kf * jnp.exp(bl - bh)

    def off_block(r0, r1, c0, c1):
        ref_row = b_ref[pl.ds(r0, 1), sl]
        qt = qf[r0:r1] * jnp.exp(bh[r0:r1] - ref_row)
        ek = jnp.where((rows >= c0) & (rows < c1), ref_row - bh, NEG)
        kt = kf * jnp.exp(ek)
        return _dot_nt(qt.astype(BF16), kt.astype(BF16))

    lower = off_block(2 * SUB, C, 0, 2 * SUB)
    blocks = [None, off_block(SUB, 2 * SUB, 0, SUB), lower[0:SUB],
              lower[SUB:2 * SUB] + off_block(3 * SUB, C, 2 * SUB, 3 * SUB)]
    pieces = []
    for blk in range(C // SUB):
        r0 = blk * SUB
        q_i = qf[r0:r0 + SUB]
        b_i = bh[r0:r0 + SUB]
        for jj in range(SUB):
            b_j = b_ref[pl.ds(r0 + jj, 1), sl]
            k_j = kf_ref[pl.ds(r0 + jj, 1), sl]
            pieces.append((q_i * k_j * jnp.exp(jnp.minimum(b_i - b_j, 0.0))).astype(BF16))
    sums = _dot(jnp.concatenate(pieces, axis=0), ones)
    for blk in range(C // SUB):
        r0 = blk * SUB
        diag = jnp.zeros((SUB, LANES), F32)
        for jj in range(SUB):
            rs = sums[(r0 + jj) * SUB:(r0 + jj + 1) * SUB]
            diag = jnp.where((lane == r0 + jj) & (sub_i >= jj), rs, diag)
        diag = diag[:, 0:C]
        blocks[blk] = diag if blocks[blk] is None else blocks[blk] + diag
    att = jnp.concatenate(blocks, axis=0)
    o_ref[rs_c, pl.ds(h * dv, dv)] = (inter + _dot(att.astype(BF16), vh)).astype(o_ref.dtype)
    st_ref[h] = st * jnp.exp(bl) + _dot_tn(vh, kd.astype(BF16))


def _gla_head_bounded(row0, h, q_ref, v_ref, o_ref, st_ref, b_ref, kf_ref, scale):
    C, dk, dv = GLA_CHUNK, GLA_DK, GLA_DV
    sl = pl.ds(h * dk, dk)
    rs_c = pl.ds(row0, C)
    bh = b_ref[rs_c, sl]
    kf = kf_ref[rs_c, sl]
    qf = q_ref[rs_c, sl].astype(F32) * scale
    vh = v_ref[rs_c, pl.ds(h * dv, dv)]
    st = st_ref[h]
    qs = (qf * jnp.exp(bh)).astype(BF16)
    kt = (kf * jnp.exp(-bh)).astype(BF16)
    i = lax.broadcasted_iota(jnp.int32, (C, C), 0)
    j = lax.broadcasted_iota(jnp.int32, (C, C), 1)
    att = jnp.where(i >= j, _dot_nt(qs, kt), 0.0)
    inter = _dot_nt(qs, st.astype(BF16))
    o_ref[rs_c, pl.ds(h * dv, dv)] = (inter + _dot(att.astype(BF16), vh)).astype(o_ref.dtype)
    bl = b_ref[pl.ds(row0 + C - 1, 1), sl]
    kd = kf * jnp.exp(bl - bh)
    st_ref[h] = st * jnp.exp(bl) + _dot_tn(vh, kd.astype(BF16))


def _gla_kernel(q_ref, k_ref, v_ref, glr_ref, w2_ref, gb_ref, tril_ref, o_ref,
                st_ref, b_ref, kf_ref, bc_ref, kc_ref, *, scale):
    C, H = GLA_CHUNK, GLA_HEADS
    nch = q_ref.shape[0] // C

    @pl.when(pl.program_id(1) == 0)
    def _init():
        st_ref[...] = jnp.zeros(st_ref.shape, F32)

    whi, wlo = _hilo(w2_ref[...])
    part = tril_ref.shape[0]
    decay = None
    for r0 in range(0, nch * C, part):
        rs = pl.ds(r0, part)
        ghi, glo = _hilo(glr_ref[rs, :])
        pre = _dot(ghi, whi) + _dot(glo, whi) + _dot(ghi, wlo) + gb_ref[...]
        la = (jnp.minimum(pre, 0.0) - jnp.log(1.0 + jnp.exp(-jnp.abs(pre)))) * (1.0 / GLA_TAU)
        lhi, llo = _hilo(la)
        b_part = _dot(tril_ref[...], lhi) + _dot(tril_ref[...], llo)
        b_ref[rs, :] = b_part
        for c in range(part // C):
            d_c = -b_part[c * C + C - 1:(c + 1) * C, :]
            decay = d_c if decay is None else jnp.maximum(decay, d_c)
    kf_ref[...] = k_ref[...].astype(F32)
    bounded = jnp.max(decay) <= GLA_FACTOR_MAX_DECAY

    @pl.when(bounded)
    def _factored():
        for c in range(nch):
            for h in range(H):
                _gla_head_bounded(c * C, h, q_ref, v_ref, o_ref, st_ref, b_ref, kf_ref, scale)

    @pl.when(jnp.logical_not(bounded))
    def _exact():
        def chunk(c, carry):
            row0 = pl.multiple_of(c * C, C)
            bc_ref[...] = b_ref[pl.ds(row0, C), :]
            kc_ref[...] = kf_ref[pl.ds(row0, C), :]
            for h in range(H):
                _gla_head_exact(row0, h, q_ref, v_ref, o_ref, st_ref, bc_ref, kc_ref, scale)
            return carry
        lax.fori_loop(0, nch, chunk, 0)


def _gla(q, k, v, glr, w2p, gb, tril, *, B, S):
    C, H, dk, dv = GLA_CHUNK, GLA_HEADS, GLA_DK, GLA_DV
    rows = C * min(GLA_STEP_CHUNKS, S // C)
    nchunk = S // rows
    tok = lambda n: pl.BlockSpec((rows, n), lambda b, c: (b * nchunk + c, 0))
    const = lambda shape: pl.BlockSpec(shape, lambda b, c: (0,) * len(shape))
    return pl.pallas_call(
        functools.partial(_gla_kernel, scale=float(dk) ** -0.5),
        grid=(B, nchunk),
        in_specs=[tok(H * dk), tok(H * dk), tok(H * dv), tok(LANES),
                  const(w2p.shape), const(gb.shape), const(tril.shape)],
        out_specs=tok(H * dv),
        out_shape=jax.ShapeDtypeStruct((B * S, H * dv), BF16),
        scratch_shapes=[pltpu.VMEM((H, dv, dk), F32),
                        pltpu.VMEM((rows, H * dk), F32),
                        pltpu.VMEM((rows, H * dk), F32),
                        pltpu.VMEM((C, H * dk), F32),
                        pltpu.VMEM((C, H * dk), F32)],
        compiler_params=pltpu.CompilerParams(dimension_semantics=("parallel", "arbitrary"),
                                             vmem_limit_bytes=VMEM_LIMIT),
        name="gla",
    )(q, k, v, glr, w2p, gb, tril)


def _odd_out_kernel(x_ref, o_ref_in, z_ref, gn_ref, w_ref, g_ref, b_ref, out_ref, *, alpha):
    H, dv = GLA_HEADS, GLA_DV
    parts = []
    for h in range(H):
        sl = pl.ds(h * dv, dv)
        oh = o_ref_in[:, sl].astype(F32)
        r = lax.rsqrt(jnp.mean(oh * oh, axis=-1, keepdims=True) + LN_EPS)
        z = z_ref[:, sl].astype(F32)
        parts.append((oh * r * gn_ref[:, sl] * (z * _sigmoid(z))).astype(BF16))
    y = _dot(jnp.concatenate(parts, axis=1), w_ref[...])
    out_ref[...] = _layer_norm(alpha * x_ref[...] + y, g_ref[...], b_ref[...])


def _odd_out(x2d, o, z, gn_g, w_out, ln_g, ln_b, alpha, tm=1024):
    T, D = x2d.shape
    W = o.shape[1]
    row = lambda n: pl.BlockSpec((tm, n), lambda i: (i, 0))
    return pl.pallas_call(
        functools.partial(_odd_out_kernel, alpha=alpha),
        grid=(T // tm,),
        in_specs=[row(D), row(W), row(W),
                  pl.BlockSpec((1, W), lambda i: (0, 0)),
                  pl.BlockSpec((W, D), lambda i: (0, 0)),
                  pl.BlockSpec((1, D), lambda i: (0, 0)),
                  pl.BlockSpec((1, D), lambda i: (0, 0))],
        out_specs=row(D),
        out_shape=jax.ShapeDtypeStruct((T, D), F32),
        compiler_params=pltpu.CompilerParams(dimension_semantics=("parallel",),
                                             vmem_limit_bytes=VMEM_LIMIT),
        name="odd_out",
    )(x2d, o, z, gn_g, w_out, ln_g, ln_b)


def _key_features(S, block):
    s = np.arange(S)
    f = np.zeros((S, LANES), np.float32)
    f[s, s // block] = 1.0
    f[:, FEAT + 0] = 1.0
    f[:, FEAT + 1] = 1.0
    f[:, FEAT + 2] = FEAT * (s // FEAT)
    f[:, FEAT + 3] = s % FEAT
    return jnp.asarray(f, BF16)


def _pad_to(a, n, axis):
    pad = [(0, 0)] * a.ndim
    pad[axis] = (0, n - a.shape[axis])
    return jnp.pad(a, pad)


def _even_layer(x2d, B, S, w_in, cmp_pos, cmp_w1, cmp_b1, cmp_w2, w_out, ln_g, ln_b, alpha):
    T, D = x2d.shape
    G, hd = NSA_KV_GROUPS, HEAD_DIM
    scale = hd ** -0.5
    wq, wkv, wgate, wnz, wmq, wmk, wmv, wmz = jnp.split(w_in, np.cumsum(EVEN_SPLITS)[:-1].tolist(), axis=1)
    wkc, wvc, wks, wvs, wkw, wvw = jnp.split(wkv, 6, axis=1)
    w_tok = jnp.concatenate([wkc, wvc, wks, wkw, wmk], axis=1).astype(BF16)
    w_tr = jnp.concatenate([wq * scale, wvs, wvw, wmq * scale, wmv, wnz, wmz,
                            _pad_to(wgate, GATE_ROWS, 1)], axis=1).T.astype(BF16)
    (kcmp, vcmp, ksel, kwin, mk, nq_t, vsel_t, vwin_t, mq_t, mv_t, nz_t, mz_t, gate_t) = _proj(
        x2d, w_tok, (NSA_KV_W,) * 4 + (MOBA_W,), (F32, F32, BF16, BF16, BF16),
        w_tr, (NSA_W, NSA_KV_W, NSA_KV_W, MOBA_W, MOBA_W, NSA_W, MOBA_W, GATE_ROWS),
        (BF16,) * 7 + (F32,))

    L, d, HID = NSA_CMP_BLOCK, NSA_CMP_STRIDE, NSA_CMP_HIDDEN
    ncmp = (S - L) // d + 1
    NC = S // d
    eye = jnp.eye(G, dtype=F32)
    w1r = cmp_w1.reshape(2, 2, d, hd, HID)
    w1p = jnp.einsum('ialdj,gh->ialgdhj', w1r, eye).reshape(2, 2, d * G * hd, G * HID).astype(BF16)
    posp = jnp.broadcast_to(cmp_pos.reshape(2, 2, d, 1, hd), (2, 2, d, G, hd)).reshape(2, 2, 1, d * G * hd)
    b1p = jnp.tile(cmp_b1, (1, G)).reshape(2, 1, G * HID)
    w2p = jnp.einsum('ijd,gh->igjhd', _pad_to(cmp_w2, LANES, 2), eye).reshape(2, G * HID, G * LANES).astype(BF16)
    cfeat = np.zeros((NC, G, LANES), np.float32)
    cfeat[:, :, hd + 0] = 1.0
    cfeat[:, :, hd + 1] = 1.0
    cfeat[:, :, hd + 2] = (np.arange(NC) * d)[:, None]
    cfeat[:, :, hd + 3] = L - 1
    kvc = _cmp_mlp(kcmp, vcmp, w1p, posp, b1p, w2p, jnp.asarray(cfeat.reshape(NC, G * LANES)), ncmp,
                   B=B, S=S)

    ns = S // NSA_SEL_BLOCK
    cmp_start = np.arange(NC) * d
    sel_start = np.arange(FEAT) * NSA_SEL_BLOCK
    ovt = ((cmp_start[None, :] <= sel_start[:, None] + NSA_SEL_BLOCK - 1)
           & (cmp_start[None, :] + L - 1 >= sel_start[:, None])
           & (np.arange(NC)[None, :] < ncmp) & (np.arange(FEAT)[:, None] < ns))
    o_cmp, bias_sel = _cmp_attn(nq_t, kvc, jnp.asarray(ovt, BF16), B=B, S=S)

    feats_sel = _key_features(S, NSA_SEL_BLOCK)
    o_sel = _flash(nq_t, bias_sel, ksel, feats_sel, vsel_t,
                   B=B, S=S, units=G, tq=TQ, nh=NSA_HG, pair=False, window=None)
    o_win = _flash(nq_t, None, kwin, feats_sel, vwin_t,
                   B=B, S=S, units=G, tq=TQ, nh=NSA_HG, pair=False, window=NSA_WINDOW)

    avg = np.zeros((MOBA_NBP, S), np.float32)
    avg[np.arange(S) // MOBA_BLOCK, np.arange(S)] = 1.0 / MOBA_BLOCK
    bias_m = _moba_gate(mq_t, mk, jnp.asarray(avg, BF16), B=B, S=S)
    o_m = _flash(mq_t, bias_m, mk, _key_features(S, MOBA_BLOCK), mv_t,
                 B=B, S=S, units=MOBA_HEADS // MOBA_STEP_HEADS, tq=TQM, nh=MOBA_STEP_HEADS, pair=True, window=None)

    return _even_out(x2d, o_cmp, o_sel, o_win, o_m, gate_t, nz_t, mz_t, w_out.astype(BF16),
                     ln_g.reshape(1, D), ln_b.reshape(1, D), alpha, B=B, S=S)


def _odd_layer(x2d, B, S, w_in, gate_w2, gate_b, gn_g, w_out, ln_g, ln_b, alpha):
    T, D = x2d.shape
    H, dk, dv, C = GLA_HEADS, GLA_DK, GLA_DV, GLA_CHUNK
    wq, wk, wv, wg, wz = jnp.split(w_in, np.cumsum(ODD_SPLITS)[:-1].tolist(), axis=1)
    w_perm = jnp.concatenate([wq, wk, wv, wz, _pad_to(wg, LANES, 1)], axis=1).astype(BF16)
    q, k, v, z, glr = _proj(x2d, w_perm, (H * dk, H * dk, H * dv, H * dv, LANES),
                            (BF16, BF16, BF16, BF16, F32))
    w2p = _pad_to(gate_w2, LANES, 0)
    ncs = min(GLA_CUMSUM_CHUNKS, GLA_STEP_CHUNKS, S // C)
    tril = jnp.asarray(np.kron(np.eye(ncs), np.tril(np.ones((C, C)))), BF16)
    o = _gla(q, k, v, glr, w2p, gate_b.reshape(1, H * dk), tril, B=B, S=S)
    return _odd_out(x2d, o, z, gn_g.reshape(1, H * dv), w_out.astype(BF16),
                    ln_g.reshape(1, D), ln_b.reshape(1, D), alpha)


def kernel(x, ev_w_in, ev_cmp_pos, ev_cmp_w1, ev_cmp_b1, ev_cmp_w2, ev_w_out, ev_ln_g, ev_ln_b,
           od_w_in, od_gate_w2, od_gate_b, od_gn_g, od_w_out, od_ln_g, od_ln_b):
    B, S, D = x.shape
    depth = ev_w_in.shape[0] + od_w_in.shape[0]
    alpha = (2.0 * depth) ** 0.25
    h = x.reshape(B * S, D)
    for layer in range(depth):
        i = layer // 2
        if layer % 2 == 0:
            h = _even_layer(h, B, S, ev_w_in[i], ev_cmp_pos[i], ev_cmp_w1[i], ev_cmp_b1[i], ev_cmp_w2[i],
                            ev_w_out[i], ev_ln_g[i], ev_ln_b[i], alpha)
        else:
            h = _odd_layer(h, B, S, od_w_in[i], od_gate_w2[i], od_gate_b[i], od_gn_g[i],
                           od_w_out[i], od_ln_g[i], od_ln_b[i], alpha)
    return h.reshape(B, S, D)
```

```python
import functools

import numpy as np
import jax
import jax.numpy as jnp
from jax import lax
from jax.experimental import pallas as pl
from jax.experimental.pallas import tpu as pltpu

F32 = jnp.float32
BF16 = jnp.bfloat16

HEAD_DIM = 64
NSA_HEADS = 8
NSA_KV_GROUPS = 2
NSA_HG = NSA_HEADS // NSA_KV_GROUPS
NSA_CMP_BLOCK = 32
NSA_CMP_STRIDE = 16
NSA_CMP_HIDDEN = 128
NSA_SEL_BLOCK = 64
NSA_TOP_N = 16
NSA_WINDOW = 512
MOBA_HEADS = 8
MOBA_BLOCK = 256
MOBA_TOP_K = 3
GLA_HEADS = 4
GLA_DK = 128
GLA_DV = 256
GLA_GATE_RANK = 16
GLA_TAU = 16.0
GLA_CHUNK = 64
GLA_SUB = 16
GLA_STEP_CHUNKS = 16
GLA_CUMSUM_CHUNKS = 4
GLA_FACTOR_MAX_DECAY = 60.0
LN_EPS = 1e-5
NEG = -1e30
FORCE_BONUS = 1e4

NSA_W = NSA_HEADS * HEAD_DIM
NSA_KV_W = NSA_KV_GROUPS * HEAD_DIM
MOBA_W = MOBA_HEADS * HEAD_DIM
EVEN_SPLITS = (NSA_W, 6 * NSA_KV_W, 3 * NSA_HEADS, NSA_W, MOBA_W, MOBA_W, MOBA_W, MOBA_W)
ODD_SPLITS = (GLA_HEADS * GLA_DK, GLA_HEADS * GLA_DK, GLA_HEADS * GLA_DV, GLA_GATE_RANK, GLA_HEADS * GLA_DV)

LANES = 128
SUBLANES = 8
FEAT = 64
ALIBI_ROWS = 16
GATE_ROWS = 32
MOBA_NBP = 16
TQ = 512
TQM = 512
MOBA_STEP_HEADS = 4
TK = 512
CG = 256
MIN_WEIGHT_SUM = 1e-25
VROWS = HEAD_DIM + 16
VMEM_LIMIT = 48 * 1024 * 1024


def _dot(a, b):
    return jnp.dot(a, b, preferred_element_type=F32)


def _dot_nt(a, b):
    return lax.dot_general(a, b, (((1,), (1,)), ((), ())), preferred_element_type=F32)


def _dot_tn(a, b):
    return lax.dot_general(a, b, (((0,), (0,)), ((), ())), preferred_element_type=F32)


def _hilo(a):
    hi = a.astype(BF16)
    lo = (a - hi.astype(F32)).astype(BF16)
    return hi, lo


def _pow2_neg(n):
    return lax.bitcast_convert_type(lax.shift_left(127 - n, 23), F32)


def _sigmoid(x):
    return 0.5 * jnp.tanh(0.5 * x) + 0.5


def _layer_norm(r, g, b):
    mu = jnp.mean(r, axis=-1, keepdims=True)
    d = r - mu
    var = jnp.mean(d * d, axis=-1, keepdims=True)
    return d * lax.rsqrt(var + LN_EPS) * g + b


def _alibi_query_rows(t0, tq, head0, R):
    r = lax.broadcasted_iota(jnp.int32, (ALIBI_ROWS, R), 0)
    col = lax.broadcasted_iota(jnp.int32, (ALIBI_ROWS, R), 1)
    t = t0 + col % tq
    slope = _pow2_neg(head0 + col // tq + 1)
    feats = jnp.where(r == 0, -FEAT * slope * (t // FEAT).astype(F32),
                      jnp.where(r == 1, -slope * (t % FEAT).astype(F32), jnp.where(r < 4, slope, 0.0)))
    return jnp.concatenate([feats.astype(BF16), jnp.zeros((FEAT - ALIBI_ROWS, R), BF16)], axis=0)


def _ranks(sc_ref, n_real, n_rows=None):
    width = sc_ref.shape[1]
    n_rows = sc_ref.shape[0] if n_rows is None else n_rows
    out = []
    for v in range(n_rows // SUBLANES):
        lo = v * SUBLANES
        sc = sc_ref[lo:lo + SUBLANES, :]
        idx = lo + lax.broadcasted_iota(jnp.int32, (SUBLANES, width), 0)
        rank = jnp.zeros((SUBLANES, width), jnp.int32)
        for jp in range(n_real):
            r = sc_ref[pl.ds(jp, 1), :]
            if jp < lo:
                beats = r >= sc
            elif jp >= lo + SUBLANES - 1:
                beats = r > sc
            else:
                beats = (r > sc) | ((r == sc) & (idx > jp))
            rank = rank + jnp.where(beats, 1, 0)
        out.append(rank)
    return jnp.concatenate(out, axis=0)


def _proj_kernel(*refs, tok_splits, tr_splits):
    x_ref, wt_ref = refs[0], refs[1]
    outs = list(refs[3:] if tr_splits else refs[2:])
    x = x_ref[...]
    xb = x.astype(BF16)
    off = 0
    for n in tok_splits:
        o_ref = outs.pop(0)
        o_ref[...] = _dot(xb, wt_ref[:, off:off + n]).astype(o_ref.dtype)
        off += n
    if tr_splits:
        wtr_ref = refs[2]
        xt = x.T.astype(BF16)
        off = 0
        for n in tr_splits:
            o_ref = outs.pop(0)
            o_ref[...] = _dot(wtr_ref[off:off + n, :], xt).astype(o_ref.dtype)
            off += n


def _proj(x2d, w_tok, tok_splits, tok_dtypes, w_tr=None, tr_splits=(), tr_dtypes=(), tm=512):
    T, D = x2d.shape
    assert T % tm == 0 and sum(tok_splits) == w_tok.shape[1]
    in_specs = [pl.BlockSpec((tm, D), lambda i: (i, 0)),
                pl.BlockSpec(w_tok.shape, lambda i: (0, 0))]
    args = [x2d, w_tok]
    if tr_splits:
        assert sum(tr_splits) == w_tr.shape[0]
        in_specs.append(pl.BlockSpec(w_tr.shape, lambda i: (0, 0)))
        args.append(w_tr)
    out_specs = ([pl.BlockSpec((tm, n), lambda i: (i, 0)) for n in tok_splits]
                 + [pl.BlockSpec((n, tm), lambda i: (0, i)) for n in tr_splits])
    out_shape = ([jax.ShapeDtypeStruct((T, n), dt) for n, dt in zip(tok_splits, tok_dtypes)]
                 + [jax.ShapeDtypeStruct((n, T), dt) for n, dt in zip(tr_splits, tr_dtypes)])
    return pl.pallas_call(
        functools.partial(_proj_kernel, tok_splits=tuple(tok_splits), tr_splits=tuple(tr_splits)),
        grid=(T // tm,), in_specs=in_specs, out_specs=out_specs, out_shape=out_shape,
        compiler_params=pltpu.CompilerParams(dimension_semantics=("parallel",),
                                             vmem_limit_bytes=VMEM_LIMIT),
        name="proj",
    )(*args)


def _cmp_mlp_kernel(kx_ref, vx_ref, w1_ref, pos_ref, b1_ref, w2_ref, cf_ref, o_ref, *, ncmp):
    d = NSA_CMP_STRIDE
    nc = kx_ref.shape[0] // d
    cw = w1_ref.shape[2]
    for i, x_ref in enumerate((kx_ref, vx_ref)):
        a = bm = None
        for l in range(0, d, 2):
            x2 = jnp.concatenate([x_ref[pl.ds(l, nc, stride=d), :], x_ref[pl.ds(l + 1, nc, stride=d), :]],
                                 axis=1).astype(BF16)
            rows = pl.ds(l * x_ref.shape[1], 2 * x_ref.shape[1])
            a_l, b_l = _dot(x2, w1_ref[i, 0, rows, :]), _dot(x2, w1_ref[i, 1, rows, :])
            a, bm = (a_l, b_l) if a is None else (a + a_l, bm + b_l)
        c = b1_ref[i]
        for half in range(2):
            phi, plo = _hilo(jnp.broadcast_to(pos_ref[i, half], (SUBLANES, cw)))
            c = c + (_dot(phi, w1_ref[i, half]) + _dot(plo, w1_ref[i, half]))[0:1]
        hid = a + pltpu.roll(bm, nc - 1, axis=0) + c
        hid = hid * _sigmoid(hid)
        row = lax.broadcasted_iota(jnp.int32, hid.shape, 0)
        hid = jnp.where(row < ncmp, hid, 0.0)
        out = _dot(hid.astype(BF16), w2_ref[i])
        if i == 0:
            out = out + cf_ref[...]
        o_ref[i] = out.astype(o_ref.dtype)


def _cmp_mlp(kx, vx, w1p, posp, b1p, w2p, cfeat, ncmp, *, B, S):
    NC = S // NSA_CMP_STRIDE
    OW = w2p.shape[-1]
    full = lambda a: pl.BlockSpec(a.shape, lambda b: (0,) * a.ndim)
    return pl.pallas_call(
        functools.partial(_cmp_mlp_kernel, ncmp=ncmp),
        grid=(B,),
        in_specs=[pl.BlockSpec((S, kx.shape[1]), lambda b: (b, 0)),
                  pl.BlockSpec((S, vx.shape[1]), lambda b: (b, 0)),
                  full(w1p), full(posp), full(b1p), full(w2p), full(cfeat)],
        out_specs=pl.BlockSpec((2, None, NC, OW), lambda b: (0, b, 0, 0)),
        out_shape=jax.ShapeDtypeStruct((2, B, NC, OW), BF16),
        compiler_params=pltpu.CompilerParams(dimension_semantics=("parallel",),
                                             vmem_limit_bytes=VMEM_LIMIT),
        name="cmp_mlp",
    )(kx, vx, w1p, posp, b1p, w2p, cfeat)


def _cmp_attn_kernel(q_ref, kc_ref, vc_ref, ovt_ref, o_ref, bias_ref, sc_ref, *, tq, nh, ns, ntop):
    g = pl.program_id(1)
    R = nh * tq
    nq = q_ref.shape[1] // tq

    def q_tile(qi):
        toks = pl.ds(qi * tq, tq)
        nc = min(kc_ref.shape[0], -(-((qi + 1) * tq) // (NSA_CMP_STRIDE * 16)) * 16)
        nsel = min(ns, -(-((qi + 1) * tq) // (NSA_SEL_BLOCK * SUBLANES)) * SUBLANES)
        qcat = jnp.concatenate([q_ref[hh * HEAD_DIM:(hh + 1) * HEAD_DIM, toks] for hh in range(nh)], axis=1)
        qa = jnp.concatenate([qcat, _alibi_query_rows(qi * tq, tq, g * nh, R)], axis=0)
        s = _dot(kc_ref[0:nc, :], qa)
        n = lax.broadcasted_iota(jnp.int32, (nc, R), 0)
        t_row = qi * tq + lax.broadcasted_iota(jnp.int32, (1, R), 1) % tq
        last = (t_row - (NSA_CMP_BLOCK - 1)) // NSA_CMP_STRIDE
        s = jnp.where(n <= last, s, NEG)
        e = jnp.exp(s - jnp.max(s, axis=0, keepdims=True))
        inv = jnp.where(last >= 0, 1.0 / jnp.sum(e, axis=0, keepdims=True), 0.0)
        p = e * inv
        o_ref[:, pl.ds(qi * R, R)] = _dot_tn(vc_ref[0:nc, 0:HEAD_DIM], p.astype(BF16)).astype(o_ref.dtype)

        psum = p[:, 0:tq]
        for hh in range(1, nh):
            psum = psum + p[:, hh * tq:(hh + 1) * tq]
        phi, plo = _hilo(psum)
        ovt = ovt_ref[0:nsel, 0:nc]
        imp = _dot(ovt, phi) + _dot(ovt, plo)

        j = lax.broadcasted_iota(jnp.int32, (nsel, tq), 0)
        cur = (qi * tq + lax.broadcasted_iota(jnp.int32, (nsel, tq), 1)) // NSA_SEL_BLOCK
        if nsel <= ntop:
            sel = j <= cur
        else:
            forced = jnp.where((j == 0) | (j == cur) | (j == cur - 1), FORCE_BONUS, 0.0)
            sc_ref[0:nsel, :] = jnp.where(j <= cur, imp + forced, NEG)
            sel = (_ranks(sc_ref, nsel, nsel) < ntop) & (j <= cur)
        bias_ref[0:nsel, toks] = jnp.where(sel, 0.0, NEG).astype(bias_ref.dtype)
        if nsel < FEAT:
            bias_ref[nsel:FEAT, toks] = jnp.full((FEAT - nsel, tq), NEG, bias_ref.dtype)

    for qi in range(nq):
        q_tile(qi)


def _cmp_attn(nq_t, kvc, ovt, *, B, S):
    tq, nh, G = TQ, NSA_HG, NSA_KV_GROUPS
    R = nh * tq
    nq = S // tq
    NC = kvc.shape[2]
    ns = S // NSA_SEL_BLOCK
    assert ns <= FEAT
    ntop = min(NSA_TOP_N, ns)
    return pl.pallas_call(
        functools.partial(_cmp_attn_kernel, tq=tq, nh=nh, ns=ns, ntop=ntop),
        grid=(B, G),
        in_specs=[pl.BlockSpec((nh * HEAD_DIM, S), lambda b, g: (g, b)),
                  pl.BlockSpec((None, None, NC, LANES), lambda b, g: (0, b, 0, g)),
                  pl.BlockSpec((None, None, NC, LANES), lambda b, g: (1, b, 0, g)),
                  pl.BlockSpec(ovt.shape, lambda b, g: (0, 0))],
        out_specs=[pl.BlockSpec((None, None, HEAD_DIM, S * nh), lambda b, g: (b, g, 0, 0)),
                   pl.BlockSpec((None, None, FEAT, S), lambda b, g: (b, g, 0, 0))],
        out_shape=[jax.ShapeDtypeStruct((B, G, HEAD_DIM, S * nh), BF16),
                   jax.ShapeDtypeStruct((B, G, FEAT, S), BF16)],
        scratch_shapes=[pltpu.VMEM((FEAT, tq), F32)],
        compiler_params=pltpu.CompilerParams(dimension_semantics=("parallel", "parallel"),
                                             vmem_limit_bytes=VMEM_LIMIT),
        name="cmp_attn",
    )(nq_t, kvc, kvc, ovt)


def _moba_gate_kernel(q_ref, k_ref, a_ref, bias_ref, km_ref, sc_ref, *, tq, nb, ntop):
    h = pl.program_id(1)
    qi = pl.program_id(2)

    @pl.when(qi == 0)
    def _means():
        km_ref[...] = _dot(a_ref[...], k_ref[...])

    nbp = km_ref.shape[0]
    lane = lax.broadcasted_iota(jnp.int32, (nbp, LANES), 1)
    km = jnp.where(lane // HEAD_DIM == h % 2, km_ref[...], 0.0)
    khi, klo = _hilo(km)
    q = q_ref[...]
    qq = jnp.concatenate([q, q], axis=0)
    g_t = _dot(khi, qq) + _dot(klo, qq)
    n = lax.broadcasted_iota(jnp.int32, (nbp, tq), 0)
    cur = (qi * tq + lax.broadcasted_iota(jnp.int32, (nbp, tq), 1)) // MOBA_BLOCK
    sc = jnp.where(n < cur, g_t, NEG)
    sc_ref[...] = sc
    sel = ((_ranks(sc_ref, nb) < ntop) & (n < cur)) | (n == cur)
    bias_ref[0:nbp, :] = jnp.where(sel, 0.0, NEG).astype(bias_ref.dtype)
    bias_ref[nbp:FEAT, :] = jnp.zeros((FEAT - nbp, tq), bias_ref.dtype)


def _moba_gate(mq_t, mk, avg, *, B, S):
    tq, H = S, MOBA_HEADS
    nq = S // tq
    nb = S // MOBA_BLOCK
    nbp = avg.shape[0]
    assert nb <= nbp
    ntop = min(MOBA_TOP_K, nb)
    return pl.pallas_call(
        functools.partial(_moba_gate_kernel, tq=tq, nb=nb, ntop=ntop),
        grid=(B, H, nq),
        in_specs=[pl.BlockSpec((HEAD_DIM, tq), lambda b, h, i: (h, b * nq + i)),
                  pl.BlockSpec((S, LANES), lambda b, h, i: (b, h // 2)),
                  pl.BlockSpec(avg.shape, lambda b, h, i: (0, 0))],
        out_specs=pl.BlockSpec((None, None, FEAT, tq), lambda b, h, i: (b, h, 0, i)),
        out_shape=jax.ShapeDtypeStruct((B, H, FEAT, S), BF16),
        scratch_shapes=[pltpu.VMEM((nbp, LANES), F32), pltpu.VMEM((nbp, tq), F32)],
        compiler_params=pltpu.CompilerParams(dimension_semantics=("parallel", "parallel", "arbitrary"),
                                             vmem_limit_bytes=VMEM_LIMIT),
        name="moba_gate",
    )(mq_t, mk, avg)


def _flash_q_tile(qi, running_max, q_ref, bias_ref, mask_ref, o_ref, ka_ref, vta_ref, kn_ref, qa_ref, bnd_ref,
                  m_ref, acc_ref, alpha_ref, p_ref, *, tq, tk, nh, pair, window):
    u = pl.program_id(1)
    R = nh * tq
    static = isinstance(qi, int)
    q0 = qi * tq if static else pl.multiple_of(qi * tq, tq)
    toks = pl.ds(q0, tq)

    qcat = jnp.concatenate([q_ref[hh * HEAD_DIM:(hh + 1) * HEAD_DIM, toks] for hh in range(nh)], axis=1)
    qf = qcat.astype(F32)
    qq = jnp.concatenate([qf, qf], axis=0)
    rowi = lax.broadcasted_iota(jnp.int32, (2 * HEAD_DIM, R), 0)
    coli = lax.broadcasted_iota(jnp.int32, (2 * HEAD_DIM, R), 1)
    half = (coli // tq) % 2 if pair else u % 2
    qa_ref[0:2 * HEAD_DIM, :] = jnp.where(rowi // HEAD_DIM == half, qq, 0.0).astype(BF16)
    if bias_ref is None:
        qa_ref[2 * HEAD_DIM:2 * HEAD_DIM + FEAT, :] = jnp.zeros((FEAT, R), BF16)
    else:
        qa_ref[2 * HEAD_DIM:2 * HEAD_DIM + FEAT, :] = jnp.concatenate(
            [bias_ref[hh if pair else 0, :, toks] for hh in range(nh)], axis=1)
    qa_ref[2 * HEAD_DIM + FEAT:2 * LANES, :] = _alibi_query_rows(q0, tq, u * nh, R)
    if pair:
        kn = jnp.concatenate([kn_ref[hh:hh + 1, :] for hh in range(nh) for _ in range(tq // LANES)], axis=1)
    else:
        kn = jnp.concatenate([jnp.where(u % 2 == 0, kn_ref[0:1, :], kn_ref[1:2, :])] * (R // LANES), axis=1)
    qn = jnp.sqrt(jnp.sum(qf * qf, axis=0, keepdims=True))
    bnd_ref[...] = jnp.broadcast_to(qn * kn, bnd_ref.shape)

    ncg = R // CG
    kd = (q0 + tq - 1) // tk

    def key_rows(kind, c):
        off = (c * CG) % tq
        if kind is None:
            return 0, tk, None
        if kind == "diag":
            return 0, off + CG, off // CG
        return off, tk, (tk + off) // CG

    def values(ki, kind=None):
        for c in range(ncg):
            cols = pl.ds(c * CG, CG)
            r0, r1, _ = key_rows(kind, c)
            v = (c * CG) // tq if pair else 0
            pv = _dot(vta_ref[v, ki, :, r0:r1], p_ref[r0:r1, cols])
            if running_max:
                acc_ref[:, cols] = alpha_ref[0:1, cols] * acc_ref[:, cols] + pv
            else:
                acc_ref[:, cols] = acc_ref[:, cols] + pv

    def scores(ki, kind=None, prev=None, prev_kind=None):
        k0 = ki * tk if isinstance(ki, int) else pl.multiple_of(ki * tk, tk)
        s_groups = []
        for c in range(ncg):
            r0, r1, _ = key_rows(kind, c)
            kb = (c * CG) // (2 * tq) if pair else 0
            s_groups.append(_dot(ka_ref[kb, pl.ds(k0 + r0, r1 - r0), :], qa_ref[:, pl.ds(c * CG, CG)]))
        if prev is not None:
            values(prev, prev_kind)
        for c, s in enumerate(s_groups):
            cols = pl.ds(c * CG, CG)
            r0, r1, entry = key_rows(kind, c)
            if entry is not None:
                s = s + mask_ref[entry, r0:r1, :]
            if running_max:
                m_prev = m_ref[0:1, cols]
                m_new = jnp.maximum(m_prev, jnp.max(s, axis=0, keepdims=True))
                alpha_ref[:, cols] = jnp.broadcast_to(jnp.exp(m_prev - m_new), (alpha_ref.shape[0], CG))
                m_ref[:, cols] = jnp.broadcast_to(m_new, (m_ref.shape[0], CG))
                p_ref[r0:r1, cols] = jnp.exp(s - m_new).astype(BF16)
            else:
                p_ref[r0:r1, cols] = jnp.exp(s - bnd_ref[0:1, cols]).astype(BF16)

    def steady(lo, hi):
        def pair_body(j, carry):
            ki = lo + 2 * j
            scores(ki, prev=ki - 1)
            scores(ki + 1, prev=ki)
            return carry
        lax.fori_loop(0, (hi - lo) // 2, pair_body, 0)

        @pl.when((hi - lo) % 2 == 1)
        def _odd_one():
            scores(hi - 1, prev=hi - 2)

    def only_diagonal():
        scores(0, "diag")
        values(0, "diag")

    def past_then_diagonal():
        if window is not None:
            scores(kd - 1, "wedge")
            scores(kd, "diag", prev=kd - 1, prev_kind="wedge")
        elif static:
            scores(0)
            for ki in range(1, kd):
                scores(ki, prev=ki - 1)
            scores(kd, "diag", prev=kd - 1)
        else:
            scores(0)
            steady(1, kd)
            scores(kd, "diag", prev=kd - 1)
        values(kd, "diag")

    acc_ref[...] = jnp.zeros(acc_ref.shape, F32)
    if running_max:
        m_ref[...] = jnp.full(m_ref.shape, NEG, F32)
    if static:
        only_diagonal() if kd == 0 else past_then_diagonal()
    else:
        pl.when(kd == 0)(only_diagonal)
        pl.when(kd > 0)(past_then_diagonal)

    acc = acc_ref[...]
    out = (acc[0:HEAD_DIM] / acc[HEAD_DIM:HEAD_DIM + 1]).astype(o_ref.dtype)
    if pair:
        for hh in range(nh):
            o_ref[hh, :, toks] = out[:, hh * tq:(hh + 1) * tq]
    else:
        o_ref[0, :, pl.ds(qi * R if static else pl.multiple_of(qi * R, R), R)] = out
    return jnp.min(acc[HEAD_DIM:HEAD_DIM + 1, :])


def _flash_kernel(*refs, tq, tk, nh, pair, window, has_bias):
    q_ref = refs[0]
    bias_ref = refs[1] if has_bias else None
    (k_ref, kf_ref, vt_ref, mask_ref, o_ref, ka_ref, vta_ref, kn_ref, qa_ref, bnd_ref, m_ref, acc_ref,
     alpha_ref, p_ref, low_ref) = refs[2 if has_bias else 1:]
    S = k_ref.shape[0]
    for kb in range(ka_ref.shape[0]):
        k = k_ref[:, kb * LANES:(kb + 1) * LANES]
        ka_ref[kb, :, 0:LANES] = k
        ka_ref[kb, :, LANES:2 * LANES] = kf_ref[...]
        k2 = jnp.square(k.astype(F32))
        lane = lax.broadcasted_iota(jnp.int32, k2.shape, 1)
        for hf in range(2):
            n2 = jnp.sum(jnp.where(lane // HEAD_DIM == hf, k2, 0.0), axis=1, keepdims=True)
            kn_ref[2 * kb + hf:2 * kb + hf + 1, :] = jnp.broadcast_to(
                jnp.sqrt(jnp.max(n2, axis=0, keepdims=True)), (1, LANES))
    ones_row = jnp.where(lax.broadcasted_iota(jnp.int32, (VROWS - HEAD_DIM, tk), 0) == 0, 1.0, 0.0)
    for v in range(nh if pair else 1):
        for jt in range(S // tk):
            vta_ref[v, jt, 0:HEAD_DIM, :] = vt_ref[v * HEAD_DIM:(v + 1) * HEAD_DIM, jt * tk:(jt + 1) * tk]
            vta_ref[v, jt, HEAD_DIM:VROWS, :] = ones_row.astype(BF16)

    tile = functools.partial(_flash_q_tile, q_ref=q_ref, bias_ref=bias_ref, mask_ref=mask_ref, o_ref=o_ref,
                             ka_ref=ka_ref, vta_ref=vta_ref, kn_ref=kn_ref, qa_ref=qa_ref, bnd_ref=bnd_ref,
                             m_ref=m_ref, acc_ref=acc_ref, alpha_ref=alpha_ref, p_ref=p_ref,
                             tq=tq, tk=tk, nh=nh, pair=pair, window=window)
    nq = S // tq
    if window is not None:
        for qi in range(nq):
            low_ref[qi] = tile(qi, False)
    else:
        def bound_stabilised(qi, carry):
            low_ref[qi] = tile(qi, False)
            return carry
        lax.fori_loop(0, nq, bound_stabilised, 0)

    def redo(qi, carry):
        @pl.when(low_ref[qi] < MIN_WEIGHT_SUM)
        def _running_max():
            tile(qi, True)
        return carry

    lax.fori_loop(0, nq, redo, 0)


def _mask_table(tk, window):
    n_delta = (tk + (window or 0)) // CG
    d = np.arange(n_delta)[:, None, None] * CG + np.arange(CG)[None, None, :] - np.arange(tk)[None, :, None]
    ok = (d >= 0) & ((d < window) if window else True)
    return jnp.asarray(np.where(ok, 0.0, NEG), F32)


def _flash(q_t, bias_t, k, kfeat, v_t, *, B, S, units, tq, nh, pair, window):
    tk = TK
    R = nh * tq
    nq = S // tq
    nv = nh if pair else 1
    nb = nh if pair else 1
    nkb = nh // 2 if pair else 1
    assert tq == tk and S % tk == 0 and tk % CG == 0 and window in (None, tk)
    mask = _mask_table(tk, window)
    kcol = (lambda u: u) if pair else (lambda u: u // 2)
    out_w = tq if pair else R
    out_shape = (B, units * nb, HEAD_DIM, S if pair else S * nh)
    has_bias = bias_t is not None
    in_specs = [pl.BlockSpec((nh * HEAD_DIM, S), lambda b, u: (u, b))]
    if has_bias:
        in_specs.append(pl.BlockSpec((None, nb, FEAT, S), lambda b, u: (b, u, 0, 0)))
    in_specs += [pl.BlockSpec((S, nkb * LANES), lambda b, u: (b, kcol(u))),
                 pl.BlockSpec((S, LANES), lambda b, u: (0, 0)),
                 pl.BlockSpec((nv * HEAD_DIM, S), lambda b, u: (u, b)),
                 pl.BlockSpec(mask.shape, lambda b, u: (0, 0, 0))]
    args = (q_t,) + ((bias_t,) if has_bias else ()) + (k, kfeat, v_t, mask)
    return pl.pallas_call(
        functools.partial(_flash_kernel, tq=tq, tk=tk, nh=nh, pair=pair, window=window, has_bias=has_bias),
        grid=(B, units),
        in_specs=in_specs,
        out_specs=pl.BlockSpec((None, nb, HEAD_DIM, out_w * nq), lambda b, u: (b, u, 0, 0)),
        out_shape=jax.ShapeDtypeStruct(out_shape, BF16),
        scratch_shapes=[pltpu.VMEM((nkb, S, 2 * LANES), BF16),
                        pltpu.VMEM((nv, S // tk, VROWS, tk), BF16),
                        pltpu.VMEM((SUBLANES, LANES), F32),
                        pltpu.VMEM((2 * LANES, R), BF16),
                        pltpu.VMEM((SUBLANES, R), F32),
                        pltpu.VMEM((SUBLANES, R), F32),
                        pltpu.VMEM((VROWS, R), F32),
                        pltpu.VMEM((SUBLANES, R), F32),
                        pltpu.VMEM((tk, R), BF16),
                        pltpu.SMEM((nq,), F32)],
        compiler_params=pltpu.CompilerParams(dimension_semantics=("parallel", "parallel"),
                                             vmem_limit_bytes=VMEM_LIMIT),
        name="flash",
    )(*args)


def _even_out_kernel(x_ref, ocmp_ref, osel_ref, owin_ref, om_ref, gate_ref, nz_ref, mz_ref,
                     w_ref, g_ref, b_ref, o_ref, sig_ref, *, alpha, tm):
    G, nh = NSA_KV_GROUPS, NSA_HG
    sig_ref[...] = _sigmoid(gate_ref[...])
    heads = []
    for g in range(G):
        for hh in range(nh):
            acc = None
            for br, ref in enumerate((ocmp_ref, osel_ref, owin_ref)):
                gate = sig_ref[pl.ds(3 * (g * nh + hh) + br, 1), :]
                term = gate * ref[g, :, hh * tm:(hh + 1) * tm].astype(F32)
                acc = term if acc is None else acc + term
            heads.append(acc)
    o_nsa = jnp.concatenate(heads, axis=0)
    nz = nz_ref[...].astype(F32)
    mz = mz_ref[...].astype(F32)
    o_m = jnp.concatenate([om_ref[h] for h in range(MOBA_HEADS)], axis=0).astype(F32)
    a_t = jnp.concatenate([o_nsa * (nz * _sigmoid(nz)), o_m * (mz * _sigmoid(mz))], axis=0)
    y = _dot_tn(a_t.astype(BF16), w_ref[...])
    o_ref[...] = _layer_norm(alpha * x_ref[...] + y, g_ref[...], b_ref[...])


def _even_out(x2d, ocmp, osel, owin, om, gate_t, nz_t, mz_t, w_out, ln_g, ln_b, alpha, *, B, S):
    T, D = x2d.shape
    tm = TQ
    nq = S // tm
    G, nh, H = NSA_KV_GROUPS, NSA_HG, MOBA_HEADS
    nsa = pl.BlockSpec((None, G, HEAD_DIM, nh * tm), lambda i: (i // nq, 0, 0, i % nq))
    feat = lambda n: pl.BlockSpec((n, tm), lambda i: (0, i))
    full = lambda a: pl.BlockSpec(a.shape, lambda i: (0,) * a.ndim)
    return pl.pallas_call(
        functools.partial(_even_out_kernel, alpha=alpha, tm=tm),
        grid=(T // tm,),
        in_specs=[pl.BlockSpec((tm, D), lambda i: (i, 0)), nsa, nsa, nsa,
                  pl.BlockSpec((None, H, HEAD_DIM, tm), lambda i: (i // nq, 0, 0, i % nq)),
                  feat(GATE_ROWS), feat(NSA_W), feat(MOBA_W),
                  full(w_out), full(ln_g), full(ln_b)],
        out_specs=pl.BlockSpec((tm, D), lambda i: (i, 0)),
        out_shape=jax.ShapeDtypeStruct((T, D), F32),
        scratch_shapes=[pltpu.VMEM((GATE_ROWS, tm), F32)],
        compiler_params=pltpu.CompilerParams(dimension_semantics=("parallel",),
                                             vmem_limit_bytes=VMEM_LIMIT),
        name="even_out",
    )(x2d, ocmp, osel, owin, om, gate_t, nz_t, mz_t, w_out, ln_g, ln_b)


def _gla_head_exact(row0, h, q_ref, v_ref, o_ref, st_ref, b_ref, kf_ref, scale):
    C, dk, dv, SUB = GLA_CHUNK, GLA_DK, GLA_DV, GLA_SUB
    rows = lax.broadcasted_iota(jnp.int32, (C, dk), 0)
    sub_i = lax.broadcasted_iota(jnp.int32, (SUB, LANES), 0)
    lane = lax.broadcasted_iota(jnp.int32, (SUB, LANES), 1)
    ones = jnp.ones((dk, LANES), BF16)
    sl = pl.ds(h * dk, dk)
    rs_c = pl.ds(row0, C)
    bh = b_ref[:, sl]
    kf = kf_ref[:, sl]
    qf = q_ref[rs_c, sl].astype(F32) * scale
    vh = v_ref[rs_c, pl.ds(h * dv, dv)]
    st = st_ref[h]
    inter = _dot_nt((qf * jnp.exp(bh)).astype(BF16), st.astype(BF16))
    bl = b_ref[pl.ds(C - 1, 1), sl]
    kd = kf * jnp.exp(bl - bh)

    def off_block(r0, r1, c0, c1):
        ref_row = b_ref[pl.ds(r0, 1), sl]
        qt = qf[r0:r1] * jnp.exp(bh[r0:r1] - ref_row)
        ek = jnp.where((rows >= c0) & (rows < c1), ref_row - bh, NEG)
        kt = kf * jnp.exp(ek)
        return _dot_nt(qt.astype(BF16), kt.astype(BF16))

    lower = off_block(2 * SUB, C, 0, 2 * SUB)
    blocks = [None, off_block(SUB, 2 * SUB, 0, SUB), lower[0:SUB],
              lower[SUB:2 * SUB] + off_block(3 * SUB, C, 2 * SUB, 3 * SUB)]
    pieces = []
    for blk in range(C // SUB):
        r0 = blk * SUB
        q_i = qf[r0:r0 + SUB]
        b_i = bh[r0:r0 + SUB]
        for jj in range(SUB):
            b_j = b_ref[pl.ds(r0 + jj, 1), sl]
            k_j = kf_ref[pl.ds(r0 + jj, 1), sl]
            pieces.append((q_i * k_j * jnp.exp(jnp.minimum(b_i - b_j, 0.0))).astype(BF16))
    sums = _dot(jnp.concatenate(pieces, axis=0), ones)
    for blk in range(C // SUB):
        r0 = blk * SUB
        diag = jnp.zeros((SUB, LANES), F32)
        for jj in range(SUB):
            rs = sums[(r0 + jj) * SUB:(r0 + jj + 1) * SUB]
            diag = jnp.where((lane == r0 + jj) & (sub_i >= jj), rs, diag)
        diag = diag[:, 0:C]
        blocks[blk] = diag if blocks[blk] is None else blocks[blk] + diag
    att = jnp.concatenate(blocks, axis=0)
    o_ref[rs_c, pl.ds(h * dv, dv)] = (inter + _dot(att.astype(BF16), vh)).astype(o_ref.dtype)
    st_ref[h] = st * jnp.exp(bl) + _dot_tn(vh, kd.astype(BF16))


def _gla_head_bounded(row0, h, q_ref, v_ref, o_ref, st_ref, b_ref, kf_ref, scale):
    C, dk, dv = GLA_CHUNK, GLA_DK, GLA_DV
    sl = pl.ds(h * dk, dk)
    rs_c = pl.ds(row0, C)
    bh = b_ref[rs_c, sl]
    kf = kf_ref[rs_c, sl]
    qf = q_ref[rs_c, sl].astype(F32) * scale
    vh = v_ref[rs_c, pl.ds(h * dv, dv)]
    st = st_ref[h]
    qs = (qf * jnp.exp(bh)).astype(BF16)
    kt = (kf * jnp.exp(-bh)).astype(BF16)
    i = lax.broadcasted_iota(jnp.int32, (C, C), 0)
    j = lax.broadcasted_iota(jnp.int32, (C, C), 1)
    att = jnp.where(i >= j, _dot_nt(qs, kt), 0.0)
    inter = _dot_nt(qs, st.astype(BF16))
    o_ref[rs_c, pl.ds(h * dv, dv)] = (inter + _dot(att.astype(BF16), vh)).astype(o_ref.dtype)
    bl = b_ref[pl.ds(row0 + C - 1, 1), sl]
    kd = kf * jnp.exp(bl - bh)
    st_ref[h] = st * jnp.exp(bl) + _dot_tn(vh, kd.astype(BF16))


def _gla_kernel(q_ref, k_ref, v_ref, glr_ref, w2_ref, gb_ref, tril_ref, o_ref,
                st_ref, b_ref, kf_ref, bc_ref, kc_ref, *, scale):
    C, H = GLA_CHUNK, GLA_HEADS
    nch = q_ref.shape[0] // C

    @pl.when(pl.program_id(1) == 0)
    def _init():
        st_ref[...] = jnp.zeros(st_ref.shape, F32)

    whi, wlo = _hilo(w2_ref[...])
    part = tril_ref.shape[0]
    decay = None
    for r0 in range(0, nch * C, part):
        rs = pl.ds(r0, part)
        ghi, glo = _hilo(glr_ref[rs, :])
        pre = _dot(ghi, whi) + _dot(glo, whi) + _dot(ghi, wlo) + gb_ref[...]
        la = (jnp.minimum(pre, 0.0) - jnp.log(1.0 + jnp.exp(-jnp.abs(pre)))) * (1.0 / GLA_TAU)
        lhi, llo = _hilo(la)
        b_part = _dot(tril_ref[...], lhi) + _dot(tril_ref[...], llo)
        b_ref[rs, :] = b_part
        for c in range(part // C):
            d_c = -b_part[c * C + C - 1:(c + 1) * C, :]
            decay = d_c if decay is None else jnp.maximum(decay, d_c)
    kf_ref[...] = k_ref[...].astype(F32)
    bounded = jnp.max(decay) <= GLA_FACTOR_MAX_DECAY

    @pl.when(bounded)
    def _factored():
        for c in range(nch):
            for h in range(H):
                _gla_head_bounded(c * C, h, q_ref, v_ref, o_ref, st_ref, b_ref, kf_ref, scale)

    @pl.when(jnp.logical_not(bounded))
    def _exact():
        def chunk(c, carry):
            row0 = pl.multiple_of(c * C, C)
            bc_ref[...] = b_ref[pl.ds(row0, C), :]
            kc_ref[...] = kf_ref[pl.ds(row0, C), :]
            for h in range(H):
                _gla_head_exact(row0, h, q_ref, v_ref, o_ref, st_ref, bc_ref, kc_ref, scale)
            return carry
        lax.fori_loop(0, nch, chunk, 0)


def _gla(q, k, v, glr, w2p, gb, tril, *, B, S):
    C, H, dk, dv = GLA_CHUNK, GLA_HEADS, GLA_DK, GLA_DV
    rows = C * min(GLA_STEP_CHUNKS, S // C)
    nchunk = S // rows
    tok = lambda n: pl.BlockSpec((rows, n), lambda b, c: (b * nchunk + c, 0))
    const = lambda shape: pl.BlockSpec(shape, lambda b, c: (0,) * len(shape))
    return pl.pallas_call(
        functools.partial(_gla_kernel, scale=float(dk) ** -0.5),
        grid=(B, nchunk),
        in_specs=[tok(H * dk), tok(H * dk), tok(H * dv), tok(LANES),
                  const(w2p.shape), const(gb.shape), const(tril.shape)],
        out_specs=tok(H * dv),
        out_shape=jax.ShapeDtypeStruct((B * S, H * dv), BF16),
        scratch_shapes=[pltpu.VMEM((H, dv, dk), F32),
                        pltpu.VMEM((rows, H * dk), F32),
                        pltpu.VMEM((rows, H * dk), F32),
                        pltpu.VMEM((C, H * dk), F32),
                        pltpu.VMEM((C, H * dk), F32)],
        compiler_params=pltpu.CompilerParams(dimension_semantics=("parallel", "arbitrary"),
                                             vmem_limit_bytes=VMEM_LIMIT),
        name="gla",
    )(q, k, v, glr, w2p, gb, tril)


def _odd_out_kernel(x_ref, o_ref_in, z_ref, gn_ref, w_ref, g_ref, b_ref, out_ref, *, alpha):
    H, dv = GLA_HEADS, GLA_DV
    parts = []
    for h in range(H):
        sl = pl.ds(h * dv, dv)
        oh = o_ref_in[:, sl].astype(F32)
        r = lax.rsqrt(jnp.mean(oh * oh, axis=-1, keepdims=True) + LN_EPS)
        z = z_ref[:, sl].astype(F32)
        parts.append((oh * r * gn_ref[:, sl] * (z * _sigmoid(z))).astype(BF16))
    y = _dot(jnp.concatenate(parts, axis=1), w_ref[...])
    out_ref[...] = _layer_norm(alpha * x_ref[...] + y, g_ref[...], b_ref[...])


def _odd_out(x2d, o, z, gn_g, w_out, ln_g, ln_b, alpha, tm=1024):
    T, D = x2d.shape
    W = o.shape[1]
    row = lambda n: pl.BlockSpec((tm, n), lambda i: (i, 0))
    return pl.pallas_call(
        functools.partial(_odd_out_kernel, alpha=alpha),
        grid=(T // tm,),
        in_specs=[row(D), row(W), row(W),
                  pl.BlockSpec((1, W), lambda i: (0, 0)),
                  pl.BlockSpec((W, D), lambda i: (0, 0)),
                  pl.BlockSpec((1, D), lambda i: (0, 0)),
                  pl.BlockSpec((1, D), lambda i: (0, 0))],
        out_specs=row(D),
        out_shape=jax.ShapeDtypeStruct((T, D), F32),
        compiler_params=pltpu.CompilerParams(dimension_semantics=("parallel",),
                                             vmem_limit_bytes=VMEM_LIMIT),
        name="odd_out",
    )(x2d, o, z, gn_g, w_out, ln_g, ln_b)


def _key_features(S, block):
    s = np.arange(S)
    f = np.zeros((S, LANES), np.float32)
    f[s, s // block] = 1.0
    f[:, FEAT + 0] = 1.0
    f[:, FEAT + 1] = 1.0
    f[:, FEAT + 2] = FEAT * (s // FEAT)
    f[:, FEAT + 3] = s % FEAT
    return jnp.asarray(f, BF16)


def _pad_to(a, n, axis):
    pad = [(0, 0)] * a.ndim
    pad[axis] = (0, n - a.shape[axis])
    return jnp.pad(a, pad)


def _even_layer(x2d, B, S, w_in, cmp_pos, cmp_w1, cmp_b1, cmp_w2, w_out, ln_g, ln_b, alpha):
    T, D = x2d.shape
    G, hd = NSA_KV_GROUPS, HEAD_DIM
    scale = hd ** -0.5
    wq, wkv, wgate, wnz, wmq, wmk, wmv, wmz = jnp.split(w_in, np.cumsum(EVEN_SPLITS)[:-1].tolist(), axis=1)
    wkc, wvc, wks, wvs, wkw, wvw = jnp.split(wkv, 6, axis=1)
    w_tok = jnp.concatenate([wkc, wvc, wks, wkw, wmk], axis=1).astype(BF16)
    w_tr = jnp.concatenate([wq * scale, wvs, wvw, wmq * scale, wmv, wnz, wmz,
                            _pad_to(wgate, GATE_ROWS, 1)], axis=1).T.astype(BF16)
    (kcmp, vcmp, ksel, kwin, mk, nq_t, vsel_t, vwin_t, mq_t, mv_t, nz_t, mz_t, gate_t) = _proj(
        x2d, w_tok, (NSA_KV_W,) * 4 + (MOBA_W,), (F32, F32, BF16, BF16, BF16),
        w_tr, (NSA_W, NSA_KV_W, NSA_KV_W, MOBA_W, MOBA_W, NSA_W, MOBA_W, GATE_ROWS),
        (BF16,) * 7 + (F32,))

    L, d, HID = NSA_CMP_BLOCK, NSA_CMP_STRIDE, NSA_CMP_HIDDEN
    ncmp = (S - L) // d + 1
    NC = S // d
    eye = jnp.eye(G, dtype=F32)
    w1r = cmp_w1.reshape(2, 2, d, hd, HID)
    w1p = jnp.einsum('ialdj,gh->ialgdhj', w1r, eye).reshape(2, 2, d * G * hd, G * HID).astype(BF16)
    posp = jnp.broadcast_to(cmp_pos.reshape(2, 2, d, 1, hd), (2, 2, d, G, hd)).reshape(2, 2, 1, d * G * hd)
    b1p = jnp.tile(cmp_b1, (1, G)).reshape(2, 1, G * HID)
    w2p = jnp.einsum('ijd,gh->igjhd', _pad_to(cmp_w2, LANES, 2), eye).reshape(2, G * HID, G * LANES).astype(BF16)
    cfeat = np.zeros((NC, G, LANES), np.float32)
    cfeat[:, :, hd + 0] = 1.0
    cfeat[:, :, hd + 1] = 1.0
    cfeat[:, :, hd + 2] = (np.arange(NC) * d)[:, None]
    cfeat[:, :, hd + 3] = L - 1
    kvc = _cmp_mlp(kcmp, vcmp, w1p, posp, b1p, w2p, jnp.asarray(cfeat.reshape(NC, G * LANES)), ncmp,
                   B=B, S=S)

    ns = S // NSA_SEL_BLOCK
    cmp_start = np.arange(NC) * d
    sel_start = np.arange(FEAT) * NSA_SEL_BLOCK
    ovt = ((cmp_start[None, :] <= sel_start[:, None] + NSA_SEL_BLOCK - 1)
           & (cmp_start[None, :] + L - 1 >= sel_start[:, None])
           & (np.arange(NC)[None, :] < ncmp) & (np.arange(FEAT)[:, None] < ns))
    o_cmp, bias_sel = _cmp_attn(nq_t, kvc, jnp.asarray(ovt, BF16), B=B, S=S)

    feats_sel = _key_features(S, NSA_SEL_BLOCK)
    o_sel = _flash(nq_t, bias_sel, ksel, feats_sel, vsel_t,
                   B=B, S=S, units=G, tq=TQ, nh=NSA_HG, pair=False, window=None)
    o_win = _flash(nq_t, None, kwin, feats_sel, vwin_t,
                   B=B, S=S, units=G, tq=TQ, nh=NSA_HG, pair=False, window=NSA_WINDOW)

    avg = np.zeros((MOBA_NBP, S), np.float32)
    avg[np.arange(S) // MOBA_BLOCK, np.arange(S)] = 1.0 / MOBA_BLOCK
    bias_m = _moba_gate(mq_t, mk, jnp.asarray(avg, BF16), B=B, S=S)
    o_m = _flash(mq_t, bias_m, mk, _key_features(S, MOBA_BLOCK), mv_t,
                 B=B, S=S, units=MOBA_HEADS // MOBA_STEP_HEADS, tq=TQM, nh=MOBA_STEP_HEADS, pair=True, window=None)

    return _even_out(x2d, o_cmp, o_sel, o_win, o_m, gate_t, nz_t, mz_t, w_out.astype(BF16),
                     ln_g.reshape(1, D), ln_b.reshape(1, D), alpha, B=B, S=S)


def _odd_layer(x2d, B, S, w_in, gate_w2, gate_b, gn_g, w_out, ln_g, ln_b, alpha):
    T, D = x2d.shape
    H, dk, dv, C = GLA_HEADS, GLA_DK, GLA_DV, GLA_CHUNK
    wq, wk, wv, wg, wz = jnp.split(w_in, np.cumsum(ODD_SPLITS)[:-1].tolist(), axis=1)
    w_perm = jnp.concatenate([wq, wk, wv, wz, _pad_to(wg, LANES, 1)], axis=1).astype(BF16)
    q, k, v, z, glr = _proj(x2d, w_perm, (H * dk, H * dk, H * dv, H * dv, LANES),
                            (BF16, BF16, BF16, BF16, F32))
    w2p = _pad_to(gate_w2, LANES, 0)
    ncs = min(GLA_CUMSUM_CHUNKS, GLA_STEP_CHUNKS, S // C)
    tril = jnp.asarray(np.kron(np.eye(ncs), np.tril(np.ones((C, C)))), BF16)
    o = _gla(q, k, v, glr, w2p, gate_b.reshape(1, H * dk), tril, B=B, S=S)
    return _odd_out(x2d, o, z, gn_g.reshape(1, H * dv), w_out.astype(BF16),
                    ln_g.reshape(1, D), ln_b.reshape(1, D), alpha)


def kernel(x, ev_w_in, ev_cmp_pos, ev_cmp_w1, ev_cmp_b1, ev_cmp_w2, ev_w_out, ev_ln_g, ev_ln_b,
           od_w_in, od_gate_w2, od_gate_b, od_gn_g, od_w_out, od_ln_g, od_ln_b):
    B, S, D = x.shape
    depth = ev_w_in.shape[0] + od_w_in.shape[0]
    alpha = (2.0 * depth) ** 0.25
    h = x.reshape(B * S, D)
    for layer in range(depth):
        i = layer // 2
        if layer % 2 == 0:
            h = _even_layer(h, B, S, ev_w_in[i], ev_cmp_pos[i], ev_cmp_w1[i], ev_cmp_b1[i], ev_cmp_w2[i],
                            ev_w_out[i], ev_ln_g[i], ev_ln_b[i], alpha)
        else:
            h = _odd_layer(h, B, S, od_w_in[i], od_gate_w2[i], od_gate_b[i], od_gn_g[i],
                           od_w_out[i], od_ln_g[i], od_ln_b[i], alpha)
    return h.reshape(B, S, D)
```

```python
import functools

import numpy as np
import jax
import jax.numpy as jnp
from jax import lax
from jax.experimental import pallas as pl
from jax.experimental.pallas import tpu as pltpu

F32 = jnp.float32
BF16 = jnp.bfloat16

HEAD_DIM = 64
NSA_HEADS = 8
NSA_KV_GROUPS = 2
NSA_HG = NSA_HEADS // NSA_KV_GROUPS
NSA_CMP_BLOCK = 32
NSA_CMP_STRIDE = 16
NSA_CMP_HIDDEN = 128
NSA_SEL_BLOCK = 64
NSA_TOP_N = 16
NSA_WINDOW = 512
MOBA_HEADS = 8
MOBA_BLOCK = 256
MOBA_TOP_K = 3
GLA_HEADS = 4
GLA_DK = 128
GLA_DV = 256
GLA_GATE_RANK = 16
GLA_TAU = 16.0
GLA_CHUNK = 64
GLA_SUB = 16
GLA_STEP_CHUNKS = 16
GLA_CUMSUM_CHUNKS = 4
GLA_FACTOR_MAX_DECAY = 60.0
LN_EPS = 1e-5
NEG = -1e30
FORCE_BONUS = 1e4

NSA_W = NSA_HEADS * HEAD_DIM
NSA_KV_W = NSA_KV_GROUPS * HEAD_DIM
MOBA_W = MOBA_HEADS * HEAD_DIM
EVEN_SPLITS = (NSA_W, 6 * NSA_KV_W, 3 * NSA_HEADS, NSA_W, MOBA_W, MOBA_W, MOBA_W, MOBA_W)
ODD_SPLITS = (GLA_HEADS * GLA_DK, GLA_HEADS * GLA_DK, GLA_HEADS * GLA_DV, GLA_GATE_RANK, GLA_HEADS * GLA_DV)

LANES = 128
SUBLANES = 8
FEAT = 64
ALIBI_ROWS = 16
GATE_ROWS = 32
MOBA_NBP = 16
TQ = 512
TQM = 512
MOBA_STEP_HEADS = 4
TK = 512
CG = 256
MIN_WEIGHT_SUM = 1e-25
VROWS = HEAD_DIM + 16
VMEM_LIMIT = 48 * 1024 * 1024


def _dot(a, b):
    return jnp.dot(a, b, preferred_element_type=F32)


def _dot_nt(a, b):
    return lax.dot_general(a, b, (((1,), (1,)), ((), ())), preferred_element_type=F32)


def _dot_tn(a, b):
    return lax.dot_general(a, b, (((0,), (0,)), ((), ())), preferred_element_type=F32)


def _hilo(a):
    hi = a.astype(BF16)
    lo = (a - hi.astype(F32)).astype(BF16)
    return hi, lo


def _pow2_neg(n):
    return lax.bitcast_convert_type(lax.shift_left(127 - n, 23), F32)


def _sigmoid(x):
    return 0.5 * jnp.tanh(0.5 * x) + 0.5


def _layer_norm(r, g, b):
    mu = jnp.mean(r, axis=-1, keepdims=True)
    d = r - mu
    var = jnp.mean(d * d, axis=-1, keepdims=True)
    return d * lax.rsqrt(var + LN_EPS) * g + b


def _alibi_query_rows(t0, tq, head0, R):
    r = lax.broadcasted_iota(jnp.int32, (ALIBI_ROWS, R), 0)
    col = lax.broadcasted_iota(jnp.int32, (ALIBI_ROWS, R), 1)
    t = t0 + col % tq
    slope = _pow2_neg(head0 + col // tq + 1)
    feats = jnp.where(r == 0, -FEAT * slope * (t // FEAT).astype(F32),
                      jnp.where(r == 1, -slope * (t % FEAT).astype(F32), jnp.where(r < 4, slope, 0.0)))
    return jnp.concatenate([feats.astype(BF16), jnp.zeros((FEAT - ALIBI_ROWS, R), BF16)], axis=0)


def _ranks(sc_ref, n_real, n_rows=None):
    width = sc_ref.shape[1]
    n_rows = sc_ref.shape[0] if n_rows is None else n_rows
    out = []
    for v in range(n_rows // SUBLANES):
        lo = v * SUBLANES
        sc = sc_ref[lo:lo + SUBLANES, :]
        idx = lo + lax.broadcasted_iota(jnp.int32, (SUBLANES, width), 0)
        rank = jnp.zeros((SUBLANES, width), jnp.int32)
        for jp in range(n_real):
            r = sc_ref[pl.ds(jp, 1), :]
            if jp < lo:
                beats = r >= sc
            elif jp >= lo + SUBLANES - 1:
                beats = r > sc
            else:
                beats = (r > sc) | ((r == sc) & (idx > jp))
            rank = rank + jnp.where(beats, 1, 0)
        out.append(rank)
    return jnp.concatenate(out, axis=0)


def _proj_kernel(*refs, tok_splits, tr_splits):
    x_ref, wt_ref = refs[0], refs[1]
    outs = list(refs[3:] if tr_splits else refs[2:])
    x = x_ref[...]
    xb = x.astype(BF16)
    off = 0
    for n in tok_splits:
        o_ref = outs.pop(0)
        o_ref[...] = _dot(xb, wt_ref[:, off:off + n]).astype(o_ref.dtype)
        off += n
    if tr_splits:
        wtr_ref = refs[2]
        xt = x.T.astype(BF16)
        off = 0
        for n in tr_splits:
            o_ref = outs.pop(0)
            o_ref[...] = _dot(wtr_ref[off:off + n, :], xt).astype(o_ref.dtype)
            off += n


def _proj(x2d, w_tok, tok_splits, tok_dtypes, w_tr=None, tr_splits=(), tr_dtypes=(), tm=512):
    T, D = x2d.shape
    assert T % tm == 0 and sum(tok_splits) == w_tok.shape[1]
    in_specs = [pl.BlockSpec((tm, D), lambda i: (i, 0)),
                pl.BlockSpec(w_tok.shape, lambda i: (0, 0))]
    args = [x2d, w_tok]
    if tr_splits:
        assert sum(tr_splits) == w_tr.shape[0]
        in_specs.append(pl.BlockSpec(w_tr.shape, lambda i: (0, 0)))
        args.append(w_tr)
    out_specs = ([pl.BlockSpec((tm, n), lambda i: (i, 0)) for n in tok_splits]
                 + [pl.BlockSpec((n, tm), lambda i: (0, i)) for n in tr_splits])
    out_shape = ([jax.ShapeDtypeStruct((T, n), dt) for n, dt in zip(tok_splits, tok_dtypes)]
                 + [jax.ShapeDtypeStruct((n, T), dt) for n, dt in zip(tr_splits, tr_dtypes)])
    return pl.pallas_call(
        functools.partial(_proj_kernel, tok_splits=tuple(tok_splits), tr_splits=tuple(tr_splits)),
        grid=(T // tm,), in_specs=in_specs, out_specs=out_specs, out_shape=out_shape,
        compiler_params=pltpu.CompilerParams(dimension_semantics=("parallel",),
                                             vmem_limit_bytes=VMEM_LIMIT),
        name="proj",
    )(*args)


def _cmp_mlp_kernel(kx_ref, vx_ref, w1_ref, pos_ref, b1_ref, w2_ref, cf_ref, o_ref, *, ncmp):
    d = NSA_CMP_STRIDE
    nc = kx_ref.shape[0] // d
    cw = w1_ref.shape[2]
    for i, x_ref in enumerate((kx_ref, vx_ref)):
        a = bm = None
        for l in range(0, d, 2):
            x2 = jnp.concatenate([x_ref[pl.ds(l, nc, stride=d), :], x_ref[pl.ds(l + 1, nc, stride=d), :]],
                                 axis=1).astype(BF16)
            rows = pl.ds(l * x_ref.shape[1], 2 * x_ref.shape[1])
            a_l, b_l = _dot(x2, w1_ref[i, 0, rows, :]), _dot(x2, w1_ref[i, 1, rows, :])
            a, bm = (a_l, b_l) if a is None else (a + a_l, bm + b_l)
        c = b1_ref[i]
        for half in range(2):
            phi, plo = _hilo(jnp.broadcast_to(pos_ref[i, half], (SUBLANES, cw)))
            c = c + (_dot(phi, w1_ref[i, half]) + _dot(plo, w1_ref[i, half]))[0:1]
        hid = a + pltpu.roll(bm, nc - 1, axis=0) + c
        hid = hid * _sigmoid(hid)
        row = lax.broadcasted_iota(jnp.int32, hid.shape, 0)
        hid = jnp.where(row < ncmp, hid, 0.0)
        out = _dot(hid.astype(BF16), w2_ref[i])
        if i == 0:
            out = out + cf_ref[...]
        o_ref[i] = out.astype(o_ref.dtype)


def _cmp_mlp(kx, vx, w1p, posp, b1p, w2p, cfeat, ncmp, *, B, S):
    NC = S // NSA_CMP_STRIDE
    OW = w2p.shape[-1]
    full = lambda a: pl.BlockSpec(a.shape, lambda b: (0,) * a.ndim)
    return pl.pallas_call(
        functools.partial(_cmp_mlp_kernel, ncmp=ncmp),
        grid=(B,),
        in_specs=[pl.BlockSpec((S, kx.shape[1]), lambda b: (b, 0)),
                  pl.BlockSpec((S, vx.shape[1]), lambda b: (b, 0)),
                  full(w1p), full(posp), full(b1p), full(w2p), full(cfeat)],
        out_specs=pl.BlockSpec((2, None, NC, OW), lambda b: (0, b, 0, 0)),
        out_shape=jax.ShapeDtypeStruct((2, B, NC, OW), BF16),
        compiler_params=pltpu.CompilerParams(dimension_semantics=("parallel",),
                                             vmem_limit_bytes=VMEM_LIMIT),
        name="cmp_mlp",
    )(kx, vx, w1p, posp, b1p, w2p, cfeat)


def _cmp_attn_kernel(q_ref, kc_ref, vc_ref, ovt_ref, o_ref, bias_ref, sc_ref, *, tq, nh, ns, ntop):
    g = pl.program_id(1)
    R = nh * tq
    nq = q_ref.shape[1] // tq

    def q_tile(qi):
        toks = pl.ds(qi * tq, tq)
        nc = min(kc_ref.shape[0], -(-((qi + 1) * tq) // (NSA_CMP_STRIDE * 16)) * 16)
        nsel = min(ns, -(-((qi + 1) * tq) // (NSA_SEL_BLOCK * SUBLANES)) * SUBLANES)
        qcat = jnp.concatenate([q_ref[hh * HEAD_DIM:(hh + 1) * HEAD_DIM, toks] for hh in range(nh)], axis=1)
        qa = jnp.concatenate([qcat, _alibi_query_rows(qi * tq, tq, g * nh, R)], axis=0)
        s = _dot(kc_ref[0:nc, :], qa)
        n = lax.broadcasted_iota(jnp.int32, (nc, R), 0)
        t_row = qi * tq + lax.broadcasted_iota(jnp.int32, (1, R), 1) % tq
        last = (t_row - (NSA_CMP_BLOCK - 1)) // NSA_CMP_STRIDE
        s = jnp.where(n <= last, s, NEG)
        e = jnp.exp(s - jnp.max(s, axis=0, keepdims=True))
        inv = jnp.where(last >= 0, 1.0 / jnp.sum(e, axis=0, keepdims=True), 0.0)
        p = e * inv
        o_ref[:, pl.ds(qi * R, R)] = _dot_tn(vc_ref[0:nc, 0:HEAD_DIM], p.astype(BF16)).astype(o_ref.dtype)

        psum = p[:, 0:tq]
        for hh in range(1, nh):
            psum = psum + p[:, hh * tq:(hh + 1) * tq]
        phi, plo = _hilo(psum)
        ovt = ovt_ref[0:nsel, 0:nc]
        imp = _dot(ovt, phi) + _dot(ovt, plo)

        j = lax.broadcasted_iota(jnp.int32, (nsel, tq), 0)
        cur = (qi * tq + lax.broadcasted_iota(jnp.int32, (nsel, tq), 1)) // NSA_SEL_BLOCK
        if nsel <= ntop:
            sel = j <= cur
        else:
            forced = jnp.where((j == 0) | (j == cur) | (j == cur - 1), FORCE_BONUS, 0.0)
            sc_ref[0:nsel, :] = jnp.where(j <= cur, imp + forced, NEG)
            sel = (_ranks(sc_ref, nsel, nsel) < ntop) & (j <= cur)
        bias_ref[0:nsel, toks] = jnp.where(sel, 0.0, NEG).astype(bias_ref.dtype)
        if nsel < FEAT:
            bias_ref[nsel:FEAT, toks] = jnp.full((FEAT - nsel, tq), NEG, bias_ref.dtype)

    for qi in range(nq):
        q_tile(qi)


def _cmp_attn(nq_t, kvc, ovt, *, B, S):
    tq, nh, G = TQ, NSA_HG, NSA_KV_GROUPS
    R = nh * tq
    nq = S // tq
    NC = kvc.shape[2]
    ns = S // NSA_SEL_BLOCK
    assert ns <= FEAT
    ntop = min(NSA_TOP_N, ns)
    return pl.pallas_call(
        functools.partial(_cmp_attn_kernel, tq=tq, nh=nh, ns=ns, ntop=ntop),
        grid=(B, G),
        in_specs=[pl.BlockSpec((nh * HEAD_DIM, S), lambda b, g: (g, b)),
                  pl.BlockSpec((None, None, NC, LANES), lambda b, g: (0, b, 0, g)),
                  pl.BlockSpec((None, None, NC, LANES), lambda b, g: (1, b, 0, g)),
                  pl.BlockSpec(ovt.shape, lambda b, g: (0, 0))],
        out_specs=[pl.BlockSpec((None, None, HEAD_DIM, S * nh), lambda b, g: (b, g, 0, 0)),
                   pl.BlockSpec((None, None, FEAT, S), lambda b, g: (b, g, 0, 0))],
        out_shape=[jax.ShapeDtypeStruct((B, G, HEAD_DIM, S * nh), BF16),
                   jax.ShapeDtypeStruct((B, G, FEAT, S), BF16)],
        scratch_shapes=[pltpu.VMEM((FEAT, tq), F32)],
        compiler_params=pltpu.CompilerParams(dimension_semantics=("parallel", "parallel"),
                                             vmem_limit_bytes=VMEM_LIMIT),
        name="cmp_attn",
    )(nq_t, kvc, kvc, ovt)


def _moba_gate_kernel(q_ref, k_ref, a_ref, bias_ref, km_ref, sc_ref, *, tq, nb, ntop):
    h = pl.program_id(1)
    qi = pl.program_id(2)

    @pl.when(qi == 0)
    def _means():
        km_ref[...] = _dot(a_ref[...], k_ref[...])

    nbp = km_ref.shape[0]
    lane = lax.broadcasted_iota(jnp.int32, (nbp, LANES), 1)
    km = jnp.where(lane // HEAD_DIM == h % 2, km_ref[...], 0.0)
    khi, klo = _hilo(km)
    q = q_ref[...]
    qq = jnp.concatenate([q, q], axis=0)
    g_t = _dot(khi, qq) + _dot(klo, qq)
    n = lax.broadcasted_iota(jnp.int32, (nbp, tq), 0)
    cur = (qi * tq + lax.broadcasted_iota(jnp.int32, (nbp, tq), 1)) // MOBA_BLOCK
    sc = jnp.where(n < cur, g_t, NEG)
    sc_ref[...] = sc
    sel = ((_ranks(sc_ref, nb) < ntop) & (n < cur)) | (n == cur)
    bias_ref[0:nbp, :] = jnp.where(sel, 0.0, NEG).astype(bias_ref.dtype)
    bias_ref[nbp:FEAT, :] = jnp.zeros((FEAT - nbp, tq), bias_ref.dtype)


def _moba_gate(mq_t, mk, avg, *, B, S):
    tq, H = S, MOBA_HEADS
    nq = S // tq
    nb = S // MOBA_BLOCK
    nbp = avg.shape[0]
    assert nb <= nbp
    ntop = min(MOBA_TOP_K, nb)
    return pl.pallas_call(
        functools.partial(_moba_gate_kernel, tq=tq, nb=nb, ntop=ntop),
        grid=(B, H, nq),
        in_specs=[pl.BlockSpec((HEAD_DIM, tq), lambda b, h, i: (h, b * nq + i)),
                  pl.BlockSpec((S, LANES), lambda b, h, i: (b, h // 2)),
                  pl.BlockSpec(avg.shape, lambda b, h, i: (0, 0))],
        out_specs=pl.BlockSpec((None, None, FEAT, tq), lambda b, h, i: (b, h, 0, i)),
        out_shape=jax.ShapeDtypeStruct((B, H, FEAT, S), BF16),
        scratch_shapes=[pltpu.VMEM((nbp, LANES), F32), pltpu.VMEM((nbp, tq), F32)],
        compiler_params=pltpu.CompilerParams(dimension_semantics=("parallel", "parallel", "arbitrary"),
                                             vmem_limit_bytes=VMEM_LIMIT),
        name="moba_gate",
    )(mq_t, mk, avg)


def _flash_q_tile(qi, running_max, q_ref, bias_ref, mask_ref, o_ref, ka_ref, vta_ref, kn_ref, qa_ref, bnd_ref,
                  m_ref, acc_ref, alpha_ref, p_ref, *, tq, tk, nh, pair, window, has_past=None):
    u = pl.program_id(1)
    R = nh * tq
    static = isinstance(qi, int)
    q0 = qi * tq if static else pl.multiple_of(qi * tq, tq)
    toks = pl.ds(q0, tq)

    qcat = jnp.concatenate([q_ref[hh * HEAD_DIM:(hh + 1) * HEAD_DIM, toks] for hh in range(nh)], axis=1)
    qf = qcat.astype(F32)
    qq = jnp.concatenate([qf, qf], axis=0)
    rowi = lax.broadcasted_iota(jnp.int32, (2 * HEAD_DIM, R), 0)
    coli = lax.broadcasted_iota(jnp.int32, (2 * HEAD_DIM, R), 1)
    half = (coli // tq) % 2 if pair else u % 2
    qa_ref[0:2 * HEAD_DIM, :] = jnp.where(rowi // HEAD_DIM == half, qq, 0.0).astype(BF16)
    if bias_ref is None:
        qa_ref[2 * HEAD_DIM:2 * HEAD_DIM + FEAT, :] = jnp.zeros((FEAT, R), BF16)
    else:
        qa_ref[2 * HEAD_DIM:2 * HEAD_DIM + FEAT, :] = jnp.concatenate(
            [bias_ref[hh if pair else 0, :, toks] for hh in range(nh)], axis=1)
    qa_ref[2 * HEAD_DIM + FEAT:2 * LANES, :] = _alibi_query_rows(q0, tq, u * nh, R)
    if pair:
        kn = jnp.concatenate([kn_ref[hh:hh + 1, :] for hh in range(nh) for _ in range(tq // LANES)], axis=1)
    else:
        kn = jnp.concatenate([jnp.where(u % 2 == 0, kn_ref[0:1, :], kn_ref[1:2, :])] * (R // LANES), axis=1)
    qn = jnp.sqrt(jnp.sum(qf * qf, axis=0, keepdims=True))
    bnd_ref[...] = jnp.broadcast_to(qn * kn, bnd_ref.shape)

    ncg = R // CG
    kd = (q0 + tq - 1) // tk

    def key_rows(kind, c):
        off = (c * CG) % tq
        if kind is None:
            return 0, tk, None
        if kind == "diag":
            return 0, off + CG, off // CG
        return off, tk, (tk + off) // CG

    def values(ki, kind=None):
        for c in range(ncg):
            cols = pl.ds(c * CG, CG)
            r0, r1, _ = key_rows(kind, c)
            v = (c * CG) // tq if pair else 0
            pv = _dot(vta_ref[v, ki, :, r0:r1], p_ref[r0:r1, cols])
            if running_max:
                acc_ref[:, cols] = alpha_ref[0:1, cols] * acc_ref[:, cols] + pv
            else:
                acc_ref[:, cols] = acc_ref[:, cols] + pv

    def scores(ki, kind=None, prev=None, prev_kind=None):
        k0 = ki * tk if isinstance(ki, int) else pl.multiple_of(ki * tk, tk)
        s_groups = []
        for c in range(ncg):
            r0, r1, _ = key_rows(kind, c)
            kb = (c * CG) // (2 * tq) if pair else 0
            s_groups.append(_dot(ka_ref[kb, pl.ds(k0 + r0, r1 - r0), :], qa_ref[:, pl.ds(c * CG, CG)]))
        if prev is not None:
            values(prev, prev_kind)
        for c, s in enumerate(s_groups):
            cols = pl.ds(c * CG, CG)
            r0, r1, entry = key_rows(kind, c)
            if entry is not None:
                s = s + mask_ref[entry, r0:r1, :]
            if running_max:
                m_prev = m_ref[0:1, cols]
                m_new = jnp.maximum(m_prev, jnp.max(s, axis=0, keepdims=True))
                alpha_ref[:, cols] = jnp.broadcast_to(jnp.exp(m_prev - m_new), (alpha_ref.shape[0], CG))
                m_ref[:, cols] = jnp.broadcast_to(m_new, (m_ref.shape[0], CG))
                p_ref[r0:r1, cols] = jnp.exp(s - m_new).astype(BF16)
            else:
                p_ref[r0:r1, cols] = jnp.exp(s - bnd_ref[0:1, cols]).astype(BF16)

    def steady(lo, hi):
        def pair_body(j, carry):
            ki = lo + 2 * j
            scores(ki, prev=ki - 1)
            scores(ki + 1, prev=ki)
            return carry
        lax.fori_loop(0, (hi - lo) // 2, pair_body, 0)

        @pl.when((hi - lo) % 2 == 1)
        def _odd_one():
            scores(hi - 1, prev=hi - 2)

    def only_diagonal():
        scores(0, "diag")
        values(0, "diag")

    def past_then_diagonal():
        if window is not None:
            scores(kd - 1, "wedge")
            scores(kd, "diag", prev=kd - 1, prev_kind="wedge")
        elif static:
            scores(0)
            for ki in range(1, kd):
                scores(ki, prev=ki - 1)
            scores(kd, "diag", prev=kd - 1)
        else:
            scores(0)
            steady(1, kd)
            scores(kd, "diag", prev=kd - 1)
        values(kd, "diag")

    acc_ref[...] = jnp.zeros(acc_ref.shape, F32)
    if running_max:
        m_ref[...] = jnp.full(m_ref.shape, NEG, F32)
    if static:
        only_diagonal() if kd == 0 else past_then_diagonal()
    elif has_past:
        past_then_diagonal()
    else:
        pl.when(kd == 0)(only_diagonal)
        pl.when(kd > 0)(past_then_diagonal)

    acc = acc_ref[...]
    out = (acc[0:HEAD_DIM] / acc[HEAD_DIM:HEAD_DIM + 1]).astype(o_ref.dtype)
    if pair:
        for hh in range(nh):
            o_ref[hh, :, toks] = out[:, hh * tq:(hh + 1) * tq]
    else:
        o_ref[0, :, pl.ds(qi * R if static else pl.multiple_of(qi * R, R), R)] = out
    return jnp.min(acc[HEAD_DIM:HEAD_DIM + 1, :])


def _flash_kernel(*refs, tq, tk, nh, pair, window, has_bias):
    q_ref = refs[0]
    bias_ref = refs[1] if has_bias else None
    (k_ref, kf_ref, vt_ref, mask_ref, o_ref, ka_ref, vta_ref, kn_ref, qa_ref, bnd_ref, m_ref, acc_ref,
     alpha_ref, p_ref, low_ref) = refs[2 if has_bias else 1:]
    S = k_ref.shape[0]
    for kb in range(ka_ref.shape[0]):
        k = k_ref[:, kb * LANES:(kb + 1) * LANES]
        ka_ref[kb, :, 0:LANES] = k
        ka_ref[kb, :, LANES:2 * LANES] = kf_ref[...]
        k2 = jnp.square(k.astype(F32))
        lane = lax.broadcasted_iota(jnp.int32, k2.shape, 1)
        for hf in range(2):
            n2 = jnp.sum(jnp.where(lane // HEAD_DIM == hf, k2, 0.0), axis=1, keepdims=True)
            kn_ref[2 * kb + hf:2 * kb + hf + 1, :] = jnp.broadcast_to(
                jnp.sqrt(jnp.max(n2, axis=0, keepdims=True)), (1, LANES))
    ones_row = jnp.where(lax.broadcasted_iota(jnp.int32, (VROWS - HEAD_DIM, tk), 0) == 0, 1.0, 0.0)
    for v in range(nh if pair else 1):
        for jt in range(S // tk):
            vta_ref[v, jt, 0:HEAD_DIM, :] = vt_ref[v * HEAD_DIM:(v + 1) * HEAD_DIM, jt * tk:(jt + 1) * tk]
            vta_ref[v, jt, HEAD_DIM:VROWS, :] = ones_row.astype(BF16)

    tile = functools.partial(_flash_q_tile, q_ref=q_ref, bias_ref=bias_ref, mask_ref=mask_ref, o_ref=o_ref,
                             ka_ref=ka_ref, vta_ref=vta_ref, kn_ref=kn_ref, qa_ref=qa_ref, bnd_ref=bnd_ref,
                             m_ref=m_ref, acc_ref=acc_ref, alpha_ref=alpha_ref, p_ref=p_ref,
                             tq=tq, tk=tk, nh=nh, pair=pair, window=window)
    nq = S // tq
    low_ref[0] = tile(0, False)

    def bound_stabilised(qi, carry):
        low_ref[qi] = tile(qi, False, has_past=True)
        return carry
    lax.fori_loop(1, nq, bound_stabilised, 0)

    def redo(qi, carry):
        @pl.when(low_ref[qi] < MIN_WEIGHT_SUM)
        def _running_max():
            tile(qi, True)
        return carry

    lax.fori_loop(0, nq, redo, 0)


def _mask_table(tk, window):
    n_delta = (tk + (window or 0)) // CG
    d = np.arange(n_delta)[:, None, None] * CG + np.arange(CG)[None, None, :] - np.arange(tk)[None, :, None]
    ok = (d >= 0) & ((d < window) if window else True)
    return jnp.asarray(np.where(ok, 0.0, NEG), F32)


def _flash(q_t, bias_t, k, kfeat, v_t, *, B, S, units, tq, nh, pair, window):
    tk = TK
    R = nh * tq
    nq = S // tq
    nv = nh if pair else 1
    nb = nh if pair else 1
    nkb = nh // 2 if pair else 1
    assert tq == tk and S % tk == 0 and tk % CG == 0 and window in (None, tk)
    mask = _mask_table(tk, window)
    kcol = (lambda u: u) if pair else (lambda u: u // 2)
    out_w = tq if pair else R
    out_shape = (B, units * nb, HEAD_DIM, S if pair else S * nh)
    has_bias = bias_t is not None
    in_specs = [pl.BlockSpec((nh * HEAD_DIM, S), lambda b, u: (u, b))]
    if has_bias:
        in_specs.append(pl.BlockSpec((None, nb, FEAT, S), lambda b, u: (b, u, 0, 0)))
    in_specs += [pl.BlockSpec((S, nkb * LANES), lambda b, u: (b, kcol(u))),
                 pl.BlockSpec((S, LANES), lambda b, u: (0, 0)),
                 pl.BlockSpec((nv * HEAD_DIM, S), lambda b, u: (u, b)),
                 pl.BlockSpec(mask.shape, lambda b, u: (0, 0, 0))]
    args = (q_t,) + ((bias_t,) if has_bias else ()) + (k, kfeat, v_t, mask)
    return pl.pallas_call(
        functools.partial(_flash_kernel, tq=tq, tk=tk, nh=nh, pair=pair, window=window, has_bias=has_bias),
        grid=(B, units),
        in_specs=in_specs,
        out_specs=pl.BlockSpec((None, nb, HEAD_DIM, out_w * nq), lambda b, u: (b, u, 0, 0)),
        out_shape=jax.ShapeDtypeStruct(out_shape, BF16),
        scratch_shapes=[pltpu.VMEM((nkb, S, 2 * LANES), BF16),
                        pltpu.VMEM((nv, S // tk, VROWS, tk), BF16),
                        pltpu.VMEM((SUBLANES, LANES), F32),
                        pltpu.VMEM((2 * LANES, R), BF16),
                        pltpu.VMEM((SUBLANES, R), F32),
                        pltpu.VMEM((SUBLANES, R), F32),
                        pltpu.VMEM((VROWS, R), F32),
                        pltpu.VMEM((SUBLANES, R), F32),
                        pltpu.VMEM((tk, R), BF16),
                        pltpu.SMEM((nq,), F32)],
        compiler_params=pltpu.CompilerParams(dimension_semantics=("parallel", "parallel"),
                                             vmem_limit_bytes=VMEM_LIMIT),
        name="flash",
    )(*args)


def _even_out_kernel(x_ref, ocmp_ref, osel_ref, owin_ref, om_ref, gate_ref, nz_ref, mz_ref,
                     w_ref, g_ref, b_ref, o_ref, sig_ref, *, alpha, tm):
    G, nh = NSA_KV_GROUPS, NSA_HG
    sig_ref[...] = _sigmoid(gate_ref[...])
    heads = []
    for g in range(G):
        for hh in range(nh):
            acc = None
            for br, ref in enumerate((ocmp_ref, osel_ref, owin_ref)):
                gate = sig_ref[pl.ds(3 * (g * nh + hh) + br, 1), :]
                term = gate * ref[g, :, hh * tm:(hh + 1) * tm].astype(F32)
                acc = term if acc is None else acc + term
            heads.append(acc)
    o_nsa = jnp.concatenate(heads, axis=0)
    nz = nz_ref[...].astype(F32)
    mz = mz_ref[...].astype(F32)
    o_m = jnp.concatenate([om_ref[h] for h in range(MOBA_HEADS)], axis=0).astype(F32)
    a_t = jnp.concatenate([o_nsa * (nz * _sigmoid(nz)), o_m * (mz * _sigmoid(mz))], axis=0)
    y = _dot_tn(a_t.astype(BF16), w_ref[...])
    o_ref[...] = _layer_norm(alpha * x_ref[...] + y, g_ref[...], b_ref[...])


def _even_out(x2d, ocmp, osel, owin, om, gate_t, nz_t, mz_t, w_out, ln_g, ln_b, alpha, *, B, S):
    T, D = x2d.shape
    tm = TQ
    nq = S // tm
    G, nh, H = NSA_KV_GROUPS, NSA_HG, MOBA_HEADS
    nsa = pl.BlockSpec((None, G, HEAD_DIM, nh * tm), lambda i: (i // nq, 0, 0, i % nq))
    feat = lambda n: pl.BlockSpec((n, tm), lambda i: (0, i))
    full = lambda a: pl.BlockSpec(a.shape, lambda i: (0,) * a.ndim)
    return pl.pallas_call(
        functools.partial(_even_out_kernel, alpha=alpha, tm=tm),
        grid=(T // tm,),
        in_specs=[pl.BlockSpec((tm, D), lambda i: (i, 0)), nsa, nsa, nsa,
                  pl.BlockSpec((None, H, HEAD_DIM, tm), lambda i: (i // nq, 0, 0, i % nq)),
                  feat(GATE_ROWS), feat(NSA_W), feat(MOBA_W),
                  full(w_out), full(ln_g), full(ln_b)],
        out_specs=pl.BlockSpec((tm, D), lambda i: (i, 0)),
        out_shape=jax.ShapeDtypeStruct((T, D), F32),
        scratch_shapes=[pltpu.VMEM((GATE_ROWS, tm), F32)],
        compiler_params=pltpu.CompilerParams(dimension_semantics=("parallel",),
                                             vmem_limit_bytes=VMEM_LIMIT),
        name="even_out",
    )(x2d, ocmp, osel, owin, om, gate_t, nz_t, mz_t, w_out, ln_g, ln_b)


def _gla_head_exact(row0, h, q_ref, v_ref, o_ref, st_ref, b_ref, kf_ref, scale):
    C, dk, dv, SUB = GLA_CHUNK, GLA_DK, GLA_DV, GLA_SUB
    rows = lax.broadcasted_iota(jnp.int32, (C, dk), 0)
    sub_i = lax.broadcasted_iota(jnp.int32, (SUB, LANES), 0)
    lane = lax.broadcasted_iota(jnp.int32, (SUB, LANES), 1)
    ones = jnp.ones((dk, LANES), BF16)
    sl = pl.ds(h * dk, dk)
    rs_c = pl.ds(row0, C)
    bh = b_ref[:, sl]
    kf = kf_ref[:, sl]
    qf = q_ref[rs_c, sl].astype(F32) * scale
    vh = v_ref[rs_c, pl.ds(h * dv, dv)]
    st = st_ref[h]
    inter = _dot_nt((qf * jnp.exp(bh)).astype(BF16), st.astype(BF16))
    bl = b_ref[pl.ds(C - 1, 1), sl]
    kd = kf * jnp.exp(bl - bh)

    def off_block(r0, r1, c0, c1):
        ref_row = b_ref[pl.ds(r0, 1), sl]
        qt = qf[r0:r1] * jnp.exp(bh[r0:r1] - ref_row)
        ek = jnp.where((rows >= c0) & (rows < c1), ref_row - bh, NEG)
        kt = kf * jnp.exp(ek)
        return _dot_nt(qt.astype(BF16), kt.astype(BF16))

    lower = off_block(2 * SUB, C, 0, 2 * SUB)
    blocks = [None, off_block(SUB, 2 * SUB, 0, SUB), lower[0:SUB],
              lower[SUB:2 * SUB] + off_block(3 * SUB, C, 2 * SUB, 3 * SUB)]
    pieces = []
    for blk in range(C // SUB):
        r0 = blk * SUB
        q_i = qf[r0:r0 + SUB]
        b_i = bh[r0:r0 + SUB]
        for jj in range(SUB):
            b_j = b_ref[pl.ds(r0 + jj, 1), sl]
            k_j = kf_ref[pl.ds(r0 + jj, 1), sl]
            pieces.append((q_i * k_j * jnp.exp(jnp.minimum(b_i - b_j, 0.0))).astype(BF16))
    sums = _dot(jnp.concatenate(pieces, axis=0), ones)
    for blk in range(C // SUB):
        r0 = blk * SUB
        diag = jnp.zeros((SUB, LANES), F32)
        for jj in range(SUB):
            rs = sums[(r0 + jj) * SUB:(r0 + jj + 1) * SUB]
            diag = jnp.where((lane == r0 + jj) & (sub_i >= jj), rs, diag)
        diag = diag[:, 0:C]
        blocks[blk] = diag if blocks[blk] is None else blocks[blk] + diag
    att = jnp.concatenate(blocks, axis=0)
    o_ref[rs_c, pl.ds(h * dv, dv)] = (inter + _dot(att.astype(BF16), vh)).astype(o_ref.dtype)
    st_ref[h] = st * jnp.exp(bl) + _dot_tn(vh, kd.astype(BF16))


def _gla_head_bounded(row0, h, q_ref, v_ref, o_ref, st_ref, b_ref, kf_ref, scale):
    C, dk, dv = GLA_CHUNK, GLA_DK, GLA_DV
    sl = pl.ds(h * dk, dk)
    rs_c = pl.ds(row0, C)
    bh = b_ref[rs_c, sl]
    kf = kf_ref[rs_c, sl]
    qf = q_ref[rs_c, sl].astype(F32) * scale
    vh = v_ref[rs_c, pl.ds(h * dv, dv)]
    st = st_ref[h]
    qs = (qf * jnp.exp(bh)).astype(BF16)
    kt = (kf * jnp.exp(-bh)).astype(BF16)
    i = lax.broadcasted_iota(jnp.int32, (C, C), 0)
    j = lax.broadcasted_iota(jnp.int32, (C, C), 1)
    att = jnp.where(i >= j, _dot_nt(qs, kt), 0.0)
    inter = _dot_nt(qs, st.astype(BF16))
    o_ref[rs_c, pl.ds(h * dv, dv)] = (inter + _dot(att.astype(BF16), vh)).astype(o_ref.dtype)
    bl = b_ref[pl.ds(row0 + C - 1, 1), sl]
    kd = kf * jnp.exp(bl - bh)
    st_ref[h] = st * jnp.exp(bl) + _dot_tn(vh, kd.astype(BF16))


def _gla_kernel(q_ref, k_ref, v_ref, glr_ref, w2_ref, gb_ref, tril_ref, o_ref,
                st_ref, b_ref, kf_ref, bc_ref, kc_ref, *, scale):
    C, H = GLA_CHUNK, GLA_HEADS
    nch = q_ref.shape[0] // C

    @pl.when(pl.program_id(1) == 0)
    def _init():
        st_ref[...] = jnp.zeros(st_ref.shape, F32)

    whi, wlo = _hilo(w2_ref[...])
    part = tril_ref.shape[0]
    decay = None
    for r0 in range(0, nch * C, part):
        rs = pl.ds(r0, part)
        ghi, glo = _hilo(glr_ref[rs, :])
        pre = _dot(ghi, whi) + _dot(glo, whi) + _dot(ghi, wlo) + gb_ref[...]
        la = (jnp.minimum(pre, 0.0) - jnp.log(1.0 + jnp.exp(-jnp.abs(pre)))) * (1.0 / GLA_TAU)
        lhi, llo = _hilo(la)
        b_part = _dot(tril_ref[...], lhi) + _dot(tril_ref[...], llo)
        b_ref[rs, :] = b_part
        for c in range(part // C):
            d_c = -b_part[c * C + C - 1:(c + 1) * C, :]
            decay = d_c if decay is None else jnp.maximum(decay, d_c)
    kf_ref[...] = k_ref[...].astype(F32)
    bounded = jnp.max(decay) <= GLA_FACTOR_MAX_DECAY

    @pl.when(bounded)
    def _factored():
        for c in range(nch):
            for h in range(H):
                _gla_head_bounded(c * C, h, q_ref, v_ref, o_ref, st_ref, b_ref, kf_ref, scale)

    @pl.when(jnp.logical_not(bounded))
    def _exact():
        def chunk(c, carry):
            row0 = pl.multiple_of(c * C, C)
            bc_ref[...] = b_ref[pl.ds(row0, C), :]
            kc_ref[...] = kf_ref[pl.ds(row0, C), :]
            for h in range(H):
                _gla_head_exact(row0, h, q_ref, v_ref, o_ref, st_ref, bc_ref, kc_ref, scale)
            return carry
        lax.fori_loop(0, nch, chunk, 0)


def _gla(q, k, v, glr, w2p, gb, tril, *, B, S):
    C, H, dk, dv = GLA_CHUNK, GLA_HEADS, GLA_DK, GLA_DV
    rows = C * min(GLA_STEP_CHUNKS, S // C)
    nchunk = S // rows
    tok = lambda n: pl.BlockSpec((rows, n), lambda b, c: (b * nchunk + c, 0))
    const = lambda shape: pl.BlockSpec(shape, lambda b, c: (0,) * len(shape))
    return pl.pallas_call(
        functools.partial(_gla_kernel, scale=float(dk) ** -0.5),
        grid=(B, nchunk),
        in_specs=[tok(H * dk), tok(H * dk), tok(H * dv), tok(LANES),
                  const(w2p.shape), const(gb.shape), const(tril.shape)],
        out_specs=tok(H * dv),
        out_shape=jax.ShapeDtypeStruct((B * S, H * dv), BF16),
        scratch_shapes=[pltpu.VMEM((H, dv, dk), F32),
                        pltpu.VMEM((rows, H * dk), F32),
                        pltpu.VMEM((rows, H * dk), F32),
                        pltpu.VMEM((C, H * dk), F32),
                        pltpu.VMEM((C, H * dk), F32)],
        compiler_params=pltpu.CompilerParams(dimension_semantics=("parallel", "arbitrary"),
                                             vmem_limit_bytes=VMEM_LIMIT),
        name="gla",
    )(q, k, v, glr, w2p, gb, tril)


def _odd_out_kernel(x_ref, o_ref_in, z_ref, gn_ref, w_ref, g_ref, b_ref, out_ref, *, alpha):
    H, dv = GLA_HEADS, GLA_DV
    parts = []
    for h in range(H):
        sl = pl.ds(h * dv, dv)
        oh = o_ref_in[:, sl].astype(F32)
        r = lax.rsqrt(jnp.mean(oh * oh, axis=-1, keepdims=True) + LN_EPS)
        z = z_ref[:, sl].astype(F32)
        parts.append((oh * r * gn_ref[:, sl] * (z * _sigmoid(z))).astype(BF16))
    y = _dot(jnp.concatenate(parts, axis=1), w_ref[...])
    out_ref[...] = _layer_norm(alpha * x_ref[...] + y, g_ref[...], b_ref[...])


def _odd_out(x2d, o, z, gn_g, w_out, ln_g, ln_b, alpha, tm=1024):
    T, D = x2d.shape
    W = o.shape[1]
    row = lambda n: pl.BlockSpec((tm, n), lambda i: (i, 0))
    return pl.pallas_call(
        functools.partial(_odd_out_kernel, alpha=alpha),
        grid=(T // tm,),
        in_specs=[row(D), row(W), row(W),
                  pl.BlockSpec((1, W), lambda i: (0, 0)),
                  pl.BlockSpec((W, D), lambda i: (0, 0)),
                  pl.BlockSpec((1, D), lambda i: (0, 0)),
                  pl.BlockSpec((1, D), lambda i: (0, 0))],
        out_specs=row(D),
        out_shape=jax.ShapeDtypeStruct((T, D), F32),
        compiler_params=pltpu.CompilerParams(dimension_semantics=("parallel",),
                                             vmem_limit_bytes=VMEM_LIMIT),
        name="odd_out",
    )(x2d, o, z, gn_g, w_out, ln_g, ln_b)


def _key_features(S, block):
    s = np.arange(S)
    f = np.zeros((S, LANES), np.float32)
    f[s, s // block] = 1.0
    f[:, FEAT + 0] = 1.0
    f[:, FEAT + 1] = 1.0
    f[:, FEAT + 2] = FEAT * (s // FEAT)
    f[:, FEAT + 3] = s % FEAT
    return jnp.asarray(f, BF16)


def _pad_to(a, n, axis):
    pad = [(0, 0)] * a.ndim
    pad[axis] = (0, n - a.shape[axis])
    return jnp.pad(a, pad)


def _even_layer(x2d, B, S, w_in, cmp_pos, cmp_w1, cmp_b1, cmp_w2, w_out, ln_g, ln_b, alpha):
    T, D = x2d.shape
    G, hd = NSA_KV_GROUPS, HEAD_DIM
    scale = hd ** -0.5
    wq, wkv, wgate, wnz, wmq, wmk, wmv, wmz = jnp.split(w_in, np.cumsum(EVEN_SPLITS)[:-1].tolist(), axis=1)
    wkc, wvc, wks, wvs, wkw, wvw = jnp.split(wkv, 6, axis=1)
    w_tok = jnp.concatenate([wkc, wvc, wks, wkw, wmk], axis=1).astype(BF16)
    w_tr = jnp.concatenate([wq * scale, wvs, wvw, wmq * scale, wmv, wnz, wmz,
                            _pad_to(wgate, GATE_ROWS, 1)], axis=1).T.astype(BF16)
    (kcmp, vcmp, ksel, kwin, mk, nq_t, vsel_t, vwin_t, mq_t, mv_t, nz_t, mz_t, gate_t) = _proj(
        x2d, w_tok, (NSA_KV_W,) * 4 + (MOBA_W,), (F32, F32, BF16, BF16, BF16),
        w_tr, (NSA_W, NSA_KV_W, NSA_KV_W, MOBA_W, MOBA_W, NSA_W, MOBA_W, GATE_ROWS),
        (BF16,) * 7 + (F32,))

    L, d, HID = NSA_CMP_BLOCK, NSA_CMP_STRIDE, NSA_CMP_HIDDEN
    ncmp = (S - L) // d + 1
    NC = S // d
    eye = jnp.eye(G, dtype=F32)
    w1r = cmp_w1.reshape(2, 2, d, hd, HID)
    w1p = jnp.einsum('ialdj,gh->ialgdhj', w1r, eye).reshape(2, 2, d * G * hd, G * HID).astype(BF16)
    posp = jnp.broadcast_to(cmp_pos.reshape(2, 2, d, 1, hd), (2, 2, d, G, hd)).reshape(2, 2, 1, d * G * hd)
    b1p = jnp.tile(cmp_b1, (1, G)).reshape(2, 1, G * HID)
    w2p = jnp.einsum('ijd,gh->igjhd', _pad_to(cmp_w2, LANES, 2), eye).reshape(2, G * HID, G * LANES).astype(BF16)
    cfeat = np.zeros((NC, G, LANES), np.float32)
    cfeat[:, :, hd + 0] = 1.0
    cfeat[:, :, hd + 1] = 1.0
    cfeat[:, :, hd + 2] = (np.arange(NC) * d)[:, None]
    cfeat[:, :, hd + 3] = L - 1
    kvc = _cmp_mlp(kcmp, vcmp, w1p, posp, b1p, w2p, jnp.asarray(cfeat.reshape(NC, G * LANES)), ncmp,
                   B=B, S=S)

    ns = S // NSA_SEL_BLOCK
    cmp_start = np.arange(NC) * d
    sel_start = np.arange(FEAT) * NSA_SEL_BLOCK
    ovt = ((cmp_start[None, :] <= sel_start[:, None] + NSA_SEL_BLOCK - 1)
           & (cmp_start[None, :] + L - 1 >= sel_start[:, None])
           & (np.arange(NC)[None, :] < ncmp) & (np.arange(FEAT)[:, None] < ns))
    o_cmp, bias_sel = _cmp_attn(nq_t, kvc, jnp.asarray(ovt, BF16), B=B, S=S)

    feats_sel = _key_features(S, NSA_SEL_BLOCK)
    o_sel = _flash(nq_t, bias_sel, ksel, feats_sel, vsel_t,
                   B=B, S=S, units=G, tq=TQ, nh=NSA_HG, pair=False, window=None)
    o_win = _flash(nq_t, None, kwin, feats_sel, vwin_t,
                   B=B, S=S, units=G, tq=TQ, nh=NSA_HG, pair=False, window=NSA_WINDOW)

    avg = np.zeros((MOBA_NBP, S), np.float32)
    avg[np.arange(S) // MOBA_BLOCK, np.arange(S)] = 1.0 / MOBA_BLOCK
    bias_m = _moba_gate(mq_t, mk, jnp.asarray(avg, BF16), B=B, S=S)
    o_m = _flash(mq_t, bias_m, mk, _key_features(S, MOBA_BLOCK), mv_t,
                 B=B, S=S, units=MOBA_HEADS // MOBA_STEP_HEADS, tq=TQM, nh=MOBA_STEP_HEADS, pair=True, window=None)

    return _even_out(x2d, o_cmp, o_sel, o_win, o_m, gate_t, nz_t, mz_t, w_out.astype(BF16),
                     ln_g.reshape(1, D), ln_b.reshape(1, D), alpha, B=B, S=S)


def _odd_layer(x2d, B, S, w_in, gate_w2, gate_b, gn_g, w_out, ln_g, ln_b, alpha):
    T, D = x2d.shape
    H, dk, dv, C = GLA_HEADS, GLA_DK, GLA_DV, GLA_CHUNK
    wq, wk, wv, wg, wz = jnp.split(w_in, np.cumsum(ODD_SPLITS)[:-1].tolist(), axis=1)
    w_perm = jnp.concatenate([wq, wk, wv, wz, _pad_to(wg, LANES, 1)], axis=1).astype(BF16)
    q, k, v, z, glr = _proj(x2d, w_perm, (H * dk, H * dk, H * dv, H * dv, LANES),
                            (BF16, BF16, BF16, BF16, F32))
    w2p = _pad_to(gate_w2, LANES, 0)
    ncs = min(GLA_CUMSUM_CHUNKS, GLA_STEP_CHUNKS, S // C)
    tril = jnp.asarray(np.kron(np.eye(ncs), np.tril(np.ones((C, C)))), BF16)
    o = _gla(q, k, v, glr, w2p, gate_b.reshape(1, H * dk), tril, B=B, S=S)
    return _odd_out(x2d, o, z, gn_g.reshape(1, H * dv), w_out.astype(BF16),
                    ln_g.reshape(1, D), ln_b.reshape(1, D), alpha)


def kernel(x, ev_w_in, ev_cmp_pos, ev_cmp_w1, ev_cmp_b1, ev_cmp_w2, ev_w_out, ev_ln_g, ev_ln_b,
           od_w_in, od_gate_w2, od_gate_b, od_gn_g, od_w_out, od_ln_g, od_ln_b):
    B, S, D = x.shape
    depth = ev_w_in.shape[0] + od_w_in.shape[0]
    alpha = (2.0 * depth) ** 0.25
    h = x.reshape(B * S, D)
    for layer in range(depth):
        i = layer // 2
        if layer % 2 == 0:
            h = _even_layer(h, B, S, ev_w_in[i], ev_cmp_pos[i], ev_cmp_w1[i], ev_cmp_b1[i], ev_cmp_w2[i],
                            ev_w_out[i], ev_ln_g[i], ev_ln_b[i], alpha)
        else:
            h = _odd_layer(h, B, S, od_w_in[i], od_gate_w2[i], od_gate_b[i], od_gn_g[i],
                           od_w_out[i], od_ln_g[i], od_ln_b[i], alpha)
    return h.reshape(B, S, D)
```

```python
import functools

import numpy as np
import jax
import jax.numpy as jnp
from jax import lax
from jax.experimental import pallas as pl
from jax.experimental.pallas import tpu as pltpu

F32 = jnp.float32
BF16 = jnp.bfloat16

HEAD_DIM = 64
NSA_HEADS = 8
NSA_KV_GROUPS = 2
NSA_HG = NSA_HEADS // NSA_KV_GROUPS
NSA_CMP_BLOCK = 32
NSA_CMP_STRIDE = 16
NSA_CMP_HIDDEN = 128
NSA_SEL_BLOCK = 64
NSA_TOP_N = 16
NSA_WINDOW = 512
MOBA_HEADS = 8
MOBA_BLOCK = 256
MOBA_TOP_K = 3
GLA_HEADS = 4
GLA_DK = 128
GLA_DV = 256
GLA_GATE_RANK = 16
GLA_TAU = 16.0
GLA_CHUNK = 64
GLA_SUB = 16
GLA_STEP_CHUNKS = 16
GLA_CUMSUM_CHUNKS = 4
GLA_FACTOR_MAX_DECAY = 60.0
LN_EPS = 1e-5
NEG = -1e30
FORCE_BONUS = 1e4

NSA_W = NSA_HEADS * HEAD_DIM
NSA_KV_W = NSA_KV_GROUPS * HEAD_DIM
MOBA_W = MOBA_HEADS * HEAD_DIM
EVEN_SPLITS = (NSA_W, 6 * NSA_KV_W, 3 * NSA_HEADS, NSA_W, MOBA_W, MOBA_W, MOBA_W, MOBA_W)
ODD_SPLITS = (GLA_HEADS * GLA_DK, GLA_HEADS * GLA_DK, GLA_HEADS * GLA_DV, GLA_GATE_RANK, GLA_HEADS * GLA_DV)

LANES = 128
SUBLANES = 8
FEAT = 64
ALIBI_ROWS = 16
GATE_ROWS = 32
MOBA_NBP = 16
TQ = 512
TQM = 512
MOBA_STEP_HEADS = 4
TK = 512
CG = 256
MIN_WEIGHT_SUM = 1e-25
VROWS = HEAD_DIM + 16
VMEM_LIMIT = 48 * 1024 * 1024


def _dot(a, b):
    return jnp.dot(a, b, preferred_element_type=F32)


def _dot_nt(a, b):
    return lax.dot_general(a, b, (((1,), (1,)), ((), ())), preferred_element_type=F32)


def _dot_tn(a, b):
    return lax.dot_general(a, b, (((0,), (0,)), ((), ())), preferred_element_type=F32)


def _hilo(a):
    hi = a.astype(BF16)
    lo = (a - hi.astype(F32)).astype(BF16)
    return hi, lo


def _pow2_neg(n):
    return lax.bitcast_convert_type(lax.shift_left(127 - n, 23), F32)


def _sigmoid(x):
    return 0.5 * jnp.tanh(0.5 * x) + 0.5


def _layer_norm(r, g, b):
    mu = jnp.mean(r, axis=-1, keepdims=True)
    d = r - mu
    var = jnp.mean(d * d, axis=-1, keepdims=True)
    return d * lax.rsqrt(var + LN_EPS) * g + b


def _alibi_query_rows(t0, tq, head0, R):
    r = lax.broadcasted_iota(jnp.int32, (ALIBI_ROWS, R), 0)
    col = lax.broadcasted_iota(jnp.int32, (ALIBI_ROWS, R), 1)
    t = t0 + col % tq
    slope = _pow2_neg(head0 + col // tq + 1)
    feats = jnp.where(r == 0, -FEAT * slope * (t // FEAT).astype(F32),
                      jnp.where(r == 1, -slope * (t % FEAT).astype(F32), jnp.where(r < 4, slope, 0.0)))
    return jnp.concatenate([feats.astype(BF16), jnp.zeros((FEAT - ALIBI_ROWS, R), BF16)], axis=0)


def _ranks(sc_ref, n_real, n_rows=None):
    width = sc_ref.shape[1]
    n_rows = sc_ref.shape[0] if n_rows is None else n_rows
    out = []
    for v in range(n_rows // SUBLANES):
        lo = v * SUBLANES
        sc = sc_ref[lo:lo + SUBLANES, :]
        idx = lo + lax.broadcasted_iota(jnp.int32, (SUBLANES, width), 0)
        rank = jnp.zeros((SUBLANES, width), jnp.int32)
        for jp in range(n_real):
            r = sc_ref[pl.ds(jp, 1), :]
            if jp < lo:
                beats = r >= sc
            elif jp >= lo + SUBLANES - 1:
                beats = r > sc
            else:
                beats = (r > sc) | ((r == sc) & (idx > jp))
            rank = rank + jnp.where(beats, 1, 0)
        out.append(rank)
    return jnp.concatenate(out, axis=0)


def _proj_kernel(*refs, tok_splits, tr_splits):
    x_ref, wt_ref = refs[0], refs[1]
    outs = list(refs[3:] if tr_splits else refs[2:])
    x = x_ref[...]
    xb = x.astype(BF16)
    off = 0
    for n in tok_splits:
        o_ref = outs.pop(0)
        o_ref[...] = _dot(xb, wt_ref[:, off:off + n]).astype(o_ref.dtype)
        off += n
    if tr_splits:
        wtr_ref = refs[2]
        xt = x.T.astype(BF16)
        off = 0
        for n in tr_splits:
            o_ref = outs.pop(0)
            o_ref[...] = _dot(wtr_ref[off:off + n, :], xt).astype(o_ref.dtype)
            off += n


def _proj(x2d, w_tok, tok_splits, tok_dtypes, w_tr=None, tr_splits=(), tr_dtypes=(), tm=512):
    T, D = x2d.shape
    assert T % tm == 0 and sum(tok_splits) == w_tok.shape[1]
    in_specs = [pl.BlockSpec((tm, D), lambda i: (i, 0)),
                pl.BlockSpec(w_tok.shape, lambda i: (0, 0))]
    args = [x2d, w_tok]
    if tr_splits:
        assert sum(tr_splits) == w_tr.shape[0]
        in_specs.append(pl.BlockSpec(w_tr.shape, lambda i: (0, 0)))
        args.append(w_tr)
    out_specs = ([pl.BlockSpec((tm, n), lambda i: (i, 0)) for n in tok_splits]
                 + [pl.BlockSpec((n, tm), lambda i: (0, i)) for n in tr_splits])
    out_shape = ([jax.ShapeDtypeStruct((T, n), dt) for n, dt in zip(tok_splits, tok_dtypes)]
                 + [jax.ShapeDtypeStruct((n, T), dt) for n, dt in zip(tr_splits, tr_dtypes)])
    return pl.pallas_call(
        functools.partial(_proj_kernel, tok_splits=tuple(tok_splits), tr_splits=tuple(tr_splits)),
        grid=(T // tm,), in_specs=in_specs, out_specs=out_specs, out_shape=out_shape,
        compiler_params=pltpu.CompilerParams(dimension_semantics=("parallel",),
                                             vmem_limit_bytes=VMEM_LIMIT),
        name="proj",
    )(*args)


def _cmp_mlp_kernel(kx_ref, vx_ref, w1_ref, pos_ref, b1_ref, w2_ref, cf_ref, o_ref, *, ncmp):
    d = NSA_CMP_STRIDE
    nc = kx_ref.shape[0] // d
    cw = w1_ref.shape[2]
    for i, x_ref in enumerate((kx_ref, vx_ref)):
        a = bm = None
        for l in range(0, d, 2):
            x2 = jnp.concatenate([x_ref[pl.ds(l, nc, stride=d), :], x_ref[pl.ds(l + 1, nc, stride=d), :]],
                                 axis=1).astype(BF16)
            rows = pl.ds(l * x_ref.shape[1], 2 * x_ref.shape[1])
            a_l, b_l = _dot(x2, w1_ref[i, 0, rows, :]), _dot(x2, w1_ref[i, 1, rows, :])
            a, bm = (a_l, b_l) if a is None else (a + a_l, bm + b_l)
        c = b1_ref[i]
        for half in range(2):
            phi, plo = _hilo(jnp.broadcast_to(pos_ref[i, half], (SUBLANES, cw)))
            c = c + (_dot(phi, w1_ref[i, half]) + _dot(plo, w1_ref[i, half]))[0:1]
        hid = a + pltpu.roll(bm, nc - 1, axis=0) + c
        hid = hid * _sigmoid(hid)
        row = lax.broadcasted_iota(jnp.int32, hid.shape, 0)
        hid = jnp.where(row < ncmp, hid, 0.0)
        out = _dot(hid.astype(BF16), w2_ref[i])
        if i == 0:
            out = out + cf_ref[...]
        o_ref[i] = out.astype(o_ref.dtype)


def _cmp_mlp(kx, vx, w1p, posp, b1p, w2p, cfeat, ncmp, *, B, S):
    NC = S // NSA_CMP_STRIDE
    OW = w2p.shape[-1]
    full = lambda a: pl.BlockSpec(a.shape, lambda b: (0,) * a.ndim)
    return pl.pallas_call(
        functools.partial(_cmp_mlp_kernel, ncmp=ncmp),
        grid=(B,),
        in_specs=[pl.BlockSpec((S, kx.shape[1]), lambda b: (b, 0)),
                  pl.BlockSpec((S, vx.shape[1]), lambda b: (b, 0)),
                  full(w1p), full(posp), full(b1p), full(w2p), full(cfeat)],
        out_specs=pl.BlockSpec((2, None, NC, OW), lambda b: (0, b, 0, 0)),
        out_shape=jax.ShapeDtypeStruct((2, B, NC, OW), BF16),
        compiler_params=pltpu.CompilerParams(dimension_semantics=("parallel",),
                                             vmem_limit_bytes=VMEM_LIMIT),
        name="cmp_mlp",
    )(kx, vx, w1p, posp, b1p, w2p, cfeat)


def _cmp_attn_kernel(q_ref, kc_ref, vc_ref, ovt_ref, o_ref, bias_ref, sc_ref, *, tq, nh, ns, ntop):
    g = pl.program_id(1)
    R = nh * tq
    nq = q_ref.shape[1] // tq

    def q_tile(qi):
        toks = pl.ds(qi * tq, tq)
        nc = min(kc_ref.shape[0], -(-((qi + 1) * tq) // (NSA_CMP_STRIDE * 16)) * 16)
        nsel = min(ns, -(-((qi + 1) * tq) // (NSA_SEL_BLOCK * SUBLANES)) * SUBLANES)
        qcat = jnp.concatenate([q_ref[hh * HEAD_DIM:(hh + 1) * HEAD_DIM, toks] for hh in range(nh)], axis=1)
        qa = jnp.concatenate([qcat, _alibi_query_rows(qi * tq, tq, g * nh, R)], axis=0)
        s = _dot(kc_ref[0:nc, :], qa)
        n = lax.broadcasted_iota(jnp.int32, (nc, R), 0)
        t_row = qi * tq + lax.broadcasted_iota(jnp.int32, (1, R), 1) % tq
        last = (t_row - (NSA_CMP_BLOCK - 1)) // NSA_CMP_STRIDE
        s = jnp.where(n <= last, s, NEG)
        e = jnp.exp(s - jnp.max(s, axis=0, keepdims=True))
        inv = jnp.where(last >= 0, 1.0 / jnp.sum(e, axis=0, keepdims=True), 0.0)
        p = e * inv
        o_ref[:, pl.ds(qi * R, R)] = _dot_tn(vc_ref[0:nc, 0:HEAD_DIM], p.astype(BF16)).astype(o_ref.dtype)

        psum = p[:, 0:tq]
        for hh in range(1, nh):
            psum = psum + p[:, hh * tq:(hh + 1) * tq]
        phi, plo = _hilo(psum)
        ovt = ovt_ref[0:nsel, 0:nc]
        imp = _dot(ovt, phi) + _dot(ovt, plo)

        j = lax.broadcasted_iota(jnp.int32, (nsel, tq), 0)
        cur = (qi * tq + lax.broadcasted_iota(jnp.int32, (nsel, tq), 1)) // NSA_SEL_BLOCK
        if nsel <= ntop:
            sel = j <= cur
        else:
            forced = jnp.where((j == 0) | (j == cur) | (j == cur - 1), FORCE_BONUS, 0.0)
            sc_ref[0:nsel, :] = jnp.where(j <= cur, imp + forced, NEG)
            sel = (_ranks(sc_ref, nsel, nsel) < ntop) & (j <= cur)
        bias_ref[0:nsel, toks] = jnp.where(sel, 0.0, NEG).astype(bias_ref.dtype)
        if nsel < FEAT:
            bias_ref[nsel:FEAT, toks] = jnp.full((FEAT - nsel, tq), NEG, bias_ref.dtype)

    for qi in range(nq):
        q_tile(qi)


def _cmp_attn(nq_t, kvc, ovt, *, B, S):
    tq, nh, G = TQ, NSA_HG, NSA_KV_GROUPS
    R = nh * tq
    nq = S // tq
    NC = kvc.shape[2]
    ns = S // NSA_SEL_BLOCK
    assert ns <= FEAT
    ntop = min(NSA_TOP_N, ns)
    return pl.pallas_call(
        functools.partial(_cmp_attn_kernel, tq=tq, nh=nh, ns=ns, ntop=ntop),
        grid=(B, G),
        in_specs=[pl.BlockSpec((nh * HEAD_DIM, S), lambda b, g: (g, b)),
                  pl.BlockSpec((None, None, NC, LANES), lambda b, g: (0, b, 0, g)),
                  pl.BlockSpec((None, None, NC, LANES), lambda b, g: (1, b, 0, g)),
                  pl.BlockSpec(ovt.shape, lambda b, g: (0, 0))],
        out_specs=[pl.BlockSpec((None, None, HEAD_DIM, S * nh), lambda b, g: (b, g, 0, 0)),
                   pl.BlockSpec((None, None, FEAT, S), lambda b, g: (b, g, 0, 0))],
        out_shape=[jax.ShapeDtypeStruct((B, G, HEAD_DIM, S * nh), BF16),
                   jax.ShapeDtypeStruct((B, G, FEAT, S), BF16)],
        scratch_shapes=[pltpu.VMEM((FEAT, tq), F32)],
        compiler_params=pltpu.CompilerParams(dimension_semantics=("parallel", "parallel"),
                                             vmem_limit_bytes=VMEM_LIMIT),
        name="cmp_attn",
    )(nq_t, kvc, kvc, ovt)


def _moba_gate_kernel(q_ref, k_ref, a_ref, bias_ref, km_ref, sc_ref, *, tq, nb, ntop):
    h = pl.program_id(1)
    qi = pl.program_id(2)

    @pl.when(qi == 0)
    def _means():
        km_ref[...] = _dot(a_ref[...], k_ref[...])

    nbp = km_ref.shape[0]
    lane = lax.broadcasted_iota(jnp.int32, (nbp, LANES), 1)
    km = jnp.where(lane // HEAD_DIM == h % 2, km_ref[...], 0.0)
    khi, klo = _hilo(km)
    q = q_ref[...]
    qq = jnp.concatenate([q, q], axis=0)
    g_t = _dot(khi, qq) + _dot(klo, qq)
    n = lax.broadcasted_iota(jnp.int32, (nbp, tq), 0)
    cur = (qi * tq + lax.broadcasted_iota(jnp.int32, (nbp, tq), 1)) // MOBA_BLOCK
    sc = jnp.where(n < cur, g_t, NEG)
    sc_ref[...] = sc
    sel = ((_ranks(sc_ref, nb) < ntop) & (n < cur)) | (n == cur)
    bias_ref[0:nbp, :] = jnp.where(sel, 0.0, NEG).astype(bias_ref.dtype)
    bias_ref[nbp:FEAT, :] = jnp.zeros((FEAT - nbp, tq), bias_ref.dtype)


def _moba_gate(mq_t, mk, avg, *, B, S):
    tq, H = S, MOBA_HEADS
    nq = S // tq
    nb = S // MOBA_BLOCK
    nbp = avg.shape[0]
    assert nb <= nbp
    ntop = min(MOBA_TOP_K, nb)
    return pl.pallas_call(
        functools.partial(_moba_gate_kernel, tq=tq, nb=nb, ntop=ntop),
        grid=(B, H, nq),
        in_specs=[pl.BlockSpec((HEAD_DIM, tq), lambda b, h, i: (h, b * nq + i)),
                  pl.BlockSpec((S, LANES), lambda b, h, i: (b, h // 2)),
                  pl.BlockSpec(avg.shape, lambda b, h, i: (0, 0))],
        out_specs=pl.BlockSpec((None, None, FEAT, tq), lambda b, h, i: (b, h, 0, i)),
        out_shape=jax.ShapeDtypeStruct((B, H, FEAT, S), BF16),
        scratch_shapes=[pltpu.VMEM((nbp, LANES), F32), pltpu.VMEM((nbp, tq), F32)],
        compiler_params=pltpu.CompilerParams(dimension_semantics=("parallel", "parallel", "arbitrary"),
                                             vmem_limit_bytes=VMEM_LIMIT),
        name="moba_gate",
    )(mq_t, mk, avg)


def _flash_q_tile(qi, running_max, q_ref, bias_ref, mask_ref, o_ref, ka_ref, vta_ref, kn_ref, qa_ref, bnd_ref,
                  m_ref, acc_ref, alpha_ref, p_ref, *, tq, tk, nh, pair, window, has_past=None):
    u = pl.program_id(1)
    R = nh * tq
    static = isinstance(qi, int)
    q0 = qi * tq if static else pl.multiple_of(qi * tq, tq)
    toks = pl.ds(q0, tq)

    qcat = jnp.concatenate([q_ref[hh * HEAD_DIM:(hh + 1) * HEAD_DIM, toks] for hh in range(nh)], axis=1)
    qf = qcat.astype(F32)
    qq = jnp.concatenate([qf, qf], axis=0)
    rowi = lax.broadcasted_iota(jnp.int32, (2 * HEAD_DIM, R), 0)
    coli = lax.broadcasted_iota(jnp.int32, (2 * HEAD_DIM, R), 1)
    half = (coli // tq) % 2 if pair else u % 2
    qa_ref[0:2 * HEAD_DIM, :] = jnp.where(rowi // HEAD_DIM == half, qq, 0.0).astype(BF16)
    if bias_ref is None:
        qa_ref[2 * HEAD_DIM:2 * HEAD_DIM + FEAT, :] = jnp.zeros((FEAT, R), BF16)
    else:
        qa_ref[2 * HEAD_DIM:2 * HEAD_DIM + FEAT, :] = jnp.concatenate(
            [bias_ref[hh if pair else 0, :, toks] for hh in range(nh)], axis=1)
    qa_ref[2 * HEAD_DIM + FEAT:2 * LANES, :] = _alibi_query_rows(q0, tq, u * nh, R)
    if pair:
        kn = jnp.concatenate([kn_ref[hh:hh + 1, :] for hh in range(nh) for _ in range(tq // LANES)], axis=1)
    else:
        kn = jnp.concatenate([jnp.where(u % 2 == 0, kn_ref[0:1, :], kn_ref[1:2, :])] * (R // LANES), axis=1)
    qn = jnp.sqrt(jnp.sum(qf * qf, axis=0, keepdims=True))
    bnd_ref[...] = jnp.broadcast_to(qn * kn, bnd_ref.shape)

    ncg = R // CG
    kd = (q0 + tq - 1) // tk

    def key_rows(kind, c):
        off = (c * CG) % tq
        if kind is None:
            return 0, tk, None
        if kind == "diag":
            return 0, off + CG, off // CG
        return off, tk, (tk + off) // CG

    def values(ki, kind=None):
        for c in range(ncg):
            cols = pl.ds(c * CG, CG)
            r0, r1, _ = key_rows(kind, c)
            v = (c * CG) // tq if pair else 0
            pv = _dot(vta_ref[v, ki, :, r0:r1], p_ref[r0:r1, cols])
            if running_max:
                acc_ref[:, cols] = alpha_ref[0:1, cols] * acc_ref[:, cols] + pv
            else:
                acc_ref[:, cols] = acc_ref[:, cols] + pv

    def scores(ki, kind=None, prev=None, prev_kind=None):
        k0 = ki * tk if isinstance(ki, int) else pl.multiple_of(ki * tk, tk)
        s_groups = []
        for c in range(ncg):
            r0, r1, _ = key_rows(kind, c)
            kb = (c * CG) // (2 * tq) if pair else 0
            s_groups.append(_dot(ka_ref[kb, pl.ds(k0 + r0, r1 - r0), :], qa_ref[:, pl.ds(c * CG, CG)]))
        if prev is not None:
            values(prev, prev_kind)
        for c, s in enumerate(s_groups):
            cols = pl.ds(c * CG, CG)
            r0, r1, entry = key_rows(kind, c)
            if entry is not None:
                s = s + mask_ref[entry, r0:r1, :]
            if running_max:
                m_prev = m_ref[0:1, cols]
                m_new = jnp.maximum(m_prev, jnp.max(s, axis=0, keepdims=True))
                alpha_ref[:, cols] = jnp.broadcast_to(jnp.exp(m_prev - m_new), (alpha_ref.shape[0], CG))
                m_ref[:, cols] = jnp.broadcast_to(m_new, (m_ref.shape[0], CG))
                p_ref[r0:r1, cols] = jnp.exp(s - m_new).astype(BF16)
            else:
                p_ref[r0:r1, cols] = jnp.exp(s - bnd_ref[0:1, cols]).astype(BF16)

    def steady(lo, hi):
        def pair_body(j, carry):
            ki = lo + 2 * j
            scores(ki, prev=ki - 1)
            scores(ki + 1, prev=ki)
            return carry
        lax.fori_loop(0, (hi - lo) // 2, pair_body, 0)

        @pl.when((hi - lo) % 2 == 1)
        def _odd_one():
            scores(hi - 1, prev=hi - 2)

    def only_diagonal():
        scores(0, "diag")
        values(0, "diag")

    def past_then_diagonal():
        if window is not None:
            scores(kd - 1, "wedge")
            scores(kd, "diag", prev=kd - 1, prev_kind="wedge")
        elif static:
            scores(0)
            for ki in range(1, kd):
                scores(ki, prev=ki - 1)
            scores(kd, "diag", prev=kd - 1)
        else:
            scores(0)
            steady(1, kd)
            scores(kd, "diag", prev=kd - 1)
        values(kd, "diag")

    acc_ref[...] = jnp.zeros(acc_ref.shape, F32)
    if running_max:
        m_ref[...] = jnp.full(m_ref.shape, NEG, F32)
    if static:
        only_diagonal() if kd == 0 else past_then_diagonal()
    elif has_past:
        past_then_diagonal()
    else:
        pl.when(kd == 0)(only_diagonal)
        pl.when(kd > 0)(past_then_diagonal)

    acc = acc_ref[...]
    out = (acc[0:HEAD_DIM] / acc[HEAD_DIM:HEAD_DIM + 1]).astype(o_ref.dtype)
    if pair:
        for hh in range(nh):
            o_ref[hh, :, toks] = out[:, hh * tq:(hh + 1) * tq]
    else:
        o_ref[0, :, pl.ds(qi * R if static else pl.multiple_of(qi * R, R), R)] = out
    return jnp.min(acc[HEAD_DIM:HEAD_DIM + 1, :])


def _flash_kernel(*refs, tq, tk, nh, pair, window, has_bias):
    q_ref = refs[0]
    bias_ref = refs[1] if has_bias else None
    (k_ref, kf_ref, vt_ref, mask_ref, o_ref, ka_ref, vta_ref, kn_ref, qa_ref, bnd_ref, m_ref, acc_ref,
     alpha_ref, p_ref, low_ref) = refs[2 if has_bias else 1:]
    S = k_ref.shape[0]
    for kb in range(ka_ref.shape[0]):
        k = k_ref[:, kb * LANES:(kb + 1) * LANES]
        ka_ref[kb, :, 0:LANES] = k
        ka_ref[kb, :, LANES:2 * LANES] = kf_ref[...]
        k2 = jnp.square(k.astype(F32))
        lane = lax.broadcasted_iota(jnp.int32, k2.shape, 1)
        for hf in range(2):
            n2 = jnp.sum(jnp.where(lane // HEAD_DIM == hf, k2, 0.0), axis=1, keepdims=True)
            kn_ref[2 * kb + hf:2 * kb + hf + 1, :] = jnp.broadcast_to(
                jnp.sqrt(jnp.max(n2, axis=0, keepdims=True)), (1, LANES))
    ones_row = jnp.where(lax.broadcasted_iota(jnp.int32, (VROWS - HEAD_DIM, tk), 0) == 0, 1.0, 0.0)
    for v in range(nh if pair else 1):
        for jt in range(S // tk):
            vta_ref[v, jt, 0:HEAD_DIM, :] = vt_ref[v * HEAD_DIM:(v + 1) * HEAD_DIM, jt * tk:(jt + 1) * tk]
            vta_ref[v, jt, HEAD_DIM:VROWS, :] = ones_row.astype(BF16)

    tile = functools.partial(_flash_q_tile, q_ref=q_ref, bias_ref=bias_ref, mask_ref=mask_ref, o_ref=o_ref,
                             ka_ref=ka_ref, vta_ref=vta_ref, kn_ref=kn_ref, qa_ref=qa_ref, bnd_ref=bnd_ref,
                             m_ref=m_ref, acc_ref=acc_ref, alpha_ref=alpha_ref, p_ref=p_ref,
                             tq=tq, tk=tk, nh=nh, pair=pair, window=window)
    nq = S // tq
    low_ref[0] = tile(0, False)
    if window is not None:
        for qi in range(1, nq):
            low_ref[qi] = tile(qi, False)
    else:
        def bound_stabilised(qi, carry):
            low_ref[qi] = tile(qi, False, has_past=True)
            return carry
        lax.fori_loop(1, nq, bound_stabilised, 0)

    def redo(qi, carry):
        @pl.when(low_ref[qi] < MIN_WEIGHT_SUM)
        def _running_max():
            tile(qi, True)
        return carry

    lax.fori_loop(0, nq, redo, 0)


def _mask_table(tk, window):
    n_delta = (tk + (window or 0)) // CG
    d = np.arange(n_delta)[:, None, None] * CG + np.arange(CG)[None, None, :] - np.arange(tk)[None, :, None]
    ok = (d >= 0) & ((d < window) if window else True)
    return jnp.asarray(np.where(ok, 0.0, NEG), F32)


def _flash(q_t, bias_t, k, kfeat, v_t, *, B, S, units, tq, nh, pair, window):
    tk = TK
    R = nh * tq
    nq = S // tq
    nv = nh if pair else 1
    nb = nh if pair else 1
    nkb = nh // 2 if pair else 1
    assert tq == tk and S % tk == 0 and tk % CG == 0 and window in (None, tk)
    mask = _mask_table(tk, window)
    kcol = (lambda u: u) if pair else (lambda u: u // 2)
    out_w = tq if pair else R
    out_shape = (B, units * nb, HEAD_DIM, S if pair else S * nh)
    has_bias = bias_t is not None
    in_specs = [pl.BlockSpec((nh * HEAD_DIM, S), lambda b, u: (u, b))]
    if has_bias:
        in_specs.append(pl.BlockSpec((None, nb, FEAT, S), lambda b, u: (b, u, 0, 0)))
    in_specs += [pl.BlockSpec((S, nkb * LANES), lambda b, u: (b, kcol(u))),
                 pl.BlockSpec((S, LANES), lambda b, u: (0, 0)),
                 pl.BlockSpec((nv * HEAD_DIM, S), lambda b, u: (u, b)),
                 pl.BlockSpec(mask.shape, lambda b, u: (0, 0, 0))]
    args = (q_t,) + ((bias_t,) if has_bias else ()) + (k, kfeat, v_t, mask)
    return pl.pallas_call(
        functools.partial(_flash_kernel, tq=tq, tk=tk, nh=nh, pair=pair, window=window, has_bias=has_bias),
        grid=(B, units),
        in_specs=in_specs,
        out_specs=pl.BlockSpec((None, nb, HEAD_DIM, out_w * nq), lambda b, u: (b, u, 0, 0)),
        out_shape=jax.ShapeDtypeStruct(out_shape, BF16),
        scratch_shapes=[pltpu.VMEM((nkb, S, 2 * LANES), BF16),
                        pltpu.VMEM((nv, S // tk, VROWS, tk), BF16),
                        pltpu.VMEM((SUBLANES, LANES), F32),
                        pltpu.VMEM((2 * LANES, R), BF16),
                        pltpu.VMEM((SUBLANES, R), F32),
                        pltpu.VMEM((SUBLANES, R), F32),
                        pltpu.VMEM((VROWS, R), F32),
                        pltpu.VMEM((SUBLANES, R), F32),
                        pltpu.VMEM((tk, R), BF16),
                        pltpu.SMEM((nq,), F32)],
        compiler_params=pltpu.CompilerParams(dimension_semantics=("parallel", "parallel"),
                                             vmem_limit_bytes=VMEM_LIMIT),
        name="flash",
    )(*args)


def _even_out_kernel(x_ref, ocmp_ref, osel_ref, owin_ref, om_ref, gate_ref, nz_ref, mz_ref,
                     w_ref, g_ref, b_ref, o_ref, sig_ref, *, alpha, tm):
    G, nh = NSA_KV_GROUPS, NSA_HG
    sig_ref[...] = _sigmoid(gate_ref[...])
    heads = []
    for g in range(G):
        for hh in range(nh):
            acc = None
            for br, ref in enumerate((ocmp_ref, osel_ref, owin_ref)):
                gate = sig_ref[pl.ds(3 * (g * nh + hh) + br, 1), :]
                term = gate * ref[g, :, hh * tm:(hh + 1) * tm].astype(F32)
                acc = term if acc is None else acc + term
            heads.append(acc)
    o_nsa = jnp.concatenate(heads, axis=0)
    nz = nz_ref[...].astype(F32)
    mz = mz_ref[...].astype(F32)
    o_m = jnp.concatenate([om_ref[h] for h in range(MOBA_HEADS)], axis=0).astype(F32)
    a_t = jnp.concatenate([o_nsa * (nz * _sigmoid(nz)), o_m * (mz * _sigmoid(mz))], axis=0)
    y = _dot_tn(a_t.astype(BF16), w_ref[...])
    o_ref[...] = _layer_norm(alpha * x_ref[...] + y, g_ref[...], b_ref[...])


def _even_out(x2d, ocmp, osel, owin, om, gate_t, nz_t, mz_t, w_out, ln_g, ln_b, alpha, *, B, S):
    T, D = x2d.shape
    tm = TQ
    nq = S // tm
    G, nh, H = NSA_KV_GROUPS, NSA_HG, MOBA_HEADS
    nsa = pl.BlockSpec((None, G, HEAD_DIM, nh * tm), lambda i: (i // nq, 0, 0, i % nq))
    feat = lambda n: pl.BlockSpec((n, tm), lambda i: (0, i))
    full = lambda a: pl.BlockSpec(a.shape, lambda i: (0,) * a.ndim)
    return pl.pallas_call(
        functools.partial(_even_out_kernel, alpha=alpha, tm=tm),
        grid=(T // tm,),
        in_specs=[pl.BlockSpec((tm, D), lambda i: (i, 0)), nsa, nsa, nsa,
                  pl.BlockSpec((None, H, HEAD_DIM, tm), lambda i: (i // nq, 0, 0, i % nq)),
                  feat(GATE_ROWS), feat(NSA_W), feat(MOBA_W),
                  full(w_out), full(ln_g), full(ln_b)],
        out_specs=pl.BlockSpec((tm, D), lambda i: (i, 0)),
        out_shape=jax.ShapeDtypeStruct((T, D), F32),
        scratch_shapes=[pltpu.VMEM((GATE_ROWS, tm), F32)],
        compiler_params=pltpu.CompilerParams(dimension_semantics=("parallel",),
                                             vmem_limit_bytes=VMEM_LIMIT),
        name="even_out",
    )(x2d, ocmp, osel, owin, om, gate_t, nz_t, mz_t, w_out, ln_g, ln_b)


def _gla_head_exact(row0, h, q_ref, v_ref, o_ref, st_ref, b_ref, kf_ref, scale):
    C, dk, dv, SUB = GLA_CHUNK, GLA_DK, GLA_DV, GLA_SUB
    rows = lax.broadcasted_iota(jnp.int32, (C, dk), 0)
    sub_i = lax.broadcasted_iota(jnp.int32, (SUB, LANES), 0)
    lane = lax.broadcasted_iota(jnp.int32, (SUB, LANES), 1)
    ones = jnp.ones((dk, LANES), BF16)
    sl = pl.ds(h * dk, dk)
    rs_c = pl.ds(row0, C)
    bh = b_ref[:, sl]
    kf = kf_ref[:, sl]
    qf = q_ref[rs_c, sl].astype(F32) * scale
    vh = v_ref[rs_c, pl.ds(h * dv, dv)]
    st = st_ref[h]
    inter = _dot_nt((qf * jnp.exp(bh)).astype(BF16), st.astype(BF16))
    bl = b_ref[pl.ds(C - 1, 1), sl]
    kd = kf * jnp.exp(bl - bh)

    def off_block(r0, r1, c0, c1):
        ref_row = b_ref[pl.ds(r0, 1), sl]
        qt = qf[r0:r1] * jnp.exp(bh[r0:r1] - ref_row)
        ek = jnp.where((rows >= c0) & (rows < c1), ref_row - bh, NEG)
        kt = kf * jnp.exp(ek)
        return _dot_nt(qt.astype(BF16), kt.astype(BF16))

    lower = off_block(2 * SUB, C, 0, 2 * SUB)
    blocks = [None, off_block(SUB, 2 * SUB, 0, SUB), lower[0:SUB],
              lower[SUB:2 * SUB] + off_block(3 * SUB, C, 2 * SUB, 3 * SUB)]
    pieces = []
    for blk in range(C // SUB):
        r0 = blk * SUB
        q_i = qf[r0:r0 + SUB]
        b_i = bh[r0:r0 + SUB]
        for jj in range(SUB):
            b_j = b_ref[pl.ds(r0 + jj, 1), sl]
            k_j = kf_ref[pl.ds(r0 + jj, 1), sl]
            pieces.append((q_i * k_j * jnp.exp(jnp.minimum(b_i - b_j, 0.0))).astype(BF16))
    sums = _dot(jnp.concatenate(pieces, axis=0), ones)
    for blk in range(C // SUB):
        r0 = blk * SUB
        diag = jnp.zeros((SUB, LANES), F32)
        for jj in range(SUB):
            rs = sums[(r0 + jj) * SUB:(r0 + jj + 1) * SUB]
            diag = jnp.where((lane == r0 + jj) & (sub_i >= jj), rs, diag)
        diag = diag[:, 0:C]
        blocks[blk] = diag if blocks[blk] is None else blocks[blk] + diag
    att = jnp.concatenate(blocks, axis=0)
    o_ref[rs_c, pl.ds(h * dv, dv)] = (inter + _dot(att.astype(BF16), vh)).astype(o_ref.dtype)
    st_ref[h] = st * jnp.exp(bl) + _dot_tn(vh, kd.astype(BF16))


def _gla_head_bounded(row0, h, q_ref, v_ref, o_ref, st_ref, b_ref, kf_ref, scale):
    C, dk, dv = GLA_CHUNK, GLA_DK, GLA_DV
    sl = pl.ds(h * dk, dk)
    rs_c = pl.ds(row0, C)
    bh = b_ref[rs_c, sl]
    kf = kf_ref[rs_c, sl]
    qf = q_ref[rs_c, sl].astype(F32) * scale
    vh = v_ref[rs_c, pl.ds(h * dv, dv)]
    st = st_ref[h]
    qs = (qf * jnp.exp(bh)).astype(BF16)
    kt = (kf * jnp.exp(-bh)).astype(BF16)
    i = lax.broadcasted_iota(jnp.int32, (C, C), 0)
    j = lax.broadcasted_iota(jnp.int32, (C, C), 1)
    att = jnp.where(i >= j, _dot_nt(qs, kt), 0.0)
    inter = _dot_nt(qs, st.astype(BF16))
    o_ref[rs_c, pl.ds(h * dv, dv)] = (inter + _dot(att.astype(BF16), vh)).astype(o_ref.dtype)
    bl = b_ref[pl.ds(row0 + C - 1, 1), sl]
    kd = kf * jnp.exp(bl - bh)
    st_ref[h] = st * jnp.exp(bl) + _dot_tn(vh, kd.astype(BF16))


def _gla_kernel(q_ref, k_ref, v_ref, glr_ref, w2_ref, gb_ref, tril_ref, o_ref,
                st_ref, b_ref, kf_ref, bc_ref, kc_ref, *, scale):
    C, H = GLA_CHUNK, GLA_HEADS
    nch = q_ref.shape[0] // C

    @pl.when(pl.program_id(1) == 0)
    def _init():
        st_ref[...] = jnp.zeros(st_ref.shape, F32)

    whi, wlo = _hilo(w2_ref[...])
    part = tril_ref.shape[0]
    decay = None
    for r0 in range(0, nch * C, part):
        rs = pl.ds(r0, part)
        ghi, glo = _hilo(glr_ref[rs, :])
        pre = _dot(ghi, whi) + _dot(glo, whi) + _dot(ghi, wlo) + gb_ref[...]
        la = (jnp.minimum(pre, 0.0) - jnp.log(1.0 + jnp.exp(-jnp.abs(pre)))) * (1.0 / GLA_TAU)
        lhi, llo = _hilo(la)
        b_part = _dot(tril_ref[...], lhi) + _dot(tril_ref[...], llo)
        b_ref[rs, :] = b_part
        for c in range(part // C):
            d_c = -b_part[c * C + C - 1:(c + 1) * C, :]
            decay = d_c if decay is None else jnp.maximum(decay, d_c)
    kf_ref[...] = k_ref[...].astype(F32)
    bounded = jnp.max(decay) <= GLA_FACTOR_MAX_DECAY

    @pl.when(bounded)
    def _factored():
        for c in range(nch):
            for h in range(H):
                _gla_head_bounded(c * C, h, q_ref, v_ref, o_ref, st_ref, b_ref, kf_ref, scale)

    @pl.when(jnp.logical_not(bounded))
    def _exact():
        def chunk(c, carry):
            row0 = pl.multiple_of(c * C, C)
            bc_ref[...] = b_ref[pl.ds(row0, C), :]
            kc_ref[...] = kf_ref[pl.ds(row0, C), :]
            for h in range(H):
                _gla_head_exact(row0, h, q_ref, v_ref, o_ref, st_ref, bc_ref, kc_ref, scale)
            return carry
        lax.fori_loop(0, nch, chunk, 0)


def _gla(q, k, v, glr, w2p, gb, tril, *, B, S):
    C, H, dk, dv = GLA_CHUNK, GLA_HEADS, GLA_DK, GLA_DV
    rows = C * min(GLA_STEP_CHUNKS, S // C)
    nchunk = S // rows
    tok = lambda n: pl.BlockSpec((rows, n), lambda b, c: (b * nchunk + c, 0))
    const = lambda shape: pl.BlockSpec(shape, lambda b, c: (0,) * len(shape))
    return pl.pallas_call(
        functools.partial(_gla_kernel, scale=float(dk) ** -0.5),
        grid=(B, nchunk),
        in_specs=[tok(H * dk), tok(H * dk), tok(H * dv), tok(LANES),
                  const(w2p.shape), const(gb.shape), const(tril.shape)],
        out_specs=tok(H * dv),
        out_shape=jax.ShapeDtypeStruct((B * S, H * dv), BF16),
        scratch_shapes=[pltpu.VMEM((H, dv, dk), F32),
                        pltpu.VMEM((rows, H * dk), F32),
                        pltpu.VMEM((rows, H * dk), F32),
                        pltpu.VMEM((C, H * dk), F32),
                        pltpu.VMEM((C, H * dk), F32)],
        compiler_params=pltpu.CompilerParams(dimension_semantics=("parallel", "arbitrary"),
                                             vmem_limit_bytes=VMEM_LIMIT),
        name="gla",
    )(q, k, v, glr, w2p, gb, tril)


def _odd_out_kernel(x_ref, o_ref_in, z_ref, gn_ref, w_ref, g_ref, b_ref, out_ref, *, alpha):
    H, dv = GLA_HEADS, GLA_DV
    parts = []
    for h in range(H):
        sl = pl.ds(h * dv, dv)
        oh = o_ref_in[:, sl].astype(F32)
        r = lax.rsqrt(jnp.mean(oh * oh, axis=-1, keepdims=True) + LN_EPS)
        z = z_ref[:, sl].astype(F32)
        parts.append((oh * r * gn_ref[:, sl] * (z * _sigmoid(z))).astype(BF16))
    y = _dot(jnp.concatenate(parts, axis=1), w_ref[...])
    out_ref[...] = _layer_norm(alpha * x_ref[...] + y, g_ref[...], b_ref[...])


def _odd_out(x2d, o, z, gn_g, w_out, ln_g, ln_b, alpha, tm=1024):
    T, D = x2d.shape
    W = o.shape[1]
    row = lambda n: pl.BlockSpec((tm, n), lambda i: (i, 0))
    return pl.pallas_call(
        functools.partial(_odd_out_kernel, alpha=alpha),
        grid=(T // tm,),
        in_specs=[row(D), row(W), row(W),
                  pl.BlockSpec((1, W), lambda i: (0, 0)),
                  pl.BlockSpec((W, D), lambda i: (0, 0)),
                  pl.BlockSpec((1, D), lambda i: (0, 0)),
                  pl.BlockSpec((1, D), lambda i: (0, 0))],
        out_specs=row(D),
        out_shape=jax.ShapeDtypeStruct((T, D), F32),
        compiler_params=pltpu.CompilerParams(dimension_semantics=("parallel",),
                                             vmem_limit_bytes=VMEM_LIMIT),
        name="odd_out",
    )(x2d, o, z, gn_g, w_out, ln_g, ln_b)


def _key_features(S, block):
    s = np.arange(S)
    f = np.zeros((S, LANES), np.float32)
    f[s, s // block] = 1.0
    f[:, FEAT + 0] = 1.0
    f[:, FEAT + 1] = 1.0
    f[:, FEAT + 2] = FEAT * (s // FEAT)
    f[:, FEAT + 3] = s % FEAT
    return jnp.asarray(f, BF16)


def _pad_to(a, n, axis):
    pad = [(0, 0)] * a.ndim
    pad[axis] = (0, n - a.shape[axis])
    return jnp.pad(a, pad)


def _even_layer(x2d, B, S, w_in, cmp_pos, cmp_w1, cmp_b1, cmp_w2, w_out, ln_g, ln_b, alpha):
    T, D = x2d.shape
    G, hd = NSA_KV_GROUPS, HEAD_DIM
    scale = hd ** -0.5
    wq, wkv, wgate, wnz, wmq, wmk, wmv, wmz = jnp.split(w_in, np.cumsum(EVEN_SPLITS)[:-1].tolist(), axis=1)
    wkc, wvc, wks, wvs, wkw, wvw = jnp.split(wkv, 6, axis=1)
    w_tok = jnp.concatenate([wkc, wvc, wks, wkw, wmk], axis=1).astype(BF16)
    w_tr = jnp.concatenate([wq * scale, wvs, wvw, wmq * scale, wmv, wnz, wmz,
                            _pad_to(wgate, GATE_ROWS, 1)], axis=1).T.astype(BF16)
    (kcmp, vcmp, ksel, kwin, mk, nq_t, vsel_t, vwin_t, mq_t, mv_t, nz_t, mz_t, gate_t) = _proj(
        x2d, w_tok, (NSA_KV_W,) * 4 + (MOBA_W,), (F32, F32, BF16, BF16, BF16),
        w_tr, (NSA_W, NSA_KV_W, NSA_KV_W, MOBA_W, MOBA_W, NSA_W, MOBA_W, GATE_ROWS),
        (BF16,) * 7 + (F32,))

    L, d, HID = NSA_CMP_BLOCK, NSA_CMP_STRIDE, NSA_CMP_HIDDEN
    ncmp = (S - L) // d + 1
    NC = S // d
    eye = jnp.eye(G, dtype=F32)
    w1r = cmp_w1.reshape(2, 2, d, hd, HID)
    w1p = jnp.einsum('ialdj,gh->ialgdhj', w1r, eye).reshape(2, 2, d * G * hd, G * HID).astype(BF16)
    posp = jnp.broadcast_to(cmp_pos.reshape(2, 2, d, 1, hd), (2, 2, d, G, hd)).reshape(2, 2, 1, d * G * hd)
    b1p = jnp.tile(cmp_b1, (1, G)).reshape(2, 1, G * HID)
    w2p = jnp.einsum('ijd,gh->igjhd', _pad_to(cmp_w2, LANES, 2), eye).reshape(2, G * HID, G * LANES).astype(BF16)
    cfeat = np.zeros((NC, G, LANES), np.float32)
    cfeat[:, :, hd + 0] = 1.0
    cfeat[:, :, hd + 1] = 1.0
    cfeat[:, :, hd + 2] = (np.arange(NC) * d)[:, None]
    cfeat[:, :, hd + 3] = L - 1
    kvc = _cmp_mlp(kcmp, vcmp, w1p, posp, b1p, w2p, jnp.asarray(cfeat.reshape(NC, G * LANES)), ncmp,
                   B=B, S=S)

    ns = S // NSA_SEL_BLOCK
    cmp_start = np.arange(NC) * d
    sel_start = np.arange(FEAT) * NSA_SEL_BLOCK
    ovt = ((cmp_start[None, :] <= sel_start[:, None] + NSA_SEL_BLOCK - 1)
           & (cmp_start[None, :] + L - 1 >= sel_start[:, None])
           & (np.arange(NC)[None, :] < ncmp) & (np.arange(FEAT)[:, None] < ns))
    o_cmp, bias_sel = _cmp_attn(nq_t, kvc, jnp.asarray(ovt, BF16), B=B, S=S)

    feats_sel = _key_features(S, NSA_SEL_BLOCK)
    o_sel = _flash(nq_t, bias_sel, ksel, feats_sel, vsel_t,
                   B=B, S=S, units=G, tq=TQ, nh=NSA_HG, pair=False, window=None)
    o_win = _flash(nq_t, None, kwin, feats_sel, vwin_t,
                   B=B, S=S, units=G, tq=TQ, nh=NSA_HG, pair=False, window=NSA_WINDOW)

    avg = np.zeros((MOBA_NBP, S), np.float32)
    avg[np.arange(S) // MOBA_BLOCK, np.arange(S)] = 1.0 / MOBA_BLOCK
    bias_m = _moba_gate(mq_t, mk, jnp.asarray(avg, BF16), B=B, S=S)
    o_m = _flash(mq_t, bias_m, mk, _key_features(S, MOBA_BLOCK), mv_t,
                 B=B, S=S, units=MOBA_HEADS // MOBA_STEP_HEADS, tq=TQM, nh=MOBA_STEP_HEADS, pair=True, window=None)

    return _even_out(x2d, o_cmp, o_sel, o_win, o_m, gate_t, nz_t, mz_t, w_out.astype(BF16),
                     ln_g.reshape(1, D), ln_b.reshape(1, D), alpha, B=B, S=S)


def _odd_layer(x2d, B, S, w_in, gate_w2, gate_b, gn_g, w_out, ln_g, ln_b, alpha):
    T, D = x2d.shape
    H, dk, dv, C = GLA_HEADS, GLA_DK, GLA_DV, GLA_CHUNK
    wq, wk, wv, wg, wz = jnp.split(w_in, np.cumsum(ODD_SPLITS)[:-1].tolist(), axis=1)
    w_perm = jnp.concatenate([wq, wk, wv, wz, _pad_to(wg, LANES, 1)], axis=1).astype(BF16)
    q, k, v, z, glr = _proj(x2d, w_perm, (H * dk, H * dk, H * dv, H * dv, LANES),
                            (BF16, BF16, BF16, BF16, F32))
    w2p = _pad_to(gate_w2, LANES, 0)
    ncs = min(GLA_CUMSUM_CHUNKS, GLA_STEP_CHUNKS, S // C)
    tril = jnp.asarray(np.kron(np.eye(ncs), np.tril(np.ones((C, C)))), BF16)
    o = _gla(q, k, v, glr, w2p, gate_b.reshape(1, H * dk), tril, B=B, S=S)
    return _odd_out(x2d, o, z, gn_g.reshape(1, H * dv), w_out.astype(BF16),
                    ln_g.reshape(1, D), ln_b.reshape(1, D), alpha)


def kernel(x, ev_w_in, ev_cmp_pos, ev_cmp_w1, ev_cmp_b1, ev_cmp_w2, ev_w_out, ev_ln_g, ev_ln_b,
           od_w_in, od_gate_w2, od_gate_b, od_gn_g, od_w_out, od_ln_g, od_ln_b):
    B, S, D = x.shape
    depth = ev_w_in.shape[0] + od_w_in.shape[0]
    alpha = (2.0 * depth) ** 0.25
    h = x.reshape(B * S, D)
    for layer in range(depth):
        i = layer // 2
        if layer % 2 == 0:
            h = _even_layer(h, B, S, ev_w_in[i], ev_cmp_pos[i], ev_cmp_w1[i], ev_cmp_b1[i], ev_cmp_w2[i],
                            ev_w_out[i], ev_ln_g[i], ev_ln_b[i], alpha)
        else:
            h = _odd_layer(h, B, S, od_w_in[i], od_gate_w2[i], od_gate_b[i], od_gn_g[i],
                           od_w_out[i], od_ln_g[i], od_ln_b[i], alpha)
    return h.reshape(B, S, D)
```

```python
import functools

import numpy as np
import jax
import jax.numpy as jnp
from jax import lax
from jax.experimental import pallas as pl
from jax.experimental.pallas import tpu as pltpu

F32 = jnp.float32
BF16 = jnp.bfloat16

HEAD_DIM = 64
NSA_HEADS = 8
NSA_KV_GROUPS = 2
NSA_HG = NSA_HEADS // NSA_KV_GROUPS
NSA_CMP_BLOCK = 32
NSA_CMP_STRIDE = 16
NSA_CMP_HIDDEN = 128
NSA_SEL_BLOCK = 64
NSA_TOP_N = 16
NSA_WINDOW = 512
MOBA_HEADS = 8
MOBA_BLOCK = 256
MOBA_TOP_K = 3
GLA_HEADS = 4
GLA_DK = 128
GLA_DV = 256
GLA_GATE_RANK = 16
GLA_TAU = 16.0
GLA_CHUNK = 64
GLA_SUB = 16
GLA_STEP_CHUNKS = 16
GLA_CUMSUM_CHUNKS = 4
GLA_FACTOR_MAX_DECAY = 60.0
LN_EPS = 1e-5
NEG = -1e30
FORCE_BONUS = 1e4

NSA_W = NSA_HEADS * HEAD_DIM
NSA_KV_W = NSA_KV_GROUPS * HEAD_DIM
MOBA_W = MOBA_HEADS * HEAD_DIM
EVEN_SPLITS = (NSA_W, 6 * NSA_KV_W, 3 * NSA_HEADS, NSA_W, MOBA_W, MOBA_W, MOBA_W, MOBA_W)
ODD_SPLITS = (GLA_HEADS * GLA_DK, GLA_HEADS * GLA_DK, GLA_HEADS * GLA_DV, GLA_GATE_RANK, GLA_HEADS * GLA_DV)

LANES = 128
SUBLANES = 8
FEAT = 64
ALIBI_ROWS = 16
GATE_ROWS = 32
MOBA_NBP = 16
TQ = 512
TQM = 512
MOBA_STEP_HEADS = 4
TK = 512
CG = 256
MIN_WEIGHT_SUM = 1e-25
VROWS = HEAD_DIM + 16
VMEM_LIMIT = 48 * 1024 * 1024


def _dot(a, b):
    return jnp.dot(a, b, preferred_element_type=F32)


def _dot_nt(a, b):
    return lax.dot_general(a, b, (((1,), (1,)), ((), ())), preferred_element_type=F32)


def _dot_tn(a, b):
    return lax.dot_general(a, b, (((0,), (0,)), ((), ())), preferred_element_type=F32)


def _hilo(a):
    hi = a.astype(BF16)
    lo = (a - hi.astype(F32)).astype(BF16)
    return hi, lo


def _pow2_neg(n):
    return lax.bitcast_convert_type(lax.shift_left(127 - n, 23), F32)


def _sigmoid(x):
    return 0.5 * jnp.tanh(0.5 * x) + 0.5


def _layer_norm(r, g, b):
    mu = jnp.mean(r, axis=-1, keepdims=True)
    d = r - mu
    var = jnp.mean(d * d, axis=-1, keepdims=True)
    return d * lax.rsqrt(var + LN_EPS) * g + b


def _alibi_query_rows(t0, tq, head0, R):
    r = lax.broadcasted_iota(jnp.int32, (ALIBI_ROWS, R), 0)
    col = lax.broadcasted_iota(jnp.int32, (ALIBI_ROWS, R), 1)
    t = t0 + col % tq
    slope = _pow2_neg(head0 + col // tq + 1)
    feats = jnp.where(r == 0, -FEAT * slope * (t // FEAT).astype(F32),
                      jnp.where(r == 1, -slope * (t % FEAT).astype(F32), jnp.where(r < 4, slope, 0.0)))
    return jnp.concatenate([feats.astype(BF16), jnp.zeros((FEAT - ALIBI_ROWS, R), BF16)], axis=0)


def _ranks(sc_ref, n_real, n_rows=None, cols=slice(None)):
    n_rows = sc_ref.shape[0] if n_rows is None else n_rows
    out = []
    for v in range(n_rows // SUBLANES):
        lo = v * SUBLANES
        sc = sc_ref[lo:lo + SUBLANES, cols]
        width = sc.shape[1]
        idx = lo + lax.broadcasted_iota(jnp.int32, (SUBLANES, width), 0)
        rank = jnp.zeros((SUBLANES, width), jnp.int32)
        for jp in range(n_real):
            r = sc_ref[pl.ds(jp, 1), cols]
            if jp < lo:
                beats = r >= sc
            elif jp >= lo + SUBLANES - 1:
                beats = r > sc
            else:
                beats = (r > sc) | ((r == sc) & (idx > jp))
            rank = rank + jnp.where(beats, 1, 0)
        out.append(rank)
    return jnp.concatenate(out, axis=0)


def _proj_kernel(*refs, tok_splits, tr_splits):
    x_ref, wt_ref = refs[0], refs[1]
    outs = list(refs[3:] if tr_splits else refs[2:])
    x = x_ref[...]
    xb = x.astype(BF16)
    off = 0
    for n in tok_splits:
        o_ref = outs.pop(0)
        o_ref[...] = _dot(xb, wt_ref[:, off:off + n]).astype(o_ref.dtype)
        off += n
    if tr_splits:
        wtr_ref = refs[2]
        xt = x.T.astype(BF16)
        off = 0
        for n in tr_splits:
            o_ref = outs.pop(0)
            o_ref[...] = _dot(wtr_ref[off:off + n, :], xt).astype(o_ref.dtype)
            off += n


def _proj(x2d, w_tok, tok_splits, tok_dtypes, w_tr=None, tr_splits=(), tr_dtypes=(), tm=512):
    T, D = x2d.shape
    assert T % tm == 0 and sum(tok_splits) == w_tok.shape[1]
    in_specs = [pl.BlockSpec((tm, D), lambda i: (i, 0)),
                pl.BlockSpec(w_tok.shape, lambda i: (0, 0))]
    args = [x2d, w_tok]
    if tr_splits:
        assert sum(tr_splits) == w_tr.shape[0]
        in_specs.append(pl.BlockSpec(w_tr.shape, lambda i: (0, 0)))
        args.append(w_tr)
    out_specs = ([pl.BlockSpec((tm, n), lambda i: (i, 0)) for n in tok_splits]
                 + [pl.BlockSpec((n, tm), lambda i: (0, i)) for n in tr_splits])
    out_shape = ([jax.ShapeDtypeStruct((T, n), dt) for n, dt in zip(tok_splits, tok_dtypes)]
                 + [jax.ShapeDtypeStruct((n, T), dt) for n, dt in zip(tr_splits, tr_dtypes)])
    return pl.pallas_call(
        functools.partial(_proj_kernel, tok_splits=tuple(tok_splits), tr_splits=tuple(tr_splits)),
        grid=(T // tm,), in_specs=in_specs, out_specs=out_specs, out_shape=out_shape,
        compiler_params=pltpu.CompilerParams(dimension_semantics=("parallel",),
                                             vmem_limit_bytes=VMEM_LIMIT),
        name="proj",
    )(*args)


def _cmp_mlp_kernel(kx_ref, vx_ref, w1_ref, pos_ref, b1_ref, w2_ref, cf_ref, o_ref, *, ncmp):
    d = NSA_CMP_STRIDE
    nc = kx_ref.shape[0] // d
    cw = w1_ref.shape[2]
    for i, x_ref in enumerate((kx_ref, vx_ref)):
        a = bm = None
        for l in range(0, d, 2):
            x2 = jnp.concatenate([x_ref[pl.ds(l, nc, stride=d), :], x_ref[pl.ds(l + 1, nc, stride=d), :]],
                                 axis=1).astype(BF16)
            rows = pl.ds(l * x_ref.shape[1], 2 * x_ref.shape[1])
            a_l, b_l = _dot(x2, w1_ref[i, 0, rows, :]), _dot(x2, w1_ref[i, 1, rows, :])
            a, bm = (a_l, b_l) if a is None else (a + a_l, bm + b_l)
        c = b1_ref[i]
        for half in range(2):
            phi, plo = _hilo(jnp.broadcast_to(pos_ref[i, half], (SUBLANES, cw)))
            c = c + (_dot(phi, w1_ref[i, half]) + _dot(plo, w1_ref[i, half]))[0:1]
        hid = a + pltpu.roll(bm, nc - 1, axis=0) + c
        hid = hid * _sigmoid(hid)
        row = lax.broadcasted_iota(jnp.int32, hid.shape, 0)
        hid = jnp.where(row < ncmp, hid, 0.0)
        out = _dot(hid.astype(BF16), w2_ref[i])
        if i == 0:
            out = out + cf_ref[...]
        o_ref[i] = out.astype(o_ref.dtype)


def _cmp_mlp(kx, vx, w1p, posp, b1p, w2p, cfeat, ncmp, *, B, S):
    NC = S // NSA_CMP_STRIDE
    OW = w2p.shape[-1]
    full = lambda a: pl.BlockSpec(a.shape, lambda b: (0,) * a.ndim)
    return pl.pallas_call(
        functools.partial(_cmp_mlp_kernel, ncmp=ncmp),
        grid=(B,),
        in_specs=[pl.BlockSpec((S, kx.shape[1]), lambda b: (b, 0)),
                  pl.BlockSpec((S, vx.shape[1]), lambda b: (b, 0)),
                  full(w1p), full(posp), full(b1p), full(w2p), full(cfeat)],
        out_specs=pl.BlockSpec((2, None, NC, OW), lambda b: (0, b, 0, 0)),
        out_shape=jax.ShapeDtypeStruct((2, B, NC, OW), BF16),
        compiler_params=pltpu.CompilerParams(dimension_semantics=("parallel",),
                                             vmem_limit_bytes=VMEM_LIMIT),
        name="cmp_mlp",
    )(kx, vx, w1p, posp, b1p, w2p, cfeat)


def _cmp_attn_kernel(q_ref, kc_ref, vc_ref, ovt_ref, o_ref, bias_ref, sc_ref, *, tq, nh, ns, ntop):
    g = pl.program_id(1)
    R = nh * tq
    nq = q_ref.shape[1] // tq

    def q_tile(qi):
        toks = pl.ds(qi * tq, tq)
        nc = min(kc_ref.shape[0], -(-((qi + 1) * tq) // (NSA_CMP_STRIDE * 16)) * 16)
        nsel = min(ns, -(-((qi + 1) * tq) // (NSA_SEL_BLOCK * SUBLANES)) * SUBLANES)
        qcat = jnp.concatenate([q_ref[hh * HEAD_DIM:(hh + 1) * HEAD_DIM, toks] for hh in range(nh)], axis=1)
        qa = jnp.concatenate([qcat, _alibi_query_rows(qi * tq, tq, g * nh, R)], axis=0)
        s = _dot(kc_ref[0:nc, :], qa)
        n = lax.broadcasted_iota(jnp.int32, (nc, R), 0)
        t_row = qi * tq + lax.broadcasted_iota(jnp.int32, (1, R), 1) % tq
        last = (t_row - (NSA_CMP_BLOCK - 1)) // NSA_CMP_STRIDE
        s = jnp.where(n <= last, s, NEG)
        e = jnp.exp(s - jnp.max(s, axis=0, keepdims=True))
        inv = jnp.where(last >= 0, 1.0 / jnp.sum(e, axis=0, keepdims=True), 0.0)
        p = e * inv
        o_ref[:, pl.ds(qi * R, R)] = _dot_tn(vc_ref[0:nc, 0:HEAD_DIM], p.astype(BF16)).astype(o_ref.dtype)

        psum = p[:, 0:tq]
        for hh in range(1, nh):
            psum = psum + p[:, hh * tq:(hh + 1) * tq]
        phi, plo = _hilo(psum)
        ovt = ovt_ref[0:nsel, 0:nc]
        imp = _dot(ovt, phi) + _dot(ovt, plo)

        j = lax.broadcasted_iota(jnp.int32, (nsel, tq), 0)
        cur = (qi * tq + lax.broadcasted_iota(jnp.int32, (nsel, tq), 1)) // NSA_SEL_BLOCK
        if nsel <= ntop:
            sel = j <= cur
        else:
            forced = jnp.where((j == 0) | (j == cur) | (j == cur - 1), FORCE_BONUS, 0.0)
            sc_ref[0:nsel, :] = jnp.where(j <= cur, imp + forced, NEG)
            sel = (_ranks(sc_ref, nsel, nsel) < ntop) & (j <= cur)
        bias_ref[0:nsel, toks] = jnp.where(sel, 0.0, NEG).astype(bias_ref.dtype)
        if nsel < FEAT:
            bias_ref[nsel:FEAT, toks] = jnp.full((FEAT - nsel, tq), NEG, bias_ref.dtype)

    for qi in range(nq):
        q_tile(qi)


def _cmp_attn(nq_t, kvc, ovt, *, B, S):
    tq, nh, G = TQ, NSA_HG, NSA_KV_GROUPS
    R = nh * tq
    nq = S // tq
    NC = kvc.shape[2]
    ns = S // NSA_SEL_BLOCK
    assert ns <= FEAT
    ntop = min(NSA_TOP_N, ns)
    return pl.pallas_call(
        functools.partial(_cmp_attn_kernel, tq=tq, nh=nh, ns=ns, ntop=ntop),
        grid=(B, G),
        in_specs=[pl.BlockSpec((nh * HEAD_DIM, S), lambda b, g: (g, b)),
                  pl.BlockSpec((None, None, NC, LANES), lambda b, g: (0, b, 0, g)),
                  pl.BlockSpec((None, None, NC, LANES), lambda b, g: (1, b, 0, g)),
                  pl.BlockSpec(ovt.shape, lambda b, g: (0, 0))],
        out_specs=[pl.BlockSpec((None, None, HEAD_DIM, S * nh), lambda b, g: (b, g, 0, 0)),
                   pl.BlockSpec((None, None, FEAT, S), lambda b, g: (b, g, 0, 0))],
        out_shape=[jax.ShapeDtypeStruct((B, G, HEAD_DIM, S * nh), BF16),
                   jax.ShapeDtypeStruct((B, G, FEAT, S), BF16)],
        scratch_shapes=[pltpu.VMEM((FEAT, tq), F32)],
        compiler_params=pltpu.CompilerParams(dimension_semantics=("parallel", "parallel"),
                                             vmem_limit_bytes=VMEM_LIMIT),
        name="cmp_attn",
    )(nq_t, kvc, kvc, ovt)


def _moba_gate_kernel(q_ref, k_ref, a_ref, bias_ref, sc_ref, *, nb, ntop):
    nbp, S = sc_ref.shape[1:]
    km = _dot(a_ref[...], k_ref[...])
    lane = lax.broadcasted_iota(jnp.int32, (nbp, LANES), 1)
    parts = []
    for hh in range(2):
        parts += _hilo(jnp.where(lane // HEAD_DIM == hh, km, 0.0))
    g = _dot(jnp.concatenate(parts, axis=0), q_ref[...])
    past = (lax.broadcasted_iota(jnp.int32, (nbp, S), 0)
            < lax.broadcasted_iota(jnp.int32, (nbp, S), 1) // MOBA_BLOCK)
    n = lax.broadcasted_iota(jnp.int32, (nbp, MOBA_BLOCK), 0)
    for hh in range(2):
        sc = sc_ref.at[hh]
        sc[...] = jnp.where(past, g[2 * hh * nbp:(2 * hh + 1) * nbp] + g[(2 * hh + 1) * nbp:(2 * hh + 2) * nbp], NEG)
        for c in range(nb):
            cols = slice(c * MOBA_BLOCK, (c + 1) * MOBA_BLOCK)
            sel = n <= c
            if c > ntop:
                nr = -(-c // SUBLANES) * SUBLANES
                rank = _ranks(sc, c, nr, cols)
                rank = rank + jnp.where(sc[0:nr, cols] < NEG, nb - c, 0)
                if nr < nbp:
                    rank = jnp.concatenate([rank, jnp.full((nbp - nr, MOBA_BLOCK), nb, jnp.int32)], axis=0)
                sel = ((rank < ntop) & (n < c)) | (n == c)
            bias_ref[hh, 0:nbp, cols] = jnp.where(sel, 0.0, NEG).astype(bias_ref.dtype)
        bias_ref[hh, nbp:FEAT, :] = jnp.zeros((FEAT - nbp, S), bias_ref.dtype)


def _moba_gate(mq_t, mk, avg, *, B, S):
    H = MOBA_HEADS
    nb = S // MOBA_BLOCK
    nbp = avg.shape[0]
    assert nb <= nbp and S % MOBA_BLOCK == 0 and MOBA_BLOCK % LANES == 0 and H % 2 == 0
    ntop = min(MOBA_TOP_K, nb)
    return pl.pallas_call(
        functools.partial(_moba_gate_kernel, nb=nb, ntop=ntop),
        grid=(B, H // 2),
        in_specs=[pl.BlockSpec((2 * HEAD_DIM, S), lambda b, h: (h, b)),
                  pl.BlockSpec((S, LANES), lambda b, h: (b, h)),
                  pl.BlockSpec(avg.shape, lambda b, h: (0, 0))],
        out_specs=pl.BlockSpec((None, 2, FEAT, S), lambda b, h: (b, h, 0, 0)),
        out_shape=jax.ShapeDtypeStruct((B, H, FEAT, S), BF16),
        scratch_shapes=[pltpu.VMEM((2, nbp, S), F32)],
        compiler_params=pltpu.CompilerParams(dimension_semantics=("parallel", "parallel"),
                                             vmem_limit_bytes=VMEM_LIMIT),
        name="moba_gate",
    )(mq_t, mk, avg)


def _flash_q_tile(qi, running_max, q_ref, bias_ref, mask_ref, o_ref, ka_ref, vta_ref, kn_ref, qa_ref, bnd_ref,
                  m_ref, acc_ref, alpha_ref, p_ref, *, tq, tk, nh, pair, window, has_past=None):
    u = pl.program_id(1)
    R = nh * tq
    static = isinstance(qi, int)
    q0 = qi * tq if static else pl.multiple_of(qi * tq, tq)
    toks = pl.ds(q0, tq)

    qcat = jnp.concatenate([q_ref[hh * HEAD_DIM:(hh + 1) * HEAD_DIM, toks] for hh in range(nh)], axis=1)
    qf = qcat.astype(F32)
    qq = jnp.concatenate([qf, qf], axis=0)
    rowi = lax.broadcasted_iota(jnp.int32, (2 * HEAD_DIM, R), 0)
    coli = lax.broadcasted_iota(jnp.int32, (2 * HEAD_DIM, R), 1)
    half = (coli // tq) % 2 if pair else u % 2
    qa_ref[0:2 * HEAD_DIM, :] = jnp.where(rowi // HEAD_DIM == half, qq, 0.0).astype(BF16)
    if bias_ref is None:
        qa_ref[2 * HEAD_DIM:2 * HEAD_DIM + FEAT, :] = jnp.zeros((FEAT, R), BF16)
    else:
        qa_ref[2 * HEAD_DIM:2 * HEAD_DIM + FEAT, :] = jnp.concatenate(
            [bias_ref[hh if pair else 0, :, toks] for hh in range(nh)], axis=1)
    qa_ref[2 * HEAD_DIM + FEAT:2 * LANES, :] = _alibi_query_rows(q0, tq, u * nh, R)
    if pair:
        kn = jnp.concatenate([kn_ref[hh:hh + 1, :] for hh in range(nh) for _ in range(tq // LANES)], axis=1)
    else:
        kn = jnp.concatenate([jnp.where(u % 2 == 0, kn_ref[0:1, :], kn_ref[1:2, :])] * (R // LANES), axis=1)
    qn = jnp.sqrt(jnp.sum(qf * qf, axis=0, keepdims=True))
    bnd_ref[...] = jnp.broadcast_to(qn * kn, bnd_ref.shape)

    ncg = R // CG
    kd = (q0 + tq - 1) // tk

    def key_rows(kind, c):
        off = (c * CG) % tq
        if kind is None:
            return 0, tk, None
        if kind == "diag":
            return 0, off + CG, off // CG
        return off, tk, (tk + off) // CG

    def values(ki, kind=None):
        for c in range(ncg):
            cols = pl.ds(c * CG, CG)
            r0, r1, _ = key_rows(kind, c)
            v = (c * CG) // tq if pair else 0
            pv = _dot(vta_ref[v, ki, :, r0:r1], p_ref[r0:r1, cols])
            if running_max:
                acc_ref[:, cols] = alpha_ref[0:1, cols] * acc_ref[:, cols] + pv
            else:
                acc_ref[:, cols] = acc_ref[:, cols] + pv

    def scores(ki, kind=None, prev=None, prev_kind=None):
        k0 = ki * tk if isinstance(ki, int) else pl.multiple_of(ki * tk, tk)
        s_groups = []
        for c in range(ncg):
            r0, r1, _ = key_rows(kind, c)
            kb = (c * CG) // (2 * tq) if pair else 0
            s_groups.append(_dot(ka_ref[kb, pl.ds(k0 + r0, r1 - r0), :], qa_ref[:, pl.ds(c * CG, CG)]))
        if prev is not None:
            values(prev, prev_kind)
        for c, s in enumerate(s_groups):
            cols = pl.ds(c * CG, CG)
            r0, r1, entry = key_rows(kind, c)
            if entry is not None:
                s = s + mask_ref[entry, r0:r1, :]
            if running_max:
                m_prev = m_ref[0:1, cols]
                m_new = jnp.maximum(m_prev, jnp.max(s, axis=0, keepdims=True))
                alpha_ref[:, cols] = jnp.broadcast_to(jnp.exp(m_prev - m_new), (alpha_ref.shape[0], CG))
                m_ref[:, cols] = jnp.broadcast_to(m_new, (m_ref.shape[0], CG))
                p_ref[r0:r1, cols] = jnp.exp(s - m_new).astype(BF16)
            else:
                p_ref[r0:r1, cols] = jnp.exp(s - bnd_ref[0:1, cols]).astype(BF16)

    def steady(lo, hi):
        def pair_body(j, carry):
            ki = lo + 2 * j
            scores(ki, prev=ki - 1)
            scores(ki + 1, prev=ki)
            return carry
        lax.fori_loop(0, (hi - lo) // 2, pair_body, 0)

        @pl.when((hi - lo) % 2 == 1)
        def _odd_one():
            scores(hi - 1, prev=hi - 2)

    def only_diagonal():
        scores(0, "diag")
        values(0, "diag")

    def past_then_diagonal():
        if window is not None:
            scores(kd - 1, "wedge")
            scores(kd, "diag", prev=kd - 1, prev_kind="wedge")
        elif static:
            scores(0)
            for ki in range(1, kd):
                scores(ki, prev=ki - 1)
            scores(kd, "diag", prev=kd - 1)
        else:
            scores(0)
            steady(1, kd)
            scores(kd, "diag", prev=kd - 1)
        values(kd, "diag")

    acc_ref[...] = jnp.zeros(acc_ref.shape, F32)
    if running_max:
        m_ref[...] = jnp.full(m_ref.shape, NEG, F32)
    if static:
        only_diagonal() if kd == 0 else past_then_diagonal()
    elif has_past:
        past_then_diagonal()
    else:
        pl.when(kd == 0)(only_diagonal)
        pl.when(kd > 0)(past_then_diagonal)

    acc = acc_ref[...]
    out = (acc[0:HEAD_DIM] / acc[HEAD_DIM:HEAD_DIM + 1]).astype(o_ref.dtype)
    if pair:
        for hh in range(nh):
            o_ref[hh, :, toks] = out[:, hh * tq:(hh + 1) * tq]
    else:
        o_ref[0, :, pl.ds(qi * R if static else pl.multiple_of(qi * R, R), R)] = out
    return jnp.min(acc[HEAD_DIM:HEAD_DIM + 1, :])


def _flash_kernel(*refs, tq, tk, nh, pair, window, has_bias):
    q_ref = refs[0]
    bias_ref = refs[1] if has_bias else None
    (k_ref, kf_ref, vt_ref, mask_ref, o_ref, ka_ref, vta_ref, kn_ref, qa_ref, bnd_ref, m_ref, acc_ref,
     alpha_ref, p_ref, low_ref) = refs[2 if has_bias else 1:]
    S = k_ref.shape[0]
    for kb in range(ka_ref.shape[0]):
        k = k_ref[:, kb * LANES:(kb + 1) * LANES]
        ka_ref[kb, :, 0:LANES] = k
        ka_ref[kb, :, LANES:2 * LANES] = kf_ref[...]
        k2 = jnp.square(k.astype(F32))
        lane = lax.broadcasted_iota(jnp.int32, k2.shape, 1)
        for hf in range(2):
            n2 = jnp.sum(jnp.where(lane // HEAD_DIM == hf, k2, 0.0), axis=1, keepdims=True)
            kn_ref[2 * kb + hf:2 * kb + hf + 1, :] = jnp.broadcast_to(
                jnp.sqrt(jnp.max(n2, axis=0, keepdims=True)), (1, LANES))
    ones_row = jnp.where(lax.broadcasted_iota(jnp.int32, (VROWS - HEAD_DIM, tk), 0) == 0, 1.0, 0.0)
    for v in range(nh if pair else 1):
        for jt in range(S // tk):
            vta_ref[v, jt, 0:HEAD_DIM, :] = vt_ref[v * HEAD_DIM:(v + 1) * HEAD_DIM, jt * tk:(jt + 1) * tk]
            vta_ref[v, jt, HEAD_DIM:VROWS, :] = ones_row.astype(BF16)

    tile = functools.partial(_flash_q_tile, q_ref=q_ref, bias_ref=bias_ref, mask_ref=mask_ref, o_ref=o_ref,
                             ka_ref=ka_ref, vta_ref=vta_ref, kn_ref=kn_ref, qa_ref=qa_ref, bnd_ref=bnd_ref,
                             m_ref=m_ref, acc_ref=acc_ref, alpha_ref=alpha_ref, p_ref=p_ref,
                             tq=tq, tk=tk, nh=nh, pair=pair, window=window)
    nq = S // tq
    low_ref[0] = tile(0, False)

    def bound_stabilised(qi, carry):
        low_ref[qi] = tile(qi, False, has_past=True)
        return carry
    lax.fori_loop(1, nq, bound_stabilised, 0)

    def redo(qi, carry):
        @pl.when(low_ref[qi] < MIN_WEIGHT_SUM)
        def _running_max():
            tile(qi, True)
        return carry

    lax.fori_loop(0, nq, redo, 0)


def _mask_table(tk, window):
    n_delta = (tk + (window or 0)) // CG
    d = np.arange(n_delta)[:, None, None] * CG + np.arange(CG)[None, None, :] - np.arange(tk)[None, :, None]
    ok = (d >= 0) & ((d < window) if window else True)
    return jnp.asarray(np.where(ok, 0.0, NEG), F32)


def _flash(q_t, bias_t, k, kfeat, v_t, *, B, S, units, tq, nh, pair, window):
    tk = TK
    R = nh * tq
    nq = S // tq
    nv = nh if pair else 1
    nb = nh if pair else 1
    nkb = nh // 2 if pair else 1
    assert tq == tk and S % tk == 0 and tk % CG == 0 and window in (None, tk)
    mask = _mask_table(tk, window)
    kcol = (lambda u: u) if pair else (lambda u: u // 2)
    out_w = tq if pair else R
    out_shape = (B, units * nb, HEAD_DIM, S if pair else S * nh)
    has_bias = bias_t is not None
    in_specs = [pl.BlockSpec((nh * HEAD_DIM, S), lambda b, u: (u, b))]
    if has_bias:
        in_specs.append(pl.BlockSpec((None, nb, FEAT, S), lambda b, u: (b, u, 0, 0)))
    in_specs += [pl.BlockSpec((S, nkb * LANES), lambda b, u: (b, kcol(u))),
                 pl.BlockSpec((S, LANES), lambda b, u: (0, 0)),
                 pl.BlockSpec((nv * HEAD_DIM, S), lambda b, u: (u, b)),
                 pl.BlockSpec(mask.shape, lambda b, u: (0, 0, 0))]
    args = (q_t,) + ((bias_t,) if has_bias else ()) + (k, kfeat, v_t, mask)
    return pl.pallas_call(
        functools.partial(_flash_kernel, tq=tq, tk=tk, nh=nh, pair=pair, window=window, has_bias=has_bias),
        grid=(B, units),
        in_specs=in_specs,
        out_specs=pl.BlockSpec((None, nb, HEAD_DIM, out_w * nq), lambda b, u: (b, u, 0, 0)),
        out_shape=jax.ShapeDtypeStruct(out_shape, BF16),
        scratch_shapes=[pltpu.VMEM((nkb, S, 2 * LANES), BF16),
                        pltpu.VMEM((nv, S // tk, VROWS, tk), BF16),
                        pltpu.VMEM((SUBLANES, LANES), F32),
                        pltpu.VMEM((2 * LANES, R), BF16),
                        pltpu.VMEM((SUBLANES, R), F32),
                        pltpu.VMEM((SUBLANES, R), F32),
                        pltpu.VMEM((VROWS, R), F32),
                        pltpu.VMEM((SUBLANES, R), F32),
                        pltpu.VMEM((tk, R), BF16),
                        pltpu.SMEM((nq,), F32)],
        compiler_params=pltpu.CompilerParams(dimension_semantics=("parallel", "parallel"),
                                             vmem_limit_bytes=VMEM_LIMIT),
        name="flash",
    )(*args)


def _even_out_kernel(x_ref, ocmp_ref, osel_ref, owin_ref, om_ref, gate_ref, nz_ref, mz_ref,
                     w_ref, g_ref, b_ref, o_ref, sig_ref, *, alpha, tm):
    G, nh = NSA_KV_GROUPS, NSA_HG
    sig_ref[...] = _sigmoid(gate_ref[...])
    heads = []
    for g in range(G):
        for hh in range(nh):
            acc = None
            for br, ref in enumerate((ocmp_ref, osel_ref, owin_ref)):
                gate = sig_ref[pl.ds(3 * (g * nh + hh) + br, 1), :]
                term = gate * ref[g, :, hh * tm:(hh + 1) * tm].astype(F32)
                acc = term if acc is None else acc + term
            heads.append(acc)
    o_nsa = jnp.concatenate(heads, axis=0)
    nz = nz_ref[...].astype(F32)
    mz = mz_ref[...].astype(F32)
    o_m = jnp.concatenate([om_ref[h] for h in range(MOBA_HEADS)], axis=0).astype(F32)
    a_t = jnp.concatenate([o_nsa * (nz * _sigmoid(nz)), o_m * (mz * _sigmoid(mz))], axis=0)
    y = _dot_tn(a_t.astype(BF16), w_ref[...])
    o_ref[...] = _layer_norm(alpha * x_ref[...] + y, g_ref[...], b_ref[...])


def _even_out(x2d, ocmp, osel, owin, om, gate_t, nz_t, mz_t, w_out, ln_g, ln_b, alpha, *, B, S):
    T, D = x2d.shape
    tm = TQ
    nq = S // tm
    G, nh, H = NSA_KV_GROUPS, NSA_HG, MOBA_HEADS
    nsa = pl.BlockSpec((None, G, HEAD_DIM, nh * tm), lambda i: (i // nq, 0, 0, i % nq))
    feat = lambda n: pl.BlockSpec((n, tm), lambda i: (0, i))
    full = lambda a: pl.BlockSpec(a.shape, lambda i: (0,) * a.ndim)
    return pl.pallas_call(
        functools.partial(_even_out_kernel, alpha=alpha, tm=tm),
        grid=(T // tm,),
        in_specs=[pl.BlockSpec((tm, D), lambda i: (i, 0)), nsa, nsa, nsa,
                  pl.BlockSpec((None, H, HEAD_DIM, tm), lambda i: (i // nq, 0, 0, i % nq)),
                  feat(GATE_ROWS), feat(NSA_W), feat(MOBA_W),
                  full(w_out), full(ln_g), full(ln_b)],
        out_specs=pl.BlockSpec((tm, D), lambda i: (i, 0)),
        out_shape=jax.ShapeDtypeStruct((T, D), F32),
        scratch_shapes=[pltpu.VMEM((GATE_ROWS, tm), F32)],
        compiler_params=pltpu.CompilerParams(dimension_semantics=("parallel",),
                                             vmem_limit_bytes=VMEM_LIMIT),
        name="even_out",
    )(x2d, ocmp, osel, owin, om, gate_t, nz_t, mz_t, w_out, ln_g, ln_b)


def _gla_head_exact(row0, h, q_ref, v_ref, o_ref, st_ref, b_ref, kf_ref, scale):
    C, dk, dv, SUB = GLA_CHUNK, GLA_DK, GLA_DV, GLA_SUB
    rows = lax.broadcasted_iota(jnp.int32, (C, dk), 0)
    sub_i = lax.broadcasted_iota(jnp.int32, (SUB, LANES), 0)
    lane = lax.broadcasted_iota(jnp.int32, (SUB, LANES), 1)
    ones = jnp.ones((dk, LANES), BF16)
    sl = pl.ds(h * dk, dk)
    rs_c = pl.ds(row0, C)
    bh = b_ref[:, sl]
    kf = kf_ref[:, sl]
    qf = q_ref[rs_c, sl].astype(F32) * scale
    vh = v_ref[rs_c, pl.ds(h * dv, dv)]
    st = st_ref[h]
    inter = _dot_nt((qf * jnp.exp(bh)).astype(BF16), st.astype(BF16))
    bl = b_ref[pl.ds(C - 1, 1), sl]
    kd = kf * jnp.exp(bl - bh)

    def off_block(r0, r1, c0, c1):
        ref_row = b_ref[pl.ds(r0, 1), sl]
        qt = qf[r0:r1] * jnp.exp(bh[r0:r1] - ref_row)
        ek = jnp.where((rows >= c0) & (rows < c1), ref_row - bh, NEG)
        kt = kf * jnp.exp(ek)
        return _dot_nt(qt.astype(BF16), kt.astype(BF16))

    lower = off_block(2 * SUB, C, 0, 2 * SUB)
    blocks = [None, off_block(SUB, 2 * SUB, 0, SUB), lower[0:SUB],
              lower[SUB:2 * SUB] + off_block(3 * SUB, C, 2 * SUB, 3 * SUB)]
    pieces = []
    for blk in range(C // SUB):
        r0 = blk * SUB
        q_i = qf[r0:r0 + SUB]
        b_i = bh[r0:r0 + SUB]
        for jj in range(SUB):
            b_j = b_ref[pl.ds(r0 + jj, 1), sl]
            k_j = kf_ref[pl.ds(r0 + jj, 1), sl]
            pieces.append((q_i * k_j * jnp.exp(jnp.minimum(b_i - b_j, 0.0))).astype(BF16))
    sums = _dot(jnp.concatenate(pieces, axis=0), ones)
    for blk in range(C // SUB):
        r0 = blk * SUB
        diag = jnp.zeros((SUB, LANES), F32)
        for jj in range(SUB):
            rs = sums[(r0 + jj) * SUB:(r0 + jj + 1) * SUB]
            diag = jnp.where((lane == r0 + jj) & (sub_i >= jj), rs, diag)
        diag = diag[:, 0:C]
        blocks[blk] = diag if blocks[blk] is None else blocks[blk] + diag
    att = jnp.concatenate(blocks, axis=0)
    o_ref[rs_c, pl.ds(h * dv, dv)] = (inter + _dot(att.astype(BF16), vh)).astype(o_ref.dtype)
    st_ref[h] = st * jnp.exp(bl) + _dot_tn(vh, kd.astype(BF16))


def _gla_head_bounded(row0, h, q_ref, v_ref, o_ref, st_ref, b_ref, kf_ref, scale):
    C, dk, dv = GLA_CHUNK, GLA_DK, GLA_DV
    sl = pl.ds(h * dk, dk)
    rs_c = pl.ds(row0, C)
    bh = b_ref[rs_c, sl]
    kf = kf_ref[rs_c, sl]
    qf = q_ref[rs_c, sl].astype(F32) * scale
    vh = v_ref[rs_c, pl.ds(h * dv, dv)]
    st = st_ref[h]
    qs = (qf * jnp.exp(bh)).astype(BF16)
    kt = (kf * jnp.exp(-bh)).astype(BF16)
    i = lax.broadcasted_iota(jnp.int32, (C, C), 0)
    j = lax.broadcasted_iota(jnp.int32, (C, C), 1)
    att = jnp.where(i >= j, _dot_nt(qs, kt), 0.0)
    inter = _dot_nt(qs, st.astype(BF16))
    o_ref[rs_c, pl.ds(h * dv, dv)] = (inter + _dot(att.astype(BF16), vh)).astype(o_ref.dtype)
    bl = b_ref[pl.ds(row0 + C - 1, 1), sl]
    kd = kf * jnp.exp(bl - bh)
    st_ref[h] = st * jnp.exp(bl) + _dot_tn(vh, kd.astype(BF16))


def _gla_kernel(q_ref, k_ref, v_ref, glr_ref, w2_ref, gb_ref, tril_ref, o_ref,
                st_ref, b_ref, kf_ref, bc_ref, kc_ref, *, scale):
    C, H = GLA_CHUNK, GLA_HEADS
    nch = q_ref.shape[0] // C

    @pl.when(pl.program_id(1) == 0)
    def _init():
        st_ref[...] = jnp.zeros(st_ref.shape, F32)

    whi, wlo = _hilo(w2_ref[...])
    part = tril_ref.shape[0]
    decay = None
    for r0 in range(0, nch * C, part):
        rs = pl.ds(r0, part)
        ghi, glo = _hilo(glr_ref[rs, :])
        pre = _dot(ghi, whi) + _dot(glo, whi) + _dot(ghi, wlo) + gb_ref[...]
        la = (jnp.minimum(pre, 0.0) - jnp.log(1.0 + jnp.exp(-jnp.abs(pre)))) * (1.0 / GLA_TAU)
        lhi, llo = _hilo(la)
        b_part = _dot(tril_ref[...], lhi) + _dot(tril_ref[...], llo)
        b_ref[rs, :] = b_part
        for c in range(part // C):
            d_c = -b_part[c * C + C - 1:(c + 1) * C, :]
            decay = d_c if decay is None else jnp.maximum(decay, d_c)
    kf_ref[...] = k_ref[...].astype(F32)
    bounded = jnp.max(decay) <= GLA_FACTOR_MAX_DECAY

    @pl.when(bounded)
    def _factored():
        for c in range(nch):
            for h in range(H):
                _gla_head_bounded(c * C, h, q_ref, v_ref, o_ref, st_ref, b_ref, kf_ref, scale)

    @pl.when(jnp.logical_not(bounded))
    def _exact():
        def chunk(c, carry):
            row0 = pl.multiple_of(c * C, C)
            bc_ref[...] = b_ref[pl.ds(row0, C), :]
            kc_ref[...] = kf_ref[pl.ds(row0, C), :]
            for h in range(H):
                _gla_head_exact(row0, h, q_ref, v_ref, o_ref, st_ref, bc_ref, kc_ref, scale)
            return carry
        lax.fori_loop(0, nch, chunk, 0)


def _gla(q, k, v, glr, w2p, gb, tril, *, B, S):
    C, H, dk, dv = GLA_CHUNK, GLA_HEADS, GLA_DK, GLA_DV
    rows = C * min(GLA_STEP_CHUNKS, S // C)
    nchunk = S // rows
    tok = lambda n: pl.BlockSpec((rows, n), lambda b, c: (b * nchunk + c, 0))
    const = lambda shape: pl.BlockSpec(shape, lambda b, c: (0,) * len(shape))
    return pl.pallas_call(
        functools.partial(_gla_kernel, scale=float(dk) ** -0.5),
        grid=(B, nchunk),
        in_specs=[tok(H * dk), tok(H * dk), tok(H * dv), tok(LANES),
                  const(w2p.shape), const(gb.shape), const(tril.shape)],
        out_specs=tok(H * dv),
        out_shape=jax.ShapeDtypeStruct((B * S, H * dv), BF16),
        scratch_shapes=[pltpu.VMEM((H, dv, dk), F32),
                        pltpu.VMEM((rows, H * dk), F32),
                        pltpu.VMEM((rows, H * dk), F32),
                        pltpu.VMEM((C, H * dk), F32),
                        pltpu.VMEM((C, H * dk), F32)],
        compiler_params=pltpu.CompilerParams(dimension_semantics=("parallel", "arbitrary"),
                                             vmem_limit_bytes=VMEM_LIMIT),
        name="gla",
    )(q, k, v, glr, w2p, gb, tril)


def _odd_out_kernel(x_ref, o_ref_in, z_ref, gn_ref, w_ref, g_ref, b_ref, out_ref, *, alpha):
    H, dv = GLA_HEADS, GLA_DV
    parts = []
    for h in range(H):
        sl = pl.ds(h * dv, dv)
        oh = o_ref_in[:, sl].astype(F32)
        r = lax.rsqrt(jnp.mean(oh * oh, axis=-1, keepdims=True) + LN_EPS)
        z = z_ref[:, sl].astype(F32)
        parts.append((oh * r * gn_ref[:, sl] * (z * _sigmoid(z))).astype(BF16))
    y = _dot(jnp.concatenate(parts, axis=1), w_ref[...])
    out_ref[...] = _layer_norm(alpha * x_ref[...] + y, g_ref[...], b_ref[...])


def _odd_out(x2d, o, z, gn_g, w_out, ln_g, ln_b, alpha, tm=1024):
    T, D = x2d.shape
    W = o.shape[1]
    row = lambda n: pl.BlockSpec((tm, n), lambda i: (i, 0))
    return pl.pallas_call(
        functools.partial(_odd_out_kernel, alpha=alpha),
        grid=(T // tm,),
        in_specs=[row(D), row(W), row(W),
                  pl.BlockSpec((1, W), lambda i: (0, 0)),
                  pl.BlockSpec((W, D), lambda i: (0, 0)),
                  pl.BlockSpec((1, D), lambda i: (0, 0)),
                  pl.BlockSpec((1, D), lambda i: (0, 0))],
        out_specs=row(D),
        out_shape=jax.ShapeDtypeStruct((T, D), F32),
        compiler_params=pltpu.CompilerParams(dimension_semantics=("parallel",),
                                             vmem_limit_bytes=VMEM_LIMIT),
        name="odd_out",
    )(x2d, o, z, gn_g, w_out, ln_g, ln_b)


def _key_features(S, block):
    s = np.arange(S)
    f = np.zeros((S, LANES), np.float32)
    f[s, s // block] = 1.0
    f[:, FEAT + 0] = 1.0
    f[:, FEAT + 1] = 1.0
    f[:, FEAT + 2] = FEAT * (s // FEAT)
    f[:, FEAT + 3] = s % FEAT
    return jnp.asarray(f, BF16)


def _pad_to(a, n, axis):
    pad = [(0, 0)] * a.ndim
    pad[axis] = (0, n - a.shape[axis])
    return jnp.pad(a, pad)


def _even_layer(x2d, B, S, w_in, cmp_pos, cmp_w1, cmp_b1, cmp_w2, w_out, ln_g, ln_b, alpha):
    T, D = x2d.shape
    G, hd = NSA_KV_GROUPS, HEAD_DIM
    scale = hd ** -0.5
    wq, wkv, wgate, wnz, wmq, wmk, wmv, wmz = jnp.split(w_in, np.cumsum(EVEN_SPLITS)[:-1].tolist(), axis=1)
    wkc, wvc, wks, wvs, wkw, wvw = jnp.split(wkv, 6, axis=1)
    w_tok = jnp.concatenate([wkc, wvc, wks, wkw, wmk], axis=1).astype(BF16)
    w_tr = jnp.concatenate([wq * scale, wvs, wvw, wmq * scale, wmv, wnz, wmz,
                            _pad_to(wgate, GATE_ROWS, 1)], axis=1).T.astype(BF16)
    (kcmp, vcmp, ksel, kwin, mk, nq_t, vsel_t, vwin_t, mq_t, mv_t, nz_t, mz_t, gate_t) = _proj(
        x2d, w_tok, (NSA_KV_W,) * 4 + (MOBA_W,), (F32, F32, BF16, BF16, BF16),
        w_tr, (NSA_W, NSA_KV_W, NSA_KV_W, MOBA_W, MOBA_W, NSA_W, MOBA_W, GATE_ROWS),
        (BF16,) * 7 + (F32,))

    L, d, HID = NSA_CMP_BLOCK, NSA_CMP_STRIDE, NSA_CMP_HIDDEN
    ncmp = (S - L) // d + 1
    NC = S // d
    eye = jnp.eye(G, dtype=F32)
    w1r = cmp_w1.reshape(2, 2, d, hd, HID)
    w1p = jnp.einsum('ialdj,gh->ialgdhj', w1r, eye).reshape(2, 2, d * G * hd, G * HID).astype(BF16)
    posp = jnp.broadcast_to(cmp_pos.reshape(2, 2, d, 1, hd), (2, 2, d, G, hd)).reshape(2, 2, 1, d * G * hd)
    b1p = jnp.tile(cmp_b1, (1, G)).reshape(2, 1, G * HID)
    w2p = jnp.einsum('ijd,gh->igjhd', _pad_to(cmp_w2, LANES, 2), eye).reshape(2, G * HID, G * LANES).astype(BF16)
    cfeat = np.zeros((NC, G, LANES), np.float32)
    cfeat[:, :, hd + 0] = 1.0
    cfeat[:, :, hd + 1] = 1.0
    cfeat[:, :, hd + 2] = (np.arange(NC) * d)[:, None]
    cfeat[:, :, hd + 3] = L - 1
    kvc = _cmp_mlp(kcmp, vcmp, w1p, posp, b1p, w2p, jnp.asarray(cfeat.reshape(NC, G * LANES)), ncmp,
                   B=B, S=S)

    ns = S // NSA_SEL_BLOCK
    cmp_start = np.arange(NC) * d
    sel_start = np.arange(FEAT) * NSA_SEL_BLOCK
    ovt = ((cmp_start[None, :] <= sel_start[:, None] + NSA_SEL_BLOCK - 1)
           & (cmp_start[None, :] + L - 1 >= sel_start[:, None])
           & (np.arange(NC)[None, :] < ncmp) & (np.arange(FEAT)[:, None] < ns))
    o_cmp, bias_sel = _cmp_attn(nq_t, kvc, jnp.asarray(ovt, BF16), B=B, S=S)

    feats_sel = _key_features(S, NSA_SEL_BLOCK)
    o_sel = _flash(nq_t, bias_sel, ksel, feats_sel, vsel_t,
                   B=B, S=S, units=G, tq=TQ, nh=NSA_HG, pair=False, window=None)
    o_win = _flash(nq_t, None, kwin, feats_sel, vwin_t,
                   B=B, S=S, units=G, tq=TQ, nh=NSA_HG, pair=False, window=NSA_WINDOW)

    avg = np.zeros((MOBA_NBP, S), np.float32)
    avg[np.arange(S) // MOBA_BLOCK, np.arange(S)] = 1.0 / MOBA_BLOCK
    bias_m = _moba_gate(mq_t, mk, jnp.asarray(avg, BF16), B=B, S=S)
    o_m = _flash(mq_t, bias_m, mk, _key_features(S, MOBA_BLOCK), mv_t,
                 B=B, S=S, units=MOBA_HEADS // MOBA_STEP_HEADS, tq=TQM, nh=MOBA_STEP_HEADS, pair=True, window=None)

    return _even_out(x2d, o_cmp, o_sel, o_win, o_m, gate_t, nz_t, mz_t, w_out.astype(BF16),
                     ln_g.reshape(1, D), ln_b.reshape(1, D), alpha, B=B, S=S)


def _odd_layer(x2d, B, S, w_in, gate_w2, gate_b, gn_g, w_out, ln_g, ln_b, alpha):
    T, D = x2d.shape
    H, dk, dv, C = GLA_HEADS, GLA_DK, GLA_DV, GLA_CHUNK
    wq, wk, wv, wg, wz = jnp.split(w_in, np.cumsum(ODD_SPLITS)[:-1].tolist(), axis=1)
    w_perm = jnp.concatenate([wq, wk, wv, wz, _pad_to(wg, LANES, 1)], axis=1).astype(BF16)
    q, k, v, z, glr = _proj(x2d, w_perm, (H * dk, H * dk, H * dv, H * dv, LANES),
                            (BF16, BF16, BF16, BF16, F32))
    w2p = _pad_to(gate_w2, LANES, 0)
    ncs = min(GLA_CUMSUM_CHUNKS, GLA_STEP_CHUNKS, S // C)
    tril = jnp.asarray(np.kron(np.eye(ncs), np.tril(np.ones((C, C)))), BF16)
    o = _gla(q, k, v, glr, w2p, gate_b.reshape(1, H * dk), tril, B=B, S=S)
    return _odd_out(x2d, o, z, gn_g.reshape(1, H * dv), w_out.astype(BF16),
                    ln_g.reshape(1, D), ln_b.reshape(1, D), alpha)


def kernel(x, ev_w_in, ev_cmp_pos, ev_cmp_w1, ev_cmp_b1, ev_cmp_w2, ev_w_out, ev_ln_g, ev_ln_b,
           od_w_in, od_gate_w2, od_gate_b, od_gn_g, od_w_out, od_ln_g, od_ln_b):
    B, S, D = x.shape
    depth = ev_w_in.shape[0] + od_w_in.shape[0]
    alpha = (2.0 * depth) ** 0.25
    h = x.reshape(B * S, D)
    for layer in range(depth):
        i = layer // 2
        if layer % 2 == 0:
            h = _even_layer(h, B, S, ev_w_in[i], ev_cmp_pos[i], ev_cmp_w1[i], ev_cmp_b1[i], ev_cmp_w2[i],
                            ev_w_out[i], ev_ln_g[i], ev_ln_b[i], alpha)
        else:
            h = _odd_layer(h, B, S, od_w_in[i], od_gate_w2[i], od_gate_b[i], od_gn_g[i],
                           od_w_out[i], od_ln_g[i], od_ln_b[i], alpha)
    return h.reshape(B, S, D)
```

```python
import functools

import numpy as np
import jax
import jax.numpy as jnp
from jax import lax
from jax.experimental import pallas as pl
from jax.experimental.pallas import tpu as pltpu

F32 = jnp.float32
BF16 = jnp.bfloat16

HEAD_DIM = 64
NSA_HEADS = 8
NSA_KV_GROUPS = 2
NSA_HG = NSA_HEADS // NSA_KV_GROUPS
NSA_CMP_BLOCK = 32
NSA_CMP_STRIDE = 16
NSA_CMP_HIDDEN = 128
NSA_SEL_BLOCK = 64
NSA_TOP_N = 16
NSA_WINDOW = 512
MOBA_HEADS = 8
MOBA_BLOCK = 256
MOBA_TOP_K = 3
GLA_HEADS = 4
GLA_DK = 128
GLA_DV = 256
GLA_GATE_RANK = 16
GLA_TAU = 16.0
GLA_CHUNK = 64
GLA_SUB = 16
GLA_STEP_CHUNKS = 16
GLA_CUMSUM_CHUNKS = 4
GLA_FACTOR_MAX_DECAY = 60.0
LN_EPS = 1e-5
NEG = -1e30
FORCE_BONUS = 1e4

NSA_W = NSA_HEADS * HEAD_DIM
NSA_KV_W = NSA_KV_GROUPS * HEAD_DIM
MOBA_W = MOBA_HEADS * HEAD_DIM
EVEN_SPLITS = (NSA_W, 6 * NSA_KV_W, 3 * NSA_HEADS, NSA_W, MOBA_W, MOBA_W, MOBA_W, MOBA_W)
ODD_SPLITS = (GLA_HEADS * GLA_DK, GLA_HEADS * GLA_DK, GLA_HEADS * GLA_DV, GLA_GATE_RANK, GLA_HEADS * GLA_DV)

LANES = 128
SUBLANES = 8
FEAT = 64
ALIBI_ROWS = 16
GATE_ROWS = 32
MOBA_NBP = 16
TQ = 512
TQM = 512
MOBA_STEP_HEADS = 4
TK = 512
CG = 256
MIN_WEIGHT_SUM = 1e-25
VROWS = HEAD_DIM + 16
VMEM_LIMIT = 48 * 1024 * 1024


def _dot(a, b):
    return jnp.dot(a, b, preferred_element_type=F32)


def _dot_nt(a, b):
    return lax.dot_general(a, b, (((1,), (1,)), ((), ())), preferred_element_type=F32)


def _dot_tn(a, b):
    return lax.dot_general(a, b, (((0,), (0,)), ((), ())), preferred_element_type=F32)


def _hilo(a):
    hi = a.astype(BF16)
    lo = (a - hi.astype(F32)).astype(BF16)
    return hi, lo


def _pow2_neg(n):
    return lax.bitcast_convert_type(lax.shift_left(127 - n, 23), F32)


def _sigmoid(x):
    return 0.5 * jnp.tanh(0.5 * x) + 0.5


def _layer_norm(r, g, b):
    mu = jnp.mean(r, axis=-1, keepdims=True)
    d = r - mu
    var = jnp.mean(d * d, axis=-1, keepdims=True)
    return d * lax.rsqrt(var + LN_EPS) * g + b


def _alibi_query_rows(t0, tq, head0, R):
    r = lax.broadcasted_iota(jnp.int32, (ALIBI_ROWS, R), 0)
    col = lax.broadcasted_iota(jnp.int32, (ALIBI_ROWS, R), 1)
    t = t0 + col % tq
    slope = _pow2_neg(head0 + col // tq + 1)
    feats = jnp.where(r == 0, -FEAT * slope * (t // FEAT).astype(F32),
                      jnp.where(r == 1, -slope * (t % FEAT).astype(F32), jnp.where(r < 4, slope, 0.0)))
    return jnp.concatenate([feats.astype(BF16), jnp.zeros((FEAT - ALIBI_ROWS, R), BF16)], axis=0)


def _ranks(sc_ref, n_real, n_rows=None, cols=slice(None)):
    n_rows = sc_ref.shape[0] if n_rows is None else n_rows
    out = []
    for v in range(n_rows // SUBLANES):
        lo = v * SUBLANES
        sc = sc_ref[lo:lo + SUBLANES, cols]
        width = sc.shape[1]
        idx = lo + lax.broadcasted_iota(jnp.int32, (SUBLANES, width), 0)
        rank = jnp.zeros((SUBLANES, width), jnp.int32)
        for jp in range(n_real):
            r = sc_ref[pl.ds(jp, 1), cols]
            if jp < lo:
                beats = r >= sc
            elif jp >= lo + SUBLANES - 1:
                beats = r > sc
            else:
                beats = (r > sc) | ((r == sc) & (idx > jp))
            rank = rank + jnp.where(beats, 1, 0)
        out.append(rank)
    return jnp.concatenate(out, axis=0)


def _proj_kernel(*refs, tok_splits, tr_splits):
    x_ref, wt_ref = refs[0], refs[1]
    outs = list(refs[3:] if tr_splits else refs[2:])
    x = x_ref[...]
    xb = x.astype(BF16)
    off = 0
    for n in tok_splits:
        o_ref = outs.pop(0)
        o_ref[...] = _dot(xb, wt_ref[:, off:off + n]).astype(o_ref.dtype)
        off += n
    if tr_splits:
        wtr_ref = refs[2]
        xt = x.T.astype(BF16)
        off = 0
        for n in tr_splits:
            o_ref = outs.pop(0)
            o_ref[...] = _dot(wtr_ref[off:off + n, :], xt).astype(o_ref.dtype)
            off += n


def _proj(x2d, w_tok, tok_splits, tok_dtypes, w_tr=None, tr_splits=(), tr_dtypes=(), tm=512):
    T, D = x2d.shape
    assert T % tm == 0 and sum(tok_splits) == w_tok.shape[1]
    in_specs = [pl.BlockSpec((tm, D), lambda i: (i, 0)),
                pl.BlockSpec(w_tok.shape, lambda i: (0, 0))]
    args = [x2d, w_tok]
    if tr_splits:
        assert sum(tr_splits) == w_tr.shape[0]
        in_specs.append(pl.BlockSpec(w_tr.shape, lambda i: (0, 0)))
        args.append(w_tr)
    out_specs = ([pl.BlockSpec((tm, n), lambda i: (i, 0)) for n in tok_splits]
                 + [pl.BlockSpec((n, tm), lambda i: (0, i)) for n in tr_splits])
    out_shape = ([jax.ShapeDtypeStruct((T, n), dt) for n, dt in zip(tok_splits, tok_dtypes)]
                 + [jax.ShapeDtypeStruct((n, T), dt) for n, dt in zip(tr_splits, tr_dtypes)])
    return pl.pallas_call(
        functools.partial(_proj_kernel, tok_splits=tuple(tok_splits), tr_splits=tuple(tr_splits)),
        grid=(T // tm,), in_specs=in_specs, out_specs=out_specs, out_shape=out_shape,
        compiler_params=pltpu.CompilerParams(dimension_semantics=("parallel",),
                                             vmem_limit_bytes=VMEM_LIMIT),
        name="proj",
    )(*args)


def _cmp_mlp_kernel(kx_ref, vx_ref, w1_ref, pos_ref, b1_ref, w2_ref, cf_ref, o_ref, *, ncmp):
    d = NSA_CMP_STRIDE
    nc = kx_ref.shape[0] // d
    cw = w1_ref.shape[2]
    for i, x_ref in enumerate((kx_ref, vx_ref)):
        a = bm = None
        for l in range(0, d, 2):
            x2 = jnp.concatenate([x_ref[pl.ds(l, nc, stride=d), :], x_ref[pl.ds(l + 1, nc, stride=d), :]],
                                 axis=1).astype(BF16)
            rows = pl.ds(l * x_ref.shape[1], 2 * x_ref.shape[1])
            a_l, b_l = _dot(x2, w1_ref[i, 0, rows, :]), _dot(x2, w1_ref[i, 1, rows, :])
            a, bm = (a_l, b_l) if a is None else (a + a_l, bm + b_l)
        c = b1_ref[i]
        for half in range(2):
            phi, plo = _hilo(jnp.broadcast_to(pos_ref[i, half], (SUBLANES, cw)))
            c = c + (_dot(phi, w1_ref[i, half]) + _dot(plo, w1_ref[i, half]))[0:1]
        hid = a + pltpu.roll(bm, nc - 1, axis=0) + c
        hid = hid * _sigmoid(hid)
        row = lax.broadcasted_iota(jnp.int32, hid.shape, 0)
        hid = jnp.where(row < ncmp, hid, 0.0)
        out = _dot(hid.astype(BF16), w2_ref[i])
        if i == 0:
            out = out + cf_ref[...]
        o_ref[i] = out.astype(o_ref.dtype)


def _cmp_mlp(kx, vx, w1p, posp, b1p, w2p, cfeat, ncmp, *, B, S):
    NC = S // NSA_CMP_STRIDE
    OW = w2p.shape[-1]
    full = lambda a: pl.BlockSpec(a.shape, lambda b: (0,) * a.ndim)
    return pl.pallas_call(
        functools.partial(_cmp_mlp_kernel, ncmp=ncmp),
        grid=(B,),
        in_specs=[pl.BlockSpec((S, kx.shape[1]), lambda b: (b, 0)),
                  pl.BlockSpec((S, vx.shape[1]), lambda b: (b, 0)),
                  full(w1p), full(posp), full(b1p), full(w2p), full(cfeat)],
        out_specs=pl.BlockSpec((2, None, NC, OW), lambda b: (0, b, 0, 0)),
        out_shape=jax.ShapeDtypeStruct((2, B, NC, OW), BF16),
        compiler_params=pltpu.CompilerParams(dimension_semantics=("parallel",),
                                             vmem_limit_bytes=VMEM_LIMIT),
        name="cmp_mlp",
    )(kx, vx, w1p, posp, b1p, w2p, cfeat)


def _cmp_attn_kernel(q_ref, kc_ref, vc_ref, ovt_ref, o_ref, bias_ref, sc_ref, *, tq, nh, ns, ntop):
    g = pl.program_id(1)
    R = nh * tq
    nq = q_ref.shape[1] // tq

    def q_tile(qi):
        toks = pl.ds(qi * tq, tq)
        nc = min(kc_ref.shape[0], -(-((qi + 1) * tq) // (NSA_CMP_STRIDE * 16)) * 16)
        nsel = min(ns, -(-((qi + 1) * tq) // (NSA_SEL_BLOCK * SUBLANES)) * SUBLANES)
        qcat = jnp.concatenate([q_ref[hh * HEAD_DIM:(hh + 1) * HEAD_DIM, toks] for hh in range(nh)], axis=1)
        qa = jnp.concatenate([qcat, _alibi_query_rows(qi * tq, tq, g * nh, R)], axis=0)
        s = _dot(kc_ref[0:nc, :], qa)
        n = lax.broadcasted_iota(jnp.int32, (nc, R), 0)
        t_row = qi * tq + lax.broadcasted_iota(jnp.int32, (1, R), 1) % tq
        last = (t_row - (NSA_CMP_BLOCK - 1)) // NSA_CMP_STRIDE
        s = jnp.where(n <= last, s, NEG)
        e = jnp.exp(s - jnp.max(s, axis=0, keepdims=True))
        inv = jnp.where(last >= 0, 1.0 / jnp.sum(e, axis=0, keepdims=True), 0.0)
        p = e * inv
        o_ref[:, pl.ds(qi * R, R)] = _dot_tn(vc_ref[0:nc, 0:HEAD_DIM], p.astype(BF16)).astype(o_ref.dtype)

        psum = p[:, 0:tq]
        for hh in range(1, nh):
            psum = psum + p[:, hh * tq:(hh + 1) * tq]
        phi, plo = _hilo(psum)
        ovt = ovt_ref[0:nsel, 0:nc]
        imp = _dot(ovt, phi) + _dot(ovt, plo)

        j = lax.broadcasted_iota(jnp.int32, (nsel, tq), 0)
        cur = (qi * tq + lax.broadcasted_iota(jnp.int32, (nsel, tq), 1)) // NSA_SEL_BLOCK
        if nsel <= ntop:
            sel = j <= cur
        else:
            forced = jnp.where((j == 0) | (j == cur) | (j == cur - 1), FORCE_BONUS, 0.0)
            sc_ref[0:nsel, :] = jnp.where(j <= cur, imp + forced, NEG)
            sel = (_ranks(sc_ref, nsel, nsel) < ntop) & (j <= cur)
        bias_ref[0:nsel, toks] = jnp.where(sel, 0.0, NEG).astype(bias_ref.dtype)
        if nsel < FEAT:
            bias_ref[nsel:FEAT, toks] = jnp.full((FEAT - nsel, tq), NEG, bias_ref.dtype)

    for qi in range(nq):
        q_tile(qi)


def _cmp_attn(nq_t, kvc, ovt, *, B, S):
    tq, nh, G = TQ, NSA_HG, NSA_KV_GROUPS
    R = nh * tq
    nq = S // tq
    NC = kvc.shape[2]
    ns = S // NSA_SEL_BLOCK
    assert ns <= FEAT
    ntop = min(NSA_TOP_N, ns)
    return pl.pallas_call(
        functools.partial(_cmp_attn_kernel, tq=tq, nh=nh, ns=ns, ntop=ntop),
        grid=(B, G),
        in_specs=[pl.BlockSpec((nh * HEAD_DIM, S), lambda b, g: (g, b)),
                  pl.BlockSpec((None, None, NC, LANES), lambda b, g: (0, b, 0, g)),
                  pl.BlockSpec((None, None, NC, LANES), lambda b, g: (1, b, 0, g)),
                  pl.BlockSpec(ovt.shape, lambda b, g: (0, 0))],
        out_specs=[pl.BlockSpec((None, None, HEAD_DIM, S * nh), lambda b, g: (b, g, 0, 0)),
                   pl.BlockSpec((None, None, FEAT, S), lambda b, g: (b, g, 0, 0))],
        out_shape=[jax.ShapeDtypeStruct((B, G, HEAD_DIM, S * nh), BF16),
                   jax.ShapeDtypeStruct((B, G, FEAT, S), BF16)],
        scratch_shapes=[pltpu.VMEM((FEAT, tq), F32)],
        compiler_params=pltpu.CompilerParams(dimension_semantics=("parallel", "parallel"),
                                             vmem_limit_bytes=VMEM_LIMIT),
        name="cmp_attn",
    )(nq_t, kvc, kvc, ovt)


def _moba_gate_kernel(q_ref, k_ref, a_ref, bias_ref, sc_ref, *, nb, ntop):
    nbp, S = sc_ref.shape[1:]
    km = _dot(a_ref[...], k_ref[...])
    lane = lax.broadcasted_iota(jnp.int32, (nbp, LANES), 1)
    parts = []
    for hh in range(2):
        parts += _hilo(jnp.where(lane // HEAD_DIM == hh, km, 0.0))
    g = _dot(jnp.concatenate(parts, axis=0), q_ref[...])
    past = (lax.broadcasted_iota(jnp.int32, (nbp, S), 0)
            < lax.broadcasted_iota(jnp.int32, (nbp, S), 1) // MOBA_BLOCK)
    n = lax.broadcasted_iota(jnp.int32, (nbp, MOBA_BLOCK), 0)
    for hh in range(2):
        sc = sc_ref.at[hh]
        sc[...] = jnp.where(past, g[2 * hh * nbp:(2 * hh + 1) * nbp] + g[(2 * hh + 1) * nbp:(2 * hh + 2) * nbp], NEG)
        for c in range(nb):
            cols = slice(c * MOBA_BLOCK, (c + 1) * MOBA_BLOCK)
            sel = n <= c
            if c > ntop:
                nr = -(-c // SUBLANES) * SUBLANES
                rank = _ranks(sc, c, nr, cols)
                rank = rank + jnp.where(sc[0:nr, cols] < NEG, nb - c, 0)
                if nr < nbp:
                    rank = jnp.concatenate([rank, jnp.full((nbp - nr, MOBA_BLOCK), nb, jnp.int32)], axis=0)
                sel = ((rank < ntop) & (n < c)) | (n == c)
            bias_ref[hh, 0:nbp, cols] = jnp.where(sel, 0.0, NEG).astype(bias_ref.dtype)
        bias_ref[hh, nbp:FEAT, :] = jnp.zeros((FEAT - nbp, S), bias_ref.dtype)


def _moba_gate(mq_t, mk, avg, *, B, S):
    H = MOBA_HEADS
    nb = S // MOBA_BLOCK
    nbp = avg.shape[0]
    assert nb <= nbp and S % MOBA_BLOCK == 0 and MOBA_BLOCK % LANES == 0 and H % 2 == 0
    ntop = min(MOBA_TOP_K, nb)
    return pl.pallas_call(
        functools.partial(_moba_gate_kernel, nb=nb, ntop=ntop),
        grid=(B, H // 2),
        in_specs=[pl.BlockSpec((2 * HEAD_DIM, S), lambda b, h: (h, b)),
                  pl.BlockSpec((S, LANES), lambda b, h: (b, h)),
                  pl.BlockSpec(avg.shape, lambda b, h: (0, 0))],
        out_specs=pl.BlockSpec((None, 2, FEAT, S), lambda b, h: (b, h, 0, 0)),
        out_shape=jax.ShapeDtypeStruct((B, H, FEAT, S), BF16),
        scratch_shapes=[pltpu.VMEM((2, nbp, S), F32)],
        compiler_params=pltpu.CompilerParams(dimension_semantics=("parallel", "parallel"),
                                             vmem_limit_bytes=VMEM_LIMIT),
        name="moba_gate",
    )(mq_t, mk, avg)


def _flash_q_tile(qi, running_max, q_ref, bias_ref, mask_ref, o_ref, ka_ref, vta_ref, kn_ref, qa_ref, bnd_ref,
                  m_ref, acc_ref, alpha_ref, p_ref, *, tq, tk, nh, pair, window, has_past=None):
    u = pl.program_id(1)
    R = nh * tq
    static = isinstance(qi, int)
    q0 = qi * tq if static else pl.multiple_of(qi * tq, tq)
    toks = pl.ds(q0, tq)

    qcat = jnp.concatenate([q_ref[hh * HEAD_DIM:(hh + 1) * HEAD_DIM, toks] for hh in range(nh)], axis=1)
    qf = qcat.astype(F32)
    qq = jnp.concatenate([qf, qf], axis=0)
    rowi = lax.broadcasted_iota(jnp.int32, (2 * HEAD_DIM, R), 0)
    coli = lax.broadcasted_iota(jnp.int32, (2 * HEAD_DIM, R), 1)
    half = (coli // tq) % 2 if pair else u % 2
    qa_ref[0:2 * HEAD_DIM, :] = jnp.where(rowi // HEAD_DIM == half, qq, 0.0).astype(BF16)
    if bias_ref is None:
        qa_ref[2 * HEAD_DIM:2 * HEAD_DIM + FEAT, :] = jnp.zeros((FEAT, R), BF16)
    else:
        qa_ref[2 * HEAD_DIM:2 * HEAD_DIM + FEAT, :] = jnp.concatenate(
            [bias_ref[hh if pair else 0, :, toks] for hh in range(nh)], axis=1)
    qa_ref[2 * HEAD_DIM + FEAT:2 * LANES, :] = _alibi_query_rows(q0, tq, u * nh, R)
    if pair:
        kn = jnp.concatenate([kn_ref[hh:hh + 1, :] for hh in range(nh) for _ in range(tq // LANES)], axis=1)
    else:
        kn = jnp.concatenate([jnp.where(u % 2 == 0, kn_ref[0:1, :], kn_ref[1:2, :])] * (R // LANES), axis=1)
    qn = jnp.sqrt(jnp.sum(qf * qf, axis=0, keepdims=True))
    bnd_ref[...] = jnp.broadcast_to(qn * kn, bnd_ref.shape)

    ncg = R // CG
    kd = (q0 + tq - 1) // tk

    def key_rows(kind, c):
        off = (c * CG) % tq
        if kind is None:
            return 0, tk, None
        if kind == "diag":
            return 0, off + CG, off // CG
        return off, tk, (tk + off) // CG

    def values(ki, kind=None):
        for c in range(ncg):
            cols = pl.ds(c * CG, CG)
            r0, r1, _ = key_rows(kind, c)
            v = (c * CG) // tq if pair else 0
            pv = _dot(vta_ref[v, ki, :, r0:r1], p_ref[r0:r1, cols])
            if running_max:
                acc_ref[:, cols] = alpha_ref[0:1, cols] * acc_ref[:, cols] + pv
            else:
                acc_ref[:, cols] = acc_ref[:, cols] + pv

    def scores(ki, kind=None, prev=None, prev_kind=None):
        k0 = ki * tk if isinstance(ki, int) else pl.multiple_of(ki * tk, tk)
        s_groups = []
        for c in range(ncg):
            r0, r1, _ = key_rows(kind, c)
            kb = (c * CG) // (2 * tq) if pair else 0
            s_groups.append(_dot(ka_ref[kb, pl.ds(k0 + r0, r1 - r0), :], qa_ref[:, pl.ds(c * CG, CG)]))
        if prev is not None:
            values(prev, prev_kind)
        for c, s in enumerate(s_groups):
            cols = pl.ds(c * CG, CG)
            r0, r1, entry = key_rows(kind, c)
            if entry is not None:
                s = s + mask_ref[entry, r0:r1, :]
            if running_max:
                m_prev = m_ref[0:1, cols]
                m_new = jnp.maximum(m_prev, jnp.max(s, axis=0, keepdims=True))
                alpha_ref[:, cols] = jnp.broadcast_to(jnp.exp(m_prev - m_new), (alpha_ref.shape[0], CG))
                m_ref[:, cols] = jnp.broadcast_to(m_new, (m_ref.shape[0], CG))
                p_ref[r0:r1, cols] = jnp.exp(s - m_new).astype(BF16)
            else:
                p_ref[r0:r1, cols] = jnp.exp(s - bnd_ref[0:1, cols]).astype(BF16)

    def steady(lo, hi):
        def pair_body(j, carry):
            ki = lo + 2 * j
            scores(ki, prev=ki - 1)
            scores(ki + 1, prev=ki)
            return carry
        lax.fori_loop(0, (hi - lo) // 2, pair_body, 0)

        @pl.when((hi - lo) % 2 == 1)
        def _odd_one():
            scores(hi - 1, prev=hi - 2)

    def only_diagonal():
        scores(0, "diag")
        values(0, "diag")

    def past_then_diagonal():
        if window is not None:
            scores(kd - 1, "wedge")
            scores(kd, "diag", prev=kd - 1, prev_kind="wedge")
        elif static:
            scores(0)
            for ki in range(1, kd):
                scores(ki, prev=ki - 1)
            scores(kd, "diag", prev=kd - 1)
        else:
            scores(0)
            steady(1, kd)
            scores(kd, "diag", prev=kd - 1)
        values(kd, "diag")

    acc_ref[...] = jnp.zeros(acc_ref.shape, F32)
    if running_max:
        m_ref[...] = jnp.full(m_ref.shape, NEG, F32)
    if static:
        only_diagonal() if kd == 0 else past_then_diagonal()
    elif has_past:
        past_then_diagonal()
    else:
        pl.when(kd == 0)(only_diagonal)
        pl.when(kd > 0)(past_then_diagonal)

    acc = acc_ref[...]
    out = (acc[0:HEAD_DIM] / acc[HEAD_DIM:HEAD_DIM + 1]).astype(o_ref.dtype)
    if pair:
        for hh in range(nh):
            o_ref[hh, :, toks] = out[:, hh * tq:(hh + 1) * tq]
    else:
        o_ref[0, :, pl.ds(qi * R if static else pl.multiple_of(qi * R, R), R)] = out
    return jnp.min(acc[HEAD_DIM:HEAD_DIM + 1, :])


def _flash_kernel(*refs, tq, tk, nh, pair, window, has_bias):
    q_ref = refs[0]
    bias_ref = refs[1] if has_bias else None
    (k_ref, kf_ref, vt_ref, mask_ref, o_ref, ka_ref, vta_ref, kn_ref, qa_ref, bnd_ref, m_ref, acc_ref,
     alpha_ref, p_ref, low_ref) = refs[2 if has_bias else 1:]
    S = k_ref.shape[0]
    for kb in range(ka_ref.shape[0]):
        k = k_ref[:, kb * LANES:(kb + 1) * LANES]
        ka_ref[kb, :, 0:LANES] = k
        ka_ref[kb, :, LANES:2 * LANES] = kf_ref[...]
        k2 = jnp.square(k.astype(F32))
        lane = lax.broadcasted_iota(jnp.int32, k2.shape, 1)
        for hf in range(2):
            n2 = jnp.sum(jnp.where(lane // HEAD_DIM == hf, k2, 0.0), axis=1, keepdims=True)
            kn_ref[2 * kb + hf:2 * kb + hf + 1, :] = jnp.broadcast_to(
                jnp.sqrt(jnp.max(n2, axis=0, keepdims=True)), (1, LANES))
    ones_row = jnp.where(lax.broadcasted_iota(jnp.int32, (VROWS - HEAD_DIM, tk), 0) == 0, 1.0, 0.0)
    for v in range(nh if pair else 1):
        for jt in range(S // tk):
            vta_ref[v, jt, 0:HEAD_DIM, :] = vt_ref[v * HEAD_DIM:(v + 1) * HEAD_DIM, jt * tk:(jt + 1) * tk]
            vta_ref[v, jt, HEAD_DIM:VROWS, :] = ones_row.astype(BF16)

    tile = functools.partial(_flash_q_tile, q_ref=q_ref, bias_ref=bias_ref, mask_ref=mask_ref, o_ref=o_ref,
                             ka_ref=ka_ref, vta_ref=vta_ref, kn_ref=kn_ref, qa_ref=qa_ref, bnd_ref=bnd_ref,
                             m_ref=m_ref, acc_ref=acc_ref, alpha_ref=alpha_ref, p_ref=p_ref,
                             tq=tq, tk=tk, nh=nh, pair=pair, window=window)
    nq = S // tq
    low_ref[0] = tile(0, False)

    def bound_stabilised(qi, carry):
        low_ref[qi] = tile(qi, False, has_past=True)
        return carry
    lax.fori_loop(1, nq, bound_stabilised, 0)

    def redo(qi, carry):
        @pl.when(low_ref[qi] < MIN_WEIGHT_SUM)
        def _running_max():
            tile(qi, True)
        return carry

    lax.fori_loop(0, nq, redo, 0)


def _mask_table(tk, window):
    n_delta = (tk + (window or 0)) // CG
    d = np.arange(n_delta)[:, None, None] * CG + np.arange(CG)[None, None, :] - np.arange(tk)[None, :, None]
    ok = (d >= 0) & ((d < window) if window else True)
    return jnp.asarray(np.where(ok, 0.0, NEG), F32)


def _flash(q_t, bias_t, k, kfeat, v_t, *, B, S, units, tq, nh, pair, window):
    tk = TK
    R = nh * tq
    nq = S // tq
    nv = nh if pair else 1
    nb = nh if pair else 1
    nkb = nh // 2 if pair else 1
    assert tq == tk and S % tk == 0 and tk % CG == 0 and window in (None, tk)
    mask = _mask_table(tk, window)
    kcol = (lambda u: u) if pair else (lambda u: u // 2)
    out_w = tq if pair else R
    out_shape = (B, units * nb, HEAD_DIM, S if pair else S * nh)
    has_bias = bias_t is not None
    in_specs = [pl.BlockSpec((nh * HEAD_DIM, S), lambda b, u: (u, b))]
    if has_bias:
        in_specs.append(pl.BlockSpec((None, nb, FEAT, S), lambda b, u: (b, u, 0, 0)))
    in_specs += [pl.BlockSpec((S, nkb * LANES), lambda b, u: (b, kcol(u))),
                 pl.BlockSpec((S, LANES), lambda b, u: (0, 0)),
                 pl.BlockSpec((nv * HEAD_DIM, S), lambda b, u: (u, b)),
                 pl.BlockSpec(mask.shape, lambda b, u: (0, 0, 0))]
    args = (q_t,) + ((bias_t,) if has_bias else ()) + (k, kfeat, v_t, mask)
    return pl.pallas_call(
        functools.partial(_flash_kernel, tq=tq, tk=tk, nh=nh, pair=pair, window=window, has_bias=has_bias),
        grid=(B, units),
        in_specs=in_specs,
        out_specs=pl.BlockSpec((None, nb, HEAD_DIM, out_w * nq), lambda b, u: (b, u, 0, 0)),
        out_shape=jax.ShapeDtypeStruct(out_shape, BF16),
        scratch_shapes=[pltpu.VMEM((nkb, S, 2 * LANES), BF16),
                        pltpu.VMEM((nv, S // tk, VROWS, tk), BF16),
                        pltpu.VMEM((SUBLANES, LANES), F32),
                        pltpu.VMEM((2 * LANES, R), BF16),
                        pltpu.VMEM((SUBLANES, R), F32),
                        pltpu.VMEM((SUBLANES, R), F32),
                        pltpu.VMEM((VROWS, R), F32),
                        pltpu.VMEM((SUBLANES, R), F32),
                        pltpu.VMEM((tk, R), BF16),
                        pltpu.SMEM((nq,), F32)],
        compiler_params=pltpu.CompilerParams(dimension_semantics=("parallel", "parallel"),
                                             vmem_limit_bytes=VMEM_LIMIT),
        name="flash",
    )(*args)


def _even_out_kernel(x_ref, ocmp_ref, osel_ref, owin_ref, om_ref, gate_ref, nz_ref, mz_ref,
                     w_ref, g_ref, b_ref, o_ref, sig_ref, *, alpha, tm):
    G, nh = NSA_KV_GROUPS, NSA_HG
    sig_ref[...] = _sigmoid(gate_ref[...])
    heads = []
    for g in range(G):
        for hh in range(nh):
            acc = None
            for br, ref in enumerate((ocmp_ref, osel_ref, owin_ref)):
                gate = sig_ref[pl.ds(3 * (g * nh + hh) + br, 1), :]
                term = gate * ref[g, :, hh * tm:(hh + 1) * tm].astype(F32)
                acc = term if acc is None else acc + term
            heads.append(acc)
    o_nsa = jnp.concatenate(heads, axis=0)
    nz = nz_ref[...].astype(F32)
    mz = mz_ref[...].astype(F32)
    o_m = jnp.concatenate([om_ref[h] for h in range(MOBA_HEADS)], axis=0).astype(F32)
    a_t = jnp.concatenate([o_nsa * (nz * _sigmoid(nz)), o_m * (mz * _sigmoid(mz))], axis=0)
    y = _dot_tn(a_t.astype(BF16), w_ref[...])
    o_ref[...] = _layer_norm(alpha * x_ref[...] + y, g_ref[...], b_ref[...])


def _even_out(x2d, ocmp, osel, owin, om, gate_t, nz_t, mz_t, w_out, ln_g, ln_b, alpha, *, B, S):
    T, D = x2d.shape
    tm = TQ
    nq = S // tm
    G, nh, H = NSA_KV_GROUPS, NSA_HG, MOBA_HEADS
    nsa = pl.BlockSpec((None, G, HEAD_DIM, nh * tm), lambda i: (i // nq, 0, 0, i % nq))
    feat = lambda n: pl.BlockSpec((n, tm), lambda i: (0, i))
    full = lambda a: pl.BlockSpec(a.shape, lambda i: (0,) * a.ndim)
    return pl.pallas_call(
        functools.partial(_even_out_kernel, alpha=alpha, tm=tm),
        grid=(T // tm,),
        in_specs=[pl.BlockSpec((tm, D), lambda i: (i, 0)), nsa, nsa, nsa,
                  pl.BlockSpec((None, H, HEAD_DIM, tm), lambda i: (i // nq, 0, 0, i % nq)),
                  feat(GATE_ROWS), feat(NSA_W), feat(MOBA_W),
                  full(w_out), full(ln_g), full(ln_b)],
        out_specs=pl.BlockSpec((tm, D), lambda i: (i, 0)),
        out_shape=jax.ShapeDtypeStruct((T, D), F32),
        scratch_shapes=[pltpu.VMEM((GATE_ROWS, tm), F32)],
        compiler_params=pltpu.CompilerParams(dimension_semantics=("parallel",),
                                             vmem_limit_bytes=VMEM_LIMIT),
        name="even_out",
    )(x2d, ocmp, osel, owin, om, gate_t, nz_t, mz_t, w_out, ln_g, ln_b)


def _gla_head_exact(row0, h, q_ref, v_ref, o_ref, st_ref, b_ref, kf_ref, scale):
    C, dk, dv, SUB = GLA_CHUNK, GLA_DK, GLA_DV, GLA_SUB
    rows = lax.broadcasted_iota(jnp.int32, (C, dk), 0)
    sub_i = lax.broadcasted_iota(jnp.int32, (SUB, LANES), 0)
    lane = lax.broadcasted_iota(jnp.int32, (SUB, LANES), 1)
    ones = jnp.ones((dk, LANES), BF16)
    sl = pl.ds(h * dk, dk)
    rs_c = pl.ds(row0, C)
    bh = b_ref[:, sl]
    kf = kf_ref[:, sl]
    qf = q_ref[rs_c, sl].astype(F32) * scale
    vh = v_ref[rs_c, pl.ds(h * dv, dv)]
    st = st_ref[h]
    inter = _dot_nt((qf * jnp.exp(bh)).astype(BF16), st.astype(BF16))
    bl = b_ref[pl.ds(C - 1, 1), sl]
    kd = kf * jnp.exp(bl - bh)

    def off_block(r0, r1, c0, c1):
        ref_row = b_ref[pl.ds(r0, 1), sl]
        qt = qf[r0:r1] * jnp.exp(bh[r0:r1] - ref_row)
        ek = jnp.where((rows >= c0) & (rows < c1), ref_row - bh, NEG)
        kt = kf * jnp.exp(ek)
        return _dot_nt(qt.astype(BF16), kt.astype(BF16))

    lower = off_block(2 * SUB, C, 0, 2 * SUB)
    blocks = [None, off_block(SUB, 2 * SUB, 0, SUB), lower[0:SUB],
              lower[SUB:2 * SUB] + off_block(3 * SUB, C, 2 * SUB, 3 * SUB)]
    pieces = []
    for blk in range(C // SUB):
        r0 = blk * SUB
        q_i = qf[r0:r0 + SUB]
        b_i = bh[r0:r0 + SUB]
        for jj in range(SUB):
            b_j = b_ref[pl.ds(r0 + jj, 1), sl]
            k_j = kf_ref[pl.ds(r0 + jj, 1), sl]
            pieces.append((q_i * k_j * jnp.exp(jnp.minimum(b_i - b_j, 0.0))).astype(BF16))
    sums = _dot(jnp.concatenate(pieces, axis=0), ones)
    for blk in range(C // SUB):
        r0 = blk * SUB
        diag = jnp.zeros((SUB, LANES), F32)
        for jj in range(SUB):
            rs = sums[(r0 + jj) * SUB:(r0 + jj + 1) * SUB]
            diag = jnp.where((lane == r0 + jj) & (sub_i >= jj), rs, diag)
        diag = diag[:, 0:C]
        blocks[blk] = diag if blocks[blk] is None else blocks[blk] + diag
    att = jnp.concatenate(blocks, axis=0)
    o_ref[rs_c, pl.ds(h * dv, dv)] = (inter + _dot(att.astype(BF16), vh)).astype(o_ref.dtype)
    st_ref[h] = st * jnp.exp(bl) + _dot_tn(vh, kd.astype(BF16))


def _gla_head_bounded(row0, h, q_ref, v_ref, o_ref, st_ref, b_ref, kf_ref, scale):
    C, dk, dv = GLA_CHUNK, GLA_DK, GLA_DV
    sl = pl.ds(h * dk, dk)
    rs_c = pl.ds(row0, C)
    bh = b_ref[rs_c, sl]
    kf = kf_ref[rs_c, sl]
    qf = q_ref[rs_c, sl].astype(F32) * scale
    vh = v_ref[rs_c, pl.ds(h * dv, dv)]
    st = st_ref[h]
    qs = (qf * jnp.exp(bh)).astype(BF16)
    kt = (kf * jnp.exp(-bh)).astype(BF16)
    i = lax.broadcasted_iota(jnp.int32, (C, C), 0)
    j = lax.broadcasted_iota(jnp.int32, (C, C), 1)
    att = jnp.where(i >= j, _dot_nt(qs, kt), 0.0)
    inter = _dot_nt(qs, st.astype(BF16))
    o_ref[rs_c, pl.ds(h * dv, dv)] = (inter + _dot(att.astype(BF16), vh)).astype(o_ref.dtype)
    bl = b_ref[pl.ds(row0 + C - 1, 1), sl]
    kd = kf * jnp.exp(bl - bh)
    st_ref[h] = st * jnp.exp(bl) + _dot_tn(vh, kd.astype(BF16))


def _gla_kernel(q_ref, k_ref, v_ref, glr_ref, w2_ref, gb_ref, tril_ref, o_ref,
                st_ref, b_ref, kf_ref, bc_ref, kc_ref, *, scale):
    C, H = GLA_CHUNK, GLA_HEADS
    nch = q_ref.shape[0] // C

    @pl.when(pl.program_id(1) == 0)
    def _init():
        st_ref[...] = jnp.zeros(st_ref.shape, F32)

    whi, wlo = _hilo(w2_ref[...])
    part = tril_ref.shape[0]
    decay = None
    for r0 in range(0, nch * C, part):
        rs = pl.ds(r0, part)
        ghi, glo = _hilo(glr_ref[rs, :])
        pre = _dot(ghi, whi) + _dot(glo, whi) + _dot(ghi, wlo) + gb_ref[...]
        la = (jnp.minimum(pre, 0.0) - jnp.log(1.0 + jnp.exp(-jnp.abs(pre)))) * (1.0 / GLA_TAU)
        lhi, llo = _hilo(la)
        b_part = _dot(tril_ref[...], lhi) + _dot(tril_ref[...], llo)
        b_ref[rs, :] = b_part
        for c in range(part // C):
            d_c = -b_part[c * C + C - 1:(c + 1) * C, :]
            decay = d_c if decay is None else jnp.maximum(decay, d_c)
    kf_ref[...] = k_ref[...].astype(F32)
    bounded = jnp.max(decay) <= GLA_FACTOR_MAX_DECAY

    @pl.when(bounded)
    def _factored():
        for c in range(nch):
            for h in range(H):
                _gla_head_bounded(c * C, h, q_ref, v_ref, o_ref, st_ref, b_ref, kf_ref, scale)

    @pl.when(jnp.logical_not(bounded))
    def _exact():
        def chunk(c, carry):
            row0 = pl.multiple_of(c * C, C)
            bc_ref[...] = b_ref[pl.ds(row0, C), :]
            kc_ref[...] = kf_ref[pl.ds(row0, C), :]
            for h in range(H):
                _gla_head_exact(row0, h, q_ref, v_ref, o_ref, st_ref, bc_ref, kc_ref, scale)
            return carry
        lax.fori_loop(0, nch, chunk, 0)


def _gla(q, k, v, glr, w2p, gb, tril, *, B, S):
    C, H, dk, dv = GLA_CHUNK, GLA_HEADS, GLA_DK, GLA_DV
    rows = C * min(GLA_STEP_CHUNKS, S // C)
    nchunk = S // rows
    tok = lambda n: pl.BlockSpec((rows, n), lambda b, c: (b * nchunk + c, 0))
    const = lambda shape: pl.BlockSpec(shape, lambda b, c: (0,) * len(shape))
    return pl.pallas_call(
        functools.partial(_gla_kernel, scale=float(dk) ** -0.5),
        grid=(B, nchunk),
        in_specs=[tok(H * dk), tok(H * dk), tok(H * dv), tok(LANES),
                  const(w2p.shape), const(gb.shape), const(tril.shape)],
        out_specs=tok(H * dv),
        out_shape=jax.ShapeDtypeStruct((B * S, H * dv), BF16),
        scratch_shapes=[pltpu.VMEM((H, dv, dk), F32),
                        pltpu.VMEM((rows, H * dk), F32),
                        pltpu.VMEM((rows, H * dk), F32),
                        pltpu.VMEM((C, H * dk), F32),
                        pltpu.VMEM((C, H * dk), F32)],
        compiler_params=pltpu.CompilerParams(dimension_semantics=("parallel", "arbitrary"),
                                             vmem_limit_bytes=VMEM_LIMIT),
        name="gla",
    )(q, k, v, glr, w2p, gb, tril)


def _odd_out_kernel(x_ref, o_ref_in, z_ref, gn_ref, w_ref, g_ref, b_ref, out_ref, *, alpha):
    H, dv = GLA_HEADS, GLA_DV
    parts = []
    for h in range(H):
        sl = pl.ds(h * dv, dv)
        oh = o_ref_in[:, sl].astype(F32)
        r = lax.rsqrt(jnp.mean(oh * oh, axis=-1, keepdims=True) + LN_EPS)
        z = z_ref[:, sl].astype(F32)
        parts.append((oh * r * gn_ref[:, sl] * (z * _sigmoid(z))).astype(BF16))
    y = _dot(jnp.concatenate(parts, axis=1), w_ref[...])
    out_ref[...] = _layer_norm(alpha * x_ref[...] + y, g_ref[...], b_ref[...])


def _odd_out(x2d, o, z, gn_g, w_out, ln_g, ln_b, alpha, tm=1024):
    T, D = x2d.shape
    W = o.shape[1]
    row = lambda n: pl.BlockSpec((tm, n), lambda i: (i, 0))
    return pl.pallas_call(
        functools.partial(_odd_out_kernel, alpha=alpha),
        grid=(T // tm,),
        in_specs=[row(D), row(W), row(W),
                  pl.BlockSpec((1, W), lambda i: (0, 0)),
                  pl.BlockSpec((W, D), lambda i: (0, 0)),
                  pl.BlockSpec((1, D), lambda i: (0, 0)),
                  pl.BlockSpec((1, D), lambda i: (0, 0))],
        out_specs=row(D),
        out_shape=jax.ShapeDtypeStruct((T, D), F32),
        compiler_params=pltpu.CompilerParams(dimension_semantics=("parallel",),
                                             vmem_limit_bytes=VMEM_LIMIT),
        name="odd_out",
    )(x2d, o, z, gn_g, w_out, ln_g, ln_b)


def _key_features(S, block):
    s = np.arange(S)
    f = np.zeros((S, LANES), np.float32)
    f[s, s // block] = 1.0
    f[:, FEAT + 0] = 1.0
    f[:, FEAT + 1] = 1.0
    f[:, FEAT + 2] = FEAT * (s // FEAT)
    f[:, FEAT + 3] = s % FEAT
    return jnp.asarray(f, BF16)


def _pad_to(a, n, axis):
    pad = [(0, 0)] * a.ndim
    pad[axis] = (0, n - a.shape[axis])
    return jnp.pad(a, pad)


def _even_layer(x2d, B, S, w_in, cmp_pos, cmp_w1, cmp_b1, cmp_w2, w_out, ln_g, ln_b, alpha):
    T, D = x2d.shape
    G, hd = NSA_KV_GROUPS, HEAD_DIM
    scale = hd ** -0.5
    wq, wkv, wgate, wnz, wmq, wmk, wmv, wmz = jnp.split(w_in, np.cumsum(EVEN_SPLITS)[:-1].tolist(), axis=1)
    wkc, wvc, wks, wvs, wkw, wvw = jnp.split(wkv, 6, axis=1)
    w_tok = jnp.concatenate([wkc, wvc, wks, wkw, wmk], axis=1).astype(BF16)
    w_tr = jnp.concatenate([wq * scale, wvs, wvw, wmq * scale, wmv, wnz, wmz,
                            _pad_to(wgate, GATE_ROWS, 1)], axis=1).T.astype(BF16)
    (kcmp, vcmp, ksel, kwin, mk, nq_t, vsel_t, vwin_t, mq_t, mv_t, nz_t, mz_t, gate_t) = _proj(
        x2d, w_tok, (NSA_KV_W,) * 4 + (MOBA_W,), (F32, F32, BF16, BF16, BF16),
        w_tr, (NSA_W, NSA_KV_W, NSA_KV_W, MOBA_W, MOBA_W, NSA_W, MOBA_W, GATE_ROWS),
        (BF16,) * 7 + (F32,))

    L, d, HID = NSA_CMP_BLOCK, NSA_CMP_STRIDE, NSA_CMP_HIDDEN
    ncmp = (S - L) // d + 1
    NC = S // d
    def group_diagonal(w, axis):
        z = jnp.zeros_like(w)
        return jnp.stack([jnp.concatenate([w if h == g else z for h in range(G)], axis=-1) for g in range(G)], axis=axis)

    w1r = cmp_w1.reshape(2, 2, d, hd, HID).astype(BF16)
    w1p = group_diagonal(w1r, 3).reshape(2, 2, d * G * hd, G * HID)
    posp = jnp.broadcast_to(cmp_pos.reshape(2, 2, d, 1, hd), (2, 2, d, G, hd)).reshape(2, 2, 1, d * G * hd)
    b1p = jnp.tile(cmp_b1, (1, G)).reshape(2, 1, G * HID)
    w2p = group_diagonal(_pad_to(cmp_w2, LANES, 2).astype(BF16), 1).reshape(2, G * HID, G * LANES)
    cfeat = np.zeros((NC, G, LANES), np.float32)
    cfeat[:, :, hd + 0] = 1.0
    cfeat[:, :, hd + 1] = 1.0
    cfeat[:, :, hd + 2] = (np.arange(NC) * d)[:, None]
    cfeat[:, :, hd + 3] = L - 1
    kvc = _cmp_mlp(kcmp, vcmp, w1p, posp, b1p, w2p, jnp.asarray(cfeat.reshape(NC, G * LANES)), ncmp,
                   B=B, S=S)

    ns = S // NSA_SEL_BLOCK
    cmp_start = np.arange(NC) * d
    sel_start = np.arange(FEAT) * NSA_SEL_BLOCK
    ovt = ((cmp_start[None, :] <= sel_start[:, None] + NSA_SEL_BLOCK - 1)
           & (cmp_start[None, :] + L - 1 >= sel_start[:, None])
           & (np.arange(NC)[None, :] < ncmp) & (np.arange(FEAT)[:, None] < ns))
    o_cmp, bias_sel = _cmp_attn(nq_t, kvc, jnp.asarray(ovt, BF16), B=B, S=S)

    feats_sel = _key_features(S, NSA_SEL_BLOCK)
    o_sel = _flash(nq_t, bias_sel, ksel, feats_sel, vsel_t,
                   B=B, S=S, units=G, tq=TQ, nh=NSA_HG, pair=False, window=None)
    o_win = _flash(nq_t, None, kwin, feats_sel, vwin_t,
                   B=B, S=S, units=G, tq=TQ, nh=NSA_HG, pair=False, window=NSA_WINDOW)

    avg = np.zeros((MOBA_NBP, S), np.float32)
    avg[np.arange(S) // MOBA_BLOCK, np.arange(S)] = 1.0 / MOBA_BLOCK
    bias_m = _moba_gate(mq_t, mk, jnp.asarray(avg, BF16), B=B, S=S)
    o_m = _flash(mq_t, bias_m, mk, _key_features(S, MOBA_BLOCK), mv_t,
                 B=B, S=S, units=MOBA_HEADS // MOBA_STEP_HEADS, tq=TQM, nh=MOBA_STEP_HEADS, pair=True, window=None)

    return _even_out(x2d, o_cmp, o_sel, o_win, o_m, gate_t, nz_t, mz_t, w_out.astype(BF16),
                     ln_g.reshape(1, D), ln_b.reshape(1, D), alpha, B=B, S=S)


def _odd_layer(x2d, B, S, w_in, gate_w2, gate_b, gn_g, w_out, ln_g, ln_b, alpha):
    T, D = x2d.shape
    H, dk, dv, C = GLA_HEADS, GLA_DK, GLA_DV, GLA_CHUNK
    wq, wk, wv, wg, wz = jnp.split(w_in, np.cumsum(ODD_SPLITS)[:-1].tolist(), axis=1)
    w_perm = jnp.concatenate([wq, wk, wv, wz, _pad_to(wg, LANES, 1)], axis=1).astype(BF16)
    q, k, v, z, glr = _proj(x2d, w_perm, (H * dk, H * dk, H * dv, H * dv, LANES),
                            (BF16, BF16, BF16, BF16, F32))
    w2p = _pad_to(gate_w2, LANES, 0)
    ncs = min(GLA_CUMSUM_CHUNKS, GLA_STEP_CHUNKS, S // C)
    tril = jnp.asarray(np.kron(np.eye(ncs), np.tril(np.ones((C, C)))), BF16)
    o = _gla(q, k, v, glr, w2p, gate_b.reshape(1, H * dk), tril, B=B, S=S)
    return _odd_out(x2d, o, z, gn_g.reshape(1, H * dv), w_out.astype(BF16),
                    ln_g.reshape(1, D), ln_b.reshape(1, D), alpha)


def kernel(x, ev_w_in, ev_cmp_pos, ev_cmp_w1, ev_cmp_b1, ev_cmp_w2, ev_w_out, ev_ln_g, ev_ln_b,
           od_w_in, od_gate_w2, od_gate_b, od_gn_g, od_w_out, od_ln_g, od_ln_b):
    B, S, D = x.shape
    depth = ev_w_in.shape[0] + od_w_in.shape[0]
    alpha = (2.0 * depth) ** 0.25
    h = x.reshape(B * S, D)
    for layer in range(depth):
        i = layer // 2
        if layer % 2 == 0:
            h = _even_layer(h, B, S, ev_w_in[i], ev_cmp_pos[i], ev_cmp_w1[i], ev_cmp_b1[i], ev_cmp_w2[i],
                            ev_w_out[i], ev_ln_g[i], ev_ln_b[i], alpha)
        else:
            h = _odd_layer(h, B, S, od_w_in[i], od_gate_w2[i], od_gate_b[i], od_gn_g[i],
                           od_w_out[i], od_ln_g[i], od_ln_b[i], alpha)
    return h.reshape(B, S, D)
```

```python
import functools

import numpy as np
import jax
import jax.numpy as jnp
from jax import lax
from jax.experimental import pallas as pl
from jax.experimental.pallas import tpu as pltpu

F32 = jnp.float32
BF16 = jnp.bfloat16

HEAD_DIM = 64
NSA_HEADS = 8
NSA_KV_GROUPS = 2
NSA_HG = NSA_HEADS // NSA_KV_GROUPS
NSA_CMP_BLOCK = 32
NSA_CMP_STRIDE = 16
NSA_CMP_HIDDEN = 128
NSA_SEL_BLOCK = 64
NSA_TOP_N = 16
NSA_WINDOW = 512
MOBA_HEADS = 8
MOBA_BLOCK = 256
MOBA_TOP_K = 3
GLA_HEADS = 4
GLA_DK = 128
GLA_DV = 256
GLA_GATE_RANK = 16
GLA_TAU = 16.0
GLA_CHUNK = 64
GLA_SUB = 16
GLA_STEP_CHUNKS = 16
GLA_CUMSUM_CHUNKS = 4
GLA_FACTOR_MAX_DECAY = 60.0
LN_EPS = 1e-5
NEG = -1e30
FORCE_BONUS = 1e4

NSA_W = NSA_HEADS * HEAD_DIM
NSA_KV_W = NSA_KV_GROUPS * HEAD_DIM
MOBA_W = MOBA_HEADS * HEAD_DIM
EVEN_SPLITS = (NSA_W, 6 * NSA_KV_W, 3 * NSA_HEADS, NSA_W, MOBA_W, MOBA_W, MOBA_W, MOBA_W)
ODD_SPLITS = (GLA_HEADS * GLA_DK, GLA_HEADS * GLA_DK, GLA_HEADS * GLA_DV, GLA_GATE_RANK, GLA_HEADS * GLA_DV)

LANES = 128
SUBLANES = 8
FEAT = 64
ALIBI_ROWS = 16
GATE_ROWS = 32
MOBA_NBP = 16
TQ = 512
TQM = 512
MOBA_STEP_HEADS = 4
TK = 512
CG = 256
MIN_WEIGHT_SUM = 1e-25
VROWS = HEAD_DIM + 16
VMEM_LIMIT = 48 * 1024 * 1024


def _dot(a, b):
    return jnp.dot(a, b, preferred_element_type=F32)


def _dot_nt(a, b):
    return lax.dot_general(a, b, (((1,), (1,)), ((), ())), preferred_element_type=F32)


def _dot_tn(a, b):
    return lax.dot_general(a, b, (((0,), (0,)), ((), ())), preferred_element_type=F32)


def _hilo(a):
    hi = a.astype(BF16)
    lo = (a - hi.astype(F32)).astype(BF16)
    return hi, lo


def _pow2_neg(n):
    return lax.bitcast_convert_type(lax.shift_left(127 - n, 23), F32)


def _sigmoid(x):
    return 0.5 * jnp.tanh(0.5 * x) + 0.5


def _layer_norm(r, g, b):
    mu = jnp.mean(r, axis=-1, keepdims=True)
    d = r - mu
    var = jnp.mean(d * d, axis=-1, keepdims=True)
    return d * lax.rsqrt(var + LN_EPS) * g + b


def _alibi_query_rows(t0, tq, head0, R):
    r = lax.broadcasted_iota(jnp.int32, (ALIBI_ROWS, R), 0)
    col = lax.broadcasted_iota(jnp.int32, (ALIBI_ROWS, R), 1)
    t = t0 + col % tq
    slope = _pow2_neg(head0 + col // tq + 1)
    feats = jnp.where(r == 0, -FEAT * slope * (t // FEAT).astype(F32),
                      jnp.where(r == 1, -slope * (t % FEAT).astype(F32), jnp.where(r < 4, slope, 0.0)))
    return jnp.concatenate([feats.astype(BF16), jnp.zeros((FEAT - ALIBI_ROWS, R), BF16)], axis=0)


def _ranks(sc_ref, n_real, n_rows=None, cols=slice(None)):
    n_rows = sc_ref.shape[0] if n_rows is None else n_rows
    out = []
    for v in range(n_rows // SUBLANES):
        lo = v * SUBLANES
        sc = sc_ref[lo:lo + SUBLANES, cols]
        width = sc.shape[1]
        idx = lo + lax.broadcasted_iota(jnp.int32, (SUBLANES, width), 0)
        rank = jnp.zeros((SUBLANES, width), jnp.int32)
        for jp in range(n_real):
            r = sc_ref[pl.ds(jp, 1), cols]
            if jp < lo:
                beats = r >= sc
            elif jp >= lo + SUBLANES - 1:
                beats = r > sc
            else:
                beats = (r > sc) | ((r == sc) & (idx > jp))
            rank = rank + jnp.where(beats, 1, 0)
        out.append(rank)
    return jnp.concatenate(out, axis=0)


def _proj_kernel(*refs, tok_splits, tr_splits):
    x_ref, wt_ref = refs[0], refs[1]
    outs = list(refs[3:] if tr_splits else refs[2:])
    x = x_ref[...]
    xb = x.astype(BF16)
    off = 0
    for n in tok_splits:
        o_ref = outs.pop(0)
        o_ref[...] = _dot(xb, wt_ref[:, off:off + n]).astype(o_ref.dtype)
        off += n
    if tr_splits:
        wtr_ref = refs[2]
        xt = x.T.astype(BF16)
        off = 0
        for n in tr_splits:
            o_ref = outs.pop(0)
            o_ref[...] = _dot(wtr_ref[off:off + n, :], xt).astype(o_ref.dtype)
            off += n


def _proj(x2d, w_tok, tok_splits, tok_dtypes, w_tr=None, tr_splits=(), tr_dtypes=(), tm=512):
    T, D = x2d.shape
    assert T % tm == 0 and sum(tok_splits) == w_tok.shape[1]
    in_specs = [pl.BlockSpec((tm, D), lambda i: (i, 0)),
                pl.BlockSpec(w_tok.shape, lambda i: (0, 0))]
    args = [x2d, w_tok]
    if tr_splits:
        assert sum(tr_splits) == w_tr.shape[0]
        in_specs.append(pl.BlockSpec(w_tr.shape, lambda i: (0, 0)))
        args.append(w_tr)
    out_specs = ([pl.BlockSpec((tm, n), lambda i: (i, 0)) for n in tok_splits]
                 + [pl.BlockSpec((n, tm), lambda i: (0, i)) for n in tr_splits])
    out_shape = ([jax.ShapeDtypeStruct((T, n), dt) for n, dt in zip(tok_splits, tok_dtypes)]
                 + [jax.ShapeDtypeStruct((n, T), dt) for n, dt in zip(tr_splits, tr_dtypes)])
    return pl.pallas_call(
        functools.partial(_proj_kernel, tok_splits=tuple(tok_splits), tr_splits=tuple(tr_splits)),
        grid=(T // tm,), in_specs=in_specs, out_specs=out_specs, out_shape=out_shape,
        compiler_params=pltpu.CompilerParams(dimension_semantics=("parallel",),
                                             vmem_limit_bytes=VMEM_LIMIT),
        name="proj",
    )(*args)


def _cmp_mlp_kernel(kx_ref, vx_ref, w1_ref, pos_ref, b1_ref, w2_ref, cf_ref, o_ref, *, ncmp):
    d = NSA_CMP_STRIDE
    nc = kx_ref.shape[0] // d
    cw = w1_ref.shape[2]
    for i, x_ref in enumerate((kx_ref, vx_ref)):
        a = bm = None
        for l in range(0, d, 2):
            x2 = jnp.concatenate([x_ref[pl.ds(l, nc, stride=d), :], x_ref[pl.ds(l + 1, nc, stride=d), :]],
                                 axis=1).astype(BF16)
            rows = pl.ds(l * x_ref.shape[1], 2 * x_ref.shape[1])
            a_l, b_l = _dot(x2, w1_ref[i, 0, rows, :]), _dot(x2, w1_ref[i, 1, rows, :])
            a, bm = (a_l, b_l) if a is None else (a + a_l, bm + b_l)
        c = b1_ref[i]
        for half in range(2):
            phi, plo = _hilo(jnp.broadcast_to(pos_ref[i, half], (SUBLANES, cw)))
            c = c + (_dot(phi, w1_ref[i, half]) + _dot(plo, w1_ref[i, half]))[0:1]
        hid = a + pltpu.roll(bm, nc - 1, axis=0) + c
        hid = hid * _sigmoid(hid)
        row = lax.broadcasted_iota(jnp.int32, hid.shape, 0)
        hid = jnp.where(row < ncmp, hid, 0.0)
        out = _dot(hid.astype(BF16), w2_ref[i])
        if i == 0:
            out = out + cf_ref[...]
        o_ref[i] = out.astype(o_ref.dtype)


def _cmp_mlp(kx, vx, w1p, posp, b1p, w2p, cfeat, ncmp, *, B, S):
    NC = S // NSA_CMP_STRIDE
    OW = w2p.shape[-1]
    full = lambda a: pl.BlockSpec(a.shape, lambda b: (0,) * a.ndim)
    return pl.pallas_call(
        functools.partial(_cmp_mlp_kernel, ncmp=ncmp),
        grid=(B,),
        in_specs=[pl.BlockSpec((S, kx.shape[1]), lambda b: (b, 0)),
                  pl.BlockSpec((S, vx.shape[1]), lambda b: (b, 0)),
                  full(w1p), full(posp), full(b1p), full(w2p), full(cfeat)],
        out_specs=pl.BlockSpec((2, None, NC, OW), lambda b: (0, b, 0, 0)),
        out_shape=jax.ShapeDtypeStruct((2, B, NC, OW), BF16),
        compiler_params=pltpu.CompilerParams(dimension_semantics=("parallel",),
                                             vmem_limit_bytes=VMEM_LIMIT),
        name="cmp_mlp",
    )(kx, vx, w1p, posp, b1p, w2p, cfeat)


def _cmp_attn_kernel(q_ref, kc_ref, vc_ref, ovt_ref, o_ref, bias_ref, sc_ref, *, tq, nh, ns, ntop):
    g = pl.program_id(1)
    R = nh * tq
    nq = q_ref.shape[1] // tq

    def q_tile(qi):
        toks = pl.ds(qi * tq, tq)
        nc = min(kc_ref.shape[0], -(-((qi + 1) * tq) // (NSA_CMP_STRIDE * 16)) * 16)
        nsel = min(ns, -(-((qi + 1) * tq) // (NSA_SEL_BLOCK * SUBLANES)) * SUBLANES)
        qcat = jnp.concatenate([q_ref[hh * HEAD_DIM:(hh + 1) * HEAD_DIM, toks] for hh in range(nh)], axis=1)
        qa = jnp.concatenate([qcat, _alibi_query_rows(qi * tq, tq, g * nh, R)], axis=0)
        s = _dot(kc_ref[0:nc, :], qa)
        n = lax.broadcasted_iota(jnp.int32, (nc, R), 0)
        t_row = qi * tq + lax.broadcasted_iota(jnp.int32, (1, R), 1) % tq
        last = (t_row - (NSA_CMP_BLOCK - 1)) // NSA_CMP_STRIDE
        s = jnp.where(n <= last, s, NEG)
        e = jnp.exp(s - jnp.max(s, axis=0, keepdims=True))
        inv = jnp.where(last >= 0, 1.0 / jnp.sum(e, axis=0, keepdims=True), 0.0)
        p = e * inv
        o_ref[:, pl.ds(qi * R, R)] = _dot_tn(vc_ref[0:nc, 0:HEAD_DIM], p.astype(BF16)).astype(o_ref.dtype)

        psum = p[:, 0:tq]
        for hh in range(1, nh):
            psum = psum + p[:, hh * tq:(hh + 1) * tq]
        phi, plo = _hilo(psum)
        ovt = ovt_ref[0:nsel, 0:nc]
        imp = _dot(ovt, phi) + _dot(ovt, plo)

        j = lax.broadcasted_iota(jnp.int32, (nsel, tq), 0)
        cur = (qi * tq + lax.broadcasted_iota(jnp.int32, (nsel, tq), 1)) // NSA_SEL_BLOCK
        if nsel <= ntop:
            sel = j <= cur
        else:
            forced = jnp.where((j == 0) | (j == cur) | (j == cur - 1), FORCE_BONUS, 0.0)
            sc_ref[0:nsel, :] = jnp.where(j <= cur, imp + forced, NEG)
            sel = (_ranks(sc_ref, nsel, nsel) < ntop) & (j <= cur)
        bias_ref[0:nsel, toks] = jnp.where(sel, 0.0, NEG).astype(bias_ref.dtype)
        if nsel < FEAT:
            bias_ref[nsel:FEAT, toks] = jnp.full((FEAT - nsel, tq), NEG, bias_ref.dtype)

    for qi in range(nq):
        q_tile(qi)


def _cmp_attn(nq_t, kvc, ovt, *, B, S):
    tq, nh, G = TQ, NSA_HG, NSA_KV_GROUPS
    R = nh * tq
    nq = S // tq
    NC = kvc.shape[2]
    ns = S // NSA_SEL_BLOCK
    assert ns <= FEAT
    ntop = min(NSA_TOP_N, ns)
    return pl.pallas_call(
        functools.partial(_cmp_attn_kernel, tq=tq, nh=nh, ns=ns, ntop=ntop),
        grid=(B, G),
        in_specs=[pl.BlockSpec((nh * HEAD_DIM, S), lambda b, g: (g, b)),
                  pl.BlockSpec((None, None, NC, LANES), lambda b, g: (0, b, 0, g)),
                  pl.BlockSpec((None, None, NC, LANES), lambda b, g: (1, b, 0, g)),
                  pl.BlockSpec(ovt.shape, lambda b, g: (0, 0))],
        out_specs=[pl.BlockSpec((None, None, HEAD_DIM, S * nh), lambda b, g: (b, g, 0, 0)),
                   pl.BlockSpec((None, None, FEAT, S), lambda b, g: (b, g, 0, 0))],
        out_shape=[jax.ShapeDtypeStruct((B, G, HEAD_DIM, S * nh), BF16),
                   jax.ShapeDtypeStruct((B, G, FEAT, S), BF16)],
        scratch_shapes=[pltpu.VMEM((FEAT, tq), F32)],
        compiler_params=pltpu.CompilerParams(dimension_semantics=("parallel", "parallel"),
                                             vmem_limit_bytes=VMEM_LIMIT),
        name="cmp_attn",
    )(nq_t, kvc, kvc, ovt)


def _moba_gate_kernel(q_ref, k_ref, a_ref, bias_ref, sc_ref, *, nb, ntop):
    nbp, S = sc_ref.shape[1:]
    km = _dot(a_ref[...], k_ref[...])
    lane = lax.broadcasted_iota(jnp.int32, (nbp, LANES), 1)
    parts = []
    for hh in range(2):
        parts += _hilo(jnp.where(lane // HEAD_DIM == hh, km, 0.0))
    g = _dot(jnp.concatenate(parts, axis=0), q_ref[...])
    past = (lax.broadcasted_iota(jnp.int32, (nbp, S), 0)
            < lax.broadcasted_iota(jnp.int32, (nbp, S), 1) // MOBA_BLOCK)
    n = lax.broadcasted_iota(jnp.int32, (nbp, MOBA_BLOCK), 0)
    for hh in range(2):
        sc = sc_ref.at[hh]
        sc[...] = jnp.where(past, g[2 * hh * nbp:(2 * hh + 1) * nbp] + g[(2 * hh + 1) * nbp:(2 * hh + 2) * nbp], NEG)
        for c in range(nb):
            cols = slice(c * MOBA_BLOCK, (c + 1) * MOBA_BLOCK)
            sel = n <= c
            if c > ntop:
                nr = -(-c // SUBLANES) * SUBLANES
                rank = _ranks(sc, c, nr, cols)
                rank = rank + jnp.where(sc[0:nr, cols] < NEG, nb - c, 0)
                if nr < nbp:
                    rank = jnp.concatenate([rank, jnp.full((nbp - nr, MOBA_BLOCK), nb, jnp.int32)], axis=0)
                sel = ((rank < ntop) & (n < c)) | (n == c)
            bias_ref[hh, 0:nbp, cols] = jnp.where(sel, 0.0, NEG).astype(bias_ref.dtype)
        bias_ref[hh, nbp:FEAT, :] = jnp.zeros((FEAT - nbp, S), bias_ref.dtype)


def _moba_gate(mq_t, mk, avg, *, B, S):
    H = MOBA_HEADS
    nb = S // MOBA_BLOCK
    nbp = avg.shape[0]
    assert nb <= nbp and S % MOBA_BLOCK == 0 and MOBA_BLOCK % LANES == 0 and H % 2 == 0
    ntop = min(MOBA_TOP_K, nb)
    return pl.pallas_call(
        functools.partial(_moba_gate_kernel, nb=nb, ntop=ntop),
        grid=(B, H // 2),
        in_specs=[pl.BlockSpec((2 * HEAD_DIM, S), lambda b, h: (h, b)),
                  pl.BlockSpec((S, LANES), lambda b, h: (b, h)),
                  pl.BlockSpec(avg.shape, lambda b, h: (0, 0))],
        out_specs=pl.BlockSpec((None, 2, FEAT, S), lambda b, h: (b, h, 0, 0)),
        out_shape=jax.ShapeDtypeStruct((B, H, FEAT, S), BF16),
        scratch_shapes=[pltpu.VMEM((2, nbp, S), F32)],
        compiler_params=pltpu.CompilerParams(dimension_semantics=("parallel", "parallel"),
                                             vmem_limit_bytes=VMEM_LIMIT),
        name="moba_gate",
    )(mq_t, mk, avg)


def _flash_q_tile(qi, running_max, q_ref, bias_ref, mask_ref, o_ref, ka_ref, vta_ref, kn_ref, qa_ref, bnd_ref,
                  m_ref, acc_ref, alpha_ref, p_ref, *, tq, tk, nh, pair, window, has_past=None):
    u = pl.program_id(1)
    R = nh * tq
    static = isinstance(qi, int)
    q0 = qi * tq if static else pl.multiple_of(qi * tq, tq)
    toks = pl.ds(q0, tq)

    qcat = jnp.concatenate([q_ref[hh * HEAD_DIM:(hh + 1) * HEAD_DIM, toks] for hh in range(nh)], axis=1)
    qf = qcat.astype(F32)
    qq = jnp.concatenate([qf, qf], axis=0)
    rowi = lax.broadcasted_iota(jnp.int32, (2 * HEAD_DIM, R), 0)
    coli = lax.broadcasted_iota(jnp.int32, (2 * HEAD_DIM, R), 1)
    half = (coli // tq) % 2 if pair else u % 2
    qa_ref[0:2 * HEAD_DIM, :] = jnp.where(rowi // HEAD_DIM == half, qq, 0.0).astype(BF16)
    if bias_ref is None:
        qa_ref[2 * HEAD_DIM:2 * HEAD_DIM + FEAT, :] = jnp.zeros((FEAT, R), BF16)
    else:
        qa_ref[2 * HEAD_DIM:2 * HEAD_DIM + FEAT, :] = jnp.concatenate(
            [bias_ref[hh if pair else 0, :, toks] for hh in range(nh)], axis=1)
    qa_ref[2 * HEAD_DIM + FEAT:2 * LANES, :] = _alibi_query_rows(q0, tq, u * nh, R)
    if pair:
        kn = jnp.concatenate([kn_ref[hh:hh + 1, :] for hh in range(nh) for _ in range(tq // LANES)], axis=1)
    else:
        kn = jnp.concatenate([jnp.where(u % 2 == 0, kn_ref[0:1, :], kn_ref[1:2, :])] * (R // LANES), axis=1)
    qn = jnp.sqrt(jnp.sum(qf * qf, axis=0, keepdims=True))
    bnd_ref[...] = jnp.broadcast_to(qn * kn, bnd_ref.shape)

    ncg = R // CG
    kd = (q0 + tq - 1) // tk

    def key_rows(kind, c):
        off = (c * CG) % tq
        if kind is None:
            return 0, tk, None
        if kind == "diag":
            return 0, off + CG, off // CG
        return off, tk, (tk + off) // CG

    def values(ki, kind=None):
        for c in range(ncg):
            cols = pl.ds(c * CG, CG)
            r0, r1, _ = key_rows(kind, c)
            v = (c * CG) // tq if pair else 0
            pv = _dot(vta_ref[v, ki, :, r0:r1], p_ref[r0:r1, cols])
            if running_max:
                acc_ref[:, cols] = alpha_ref[0:1, cols] * acc_ref[:, cols] + pv
            else:
                acc_ref[:, cols] = acc_ref[:, cols] + pv

    def scores(ki, kind=None, prev=None, prev_kind=None):
        k0 = ki * tk if isinstance(ki, int) else pl.multiple_of(ki * tk, tk)
        s_groups = []
        for c in range(ncg):
            r0, r1, _ = key_rows(kind, c)
            kb = (c * CG) // (2 * tq) if pair else 0
            s_groups.append(_dot(ka_ref[kb, pl.ds(k0 + r0, r1 - r0), :], qa_ref[:, pl.ds(c * CG, CG)]))
        if prev is not None:
            values(prev, prev_kind)
        for c, s in enumerate(s_groups):
            cols = pl.ds(c * CG, CG)
            r0, r1, entry = key_rows(kind, c)
            if entry is not None:
                s = s + mask_ref[entry, r0:r1, :]
            if running_max:
                m_prev = m_ref[0:1, cols]
                m_new = jnp.maximum(m_prev, jnp.max(s, axis=0, keepdims=True))
                alpha_ref[:, cols] = jnp.broadcast_to(jnp.exp(m_prev - m_new), (alpha_ref.shape[0], CG))
                m_ref[:, cols] = jnp.broadcast_to(m_new, (m_ref.shape[0], CG))
                p_ref[r0:r1, cols] = jnp.exp(s - m_new).astype(BF16)
            else:
                p_ref[r0:r1, cols] = jnp.exp(s - bnd_ref[0:1, cols]).astype(BF16)

    def steady(lo, hi):
        def pair_body(j, carry):
            ki = lo + 2 * j
            scores(ki, prev=ki - 1)
            scores(ki + 1, prev=ki)
            return carry
        lax.fori_loop(0, (hi - lo) // 2, pair_body, 0)

        @pl.when((hi - lo) % 2 == 1)
        def _odd_one():
            scores(hi - 1, prev=hi - 2)

    def only_diagonal():
        scores(0, "diag")
        values(0, "diag")

    def past_then_diagonal():
        if window is not None:
            scores(kd - 1, "wedge")
            scores(kd, "diag", prev=kd - 1, prev_kind="wedge")
        elif static:
            scores(0)
            for ki in range(1, kd):
                scores(ki, prev=ki - 1)
            scores(kd, "diag", prev=kd - 1)
        else:
            scores(0)
            steady(1, kd)
            scores(kd, "diag", prev=kd - 1)
        values(kd, "diag")

    acc_ref[...] = jnp.zeros(acc_ref.shape, F32)
    if running_max:
        m_ref[...] = jnp.full(m_ref.shape, NEG, F32)
    if static:
        only_diagonal() if kd == 0 else past_then_diagonal()
    elif has_past:
        past_then_diagonal()
    else:
        pl.when(kd == 0)(only_diagonal)
        pl.when(kd > 0)(past_then_diagonal)

    acc = acc_ref[...]
    out = (acc[0:HEAD_DIM] / acc[HEAD_DIM:HEAD_DIM + 1]).astype(o_ref.dtype)
    if pair:
        for hh in range(nh):
            o_ref[hh, :, toks] = out[:, hh * tq:(hh + 1) * tq]
    else:
        o_ref[0, :, pl.ds(qi * R if static else pl.multiple_of(qi * R, R), R)] = out
    return jnp.min(acc[HEAD_DIM:HEAD_DIM + 1, :])


def _flash_kernel(*refs, tq, tk, nh, pair, window, has_bias):
    q_ref = refs[0]
    bias_ref = refs[1] if has_bias else None
    (k_ref, kf_ref, vt_ref, mask_ref, o_ref, ka_ref, vta_ref, kn_ref, qa_ref, bnd_ref, m_ref, acc_ref,
     alpha_ref, p_ref, low_ref) = refs[2 if has_bias else 1:]
    S = k_ref.shape[0]
    for kb in range(ka_ref.shape[0]):
        k = k_ref[:, kb * LANES:(kb + 1) * LANES]
        ka_ref[kb, :, 0:LANES] = k
        ka_ref[kb, :, LANES:2 * LANES] = kf_ref[...]
        k2 = jnp.square(k.astype(F32))
        lane = lax.broadcasted_iota(jnp.int32, k2.shape, 1)
        for hf in range(2):
            n2 = jnp.sum(jnp.where(lane // HEAD_DIM == hf, k2, 0.0), axis=1, keepdims=True)
            kn_ref[2 * kb + hf:2 * kb + hf + 1, :] = jnp.broadcast_to(
                jnp.sqrt(jnp.max(n2, axis=0, keepdims=True)), (1, LANES))
    ones_row = jnp.where(lax.broadcasted_iota(jnp.int32, (VROWS - HEAD_DIM, tk), 0) == 0, 1.0, 0.0)
    for v in range(nh if pair else 1):
        for jt in range(S // tk):
            vta_ref[v, jt, 0:HEAD_DIM, :] = vt_ref[v * HEAD_DIM:(v + 1) * HEAD_DIM, jt * tk:(jt + 1) * tk]
            vta_ref[v, jt, HEAD_DIM:VROWS, :] = ones_row.astype(BF16)

    tile = functools.partial(_flash_q_tile, q_ref=q_ref, bias_ref=bias_ref, mask_ref=mask_ref, o_ref=o_ref,
                             ka_ref=ka_ref, vta_ref=vta_ref, kn_ref=kn_ref, qa_ref=qa_ref, bnd_ref=bnd_ref,
                             m_ref=m_ref, acc_ref=acc_ref, alpha_ref=alpha_ref, p_ref=p_ref,
                             tq=tq, tk=tk, nh=nh, pair=pair, window=window)
    nq = S // tq
    low_ref[0] = tile(0, False)

    def bound_stabilised(qi, carry):
        low_ref[qi] = tile(qi, False, has_past=True)
        return carry
    lax.fori_loop(1, nq, bound_stabilised, 0)

    def redo(qi, carry):
        @pl.when(low_ref[qi] < MIN_WEIGHT_SUM)
        def _running_max():
            tile(qi, True)
        return carry

    lax.fori_loop(0, nq, redo, 0)


def _mask_table(tk, window):
    n_delta = (tk + (window or 0)) // CG
    d = np.arange(n_delta)[:, None, None] * CG + np.arange(CG)[None, None, :] - np.arange(tk)[None, :, None]
    ok = (d >= 0) & ((d < window) if window else True)
    return jnp.asarray(np.where(ok, 0.0, NEG), F32)


def _flash(q_t, bias_t, k, kfeat, v_t, *, B, S, units, tq, nh, pair, window):
    tk = TK
    R = nh * tq
    nq = S // tq
    nv = nh if pair else 1
    nb = nh if pair else 1
    nkb = nh // 2 if pair else 1
    assert tq == tk and S % tk == 0 and tk % CG == 0 and window in (None, tk)
    mask = _mask_table(tk, window)
    kcol = (lambda u: u) if pair else (lambda u: u // 2)
    out_w = tq if pair else R
    out_shape = (B, units * nb, HEAD_DIM, S if pair else S * nh)
    has_bias = bias_t is not None
    in_specs = [pl.BlockSpec((nh * HEAD_DIM, S), lambda b, u: (u, b))]
    if has_bias:
        in_specs.append(pl.BlockSpec((None, nb, FEAT, S), lambda b, u: (b, u, 0, 0)))
    in_specs += [pl.BlockSpec((S, nkb * LANES), lambda b, u: (b, kcol(u))),
                 pl.BlockSpec((S, LANES), lambda b, u: (0, 0)),
                 pl.BlockSpec((nv * HEAD_DIM, S), lambda b, u: (u, b)),
                 pl.BlockSpec(mask.shape, lambda b, u: (0, 0, 0))]
    args = (q_t,) + ((bias_t,) if has_bias else ()) + (k, kfeat, v_t, mask)
    return pl.pallas_call(
        functools.partial(_flash_kernel, tq=tq, tk=tk, nh=nh, pair=pair, window=window, has_bias=has_bias),
        grid=(B, units),
        in_specs=in_specs,
        out_specs=pl.BlockSpec((None, nb, HEAD_DIM, out_w * nq), lambda b, u: (b, u, 0, 0)),
        out_shape=jax.ShapeDtypeStruct(out_shape, BF16),
        scratch_shapes=[pltpu.VMEM((nkb, S, 2 * LANES), BF16),
                        pltpu.VMEM((nv, S // tk, VROWS, tk), BF16),
                        pltpu.VMEM((SUBLANES, LANES), F32),
                        pltpu.VMEM((2 * LANES, R), BF16),
                        pltpu.VMEM((SUBLANES, R), F32),
                        pltpu.VMEM((SUBLANES, R), F32),
                        pltpu.VMEM((VROWS, R), F32),
                        pltpu.VMEM((SUBLANES, R), F32),
                        pltpu.VMEM((tk, R), BF16),
                        pltpu.SMEM((nq,), F32)],
        compiler_params=pltpu.CompilerParams(dimension_semantics=("parallel", "parallel"),
                                             vmem_limit_bytes=VMEM_LIMIT),
        name="flash",
    )(*args)


def _even_out_kernel(x_ref, ocmp_ref, osel_ref, owin_ref, om_ref, gate_ref, nz_ref, mz_ref,
                     w_ref, g_ref, b_ref, o_ref, sig_ref, *, alpha, tm):
    G, nh = NSA_KV_GROUPS, NSA_HG
    sig_ref[...] = _sigmoid(gate_ref[...])
    heads = []
    for g in range(G):
        for hh in range(nh):
            acc = None
            for br, ref in enumerate((ocmp_ref, osel_ref, owin_ref)):
                gate = sig_ref[pl.ds(3 * (g * nh + hh) + br, 1), :]
                term = gate * ref[g, :, hh * tm:(hh + 1) * tm].astype(F32)
                acc = term if acc is None else acc + term
            heads.append(acc)
    o_nsa = jnp.concatenate(heads, axis=0)
    nz = nz_ref[...].astype(F32)
    mz = mz_ref[...].astype(F32)
    o_m = jnp.concatenate([om_ref[h] for h in range(MOBA_HEADS)], axis=0).astype(F32)
    a_t = jnp.concatenate([o_nsa * (nz * _sigmoid(nz)), o_m * (mz * _sigmoid(mz))], axis=0)
    y = _dot_tn(a_t.astype(BF16), w_ref[...])
    o_ref[...] = _layer_norm(alpha * x_ref[...] + y, g_ref[...], b_ref[...])


def _even_out(x2d, ocmp, osel, owin, om, gate_t, nz_t, mz_t, w_out, ln_g, ln_b, alpha, *, B, S):
    T, D = x2d.shape
    tm = TQ
    nq = S // tm
    G, nh, H = NSA_KV_GROUPS, NSA_HG, MOBA_HEADS
    nsa = pl.BlockSpec((None, G, HEAD_DIM, nh * tm), lambda i: (i // nq, 0, 0, i % nq))
    feat = lambda n: pl.BlockSpec((n, tm), lambda i: (0, i))
    full = lambda a: pl.BlockSpec(a.shape, lambda i: (0,) * a.ndim)
    return pl.pallas_call(
        functools.partial(_even_out_kernel, alpha=alpha, tm=tm),
        grid=(T // tm,),
        in_specs=[pl.BlockSpec((tm, D), lambda i: (i, 0)), nsa, nsa, nsa,
                  pl.BlockSpec((None, H, HEAD_DIM, tm), lambda i: (i // nq, 0, 0, i % nq)),
                  feat(GATE_ROWS), feat(NSA_W), feat(MOBA_W),
                  full(w_out), full(ln_g), full(ln_b)],
        out_specs=pl.BlockSpec((tm, D), lambda i: (i, 0)),
        out_shape=jax.ShapeDtypeStruct((T, D), F32),
        scratch_shapes=[pltpu.VMEM((GATE_ROWS, tm), F32)],
        compiler_params=pltpu.CompilerParams(dimension_semantics=("parallel",),
                                             vmem_limit_bytes=VMEM_LIMIT),
        name="even_out",
    )(x2d, ocmp, osel, owin, om, gate_t, nz_t, mz_t, w_out, ln_g, ln_b)


def _gla_head_exact(row0, h, q_ref, v_ref, o_ref, st_ref, b_ref, kf_ref, scale):
    C, dk, dv, SUB = GLA_CHUNK, GLA_DK, GLA_DV, GLA_SUB
    rows = lax.broadcasted_iota(jnp.int32, (C, dk), 0)
    sub_i = lax.broadcasted_iota(jnp.int32, (SUB, LANES), 0)
    lane = lax.broadcasted_iota(jnp.int32, (SUB, LANES), 1)
    ones = jnp.ones((dk, LANES), BF16)
    sl = pl.ds(h * dk, dk)
    rs_c = pl.ds(row0, C)
    bh = b_ref[:, sl]
    kf = kf_ref[:, sl]
    qf = q_ref[rs_c, sl].astype(F32) * scale
    vh = v_ref[rs_c, pl.ds(h * dv, dv)]
    st = st_ref[h]
    inter = _dot_nt((qf * jnp.exp(bh)).astype(BF16), st.astype(BF16))
    bl = b_ref[pl.ds(C - 1, 1), sl]
    kd = kf * jnp.exp(bl - bh)

    def off_block(r0, r1, c0, c1):
        ref_row = b_ref[pl.ds(r0, 1), sl]
        qt = qf[r0:r1] * jnp.exp(bh[r0:r1] - ref_row)
        ek = jnp.where((rows >= c0) & (rows < c1), ref_row - bh, NEG)
        kt = kf * jnp.exp(ek)
        return _dot_nt(qt.astype(BF16), kt.astype(BF16))

    lower = off_block(2 * SUB, C, 0, 2 * SUB)
    blocks = [None, off_block(SUB, 2 * SUB, 0, SUB), lower[0:SUB],
              lower[SUB:2 * SUB] + off_block(3 * SUB, C, 2 * SUB, 3 * SUB)]
    pieces = []
    for blk in range(C // SUB):
        r0 = blk * SUB
        q_i = qf[r0:r0 + SUB]
        b_i = bh[r0:r0 + SUB]
        for jj in range(SUB):
            b_j = b_ref[pl.ds(r0 + jj, 1), sl]
            k_j = kf_ref[pl.ds(r0 + jj, 1), sl]
            pieces.append((q_i * k_j * jnp.exp(jnp.minimum(b_i - b_j, 0.0))).astype(BF16))
    sums = _dot(jnp.concatenate(pieces, axis=0), ones)
    for blk in range(C // SUB):
        r0 = blk * SUB
        diag = jnp.zeros((SUB, LANES), F32)
        for jj in range(SUB):
            rs = sums[(r0 + jj) * SUB:(r0 + jj + 1) * SUB]
            diag = jnp.where((lane == r0 + jj) & (sub_i >= jj), rs, diag)
        diag = diag[:, 0:C]
        blocks[blk] = diag if blocks[blk] is None else blocks[blk] + diag
    att = jnp.concatenate(blocks, axis=0)
    o_ref[rs_c, pl.ds(h * dv, dv)] = (inter + _dot(att.astype(BF16), vh)).astype(o_ref.dtype)
    st_ref[h] = st * jnp.exp(bl) + _dot_tn(vh, kd.astype(BF16))


def _gla_head_bounded(row0, h, q_ref, v_ref, o_ref, st_ref, b_ref, kf_ref, scale):
    C, dk, dv = GLA_CHUNK, GLA_DK, GLA_DV
    sl = pl.ds(h * dk, dk)
    rs_c = pl.ds(row0, C)
    bh = b_ref[rs_c, sl]
    kf = kf_ref[rs_c, sl]
    qf = q_ref[rs_c, sl].astype(F32) * scale
    vh = v_ref[rs_c, pl.ds(h * dv, dv)]
    st = st_ref[h]
    qs = (qf * jnp.exp(bh)).astype(BF16)
    kt = (kf * jnp.exp(-bh)).astype(BF16)
    i = lax.broadcasted_iota(jnp.int32, (C, C), 0)
    j = lax.broadcasted_iota(jnp.int32, (C, C), 1)
    att = jnp.where(i >= j, _dot_nt(qs, kt), 0.0)
    inter = _dot_nt(qs, st.astype(BF16))
    o_ref[rs_c, pl.ds(h * dv, dv)] = (inter + _dot(att.astype(BF16), vh)).astype(o_ref.dtype)
    bl = b_ref[pl.ds(row0 + C - 1, 1), sl]
    kd = kf * jnp.exp(bl - bh)
    st_ref[h] = st * jnp.exp(bl) + _dot_tn(vh, kd.astype(BF16))


def _gla_kernel(q_ref, k_ref, v_ref, glr_ref, w2_ref, gb_ref, tril_ref, o_ref,
                st_ref, b_ref, kf_ref, bc_ref, kc_ref, *, scale):
    C, H = GLA_CHUNK, GLA_HEADS
    nch = q_ref.shape[0] // C

    @pl.when(pl.program_id(1) == 0)
    def _init():
        st_ref[...] = jnp.zeros(st_ref.shape, F32)

    whi, wlo = _hilo(w2_ref[...])
    part = tril_ref.shape[0]
    decay = None
    for r0 in range(0, nch * C, part):
        rs = pl.ds(r0, part)
        ghi, glo = _hilo(glr_ref[rs, :])
        pre = _dot(ghi, whi) + _dot(glo, whi) + _dot(ghi, wlo) + gb_ref[...]
        la = (jnp.minimum(pre, 0.0) - jnp.log(1.0 + jnp.exp(-jnp.abs(pre)))) * (1.0 / GLA_TAU)
        lhi, llo = _hilo(la)
        b_part = _dot(tril_ref[...], lhi) + _dot(tril_ref[...], llo)
        b_ref[rs, :] = b_part
        for c in range(part // C):
            d_c = -b_part[c * C + C - 1:(c + 1) * C, :]
            decay = d_c if decay is None else jnp.maximum(decay, d_c)
    kf_ref[...] = k_ref[...].astype(F32)
    bounded = jnp.max(decay) <= GLA_FACTOR_MAX_DECAY

    @pl.when(bounded)
    def _factored():
        for c in range(nch):
            for h in range(H):
                _gla_head_bounded(c * C, h, q_ref, v_ref, o_ref, st_ref, b_ref, kf_ref, scale)

    @pl.when(jnp.logical_not(bounded))
    def _exact():
        def chunk(c, carry):
            row0 = pl.multiple_of(c * C, C)
            bc_ref[...] = b_ref[pl.ds(row0, C), :]
            kc_ref[...] = kf_ref[pl.ds(row0, C), :]
            for h in range(H):
                _gla_head_exact(row0, h, q_ref, v_ref, o_ref, st_ref, bc_ref, kc_ref, scale)
            return carry
        lax.fori_loop(0, nch, chunk, 0)


def _gla(q, k, v, glr, w2p, gb, tril, *, B, S):
    C, H, dk, dv = GLA_CHUNK, GLA_HEADS, GLA_DK, GLA_DV
    rows = C * min(GLA_STEP_CHUNKS, S // C)
    nchunk = S // rows
    tok = lambda n: pl.BlockSpec((rows, n), lambda b, c: (b * nchunk + c, 0))
    const = lambda shape: pl.BlockSpec(shape, lambda b, c: (0,) * len(shape))
    return pl.pallas_call(
        functools.partial(_gla_kernel, scale=float(dk) ** -0.5),
        grid=(B, nchunk),
        in_specs=[tok(H * dk), tok(H * dk), tok(H * dv), tok(LANES),
                  const(w2p.shape), const(gb.shape), const(tril.shape)],
        out_specs=tok(H * dv),
        out_shape=jax.ShapeDtypeStruct((B * S, H * dv), BF16),
        scratch_shapes=[pltpu.VMEM((H, dv, dk), F32),
                        pltpu.VMEM((rows, H * dk), F32),
                        pltpu.VMEM((rows, H * dk), F32),
                        pltpu.VMEM((C, H * dk), F32),
                        pltpu.VMEM((C, H * dk), F32)],
        compiler_params=pltpu.CompilerParams(dimension_semantics=("parallel", "arbitrary"),
                                             vmem_limit_bytes=VMEM_LIMIT),
        name="gla",
    )(q, k, v, glr, w2p, gb, tril)


def _odd_out_kernel(x_ref, o_ref_in, z_ref, gn_ref, w_ref, g_ref, b_ref, out_ref, *, alpha):
    H, dv = GLA_HEADS, GLA_DV
    parts = []
    for h in range(H):
        sl = pl.ds(h * dv, dv)
        oh = o_ref_in[:, sl].astype(F32)
        r = lax.rsqrt(jnp.mean(oh * oh, axis=-1, keepdims=True) + LN_EPS)
        z = z_ref[:, sl].astype(F32)
        parts.append((oh * r * gn_ref[:, sl] * (z * _sigmoid(z))).astype(BF16))
    y = _dot(jnp.concatenate(parts, axis=1), w_ref[...])
    out_ref[...] = _layer_norm(alpha * x_ref[...] + y, g_ref[...], b_ref[...])


def _odd_out(x2d, o, z, gn_g, w_out, ln_g, ln_b, alpha, tm=1024):
    T, D = x2d.shape
    W = o.shape[1]
    row = lambda n: pl.BlockSpec((tm, n), lambda i: (i, 0))
    return pl.pallas_call(
        functools.partial(_odd_out_kernel, alpha=alpha),
        grid=(T // tm,),
        in_specs=[row(D), row(W), row(W),
                  pl.BlockSpec((1, W), lambda i: (0, 0)),
                  pl.BlockSpec((W, D), lambda i: (0, 0)),
                  pl.BlockSpec((1, D), lambda i: (0, 0)),
                  pl.BlockSpec((1, D), lambda i: (0, 0))],
        out_specs=row(D),
        out_shape=jax.ShapeDtypeStruct((T, D), F32),
        compiler_params=pltpu.CompilerParams(dimension_semantics=("parallel",),
                                             vmem_limit_bytes=VMEM_LIMIT),
        name="odd_out",
    )(x2d, o, z, gn_g, w_out, ln_g, ln_b)


def _odd_weights_kernel(w_ref, o_ref, *, n_qkv, n_gate, n_z):
    o_ref[:, 0:n_qkv] = w_ref[:, 0:n_qkv].astype(BF16)
    o_ref[:, n_qkv:n_qkv + n_z] = w_ref[:, n_qkv + n_gate:n_qkv + n_gate + n_z].astype(BF16)
    g = w_ref[:, n_qkv:n_qkv + LANES]
    lane = lax.broadcasted_iota(jnp.int32, g.shape, 1)
    o_ref[:, n_qkv + n_z:] = jnp.where(lane < n_gate, g, 0.0).astype(BF16)


def _odd_weights(w_in, tr=128):
    D, W = w_in.shape
    n_qkv, n_gate, n_z = sum(ODD_SPLITS[:3]), ODD_SPLITS[3], ODD_SPLITS[4]
    assert n_qkv % LANES == 0 and n_z % LANES == 0 and n_gate <= LANES <= n_gate + n_z and D % tr == 0
    return pl.pallas_call(
        functools.partial(_odd_weights_kernel, n_qkv=n_qkv, n_gate=n_gate, n_z=n_z),
        grid=(D // tr,),
        in_specs=[pl.BlockSpec((tr, W), lambda i: (i, 0))],
        out_specs=pl.BlockSpec((tr, n_qkv + n_z + LANES), lambda i: (i, 0)),
        out_shape=jax.ShapeDtypeStruct((D, n_qkv + n_z + LANES), BF16),
        compiler_params=pltpu.CompilerParams(dimension_semantics=("parallel",),
                                             vmem_limit_bytes=VMEM_LIMIT),
        name="odd_weights",
    )(w_in)


def _key_features(S, block):
    s = np.arange(S)
    f = np.zeros((S, LANES), np.float32)
    f[s, s // block] = 1.0
    f[:, FEAT + 0] = 1.0
    f[:, FEAT + 1] = 1.0
    f[:, FEAT + 2] = FEAT * (s // FEAT)
    f[:, FEAT + 3] = s % FEAT
    return jnp.asarray(f, BF16)


def _pad_to(a, n, axis):
    pad = [(0, 0)] * a.ndim
    pad[axis] = (0, n - a.shape[axis])
    return jnp.pad(a, pad)


def _even_layer(x2d, B, S, w_in, cmp_pos, cmp_w1, cmp_b1, cmp_w2, w_out, ln_g, ln_b, alpha):
    T, D = x2d.shape
    G, hd = NSA_KV_GROUPS, HEAD_DIM
    scale = hd ** -0.5
    wq, wkv, wgate, wnz, wmq, wmk, wmv, wmz = jnp.split(w_in, np.cumsum(EVEN_SPLITS)[:-1].tolist(), axis=1)
    wkc, wvc, wks, wvs, wkw, wvw = jnp.split(wkv, 6, axis=1)
    w_tok = jnp.concatenate([wkc, wvc, wks, wkw, wmk], axis=1).astype(BF16)
    w_tr = jnp.concatenate([wq * scale, wvs, wvw, wmq * scale, wmv, wnz, wmz,
                            _pad_to(wgate, GATE_ROWS, 1)], axis=1).T.astype(BF16)
    (kcmp, vcmp, ksel, kwin, mk, nq_t, vsel_t, vwin_t, mq_t, mv_t, nz_t, mz_t, gate_t) = _proj(
        x2d, w_tok, (NSA_KV_W,) * 4 + (MOBA_W,), (F32, F32, BF16, BF16, BF16),
        w_tr, (NSA_W, NSA_KV_W, NSA_KV_W, MOBA_W, MOBA_W, NSA_W, MOBA_W, GATE_ROWS),
        (BF16,) * 7 + (F32,))

    L, d, HID = NSA_CMP_BLOCK, NSA_CMP_STRIDE, NSA_CMP_HIDDEN
    ncmp = (S - L) // d + 1
    NC = S // d
    def group_diagonal(w, axis):
        z = jnp.zeros_like(w)
        return jnp.stack([jnp.concatenate([w if h == g else z for h in range(G)], axis=-1) for g in range(G)], axis=axis)

    w1r = cmp_w1.reshape(2, 2, d, hd, HID).astype(BF16)
    w1p = group_diagonal(w1r, 3).reshape(2, 2, d * G * hd, G * HID)
    posp = jnp.broadcast_to(cmp_pos.reshape(2, 2, d, 1, hd), (2, 2, d, G, hd)).reshape(2, 2, 1, d * G * hd)
    b1p = jnp.tile(cmp_b1, (1, G)).reshape(2, 1, G * HID)
    w2p = group_diagonal(_pad_to(cmp_w2, LANES, 2).astype(BF16), 1).reshape(2, G * HID, G * LANES)
    cfeat = np.zeros((NC, G, LANES), np.float32)
    cfeat[:, :, hd + 0] = 1.0
    cfeat[:, :, hd + 1] = 1.0
    cfeat[:, :, hd + 2] = (np.arange(NC) * d)[:, None]
    cfeat[:, :, hd + 3] = L - 1
    kvc = _cmp_mlp(kcmp, vcmp, w1p, posp, b1p, w2p, jnp.asarray(cfeat.reshape(NC, G * LANES)), ncmp,
                   B=B, S=S)

    ns = S // NSA_SEL_BLOCK
    cmp_start = np.arange(NC) * d
    sel_start = np.arange(FEAT) * NSA_SEL_BLOCK
    ovt = ((cmp_start[None, :] <= sel_start[:, None] + NSA_SEL_BLOCK - 1)
           & (cmp_start[None, :] + L - 1 >= sel_start[:, None])
           & (np.arange(NC)[None, :] < ncmp) & (np.arange(FEAT)[:, None] < ns))
    o_cmp, bias_sel = _cmp_attn(nq_t, kvc, jnp.asarray(ovt, BF16), B=B, S=S)

    feats_sel = _key_features(S, NSA_SEL_BLOCK)
    o_sel = _flash(nq_t, bias_sel, ksel, feats_sel, vsel_t,
                   B=B, S=S, units=G, tq=TQ, nh=NSA_HG, pair=False, window=None)
    o_win = _flash(nq_t, None, kwin, feats_sel, vwin_t,
                   B=B, S=S, units=G, tq=TQ, nh=NSA_HG, pair=False, window=NSA_WINDOW)

    avg = np.zeros((MOBA_NBP, S), np.float32)
    avg[np.arange(S) // MOBA_BLOCK, np.arange(S)] = 1.0 / MOBA_BLOCK
    bias_m = _moba_gate(mq_t, mk, jnp.asarray(avg, BF16), B=B, S=S)
    o_m = _flash(mq_t, bias_m, mk, _key_features(S, MOBA_BLOCK), mv_t,
                 B=B, S=S, units=MOBA_HEADS // MOBA_STEP_HEADS, tq=TQM, nh=MOBA_STEP_HEADS, pair=True, window=None)

    return _even_out(x2d, o_cmp, o_sel, o_win, o_m, gate_t, nz_t, mz_t, w_out.astype(BF16),
                     ln_g.reshape(1, D), ln_b.reshape(1, D), alpha, B=B, S=S)


def _odd_layer(x2d, B, S, w_in, gate_w2, gate_b, gn_g, w_out, ln_g, ln_b, alpha):
    T, D = x2d.shape
    H, dk, dv, C = GLA_HEADS, GLA_DK, GLA_DV, GLA_CHUNK
    q, k, v, z, glr = _proj(x2d, _odd_weights(w_in), (H * dk, H * dk, H * dv, H * dv, LANES),
                            (BF16, BF16, BF16, BF16, F32))
    w2p = _pad_to(gate_w2, LANES, 0)
    ncs = min(GLA_CUMSUM_CHUNKS, GLA_STEP_CHUNKS, S // C)
    tril = jnp.asarray(np.kron(np.eye(ncs), np.tril(np.ones((C, C)))), BF16)
    o = _gla(q, k, v, glr, w2p, gate_b.reshape(1, H * dk), tril, B=B, S=S)
    return _odd_out(x2d, o, z, gn_g.reshape(1, H * dv), w_out.astype(BF16),
                    ln_g.reshape(1, D), ln_b.reshape(1, D), alpha)


def kernel(x, ev_w_in, ev_cmp_pos, ev_cmp_w1, ev_cmp_b1, ev_cmp_w2, ev_w_out, ev_ln_g, ev_ln_b,
           od_w_in, od_gate_w2, od_gate_b, od_gn_g, od_w_out, od_ln_g, od_ln_b):
    B, S, D = x.shape
    depth = ev_w_in.shape[0] + od_w_in.shape[0]
    alpha = (2.0 * depth) ** 0.25
    h = x.reshape(B * S, D)
    for layer in range(depth):
        i = layer // 2
        if layer % 2 == 0:
            h = _even_layer(h, B, S, ev_w_in[i], ev_cmp_pos[i], ev_cmp_w1[i], ev_cmp_b1[i], ev_cmp_w2[i],
                            ev_w_out[i], ev_ln_g[i], ev_ln_b[i], alpha)
        else:
            h = _odd_layer(h, B, S, od_w_in[i], od_gate_w2[i], od_gate_b[i], od_gn_g[i],
                           od_w_out[i], od_ln_g[i], od_ln_b[i], alpha)
    return h.reshape(B, S, D)
```

```python
import functools

import numpy as np
import jax
import jax.numpy as jnp
from jax import lax
from jax.experimental import pallas as pl
from jax.experimental.pallas import tpu as pltpu

F32 = jnp.float32
BF16 = jnp.bfloat16

HEAD_DIM = 64
NSA_HEADS = 8
NSA_KV_GROUPS = 2
NSA_HG = NSA_HEADS // NSA_KV_GROUPS
NSA_CMP_BLOCK = 32
NSA_CMP_STRIDE = 16
NSA_CMP_HIDDEN = 128
NSA_SEL_BLOCK = 64
NSA_TOP_N = 16
NSA_WINDOW = 512
MOBA_HEADS = 8
MOBA_BLOCK = 256
MOBA_TOP_K = 3
GLA_HEADS = 4
GLA_DK = 128
GLA_DV = 256
GLA_GATE_RANK = 16
GLA_TAU = 16.0
GLA_CHUNK = 64
GLA_SUB = 16
GLA_STEP_CHUNKS = 16
GLA_CUMSUM_CHUNKS = 4
GLA_FACTOR_MAX_DECAY = 60.0
LN_EPS = 1e-5
NEG = -1e30
FORCE_BONUS = 1e4

NSA_W = NSA_HEADS * HEAD_DIM
NSA_KV_W = NSA_KV_GROUPS * HEAD_DIM
MOBA_W = MOBA_HEADS * HEAD_DIM
EVEN_SPLITS = (NSA_W, 6 * NSA_KV_W, 3 * NSA_HEADS, NSA_W, MOBA_W, MOBA_W, MOBA_W, MOBA_W)
ODD_SPLITS = (GLA_HEADS * GLA_DK, GLA_HEADS * GLA_DK, GLA_HEADS * GLA_DV, GLA_GATE_RANK, GLA_HEADS * GLA_DV)

LANES = 128
SUBLANES = 8
FEAT = 64
ALIBI_ROWS = 16
GATE_ROWS = 32
MOBA_NBP = 16
TQ = 512
TQM = 512
MOBA_STEP_HEADS = 4
TK = 512
CG = 256
MIN_WEIGHT_SUM = 1e-25
VROWS = HEAD_DIM + 16
VMEM_LIMIT = 48 * 1024 * 1024


def _dot(a, b):
    return jnp.dot(a, b, preferred_element_type=F32)


def _dot_nt(a, b):
    return lax.dot_general(a, b, (((1,), (1,)), ((), ())), preferred_element_type=F32)


def _dot_tn(a, b):
    return lax.dot_general(a, b, (((0,), (0,)), ((), ())), preferred_element_type=F32)


def _hilo(a):
    hi = a.astype(BF16)
    lo = (a - hi.astype(F32)).astype(BF16)
    return hi, lo


def _pow2_neg(n):
    return lax.bitcast_convert_type(lax.shift_left(127 - n, 23), F32)


def _sigmoid(x):
    return 0.5 * jnp.tanh(0.5 * x) + 0.5


def _layer_norm(r, g, b):
    mu = jnp.mean(r, axis=-1, keepdims=True)
    d = r - mu
    var = jnp.mean(d * d, axis=-1, keepdims=True)
    return d * lax.rsqrt(var + LN_EPS) * g + b


def _alibi_query_rows(t0, tq, head0, R):
    r = lax.broadcasted_iota(jnp.int32, (ALIBI_ROWS, R), 0)
    col = lax.broadcasted_iota(jnp.int32, (ALIBI_ROWS, R), 1)
    t = t0 + col % tq
    slope = _pow2_neg(head0 + col // tq + 1)
    feats = jnp.where(r == 0, -FEAT * slope * (t // FEAT).astype(F32),
                      jnp.where(r == 1, -slope * (t % FEAT).astype(F32), jnp.where(r < 4, slope, 0.0)))
    return jnp.concatenate([feats.astype(BF16), jnp.zeros((FEAT - ALIBI_ROWS, R), BF16)], axis=0)


def _ranks(sc_ref, n_real, n_rows=None, cols=slice(None)):
    n_rows = sc_ref.shape[0] if n_rows is None else n_rows
    out = []
    for v in range(n_rows // SUBLANES):
        lo = v * SUBLANES
        sc = sc_ref[lo:lo + SUBLANES, cols]
        width = sc.shape[1]
        idx = lo + lax.broadcasted_iota(jnp.int32, (SUBLANES, width), 0)
        rank = jnp.zeros((SUBLANES, width), jnp.int32)
        for jp in range(n_real):
            r = sc_ref[pl.ds(jp, 1), cols]
            if jp < lo:
                beats = r >= sc
            elif jp >= lo + SUBLANES - 1:
                beats = r > sc
            else:
                beats = (r > sc) | ((r == sc) & (idx > jp))
            rank = rank + jnp.where(beats, 1, 0)
        out.append(rank)
    return jnp.concatenate(out, axis=0)


def _proj_kernel(*refs, tok_splits, tr_splits, tok_transposed):
    x_ref, wt_ref = refs[0], refs[1]
    outs = list(refs[3:] if tr_splits else refs[2:])
    x = x_ref[...]
    xb = x.astype(BF16)
    off = 0
    for n in tok_splits:
        o_ref = outs.pop(0)
        y = _dot_nt(xb, wt_ref[off:off + n, :]) if tok_transposed else _dot(xb, wt_ref[:, off:off + n])
        o_ref[...] = y.astype(o_ref.dtype)
        off += n
    if tr_splits:
        wtr_ref = refs[2]
        xt = x.T.astype(BF16)
        off = 0
        for n in tr_splits:
            o_ref = outs.pop(0)
            o_ref[...] = _dot(wtr_ref[off:off + n, :], xt).astype(o_ref.dtype)
            off += n


def _proj(x2d, w_tok, tok_splits, tok_dtypes, w_tr=None, tr_splits=(), tr_dtypes=(), tm=512, tok_transposed=False):
    T, D = x2d.shape
    assert T % tm == 0 and sum(tok_splits) == w_tok.shape[0 if tok_transposed else 1]
    in_specs = [pl.BlockSpec((tm, D), lambda i: (i, 0)),
                pl.BlockSpec(w_tok.shape, lambda i: (0, 0))]
    args = [x2d, w_tok]
    if tr_splits:
        assert sum(tr_splits) == w_tr.shape[0]
        in_specs.append(pl.BlockSpec(w_tr.shape, lambda i: (0, 0)))
        args.append(w_tr)
    out_specs = ([pl.BlockSpec((tm, n), lambda i: (i, 0)) for n in tok_splits]
                 + [pl.BlockSpec((n, tm), lambda i: (0, i)) for n in tr_splits])
    out_shape = ([jax.ShapeDtypeStruct((T, n), dt) for n, dt in zip(tok_splits, tok_dtypes)]
                 + [jax.ShapeDtypeStruct((n, T), dt) for n, dt in zip(tr_splits, tr_dtypes)])
    return pl.pallas_call(
        functools.partial(_proj_kernel, tok_splits=tuple(tok_splits), tr_splits=tuple(tr_splits),
                          tok_transposed=tok_transposed),
        grid=(T // tm,), in_specs=in_specs, out_specs=out_specs, out_shape=out_shape,
        compiler_params=pltpu.CompilerParams(dimension_semantics=("parallel",),
                                             vmem_limit_bytes=VMEM_LIMIT),
        name="proj",
    )(*args)


def _cmp_mlp_kernel(kx_ref, vx_ref, w1_ref, pos_ref, b1_ref, w2_ref, cf_ref, o_ref, *, ncmp):
    d = NSA_CMP_STRIDE
    nc = kx_ref.shape[0] // d
    cw = w1_ref.shape[2]
    for i, x_ref in enumerate((kx_ref, vx_ref)):
        a = bm = None
        for l in range(0, d, 2):
            x2 = jnp.concatenate([x_ref[pl.ds(l, nc, stride=d), :], x_ref[pl.ds(l + 1, nc, stride=d), :]],
                                 axis=1).astype(BF16)
            rows = pl.ds(l * x_ref.shape[1], 2 * x_ref.shape[1])
            a_l, b_l = _dot(x2, w1_ref[i, 0, rows, :]), _dot(x2, w1_ref[i, 1, rows, :])
            a, bm = (a_l, b_l) if a is None else (a + a_l, bm + b_l)
        c = b1_ref[i]
        for half in range(2):
            phi, plo = _hilo(jnp.broadcast_to(pos_ref[i, half], (SUBLANES, cw)))
            c = c + (_dot(phi, w1_ref[i, half]) + _dot(plo, w1_ref[i, half]))[0:1]
        hid = a + pltpu.roll(bm, nc - 1, axis=0) + c
        hid = hid * _sigmoid(hid)
        row = lax.broadcasted_iota(jnp.int32, hid.shape, 0)
        hid = jnp.where(row < ncmp, hid, 0.0)
        out = _dot(hid.astype(BF16), w2_ref[i])
        if i == 0:
            out = out + cf_ref[...]
        o_ref[i] = out.astype(o_ref.dtype)


def _cmp_mlp(kx, vx, w1p, posp, b1p, w2p, cfeat, ncmp, *, B, S):
    NC = S // NSA_CMP_STRIDE
    OW = w2p.shape[-1]
    full = lambda a: pl.BlockSpec(a.shape, lambda b: (0,) * a.ndim)
    return pl.pallas_call(
        functools.partial(_cmp_mlp_kernel, ncmp=ncmp),
        grid=(B,),
        in_specs=[pl.BlockSpec((S, kx.shape[1]), lambda b: (b, 0)),
                  pl.BlockSpec((S, vx.shape[1]), lambda b: (b, 0)),
                  full(w1p), full(posp), full(b1p), full(w2p), full(cfeat)],
        out_specs=pl.BlockSpec((2, None, NC, OW), lambda b: (0, b, 0, 0)),
        out_shape=jax.ShapeDtypeStruct((2, B, NC, OW), BF16),
        compiler_params=pltpu.CompilerParams(dimension_semantics=("parallel",),
                                             vmem_limit_bytes=VMEM_LIMIT),
        name="cmp_mlp",
    )(kx, vx, w1p, posp, b1p, w2p, cfeat)


def _cmp_attn_kernel(q_ref, kc_ref, vc_ref, ovt_ref, o_ref, bias_ref, sc_ref, *, tq, nh, ns, ntop):
    g = pl.program_id(1)
    R = nh * tq
    nq = q_ref.shape[1] // tq

    def q_tile(qi):
        toks = pl.ds(qi * tq, tq)
        nc = min(kc_ref.shape[0], -(-((qi + 1) * tq) // (NSA_CMP_STRIDE * 16)) * 16)
        nsel = min(ns, -(-((qi + 1) * tq) // (NSA_SEL_BLOCK * SUBLANES)) * SUBLANES)
        qcat = jnp.concatenate([q_ref[hh * HEAD_DIM:(hh + 1) * HEAD_DIM, toks] for hh in range(nh)], axis=1)
        qa = jnp.concatenate([qcat, _alibi_query_rows(qi * tq, tq, g * nh, R)], axis=0)
        s = _dot(kc_ref[0:nc, :], qa)
        n = lax.broadcasted_iota(jnp.int32, (nc, R), 0)
        t_row = qi * tq + lax.broadcasted_iota(jnp.int32, (1, R), 1) % tq
        last = (t_row - (NSA_CMP_BLOCK - 1)) // NSA_CMP_STRIDE
        s = jnp.where(n <= last, s, NEG)
        e = jnp.exp(s - jnp.max(s, axis=0, keepdims=True))
        inv = jnp.where(last >= 0, 1.0 / jnp.sum(e, axis=0, keepdims=True), 0.0)
        p = e * inv
        o_ref[:, pl.ds(qi * R, R)] = _dot_tn(vc_ref[0:nc, 0:HEAD_DIM], p.astype(BF16)).astype(o_ref.dtype)

        psum = p[:, 0:tq]
        for hh in range(1, nh):
            psum = psum + p[:, hh * tq:(hh + 1) * tq]
        phi, plo = _hilo(psum)
        ovt = ovt_ref[0:nsel, 0:nc]
        imp = _dot(ovt, phi) + _dot(ovt, plo)

        j = lax.broadcasted_iota(jnp.int32, (nsel, tq), 0)
        cur = (qi * tq + lax.broadcasted_iota(jnp.int32, (nsel, tq), 1)) // NSA_SEL_BLOCK
        if nsel <= ntop:
            sel = j <= cur
        else:
            forced = jnp.where((j == 0) | (j == cur) | (j == cur - 1), FORCE_BONUS, 0.0)
            sc_ref[0:nsel, :] = jnp.where(j <= cur, imp + forced, NEG)
            sel = (_ranks(sc_ref, nsel, nsel) < ntop) & (j <= cur)
        bias_ref[0:nsel, toks] = jnp.where(sel, 0.0, NEG).astype(bias_ref.dtype)
        if nsel < FEAT:
            bias_ref[nsel:FEAT, toks] = jnp.full((FEAT - nsel, tq), NEG, bias_ref.dtype)

    for qi in range(nq):
        q_tile(qi)


def _cmp_attn(nq_t, kvc, ovt, *, B, S):
    tq, nh, G = TQ, NSA_HG, NSA_KV_GROUPS
    R = nh * tq
    nq = S // tq
    NC = kvc.shape[2]
    ns = S // NSA_SEL_BLOCK
    assert ns <= FEAT
    ntop = min(NSA_TOP_N, ns)
    return pl.pallas_call(
        functools.partial(_cmp_attn_kernel, tq=tq, nh=nh, ns=ns, ntop=ntop),
        grid=(B, G),
        in_specs=[pl.BlockSpec((nh * HEAD_DIM, S), lambda b, g: (g, b)),
                  pl.BlockSpec((None, None, NC, LANES), lambda b, g: (0, b, 0, g)),
                  pl.BlockSpec((None, None, NC, LANES), lambda b, g: (1, b, 0, g)),
                  pl.BlockSpec(ovt.shape, lambda b, g: (0, 0))],
        out_specs=[pl.BlockSpec((None, None, HEAD_DIM, S * nh), lambda b, g: (b, g, 0, 0)),
                   pl.BlockSpec((None, None, FEAT, S), lambda b, g: (b, g, 0, 0))],
        out_shape=[jax.ShapeDtypeStruct((B, G, HEAD_DIM, S * nh), BF16),
                   jax.ShapeDtypeStruct((B, G, FEAT, S), BF16)],
        scratch_shapes=[pltpu.VMEM((FEAT, tq), F32)],
        compiler_params=pltpu.CompilerParams(dimension_semantics=("parallel", "parallel"),
                                             vmem_limit_bytes=VMEM_LIMIT),
        name="cmp_attn",
    )(nq_t, kvc, kvc, ovt)


def _moba_gate_kernel(q_ref, k_ref, a_ref, bias_ref, sc_ref, *, nb, ntop):
    nbp, S = sc_ref.shape[1:]
    km = _dot(a_ref[...], k_ref[...])
    lane = lax.broadcasted_iota(jnp.int32, (nbp, LANES), 1)
    parts = []
    for hh in range(2):
        parts += _hilo(jnp.where(lane // HEAD_DIM == hh, km, 0.0))
    g = _dot(jnp.concatenate(parts, axis=0), q_ref[...])
    past = (lax.broadcasted_iota(jnp.int32, (nbp, S), 0)
            < lax.broadcasted_iota(jnp.int32, (nbp, S), 1) // MOBA_BLOCK)
    n = lax.broadcasted_iota(jnp.int32, (nbp, MOBA_BLOCK), 0)
    for hh in range(2):
        sc = sc_ref.at[hh]
        sc[...] = jnp.where(past, g[2 * hh * nbp:(2 * hh + 1) * nbp] + g[(2 * hh + 1) * nbp:(2 * hh + 2) * nbp], NEG)
        for c in range(nb):
            cols = slice(c * MOBA_BLOCK, (c + 1) * MOBA_BLOCK)
            sel = n <= c
            if c > ntop:
                nr = -(-c // SUBLANES) * SUBLANES
                rank = _ranks(sc, c, nr, cols)
                rank = rank + jnp.where(sc[0:nr, cols] < NEG, nb - c, 0)
                if nr < nbp:
                    rank = jnp.concatenate([rank, jnp.full((nbp - nr, MOBA_BLOCK), nb, jnp.int32)], axis=0)
                sel = ((rank < ntop) & (n < c)) | (n == c)
            bias_ref[hh, 0:nbp, cols] = jnp.where(sel, 0.0, NEG).astype(bias_ref.dtype)
        bias_ref[hh, nbp:FEAT, :] = jnp.zeros((FEAT - nbp, S), bias_ref.dtype)


def _moba_gate(mq_t, mk, avg, *, B, S):
    H = MOBA_HEADS
    nb = S // MOBA_BLOCK
    nbp = avg.shape[0]
    assert nb <= nbp and S % MOBA_BLOCK == 0 and MOBA_BLOCK % LANES == 0 and H % 2 == 0
    ntop = min(MOBA_TOP_K, nb)
    return pl.pallas_call(
        functools.partial(_moba_gate_kernel, nb=nb, ntop=ntop),
        grid=(B, H // 2),
        in_specs=[pl.BlockSpec((2 * HEAD_DIM, S), lambda b, h: (h, b)),
                  pl.BlockSpec((S, LANES), lambda b, h: (b, h)),
                  pl.BlockSpec(avg.shape, lambda b, h: (0, 0))],
        out_specs=pl.BlockSpec((None, 2, FEAT, S), lambda b, h: (b, h, 0, 0)),
        out_shape=jax.ShapeDtypeStruct((B, H, FEAT, S), BF16),
        scratch_shapes=[pltpu.VMEM((2, nbp, S), F32)],
        compiler_params=pltpu.CompilerParams(dimension_semantics=("parallel", "parallel"),
                                             vmem_limit_bytes=VMEM_LIMIT),
        name="moba_gate",
    )(mq_t, mk, avg)


def _flash_q_tile(qi, running_max, q_ref, bias_ref, mask_ref, o_ref, ka_ref, vta_ref, kn_ref, qa_ref, bnd_ref,
                  m_ref, acc_ref, alpha_ref, p_ref, *, tq, tk, nh, pair, window, has_past=None):
    u = pl.program_id(1)
    R = nh * tq
    static = isinstance(qi, int)
    q0 = qi * tq if static else pl.multiple_of(qi * tq, tq)
    toks = pl.ds(q0, tq)

    qcat = jnp.concatenate([q_ref[hh * HEAD_DIM:(hh + 1) * HEAD_DIM, toks] for hh in range(nh)], axis=1)
    qf = qcat.astype(F32)
    qq = jnp.concatenate([qf, qf], axis=0)
    rowi = lax.broadcasted_iota(jnp.int32, (2 * HEAD_DIM, R), 0)
    coli = lax.broadcasted_iota(jnp.int32, (2 * HEAD_DIM, R), 1)
    half = (coli // tq) % 2 if pair else u % 2
    qa_ref[0:2 * HEAD_DIM, :] = jnp.where(rowi // HEAD_DIM == half, qq, 0.0).astype(BF16)
    if bias_ref is None:
        qa_ref[2 * HEAD_DIM:2 * HEAD_DIM + FEAT, :] = jnp.zeros((FEAT, R), BF16)
    else:
        qa_ref[2 * HEAD_DIM:2 * HEAD_DIM + FEAT, :] = jnp.concatenate(
            [bias_ref[hh if pair else 0, :, toks] for hh in range(nh)], axis=1)
    qa_ref[2 * HEAD_DIM + FEAT:2 * LANES, :] = _alibi_query_rows(q0, tq, u * nh, R)
    if pair:
        kn = jnp.concatenate([kn_ref[hh:hh + 1, :] for hh in range(nh) for _ in range(tq // LANES)], axis=1)
    else:
        kn = jnp.concatenate([jnp.where(u % 2 == 0, kn_ref[0:1, :], kn_ref[1:2, :])] * (R // LANES), axis=1)
    qn = jnp.sqrt(jnp.sum(qf * qf, axis=0, keepdims=True))
    bnd_ref[...] = jnp.broadcast_to(qn * kn, bnd_ref.shape)

    ncg = R // CG
    kd = (q0 + tq - 1) // tk

    def key_rows(kind, c):
        off = (c * CG) % tq
        if kind is None:
            return 0, tk, None
        if kind == "diag":
            return 0, off + CG, off // CG
        return off, tk, (tk + off) // CG

    def values(ki, kind=None):
        for c in range(ncg):
            cols = pl.ds(c * CG, CG)
            r0, r1, _ = key_rows(kind, c)
            v = (c * CG) // tq if pair else 0
            pv = _dot(vta_ref[v, ki, :, r0:r1], p_ref[r0:r1, cols])
            if running_max:
                acc_ref[:, cols] = alpha_ref[0:1, cols] * acc_ref[:, cols] + pv
            else:
                acc_ref[:, cols] = acc_ref[:, cols] + pv

    def scores(ki, kind=None, prev=None, prev_kind=None):
        k0 = ki * tk if isinstance(ki, int) else pl.multiple_of(ki * tk, tk)
        s_groups = []
        for c in range(ncg):
            r0, r1, _ = key_rows(kind, c)
            kb = (c * CG) // (2 * tq) if pair else 0
            s_groups.append(_dot(ka_ref[kb, pl.ds(k0 + r0, r1 - r0), :], qa_ref[:, pl.ds(c * CG, CG)]))
        if prev is not None:
            values(prev, prev_kind)
        for c, s in enumerate(s_groups):
            cols = pl.ds(c * CG, CG)
            r0, r1, entry = key_rows(kind, c)
            if entry is not None:
                s = s + mask_ref[entry, r0:r1, :]
            if running_max:
                m_prev = m_ref[0:1, cols]
                m_new = jnp.maximum(m_prev, jnp.max(s, axis=0, keepdims=True))
                alpha_ref[:, cols] = jnp.broadcast_to(jnp.exp(m_prev - m_new), (alpha_ref.shape[0], CG))
                m_ref[:, cols] = jnp.broadcast_to(m_new, (m_ref.shape[0], CG))
                p_ref[r0:r1, cols] = jnp.exp(s - m_new).astype(BF16)
            else:
                p_ref[r0:r1, cols] = jnp.exp(s - bnd_ref[0:1, cols]).astype(BF16)

    def steady(lo, hi):
        def pair_body(j, carry):
            ki = lo + 2 * j
            scores(ki, prev=ki - 1)
            scores(ki + 1, prev=ki)
            return carry
        lax.fori_loop(0, (hi - lo) // 2, pair_body, 0)

        @pl.when((hi - lo) % 2 == 1)
        def _odd_one():
            scores(hi - 1, prev=hi - 2)

    def only_diagonal():
        scores(0, "diag")
        values(0, "diag")

    def past_then_diagonal():
        if window is not None:
            scores(kd - 1, "wedge")
            scores(kd, "diag", prev=kd - 1, prev_kind="wedge")
        elif static:
            scores(0)
            for ki in range(1, kd):
                scores(ki, prev=ki - 1)
            scores(kd, "diag", prev=kd - 1)
        else:
            scores(0)
            steady(1, kd)
            scores(kd, "diag", prev=kd - 1)
        values(kd, "diag")

    acc_ref[...] = jnp.zeros(acc_ref.shape, F32)
    if running_max:
        m_ref[...] = jnp.full(m_ref.shape, NEG, F32)
    if static:
        only_diagonal() if kd == 0 else past_then_diagonal()
    elif has_past:
        past_then_diagonal()
    else:
        pl.when(kd == 0)(only_diagonal)
        pl.when(kd > 0)(past_then_diagonal)

    acc = acc_ref[...]
    out = (acc[0:HEAD_DIM] / acc[HEAD_DIM:HEAD_DIM + 1]).astype(o_ref.dtype)
    if pair:
        for hh in range(nh):
            o_ref[hh, :, toks] = out[:, hh * tq:(hh + 1) * tq]
    else:
        o_ref[0, :, pl.ds(qi * R if static else pl.multiple_of(qi * R, R), R)] = out
    return jnp.min(acc[HEAD_DIM:HEAD_DIM + 1, :])


def _flash_kernel(*refs, tq, tk, nh, pair, window, has_bias):
    q_ref = refs[0]
    bias_ref = refs[1] if has_bias else None
    (k_ref, kf_ref, vt_ref, mask_ref, o_ref, ka_ref, vta_ref, kn_ref, qa_ref, bnd_ref, m_ref, acc_ref,
     alpha_ref, p_ref, low_ref) = refs[2 if has_bias else 1:]
    S = k_ref.shape[0]
    for kb in range(ka_ref.shape[0]):
        k = k_ref[:, kb * LANES:(kb + 1) * LANES]
        ka_ref[kb, :, 0:LANES] = k
        ka_ref[kb, :, LANES:2 * LANES] = kf_ref[...]
        k2 = jnp.square(k.astype(F32))
        lane = lax.broadcasted_iota(jnp.int32, k2.shape, 1)
        for hf in range(2):
            n2 = jnp.sum(jnp.where(lane // HEAD_DIM == hf, k2, 0.0), axis=1, keepdims=True)
            kn_ref[2 * kb + hf:2 * kb + hf + 1, :] = jnp.broadcast_to(
                jnp.sqrt(jnp.max(n2, axis=0, keepdims=True)), (1, LANES))
    ones_row = jnp.where(lax.broadcasted_iota(jnp.int32, (VROWS - HEAD_DIM, tk), 0) == 0, 1.0, 0.0)
    for v in range(nh if pair else 1):
        for jt in range(S // tk):
            vta_ref[v, jt, 0:HEAD_DIM, :] = vt_ref[v * HEAD_DIM:(v + 1) * HEAD_DIM, jt * tk:(jt + 1) * tk]
            vta_ref[v, jt, HEAD_DIM:VROWS, :] = ones_row.astype(BF16)

    tile = functools.partial(_flash_q_tile, q_ref=q_ref, bias_ref=bias_ref, mask_ref=mask_ref, o_ref=o_ref,
                             ka_ref=ka_ref, vta_ref=vta_ref, kn_ref=kn_ref, qa_ref=qa_ref, bnd_ref=bnd_ref,
                             m_ref=m_ref, acc_ref=acc_ref, alpha_ref=alpha_ref, p_ref=p_ref,
                             tq=tq, tk=tk, nh=nh, pair=pair, window=window)
    nq = S // tq
    low_ref[0] = tile(0, False)

    def bound_stabilised(qi, carry):
        low_ref[qi] = tile(qi, False, has_past=True)
        return carry
    lax.fori_loop(1, nq, bound_stabilised, 0)

    def redo(qi, carry):
        @pl.when(low_ref[qi] < MIN_WEIGHT_SUM)
        def _running_max():
            tile(qi, True)
        return carry

    lax.fori_loop(0, nq, redo, 0)


def _mask_table(tk, window):
    n_delta = (tk + (window or 0)) // CG
    d = np.arange(n_delta)[:, None, None] * CG + np.arange(CG)[None, None, :] - np.arange(tk)[None, :, None]
    ok = (d >= 0) & ((d < window) if window else True)
    return jnp.asarray(np.where(ok, 0.0, NEG), F32)


def _flash(q_t, bias_t, k, kfeat, v_t, *, B, S, units, tq, nh, pair, window):
    tk = TK
    R = nh * tq
    nq = S // tq
    nv = nh if pair else 1
    nb = nh if pair else 1
    nkb = nh // 2 if pair else 1
    assert tq == tk and S % tk == 0 and tk % CG == 0 and window in (None, tk)
    mask = _mask_table(tk, window)
    kcol = (lambda u: u) if pair else (lambda u: u // 2)
    out_w = tq if pair else R
    out_shape = (B, units * nb, HEAD_DIM, S if pair else S * nh)
    has_bias = bias_t is not None
    in_specs = [pl.BlockSpec((nh * HEAD_DIM, S), lambda b, u: (u, b))]
    if has_bias:
        in_specs.append(pl.BlockSpec((None, nb, FEAT, S), lambda b, u: (b, u, 0, 0)))
    in_specs += [pl.BlockSpec((S, nkb * LANES), lambda b, u: (b, kcol(u))),
                 pl.BlockSpec((S, LANES), lambda b, u: (0, 0)),
                 pl.BlockSpec((nv * HEAD_DIM, S), lambda b, u: (u, b)),
                 pl.BlockSpec(mask.shape, lambda b, u: (0, 0, 0))]
    args = (q_t,) + ((bias_t,) if has_bias else ()) + (k, kfeat, v_t, mask)
    return pl.pallas_call(
        functools.partial(_flash_kernel, tq=tq, tk=tk, nh=nh, pair=pair, window=window, has_bias=has_bias),
        grid=(B, units),
        in_specs=in_specs,
        out_specs=pl.BlockSpec((None, nb, HEAD_DIM, out_w * nq), lambda b, u: (b, u, 0, 0)),
        out_shape=jax.ShapeDtypeStruct(out_shape, BF16),
        scratch_shapes=[pltpu.VMEM((nkb, S, 2 * LANES), BF16),
                        pltpu.VMEM((nv, S // tk, VROWS, tk), BF16),
                        pltpu.VMEM((SUBLANES, LANES), F32),
                        pltpu.VMEM((2 * LANES, R), BF16),
                        pltpu.VMEM((SUBLANES, R), F32),
                        pltpu.VMEM((SUBLANES, R), F32),
                        pltpu.VMEM((VROWS, R), F32),
                        pltpu.VMEM((SUBLANES, R), F32),
                        pltpu.VMEM((tk, R), BF16),
                        pltpu.SMEM((nq,), F32)],
        compiler_params=pltpu.CompilerParams(dimension_semantics=("parallel", "parallel"),
                                             vmem_limit_bytes=VMEM_LIMIT),
        name="flash",
    )(*args)


def _even_out_kernel(x_ref, ocmp_ref, osel_ref, owin_ref, om_ref, gate_ref, nz_ref, mz_ref,
                     w_ref, g_ref, b_ref, o_ref, sig_ref, *, alpha, tm):
    G, nh = NSA_KV_GROUPS, NSA_HG
    sig_ref[...] = _sigmoid(gate_ref[...])
    heads = []
    for g in range(G):
        for hh in range(nh):
            acc = None
            for br, ref in enumerate((ocmp_ref, osel_ref, owin_ref)):
                gate = sig_ref[pl.ds(3 * (g * nh + hh) + br, 1), :]
                term = gate * ref[g, :, hh * tm:(hh + 1) * tm].astype(F32)
                acc = term if acc is None else acc + term
            heads.append(acc)
    o_nsa = jnp.concatenate(heads, axis=0)
    nz = nz_ref[...].astype(F32)
    mz = mz_ref[...].astype(F32)
    o_m = jnp.concatenate([om_ref[h] for h in range(MOBA_HEADS)], axis=0).astype(F32)
    a_t = jnp.concatenate([o_nsa * (nz * _sigmoid(nz)), o_m * (mz * _sigmoid(mz))], axis=0)
    y = _dot_tn(a_t.astype(BF16), w_ref[...])
    o_ref[...] = _layer_norm(alpha * x_ref[...] + y, g_ref[...], b_ref[...])


def _even_out(x2d, ocmp, osel, owin, om, gate_t, nz_t, mz_t, w_out, ln_g, ln_b, alpha, *, B, S):
    T, D = x2d.shape
    tm = TQ
    nq = S // tm
    G, nh, H = NSA_KV_GROUPS, NSA_HG, MOBA_HEADS
    nsa = pl.BlockSpec((None, G, HEAD_DIM, nh * tm), lambda i: (i // nq, 0, 0, i % nq))
    feat = lambda n: pl.BlockSpec((n, tm), lambda i: (0, i))
    full = lambda a: pl.BlockSpec(a.shape, lambda i: (0,) * a.ndim)
    return pl.pallas_call(
        functools.partial(_even_out_kernel, alpha=alpha, tm=tm),
        grid=(T // tm,),
        in_specs=[pl.BlockSpec((tm, D), lambda i: (i, 0)), nsa, nsa, nsa,
                  pl.BlockSpec((None, H, HEAD_DIM, tm), lambda i: (i // nq, 0, 0, i % nq)),
                  feat(GATE_ROWS), feat(NSA_W), feat(MOBA_W),
                  full(w_out), full(ln_g), full(ln_b)],
        out_specs=pl.BlockSpec((tm, D), lambda i: (i, 0)),
        out_shape=jax.ShapeDtypeStruct((T, D), F32),
        scratch_shapes=[pltpu.VMEM((GATE_ROWS, tm), F32)],
        compiler_params=pltpu.CompilerParams(dimension_semantics=("parallel",),
                                             vmem_limit_bytes=VMEM_LIMIT),
        name="even_out",
    )(x2d, ocmp, osel, owin, om, gate_t, nz_t, mz_t, w_out, ln_g, ln_b)


def _gla_head_exact(row0, h, q_ref, v_ref, o_ref, st_ref, b_ref, kf_ref, scale):
    C, dk, dv, SUB = GLA_CHUNK, GLA_DK, GLA_DV, GLA_SUB
    rows = lax.broadcasted_iota(jnp.int32, (C, dk), 0)
    sub_i = lax.broadcasted_iota(jnp.int32, (SUB, LANES), 0)
    lane = lax.broadcasted_iota(jnp.int32, (SUB, LANES), 1)
    ones = jnp.ones((dk, LANES), BF16)
    sl = pl.ds(h * dk, dk)
    rs_c = pl.ds(row0, C)
    bh = b_ref[:, sl]
    kf = kf_ref[:, sl]
    qf = q_ref[rs_c, sl].astype(F32) * scale
    vh = v_ref[rs_c, pl.ds(h * dv, dv)]
    st = st_ref[h]
    inter = _dot_nt((qf * jnp.exp(bh)).astype(BF16), st.astype(BF16))
    bl = b_ref[pl.ds(C - 1, 1), sl]
    kd = kf * jnp.exp(bl - bh)

    def off_block(r0, r1, c0, c1):
        ref_row = b_ref[pl.ds(r0, 1), sl]
        qt = qf[r0:r1] * jnp.exp(bh[r0:r1] - ref_row)
        ek = jnp.where((rows >= c0) & (rows < c1), ref_row - bh, NEG)
        kt = kf * jnp.exp(ek)
        return _dot_nt(qt.astype(BF16), kt.astype(BF16))

    lower = off_block(2 * SUB, C, 0, 2 * SUB)
    blocks = [None, off_block(SUB, 2 * SUB, 0, SUB), lower[0:SUB],
              lower[SUB:2 * SUB] + off_block(3 * SUB, C, 2 * SUB, 3 * SUB)]
    pieces = []
    for blk in range(C // SUB):
        r0 = blk * SUB
        q_i = qf[r0:r0 + SUB]
        b_i = bh[r0:r0 + SUB]
        for jj in range(SUB):
            b_j = b_ref[pl.ds(r0 + jj, 1), sl]
            k_j = kf_ref[pl.ds(r0 + jj, 1), sl]
            pieces.append((q_i * k_j * jnp.exp(jnp.minimum(b_i - b_j, 0.0))).astype(BF16))
    sums = _dot(jnp.concatenate(pieces, axis=0), ones)
    for blk in range(C // SUB):
        r0 = blk * SUB
        diag = jnp.zeros((SUB, LANES), F32)
        for jj in range(SUB):
            rs = sums[(r0 + jj) * SUB:(r0 + jj + 1) * SUB]
            diag = jnp.where((lane == r0 + jj) & (sub_i >= jj), rs, diag)
        diag = diag[:, 0:C]
        blocks[blk] = diag if blocks[blk] is None else blocks[blk] + diag
    att = jnp.concatenate(blocks, axis=0)
    o_ref[rs_c, pl.ds(h * dv, dv)] = (inter + _dot(att.astype(BF16), vh)).astype(o_ref.dtype)
    st_ref[h] = st * jnp.exp(bl) + _dot_tn(vh, kd.astype(BF16))


def _gla_head_bounded(row0, h, q_ref, v_ref, o_ref, st_ref, b_ref, kf_ref, scale):
    C, dk, dv = GLA_CHUNK, GLA_DK, GLA_DV
    sl = pl.ds(h * dk, dk)
    rs_c = pl.ds(row0, C)
    bh = b_ref[rs_c, sl]
    kf = kf_ref[rs_c, sl]
    qf = q_ref[rs_c, sl].astype(F32) * scale
    vh = v_ref[rs_c, pl.ds(h * dv, dv)]
    st = st_ref[h]
    qs = (qf * jnp.exp(bh)).astype(BF16)
    kt = (kf * jnp.exp(-bh)).astype(BF16)
    i = lax.broadcasted_iota(jnp.int32, (C, C), 0)
    j = lax.broadcasted_iota(jnp.int32, (C, C), 1)
    att = jnp.where(i >= j, _dot_nt(qs, kt), 0.0)
    inter = _dot_nt(qs, st.astype(BF16))
    o_ref[rs_c, pl.ds(h * dv, dv)] = (inter + _dot(att.astype(BF16), vh)).astype(o_ref.dtype)
    bl = b_ref[pl.ds(row0 + C - 1, 1), sl]
    kd = kf * jnp.exp(bl - bh)
    st_ref[h] = st * jnp.exp(bl) + _dot_tn(vh, kd.astype(BF16))


def _gla_kernel(q_ref, k_ref, v_ref, glr_ref, w2_ref, gb_ref, tril_ref, o_ref,
                st_ref, b_ref, kf_ref, bc_ref, kc_ref, *, scale):
    C, H = GLA_CHUNK, GLA_HEADS
    nch = q_ref.shape[0] // C

    @pl.when(pl.program_id(1) == 0)
    def _init():
        st_ref[...] = jnp.zeros(st_ref.shape, F32)

    whi, wlo = _hilo(w2_ref[...])
    part = tril_ref.shape[0]
    decay = None
    for r0 in range(0, nch * C, part):
        rs = pl.ds(r0, part)
        ghi, glo = _hilo(glr_ref[rs, :])
        pre = _dot(ghi, whi) + _dot(glo, whi) + _dot(ghi, wlo) + gb_ref[...]
        la = (jnp.minimum(pre, 0.0) - jnp.log(1.0 + jnp.exp(-jnp.abs(pre)))) * (1.0 / GLA_TAU)
        lhi, llo = _hilo(la)
        b_part = _dot(tril_ref[...], lhi) + _dot(tril_ref[...], llo)
        b_ref[rs, :] = b_part
        for c in range(part // C):
            d_c = -b_part[c * C + C - 1:(c + 1) * C, :]
            decay = d_c if decay is None else jnp.maximum(decay, d_c)
    kf_ref[...] = k_ref[...].astype(F32)
    bounded = jnp.max(decay) <= GLA_FACTOR_MAX_DECAY

    @pl.when(bounded)
    def _factored():
        for c in range(nch):
            for h in range(H):
                _gla_head_bounded(c * C, h, q_ref, v_ref, o_ref, st_ref, b_ref, kf_ref, scale)

    @pl.when(jnp.logical_not(bounded))
    def _exact():
        def chunk(c, carry):
            row0 = pl.multiple_of(c * C, C)
            bc_ref[...] = b_ref[pl.ds(row0, C), :]
            kc_ref[...] = kf_ref[pl.ds(row0, C), :]
            for h in range(H):
                _gla_head_exact(row0, h, q_ref, v_ref, o_ref, st_ref, bc_ref, kc_ref, scale)
            return carry
        lax.fori_loop(0, nch, chunk, 0)


def _gla(q, k, v, glr, w2p, gb, tril, *, B, S):
    C, H, dk, dv = GLA_CHUNK, GLA_HEADS, GLA_DK, GLA_DV
    rows = C * min(GLA_STEP_CHUNKS, S // C)
    nchunk = S // rows
    tok = lambda n: pl.BlockSpec((rows, n), lambda b, c: (b * nchunk + c, 0))
    const = lambda shape: pl.BlockSpec(shape, lambda b, c: (0,) * len(shape))
    return pl.pallas_call(
        functools.partial(_gla_kernel, scale=float(dk) ** -0.5),
        grid=(B, nchunk),
        in_specs=[tok(H * dk), tok(H * dk), tok(H * dv), tok(LANES),
                  const(w2p.shape), const(gb.shape), const(tril.shape)],
        out_specs=tok(H * dv),
        out_shape=jax.ShapeDtypeStruct((B * S, H * dv), BF16),
        scratch_shapes=[pltpu.VMEM((H, dv, dk), F32),
                        pltpu.VMEM((rows, H * dk), F32),
                        pltpu.VMEM((rows, H * dk), F32),
                        pltpu.VMEM((C, H * dk), F32),
                        pltpu.VMEM((C, H * dk), F32)],
        compiler_params=pltpu.CompilerParams(dimension_semantics=("parallel", "arbitrary"),
                                             vmem_limit_bytes=VMEM_LIMIT),
        name="gla",
    )(q, k, v, glr, w2p, gb, tril)


def _odd_out_kernel(x_ref, o_ref_in, z_ref, gn_ref, w_ref, g_ref, b_ref, out_ref, *, alpha):
    H, dv = GLA_HEADS, GLA_DV
    parts = []
    for h in range(H):
        sl = pl.ds(h * dv, dv)
        oh = o_ref_in[:, sl].astype(F32)
        r = lax.rsqrt(jnp.mean(oh * oh, axis=-1, keepdims=True) + LN_EPS)
        z = z_ref[:, sl].astype(F32)
        parts.append((oh * r * gn_ref[:, sl] * (z * _sigmoid(z))).astype(BF16))
    y = _dot(jnp.concatenate(parts, axis=1), w_ref[...])
    out_ref[...] = _layer_norm(alpha * x_ref[...] + y, g_ref[...], b_ref[...])


def _odd_out(x2d, o, z, gn_g, w_out, ln_g, ln_b, alpha, tm=1024):
    T, D = x2d.shape
    W = o.shape[1]
    row = lambda n: pl.BlockSpec((tm, n), lambda i: (i, 0))
    return pl.pallas_call(
        functools.partial(_odd_out_kernel, alpha=alpha),
        grid=(T // tm,),
        in_specs=[row(D), row(W), row(W),
                  pl.BlockSpec((1, W), lambda i: (0, 0)),
                  pl.BlockSpec((W, D), lambda i: (0, 0)),
                  pl.BlockSpec((1, D), lambda i: (0, 0)),
                  pl.BlockSpec((1, D), lambda i: (0, 0))],
        out_specs=row(D),
        out_shape=jax.ShapeDtypeStruct((T, D), F32),
        compiler_params=pltpu.CompilerParams(dimension_semantics=("parallel",),
                                             vmem_limit_bytes=VMEM_LIMIT),
        name="odd_out",
    )(x2d, o, z, gn_g, w_out, ln_g, ln_b)


def _odd_weights_kernel(w_ref, o_ref, *, n_qkv, n_gate, n_z):
    o_ref[0:n_qkv, :] = w_ref[0:n_qkv, :].astype(BF16)
    o_ref[n_qkv:n_qkv + n_z, :] = w_ref[n_qkv + n_gate:n_qkv + n_gate + n_z, :].astype(BF16)
    o_ref[n_qkv + n_z:n_qkv + n_z + n_gate, :] = w_ref[n_qkv:n_qkv + n_gate, :].astype(BF16)
    o_ref[n_qkv + n_z + n_gate:, :] = jnp.zeros((LANES - n_gate, o_ref.shape[1]), BF16)


def _odd_weights(w_t, tc=256):
    W, D = w_t.shape
    n_qkv, n_gate, n_z = sum(ODD_SPLITS[:3]), ODD_SPLITS[3], ODD_SPLITS[4]
    pack = 2 * SUBLANES
    assert n_qkv % pack == 0 and n_z % pack == 0 and n_gate % pack == 0 and n_gate < LANES and D % tc == 0
    return pl.pallas_call(
        functools.partial(_odd_weights_kernel, n_qkv=n_qkv, n_gate=n_gate, n_z=n_z),
        grid=(D // tc,),
        in_specs=[pl.BlockSpec((W, tc), lambda j: (0, j))],
        out_specs=pl.BlockSpec((n_qkv + n_z + LANES, tc), lambda j: (0, j)),
        out_shape=jax.ShapeDtypeStruct((n_qkv + n_z + LANES, D), BF16),
        compiler_params=pltpu.CompilerParams(dimension_semantics=("parallel",),
                                             vmem_limit_bytes=VMEM_LIMIT),
        name="odd_weights",
    )(w_t)


def _key_features(S, block):
    s = np.arange(S)
    f = np.zeros((S, LANES), np.float32)
    f[s, s // block] = 1.0
    f[:, FEAT + 0] = 1.0
    f[:, FEAT + 1] = 1.0
    f[:, FEAT + 2] = FEAT * (s // FEAT)
    f[:, FEAT + 3] = s % FEAT
    return jnp.asarray(f, BF16)


def _pad_to(a, n, axis):
    pad = [(0, 0)] * a.ndim
    pad[axis] = (0, n - a.shape[axis])
    return jnp.pad(a, pad)


def _even_layer(x2d, B, S, w_in, cmp_pos, cmp_w1, cmp_b1, cmp_w2, w_out, ln_g, ln_b, alpha):
    T, D = x2d.shape
    G, hd = NSA_KV_GROUPS, HEAD_DIM
    scale = hd ** -0.5
    wq, wkv, wgate, wnz, wmq, wmk, wmv, wmz = jnp.split(w_in, np.cumsum(EVEN_SPLITS)[:-1].tolist(), axis=1)
    wkc, wvc, wks, wvs, wkw, wvw = jnp.split(wkv, 6, axis=1)
    w_tok = jnp.concatenate([wkc, wvc, wks, wkw, wmk], axis=1).astype(BF16)
    w_tr = jnp.concatenate([wq * scale, wvs, wvw, wmq * scale, wmv, wnz, wmz,
                            _pad_to(wgate, GATE_ROWS, 1)], axis=1).T.astype(BF16)
    (kcmp, vcmp, ksel, kwin, mk, nq_t, vsel_t, vwin_t, mq_t, mv_t, nz_t, mz_t, gate_t) = _proj(
        x2d, w_tok, (NSA_KV_W,) * 4 + (MOBA_W,), (F32, F32, BF16, BF16, BF16),
        w_tr, (NSA_W, NSA_KV_W, NSA_KV_W, MOBA_W, MOBA_W, NSA_W, MOBA_W, GATE_ROWS),
        (BF16,) * 7 + (F32,))

    L, d, HID = NSA_CMP_BLOCK, NSA_CMP_STRIDE, NSA_CMP_HIDDEN
    ncmp = (S - L) // d + 1
    NC = S // d
    def group_diagonal(w, axis):
        z = jnp.zeros_like(w)
        return jnp.stack([jnp.concatenate([w if h == g else z for h in range(G)], axis=-1) for g in range(G)], axis=axis)

    w1r = cmp_w1.reshape(2, 2, d, hd, HID).astype(BF16)
    w1p = group_diagonal(w1r, 3).reshape(2, 2, d * G * hd, G * HID)
    posp = jnp.broadcast_to(cmp_pos.reshape(2, 2, d, 1, hd), (2, 2, d, G, hd)).reshape(2, 2, 1, d * G * hd)
    b1p = jnp.tile(cmp_b1, (1, G)).reshape(2, 1, G * HID)
    w2p = group_diagonal(_pad_to(cmp_w2, LANES, 2).astype(BF16), 1).reshape(2, G * HID, G * LANES)
    cfeat = np.zeros((NC, G, LANES), np.float32)
    cfeat[:, :, hd + 0] = 1.0
    cfeat[:, :, hd + 1] = 1.0
    cfeat[:, :, hd + 2] = (np.arange(NC) * d)[:, None]
    cfeat[:, :, hd + 3] = L - 1
    kvc = _cmp_mlp(kcmp, vcmp, w1p, posp, b1p, w2p, jnp.asarray(cfeat.reshape(NC, G * LANES)), ncmp,
                   B=B, S=S)

    ns = S // NSA_SEL_BLOCK
    cmp_start = np.arange(NC) * d
    sel_start = np.arange(FEAT) * NSA_SEL_BLOCK
    ovt = ((cmp_start[None, :] <= sel_start[:, None] + NSA_SEL_BLOCK - 1)
           & (cmp_start[None, :] + L - 1 >= sel_start[:, None])
           & (np.arange(NC)[None, :] < ncmp) & (np.arange(FEAT)[:, None] < ns))
    o_cmp, bias_sel = _cmp_attn(nq_t, kvc, jnp.asarray(ovt, BF16), B=B, S=S)

    feats_sel = _key_features(S, NSA_SEL_BLOCK)
    o_sel = _flash(nq_t, bias_sel, ksel, feats_sel, vsel_t,
                   B=B, S=S, units=G, tq=TQ, nh=NSA_HG, pair=False, window=None)
    o_win = _flash(nq_t, None, kwin, feats_sel, vwin_t,
                   B=B, S=S, units=G, tq=TQ, nh=NSA_HG, pair=False, window=NSA_WINDOW)

    avg = np.zeros((MOBA_NBP, S), np.float32)
    avg[np.arange(S) // MOBA_BLOCK, np.arange(S)] = 1.0 / MOBA_BLOCK
    bias_m = _moba_gate(mq_t, mk, jnp.asarray(avg, BF16), B=B, S=S)
    o_m = _flash(mq_t, bias_m, mk, _key_features(S, MOBA_BLOCK), mv_t,
                 B=B, S=S, units=MOBA_HEADS // MOBA_STEP_HEADS, tq=TQM, nh=MOBA_STEP_HEADS, pair=True, window=None)

    return _even_out(x2d, o_cmp, o_sel, o_win, o_m, gate_t, nz_t, mz_t, w_out.astype(BF16),
                     ln_g.reshape(1, D), ln_b.reshape(1, D), alpha, B=B, S=S)


def _odd_layer(x2d, B, S, w_in, gate_w2, gate_b, gn_g, w_out, ln_g, ln_b, alpha):
    T, D = x2d.shape
    H, dk, dv, C = GLA_HEADS, GLA_DK, GLA_DV, GLA_CHUNK
    q, k, v, z, glr = _proj(x2d, _odd_weights(w_in.T), (H * dk, H * dk, H * dv, H * dv, LANES),
                            (BF16, BF16, BF16, BF16, F32), tok_transposed=True)
    w2p = _pad_to(gate_w2, LANES, 0)
    ncs = min(GLA_CUMSUM_CHUNKS, GLA_STEP_CHUNKS, S // C)
    tril = jnp.asarray(np.kron(np.eye(ncs), np.tril(np.ones((C, C)))), BF16)
    o = _gla(q, k, v, glr, w2p, gate_b.reshape(1, H * dk), tril, B=B, S=S)
    return _odd_out(x2d, o, z, gn_g.reshape(1, H * dv), w_out.astype(BF16),
                    ln_g.reshape(1, D), ln_b.reshape(1, D), alpha)


def kernel(x, ev_w_in, ev_cmp_pos, ev_cmp_w1, ev_cmp_b1, ev_cmp_w2, ev_w_out, ev_ln_g, ev_ln_b,
           od_w_in, od_gate_w2, od_gate_b, od_gn_g, od_w_out, od_ln_g, od_ln_b):
    B, S, D = x.shape
    depth = ev_w_in.shape[0] + od_w_in.shape[0]
    alpha = (2.0 * depth) ** 0.25
    h = x.reshape(B * S, D)
    for layer in range(depth):
        i = layer // 2
        if layer % 2 == 0:
            h = _even_layer(h, B, S, ev_w_in[i], ev_cmp_pos[i], ev_cmp_w1[i], ev_cmp_b1[i], ev_cmp_w2[i],
                            ev_w_out[i], ev_ln_g[i], ev_ln_b[i], alpha)
        else:
            h = _odd_layer(h, B, S, od_w_in[i], od_gate_w2[i], od_gate_b[i], od_gn_g[i],
                           od_w_out[i], od_ln_g[i], od_ln_b[i], alpha)
    return h.reshape(B, S, D)
```

```python
import functools

import numpy as np
import jax
import jax.numpy as jnp
from jax import lax
from jax.experimental import pallas as pl
from jax.experimental.pallas import tpu as pltpu

F32 = jnp.float32
BF16 = jnp.bfloat16

HEAD_DIM = 64
NSA_HEADS = 8
NSA_KV_GROUPS = 2
NSA_HG = NSA_HEADS // NSA_KV_GROUPS
NSA_CMP_BLOCK = 32
NSA_CMP_STRIDE = 16
NSA_CMP_HIDDEN = 128
NSA_SEL_BLOCK = 64
NSA_TOP_N = 16
NSA_WINDOW = 512
MOBA_HEADS = 8
MOBA_BLOCK = 256
MOBA_TOP_K = 3
GLA_HEADS = 4
GLA_DK = 128
GLA_DV = 256
GLA_GATE_RANK = 16
GLA_TAU = 16.0
GLA_CHUNK = 64
GLA_SUB = 16
GLA_STEP_CHUNKS = 16
GLA_CUMSUM_CHUNKS = 4
GLA_FACTOR_MAX_DECAY = 60.0
LN_EPS = 1e-5
NEG = -1e30
FORCE_BONUS = 1e4

NSA_W = NSA_HEADS * HEAD_DIM
NSA_KV_W = NSA_KV_GROUPS * HEAD_DIM
MOBA_W = MOBA_HEADS * HEAD_DIM
EVEN_SPLITS = (NSA_W, 6 * NSA_KV_W, 3 * NSA_HEADS, NSA_W, MOBA_W, MOBA_W, MOBA_W, MOBA_W)
ODD_SPLITS = (GLA_HEADS * GLA_DK, GLA_HEADS * GLA_DK, GLA_HEADS * GLA_DV, GLA_GATE_RANK, GLA_HEADS * GLA_DV)

LANES = 128
SUBLANES = 8
FEAT = 64
ALIBI_ROWS = 16
GATE_ROWS = 32
MOBA_NBP = 16
TQ = 512
TQM = 512
MOBA_STEP_HEADS = 4
TK = 512
CG = 256
MIN_WEIGHT_SUM = 1e-25
VROWS = HEAD_DIM + 16
VMEM_LIMIT = 48 * 1024 * 1024


def _dot(a, b):
    return jnp.dot(a, b, preferred_element_type=F32)


def _dot_nt(a, b):
    return lax.dot_general(a, b, (((1,), (1,)), ((), ())), preferred_element_type=F32)


def _dot_tn(a, b):
    return lax.dot_general(a, b, (((0,), (0,)), ((), ())), preferred_element_type=F32)


def _hilo(a):
    hi = a.astype(BF16)
    lo = (a - hi.astype(F32)).astype(BF16)
    return hi, lo


def _pow2_neg(n):
    return lax.bitcast_convert_type(lax.shift_left(127 - n, 23), F32)


def _sigmoid(x):
    return 0.5 * jnp.tanh(0.5 * x) + 0.5


def _layer_norm(r, g, b):
    mu = jnp.mean(r, axis=-1, keepdims=True)
    d = r - mu
    var = jnp.mean(d * d, axis=-1, keepdims=True)
    return d * lax.rsqrt(var + LN_EPS) * g + b


def _alibi_query_rows(t0, tq, head0, R):
    r = lax.broadcasted_iota(jnp.int32, (ALIBI_ROWS, R), 0)
    col = lax.broadcasted_iota(jnp.int32, (ALIBI_ROWS, R), 1)
    t = t0 + col % tq
    slope = _pow2_neg(head0 + col // tq + 1)
    feats = jnp.where(r == 0, -FEAT * slope * (t // FEAT).astype(F32),
                      jnp.where(r == 1, -slope * (t % FEAT).astype(F32), jnp.where(r < 4, slope, 0.0)))
    return jnp.concatenate([feats.astype(BF16), jnp.zeros((FEAT - ALIBI_ROWS, R), BF16)], axis=0)


def _ranks(sc_ref, n_real, n_rows=None, cols=slice(None)):
    n_rows = sc_ref.shape[0] if n_rows is None else n_rows
    out = []
    for v in range(n_rows // SUBLANES):
        lo = v * SUBLANES
        sc = sc_ref[lo:lo + SUBLANES, cols]
        width = sc.shape[1]
        idx = lo + lax.broadcasted_iota(jnp.int32, (SUBLANES, width), 0)
        rank = jnp.zeros((SUBLANES, width), jnp.int32)
        for jp in range(n_real):
            r = sc_ref[pl.ds(jp, 1), cols]
            if jp < lo:
                beats = r >= sc
            elif jp >= lo + SUBLANES - 1:
                beats = r > sc
            else:
                beats = (r > sc) | ((r == sc) & (idx > jp))
            rank = rank + jnp.where(beats, 1, 0)
        out.append(rank)
    return jnp.concatenate(out, axis=0)


def _proj_kernel(*refs, tok_splits, tr_splits):
    x_ref, wt_ref = refs[0], refs[1]
    outs = list(refs[3:] if tr_splits else refs[2:])
    x = x_ref[...]
    xb = x.astype(BF16)
    off = 0
    for n in tok_splits:
        o_ref = outs.pop(0)
        o_ref[...] = _dot_nt(xb, wt_ref[off:off + n, :]).astype(o_ref.dtype)
        off += n
    if tr_splits:
        wtr_ref = refs[2]
        xt = x.T.astype(BF16)
        off = 0
        for n in tr_splits:
            o_ref = outs.pop(0)
            o_ref[...] = _dot(wtr_ref[off:off + n, :], xt).astype(o_ref.dtype)
            off += n


def _proj(x2d, w_tok, tok_splits, tok_dtypes, w_tr=None, tr_splits=(), tr_dtypes=(), tm=512):
    T, D = x2d.shape
    assert T % tm == 0 and sum(tok_splits) == w_tok.shape[0]
    in_specs = [pl.BlockSpec((tm, D), lambda i: (i, 0)),
                pl.BlockSpec(w_tok.shape, lambda i: (0, 0))]
    args = [x2d, w_tok]
    if tr_splits:
        assert sum(tr_splits) == w_tr.shape[0]
        in_specs.append(pl.BlockSpec(w_tr.shape, lambda i: (0, 0)))
        args.append(w_tr)
    out_specs = ([pl.BlockSpec((tm, n), lambda i: (i, 0)) for n in tok_splits]
                 + [pl.BlockSpec((n, tm), lambda i: (0, i)) for n in tr_splits])
    out_shape = ([jax.ShapeDtypeStruct((T, n), dt) for n, dt in zip(tok_splits, tok_dtypes)]
                 + [jax.ShapeDtypeStruct((n, T), dt) for n, dt in zip(tr_splits, tr_dtypes)])
    return pl.pallas_call(
        functools.partial(_proj_kernel, tok_splits=tuple(tok_splits), tr_splits=tuple(tr_splits)),
        grid=(T // tm,), in_specs=in_specs, out_specs=out_specs, out_shape=out_shape,
        compiler_params=pltpu.CompilerParams(dimension_semantics=("parallel",),
                                             vmem_limit_bytes=VMEM_LIMIT),
        name="proj",
    )(*args)


def _cmp_mlp_kernel(kx_ref, vx_ref, w1_ref, pos_ref, b1_ref, w2_ref, cf_ref, o_ref, *, ncmp):
    d = NSA_CMP_STRIDE
    nc = kx_ref.shape[0] // d
    cw = w1_ref.shape[2]
    for i, x_ref in enumerate((kx_ref, vx_ref)):
        a = bm = None
        for l in range(0, d, 2):
            x2 = jnp.concatenate([x_ref[pl.ds(l, nc, stride=d), :], x_ref[pl.ds(l + 1, nc, stride=d), :]],
                                 axis=1).astype(BF16)
            rows = pl.ds(l * x_ref.shape[1], 2 * x_ref.shape[1])
            a_l, b_l = _dot(x2, w1_ref[i, 0, rows, :]), _dot(x2, w1_ref[i, 1, rows, :])
            a, bm = (a_l, b_l) if a is None else (a + a_l, bm + b_l)
        c = b1_ref[i]
        for half in range(2):
            phi, plo = _hilo(jnp.broadcast_to(pos_ref[i, half], (SUBLANES, cw)))
            c = c + (_dot(phi, w1_ref[i, half]) + _dot(plo, w1_ref[i, half]))[0:1]
        hid = a + pltpu.roll(bm, nc - 1, axis=0) + c
        hid = hid * _sigmoid(hid)
        row = lax.broadcasted_iota(jnp.int32, hid.shape, 0)
        hid = jnp.where(row < ncmp, hid, 0.0)
        out = _dot(hid.astype(BF16), w2_ref[i])
        if i == 0:
            out = out + cf_ref[...]
        o_ref[i] = out.astype(o_ref.dtype)


def _cmp_mlp(kx, vx, w1p, posp, b1p, w2p, cfeat, ncmp, *, B, S):
    NC = S // NSA_CMP_STRIDE
    OW = w2p.shape[-1]
    full = lambda a: pl.BlockSpec(a.shape, lambda b: (0,) * a.ndim)
    return pl.pallas_call(
        functools.partial(_cmp_mlp_kernel, ncmp=ncmp),
        grid=(B,),
        in_specs=[pl.BlockSpec((S, kx.shape[1]), lambda b: (b, 0)),
                  pl.BlockSpec((S, vx.shape[1]), lambda b: (b, 0)),
                  full(w1p), full(posp), full(b1p), full(w2p), full(cfeat)],
        out_specs=pl.BlockSpec((2, None, NC, OW), lambda b: (0, b, 0, 0)),
        out_shape=jax.ShapeDtypeStruct((2, B, NC, OW), BF16),
        compiler_params=pltpu.CompilerParams(dimension_semantics=("parallel",),
                                             vmem_limit_bytes=VMEM_LIMIT),
        name="cmp_mlp",
    )(kx, vx, w1p, posp, b1p, w2p, cfeat)


def _cmp_attn_kernel(q_ref, kc_ref, vc_ref, ovt_ref, o_ref, bias_ref, sc_ref, *, tq, nh, ns, ntop):
    g = pl.program_id(1)
    R = nh * tq
    nq = q_ref.shape[1] // tq

    def q_tile(qi):
        toks = pl.ds(qi * tq, tq)
        nc = min(kc_ref.shape[0], -(-((qi + 1) * tq) // (NSA_CMP_STRIDE * 16)) * 16)
        nsel = min(ns, -(-((qi + 1) * tq) // (NSA_SEL_BLOCK * SUBLANES)) * SUBLANES)
        qcat = jnp.concatenate([q_ref[hh * HEAD_DIM:(hh + 1) * HEAD_DIM, toks] for hh in range(nh)], axis=1)
        qa = jnp.concatenate([qcat, _alibi_query_rows(qi * tq, tq, g * nh, R)], axis=0)
        s = _dot(kc_ref[0:nc, :], qa)
        n = lax.broadcasted_iota(jnp.int32, (nc, R), 0)
        t_row = qi * tq + lax.broadcasted_iota(jnp.int32, (1, R), 1) % tq
        last = (t_row - (NSA_CMP_BLOCK - 1)) // NSA_CMP_STRIDE
        s = jnp.where(n <= last, s, NEG)
        e = jnp.exp(s - jnp.max(s, axis=0, keepdims=True))
        inv = jnp.where(last >= 0, 1.0 / jnp.sum(e, axis=0, keepdims=True), 0.0)
        p = e * inv
        o_ref[:, pl.ds(qi * R, R)] = _dot_tn(vc_ref[0:nc, 0:HEAD_DIM], p.astype(BF16)).astype(o_ref.dtype)

        psum = p[:, 0:tq]
        for hh in range(1, nh):
            psum = psum + p[:, hh * tq:(hh + 1) * tq]
        phi, plo = _hilo(psum)
        ovt = ovt_ref[0:nsel, 0:nc]
        imp = _dot(ovt, phi) + _dot(ovt, plo)

        j = lax.broadcasted_iota(jnp.int32, (nsel, tq), 0)
        cur = (qi * tq + lax.broadcasted_iota(jnp.int32, (nsel, tq), 1)) // NSA_SEL_BLOCK
        if nsel <= ntop:
            sel = j <= cur
        else:
            forced = jnp.where((j == 0) | (j == cur) | (j == cur - 1), FORCE_BONUS, 0.0)
            sc_ref[0:nsel, :] = jnp.where(j <= cur, imp + forced, NEG)
            sel = (_ranks(sc_ref, nsel, nsel) < ntop) & (j <= cur)
        bias_ref[0:nsel, toks] = jnp.where(sel, 0.0, NEG).astype(bias_ref.dtype)
        if nsel < FEAT:
            bias_ref[nsel:FEAT, toks] = jnp.full((FEAT - nsel, tq), NEG, bias_ref.dtype)

    for qi in range(nq):
        q_tile(qi)


def _cmp_attn(nq_t, kvc, ovt, *, B, S):
    tq, nh, G = TQ, NSA_HG, NSA_KV_GROUPS
    R = nh * tq
    nq = S // tq
    NC = kvc.shape[2]
    ns = S // NSA_SEL_BLOCK
    assert ns <= FEAT
    ntop = min(NSA_TOP_N, ns)
    return pl.pallas_call(
        functools.partial(_cmp_attn_kernel, tq=tq, nh=nh, ns=ns, ntop=ntop),
        grid=(B, G),
        in_specs=[pl.BlockSpec((nh * HEAD_DIM, S), lambda b, g: (g, b)),
                  pl.BlockSpec((None, None, NC, LANES), lambda b, g: (0, b, 0, g)),
                  pl.BlockSpec((None, None, NC, LANES), lambda b, g: (1, b, 0, g)),
                  pl.BlockSpec(ovt.shape, lambda b, g: (0, 0))],
        out_specs=[pl.BlockSpec((None, None, HEAD_DIM, S * nh), lambda b, g: (b, g, 0, 0)),
                   pl.BlockSpec((None, None, FEAT, S), lambda b, g: (b, g, 0, 0))],
        out_shape=[jax.ShapeDtypeStruct((B, G, HEAD_DIM, S * nh), BF16),
                   jax.ShapeDtypeStruct((B, G, FEAT, S), BF16)],
        scratch_shapes=[pltpu.VMEM((FEAT, tq), F32)],
        compiler_params=pltpu.CompilerParams(dimension_semantics=("parallel", "parallel"),
                                             vmem_limit_bytes=VMEM_LIMIT),
        name="cmp_attn",
    )(nq_t, kvc, kvc, ovt)


def _moba_gate_kernel(q_ref, k_ref, a_ref, bias_ref, sc_ref, *, nb, ntop):
    nbp, S = sc_ref.shape[1:]
    km = _dot(a_ref[...], k_ref[...])
    lane = lax.broadcasted_iota(jnp.int32, (nbp, LANES), 1)
    parts = []
    for hh in range(2):
        parts += _hilo(jnp.where(lane // HEAD_DIM == hh, km, 0.0))
    g = _dot(jnp.concatenate(parts, axis=0), q_ref[...])
    past = (lax.broadcasted_iota(jnp.int32, (nbp, S), 0)
            < lax.broadcasted_iota(jnp.int32, (nbp, S), 1) // MOBA_BLOCK)
    n = lax.broadcasted_iota(jnp.int32, (nbp, MOBA_BLOCK), 0)
    for hh in range(2):
        sc = sc_ref.at[hh]
        sc[...] = jnp.where(past, g[2 * hh * nbp:(2 * hh + 1) * nbp] + g[(2 * hh + 1) * nbp:(2 * hh + 2) * nbp], NEG)
        for c in range(nb):
            cols = slice(c * MOBA_BLOCK, (c + 1) * MOBA_BLOCK)
            sel = n <= c
            if c > ntop:
                nr = -(-c // SUBLANES) * SUBLANES
                rank = _ranks(sc, c, nr, cols)
                rank = rank + jnp.where(sc[0:nr, cols] < NEG, nb - c, 0)
                if nr < nbp:
                    rank = jnp.concatenate([rank, jnp.full((nbp - nr, MOBA_BLOCK), nb, jnp.int32)], axis=0)
                sel = ((rank < ntop) & (n < c)) | (n == c)
            bias_ref[hh, 0:nbp, cols] = jnp.where(sel, 0.0, NEG).astype(bias_ref.dtype)
        bias_ref[hh, nbp:FEAT, :] = jnp.zeros((FEAT - nbp, S), bias_ref.dtype)


def _moba_gate(mq_t, mk, avg, *, B, S):
    H = MOBA_HEADS
    nb = S // MOBA_BLOCK
    nbp = avg.shape[0]
    assert nb <= nbp and S % MOBA_BLOCK == 0 and MOBA_BLOCK % LANES == 0 and H % 2 == 0
    ntop = min(MOBA_TOP_K, nb)
    return pl.pallas_call(
        functools.partial(_moba_gate_kernel, nb=nb, ntop=ntop),
        grid=(B, H // 2),
        in_specs=[pl.BlockSpec((2 * HEAD_DIM, S), lambda b, h: (h, b)),
                  pl.BlockSpec((S, LANES), lambda b, h: (b, h)),
                  pl.BlockSpec(avg.shape, lambda b, h: (0, 0))],
        out_specs=pl.BlockSpec((None, 2, FEAT, S), lambda b, h: (b, h, 0, 0)),
        out_shape=jax.ShapeDtypeStruct((B, H, FEAT, S), BF16),
        scratch_shapes=[pltpu.VMEM((2, nbp, S), F32)],
        compiler_params=pltpu.CompilerParams(dimension_semantics=("parallel", "parallel"),
                                             vmem_limit_bytes=VMEM_LIMIT),
        name="moba_gate",
    )(mq_t, mk, avg)


def _flash_q_tile(qi, running_max, q_ref, bias_ref, mask_ref, o_ref, ka_ref, vta_ref, kn_ref, qa_ref, bnd_ref,
                  m_ref, acc_ref, alpha_ref, p_ref, *, tq, tk, nh, pair, window, has_past=None):
    u = pl.program_id(1)
    R = nh * tq
    static = isinstance(qi, int)
    q0 = qi * tq if static else pl.multiple_of(qi * tq, tq)
    toks = pl.ds(q0, tq)

    qcat = jnp.concatenate([q_ref[hh * HEAD_DIM:(hh + 1) * HEAD_DIM, toks] for hh in range(nh)], axis=1)
    qf = qcat.astype(F32)
    qq = jnp.concatenate([qf, qf], axis=0)
    rowi = lax.broadcasted_iota(jnp.int32, (2 * HEAD_DIM, R), 0)
    coli = lax.broadcasted_iota(jnp.int32, (2 * HEAD_DIM, R), 1)
    half = (coli // tq) % 2 if pair else u % 2
    qa_ref[0:2 * HEAD_DIM, :] = jnp.where(rowi // HEAD_DIM == half, qq, 0.0).astype(BF16)
    if bias_ref is None:
        qa_ref[2 * HEAD_DIM:2 * HEAD_DIM + FEAT, :] = jnp.zeros((FEAT, R), BF16)
    else:
        qa_ref[2 * HEAD_DIM:2 * HEAD_DIM + FEAT, :] = jnp.concatenate(
            [bias_ref[hh if pair else 0, :, toks] for hh in range(nh)], axis=1)
    qa_ref[2 * HEAD_DIM + FEAT:2 * LANES, :] = _alibi_query_rows(q0, tq, u * nh, R)
    if pair:
        kn = jnp.concatenate([kn_ref[hh:hh + 1, :] for hh in range(nh) for _ in range(tq // LANES)], axis=1)
    else:
        kn = jnp.concatenate([jnp.where(u % 2 == 0, kn_ref[0:1, :], kn_ref[1:2, :])] * (R // LANES), axis=1)
    qn = jnp.sqrt(jnp.sum(qf * qf, axis=0, keepdims=True))
    bnd_ref[...] = jnp.broadcast_to(qn * kn, bnd_ref.shape)

    ncg = R // CG
    kd = (q0 + tq - 1) // tk

    def key_rows(kind, c):
        off = (c * CG) % tq
        if kind is None:
            return 0, tk, None
        if kind == "diag":
            return 0, off + CG, off // CG
        return off, tk, (tk + off) // CG

    def values(ki, kind=None):
        for c in range(ncg):
            cols = pl.ds(c * CG, CG)
            r0, r1, _ = key_rows(kind, c)
            v = (c * CG) // tq if pair else 0
            pv = _dot(vta_ref[v, ki, :, r0:r1], p_ref[r0:r1, cols])
            if running_max:
                acc_ref[:, cols] = alpha_ref[0:1, cols] * acc_ref[:, cols] + pv
            else:
                acc_ref[:, cols] = acc_ref[:, cols] + pv

    def scores(ki, kind=None, prev=None, prev_kind=None):
        k0 = ki * tk if isinstance(ki, int) else pl.multiple_of(ki * tk, tk)
        s_groups = []
        for c in range(ncg):
            r0, r1, _ = key_rows(kind, c)
            kb = (c * CG) // (2 * tq) if pair else 0
            s_groups.append(_dot(ka_ref[kb, pl.ds(k0 + r0, r1 - r0), :], qa_ref[:, pl.ds(c * CG, CG)]))
        if prev is not None:
            values(prev, prev_kind)
        for c, s in enumerate(s_groups):
            cols = pl.ds(c * CG, CG)
            r0, r1, entry = key_rows(kind, c)
            if entry is not None:
                s = s + mask_ref[entry, r0:r1, :]
            if running_max:
                m_prev = m_ref[0:1, cols]
                m_new = jnp.maximum(m_prev, jnp.max(s, axis=0, keepdims=True))
                alpha_ref[:, cols] = jnp.broadcast_to(jnp.exp(m_prev - m_new), (alpha_ref.shape[0], CG))
                m_ref[:, cols] = jnp.broadcast_to(m_new, (m_ref.shape[0], CG))
                p_ref[r0:r1, cols] = jnp.exp(s - m_new).astype(BF16)
            else:
                p_ref[r0:r1, cols] = jnp.exp(s - bnd_ref[0:1, cols]).astype(BF16)

    def steady(lo, hi):
        def pair_body(j, carry):
            ki = lo + 2 * j
            scores(ki, prev=ki - 1)
            scores(ki + 1, prev=ki)
            return carry
        lax.fori_loop(0, (hi - lo) // 2, pair_body, 0)

        @pl.when((hi - lo) % 2 == 1)
        def _odd_one():
            scores(hi - 1, prev=hi - 2)

    def only_diagonal():
        scores(0, "diag")
        values(0, "diag")

    def past_then_diagonal():
        if window is not None:
            scores(kd - 1, "wedge")
            scores(kd, "diag", prev=kd - 1, prev_kind="wedge")
        elif static:
            scores(0)
            for ki in range(1, kd):
                scores(ki, prev=ki - 1)
            scores(kd, "diag", prev=kd - 1)
        else:
            scores(0)
            steady(1, kd)
            scores(kd, "diag", prev=kd - 1)
        values(kd, "diag")

    acc_ref[...] = jnp.zeros(acc_ref.shape, F32)
    if running_max:
        m_ref[...] = jnp.full(m_ref.shape, NEG, F32)
    if static:
        only_diagonal() if kd == 0 else past_then_diagonal()
    elif has_past:
        past_then_diagonal()
    else:
        pl.when(kd == 0)(only_diagonal)
        pl.when(kd > 0)(past_then_diagonal)

    acc = acc_ref[...]
    out = (acc[0:HEAD_DIM] / acc[HEAD_DIM:HEAD_DIM + 1]).astype(o_ref.dtype)
    if pair:
        for hh in range(nh):
            o_ref[hh, :, toks] = out[:, hh * tq:(hh + 1) * tq]
    else:
        o_ref[0, :, pl.ds(qi * R if static else pl.multiple_of(qi * R, R), R)] = out
    return jnp.min(acc[HEAD_DIM:HEAD_DIM + 1, :])


def _flash_kernel(*refs, tq, tk, nh, pair, window, has_bias):
    q_ref = refs[0]
    bias_ref = refs[1] if has_bias else None
    (k_ref, kf_ref, vt_ref, mask_ref, o_ref, ka_ref, vta_ref, kn_ref, qa_ref, bnd_ref, m_ref, acc_ref,
     alpha_ref, p_ref, low_ref) = refs[2 if has_bias else 1:]
    S = k_ref.shape[0]
    for kb in range(ka_ref.shape[0]):
        k = k_ref[:, kb * LANES:(kb + 1) * LANES]
        ka_ref[kb, :, 0:LANES] = k
        ka_ref[kb, :, LANES:2 * LANES] = kf_ref[...]
        k2 = jnp.square(k.astype(F32))
        lane = lax.broadcasted_iota(jnp.int32, k2.shape, 1)
        for hf in range(2):
            n2 = jnp.sum(jnp.where(lane // HEAD_DIM == hf, k2, 0.0), axis=1, keepdims=True)
            kn_ref[2 * kb + hf:2 * kb + hf + 1, :] = jnp.broadcast_to(
                jnp.sqrt(jnp.max(n2, axis=0, keepdims=True)), (1, LANES))
    ones_row = jnp.where(lax.broadcasted_iota(jnp.int32, (VROWS - HEAD_DIM, tk), 0) == 0, 1.0, 0.0)
    for v in range(nh if pair else 1):
        for jt in range(S // tk):
            vta_ref[v, jt, 0:HEAD_DIM, :] = vt_ref[v * HEAD_DIM:(v + 1) * HEAD_DIM, jt * tk:(jt + 1) * tk]
            vta_ref[v, jt, HEAD_DIM:VROWS, :] = ones_row.astype(BF16)

    tile = functools.partial(_flash_q_tile, q_ref=q_ref, bias_ref=bias_ref, mask_ref=mask_ref, o_ref=o_ref,
                             ka_ref=ka_ref, vta_ref=vta_ref, kn_ref=kn_ref, qa_ref=qa_ref, bnd_ref=bnd_ref,
                             m_ref=m_ref, acc_ref=acc_ref, alpha_ref=alpha_ref, p_ref=p_ref,
                             tq=tq, tk=tk, nh=nh, pair=pair, window=window)
    nq = S // tq
    low_ref[0] = tile(0, False)

    def bound_stabilised(qi, carry):
        low_ref[qi] = tile(qi, False, has_past=True)
        return carry
    lax.fori_loop(1, nq, bound_stabilised, 0)

    def redo(qi, carry):
        @pl.when(low_ref[qi] < MIN_WEIGHT_SUM)
        def _running_max():
            tile(qi, True)
        return carry

    lax.fori_loop(0, nq, redo, 0)


def _mask_table(tk, window):
    n_delta = (tk + (window or 0)) // CG
    d = np.arange(n_delta)[:, None, None] * CG + np.arange(CG)[None, None, :] - np.arange(tk)[None, :, None]
    ok = (d >= 0) & ((d < window) if window else True)
    return jnp.asarray(np.where(ok, 0.0, NEG), F32)


def _flash(q_t, bias_t, k, kfeat, v_t, *, B, S, units, tq, nh, pair, window):
    tk = TK
    R = nh * tq
    nq = S // tq
    nv = nh if pair else 1
    nb = nh if pair else 1
    nkb = nh // 2 if pair else 1
    assert tq == tk and S % tk == 0 and tk % CG == 0 and window in (None, tk)
    mask = _mask_table(tk, window)
    kcol = (lambda u: u) if pair else (lambda u: u // 2)
    out_w = tq if pair else R
    out_shape = (B, units * nb, HEAD_DIM, S if pair else S * nh)
    has_bias = bias_t is not None
    in_specs = [pl.BlockSpec((nh * HEAD_DIM, S), lambda b, u: (u, b))]
    if has_bias:
        in_specs.append(pl.BlockSpec((None, nb, FEAT, S), lambda b, u: (b, u, 0, 0)))
    in_specs += [pl.BlockSpec((S, nkb * LANES), lambda b, u: (b, kcol(u))),
                 pl.BlockSpec((S, LANES), lambda b, u: (0, 0)),
                 pl.BlockSpec((nv * HEAD_DIM, S), lambda b, u: (u, b)),
                 pl.BlockSpec(mask.shape, lambda b, u: (0, 0, 0))]
    args = (q_t,) + ((bias_t,) if has_bias else ()) + (k, kfeat, v_t, mask)
    return pl.pallas_call(
        functools.partial(_flash_kernel, tq=tq, tk=tk, nh=nh, pair=pair, window=window, has_bias=has_bias),
        grid=(B, units),
        in_specs=in_specs,
        out_specs=pl.BlockSpec((None, nb, HEAD_DIM, out_w * nq), lambda b, u: (b, u, 0, 0)),
        out_shape=jax.ShapeDtypeStruct(out_shape, BF16),
        scratch_shapes=[pltpu.VMEM((nkb, S, 2 * LANES), BF16),
                        pltpu.VMEM((nv, S // tk, VROWS, tk), BF16),
                        pltpu.VMEM((SUBLANES, LANES), F32),
                        pltpu.VMEM((2 * LANES, R), BF16),
                        pltpu.VMEM((SUBLANES, R), F32),
                        pltpu.VMEM((SUBLANES, R), F32),
                        pltpu.VMEM((VROWS, R), F32),
                        pltpu.VMEM((SUBLANES, R), F32),
                        pltpu.VMEM((tk, R), BF16),
                        pltpu.SMEM((nq,), F32)],
        compiler_params=pltpu.CompilerParams(dimension_semantics=("parallel", "parallel"),
                                             vmem_limit_bytes=VMEM_LIMIT),
        name="flash",
    )(*args)


def _even_out_kernel(x_ref, ocmp_ref, osel_ref, owin_ref, om_ref, gate_ref, nz_ref, mz_ref,
                     w_ref, g_ref, b_ref, o_ref, sig_ref, *, alpha, tm):
    G, nh = NSA_KV_GROUPS, NSA_HG
    sig_ref[...] = _sigmoid(gate_ref[...])
    heads = []
    for g in range(G):
        for hh in range(nh):
            acc = None
            for br, ref in enumerate((ocmp_ref, osel_ref, owin_ref)):
                gate = sig_ref[pl.ds(3 * (g * nh + hh) + br, 1), :]
                term = gate * ref[g, :, hh * tm:(hh + 1) * tm].astype(F32)
                acc = term if acc is None else acc + term
            heads.append(acc)
    o_nsa = jnp.concatenate(heads, axis=0)
    nz = nz_ref[...].astype(F32)
    mz = mz_ref[...].astype(F32)
    o_m = jnp.concatenate([om_ref[h] for h in range(MOBA_HEADS)], axis=0).astype(F32)
    a_t = jnp.concatenate([o_nsa * (nz * _sigmoid(nz)), o_m * (mz * _sigmoid(mz))], axis=0)
    y = _dot_tn(a_t.astype(BF16), w_ref[...])
    o_ref[...] = _layer_norm(alpha * x_ref[...] + y, g_ref[...], b_ref[...])


def _even_out(x2d, ocmp, osel, owin, om, gate_t, nz_t, mz_t, w_out, ln_g, ln_b, alpha, *, B, S):
    T, D = x2d.shape
    tm = TQ
    nq = S // tm
    G, nh, H = NSA_KV_GROUPS, NSA_HG, MOBA_HEADS
    nsa = pl.BlockSpec((None, G, HEAD_DIM, nh * tm), lambda i: (i // nq, 0, 0, i % nq))
    feat = lambda n: pl.BlockSpec((n, tm), lambda i: (0, i))
    full = lambda a: pl.BlockSpec(a.shape, lambda i: (0,) * a.ndim)
    return pl.pallas_call(
        functools.partial(_even_out_kernel, alpha=alpha, tm=tm),
        grid=(T // tm,),
        in_specs=[pl.BlockSpec((tm, D), lambda i: (i, 0)), nsa, nsa, nsa,
                  pl.BlockSpec((None, H, HEAD_DIM, tm), lambda i: (i // nq, 0, 0, i % nq)),
                  feat(GATE_ROWS), feat(NSA_W), feat(MOBA_W),
                  full(w_out), full(ln_g), full(ln_b)],
        out_specs=pl.BlockSpec((tm, D), lambda i: (i, 0)),
        out_shape=jax.ShapeDtypeStruct((T, D), F32),
        scratch_shapes=[pltpu.VMEM((GATE_ROWS, tm), F32)],
        compiler_params=pltpu.CompilerParams(dimension_semantics=("parallel",),
                                             vmem_limit_bytes=VMEM_LIMIT),
        name="even_out",
    )(x2d, ocmp, osel, owin, om, gate_t, nz_t, mz_t, w_out, ln_g, ln_b)


def _gla_head_exact(row0, h, q_ref, v_ref, o_ref, st_ref, b_ref, kf_ref, scale):
    C, dk, dv, SUB = GLA_CHUNK, GLA_DK, GLA_DV, GLA_SUB
    rows = lax.broadcasted_iota(jnp.int32, (C, dk), 0)
    sub_i = lax.broadcasted_iota(jnp.int32, (SUB, LANES), 0)
    lane = lax.broadcasted_iota(jnp.int32, (SUB, LANES), 1)
    ones = jnp.ones((dk, LANES), BF16)
    sl = pl.ds(h * dk, dk)
    rs_c = pl.ds(row0, C)
    bh = b_ref[:, sl]
    kf = kf_ref[:, sl]
    qf = q_ref[rs_c, sl].astype(F32) * scale
    vh = v_ref[rs_c, pl.ds(h * dv, dv)]
    st = st_ref[h]
    inter = _dot_nt((qf * jnp.exp(bh)).astype(BF16), st.astype(BF16))
    bl = b_ref[pl.ds(C - 1, 1), sl]
    kd = kf * jnp.exp(bl - bh)

    def off_block(r0, r1, c0, c1):
        ref_row = b_ref[pl.ds(r0, 1), sl]
        qt = qf[r0:r1] * jnp.exp(bh[r0:r1] - ref_row)
        ek = jnp.where((rows >= c0) & (rows < c1), ref_row - bh, NEG)
        kt = kf * jnp.exp(ek)
        return _dot_nt(qt.astype(BF16), kt.astype(BF16))

    lower = off_block(2 * SUB, C, 0, 2 * SUB)
    blocks = [None, off_block(SUB, 2 * SUB, 0, SUB), lower[0:SUB],
              lower[SUB:2 * SUB] + off_block(3 * SUB, C, 2 * SUB, 3 * SUB)]
    pieces = []
    for blk in range(C // SUB):
        r0 = blk * SUB
        q_i = qf[r0:r0 + SUB]
        b_i = bh[r0:r0 + SUB]
        for jj in range(SUB):
            b_j = b_ref[pl.ds(r0 + jj, 1), sl]
            k_j = kf_ref[pl.ds(r0 + jj, 1), sl]
            pieces.append((q_i * k_j * jnp.exp(jnp.minimum(b_i - b_j, 0.0))).astype(BF16))
    sums = _dot(jnp.concatenate(pieces, axis=0), ones)
    for blk in range(C // SUB):
        r0 = blk * SUB
        diag = jnp.zeros((SUB, LANES), F32)
        for jj in range(SUB):
            rs = sums[(r0 + jj) * SUB:(r0 + jj + 1) * SUB]
            diag = jnp.where((lane == r0 + jj) & (sub_i >= jj), rs, diag)
        diag = diag[:, 0:C]
        blocks[blk] = diag if blocks[blk] is None else blocks[blk] + diag
    att = jnp.concatenate(blocks, axis=0)
    o_ref[rs_c, pl.ds(h * dv, dv)] = (inter + _dot(att.astype(BF16), vh)).astype(o_ref.dtype)
    st_ref[h] = st * jnp.exp(bl) + _dot_tn(vh, kd.astype(BF16))


def _gla_head_bounded(row0, h, q_ref, v_ref, o_ref, st_ref, b_ref, kf_ref, scale):
    C, dk, dv = GLA_CHUNK, GLA_DK, GLA_DV
    sl = pl.ds(h * dk, dk)
    rs_c = pl.ds(row0, C)
    bh = b_ref[rs_c, sl]
    kf = kf_ref[rs_c, sl]
    qf = q_ref[rs_c, sl].astype(F32) * scale
    vh = v_ref[rs_c, pl.ds(h * dv, dv)]
    st = st_ref[h]
    qs = (qf * jnp.exp(bh)).astype(BF16)
    kt = (kf * jnp.exp(-bh)).astype(BF16)
    i = lax.broadcasted_iota(jnp.int32, (C, C), 0)
    j = lax.broadcasted_iota(jnp.int32, (C, C), 1)
    att = jnp.where(i >= j, _dot_nt(qs, kt), 0.0)
    inter = _dot_nt(qs, st.astype(BF16))
    o_ref[rs_c, pl.ds(h * dv, dv)] = (inter + _dot(att.astype(BF16), vh)).astype(o_ref.dtype)
    bl = b_ref[pl.ds(row0 + C - 1, 1), sl]
    kd = kf * jnp.exp(bl - bh)
    st_ref[h] = st * jnp.exp(bl) + _dot_tn(vh, kd.astype(BF16))


def _gla_kernel(q_ref, k_ref, v_ref, glr_ref, w2_ref, gb_ref, tril_ref, o_ref,
                st_ref, b_ref, kf_ref, bc_ref, kc_ref, *, scale):
    C, H = GLA_CHUNK, GLA_HEADS
    nch = q_ref.shape[0] // C

    @pl.when(pl.program_id(1) == 0)
    def _init():
        st_ref[...] = jnp.zeros(st_ref.shape, F32)

    whi, wlo = _hilo(w2_ref[...])
    part = tril_ref.shape[0]
    decay = None
    for r0 in range(0, nch * C, part):
        rs = pl.ds(r0, part)
        ghi, glo = _hilo(glr_ref[rs, :])
        pre = _dot(ghi, whi) + _dot(glo, whi) + _dot(ghi, wlo) + gb_ref[...]
        la = (jnp.minimum(pre, 0.0) - jnp.log(1.0 + jnp.exp(-jnp.abs(pre)))) * (1.0 / GLA_TAU)
        lhi, llo = _hilo(la)
        b_part = _dot(tril_ref[...], lhi) + _dot(tril_ref[...], llo)
        b_ref[rs, :] = b_part
        for c in range(part // C):
            d_c = -b_part[c * C + C - 1:(c + 1) * C, :]
            decay = d_c if decay is None else jnp.maximum(decay, d_c)
    kf_ref[...] = k_ref[...].astype(F32)
    bounded = jnp.max(decay) <= GLA_FACTOR_MAX_DECAY

    @pl.when(bounded)
    def _factored():
        for c in range(nch):
            for h in range(H):
                _gla_head_bounded(c * C, h, q_ref, v_ref, o_ref, st_ref, b_ref, kf_ref, scale)

    @pl.when(jnp.logical_not(bounded))
    def _exact():
        def chunk(c, carry):
            row0 = pl.multiple_of(c * C, C)
            bc_ref[...] = b_ref[pl.ds(row0, C), :]
            kc_ref[...] = kf_ref[pl.ds(row0, C), :]
            for h in range(H):
                _gla_head_exact(row0, h, q_ref, v_ref, o_ref, st_ref, bc_ref, kc_ref, scale)
            return carry
        lax.fori_loop(0, nch, chunk, 0)


def _gla(q, k, v, glr, w2p, gb, tril, *, B, S):
    C, H, dk, dv = GLA_CHUNK, GLA_HEADS, GLA_DK, GLA_DV
    rows = C * min(GLA_STEP_CHUNKS, S // C)
    nchunk = S // rows
    tok = lambda n: pl.BlockSpec((rows, n), lambda b, c: (b * nchunk + c, 0))
    const = lambda shape: pl.BlockSpec(shape, lambda b, c: (0,) * len(shape))
    return pl.pallas_call(
        functools.partial(_gla_kernel, scale=float(dk) ** -0.5),
        grid=(B, nchunk),
        in_specs=[tok(H * dk), tok(H * dk), tok(H * dv), tok(LANES),
                  const(w2p.shape), const(gb.shape), const(tril.shape)],
        out_specs=tok(H * dv),
        out_shape=jax.ShapeDtypeStruct((B * S, H * dv), BF16),
        scratch_shapes=[pltpu.VMEM((H, dv, dk), F32),
                        pltpu.VMEM((rows, H * dk), F32),
                        pltpu.VMEM((rows, H * dk), F32),
                        pltpu.VMEM((C, H * dk), F32),
                        pltpu.VMEM((C, H * dk), F32)],
        compiler_params=pltpu.CompilerParams(dimension_semantics=("parallel", "arbitrary"),
                                             vmem_limit_bytes=VMEM_LIMIT),
        name="gla",
    )(q, k, v, glr, w2p, gb, tril)


def _odd_out_kernel(x_ref, o_ref_in, z_ref, gn_ref, w_ref, g_ref, b_ref, out_ref, *, alpha):
    H, dv = GLA_HEADS, GLA_DV
    parts = []
    for h in range(H):
        sl = pl.ds(h * dv, dv)
        oh = o_ref_in[:, sl].astype(F32)
        r = lax.rsqrt(jnp.mean(oh * oh, axis=-1, keepdims=True) + LN_EPS)
        z = z_ref[:, sl].astype(F32)
        parts.append((oh * r * gn_ref[:, sl] * (z * _sigmoid(z))).astype(BF16))
    y = _dot(jnp.concatenate(parts, axis=1), w_ref[...])
    out_ref[...] = _layer_norm(alpha * x_ref[...] + y, g_ref[...], b_ref[...])


def _odd_out(x2d, o, z, gn_g, w_out, ln_g, ln_b, alpha, tm=1024):
    T, D = x2d.shape
    W = o.shape[1]
    row = lambda n: pl.BlockSpec((tm, n), lambda i: (i, 0))
    return pl.pallas_call(
        functools.partial(_odd_out_kernel, alpha=alpha),
        grid=(T // tm,),
        in_specs=[row(D), row(W), row(W),
                  pl.BlockSpec((1, W), lambda i: (0, 0)),
                  pl.BlockSpec((W, D), lambda i: (0, 0)),
                  pl.BlockSpec((1, D), lambda i: (0, 0)),
                  pl.BlockSpec((1, D), lambda i: (0, 0))],
        out_specs=row(D),
        out_shape=jax.ShapeDtypeStruct((T, D), F32),
        compiler_params=pltpu.CompilerParams(dimension_semantics=("parallel",),
                                             vmem_limit_bytes=VMEM_LIMIT),
        name="odd_out",
    )(x2d, o, z, gn_g, w_out, ln_g, ln_b)


PACK = 2 * SUBLANES


def _pack_rows_kernel(w_ref, *o_refs, plans):
    for o_ref, plan in zip(o_refs, plans):
        off = 0
        for src, n, scale, n_out in plan:
            n_load = min(-(-n // PACK) * PACK, n_out)
            rows = w_ref[src:src + n_load, :]
            if scale != 1.0:
                rows = rows * scale
            if n_load != n:
                rows = jnp.where(lax.broadcasted_iota(jnp.int32, rows.shape, 0) < n, rows, 0.0)
            o_ref[off:off + n_load, :] = rows.astype(BF16)
            if n_out > n_load:
                o_ref[off + n_load:off + n_out, :] = jnp.zeros((n_out - n_load, o_ref.shape[1]), BF16)
            off += n_out


def _pack_rows(w_t, plans, tc=256):
    W, D = w_t.shape
    plans = tuple(tuple((p + (p[1],))[:4] for p in plan) for plan in plans)
    for plan in plans:
        for src, n, _, n_out in plan:
            assert src % SUBLANES == 0 and n_out % PACK == 0 and n <= n_out and src + n_out <= W
    assert D % tc == 0
    rows = [sum(p[3] for p in plan) for plan in plans]
    return pl.pallas_call(
        functools.partial(_pack_rows_kernel, plans=plans),
        grid=(D // tc,),
        in_specs=[pl.BlockSpec((W, tc), lambda j: (0, j))],
        out_specs=[pl.BlockSpec((r, tc), lambda j: (0, j)) for r in rows],
        out_shape=[jax.ShapeDtypeStruct((r, D), BF16) for r in rows],
        compiler_params=pltpu.CompilerParams(dimension_semantics=("parallel",),
                                             vmem_limit_bytes=VMEM_LIMIT),
        name="pack_rows",
    )(w_t)


def _key_features(S, block):
    s = np.arange(S)
    f = np.zeros((S, LANES), np.float32)
    f[s, s // block] = 1.0
    f[:, FEAT + 0] = 1.0
    f[:, FEAT + 1] = 1.0
    f[:, FEAT + 2] = FEAT * (s // FEAT)
    f[:, FEAT + 3] = s % FEAT
    return jnp.asarray(f, BF16)


def _pad_to(a, n, axis):
    pad = [(0, 0)] * a.ndim
    pad[axis] = (0, n - a.shape[axis])
    return jnp.pad(a, pad)


def _even_layer(x2d, B, S, w_in, cmp_pos, cmp_w1, cmp_b1, cmp_w2, w_out, ln_g, ln_b, alpha):
    T, D = x2d.shape
    G, hd = NSA_KV_GROUPS, HEAD_DIM
    scale = hd ** -0.5
    kvw, (oq, okv, og, onz, omq, omk, omv, omz) = NSA_KV_W, np.cumsum((0,) + EVEN_SPLITS[:-1]).tolist()
    okc, ovc, oks, ovs, okw, ovw = (okv + i * kvw for i in range(6))
    w_tok, w_tr = _pack_rows(w_in.T, (
        ((okc, kvw, 1.0), (ovc, kvw, 1.0), (oks, kvw, 1.0), (okw, kvw, 1.0), (omk, MOBA_W, 1.0)),
        ((oq, NSA_W, scale), (ovs, kvw, 1.0), (ovw, kvw, 1.0), (omq, MOBA_W, scale), (omv, MOBA_W, 1.0),
         (onz, NSA_W, 1.0), (omz, MOBA_W, 1.0), (og, 3 * NSA_HEADS, 1.0, GATE_ROWS))))
    (kcmp, vcmp, ksel, kwin, mk, nq_t, vsel_t, vwin_t, mq_t, mv_t, nz_t, mz_t, gate_t) = _proj(
        x2d, w_tok, (NSA_KV_W,) * 4 + (MOBA_W,), (F32, F32, BF16, BF16, BF16),
        w_tr, (NSA_W, NSA_KV_W, NSA_KV_W, MOBA_W, MOBA_W, NSA_W, MOBA_W, GATE_ROWS),
        (BF16,) * 7 + (F32,))

    L, d, HID = NSA_CMP_BLOCK, NSA_CMP_STRIDE, NSA_CMP_HIDDEN
    ncmp = (S - L) // d + 1
    NC = S // d
    def group_diagonal(w, axis):
        z = jnp.zeros_like(w)
        return jnp.stack([jnp.concatenate([w if h == g else z for h in range(G)], axis=-1) for g in range(G)], axis=axis)

    w1r = cmp_w1.reshape(2, 2, d, hd, HID).astype(BF16)
    w1p = group_diagonal(w1r, 3).reshape(2, 2, d * G * hd, G * HID)
    posp = jnp.broadcast_to(cmp_pos.reshape(2, 2, d, 1, hd), (2, 2, d, G, hd)).reshape(2, 2, 1, d * G * hd)
    b1p = jnp.tile(cmp_b1, (1, G)).reshape(2, 1, G * HID)
    w2p = group_diagonal(_pad_to(cmp_w2, LANES, 2).astype(BF16), 1).reshape(2, G * HID, G * LANES)
    cfeat = np.zeros((NC, G, LANES), np.float32)
    cfeat[:, :, hd + 0] = 1.0
    cfeat[:, :, hd + 1] = 1.0
    cfeat[:, :, hd + 2] = (np.arange(NC) * d)[:, None]
    cfeat[:, :, hd + 3] = L - 1
    kvc = _cmp_mlp(kcmp, vcmp, w1p, posp, b1p, w2p, jnp.asarray(cfeat.reshape(NC, G * LANES)), ncmp,
                   B=B, S=S)

    ns = S // NSA_SEL_BLOCK
    cmp_start = np.arange(NC) * d
    sel_start = np.arange(FEAT) * NSA_SEL_BLOCK
    ovt = ((cmp_start[None, :] <= sel_start[:, None] + NSA_SEL_BLOCK - 1)
           & (cmp_start[None, :] + L - 1 >= sel_start[:, None])
           & (np.arange(NC)[None, :] < ncmp) & (np.arange(FEAT)[:, None] < ns))
    o_cmp, bias_sel = _cmp_attn(nq_t, kvc, jnp.asarray(ovt, BF16), B=B, S=S)

    feats_sel = _key_features(S, NSA_SEL_BLOCK)
    o_sel = _flash(nq_t, bias_sel, ksel, feats_sel, vsel_t,
                   B=B, S=S, units=G, tq=TQ, nh=NSA_HG, pair=False, window=None)
    o_win = _flash(nq_t, None, kwin, feats_sel, vwin_t,
                   B=B, S=S, units=G, tq=TQ, nh=NSA_HG, pair=False, window=NSA_WINDOW)

    avg = np.zeros((MOBA_NBP, S), np.float32)
    avg[np.arange(S) // MOBA_BLOCK, np.arange(S)] = 1.0 / MOBA_BLOCK
    bias_m = _moba_gate(mq_t, mk, jnp.asarray(avg, BF16), B=B, S=S)
    o_m = _flash(mq_t, bias_m, mk, _key_features(S, MOBA_BLOCK), mv_t,
                 B=B, S=S, units=MOBA_HEADS // MOBA_STEP_HEADS, tq=TQM, nh=MOBA_STEP_HEADS, pair=True, window=None)

    return _even_out(x2d, o_cmp, o_sel, o_win, o_m, gate_t, nz_t, mz_t, w_out.astype(BF16),
                     ln_g.reshape(1, D), ln_b.reshape(1, D), alpha, B=B, S=S)


def _odd_layer(x2d, B, S, w_in, gate_w2, gate_b, gn_g, w_out, ln_g, ln_b, alpha):
    T, D = x2d.shape
    H, dk, dv, C = GLA_HEADS, GLA_DK, GLA_DV, GLA_CHUNK
    n_qkv, n_gate, n_z = sum(ODD_SPLITS[:3]), ODD_SPLITS[3], ODD_SPLITS[4]
    w_perm, = _pack_rows(w_in.T, (((0, n_qkv, 1.0), (n_qkv + n_gate, n_z, 1.0), (n_qkv, n_gate, 1.0, LANES)),))
    q, k, v, z, glr = _proj(x2d, w_perm, (H * dk, H * dk, H * dv, H * dv, LANES),
                            (BF16, BF16, BF16, BF16, F32))
    w2p = _pad_to(gate_w2, LANES, 0)
    ncs = min(GLA_CUMSUM_CHUNKS, GLA_STEP_CHUNKS, S // C)
    tril = jnp.asarray(np.kron(np.eye(ncs), np.tril(np.ones((C, C)))), BF16)
    o = _gla(q, k, v, glr, w2p, gate_b.reshape(1, H * dk), tril, B=B, S=S)
    return _odd_out(x2d, o, z, gn_g.reshape(1, H * dv), w_out.astype(BF16),
                    ln_g.reshape(1, D), ln_b.reshape(1, D), alpha)


def kernel(x, ev_w_in, ev_cmp_pos, ev_cmp_w1, ev_cmp_b1, ev_cmp_w2, ev_w_out, ev_ln_g, ev_ln_b,
           od_w_in, od_gate_w2, od_gate_b, od_gn_g, od_w_out, od_ln_g, od_ln_b):
    B, S, D = x.shape
    depth = ev_w_in.shape[0] + od_w_in.shape[0]
    alpha = (2.0 * depth) ** 0.25
    h = x.reshape(B * S, D)
    for layer in range(depth):
        i = layer // 2
        if layer % 2 == 0:
            h = _even_layer(h, B, S, ev_w_in[i], ev_cmp_pos[i], ev_cmp_w1[i], ev_cmp_b1[i], ev_cmp_w2[i],
                            ev_w_out[i], ev_ln_g[i], ev_ln_b[i], alpha)
        else:
            h = _odd_layer(h, B, S, od_w_in[i], od_gate_w2[i], od_gate_b[i], od_gn_g[i],
                           od_w_out[i], od_ln_g[i], od_ln_b[i], alpha)
    return h.reshape(B, S, D)
```

```python
import functools

import numpy as np
import jax
import jax.numpy as jnp
from jax import lax
from jax.experimental import pallas as pl
from jax.experimental.pallas import tpu as pltpu

F32 = jnp.float32
BF16 = jnp.bfloat16

HEAD_DIM = 64
NSA_HEADS = 8
NSA_KV_GROUPS = 2
NSA_HG = NSA_HEADS // NSA_KV_GROUPS
NSA_CMP_BLOCK = 32
NSA_CMP_STRIDE = 16
NSA_CMP_HIDDEN = 128
NSA_SEL_BLOCK = 64
NSA_TOP_N = 16
NSA_WINDOW = 512
MOBA_HEADS = 8
MOBA_BLOCK = 256
MOBA_TOP_K = 3
GLA_HEADS = 4
GLA_DK = 128
GLA_DV = 256
GLA_GATE_RANK = 16
GLA_TAU = 16.0
GLA_CHUNK = 64
GLA_SUB = 16
GLA_STEP_CHUNKS = 16
GLA_CUMSUM_CHUNKS = 4
GLA_FACTOR_MAX_DECAY = 60.0
LN_EPS = 1e-5
NEG = -1e30
FORCE_BONUS = 1e4

NSA_W = NSA_HEADS * HEAD_DIM
NSA_KV_W = NSA_KV_GROUPS * HEAD_DIM
MOBA_W = MOBA_HEADS * HEAD_DIM
EVEN_SPLITS = (NSA_W, 6 * NSA_KV_W, 3 * NSA_HEADS, NSA_W, MOBA_W, MOBA_W, MOBA_W, MOBA_W)
ODD_SPLITS = (GLA_HEADS * GLA_DK, GLA_HEADS * GLA_DK, GLA_HEADS * GLA_DV, GLA_GATE_RANK, GLA_HEADS * GLA_DV)

LANES = 128
SUBLANES = 8
FEAT = 64
ALIBI_ROWS = 16
GATE_ROWS = 32
MOBA_NBP = 16
TQ = 512
TQM = 512
MOBA_STEP_HEADS = 4
TK = 512
CG = 256
MIN_WEIGHT_SUM = 1e-25
VROWS = HEAD_DIM + 16
VMEM_LIMIT = 48 * 1024 * 1024


def _dot(a, b):
    return jnp.dot(a, b, preferred_element_type=F32)


def _dot_nt(a, b):
    return lax.dot_general(a, b, (((1,), (1,)), ((), ())), preferred_element_type=F32)


def _dot_tn(a, b):
    return lax.dot_general(a, b, (((0,), (0,)), ((), ())), preferred_element_type=F32)


def _hilo(a):
    hi = a.astype(BF16)
    lo = (a - hi.astype(F32)).astype(BF16)
    return hi, lo


def _pow2_neg(n):
    return lax.bitcast_convert_type(lax.shift_left(127 - n, 23), F32)


def _sigmoid(x):
    return 0.5 * jnp.tanh(0.5 * x) + 0.5


def _layer_norm(r, g, b):
    mu = jnp.mean(r, axis=-1, keepdims=True)
    d = r - mu
    var = jnp.mean(d * d, axis=-1, keepdims=True)
    return d * lax.rsqrt(var + LN_EPS) * g + b


def _alibi_query_rows(t0, tq, head0, R):
    r = lax.broadcasted_iota(jnp.int32, (ALIBI_ROWS, R), 0)
    col = lax.broadcasted_iota(jnp.int32, (ALIBI_ROWS, R), 1)
    t = t0 + col % tq
    slope = _pow2_neg(head0 + col // tq + 1)
    feats = jnp.where(r == 0, -FEAT * slope * (t // FEAT).astype(F32),
                      jnp.where(r == 1, -slope * (t % FEAT).astype(F32), jnp.where(r < 4, slope, 0.0)))
    return jnp.concatenate([feats.astype(BF16), jnp.zeros((FEAT - ALIBI_ROWS, R), BF16)], axis=0)


def _ranks(sc_ref, n_real, n_rows=None, cols=slice(None)):
    n_rows = sc_ref.shape[0] if n_rows is None else n_rows
    out = []
    for v in range(n_rows // SUBLANES):
        lo = v * SUBLANES
        sc = sc_ref[lo:lo + SUBLANES, cols]
        width = sc.shape[1]
        idx = lo + lax.broadcasted_iota(jnp.int32, (SUBLANES, width), 0)
        rank = jnp.zeros((SUBLANES, width), jnp.int32)
        for jp in range(n_real):
            r = sc_ref[pl.ds(jp, 1), cols]
            if jp < lo:
                beats = r >= sc
            elif jp >= lo + SUBLANES - 1:
                beats = r > sc
            else:
                beats = (r > sc) | ((r == sc) & (idx > jp))
            rank = rank + jnp.where(beats, 1, 0)
        out.append(rank)
    return jnp.concatenate(out, axis=0)


def _proj_kernel(*refs, tok_splits, tr_splits):
    x_ref, wt_ref = refs[0], refs[1]
    outs = list(refs[3:] if tr_splits else refs[2:])
    x = x_ref[...]
    xb = x.astype(BF16)
    off = 0
    for n in tok_splits:
        o_ref = outs.pop(0)
        o_ref[...] = _dot_nt(xb, wt_ref[off:off + n, :]).astype(o_ref.dtype)
        off += n
    if tr_splits:
        wtr_ref = refs[2]
        xt = x.T.astype(BF16)
        off = 0
        for n in tr_splits:
            o_ref = outs.pop(0)
            o_ref[...] = _dot(wtr_ref[off:off + n, :], xt).astype(o_ref.dtype)
            off += n


def _proj(x2d, w_tok, tok_splits, tok_dtypes, w_tr=None, tr_splits=(), tr_dtypes=(), tm=1024):
    T, D = x2d.shape
    assert T % tm == 0 and sum(tok_splits) == w_tok.shape[0]
    in_specs = [pl.BlockSpec((tm, D), lambda i: (i, 0)),
                pl.BlockSpec(w_tok.shape, lambda i: (0, 0))]
    args = [x2d, w_tok]
    if tr_splits:
        assert sum(tr_splits) == w_tr.shape[0]
        in_specs.append(pl.BlockSpec(w_tr.shape, lambda i: (0, 0)))
        args.append(w_tr)
    out_specs = ([pl.BlockSpec((tm, n), lambda i: (i, 0)) for n in tok_splits]
                 + [pl.BlockSpec((n, tm), lambda i: (0, i)) for n in tr_splits])
    out_shape = ([jax.ShapeDtypeStruct((T, n), dt) for n, dt in zip(tok_splits, tok_dtypes)]
                 + [jax.ShapeDtypeStruct((n, T), dt) for n, dt in zip(tr_splits, tr_dtypes)])
    return pl.pallas_call(
        functools.partial(_proj_kernel, tok_splits=tuple(tok_splits), tr_splits=tuple(tr_splits)),
        grid=(T // tm,), in_specs=in_specs, out_specs=out_specs, out_shape=out_shape,
        compiler_params=pltpu.CompilerParams(dimension_semantics=("parallel",),
                                             vmem_limit_bytes=VMEM_LIMIT),
        name="proj",
    )(*args)


def _cmp_mlp_kernel(kx_ref, vx_ref, w1_ref, pos_ref, b1_ref, w2_ref, cf_ref, o_ref, *, ncmp):
    d = NSA_CMP_STRIDE
    nc = kx_ref.shape[0] // d
    cw = w1_ref.shape[2]
    for i, x_ref in enumerate((kx_ref, vx_ref)):
        a = bm = None
        for l in range(0, d, 2):
            x2 = jnp.concatenate([x_ref[pl.ds(l, nc, stride=d), :], x_ref[pl.ds(l + 1, nc, stride=d), :]],
                                 axis=1).astype(BF16)
            rows = pl.ds(l * x_ref.shape[1], 2 * x_ref.shape[1])
            a_l, b_l = _dot(x2, w1_ref[i, 0, rows, :]), _dot(x2, w1_ref[i, 1, rows, :])
            a, bm = (a_l, b_l) if a is None else (a + a_l, bm + b_l)
        c = b1_ref[i]
        for half in range(2):
            phi, plo = _hilo(jnp.broadcast_to(pos_ref[i, half], (SUBLANES, cw)))
            c = c + (_dot(phi, w1_ref[i, half]) + _dot(plo, w1_ref[i, half]))[0:1]
        hid = a + pltpu.roll(bm, nc - 1, axis=0) + c
        hid = hid * _sigmoid(hid)
        row = lax.broadcasted_iota(jnp.int32, hid.shape, 0)
        hid = jnp.where(row < ncmp, hid, 0.0)
        out = _dot(hid.astype(BF16), w2_ref[i])
        if i == 0:
            out = out + cf_ref[...]
        o_ref[i] = out.astype(o_ref.dtype)


def _cmp_mlp(kx, vx, w1p, posp, b1p, w2p, cfeat, ncmp, *, B, S):
    NC = S // NSA_CMP_STRIDE
    OW = w2p.shape[-1]
    full = lambda a: pl.BlockSpec(a.shape, lambda b: (0,) * a.ndim)
    return pl.pallas_call(
        functools.partial(_cmp_mlp_kernel, ncmp=ncmp),
        grid=(B,),
        in_specs=[pl.BlockSpec((S, kx.shape[1]), lambda b: (b, 0)),
                  pl.BlockSpec((S, vx.shape[1]), lambda b: (b, 0)),
                  full(w1p), full(posp), full(b1p), full(w2p), full(cfeat)],
        out_specs=pl.BlockSpec((2, None, NC, OW), lambda b: (0, b, 0, 0)),
        out_shape=jax.ShapeDtypeStruct((2, B, NC, OW), BF16),
        compiler_params=pltpu.CompilerParams(dimension_semantics=("parallel",),
                                             vmem_limit_bytes=VMEM_LIMIT),
        name="cmp_mlp",
    )(kx, vx, w1p, posp, b1p, w2p, cfeat)


def _cmp_attn_kernel(q_ref, kc_ref, vc_ref, ovt_ref, o_ref, bias_ref, sc_ref, *, tq, nh, ns, ntop):
    g = pl.program_id(1)
    R = nh * tq
    nq = q_ref.shape[1] // tq

    def q_tile(qi):
        toks = pl.ds(qi * tq, tq)
        nc = min(kc_ref.shape[0], -(-((qi + 1) * tq) // (NSA_CMP_STRIDE * 16)) * 16)
        nsel = min(ns, -(-((qi + 1) * tq) // (NSA_SEL_BLOCK * SUBLANES)) * SUBLANES)
        qcat = jnp.concatenate([q_ref[hh * HEAD_DIM:(hh + 1) * HEAD_DIM, toks] for hh in range(nh)], axis=1)
        qa = jnp.concatenate([qcat, _alibi_query_rows(qi * tq, tq, g * nh, R)], axis=0)
        s = _dot(kc_ref[0:nc, :], qa)
        n = lax.broadcasted_iota(jnp.int32, (nc, R), 0)
        t_row = qi * tq + lax.broadcasted_iota(jnp.int32, (1, R), 1) % tq
        last = (t_row - (NSA_CMP_BLOCK - 1)) // NSA_CMP_STRIDE
        s = jnp.where(n <= last, s, NEG)
        e = jnp.exp(s - jnp.max(s, axis=0, keepdims=True))
        inv = jnp.where(last >= 0, 1.0 / jnp.sum(e, axis=0, keepdims=True), 0.0)
        p = e * inv
        o_ref[:, pl.ds(qi * R, R)] = _dot_tn(vc_ref[0:nc, 0:HEAD_DIM], p.astype(BF16)).astype(o_ref.dtype)

        psum = p[:, 0:tq]
        for hh in range(1, nh):
            psum = psum + p[:, hh * tq:(hh + 1) * tq]
        phi, plo = _hilo(psum)
        ovt = ovt_ref[0:nsel, 0:nc]
        imp = _dot(ovt, phi) + _dot(ovt, plo)

        j = lax.broadcasted_iota(jnp.int32, (nsel, tq), 0)
        cur = (qi * tq + lax.broadcasted_iota(jnp.int32, (nsel, tq), 1)) // NSA_SEL_BLOCK
        if nsel <= ntop:
            sel = j <= cur
        else:
            forced = jnp.where((j == 0) | (j == cur) | (j == cur - 1), FORCE_BONUS, 0.0)
            sc_ref[0:nsel, :] = jnp.where(j <= cur, imp + forced, NEG)
            sel = (_ranks(sc_ref, nsel, nsel) < ntop) & (j <= cur)
        bias_ref[0:nsel, toks] = jnp.where(sel, 0.0, NEG).astype(bias_ref.dtype)
        if nsel < FEAT:
            bias_ref[nsel:FEAT, toks] = jnp.full((FEAT - nsel, tq), NEG, bias_ref.dtype)

    for qi in range(nq):
        q_tile(qi)


def _cmp_attn(nq_t, kvc, ovt, *, B, S):
    tq, nh, G = TQ, NSA_HG, NSA_KV_GROUPS
    R = nh * tq
    nq = S // tq
    NC = kvc.shape[2]
    ns = S // NSA_SEL_BLOCK
    assert ns <= FEAT
    ntop = min(NSA_TOP_N, ns)
    return pl.pallas_call(
        functools.partial(_cmp_attn_kernel, tq=tq, nh=nh, ns=ns, ntop=ntop),
        grid=(B, G),
        in_specs=[pl.BlockSpec((nh * HEAD_DIM, S), lambda b, g: (g, b)),
                  pl.BlockSpec((None, None, NC, LANES), lambda b, g: (0, b, 0, g)),
                  pl.BlockSpec((None, None, NC, LANES), lambda b, g: (1, b, 0, g)),
                  pl.BlockSpec(ovt.shape, lambda b, g: (0, 0))],
        out_specs=[pl.BlockSpec((None, None, HEAD_DIM, S * nh), lambda b, g: (b, g, 0, 0)),
                   pl.BlockSpec((None, None, FEAT, S), lambda b, g: (b, g, 0, 0))],
        out_shape=[jax.ShapeDtypeStruct((B, G, HEAD_DIM, S * nh), BF16),
                   jax.ShapeDtypeStruct((B, G, FEAT, S), BF16)],
        scratch_shapes=[pltpu.VMEM((FEAT, tq), F32)],
        compiler_params=pltpu.CompilerParams(dimension_semantics=("parallel", "parallel"),
                                             vmem_limit_bytes=VMEM_LIMIT),
        name="cmp_attn",
    )(nq_t, kvc, kvc, ovt)


def _moba_gate_kernel(q_ref, k_ref, a_ref, bias_ref, sc_ref, *, nb, ntop):
    nbp, S = sc_ref.shape[1:]
    km = _dot(a_ref[...], k_ref[...])
    lane = lax.broadcasted_iota(jnp.int32, (nbp, LANES), 1)
    parts = []
    for hh in range(2):
        parts += _hilo(jnp.where(lane // HEAD_DIM == hh, km, 0.0))
    g = _dot(jnp.concatenate(parts, axis=0), q_ref[...])
    past = (lax.broadcasted_iota(jnp.int32, (nbp, S), 0)
            < lax.broadcasted_iota(jnp.int32, (nbp, S), 1) // MOBA_BLOCK)
    n = lax.broadcasted_iota(jnp.int32, (nbp, MOBA_BLOCK), 0)
    for hh in range(2):
        sc = sc_ref.at[hh]
        sc[...] = jnp.where(past, g[2 * hh * nbp:(2 * hh + 1) * nbp] + g[(2 * hh + 1) * nbp:(2 * hh + 2) * nbp], NEG)
        for c in range(nb):
            cols = slice(c * MOBA_BLOCK, (c + 1) * MOBA_BLOCK)
            sel = n <= c
            if c > ntop:
                nr = -(-c // SUBLANES) * SUBLANES
                rank = _ranks(sc, c, nr, cols)
                rank = rank + jnp.where(sc[0:nr, cols] < NEG, nb - c, 0)
                if nr < nbp:
                    rank = jnp.concatenate([rank, jnp.full((nbp - nr, MOBA_BLOCK), nb, jnp.int32)], axis=0)
                sel = ((rank < ntop) & (n < c)) | (n == c)
            bias_ref[hh, 0:nbp, cols] = jnp.where(sel, 0.0, NEG).astype(bias_ref.dtype)
        bias_ref[hh, nbp:FEAT, :] = jnp.zeros((FEAT - nbp, S), bias_ref.dtype)


def _moba_gate(mq_t, mk, avg, *, B, S):
    H = MOBA_HEADS
    nb = S // MOBA_BLOCK
    nbp = avg.shape[0]
    assert nb <= nbp and S % MOBA_BLOCK == 0 and MOBA_BLOCK % LANES == 0 and H % 2 == 0
    ntop = min(MOBA_TOP_K, nb)
    return pl.pallas_call(
        functools.partial(_moba_gate_kernel, nb=nb, ntop=ntop),
        grid=(B, H // 2),
        in_specs=[pl.BlockSpec((2 * HEAD_DIM, S), lambda b, h: (h, b)),
                  pl.BlockSpec((S, LANES), lambda b, h: (b, h)),
                  pl.BlockSpec(avg.shape, lambda b, h: (0, 0))],
        out_specs=pl.BlockSpec((None, 2, FEAT, S), lambda b, h: (b, h, 0, 0)),
        out_shape=jax.ShapeDtypeStruct((B, H, FEAT, S), BF16),
        scratch_shapes=[pltpu.VMEM((2, nbp, S), F32)],
        compiler_params=pltpu.CompilerParams(dimension_semantics=("parallel", "parallel"),
                                             vmem_limit_bytes=VMEM_LIMIT),
        name="moba_gate",
    )(mq_t, mk, avg)


def _flash_q_tile(qi, running_max, q_ref, bias_ref, mask_ref, o_ref, ka_ref, vta_ref, kn_ref, qa_ref, bnd_ref,
                  m_ref, acc_ref, alpha_ref, p_ref, *, tq, tk, nh, pair, window, has_past=None):
    u = pl.program_id(1)
    R = nh * tq
    static = isinstance(qi, int)
    q0 = qi * tq if static else pl.multiple_of(qi * tq, tq)
    toks = pl.ds(q0, tq)

    qcat = jnp.concatenate([q_ref[hh * HEAD_DIM:(hh + 1) * HEAD_DIM, toks] for hh in range(nh)], axis=1)
    qf = qcat.astype(F32)
    qq = jnp.concatenate([qf, qf], axis=0)
    rowi = lax.broadcasted_iota(jnp.int32, (2 * HEAD_DIM, R), 0)
    coli = lax.broadcasted_iota(jnp.int32, (2 * HEAD_DIM, R), 1)
    half = (coli // tq) % 2 if pair else u % 2
    qa_ref[0:2 * HEAD_DIM, :] = jnp.where(rowi // HEAD_DIM == half, qq, 0.0).astype(BF16)
    if bias_ref is None:
        qa_ref[2 * HEAD_DIM:2 * HEAD_DIM + FEAT, :] = jnp.zeros((FEAT, R), BF16)
    else:
        qa_ref[2 * HEAD_DIM:2 * HEAD_DIM + FEAT, :] = jnp.concatenate(
            [bias_ref[hh if pair else 0, :, toks] for hh in range(nh)], axis=1)
    qa_ref[2 * HEAD_DIM + FEAT:2 * LANES, :] = _alibi_query_rows(q0, tq, u * nh, R)
    if pair:
        kn = jnp.concatenate([kn_ref[hh:hh + 1, :] for hh in range(nh) for _ in range(tq // LANES)], axis=1)
    else:
        kn = jnp.concatenate([jnp.where(u % 2 == 0, kn_ref[0:1, :], kn_ref[1:2, :])] * (R // LANES), axis=1)
    qn = jnp.sqrt(jnp.sum(qf * qf, axis=0, keepdims=True))
    bnd_ref[...] = jnp.broadcast_to(qn * kn, bnd_ref.shape)

    ncg = R // CG
    kd = (q0 + tq - 1) // tk

    def key_rows(kind, c):
        off = (c * CG) % tq
        if kind is None:
            return 0, tk, None
        if kind == "diag":
            return 0, off + CG, off // CG
        return off, tk, (tk + off) // CG

    def values(ki, kind=None):
        for c in range(ncg):
            cols = pl.ds(c * CG, CG)
            r0, r1, _ = key_rows(kind, c)
            v = (c * CG) // tq if pair else 0
            pv = _dot(vta_ref[v, ki, :, r0:r1], p_ref[r0:r1, cols])
            if running_max:
                acc_ref[:, cols] = alpha_ref[0:1, cols] * acc_ref[:, cols] + pv
            else:
                acc_ref[:, cols] = acc_ref[:, cols] + pv

    def scores(ki, kind=None, prev=None, prev_kind=None):
        k0 = ki * tk if isinstance(ki, int) else pl.multiple_of(ki * tk, tk)
        s_groups = []
        for c in range(ncg):
            r0, r1, _ = key_rows(kind, c)
            kb = (c * CG) // (2 * tq) if pair else 0
            s_groups.append(_dot(ka_ref[kb, pl.ds(k0 + r0, r1 - r0), :], qa_ref[:, pl.ds(c * CG, CG)]))
        if prev is not None:
            values(prev, prev_kind)
        for c, s in enumerate(s_groups):
            cols = pl.ds(c * CG, CG)
            r0, r1, entry = key_rows(kind, c)
            if entry is not None:
                s = s + mask_ref[entry, r0:r1, :]
            if running_max:
                m_prev = m_ref[0:1, cols]
                m_new = jnp.maximum(m_prev, jnp.max(s, axis=0, keepdims=True))
                alpha_ref[:, cols] = jnp.broadcast_to(jnp.exp(m_prev - m_new), (alpha_ref.shape[0], CG))
                m_ref[:, cols] = jnp.broadcast_to(m_new, (m_ref.shape[0], CG))
                p_ref[r0:r1, cols] = jnp.exp(s - m_new).astype(BF16)
            else:
                p_ref[r0:r1, cols] = jnp.exp(s - bnd_ref[0:1, cols]).astype(BF16)

    def steady(lo, hi):
        def pair_body(j, carry):
            ki = lo + 2 * j
            scores(ki, prev=ki - 1)
            scores(ki + 1, prev=ki)
            return carry
        lax.fori_loop(0, (hi - lo) // 2, pair_body, 0)

        @pl.when((hi - lo) % 2 == 1)
        def _odd_one():
            scores(hi - 1, prev=hi - 2)

    def only_diagonal():
        scores(0, "diag")
        values(0, "diag")

    def past_then_diagonal():
        if window is not None:
            scores(kd - 1, "wedge")
            scores(kd, "diag", prev=kd - 1, prev_kind="wedge")
        elif static:
            scores(0)
            for ki in range(1, kd):
                scores(ki, prev=ki - 1)
            scores(kd, "diag", prev=kd - 1)
        else:
            scores(0)
            steady(1, kd)
            scores(kd, "diag", prev=kd - 1)
        values(kd, "diag")

    acc_ref[...] = jnp.zeros(acc_ref.shape, F32)
    if running_max:
        m_ref[...] = jnp.full(m_ref.shape, NEG, F32)
    if static:
        only_diagonal() if kd == 0 else past_then_diagonal()
    elif has_past:
        past_then_diagonal()
    else:
        pl.when(kd == 0)(only_diagonal)
        pl.when(kd > 0)(past_then_diagonal)

    acc = acc_ref[...]
    out = (acc[0:HEAD_DIM] / acc[HEAD_DIM:HEAD_DIM + 1]).astype(o_ref.dtype)
    if pair:
        for hh in range(nh):
            o_ref[hh, :, toks] = out[:, hh * tq:(hh + 1) * tq]
    else:
        o_ref[0, :, pl.ds(qi * R if static else pl.multiple_of(qi * R, R), R)] = out
    return jnp.min(acc[HEAD_DIM:HEAD_DIM + 1, :])


def _flash_kernel(*refs, tq, tk, nh, pair, window, has_bias):
    q_ref = refs[0]
    bias_ref = refs[1] if has_bias else None
    (k_ref, kf_ref, vt_ref, mask_ref, o_ref, ka_ref, vta_ref, kn_ref, qa_ref, bnd_ref, m_ref, acc_ref,
     alpha_ref, p_ref, low_ref) = refs[2 if has_bias else 1:]
    S = k_ref.shape[0]
    for kb in range(ka_ref.shape[0]):
        k = k_ref[:, kb * LANES:(kb + 1) * LANES]
        ka_ref[kb, :, 0:LANES] = k
        ka_ref[kb, :, LANES:2 * LANES] = kf_ref[...]
        k2 = jnp.square(k.astype(F32))
        lane = lax.broadcasted_iota(jnp.int32, k2.shape, 1)
        for hf in range(2):
            n2 = jnp.sum(jnp.where(lane // HEAD_DIM == hf, k2, 0.0), axis=1, keepdims=True)
            kn_ref[2 * kb + hf:2 * kb + hf + 1, :] = jnp.broadcast_to(
                jnp.sqrt(jnp.max(n2, axis=0, keepdims=True)), (1, LANES))
    ones_row = jnp.where(lax.broadcasted_iota(jnp.int32, (VROWS - HEAD_DIM, tk), 0) == 0, 1.0, 0.0)
    for v in range(nh if pair else 1):
        for jt in range(S // tk):
            vta_ref[v, jt, 0:HEAD_DIM, :] = vt_ref[v * HEAD_DIM:(v + 1) * HEAD_DIM, jt * tk:(jt + 1) * tk]
            vta_ref[v, jt, HEAD_DIM:VROWS, :] = ones_row.astype(BF16)

    tile = functools.partial(_flash_q_tile, q_ref=q_ref, bias_ref=bias_ref, mask_ref=mask_ref, o_ref=o_ref,
                             ka_ref=ka_ref, vta_ref=vta_ref, kn_ref=kn_ref, qa_ref=qa_ref, bnd_ref=bnd_ref,
                             m_ref=m_ref, acc_ref=acc_ref, alpha_ref=alpha_ref, p_ref=p_ref,
                             tq=tq, tk=tk, nh=nh, pair=pair, window=window)
    nq = S // tq
    low_ref[0] = tile(0, False)

    def bound_stabilised(qi, carry):
        low_ref[qi] = tile(qi, False, has_past=True)
        return carry
    lax.fori_loop(1, nq, bound_stabilised, 0)

    def redo(qi, carry):
        @pl.when(low_ref[qi] < MIN_WEIGHT_SUM)
        def _running_max():
            tile(qi, True)
        return carry

    lax.fori_loop(0, nq, redo, 0)


def _mask_table(tk, window):
    n_delta = (tk + (window or 0)) // CG
    d = np.arange(n_delta)[:, None, None] * CG + np.arange(CG)[None, None, :] - np.arange(tk)[None, :, None]
    ok = (d >= 0) & ((d < window) if window else True)
    return jnp.asarray(np.where(ok, 0.0, NEG), F32)


def _flash(q_t, bias_t, k, kfeat, v_t, *, B, S, units, tq, nh, pair, window):
    tk = TK
    R = nh * tq
    nq = S // tq
    nv = nh if pair else 1
    nb = nh if pair else 1
    nkb = nh // 2 if pair else 1
    assert tq == tk and S % tk == 0 and tk % CG == 0 and window in (None, tk)
    mask = _mask_table(tk, window)
    kcol = (lambda u: u) if pair else (lambda u: u // 2)
    out_w = tq if pair else R
    out_shape = (B, units * nb, HEAD_DIM, S if pair else S * nh)
    has_bias = bias_t is not None
    in_specs = [pl.BlockSpec((nh * HEAD_DIM, S), lambda b, u: (u, b))]
    if has_bias:
        in_specs.append(pl.BlockSpec((None, nb, FEAT, S), lambda b, u: (b, u, 0, 0)))
    in_specs += [pl.BlockSpec((S, nkb * LANES), lambda b, u: (b, kcol(u))),
                 pl.BlockSpec((S, LANES), lambda b, u: (0, 0)),
                 pl.BlockSpec((nv * HEAD_DIM, S), lambda b, u: (u, b)),
                 pl.BlockSpec(mask.shape, lambda b, u: (0, 0, 0))]
    args = (q_t,) + ((bias_t,) if has_bias else ()) + (k, kfeat, v_t, mask)
    return pl.pallas_call(
        functools.partial(_flash_kernel, tq=tq, tk=tk, nh=nh, pair=pair, window=window, has_bias=has_bias),
        grid=(B, units),
        in_specs=in_specs,
        out_specs=pl.BlockSpec((None, nb, HEAD_DIM, out_w * nq), lambda b, u: (b, u, 0, 0)),
        out_shape=jax.ShapeDtypeStruct(out_shape, BF16),
        scratch_shapes=[pltpu.VMEM((nkb, S, 2 * LANES), BF16),
                        pltpu.VMEM((nv, S // tk, VROWS, tk), BF16),
                        pltpu.VMEM((SUBLANES, LANES), F32),
                        pltpu.VMEM((2 * LANES, R), BF16),
                        pltpu.VMEM((SUBLANES, R), F32),
                        pltpu.VMEM((SUBLANES, R), F32),
                        pltpu.VMEM((VROWS, R), F32),
                        pltpu.VMEM((SUBLANES, R), F32),
                        pltpu.VMEM((tk, R), BF16),
                        pltpu.SMEM((nq,), F32)],
        compiler_params=pltpu.CompilerParams(dimension_semantics=("parallel", "parallel"),
                                             vmem_limit_bytes=VMEM_LIMIT),
        name="flash",
    )(*args)


def _even_out_kernel(x_ref, ocmp_ref, osel_ref, owin_ref, om_ref, gate_ref, nz_ref, mz_ref,
                     w_ref, g_ref, b_ref, o_ref, sig_ref, *, alpha, tm):
    G, nh = NSA_KV_GROUPS, NSA_HG
    sig_ref[...] = _sigmoid(gate_ref[...])
    heads = []
    for g in range(G):
        for hh in range(nh):
            acc = None
            for br, ref in enumerate((ocmp_ref, osel_ref, owin_ref)):
                gate = sig_ref[pl.ds(3 * (g * nh + hh) + br, 1), :]
                term = gate * ref[g, :, hh * tm:(hh + 1) * tm].astype(F32)
                acc = term if acc is None else acc + term
            heads.append(acc)
    o_nsa = jnp.concatenate(heads, axis=0)
    nz = nz_ref[...].astype(F32)
    mz = mz_ref[...].astype(F32)
    o_m = jnp.concatenate([om_ref[h] for h in range(MOBA_HEADS)], axis=0).astype(F32)
    a_t = jnp.concatenate([o_nsa * (nz * _sigmoid(nz)), o_m * (mz * _sigmoid(mz))], axis=0)
    y = _dot_tn(a_t.astype(BF16), w_ref[...])
    o_ref[...] = _layer_norm(alpha * x_ref[...] + y, g_ref[...], b_ref[...])


def _even_out(x2d, ocmp, osel, owin, om, gate_t, nz_t, mz_t, w_out, ln_g, ln_b, alpha, *, B, S):
    T, D = x2d.shape
    tm = TQ
    nq = S // tm
    G, nh, H = NSA_KV_GROUPS, NSA_HG, MOBA_HEADS
    nsa = pl.BlockSpec((None, G, HEAD_DIM, nh * tm), lambda i: (i // nq, 0, 0, i % nq))
    feat = lambda n: pl.BlockSpec((n, tm), lambda i: (0, i))
    full = lambda a: pl.BlockSpec(a.shape, lambda i: (0,) * a.ndim)
    return pl.pallas_call(
        functools.partial(_even_out_kernel, alpha=alpha, tm=tm),
        grid=(T // tm,),
        in_specs=[pl.BlockSpec((tm, D), lambda i: (i, 0)), nsa, nsa, nsa,
                  pl.BlockSpec((None, H, HEAD_DIM, tm), lambda i: (i // nq, 0, 0, i % nq)),
                  feat(GATE_ROWS), feat(NSA_W), feat(MOBA_W),
                  full(w_out), full(ln_g), full(ln_b)],
        out_specs=pl.BlockSpec((tm, D), lambda i: (i, 0)),
        out_shape=jax.ShapeDtypeStruct((T, D), F32),
        scratch_shapes=[pltpu.VMEM((GATE_ROWS, tm), F32)],
        compiler_params=pltpu.CompilerParams(dimension_semantics=("parallel",),
                                             vmem_limit_bytes=VMEM_LIMIT),
        name="even_out",
    )(x2d, ocmp, osel, owin, om, gate_t, nz_t, mz_t, w_out, ln_g, ln_b)


def _gla_head_exact(row0, h, q_ref, v_ref, o_ref, st_ref, b_ref, kf_ref, scale):
    C, dk, dv, SUB = GLA_CHUNK, GLA_DK, GLA_DV, GLA_SUB
    rows = lax.broadcasted_iota(jnp.int32, (C, dk), 0)
    sub_i = lax.broadcasted_iota(jnp.int32, (SUB, LANES), 0)
    lane = lax.broadcasted_iota(jnp.int32, (SUB, LANES), 1)
    ones = jnp.ones((dk, LANES), BF16)
    sl = pl.ds(h * dk, dk)
    rs_c = pl.ds(row0, C)
    bh = b_ref[:, sl]
    kf = kf_ref[:, sl]
    qf = q_ref[rs_c, sl].astype(F32) * scale
    vh = v_ref[rs_c, pl.ds(h * dv, dv)]
    st = st_ref[h]
    inter = _dot_nt((qf * jnp.exp(bh)).astype(BF16), st.astype(BF16))
    bl = b_ref[pl.ds(C - 1, 1), sl]
    kd = kf * jnp.exp(bl - bh)

    def off_block(r0, r1, c0, c1):
        ref_row = b_ref[pl.ds(r0, 1), sl]
        qt = qf[r0:r1] * jnp.exp(bh[r0:r1] - ref_row)
        ek = jnp.where((rows >= c0) & (rows < c1), ref_row - bh, NEG)
        kt = kf * jnp.exp(ek)
        return _dot_nt(qt.astype(BF16), kt.astype(BF16))

    lower = off_block(2 * SUB, C, 0, 2 * SUB)
    blocks = [None, off_block(SUB, 2 * SUB, 0, SUB), lower[0:SUB],
              lower[SUB:2 * SUB] + off_block(3 * SUB, C, 2 * SUB, 3 * SUB)]
    pieces = []
    for blk in range(C // SUB):
        r0 = blk * SUB
        q_i = qf[r0:r0 + SUB]
        b_i = bh[r0:r0 + SUB]
        for jj in range(SUB):
            b_j = b_ref[pl.ds(r0 + jj, 1), sl]
            k_j = kf_ref[pl.ds(r0 + jj, 1), sl]
            pieces.append((q_i * k_j * jnp.exp(jnp.minimum(b_i - b_j, 0.0))).astype(BF16))
    sums = _dot(jnp.concatenate(pieces, axis=0), ones)
    for blk in range(C // SUB):
        r0 = blk * SUB
        diag = jnp.zeros((SUB, LANES), F32)
        for jj in range(SUB):
            rs = sums[(r0 + jj) * SUB:(r0 + jj + 1) * SUB]
            diag = jnp.where((lane == r0 + jj) & (sub_i >= jj), rs, diag)
        diag = diag[:, 0:C]
        blocks[blk] = diag if blocks[blk] is None else blocks[blk] + diag
    att = jnp.concatenate(blocks, axis=0)
    o_ref[rs_c, pl.ds(h * dv, dv)] = (inter + _dot(att.astype(BF16), vh)).astype(o_ref.dtype)
    st_ref[h] = st * jnp.exp(bl) + _dot_tn(vh, kd.astype(BF16))


def _gla_head_bounded(row0, h, q_ref, v_ref, o_ref, st_ref, b_ref, kf_ref, scale):
    C, dk, dv = GLA_CHUNK, GLA_DK, GLA_DV
    sl = pl.ds(h * dk, dk)
    rs_c = pl.ds(row0, C)
    bh = b_ref[rs_c, sl]
    kf = kf_ref[rs_c, sl]
    qf = q_ref[rs_c, sl].astype(F32) * scale
    vh = v_ref[rs_c, pl.ds(h * dv, dv)]
    st = st_ref[h]
    qs = (qf * jnp.exp(bh)).astype(BF16)
    kt = (kf * jnp.exp(-bh)).astype(BF16)
    i = lax.broadcasted_iota(jnp.int32, (C, C), 0)
    j = lax.broadcasted_iota(jnp.int32, (C, C), 1)
    att = jnp.where(i >= j, _dot_nt(qs, kt), 0.0)
    inter = _dot_nt(qs, st.astype(BF16))
    o_ref[rs_c, pl.ds(h * dv, dv)] = (inter + _dot(att.astype(BF16), vh)).astype(o_ref.dtype)
    bl = b_ref[pl.ds(row0 + C - 1, 1), sl]
    kd = kf * jnp.exp(bl - bh)
    st_ref[h] = st * jnp.exp(bl) + _dot_tn(vh, kd.astype(BF16))


def _gla_kernel(q_ref, k_ref, v_ref, glr_ref, w2_ref, gb_ref, tril_ref, o_ref,
                st_ref, b_ref, kf_ref, bc_ref, kc_ref, *, scale):
    C, H = GLA_CHUNK, GLA_HEADS
    nch = q_ref.shape[0] // C

    @pl.when(pl.program_id(1) == 0)
    def _init():
        st_ref[...] = jnp.zeros(st_ref.shape, F32)

    whi, wlo = _hilo(w2_ref[...])
    part = tril_ref.shape[0]
    decay = None
    for r0 in range(0, nch * C, part):
        rs = pl.ds(r0, part)
        ghi, glo = _hilo(glr_ref[rs, :])
        pre = _dot(ghi, whi) + _dot(glo, whi) + _dot(ghi, wlo) + gb_ref[...]
        la = (jnp.minimum(pre, 0.0) - jnp.log(1.0 + jnp.exp(-jnp.abs(pre)))) * (1.0 / GLA_TAU)
        lhi, llo = _hilo(la)
        b_part = _dot(tril_ref[...], lhi) + _dot(tril_ref[...], llo)
        b_ref[rs, :] = b_part
        for c in range(part // C):
            d_c = -b_part[c * C + C - 1:(c + 1) * C, :]
            decay = d_c if decay is None else jnp.maximum(decay, d_c)
    kf_ref[...] = k_ref[...].astype(F32)
    bounded = jnp.max(decay) <= GLA_FACTOR_MAX_DECAY

    @pl.when(bounded)
    def _factored():
        for c in range(nch):
            for h in range(H):
                _gla_head_bounded(c * C, h, q_ref, v_ref, o_ref, st_ref, b_ref, kf_ref, scale)

    @pl.when(jnp.logical_not(bounded))
    def _exact():
        def chunk(c, carry):
            row0 = pl.multiple_of(c * C, C)
            bc_ref[...] = b_ref[pl.ds(row0, C), :]
            kc_ref[...] = kf_ref[pl.ds(row0, C), :]
            for h in range(H):
                _gla_head_exact(row0, h, q_ref, v_ref, o_ref, st_ref, bc_ref, kc_ref, scale)
            return carry
        lax.fori_loop(0, nch, chunk, 0)


def _gla(q, k, v, glr, w2p, gb, tril, *, B, S):
    C, H, dk, dv = GLA_CHUNK, GLA_HEADS, GLA_DK, GLA_DV
    rows = C * min(GLA_STEP_CHUNKS, S // C)
    nchunk = S // rows
    tok = lambda n: pl.BlockSpec((rows, n), lambda b, c: (b * nchunk + c, 0))
    const = lambda shape: pl.BlockSpec(shape, lambda b, c: (0,) * len(shape))
    return pl.pallas_call(
        functools.partial(_gla_kernel, scale=float(dk) ** -0.5),
        grid=(B, nchunk),
        in_specs=[tok(H * dk), tok(H * dk), tok(H * dv), tok(LANES),
                  const(w2p.shape), const(gb.shape), const(tril.shape)],
        out_specs=tok(H * dv),
        out_shape=jax.ShapeDtypeStruct((B * S, H * dv), BF16),
        scratch_shapes=[pltpu.VMEM((H, dv, dk), F32),
                        pltpu.VMEM((rows, H * dk), F32),
                        pltpu.VMEM((rows, H * dk), F32),
                        pltpu.VMEM((C, H * dk), F32),
                        pltpu.VMEM((C, H * dk), F32)],
        compiler_params=pltpu.CompilerParams(dimension_semantics=("parallel", "arbitrary"),
                                             vmem_limit_bytes=VMEM_LIMIT),
        name="gla",
    )(q, k, v, glr, w2p, gb, tril)


def _odd_out_kernel(x_ref, o_ref_in, z_ref, gn_ref, w_ref, g_ref, b_ref, out_ref, *, alpha):
    H, dv = GLA_HEADS, GLA_DV
    parts = []
    for h in range(H):
        sl = pl.ds(h * dv, dv)
        oh = o_ref_in[:, sl].astype(F32)
        r = lax.rsqrt(jnp.mean(oh * oh, axis=-1, keepdims=True) + LN_EPS)
        z = z_ref[:, sl].astype(F32)
        parts.append((oh * r * gn_ref[:, sl] * (z * _sigmoid(z))).astype(BF16))
    y = _dot(jnp.concatenate(parts, axis=1), w_ref[...])
    out_ref[...] = _layer_norm(alpha * x_ref[...] + y, g_ref[...], b_ref[...])


def _odd_out(x2d, o, z, gn_g, w_out, ln_g, ln_b, alpha, tm=1024):
    T, D = x2d.shape
    W = o.shape[1]
    row = lambda n: pl.BlockSpec((tm, n), lambda i: (i, 0))
    return pl.pallas_call(
        functools.partial(_odd_out_kernel, alpha=alpha),
        grid=(T // tm,),
        in_specs=[row(D), row(W), row(W),
                  pl.BlockSpec((1, W), lambda i: (0, 0)),
                  pl.BlockSpec((W, D), lambda i: (0, 0)),
                  pl.BlockSpec((1, D), lambda i: (0, 0)),
                  pl.BlockSpec((1, D), lambda i: (0, 0))],
        out_specs=row(D),
        out_shape=jax.ShapeDtypeStruct((T, D), F32),
        compiler_params=pltpu.CompilerParams(dimension_semantics=("parallel",),
                                             vmem_limit_bytes=VMEM_LIMIT),
        name="odd_out",
    )(x2d, o, z, gn_g, w_out, ln_g, ln_b)


PACK = 2 * SUBLANES


def _pack_rows_kernel(w_ref, *o_refs, plans):
    for o_ref, plan in zip(o_refs, plans):
        off = 0
        for src, n, scale, n_out in plan:
            n_load = min(-(-n // PACK) * PACK, n_out)
            rows = w_ref[src:src + n_load, :]
            if scale != 1.0:
                rows = rows * scale
            if n_load != n:
                rows = jnp.where(lax.broadcasted_iota(jnp.int32, rows.shape, 0) < n, rows, 0.0)
            o_ref[off:off + n_load, :] = rows.astype(BF16)
            if n_out > n_load:
                o_ref[off + n_load:off + n_out, :] = jnp.zeros((n_out - n_load, o_ref.shape[1]), BF16)
            off += n_out


def _pack_rows(w_t, plans, tc=256):
    W, D = w_t.shape
    plans = tuple(tuple((p + (p[1],))[:4] for p in plan) for plan in plans)
    for plan in plans:
        for src, n, _, n_out in plan:
            assert src % SUBLANES == 0 and n_out % PACK == 0 and n <= n_out and src + n_out <= W
    assert D % tc == 0
    rows = [sum(p[3] for p in plan) for plan in plans]
    return pl.pallas_call(
        functools.partial(_pack_rows_kernel, plans=plans),
        grid=(D // tc,),
        in_specs=[pl.BlockSpec((W, tc), lambda j: (0, j))],
        out_specs=[pl.BlockSpec((r, tc), lambda j: (0, j)) for r in rows],
        out_shape=[jax.ShapeDtypeStruct((r, D), BF16) for r in rows],
        compiler_params=pltpu.CompilerParams(dimension_semantics=("parallel",),
                                             vmem_limit_bytes=VMEM_LIMIT),
        name="pack_rows",
    )(w_t)


def _key_features(S, block):
    s = np.arange(S)
    f = np.zeros((S, LANES), np.float32)
    f[s, s // block] = 1.0
    f[:, FEAT + 0] = 1.0
    f[:, FEAT + 1] = 1.0
    f[:, FEAT + 2] = FEAT * (s // FEAT)
    f[:, FEAT + 3] = s % FEAT
    return jnp.asarray(f, BF16)


def _pad_to(a, n, axis):
    pad = [(0, 0)] * a.ndim
    pad[axis] = (0, n - a.shape[axis])
    return jnp.pad(a, pad)


def _even_layer(x2d, B, S, w_in, cmp_pos, cmp_w1, cmp_b1, cmp_w2, w_out, ln_g, ln_b, alpha):
    T, D = x2d.shape
    G, hd = NSA_KV_GROUPS, HEAD_DIM
    scale = hd ** -0.5
    kvw, (oq, okv, og, onz, omq, omk, omv, omz) = NSA_KV_W, np.cumsum((0,) + EVEN_SPLITS[:-1]).tolist()
    okc, ovc, oks, ovs, okw, ovw = (okv + i * kvw for i in range(6))
    w_tok, w_tr = _pack_rows(w_in.T, (
        ((okc, kvw, 1.0), (ovc, kvw, 1.0), (oks, kvw, 1.0), (okw, kvw, 1.0), (omk, MOBA_W, 1.0)),
        ((oq, NSA_W, scale), (ovs, kvw, 1.0), (ovw, kvw, 1.0), (omq, MOBA_W, scale), (omv, MOBA_W, 1.0),
         (onz, NSA_W, 1.0), (omz, MOBA_W, 1.0), (og, 3 * NSA_HEADS, 1.0, GATE_ROWS))))
    (kcmp, vcmp, ksel, kwin, mk, nq_t, vsel_t, vwin_t, mq_t, mv_t, nz_t, mz_t, gate_t) = _proj(
        x2d, w_tok, (NSA_KV_W,) * 4 + (MOBA_W,), (F32, F32, BF16, BF16, BF16),
        w_tr, (NSA_W, NSA_KV_W, NSA_KV_W, MOBA_W, MOBA_W, NSA_W, MOBA_W, GATE_ROWS),
        (BF16,) * 7 + (F32,))

    L, d, HID = NSA_CMP_BLOCK, NSA_CMP_STRIDE, NSA_CMP_HIDDEN
    ncmp = (S - L) // d + 1
    NC = S // d
    def group_diagonal(w, axis):
        z = jnp.zeros_like(w)
        return jnp.stack([jnp.concatenate([w if h == g else z for h in range(G)], axis=-1) for g in range(G)], axis=axis)

    w1r = cmp_w1.reshape(2, 2, d, hd, HID).astype(BF16)
    w1p = group_diagonal(w1r, 3).reshape(2, 2, d * G * hd, G * HID)
    posp = jnp.broadcast_to(cmp_pos.reshape(2, 2, d, 1, hd), (2, 2, d, G, hd)).reshape(2, 2, 1, d * G * hd)
    b1p = jnp.tile(cmp_b1, (1, G)).reshape(2, 1, G * HID)
    w2p = group_diagonal(_pad_to(cmp_w2, LANES, 2).astype(BF16), 1).reshape(2, G * HID, G * LANES)
    cfeat = np.zeros((NC, G, LANES), np.float32)
    cfeat[:, :, hd + 0] = 1.0
    cfeat[:, :, hd + 1] = 1.0
    cfeat[:, :, hd + 2] = (np.arange(NC) * d)[:, None]
    cfeat[:, :, hd + 3] = L - 1
    kvc = _cmp_mlp(kcmp, vcmp, w1p, posp, b1p, w2p, jnp.asarray(cfeat.reshape(NC, G * LANES)), ncmp,
                   B=B, S=S)

    ns = S // NSA_SEL_BLOCK
    cmp_start = np.arange(NC) * d
    sel_start = np.arange(FEAT) * NSA_SEL_BLOCK
    ovt = ((cmp_start[None, :] <= sel_start[:, None] + NSA_SEL_BLOCK - 1)
           & (cmp_start[None, :] + L - 1 >= sel_start[:, None])
           & (np.arange(NC)[None, :] < ncmp) & (np.arange(FEAT)[:, None] < ns))
    o_cmp, bias_sel = _cmp_attn(nq_t, kvc, jnp.asarray(ovt, BF16), B=B, S=S)

    feats_sel = _key_features(S, NSA_SEL_BLOCK)
    o_sel = _flash(nq_t, bias_sel, ksel, feats_sel, vsel_t,
                   B=B, S=S, units=G, tq=TQ, nh=NSA_HG, pair=False, window=None)
    o_win = _flash(nq_t, None, kwin, feats_sel, vwin_t,
                   B=B, S=S, units=G, tq=TQ, nh=NSA_HG, pair=False, window=NSA_WINDOW)

    avg = np.zeros((MOBA_NBP, S), np.float32)
    avg[np.arange(S) // MOBA_BLOCK, np.arange(S)] = 1.0 / MOBA_BLOCK
    bias_m = _moba_gate(mq_t, mk, jnp.asarray(avg, BF16), B=B, S=S)
    o_m = _flash(mq_t, bias_m, mk, _key_features(S, MOBA_BLOCK), mv_t,
                 B=B, S=S, units=MOBA_HEADS // MOBA_STEP_HEADS, tq=TQM, nh=MOBA_STEP_HEADS, pair=True, window=None)

    return _even_out(x2d, o_cmp, o_sel, o_win, o_m, gate_t, nz_t, mz_t, w_out.astype(BF16),
                     ln_g.reshape(1, D), ln_b.reshape(1, D), alpha, B=B, S=S)


def _odd_layer(x2d, B, S, w_in, gate_w2, gate_b, gn_g, w_out, ln_g, ln_b, alpha):
    T, D = x2d.shape
    H, dk, dv, C = GLA_HEADS, GLA_DK, GLA_DV, GLA_CHUNK
    n_qkv, n_gate, n_z = sum(ODD_SPLITS[:3]), ODD_SPLITS[3], ODD_SPLITS[4]
    w_perm, = _pack_rows(w_in.T, (((0, n_qkv, 1.0), (n_qkv + n_gate, n_z, 1.0), (n_qkv, n_gate, 1.0, LANES)),))
    q, k, v, z, glr = _proj(x2d, w_perm, (H * dk, H * dk, H * dv, H * dv, LANES),
                            (BF16, BF16, BF16, BF16, F32))
    w2p = _pad_to(gate_w2, LANES, 0)
    ncs = min(GLA_CUMSUM_CHUNKS, GLA_STEP_CHUNKS, S // C)
    tril = jnp.asarray(np.kron(np.eye(ncs), np.tril(np.ones((C, C)))), BF16)
    o = _gla(q, k, v, glr, w2p, gate_b.reshape(1, H * dk), tril, B=B, S=S)
    return _odd_out(x2d, o, z, gn_g.reshape(1, H * dv), w_out.astype(BF16),
                    ln_g.reshape(1, D), ln_b.reshape(1, D), alpha)


def kernel(x, ev_w_in, ev_cmp_pos, ev_cmp_w1, ev_cmp_b1, ev_cmp_w2, ev_w_out, ev_ln_g, ev_ln_b,
           od_w_in, od_gate_w2, od_gate_b, od_gn_g, od_w_out, od_ln_g, od_ln_b):
    B, S, D = x.shape
    depth = ev_w_in.shape[0] + od_w_in.shape[0]
    alpha = (2.0 * depth) ** 0.25
    h = x.reshape(B * S, D)
    for layer in range(depth):
        i = layer // 2
        if layer % 2 == 0:
            h = _even_layer(h, B, S, ev_w_in[i], ev_cmp_pos[i], ev_cmp_w1[i], ev_cmp_b1[i], ev_cmp_w2[i],
                            ev_w_out[i], ev_ln_g[i], ev_ln_b[i], alpha)
        else:
            h = _odd_layer(h, B, S, od_w_in[i], od_gate_w2[i], od_gate_b[i], od_gn_g[i],
                           od_w_out[i], od_ln_g[i], od_ln_b[i], alpha)
    return h.reshape(B, S, D)
```

```python
import functools

import numpy as np
import jax
import jax.numpy as jnp
from jax import lax
from jax.experimental import pallas as pl
from jax.experimental.pallas import tpu as pltpu

F32 = jnp.float32
BF16 = jnp.bfloat16

HEAD_DIM = 64
NSA_HEADS = 8
NSA_KV_GROUPS = 2
NSA_HG = NSA_HEADS // NSA_KV_GROUPS
NSA_CMP_BLOCK = 32
NSA_CMP_STRIDE = 16
NSA_CMP_HIDDEN = 128
NSA_SEL_BLOCK = 64
NSA_TOP_N = 16
NSA_WINDOW = 512
MOBA_HEADS = 8
MOBA_BLOCK = 256
MOBA_TOP_K = 3
GLA_HEADS = 4
GLA_DK = 128
GLA_DV = 256
GLA_GATE_RANK = 16
GLA_TAU = 16.0
GLA_CHUNK = 64
GLA_SUB = 16
GLA_STEP_CHUNKS = 16
GLA_CUMSUM_CHUNKS = 4
GLA_FACTOR_MAX_DECAY = 60.0
LN_EPS = 1e-5
NEG = -1e30
FORCE_BONUS = 1e4

NSA_W = NSA_HEADS * HEAD_DIM
NSA_KV_W = NSA_KV_GROUPS * HEAD_DIM
MOBA_W = MOBA_HEADS * HEAD_DIM
EVEN_SPLITS = (NSA_W, 6 * NSA_KV_W, 3 * NSA_HEADS, NSA_W, MOBA_W, MOBA_W, MOBA_W, MOBA_W)
ODD_SPLITS = (GLA_HEADS * GLA_DK, GLA_HEADS * GLA_DK, GLA_HEADS * GLA_DV, GLA_GATE_RANK, GLA_HEADS * GLA_DV)

LANES = 128
SUBLANES = 8
FEAT = 64
ALIBI_ROWS = 16
GATE_ROWS = 32
MOBA_NBP = 16
TQ = 512
TQM = 512
MOBA_STEP_HEADS = 4
TK = 512
CG = 256
MIN_WEIGHT_SUM = 1e-25
VROWS = HEAD_DIM + 16
VMEM_LIMIT = 48 * 1024 * 1024


def _dot(a, b):
    return jnp.dot(a, b, preferred_element_type=F32)


def _dot_nt(a, b):
    return lax.dot_general(a, b, (((1,), (1,)), ((), ())), preferred_element_type=F32)


def _dot_tn(a, b):
    return lax.dot_general(a, b, (((0,), (0,)), ((), ())), preferred_element_type=F32)


def _hilo(a):
    hi = a.astype(BF16)
    lo = (a - hi.astype(F32)).astype(BF16)
    return hi, lo


def _pow2_neg(n):
    return lax.bitcast_convert_type(lax.shift_left(127 - n, 23), F32)


def _sigmoid(x):
    return 0.5 * jnp.tanh(0.5 * x) + 0.5


def _layer_norm(r, g, b):
    mu = jnp.mean(r, axis=-1, keepdims=True)
    d = r - mu
    var = jnp.mean(d * d, axis=-1, keepdims=True)
    return d * lax.rsqrt(var + LN_EPS) * g + b


def _alibi_query_rows(t0, tq, head0, R):
    r = lax.broadcasted_iota(jnp.int32, (ALIBI_ROWS, R), 0)
    col = lax.broadcasted_iota(jnp.int32, (ALIBI_ROWS, R), 1)
    t = t0 + col % tq
    slope = _pow2_neg(head0 + col // tq + 1)
    feats = jnp.where(r == 0, -FEAT * slope * (t // FEAT).astype(F32),
                      jnp.where(r == 1, -slope * (t % FEAT).astype(F32), jnp.where(r < 4, slope, 0.0)))
    return jnp.concatenate([feats.astype(BF16), jnp.zeros((FEAT - ALIBI_ROWS, R), BF16)], axis=0)


def _ranks(sc_ref, n_real, n_rows=None, cols=slice(None)):
    n_rows = sc_ref.shape[0] if n_rows is None else n_rows
    out = []
    for v in range(n_rows // SUBLANES):
        lo = v * SUBLANES
        sc = sc_ref[lo:lo + SUBLANES, cols]
        width = sc.shape[1]
        idx = lo + lax.broadcasted_iota(jnp.int32, (SUBLANES, width), 0)
        rank = jnp.zeros((SUBLANES, width), jnp.int32)
        for jp in range(n_real):
            r = sc_ref[pl.ds(jp, 1), cols]
            if jp < lo:
                beats = r >= sc
            elif jp >= lo + SUBLANES - 1:
                beats = r > sc
            else:
                beats = (r > sc) | ((r == sc) & (idx > jp))
            rank = rank + jnp.where(beats, 1, 0)
        out.append(rank)
    return jnp.concatenate(out, axis=0)


def _proj_kernel(*refs, tok_splits, tr_splits):
    x_ref, wt_ref = refs[0], refs[1]
    outs = list(refs[3:] if tr_splits else refs[2:])
    x = x_ref[...]
    xb = x.astype(BF16)
    off = 0
    for n in tok_splits:
        o_ref = outs.pop(0)
        o_ref[...] = _dot_nt(xb, wt_ref[off:off + n, :]).astype(o_ref.dtype)
        off += n
    if tr_splits:
        wtr_ref = refs[2]
        xt = x.T.astype(BF16)
        off = 0
        for n in tr_splits:
            o_ref = outs.pop(0)
            o_ref[...] = _dot(wtr_ref[off:off + n, :], xt).astype(o_ref.dtype)
            off += n


def _proj(x2d, w_tok, tok_splits, tok_dtypes, w_tr=None, tr_splits=(), tr_dtypes=(), tm=512):
    T, D = x2d.shape
    assert T % tm == 0 and sum(tok_splits) == w_tok.shape[0]
    in_specs = [pl.BlockSpec((tm, D), lambda i: (i, 0)),
                pl.BlockSpec(w_tok.shape, lambda i: (0, 0))]
    args = [x2d, w_tok]
    if tr_splits:
        assert sum(tr_splits) == w_tr.shape[0]
        in_specs.append(pl.BlockSpec(w_tr.shape, lambda i: (0, 0)))
        args.append(w_tr)
    out_specs = ([pl.BlockSpec((tm, n), lambda i: (i, 0)) for n in tok_splits]
                 + [pl.BlockSpec((n, tm), lambda i: (0, i)) for n in tr_splits])
    out_shape = ([jax.ShapeDtypeStruct((T, n), dt) for n, dt in zip(tok_splits, tok_dtypes)]
                 + [jax.ShapeDtypeStruct((n, T), dt) for n, dt in zip(tr_splits, tr_dtypes)])
    return pl.pallas_call(
        functools.partial(_proj_kernel, tok_splits=tuple(tok_splits), tr_splits=tuple(tr_splits)),
        grid=(T // tm,), in_specs=in_specs, out_specs=out_specs, out_shape=out_shape,
        compiler_params=pltpu.CompilerParams(dimension_semantics=("parallel",),
                                             vmem_limit_bytes=VMEM_LIMIT),
        name="proj",
    )(*args)


def _cmp_mlp_kernel(kx_ref, vx_ref, w1_ref, pos_ref, b1_ref, w2_ref, cf_ref, o_ref, *, ncmp):
    d = NSA_CMP_STRIDE
    nc = kx_ref.shape[0] // d
    cw = w1_ref.shape[2]
    for i, x_ref in enumerate((kx_ref, vx_ref)):
        a = bm = None
        for l in range(0, d, 2):
            x2 = jnp.concatenate([x_ref[pl.ds(l, nc, stride=d), :], x_ref[pl.ds(l + 1, nc, stride=d), :]],
                                 axis=1).astype(BF16)
            rows = pl.ds(l * x_ref.shape[1], 2 * x_ref.shape[1])
            a_l, b_l = _dot(x2, w1_ref[i, 0, rows, :]), _dot(x2, w1_ref[i, 1, rows, :])
            a, bm = (a_l, b_l) if a is None else (a + a_l, bm + b_l)
        c = b1_ref[i]
        for half in range(2):
            phi, plo = _hilo(jnp.broadcast_to(pos_ref[i, half], (SUBLANES, cw)))
            c = c + (_dot(phi, w1_ref[i, half]) + _dot(plo, w1_ref[i, half]))[0:1]
        hid = a + pltpu.roll(bm, nc - 1, axis=0) + c
        hid = hid * _sigmoid(hid)
        row = lax.broadcasted_iota(jnp.int32, hid.shape, 0)
        hid = jnp.where(row < ncmp, hid, 0.0)
        out = _dot(hid.astype(BF16), w2_ref[i])
        if i == 0:
            out = out + cf_ref[...]
        o_ref[i] = out.astype(o_ref.dtype)


def _cmp_mlp(kx, vx, w1p, posp, b1p, w2p, cfeat, ncmp, *, B, S):
    NC = S // NSA_CMP_STRIDE
    OW = w2p.shape[-1]
    full = lambda a: pl.BlockSpec(a.shape, lambda b: (0,) * a.ndim)
    return pl.pallas_call(
        functools.partial(_cmp_mlp_kernel, ncmp=ncmp),
        grid=(B,),
        in_specs=[pl.BlockSpec((S, kx.shape[1]), lambda b: (b, 0)),
                  pl.BlockSpec((S, vx.shape[1]), lambda b: (b, 0)),
                  full(w1p), full(posp), full(b1p), full(w2p), full(cfeat)],
        out_specs=pl.BlockSpec((2, None, NC, OW), lambda b: (0, b, 0, 0)),
        out_shape=jax.ShapeDtypeStruct((2, B, NC, OW), BF16),
        compiler_params=pltpu.CompilerParams(dimension_semantics=("parallel",),
                                             vmem_limit_bytes=VMEM_LIMIT),
        name="cmp_mlp",
    )(kx, vx, w1p, posp, b1p, w2p, cfeat)


def _cmp_attn_kernel(q_ref, kc_ref, vc_ref, ovt_ref, o_ref, bias_ref, sc_ref, *, tq, nh, ns, ntop):
    g = pl.program_id(1)
    R = nh * tq
    nq = q_ref.shape[1] // tq

    def q_tile(qi):
        toks = pl.ds(qi * tq, tq)
        nc = min(kc_ref.shape[0], -(-((qi + 1) * tq) // (NSA_CMP_STRIDE * 16)) * 16)
        nsel = min(ns, -(-((qi + 1) * tq) // (NSA_SEL_BLOCK * SUBLANES)) * SUBLANES)
        qcat = jnp.concatenate([q_ref[hh * HEAD_DIM:(hh + 1) * HEAD_DIM, toks] for hh in range(nh)], axis=1)
        qa = jnp.concatenate([qcat, _alibi_query_rows(qi * tq, tq, g * nh, R)], axis=0)
        s = _dot(kc_ref[0:nc, :], qa)
        n = lax.broadcasted_iota(jnp.int32, (nc, R), 0)
        t_row = qi * tq + lax.broadcasted_iota(jnp.int32, (1, R), 1) % tq
        last = (t_row - (NSA_CMP_BLOCK - 1)) // NSA_CMP_STRIDE
        s = jnp.where(n <= last, s, NEG)
        e = jnp.exp(s - jnp.max(s, axis=0, keepdims=True))
        inv = jnp.where(last >= 0, 1.0 / jnp.sum(e, axis=0, keepdims=True), 0.0)
        p = e * inv
        o_ref[:, pl.ds(qi * R, R)] = _dot_tn(vc_ref[0:nc, 0:HEAD_DIM], p.astype(BF16)).astype(o_ref.dtype)

        psum = p[:, 0:tq]
        for hh in range(1, nh):
            psum = psum + p[:, hh * tq:(hh + 1) * tq]
        phi, plo = _hilo(psum)
        ovt = ovt_ref[0:nsel, 0:nc]
        imp = _dot(ovt, phi) + _dot(ovt, plo)

        j = lax.broadcasted_iota(jnp.int32, (nsel, tq), 0)
        cur = (qi * tq + lax.broadcasted_iota(jnp.int32, (nsel, tq), 1)) // NSA_SEL_BLOCK
        if nsel <= ntop:
            sel = j <= cur
        else:
            forced = jnp.where((j == 0) | (j == cur) | (j == cur - 1), FORCE_BONUS, 0.0)
            sc_ref[0:nsel, :] = jnp.where(j <= cur, imp + forced, NEG)
            sel = (_ranks(sc_ref, nsel, nsel) < ntop) & (j <= cur)
        bias_ref[0:nsel, toks] = jnp.where(sel, 0.0, NEG).astype(bias_ref.dtype)
        if nsel < FEAT:
            bias_ref[nsel:FEAT, toks] = jnp.full((FEAT - nsel, tq), NEG, bias_ref.dtype)

    for qi in range(nq):
        q_tile(qi)


def _cmp_attn(nq_t, kvc, ovt, *, B, S):
    tq, nh, G = TQ, NSA_HG, NSA_KV_GROUPS
    R = nh * tq
    nq = S // tq
    NC = kvc.shape[2]
    ns = S // NSA_SEL_BLOCK
    assert ns <= FEAT
    ntop = min(NSA_TOP_N, ns)
    return pl.pallas_call(
        functools.partial(_cmp_attn_kernel, tq=tq, nh=nh, ns=ns, ntop=ntop),
        grid=(B, G),
        in_specs=[pl.BlockSpec((nh * HEAD_DIM, S), lambda b, g: (g, b)),
                  pl.BlockSpec((None, None, NC, LANES), lambda b, g: (0, b, 0, g)),
                  pl.BlockSpec((None, None, NC, LANES), lambda b, g: (1, b, 0, g)),
                  pl.BlockSpec(ovt.shape, lambda b, g: (0, 0))],
        out_specs=[pl.BlockSpec((None, None, HEAD_DIM, S * nh), lambda b, g: (b, g, 0, 0)),
                   pl.BlockSpec((None, None, FEAT, S), lambda b, g: (b, g, 0, 0))],
        out_shape=[jax.ShapeDtypeStruct((B, G, HEAD_DIM, S * nh), BF16),
                   jax.ShapeDtypeStruct((B, G, FEAT, S), BF16)],
        scratch_shapes=[pltpu.VMEM((FEAT, tq), F32)],
        compiler_params=pltpu.CompilerParams(dimension_semantics=("parallel", "parallel"),
                                             vmem_limit_bytes=VMEM_LIMIT),
        name="cmp_attn",
    )(nq_t, kvc, kvc, ovt)


def _moba_gate_kernel(q_ref, k_ref, a_ref, bias_ref, sc_ref, *, nb, ntop):
    nbp, S = sc_ref.shape[1:]
    km = _dot(a_ref[...], k_ref[...])
    lane = lax.broadcasted_iota(jnp.int32, (nbp, LANES), 1)
    parts = []
    for hh in range(2):
        parts += _hilo(jnp.where(lane // HEAD_DIM == hh, km, 0.0))
    g = _dot(jnp.concatenate(parts, axis=0), q_ref[...])
    past = (lax.broadcasted_iota(jnp.int32, (nbp, S), 0)
            < lax.broadcasted_iota(jnp.int32, (nbp, S), 1) // MOBA_BLOCK)
    n = lax.broadcasted_iota(jnp.int32, (nbp, MOBA_BLOCK), 0)
    for hh in range(2):
        sc = sc_ref.at[hh]
        sc[...] = jnp.where(past, g[2 * hh * nbp:(2 * hh + 1) * nbp] + g[(2 * hh + 1) * nbp:(2 * hh + 2) * nbp], NEG)
        for c in range(nb):
            cols = slice(c * MOBA_BLOCK, (c + 1) * MOBA_BLOCK)
            sel = n <= c
            if c > ntop:
                nr = -(-c // SUBLANES) * SUBLANES
                rank = _ranks(sc, c, nr, cols)
                rank = rank + jnp.where(sc[0:nr, cols] < NEG, nb - c, 0)
                if nr < nbp:
                    rank = jnp.concatenate([rank, jnp.full((nbp - nr, MOBA_BLOCK), nb, jnp.int32)], axis=0)
                sel = ((rank < ntop) & (n < c)) | (n == c)
            bias_ref[hh, 0:nbp, cols] = jnp.where(sel, 0.0, NEG).astype(bias_ref.dtype)
        bias_ref[hh, nbp:FEAT, :] = jnp.zeros((FEAT - nbp, S), bias_ref.dtype)


def _moba_gate(mq_t, mk, avg, *, B, S):
    H = MOBA_HEADS
    nb = S // MOBA_BLOCK
    nbp = avg.shape[0]
    assert nb <= nbp and S % MOBA_BLOCK == 0 and MOBA_BLOCK % LANES == 0 and H % 2 == 0
    ntop = min(MOBA_TOP_K, nb)
    return pl.pallas_call(
        functools.partial(_moba_gate_kernel, nb=nb, ntop=ntop),
        grid=(B, H // 2),
        in_specs=[pl.BlockSpec((2 * HEAD_DIM, S), lambda b, h: (h, b)),
                  pl.BlockSpec((S, LANES), lambda b, h: (b, h)),
                  pl.BlockSpec(avg.shape, lambda b, h: (0, 0))],
        out_specs=pl.BlockSpec((None, 2, FEAT, S), lambda b, h: (b, h, 0, 0)),
        out_shape=jax.ShapeDtypeStruct((B, H, FEAT, S), BF16),
        scratch_shapes=[pltpu.VMEM((2, nbp, S), F32)],
        compiler_params=pltpu.CompilerParams(dimension_semantics=("parallel", "parallel"),
                                             vmem_limit_bytes=VMEM_LIMIT),
        name="moba_gate",
    )(mq_t, mk, avg)


def _flash_q_tile(qi, running_max, q_ref, bias_ref, mask_ref, o_ref, ka_ref, vta_ref, kn_ref, qa_ref, bnd_ref,
                  m_ref, acc_ref, alpha_ref, p_ref, *, tq, tk, nh, pair, window, has_past=None):
    u = pl.program_id(1)
    R = nh * tq
    static = isinstance(qi, int)
    q0 = qi * tq if static else pl.multiple_of(qi * tq, tq)
    toks = pl.ds(q0, tq)

    qcat = jnp.concatenate([q_ref[hh * HEAD_DIM:(hh + 1) * HEAD_DIM, toks] for hh in range(nh)], axis=1)
    qf = qcat.astype(F32)
    qq = jnp.concatenate([qf, qf], axis=0)
    rowi = lax.broadcasted_iota(jnp.int32, (2 * HEAD_DIM, R), 0)
    coli = lax.broadcasted_iota(jnp.int32, (2 * HEAD_DIM, R), 1)
    half = (coli // tq) % 2 if pair else u % 2
    qa_ref[0:2 * HEAD_DIM, :] = jnp.where(rowi // HEAD_DIM == half, qq, 0.0).astype(BF16)
    if bias_ref is None:
        qa_ref[2 * HEAD_DIM:2 * HEAD_DIM + FEAT, :] = jnp.zeros((FEAT, R), BF16)
    else:
        qa_ref[2 * HEAD_DIM:2 * HEAD_DIM + FEAT, :] = jnp.concatenate(
            [bias_ref[hh if pair else 0, :, toks] for hh in range(nh)], axis=1)
    qa_ref[2 * HEAD_DIM + FEAT:2 * LANES, :] = _alibi_query_rows(q0, tq, u * nh, R)
    if pair:
        kn = jnp.concatenate([kn_ref[hh:hh + 1, :] for hh in range(nh) for _ in range(tq // LANES)], axis=1)
    else:
        kn = jnp.concatenate([jnp.where(u % 2 == 0, kn_ref[0:1, :], kn_ref[1:2, :])] * (R // LANES), axis=1)
    qn = jnp.sqrt(jnp.sum(qf * qf, axis=0, keepdims=True))
    bnd_ref[...] = jnp.broadcast_to(qn * kn, bnd_ref.shape)

    ncg = R // CG
    kd = (q0 + tq - 1) // tk

    def key_rows(kind, c):
        off = (c * CG) % tq
        if kind is None:
            return 0, tk, None
        if kind == "diag":
            return 0, off + CG, off // CG
        return off, tk, (tk + off) // CG

    def values(ki, kind=None):
        for c in range(ncg):
            cols = pl.ds(c * CG, CG)
            r0, r1, _ = key_rows(kind, c)
            v = (c * CG) // tq if pair else 0
            pv = _dot(vta_ref[v, ki, :, r0:r1], p_ref[r0:r1, cols])
            if running_max:
                acc_ref[:, cols] = alpha_ref[0:1, cols] * acc_ref[:, cols] + pv
            else:
                acc_ref[:, cols] = acc_ref[:, cols] + pv

    def scores(ki, kind=None, prev=None, prev_kind=None):
        k0 = ki * tk if isinstance(ki, int) else pl.multiple_of(ki * tk, tk)
        s_groups = []
        for c in range(ncg):
            r0, r1, _ = key_rows(kind, c)
            kb = (c * CG) // (2 * tq) if pair else 0
            s_groups.append(_dot(ka_ref[kb, pl.ds(k0 + r0, r1 - r0), :], qa_ref[:, pl.ds(c * CG, CG)]))
        if prev is not None:
            values(prev, prev_kind)
        for c, s in enumerate(s_groups):
            cols = pl.ds(c * CG, CG)
            r0, r1, entry = key_rows(kind, c)
            if entry is not None:
                s = s + mask_ref[entry, r0:r1, :]
            if running_max:
                m_prev = m_ref[0:1, cols]
                m_new = jnp.maximum(m_prev, jnp.max(s, axis=0, keepdims=True))
                alpha_ref[:, cols] = jnp.broadcast_to(jnp.exp(m_prev - m_new), (alpha_ref.shape[0], CG))
                m_ref[:, cols] = jnp.broadcast_to(m_new, (m_ref.shape[0], CG))
                p_ref[r0:r1, cols] = jnp.exp(s - m_new).astype(BF16)
            else:
                p_ref[r0:r1, cols] = jnp.exp(s - bnd_ref[0:1, cols]).astype(BF16)

    def steady(lo, hi):
        def pair_body(j, carry):
            ki = lo + 2 * j
            scores(ki, prev=ki - 1)
            scores(ki + 1, prev=ki)
            return carry
        lax.fori_loop(0, (hi - lo) // 2, pair_body, 0)

        @pl.when((hi - lo) % 2 == 1)
        def _odd_one():
            scores(hi - 1, prev=hi - 2)

    def only_diagonal():
        scores(0, "diag")
        values(0, "diag")

    def past_then_diagonal():
        if window is not None:
            scores(kd - 1, "wedge")
            scores(kd, "diag", prev=kd - 1, prev_kind="wedge")
        elif static:
            scores(0)
            for ki in range(1, kd):
                scores(ki, prev=ki - 1)
            scores(kd, "diag", prev=kd - 1)
        else:
            scores(0)
            steady(1, kd)
            scores(kd, "diag", prev=kd - 1)
        values(kd, "diag")

    acc_ref[...] = jnp.zeros(acc_ref.shape, F32)
    if running_max:
        m_ref[...] = jnp.full(m_ref.shape, NEG, F32)
    if static:
        only_diagonal() if kd == 0 else past_then_diagonal()
    elif has_past:
        past_then_diagonal()
    else:
        pl.when(kd == 0)(only_diagonal)
        pl.when(kd > 0)(past_then_diagonal)

    acc = acc_ref[...]
    out = (acc[0:HEAD_DIM] / acc[HEAD_DIM:HEAD_DIM + 1]).astype(o_ref.dtype)
    if pair:
        for hh in range(nh):
            o_ref[hh, :, toks] = out[:, hh * tq:(hh + 1) * tq]
    else:
        o_ref[0, :, pl.ds(qi * R if static else pl.multiple_of(qi * R, R), R)] = out
    return jnp.min(acc[HEAD_DIM:HEAD_DIM + 1, :])


def _flash_kernel(*refs, tq, tk, nh, pair, window, has_bias):
    q_ref = refs[0]
    bias_ref = refs[1] if has_bias else None
    (k_ref, kf_ref, vt_ref, mask_ref, o_ref, ka_ref, vta_ref, kn_ref, qa_ref, bnd_ref, m_ref, acc_ref,
     alpha_ref, p_ref, low_ref) = refs[2 if has_bias else 1:]
    S = k_ref.shape[0]
    for kb in range(ka_ref.shape[0]):
        k = k_ref[:, kb * LANES:(kb + 1) * LANES]
        ka_ref[kb, :, 0:LANES] = k
        ka_ref[kb, :, LANES:2 * LANES] = kf_ref[...]
        k2 = jnp.square(k.astype(F32))
        lane = lax.broadcasted_iota(jnp.int32, k2.shape, 1)
        for hf in range(2):
            n2 = jnp.sum(jnp.where(lane // HEAD_DIM == hf, k2, 0.0), axis=1, keepdims=True)
            kn_ref[2 * kb + hf:2 * kb + hf + 1, :] = jnp.broadcast_to(
                jnp.sqrt(jnp.max(n2, axis=0, keepdims=True)), (1, LANES))
    ones_row = jnp.where(lax.broadcasted_iota(jnp.int32, (VROWS - HEAD_DIM, tk), 0) == 0, 1.0, 0.0)
    for v in range(nh if pair else 1):
        for jt in range(S // tk):
            vta_ref[v, jt, 0:HEAD_DIM, :] = vt_ref[v * HEAD_DIM:(v + 1) * HEAD_DIM, jt * tk:(jt + 1) * tk]
            vta_ref[v, jt, HEAD_DIM:VROWS, :] = ones_row.astype(BF16)

    tile = functools.partial(_flash_q_tile, q_ref=q_ref, bias_ref=bias_ref, mask_ref=mask_ref, o_ref=o_ref,
                             ka_ref=ka_ref, vta_ref=vta_ref, kn_ref=kn_ref, qa_ref=qa_ref, bnd_ref=bnd_ref,
                             m_ref=m_ref, acc_ref=acc_ref, alpha_ref=alpha_ref, p_ref=p_ref,
                             tq=tq, tk=tk, nh=nh, pair=pair, window=window)
    nq = S // tq
    low_ref[0] = tile(0, False)

    def bound_stabilised(qi, carry):
        low_ref[qi] = tile(qi, False, has_past=True)
        return carry
    lax.fori_loop(1, nq, bound_stabilised, 0)

    def redo(qi, carry):
        @pl.when(low_ref[qi] < MIN_WEIGHT_SUM)
        def _running_max():
            tile(qi, True)
        return carry

    lax.fori_loop(0, nq, redo, 0)


def _mask_table(tk, window):
    n_delta = (tk + (window or 0)) // CG
    d = np.arange(n_delta)[:, None, None] * CG + np.arange(CG)[None, None, :] - np.arange(tk)[None, :, None]
    ok = (d >= 0) & ((d < window) if window else True)
    return jnp.asarray(np.where(ok, 0.0, NEG), F32)


def _flash(q_t, bias_t, k, kfeat, v_t, *, B, S, units, tq, nh, pair, window):
    tk = TK
    R = nh * tq
    nq = S // tq
    nv = nh if pair else 1
    nb = nh if pair else 1
    nkb = nh // 2 if pair else 1
    assert tq == tk and S % tk == 0 and tk % CG == 0 and window in (None, tk)
    mask = _mask_table(tk, window)
    kcol = (lambda u: u) if pair else (lambda u: u // 2)
    out_w = tq if pair else R
    out_shape = (B, units * nb, HEAD_DIM, S if pair else S * nh)
    has_bias = bias_t is not None
    in_specs = [pl.BlockSpec((nh * HEAD_DIM, S), lambda b, u: (u, b))]
    if has_bias:
        in_specs.append(pl.BlockSpec((None, nb, FEAT, S), lambda b, u: (b, u, 0, 0)))
    in_specs += [pl.BlockSpec((S, nkb * LANES), lambda b, u: (b, kcol(u))),
                 pl.BlockSpec((S, LANES), lambda b, u: (0, 0)),
                 pl.BlockSpec((nv * HEAD_DIM, S), lambda b, u: (u, b)),
                 pl.BlockSpec(mask.shape, lambda b, u: (0, 0, 0))]
    args = (q_t,) + ((bias_t,) if has_bias else ()) + (k, kfeat, v_t, mask)
    return pl.pallas_call(
        functools.partial(_flash_kernel, tq=tq, tk=tk, nh=nh, pair=pair, window=window, has_bias=has_bias),
        grid=(B, units),
        in_specs=in_specs,
        out_specs=pl.BlockSpec((None, nb, HEAD_DIM, out_w * nq), lambda b, u: (b, u, 0, 0)),
        out_shape=jax.ShapeDtypeStruct(out_shape, BF16),
        scratch_shapes=[pltpu.VMEM((nkb, S, 2 * LANES), BF16),
                        pltpu.VMEM((nv, S // tk, VROWS, tk), BF16),
                        pltpu.VMEM((SUBLANES, LANES), F32),
                        pltpu.VMEM((2 * LANES, R), BF16),
                        pltpu.VMEM((SUBLANES, R), F32),
                        pltpu.VMEM((SUBLANES, R), F32),
                        pltpu.VMEM((VROWS, R), F32),
                        pltpu.VMEM((SUBLANES, R), F32),
                        pltpu.VMEM((tk, R), BF16),
                        pltpu.SMEM((nq,), F32)],
        compiler_params=pltpu.CompilerParams(dimension_semantics=("parallel", "parallel"),
                                             vmem_limit_bytes=VMEM_LIMIT),
        name="flash",
    )(*args)


def _even_out_kernel(x_ref, ocmp_ref, osel_ref, owin_ref, om_ref, gate_ref, nz_ref, mz_ref,
                     w_ref, g_ref, b_ref, o_ref, sig_ref, *, alpha, tm):
    G, nh = NSA_KV_GROUPS, NSA_HG
    sig_ref[...] = _sigmoid(gate_ref[...])
    heads = []
    for g in range(G):
        for hh in range(nh):
            acc = None
            for br, ref in enumerate((ocmp_ref, osel_ref, owin_ref)):
                gate = sig_ref[pl.ds(3 * (g * nh + hh) + br, 1), :]
                term = gate * ref[g, :, hh * tm:(hh + 1) * tm].astype(F32)
                acc = term if acc is None else acc + term
            heads.append(acc)
    o_nsa = jnp.concatenate(heads, axis=0)
    nz = nz_ref[...].astype(F32)
    mz = mz_ref[...].astype(F32)
    o_m = jnp.concatenate([om_ref[h] for h in range(MOBA_HEADS)], axis=0).astype(F32)
    a_t = jnp.concatenate([o_nsa * (nz * _sigmoid(nz)), o_m * (mz * _sigmoid(mz))], axis=0)
    y = _dot_tn(a_t.astype(BF16), w_ref[...])
    o_ref[...] = _layer_norm(alpha * x_ref[...] + y, g_ref[...], b_ref[...])


def _even_out(x2d, ocmp, osel, owin, om, gate_t, nz_t, mz_t, w_out, ln_g, ln_b, alpha, *, B, S):
    T, D = x2d.shape
    tm = TQ
    nq = S // tm
    G, nh, H = NSA_KV_GROUPS, NSA_HG, MOBA_HEADS
    nsa = pl.BlockSpec((None, G, HEAD_DIM, nh * tm), lambda i: (i // nq, 0, 0, i % nq))
    feat = lambda n: pl.BlockSpec((n, tm), lambda i: (0, i))
    full = lambda a: pl.BlockSpec(a.shape, lambda i: (0,) * a.ndim)
    return pl.pallas_call(
        functools.partial(_even_out_kernel, alpha=alpha, tm=tm),
        grid=(T // tm,),
        in_specs=[pl.BlockSpec((tm, D), lambda i: (i, 0)), nsa, nsa, nsa,
                  pl.BlockSpec((None, H, HEAD_DIM, tm), lambda i: (i // nq, 0, 0, i % nq)),
                  feat(GATE_ROWS), feat(NSA_W), feat(MOBA_W),
                  full(w_out), full(ln_g), full(ln_b)],
        out_specs=pl.BlockSpec((tm, D), lambda i: (i, 0)),
        out_shape=jax.ShapeDtypeStruct((T, D), F32),
        scratch_shapes=[pltpu.VMEM((GATE_ROWS, tm), F32)],
        compiler_params=pltpu.CompilerParams(dimension_semantics=("parallel",),
                                             vmem_limit_bytes=VMEM_LIMIT),
        name="even_out",
    )(x2d, ocmp, osel, owin, om, gate_t, nz_t, mz_t, w_out, ln_g, ln_b)


def _gla_head_exact(row0, h, q_ref, v_ref, o_ref, st_ref, b_ref, kf_ref, scale):
    C, dk, dv, SUB = GLA_CHUNK, GLA_DK, GLA_DV, GLA_SUB
    rows = lax.broadcasted_iota(jnp.int32, (C, dk), 0)
    sub_i = lax.broadcasted_iota(jnp.int32, (SUB, LANES), 0)
    lane = lax.broadcasted_iota(jnp.int32, (SUB, LANES), 1)
    ones = jnp.ones((dk, LANES), BF16)
    sl = pl.ds(h * dk, dk)
    rs_c = pl.ds(row0, C)
    bh = b_ref[:, sl]
    kf = kf_ref[:, sl]
    qf = q_ref[rs_c, sl].astype(F32) * scale
    vh = v_ref[rs_c, pl.ds(h * dv, dv)]
    st = st_ref[h]
    inter = _dot_nt((qf * jnp.exp(bh)).astype(BF16), st.astype(BF16))
    bl = b_ref[pl.ds(C - 1, 1), sl]
    kd = kf * jnp.exp(bl - bh)

    def off_block(r0, r1, c0, c1):
        ref_row = b_ref[pl.ds(r0, 1), sl]
        qt = qf[r0:r1] * jnp.exp(bh[r0:r1] - ref_row)
        ek = jnp.where((rows >= c0) & (rows < c1), ref_row - bh, NEG)
        kt = kf * jnp.exp(ek)
        return _dot_nt(qt.astype(BF16), kt.astype(BF16))

    lower = off_block(2 * SUB, C, 0, 2 * SUB)
    blocks = [None, off_block(SUB, 2 * SUB, 0, SUB), lower[0:SUB],
              lower[SUB:2 * SUB] + off_block(3 * SUB, C, 2 * SUB, 3 * SUB)]
    pieces = []
    for blk in range(C // SUB):
        r0 = blk * SUB
        q_i = qf[r0:r0 + SUB]
        b_i = bh[r0:r0 + SUB]
        for jj in range(SUB):
            b_j = b_ref[pl.ds(r0 + jj, 1), sl]
            k_j = kf_ref[pl.ds(r0 + jj, 1), sl]
            pieces.append((q_i * k_j * jnp.exp(jnp.minimum(b_i - b_j, 0.0))).astype(BF16))
    sums = _dot(jnp.concatenate(pieces, axis=0), ones)
    for blk in range(C // SUB):
        r0 = blk * SUB
        diag = jnp.zeros((SUB, LANES), F32)
        for jj in range(SUB):
            rs = sums[(r0 + jj) * SUB:(r0 + jj + 1) * SUB]
            diag = jnp.where((lane == r0 + jj) & (sub_i >= jj), rs, diag)
        diag = diag[:, 0:C]
        blocks[blk] = diag if blocks[blk] is None else blocks[blk] + diag
    att = jnp.concatenate(blocks, axis=0)
    o_ref[rs_c, pl.ds(h * dv, dv)] = (inter + _dot(att.astype(BF16), vh)).astype(o_ref.dtype)
    st_ref[h] = st * jnp.exp(bl) + _dot_tn(vh, kd.astype(BF16))


def _gla_head_bounded(row0, h, q_ref, v_ref, o_ref, st_ref, b_ref, kf_ref, scale):
    C, dk, dv = GLA_CHUNK, GLA_DK, GLA_DV
    sl = pl.ds(h * dk, dk)
    rs_c = pl.ds(row0, C)
    bh = b_ref[rs_c, sl]
    kf = kf_ref[rs_c, sl]
    qf = q_ref[rs_c, sl].astype(F32) * scale
    vh = v_ref[rs_c, pl.ds(h * dv, dv)]
    st = st_ref[h]
    qs = (qf * jnp.exp(bh)).astype(BF16)
    kt = (kf * jnp.exp(-bh)).astype(BF16)
    i = lax.broadcasted_iota(jnp.int32, (C, C), 0)
    j = lax.broadcasted_iota(jnp.int32, (C, C), 1)
    att = jnp.where(i >= j, _dot_nt(qs, kt), 0.0)
    inter = _dot_nt(qs, st.astype(BF16))
    o_ref[rs_c, pl.ds(h * dv, dv)] = (inter + _dot(att.astype(BF16), vh)).astype(o_ref.dtype)
    bl = b_ref[pl.ds(row0 + C - 1, 1), sl]
    kd = kf * jnp.exp(bl - bh)
    st_ref[h] = st * jnp.exp(bl) + _dot_tn(vh, kd.astype(BF16))


def _gla_kernel(q_ref, k_ref, v_ref, glr_ref, w2_ref, gb_ref, tril_ref, o_ref,
                st_ref, b_ref, kf_ref, bc_ref, kc_ref, *, scale):
    C, H = GLA_CHUNK, GLA_HEADS
    nch = q_ref.shape[0] // C

    @pl.when(pl.program_id(1) == 0)
    def _init():
        st_ref[...] = jnp.zeros(st_ref.shape, F32)

    whi, wlo = _hilo(w2_ref[...])
    part = tril_ref.shape[0]
    decay = None
    for r0 in range(0, nch * C, part):
        rs = pl.ds(r0, part)
        ghi, glo = _hilo(glr_ref[rs, :])
        pre = _dot(ghi, whi) + _dot(glo, whi) + _dot(ghi, wlo) + gb_ref[...]
        la = (jnp.minimum(pre, 0.0) - jnp.log(1.0 + jnp.exp(-jnp.abs(pre)))) * (1.0 / GLA_TAU)
        lhi, llo = _hilo(la)
        b_part = _dot(tril_ref[...], lhi) + _dot(tril_ref[...], llo)
        b_ref[rs, :] = b_part
        for c in range(part // C):
            d_c = -b_part[c * C + C - 1:(c + 1) * C, :]
            decay = d_c if decay is None else jnp.maximum(decay, d_c)
    kf_ref[...] = k_ref[...].astype(F32)
    bounded = jnp.max(decay) <= GLA_FACTOR_MAX_DECAY

    @pl.when(bounded)
    def _factored():
        for c in range(nch):
            for h in range(H):
                _gla_head_bounded(c * C, h, q_ref, v_ref, o_ref, st_ref, b_ref, kf_ref, scale)

    @pl.when(jnp.logical_not(bounded))
    def _exact():
        def chunk(c, carry):
            row0 = pl.multiple_of(c * C, C)
            bc_ref[...] = b_ref[pl.ds(row0, C), :]
            kc_ref[...] = kf_ref[pl.ds(row0, C), :]
            for h in range(H):
                _gla_head_exact(row0, h, q_ref, v_ref, o_ref, st_ref, bc_ref, kc_ref, scale)
            return carry
        lax.fori_loop(0, nch, chunk, 0)


def _gla(q, k, v, glr, w2p, gb, tril, *, B, S):
    C, H, dk, dv = GLA_CHUNK, GLA_HEADS, GLA_DK, GLA_DV
    rows = C * min(GLA_STEP_CHUNKS, S // C)
    nchunk = S // rows
    tok = lambda n: pl.BlockSpec((rows, n), lambda b, c: (b * nchunk + c, 0))
    const = lambda shape: pl.BlockSpec(shape, lambda b, c: (0,) * len(shape))
    return pl.pallas_call(
        functools.partial(_gla_kernel, scale=float(dk) ** -0.5),
        grid=(B, nchunk),
        in_specs=[tok(H * dk), tok(H * dk), tok(H * dv), tok(LANES),
                  const(w2p.shape), const(gb.shape), const(tril.shape)],
        out_specs=tok(H * dv),
        out_shape=jax.ShapeDtypeStruct((B * S, H * dv), BF16),
        scratch_shapes=[pltpu.VMEM((H, dv, dk), F32),
                        pltpu.VMEM((rows, H * dk), F32),
                        pltpu.VMEM((rows, H * dk), F32),
                        pltpu.VMEM((C, H * dk), F32),
                        pltpu.VMEM((C, H * dk), F32)],
        compiler_params=pltpu.CompilerParams(dimension_semantics=("parallel", "arbitrary"),
                                             vmem_limit_bytes=VMEM_LIMIT),
        name="gla",
    )(q, k, v, glr, w2p, gb, tril)


def _odd_out_kernel(x_ref, o_ref_in, z_ref, gn_ref, w_ref, g_ref, b_ref, out_ref, *, alpha):
    H, dv = GLA_HEADS, GLA_DV
    parts = []
    for h in range(H):
        sl = pl.ds(h * dv, dv)
        oh = o_ref_in[:, sl].astype(F32)
        r = lax.rsqrt(jnp.mean(oh * oh, axis=-1, keepdims=True) + LN_EPS)
        z = z_ref[:, sl].astype(F32)
        parts.append((oh * r * gn_ref[:, sl] * (z * _sigmoid(z))).astype(BF16))
    y = _dot(jnp.concatenate(parts, axis=1), w_ref[...])
    out_ref[...] = _layer_norm(alpha * x_ref[...] + y, g_ref[...], b_ref[...])


def _odd_out(x2d, o, z, gn_g, w_out, ln_g, ln_b, alpha, tm=1024):
    T, D = x2d.shape
    W = o.shape[1]
    row = lambda n: pl.BlockSpec((tm, n), lambda i: (i, 0))
    return pl.pallas_call(
        functools.partial(_odd_out_kernel, alpha=alpha),
        grid=(T // tm,),
        in_specs=[row(D), row(W), row(W),
                  pl.BlockSpec((1, W), lambda i: (0, 0)),
                  pl.BlockSpec((W, D), lambda i: (0, 0)),
                  pl.BlockSpec((1, D), lambda i: (0, 0)),
                  pl.BlockSpec((1, D), lambda i: (0, 0))],
        out_specs=row(D),
        out_shape=jax.ShapeDtypeStruct((T, D), F32),
        compiler_params=pltpu.CompilerParams(dimension_semantics=("parallel",),
                                             vmem_limit_bytes=VMEM_LIMIT),
        name="odd_out",
    )(x2d, o, z, gn_g, w_out, ln_g, ln_b)


PACK = 2 * SUBLANES


def _pack_rows_kernel(w_ref, *o_refs, plans):
    for o_ref, plan in zip(o_refs, plans):
        off = 0
        for src, n, scale, n_out in plan:
            n_load = min(-(-n // PACK) * PACK, n_out)
            rows = w_ref[src:src + n_load, :]
            if scale != 1.0:
                rows = rows * scale
            if n_load != n:
                rows = jnp.where(lax.broadcasted_iota(jnp.int32, rows.shape, 0) < n, rows, 0.0)
            o_ref[off:off + n_load, :] = rows.astype(BF16)
            if n_out > n_load:
                o_ref[off + n_load:off + n_out, :] = jnp.zeros((n_out - n_load, o_ref.shape[1]), BF16)
            off += n_out


def _pack_rows(w_t, plans, tc=256):
    W, D = w_t.shape
    plans = tuple(tuple((p + (p[1],))[:4] for p in plan) for plan in plans)
    for plan in plans:
        for src, n, _, n_out in plan:
            assert src % SUBLANES == 0 and n_out % PACK == 0 and n <= n_out and src + n_out <= W
    assert D % tc == 0
    rows = [sum(p[3] for p in plan) for plan in plans]
    return pl.pallas_call(
        functools.partial(_pack_rows_kernel, plans=plans),
        grid=(D // tc,),
        in_specs=[pl.BlockSpec((W, tc), lambda j: (0, j))],
        out_specs=[pl.BlockSpec((r, tc), lambda j: (0, j)) for r in rows],
        out_shape=[jax.ShapeDtypeStruct((r, D), BF16) for r in rows],
        compiler_params=pltpu.CompilerParams(dimension_semantics=("parallel",),
                                             vmem_limit_bytes=VMEM_LIMIT),
        name="pack_rows",
    )(w_t)


def _key_features(S, block):
    s = np.arange(S)
    f = np.zeros((S, LANES), np.float32)
    f[s, s // block] = 1.0
    f[:, FEAT + 0] = 1.0
    f[:, FEAT + 1] = 1.0
    f[:, FEAT + 2] = FEAT * (s // FEAT)
    f[:, FEAT + 3] = s % FEAT
    return jnp.asarray(f, BF16)


def _pad_to(a, n, axis):
    pad = [(0, 0)] * a.ndim
    pad[axis] = (0, n - a.shape[axis])
    return jnp.pad(a, pad)


def _even_layer(x2d, B, S, w_in, cmp_pos, cmp_w1, cmp_b1, cmp_w2, w_out, ln_g, ln_b, alpha):
    T, D = x2d.shape
    G, hd = NSA_KV_GROUPS, HEAD_DIM
    scale = hd ** -0.5
    kvw, (oq, okv, og, onz, omq, omk, omv, omz) = NSA_KV_W, np.cumsum((0,) + EVEN_SPLITS[:-1]).tolist()
    okc, ovc, oks, ovs, okw, ovw = (okv + i * kvw for i in range(6))
    w_tok, w_tr = _pack_rows(w_in.T, (
        ((okc, kvw, 1.0), (ovc, kvw, 1.0), (oks, kvw, 1.0), (okw, kvw, 1.0), (omk, MOBA_W, 1.0)),
        ((oq, NSA_W, scale), (ovs, kvw, 1.0), (ovw, kvw, 1.0), (omq, MOBA_W, scale), (omv, MOBA_W, 1.0),
         (onz, NSA_W, 1.0), (omz, MOBA_W, 1.0), (og, 3 * NSA_HEADS, 1.0, GATE_ROWS))))
    (kcmp, vcmp, ksel, kwin, mk, nq_t, vsel_t, vwin_t, mq_t, mv_t, nz_t, mz_t, gate_t) = _proj(
        x2d, w_tok, (NSA_KV_W,) * 4 + (MOBA_W,), (F32, F32, BF16, BF16, BF16),
        w_tr, (NSA_W, NSA_KV_W, NSA_KV_W, MOBA_W, MOBA_W, NSA_W, MOBA_W, GATE_ROWS),
        (BF16,) * 7 + (F32,))

    L, d, HID = NSA_CMP_BLOCK, NSA_CMP_STRIDE, NSA_CMP_HIDDEN
    ncmp = (S - L) // d + 1
    NC = S // d
    def group_diagonal(w, axis):
        z = jnp.zeros_like(w)
        return jnp.stack([jnp.concatenate([w if h == g else z for h in range(G)], axis=-1) for g in range(G)], axis=axis)

    w1r = cmp_w1.reshape(2, 2, d, hd, HID).astype(BF16)
    w1p = group_diagonal(w1r, 3).reshape(2, 2, d * G * hd, G * HID)
    posp = jnp.broadcast_to(cmp_pos.reshape(2, 2, d, 1, hd), (2, 2, d, G, hd)).reshape(2, 2, 1, d * G * hd)
    b1p = jnp.tile(cmp_b1, (1, G)).reshape(2, 1, G * HID)
    w2p = group_diagonal(_pad_to(cmp_w2, LANES, 2).astype(BF16), 1).reshape(2, G * HID, G * LANES)
    cfeat = np.zeros((NC, G, LANES), np.float32)
    cfeat[:, :, hd + 0] = 1.0
    cfeat[:, :, hd + 1] = 1.0
    cfeat[:, :, hd + 2] = (np.arange(NC) * d)[:, None]
    cfeat[:, :, hd + 3] = L - 1
    kvc = _cmp_mlp(kcmp, vcmp, w1p, posp, b1p, w2p, jnp.asarray(cfeat.reshape(NC, G * LANES)), ncmp,
                   B=B, S=S)

    ns = S // NSA_SEL_BLOCK
    cmp_start = np.arange(NC) * d
    sel_start = np.arange(FEAT) * NSA_SEL_BLOCK
    ovt = ((cmp_start[None, :] <= sel_start[:, None] + NSA_SEL_BLOCK - 1)
           & (cmp_start[None, :] + L - 1 >= sel_start[:, None])
           & (np.arange(NC)[None, :] < ncmp) & (np.arange(FEAT)[:, None] < ns))
    o_cmp, bias_sel = _cmp_attn(nq_t, kvc, jnp.asarray(ovt, BF16), B=B, S=S)

    feats_sel = _key_features(S, NSA_SEL_BLOCK)
    o_sel = _flash(nq_t, bias_sel, ksel, feats_sel, vsel_t,
                   B=B, S=S, units=G, tq=TQ, nh=NSA_HG, pair=False, window=None)
    o_win = _flash(nq_t, None, kwin, feats_sel, vwin_t,
                   B=B, S=S, units=G, tq=TQ, nh=NSA_HG, pair=False, window=NSA_WINDOW)

    avg = np.zeros((MOBA_NBP, S), np.float32)
    avg[np.arange(S) // MOBA_BLOCK, np.arange(S)] = 1.0 / MOBA_BLOCK
    bias_m = _moba_gate(mq_t, mk, jnp.asarray(avg, BF16), B=B, S=S)
    o_m = _flash(mq_t, bias_m, mk, _key_features(S, MOBA_BLOCK), mv_t,
                 B=B, S=S, units=MOBA_HEADS // MOBA_STEP_HEADS, tq=TQM, nh=MOBA_STEP_HEADS, pair=True, window=None)

    return _even_out(x2d, o_cmp, o_sel, o_win, o_m, gate_t, nz_t, mz_t, w_out.astype(BF16),
                     ln_g.reshape(1, D), ln_b.reshape(1, D), alpha, B=B, S=S)


def _odd_layer(x2d, B, S, w_in, gate_w2, gate_b, gn_g, w_out, ln_g, ln_b, alpha):
    T, D = x2d.shape
    H, dk, dv, C = GLA_HEADS, GLA_DK, GLA_DV, GLA_CHUNK
    n_qkv, n_gate, n_z = sum(ODD_SPLITS[:3]), ODD_SPLITS[3], ODD_SPLITS[4]
    w_perm, = _pack_rows(w_in.T, (((0, n_qkv, 1.0), (n_qkv + n_gate, n_z, 1.0), (n_qkv, n_gate, 1.0, LANES)),))
    q, k, v, z, glr = _proj(x2d, w_perm, (H * dk, H * dk, H * dv, H * dv, LANES),
                            (BF16, BF16, BF16, BF16, F32), tm=1024)
    w2p = _pad_to(gate_w2, LANES, 0)
    ncs = min(GLA_CUMSUM_CHUNKS, GLA_STEP_CHUNKS, S // C)
    tril = jnp.asarray(np.kron(np.eye(ncs), np.tril(np.ones((C, C)))), BF16)
    o = _gla(q, k, v, glr, w2p, gate_b.reshape(1, H * dk), tril, B=B, S=S)
    return _odd_out(x2d, o, z, gn_g.reshape(1, H * dv), w_out.astype(BF16),
                    ln_g.reshape(1, D), ln_b.reshape(1, D), alpha)


def kernel(x, ev_w_in, ev_cmp_pos, ev_cmp_w1, ev_cmp_b1, ev_cmp_w2, ev_w_out, ev_ln_g, ev_ln_b,
           od_w_in, od_gate_w2, od_gate_b, od_gn_g, od_w_out, od_ln_g, od_ln_b):
    B, S, D = x.shape
    depth = ev_w_in.shape[0] + od_w_in.shape[0]
    alpha = (2.0 * depth) ** 0.25
    h = x.reshape(B * S, D)
    for layer in range(depth):
        i = layer // 2
        if layer % 2 == 0:
            h = _even_layer(h, B, S, ev_w_in[i], ev_cmp_pos[i], ev_cmp_w1[i], ev_cmp_b1[i], ev_cmp_w2[i],
                            ev_w_out[i], ev_ln_g[i], ev_ln_b[i], alpha)
        else:
            h = _odd_layer(h, B, S, od_w_in[i], od_gate_w2[i], od_gate_b[i], od_gn_g[i],
                           od_w_out[i], od_ln_g[i], od_ln_b[i], alpha)
    return h.reshape(B, S, D)
```

```python
import functools

import numpy as np
import jax
import jax.numpy as jnp
from jax import lax
from jax.experimental import pallas as pl
from jax.experimental.pallas import tpu as pltpu

F32 = jnp.float32
BF16 = jnp.bfloat16

HEAD_DIM = 64
NSA_HEADS = 8
NSA_KV_GROUPS = 2
NSA_HG = NSA_HEADS // NSA_KV_GROUPS
NSA_CMP_BLOCK = 32
NSA_CMP_STRIDE = 16
NSA_CMP_HIDDEN = 128
NSA_SEL_BLOCK = 64
NSA_TOP_N = 16
NSA_WINDOW = 512
MOBA_HEADS = 8
MOBA_BLOCK = 256
MOBA_TOP_K = 3
GLA_HEADS = 4
GLA_DK = 128
GLA_DV = 256
GLA_GATE_RANK = 16
GLA_TAU = 16.0
GLA_CHUNK = 64
GLA_SUB = 16
GLA_STEP_CHUNKS = 16
GLA_CUMSUM_CHUNKS = 4
GLA_FACTOR_MAX_DECAY = 60.0
LN_EPS = 1e-5
NEG = -1e30
FORCE_BONUS = 1e4

NSA_W = NSA_HEADS * HEAD_DIM
NSA_KV_W = NSA_KV_GROUPS * HEAD_DIM
MOBA_W = MOBA_HEADS * HEAD_DIM
EVEN_SPLITS = (NSA_W, 6 * NSA_KV_W, 3 * NSA_HEADS, NSA_W, MOBA_W, MOBA_W, MOBA_W, MOBA_W)
ODD_SPLITS = (GLA_HEADS * GLA_DK, GLA_HEADS * GLA_DK, GLA_HEADS * GLA_DV, GLA_GATE_RANK, GLA_HEADS * GLA_DV)

LANES = 128
SUBLANES = 8
FEAT = 64
ALIBI_ROWS = 16
GATE_ROWS = 32
MOBA_NBP = 16
TQ = 512
TQM = 512
MOBA_STEP_HEADS = 4
TK = 512
CG = 256
MIN_WEIGHT_SUM = 1e-25
VROWS = HEAD_DIM + 16
VMEM_LIMIT = 48 * 1024 * 1024


def _dot(a, b):
    return jnp.dot(a, b, preferred_element_type=F32)


def _dot_nt(a, b):
    return lax.dot_general(a, b, (((1,), (1,)), ((), ())), preferred_element_type=F32)


def _dot_tn(a, b):
    return lax.dot_general(a, b, (((0,), (0,)), ((), ())), preferred_element_type=F32)


def _hilo(a):
    hi = a.astype(BF16)
    lo = (a - hi.astype(F32)).astype(BF16)
    return hi, lo


def _pow2_neg(n):
    return lax.bitcast_convert_type(lax.shift_left(127 - n, 23), F32)


def _sigmoid(x):
    return 0.5 * jnp.tanh(0.5 * x) + 0.5


def _silu(x):
    h = 0.5 * x
    return h * jnp.tanh(h) + h


def _layer_norm(r, g, b):
    mu = jnp.mean(r, axis=-1, keepdims=True)
    d = r - mu
    var = jnp.mean(d * d, axis=-1, keepdims=True)
    return d * lax.rsqrt(var + LN_EPS) * g + b


def _alibi_query_rows(t0, tq, head0, R):
    r = lax.broadcasted_iota(jnp.int32, (ALIBI_ROWS, R), 0)
    col = lax.broadcasted_iota(jnp.int32, (ALIBI_ROWS, R), 1)
    t = t0 + col % tq
    slope = _pow2_neg(head0 + col // tq + 1)
    feats = jnp.where(r == 0, -FEAT * slope * (t // FEAT).astype(F32),
                      jnp.where(r == 1, -slope * (t % FEAT).astype(F32), jnp.where(r < 4, slope, 0.0)))
    return jnp.concatenate([feats.astype(BF16), jnp.zeros((FEAT - ALIBI_ROWS, R), BF16)], axis=0)


def _ranks(sc_ref, n_real, n_rows=None, cols=slice(None)):
    n_rows = sc_ref.shape[0] if n_rows is None else n_rows
    out = []
    for v in range(n_rows // SUBLANES):
        lo = v * SUBLANES
        sc = sc_ref[lo:lo + SUBLANES, cols]
        width = sc.shape[1]
        idx = lo + lax.broadcasted_iota(jnp.int32, (SUBLANES, width), 0)
        rank = jnp.zeros((SUBLANES, width), jnp.int32)
        for jp in range(n_real):
            r = sc_ref[pl.ds(jp, 1), cols]
            if jp < lo:
                beats = r >= sc
            elif jp >= lo + SUBLANES - 1:
                beats = r > sc
            else:
                beats = (r > sc) | ((r == sc) & (idx > jp))
            rank = rank + jnp.where(beats, 1, 0)
        out.append(rank)
    return jnp.concatenate(out, axis=0)


def _proj_kernel(*refs, tok_splits, tr_splits):
    x_ref, wt_ref = refs[0], refs[1]
    outs = list(refs[3:] if tr_splits else refs[2:])
    x = x_ref[...]
    xb = x.astype(BF16)
    off = 0
    for n in tok_splits:
        o_ref = outs.pop(0)
        o_ref[...] = _dot_nt(xb, wt_ref[off:off + n, :]).astype(o_ref.dtype)
        off += n
    if tr_splits:
        wtr_ref = refs[2]
        xt = x.T.astype(BF16)
        off = 0
        for n in tr_splits:
            o_ref = outs.pop(0)
            o_ref[...] = _dot(wtr_ref[off:off + n, :], xt).astype(o_ref.dtype)
            off += n


def _proj(x2d, w_tok, tok_splits, tok_dtypes, w_tr=None, tr_splits=(), tr_dtypes=(), tm=512):
    T, D = x2d.shape
    assert T % tm == 0 and sum(tok_splits) == w_tok.shape[0]
    in_specs = [pl.BlockSpec((tm, D), lambda i: (i, 0)),
                pl.BlockSpec(w_tok.shape, lambda i: (0, 0))]
    args = [x2d, w_tok]
    if tr_splits:
        assert sum(tr_splits) == w_tr.shape[0]
        in_specs.append(pl.BlockSpec(w_tr.shape, lambda i: (0, 0)))
        args.append(w_tr)
    out_specs = ([pl.BlockSpec((tm, n), lambda i: (i, 0)) for n in tok_splits]
                 + [pl.BlockSpec((n, tm), lambda i: (0, i)) for n in tr_splits])
    out_shape = ([jax.ShapeDtypeStruct((T, n), dt) for n, dt in zip(tok_splits, tok_dtypes)]
                 + [jax.ShapeDtypeStruct((n, T), dt) for n, dt in zip(tr_splits, tr_dtypes)])
    return pl.pallas_call(
        functools.partial(_proj_kernel, tok_splits=tuple(tok_splits), tr_splits=tuple(tr_splits)),
        grid=(T // tm,), in_specs=in_specs, out_specs=out_specs, out_shape=out_shape,
        compiler_params=pltpu.CompilerParams(dimension_semantics=("parallel",),
                                             vmem_limit_bytes=VMEM_LIMIT),
        name="proj",
    )(*args)


def _cmp_mlp_kernel(kx_ref, vx_ref, w1_ref, pos_ref, b1_ref, w2_ref, cf_ref, o_ref, *, ncmp):
    d = NSA_CMP_STRIDE
    nc = kx_ref.shape[0] // d
    cw = w1_ref.shape[2]
    for i, x_ref in enumerate((kx_ref, vx_ref)):
        a = bm = None
        for l in range(0, d, 2):
            x2 = jnp.concatenate([x_ref[pl.ds(l, nc, stride=d), :], x_ref[pl.ds(l + 1, nc, stride=d), :]],
                                 axis=1).astype(BF16)
            rows = pl.ds(l * x_ref.shape[1], 2 * x_ref.shape[1])
            a_l, b_l = _dot(x2, w1_ref[i, 0, rows, :]), _dot(x2, w1_ref[i, 1, rows, :])
            a, bm = (a_l, b_l) if a is None else (a + a_l, bm + b_l)
        c = b1_ref[i]
        for half in range(2):
            phi, plo = _hilo(jnp.broadcast_to(pos_ref[i, half], (SUBLANES, cw)))
            c = c + (_dot(phi, w1_ref[i, half]) + _dot(plo, w1_ref[i, half]))[0:1]
        hid = a + pltpu.roll(bm, nc - 1, axis=0) + c
        hid = _silu(hid)
        row = lax.broadcasted_iota(jnp.int32, hid.shape, 0)
        hid = jnp.where(row < ncmp, hid, 0.0)
        out = _dot(hid.astype(BF16), w2_ref[i])
        if i == 0:
            out = out + cf_ref[...]
        o_ref[i] = out.astype(o_ref.dtype)


def _cmp_mlp(kx, vx, w1p, posp, b1p, w2p, cfeat, ncmp, *, B, S):
    NC = S // NSA_CMP_STRIDE
    OW = w2p.shape[-1]
    full = lambda a: pl.BlockSpec(a.shape, lambda b: (0,) * a.ndim)
    return pl.pallas_call(
        functools.partial(_cmp_mlp_kernel, ncmp=ncmp),
        grid=(B,),
        in_specs=[pl.BlockSpec((S, kx.shape[1]), lambda b: (b, 0)),
                  pl.BlockSpec((S, vx.shape[1]), lambda b: (b, 0)),
                  full(w1p), full(posp), full(b1p), full(w2p), full(cfeat)],
        out_specs=pl.BlockSpec((2, None, NC, OW), lambda b: (0, b, 0, 0)),
        out_shape=jax.ShapeDtypeStruct((2, B, NC, OW), BF16),
        compiler_params=pltpu.CompilerParams(dimension_semantics=("parallel",),
                                             vmem_limit_bytes=VMEM_LIMIT),
        name="cmp_mlp",
    )(kx, vx, w1p, posp, b1p, w2p, cfeat)


def _cmp_attn_kernel(q_ref, kc_ref, vc_ref, ovt_ref, o_ref, bias_ref, sc_ref, *, tq, nh, ns, ntop):
    g = pl.program_id(1)
    R = nh * tq
    nq = q_ref.shape[1] // tq

    def q_tile(qi):
        toks = pl.ds(qi * tq, tq)
        nc = min(kc_ref.shape[0], -(-((qi + 1) * tq) // (NSA_CMP_STRIDE * 16)) * 16)
        nsel = min(ns, -(-((qi + 1) * tq) // (NSA_SEL_BLOCK * SUBLANES)) * SUBLANES)
        qcat = jnp.concatenate([q_ref[hh * HEAD_DIM:(hh + 1) * HEAD_DIM, toks] for hh in range(nh)], axis=1)
        qa = jnp.concatenate([qcat, _alibi_query_rows(qi * tq, tq, g * nh, R)], axis=0)
        s = _dot(kc_ref[0:nc, :], qa)
        n = lax.broadcasted_iota(jnp.int32, (nc, R), 0)
        t_row = qi * tq + lax.broadcasted_iota(jnp.int32, (1, R), 1) % tq
        last = (t_row - (NSA_CMP_BLOCK - 1)) // NSA_CMP_STRIDE
        s = jnp.where(n <= last, s, NEG)
        e = jnp.exp(s - jnp.max(s, axis=0, keepdims=True))
        inv = jnp.where(last >= 0, 1.0 / jnp.sum(e, axis=0, keepdims=True), 0.0)
        p = e * inv
        o_ref[:, pl.ds(qi * R, R)] = _dot_tn(vc_ref[0:nc, 0:HEAD_DIM], p.astype(BF16)).astype(o_ref.dtype)

        psum = p[:, 0:tq]
        for hh in range(1, nh):
            psum = psum + p[:, hh * tq:(hh + 1) * tq]
        phi, plo = _hilo(psum)
        ovt = ovt_ref[0:nsel, 0:nc]
        imp = _dot(ovt, phi) + _dot(ovt, plo)

        j = lax.broadcasted_iota(jnp.int32, (nsel, tq), 0)
        cur = (qi * tq + lax.broadcasted_iota(jnp.int32, (nsel, tq), 1)) // NSA_SEL_BLOCK
        if nsel <= ntop:
            sel = j <= cur
        else:
            forced = jnp.where((j == 0) | (j == cur) | (j == cur - 1), FORCE_BONUS, 0.0)
            sc_ref[0:nsel, :] = jnp.where(j <= cur, imp + forced, NEG)
            sel = (_ranks(sc_ref, nsel, nsel) < ntop) & (j <= cur)
        bias_ref[0:nsel, toks] = jnp.where(sel, 0.0, NEG).astype(bias_ref.dtype)
        if nsel < FEAT:
            bias_ref[nsel:FEAT, toks] = jnp.full((FEAT - nsel, tq), NEG, bias_ref.dtype)

    for qi in range(nq):
        q_tile(qi)


def _cmp_attn(nq_t, kvc, ovt, *, B, S):
    tq, nh, G = TQ, NSA_HG, NSA_KV_GROUPS
    R = nh * tq
    nq = S // tq
    NC = kvc.shape[2]
    ns = S // NSA_SEL_BLOCK
    assert ns <= FEAT
    ntop = min(NSA_TOP_N, ns)
    return pl.pallas_call(
        functools.partial(_cmp_attn_kernel, tq=tq, nh=nh, ns=ns, ntop=ntop),
        grid=(B, G),
        in_specs=[pl.BlockSpec((nh * HEAD_DIM, S), lambda b, g: (g, b)),
                  pl.BlockSpec((None, None, NC, LANES), lambda b, g: (0, b, 0, g)),
                  pl.BlockSpec((None, None, NC, LANES), lambda b, g: (1, b, 0, g)),
                  pl.BlockSpec(ovt.shape, lambda b, g: (0, 0))],
        out_specs=[pl.BlockSpec((None, None, HEAD_DIM, S * nh), lambda b, g: (b, g, 0, 0)),
                   pl.BlockSpec((None, None, FEAT, S), lambda b, g: (b, g, 0, 0))],
        out_shape=[jax.ShapeDtypeStruct((B, G, HEAD_DIM, S * nh), BF16),
                   jax.ShapeDtypeStruct((B, G, FEAT, S), BF16)],
        scratch_shapes=[pltpu.VMEM((FEAT, tq), F32)],
        compiler_params=pltpu.CompilerParams(dimension_semantics=("parallel", "parallel"),
                                             vmem_limit_bytes=VMEM_LIMIT),
        name="cmp_attn",
    )(nq_t, kvc, kvc, ovt)


def _moba_gate_kernel(q_ref, k_ref, a_ref, bias_ref, sc_ref, *, nb, ntop):
    nbp, S = sc_ref.shape[1:]
    km = _dot(a_ref[...], k_ref[...])
    lane = lax.broadcasted_iota(jnp.int32, (nbp, LANES), 1)
    parts = []
    for hh in range(2):
        parts += _hilo(jnp.where(lane // HEAD_DIM == hh, km, 0.0))
    g = _dot(jnp.concatenate(parts, axis=0), q_ref[...])
    past = (lax.broadcasted_iota(jnp.int32, (nbp, S), 0)
            < lax.broadcasted_iota(jnp.int32, (nbp, S), 1) // MOBA_BLOCK)
    n = lax.broadcasted_iota(jnp.int32, (nbp, MOBA_BLOCK), 0)
    for hh in range(2):
        sc = sc_ref.at[hh]
        sc[...] = jnp.where(past, g[2 * hh * nbp:(2 * hh + 1) * nbp] + g[(2 * hh + 1) * nbp:(2 * hh + 2) * nbp], NEG)
        for c in range(nb):
            cols = slice(c * MOBA_BLOCK, (c + 1) * MOBA_BLOCK)
            sel = n <= c
            if c > ntop:
                nr = -(-c // SUBLANES) * SUBLANES
                rank = _ranks(sc, c, nr, cols)
                rank = rank + jnp.where(sc[0:nr, cols] < NEG, nb - c, 0)
                if nr < nbp:
                    rank = jnp.concatenate([rank, jnp.full((nbp - nr, MOBA_BLOCK), nb, jnp.int32)], axis=0)
                sel = ((rank < ntop) & (n < c)) | (n == c)
            bias_ref[hh, 0:nbp, cols] = jnp.where(sel, 0.0, NEG).astype(bias_ref.dtype)
        bias_ref[hh, nbp:FEAT, :] = jnp.zeros((FEAT - nbp, S), bias_ref.dtype)


def _moba_gate(mq_t, mk, avg, *, B, S):
    H = MOBA_HEADS
    nb = S // MOBA_BLOCK
    nbp = avg.shape[0]
    assert nb <= nbp and S % MOBA_BLOCK == 0 and MOBA_BLOCK % LANES == 0 and H % 2 == 0
    ntop = min(MOBA_TOP_K, nb)
    return pl.pallas_call(
        functools.partial(_moba_gate_kernel, nb=nb, ntop=ntop),
        grid=(B, H // 2),
        in_specs=[pl.BlockSpec((2 * HEAD_DIM, S), lambda b, h: (h, b)),
                  pl.BlockSpec((S, LANES), lambda b, h: (b, h)),
                  pl.BlockSpec(avg.shape, lambda b, h: (0, 0))],
        out_specs=pl.BlockSpec((None, 2, FEAT, S), lambda b, h: (b, h, 0, 0)),
        out_shape=jax.ShapeDtypeStruct((B, H, FEAT, S), BF16),
        scratch_shapes=[pltpu.VMEM((2, nbp, S), F32)],
        compiler_params=pltpu.CompilerParams(dimension_semantics=("parallel", "parallel"),
                                             vmem_limit_bytes=VMEM_LIMIT),
        name="moba_gate",
    )(mq_t, mk, avg)


def _flash_q_tile(qi, running_max, q_ref, bias_ref, mask_ref, o_ref, ka_ref, vta_ref, kn_ref, qa_ref, bnd_ref,
                  m_ref, acc_ref, alpha_ref, p_ref, *, tq, tk, nh, pair, window, has_past=None):
    u = pl.program_id(1)
    R = nh * tq
    static = isinstance(qi, int)
    q0 = qi * tq if static else pl.multiple_of(qi * tq, tq)
    toks = pl.ds(q0, tq)

    qcat = jnp.concatenate([q_ref[hh * HEAD_DIM:(hh + 1) * HEAD_DIM, toks] for hh in range(nh)], axis=1)
    qf = qcat.astype(F32)
    qq = jnp.concatenate([qf, qf], axis=0)
    rowi = lax.broadcasted_iota(jnp.int32, (2 * HEAD_DIM, R), 0)
    coli = lax.broadcasted_iota(jnp.int32, (2 * HEAD_DIM, R), 1)
    half = (coli // tq) % 2 if pair else u % 2
    qa_ref[0:2 * HEAD_DIM, :] = jnp.where(rowi // HEAD_DIM == half, qq, 0.0).astype(BF16)
    if bias_ref is None:
        qa_ref[2 * HEAD_DIM:2 * HEAD_DIM + FEAT, :] = jnp.zeros((FEAT, R), BF16)
    else:
        qa_ref[2 * HEAD_DIM:2 * HEAD_DIM + FEAT, :] = jnp.concatenate(
            [bias_ref[hh if pair else 0, :, toks] for hh in range(nh)], axis=1)
    qa_ref[2 * HEAD_DIM + FEAT:2 * LANES, :] = _alibi_query_rows(q0, tq, u * nh, R)
    if pair:
        kn = jnp.concatenate([kn_ref[hh:hh + 1, :] for hh in range(nh) for _ in range(tq // LANES)], axis=1)
    else:
        kn = jnp.concatenate([jnp.where(u % 2 == 0, kn_ref[0:1, :], kn_ref[1:2, :])] * (R // LANES), axis=1)
    qn = jnp.sqrt(jnp.sum(qf * qf, axis=0, keepdims=True))
    bnd_ref[...] = jnp.broadcast_to(qn * kn, bnd_ref.shape)

    ncg = R // CG
    kd = (q0 + tq - 1) // tk

    def key_rows(kind, c):
        off = (c * CG) % tq
        if kind is None:
            return 0, tk, None
        if kind == "diag":
            return 0, off + CG, off // CG
        return off, tk, (tk + off) // CG

    def values(ki, kind=None):
        for c in range(ncg):
            cols = pl.ds(c * CG, CG)
            r0, r1, _ = key_rows(kind, c)
            v = (c * CG) // tq if pair else 0
            pv = _dot(vta_ref[v, ki, :, r0:r1], p_ref[r0:r1, cols])
            if running_max:
                acc_ref[:, cols] = alpha_ref[0:1, cols] * acc_ref[:, cols] + pv
            else:
                acc_ref[:, cols] = acc_ref[:, cols] + pv

    def scores(ki, kind=None, prev=None, prev_kind=None):
        k0 = ki * tk if isinstance(ki, int) else pl.multiple_of(ki * tk, tk)
        s_groups = []
        for c in range(ncg):
            r0, r1, _ = key_rows(kind, c)
            kb = (c * CG) // (2 * tq) if pair else 0
            s_groups.append(_dot(ka_ref[kb, pl.ds(k0 + r0, r1 - r0), :], qa_ref[:, pl.ds(c * CG, CG)]))
        if prev is not None:
            values(prev, prev_kind)
        for c, s in enumerate(s_groups):
            cols = pl.ds(c * CG, CG)
            r0, r1, entry = key_rows(kind, c)
            if entry is not None:
                s = s + mask_ref[entry, r0:r1, :]
            if running_max:
                m_prev = m_ref[0:1, cols]
                m_new = jnp.maximum(m_prev, jnp.max(s, axis=0, keepdims=True))
                alpha_ref[:, cols] = jnp.broadcast_to(jnp.exp(m_prev - m_new), (alpha_ref.shape[0], CG))
                m_ref[:, cols] = jnp.broadcast_to(m_new, (m_ref.shape[0], CG))
                p_ref[r0:r1, cols] = jnp.exp(s - m_new).astype(BF16)
            else:
                p_ref[r0:r1, cols] = jnp.exp(s - bnd_ref[0:1, cols]).astype(BF16)

    def steady(lo, hi):
        def pair_body(j, carry):
            ki = lo + 2 * j
            scores(ki, prev=ki - 1)
            scores(ki + 1, prev=ki)
            return carry
        lax.fori_loop(0, (hi - lo) // 2, pair_body, 0)

        @pl.when((hi - lo) % 2 == 1)
        def _odd_one():
            scores(hi - 1, prev=hi - 2)

    def only_diagonal():
        scores(0, "diag")
        values(0, "diag")

    def past_then_diagonal():
        if window is not None:
            scores(kd - 1, "wedge")
            scores(kd, "diag", prev=kd - 1, prev_kind="wedge")
        elif static:
            scores(0)
            for ki in range(1, kd):
                scores(ki, prev=ki - 1)
            scores(kd, "diag", prev=kd - 1)
        else:
            scores(0)
            steady(1, kd)
            scores(kd, "diag", prev=kd - 1)
        values(kd, "diag")

    acc_ref[...] = jnp.zeros(acc_ref.shape, F32)
    if running_max:
        m_ref[...] = jnp.full(m_ref.shape, NEG, F32)
    if static:
        only_diagonal() if kd == 0 else past_then_diagonal()
    elif has_past:
        past_then_diagonal()
    else:
        pl.when(kd == 0)(only_diagonal)
        pl.when(kd > 0)(past_then_diagonal)

    acc = acc_ref[...]
    out = (acc[0:HEAD_DIM] / acc[HEAD_DIM:HEAD_DIM + 1]).astype(o_ref.dtype)
    if pair:
        for hh in range(nh):
            o_ref[hh, :, toks] = out[:, hh * tq:(hh + 1) * tq]
    else:
        o_ref[0, :, pl.ds(qi * R if static else pl.multiple_of(qi * R, R), R)] = out
    return jnp.min(acc[HEAD_DIM:HEAD_DIM + 1, :])


def _flash_kernel(*refs, tq, tk, nh, pair, window, has_bias):
    q_ref = refs[0]
    bias_ref = refs[1] if has_bias else None
    (k_ref, kf_ref, vt_ref, mask_ref, o_ref, ka_ref, vta_ref, kn_ref, qa_ref, bnd_ref, m_ref, acc_ref,
     alpha_ref, p_ref, low_ref) = refs[2 if has_bias else 1:]
    S = k_ref.shape[0]
    for kb in range(ka_ref.shape[0]):
        k = k_ref[:, kb * LANES:(kb + 1) * LANES]
        ka_ref[kb, :, 0:LANES] = k
        ka_ref[kb, :, LANES:2 * LANES] = kf_ref[...]
        k2 = jnp.square(k.astype(F32))
        lane = lax.broadcasted_iota(jnp.int32, k2.shape, 1)
        for hf in range(2):
            n2 = jnp.sum(jnp.where(lane // HEAD_DIM == hf, k2, 0.0), axis=1, keepdims=True)
            kn_ref[2 * kb + hf:2 * kb + hf + 1, :] = jnp.broadcast_to(
                jnp.sqrt(jnp.max(n2, axis=0, keepdims=True)), (1, LANES))
    ones_row = jnp.where(lax.broadcasted_iota(jnp.int32, (VROWS - HEAD_DIM, tk), 0) == 0, 1.0, 0.0)
    for v in range(nh if pair else 1):
        for jt in range(S // tk):
            vta_ref[v, jt, 0:HEAD_DIM, :] = vt_ref[v * HEAD_DIM:(v + 1) * HEAD_DIM, jt * tk:(jt + 1) * tk]
            vta_ref[v, jt, HEAD_DIM:VROWS, :] = ones_row.astype(BF16)

    tile = functools.partial(_flash_q_tile, q_ref=q_ref, bias_ref=bias_ref, mask_ref=mask_ref, o_ref=o_ref,
                             ka_ref=ka_ref, vta_ref=vta_ref, kn_ref=kn_ref, qa_ref=qa_ref, bnd_ref=bnd_ref,
                             m_ref=m_ref, acc_ref=acc_ref, alpha_ref=alpha_ref, p_ref=p_ref,
                             tq=tq, tk=tk, nh=nh, pair=pair, window=window)
    nq = S // tq
    low_ref[0] = tile(0, False)

    def bound_stabilised(qi, carry):
        low_ref[qi] = tile(qi, False, has_past=True)
        return carry
    lax.fori_loop(1, nq, bound_stabilised, 0)

    def redo(qi, carry):
        @pl.when(low_ref[qi] < MIN_WEIGHT_SUM)
        def _running_max():
            tile(qi, True)
        return carry

    lax.fori_loop(0, nq, redo, 0)


def _mask_table(tk, window):
    n_delta = (tk + (window or 0)) // CG
    d = np.arange(n_delta)[:, None, None] * CG + np.arange(CG)[None, None, :] - np.arange(tk)[None, :, None]
    ok = (d >= 0) & ((d < window) if window else True)
    return jnp.asarray(np.where(ok, 0.0, NEG), F32)


def _flash(q_t, bias_t, k, kfeat, v_t, *, B, S, units, tq, nh, pair, window):
    tk = TK
    R = nh * tq
    nq = S // tq
    nv = nh if pair else 1
    nb = nh if pair else 1
    nkb = nh // 2 if pair else 1
    assert tq == tk and S % tk == 0 and tk % CG == 0 and window in (None, tk)
    mask = _mask_table(tk, window)
    kcol = (lambda u: u) if pair else (lambda u: u // 2)
    out_w = tq if pair else R
    out_shape = (B, units * nb, HEAD_DIM, S if pair else S * nh)
    has_bias = bias_t is not None
    in_specs = [pl.BlockSpec((nh * HEAD_DIM, S), lambda b, u: (u, b))]
    if has_bias:
        in_specs.append(pl.BlockSpec((None, nb, FEAT, S), lambda b, u: (b, u, 0, 0)))
    in_specs += [pl.BlockSpec((S, nkb * LANES), lambda b, u: (b, kcol(u))),
                 pl.BlockSpec((S, LANES), lambda b, u: (0, 0)),
                 pl.BlockSpec((nv * HEAD_DIM, S), lambda b, u: (u, b)),
                 pl.BlockSpec(mask.shape, lambda b, u: (0, 0, 0))]
    args = (q_t,) + ((bias_t,) if has_bias else ()) + (k, kfeat, v_t, mask)
    return pl.pallas_call(
        functools.partial(_flash_kernel, tq=tq, tk=tk, nh=nh, pair=pair, window=window, has_bias=has_bias),
        grid=(B, units),
        in_specs=in_specs,
        out_specs=pl.BlockSpec((None, nb, HEAD_DIM, out_w * nq), lambda b, u: (b, u, 0, 0)),
        out_shape=jax.ShapeDtypeStruct(out_shape, BF16),
        scratch_shapes=[pltpu.VMEM((nkb, S, 2 * LANES), BF16),
                        pltpu.VMEM((nv, S // tk, VROWS, tk), BF16),
                        pltpu.VMEM((SUBLANES, LANES), F32),
                        pltpu.VMEM((2 * LANES, R), BF16),
                        pltpu.VMEM((SUBLANES, R), F32),
                        pltpu.VMEM((SUBLANES, R), F32),
                        pltpu.VMEM((VROWS, R), F32),
                        pltpu.VMEM((SUBLANES, R), F32),
                        pltpu.VMEM((tk, R), BF16),
                        pltpu.SMEM((nq,), F32)],
        compiler_params=pltpu.CompilerParams(dimension_semantics=("parallel", "parallel"),
                                             vmem_limit_bytes=VMEM_LIMIT),
        name="flash",
    )(*args)


def _even_out_kernel(x_ref, ocmp_ref, osel_ref, owin_ref, om_ref, gate_ref, nz_ref, mz_ref,
                     w_ref, g_ref, b_ref, o_ref, sig_ref, *, alpha, tm):
    G, nh = NSA_KV_GROUPS, NSA_HG
    sig_ref[...] = _sigmoid(gate_ref[...])
    heads = []
    for g in range(G):
        for hh in range(nh):
            acc = None
            for br, ref in enumerate((ocmp_ref, osel_ref, owin_ref)):
                gate = sig_ref[pl.ds(3 * (g * nh + hh) + br, 1), :]
                term = gate * ref[g, :, hh * tm:(hh + 1) * tm].astype(F32)
                acc = term if acc is None else acc + term
            heads.append(acc)
    o_nsa = jnp.concatenate(heads, axis=0)
    nz = nz_ref[...].astype(F32)
    mz = mz_ref[...].astype(F32)
    o_m = jnp.concatenate([om_ref[h] for h in range(MOBA_HEADS)], axis=0).astype(F32)
    a_t = jnp.concatenate([o_nsa * _silu(nz), o_m * _silu(mz)], axis=0)
    y = _dot_tn(a_t.astype(BF16), w_ref[...])
    o_ref[...] = _layer_norm(alpha * x_ref[...] + y, g_ref[...], b_ref[...])


def _even_out(x2d, ocmp, osel, owin, om, gate_t, nz_t, mz_t, w_out, ln_g, ln_b, alpha, *, B, S):
    T, D = x2d.shape
    tm = TQ
    nq = S // tm
    G, nh, H = NSA_KV_GROUPS, NSA_HG, MOBA_HEADS
    nsa = pl.BlockSpec((None, G, HEAD_DIM, nh * tm), lambda i: (i // nq, 0, 0, i % nq))
    feat = lambda n: pl.BlockSpec((n, tm), lambda i: (0, i))
    full = lambda a: pl.BlockSpec(a.shape, lambda i: (0,) * a.ndim)
    return pl.pallas_call(
        functools.partial(_even_out_kernel, alpha=alpha, tm=tm),
        grid=(T // tm,),
        in_specs=[pl.BlockSpec((tm, D), lambda i: (i, 0)), nsa, nsa, nsa,
                  pl.BlockSpec((None, H, HEAD_DIM, tm), lambda i: (i // nq, 0, 0, i % nq)),
                  feat(GATE_ROWS), feat(NSA_W), feat(MOBA_W),
                  full(w_out), full(ln_g), full(ln_b)],
        out_specs=pl.BlockSpec((tm, D), lambda i: (i, 0)),
        out_shape=jax.ShapeDtypeStruct((T, D), F32),
        scratch_shapes=[pltpu.VMEM((GATE_ROWS, tm), F32)],
        compiler_params=pltpu.CompilerParams(dimension_semantics=("parallel",),
                                             vmem_limit_bytes=VMEM_LIMIT),
        name="even_out",
    )(x2d, ocmp, osel, owin, om, gate_t, nz_t, mz_t, w_out, ln_g, ln_b)


def _gla_head_exact(row0, h, q_ref, v_ref, o_ref, st_ref, b_ref, kf_ref, scale):
    C, dk, dv, SUB = GLA_CHUNK, GLA_DK, GLA_DV, GLA_SUB
    rows = lax.broadcasted_iota(jnp.int32, (C, dk), 0)
    sub_i = lax.broadcasted_iota(jnp.int32, (SUB, LANES), 0)
    lane = lax.broadcasted_iota(jnp.int32, (SUB, LANES), 1)
    ones = jnp.ones((dk, LANES), BF16)
    sl = pl.ds(h * dk, dk)
    rs_c = pl.ds(row0, C)
    bh = b_ref[:, sl]
    kf = kf_ref[:, sl]
    qf = q_ref[rs_c, sl].astype(F32) * scale
    vh = v_ref[rs_c, pl.ds(h * dv, dv)]
    st = st_ref[h]
    inter = _dot_nt((qf * jnp.exp(bh)).astype(BF16), st.astype(BF16))
    bl = b_ref[pl.ds(C - 1, 1), sl]
    kd = kf * jnp.exp(bl - bh)

    def off_block(r0, r1, c0, c1):
        ref_row = b_ref[pl.ds(r0, 1), sl]
        qt = qf[r0:r1] * jnp.exp(bh[r0:r1] - ref_row)
        ek = jnp.where((rows >= c0) & (rows < c1), ref_row - bh, NEG)
        kt = kf * jnp.exp(ek)
        return _dot_nt(qt.astype(BF16), kt.astype(BF16))

    lower = off_block(2 * SUB, C, 0, 2 * SUB)
    blocks = [None, off_block(SUB, 2 * SUB, 0, SUB), lower[0:SUB],
              lower[SUB:2 * SUB] + off_block(3 * SUB, C, 2 * SUB, 3 * SUB)]
    pieces = []
    for blk in range(C // SUB):
        r0 = blk * SUB
        q_i = qf[r0:r0 + SUB]
        b_i = bh[r0:r0 + SUB]
        for jj in range(SUB):
            b_j = b_ref[pl.ds(r0 + jj, 1), sl]
            k_j = kf_ref[pl.ds(r0 + jj, 1), sl]
            pieces.append((q_i * k_j * jnp.exp(jnp.minimum(b_i - b_j, 0.0))).astype(BF16))
    sums = _dot(jnp.concatenate(pieces, axis=0), ones)
    for blk in range(C // SUB):
        r0 = blk * SUB
        diag = jnp.zeros((SUB, LANES), F32)
        for jj in range(SUB):
            rs = sums[(r0 + jj) * SUB:(r0 + jj + 1) * SUB]
            diag = jnp.where((lane == r0 + jj) & (sub_i >= jj), rs, diag)
        diag = diag[:, 0:C]
        blocks[blk] = diag if blocks[blk] is None else blocks[blk] + diag
    att = jnp.concatenate(blocks, axis=0)
    o_ref[rs_c, pl.ds(h * dv, dv)] = (inter + _dot(att.astype(BF16), vh)).astype(o_ref.dtype)
    st_ref[h] = st * jnp.exp(bl) + _dot_tn(vh, kd.astype(BF16))


def _gla_head_bounded(row0, h, q_ref, v_ref, o_ref, st_ref, b_ref, kf_ref, scale):
    C, dk, dv = GLA_CHUNK, GLA_DK, GLA_DV
    sl = pl.ds(h * dk, dk)
    rs_c = pl.ds(row0, C)
    bh = b_ref[rs_c, sl]
    kf = kf_ref[rs_c, sl]
    qf = q_ref[rs_c, sl].astype(F32) * scale
    vh = v_ref[rs_c, pl.ds(h * dv, dv)]
    st = st_ref[h]
    qs = (qf * jnp.exp(bh)).astype(BF16)
    kt = (kf * jnp.exp(-bh)).astype(BF16)
    i = lax.broadcasted_iota(jnp.int32, (C, C), 0)
    j = lax.broadcasted_iota(jnp.int32, (C, C), 1)
    att = jnp.where(i >= j, _dot_nt(qs, kt), 0.0)
    inter = _dot_nt(qs, st.astype(BF16))
    o_ref[rs_c, pl.ds(h * dv, dv)] = (inter + _dot(att.astype(BF16), vh)).astype(o_ref.dtype)
    bl = b_ref[pl.ds(row0 + C - 1, 1), sl]
    kd = kf * jnp.exp(bl - bh)
    st_ref[h] = st * jnp.exp(bl) + _dot_tn(vh, kd.astype(BF16))


def _gla_kernel(q_ref, k_ref, v_ref, glr_ref, w2_ref, gb_ref, tril_ref, o_ref,
                st_ref, b_ref, kf_ref, bc_ref, kc_ref, *, scale):
    C, H = GLA_CHUNK, GLA_HEADS
    nch = q_ref.shape[0] // C

    @pl.when(pl.program_id(1) == 0)
    def _init():
        st_ref[...] = jnp.zeros(st_ref.shape, F32)

    whi, wlo = _hilo(w2_ref[...])
    part = tril_ref.shape[0]
    decay = None
    for r0 in range(0, nch * C, part):
        rs = pl.ds(r0, part)
        ghi, glo = _hilo(glr_ref[rs, :])
        pre = _dot(ghi, whi) + _dot(glo, whi) + _dot(ghi, wlo) + gb_ref[...]
        la = (jnp.minimum(pre, 0.0) - jnp.log(1.0 + jnp.exp(-jnp.abs(pre)))) * (1.0 / GLA_TAU)
        lhi, llo = _hilo(la)
        b_part = _dot(tril_ref[...], lhi) + _dot(tril_ref[...], llo)
        b_ref[rs, :] = b_part
        for c in range(part // C):
            d_c = -b_part[c * C + C - 1:(c + 1) * C, :]
            decay = d_c if decay is None else jnp.maximum(decay, d_c)
    kf_ref[...] = k_ref[...].astype(F32)
    bounded = jnp.max(decay) <= GLA_FACTOR_MAX_DECAY

    @pl.when(bounded)
    def _factored():
        for c in range(nch):
            for h in range(H):
                _gla_head_bounded(c * C, h, q_ref, v_ref, o_ref, st_ref, b_ref, kf_ref, scale)

    @pl.when(jnp.logical_not(bounded))
    def _exact():
        def chunk(c, carry):
            row0 = pl.multiple_of(c * C, C)
            bc_ref[...] = b_ref[pl.ds(row0, C), :]
            kc_ref[...] = kf_ref[pl.ds(row0, C), :]
            for h in range(H):
                _gla_head_exact(row0, h, q_ref, v_ref, o_ref, st_ref, bc_ref, kc_ref, scale)
            return carry
        lax.fori_loop(0, nch, chunk, 0)


def _gla(q, k, v, glr, w2p, gb, tril, *, B, S):
    C, H, dk, dv = GLA_CHUNK, GLA_HEADS, GLA_DK, GLA_DV
    rows = C * min(GLA_STEP_CHUNKS, S // C)
    nchunk = S // rows
    tok = lambda n: pl.BlockSpec((rows, n), lambda b, c: (b * nchunk + c, 0))
    const = lambda shape: pl.BlockSpec(shape, lambda b, c: (0,) * len(shape))
    return pl.pallas_call(
        functools.partial(_gla_kernel, scale=float(dk) ** -0.5),
        grid=(B, nchunk),
        in_specs=[tok(H * dk), tok(H * dk), tok(H * dv), tok(LANES),
                  const(w2p.shape), const(gb.shape), const(tril.shape)],
        out_specs=tok(H * dv),
        out_shape=jax.ShapeDtypeStruct((B * S, H * dv), BF16),
        scratch_shapes=[pltpu.VMEM((H, dv, dk), F32),
                        pltpu.VMEM((rows, H * dk), F32),
                        pltpu.VMEM((rows, H * dk), F32),
                        pltpu.VMEM((C, H * dk), F32),
                        pltpu.VMEM((C, H * dk), F32)],
        compiler_params=pltpu.CompilerParams(dimension_semantics=("parallel", "arbitrary"),
                                             vmem_limit_bytes=VMEM_LIMIT),
        name="gla",
    )(q, k, v, glr, w2p, gb, tril)


def _odd_out_kernel(x_ref, o_ref_in, z_ref, gn_ref, w_ref, g_ref, b_ref, out_ref, *, alpha):
    H, dv = GLA_HEADS, GLA_DV
    parts = []
    for h in range(H):
        sl = pl.ds(h * dv, dv)
        oh = o_ref_in[:, sl].astype(F32)
        r = lax.rsqrt(jnp.mean(oh * oh, axis=-1, keepdims=True) + LN_EPS)
        z = z_ref[:, sl].astype(F32)
        parts.append((oh * r * gn_ref[:, sl] * _silu(z)).astype(BF16))
    y = _dot(jnp.concatenate(parts, axis=1), w_ref[...])
    out_ref[...] = _layer_norm(alpha * x_ref[...] + y, g_ref[...], b_ref[...])


def _odd_out(x2d, o, z, gn_g, w_out, ln_g, ln_b, alpha, tm=1024):
    T, D = x2d.shape
    W = o.shape[1]
    row = lambda n: pl.BlockSpec((tm, n), lambda i: (i, 0))
    return pl.pallas_call(
        functools.partial(_odd_out_kernel, alpha=alpha),
        grid=(T // tm,),
        in_specs=[row(D), row(W), row(W),
                  pl.BlockSpec((1, W), lambda i: (0, 0)),
                  pl.BlockSpec((W, D), lambda i: (0, 0)),
                  pl.BlockSpec((1, D), lambda i: (0, 0)),
                  pl.BlockSpec((1, D), lambda i: (0, 0))],
        out_specs=row(D),
        out_shape=jax.ShapeDtypeStruct((T, D), F32),
        compiler_params=pltpu.CompilerParams(dimension_semantics=("parallel",),
                                             vmem_limit_bytes=VMEM_LIMIT),
        name="odd_out",
    )(x2d, o, z, gn_g, w_out, ln_g, ln_b)


PACK = 2 * SUBLANES


def _pack_rows_kernel(w_ref, *o_refs, plans):
    for o_ref, plan in zip(o_refs, plans):
        off = 0
        for src, n, scale, n_out in plan:
            n_load = min(-(-n // PACK) * PACK, n_out)
            rows = w_ref[src:src + n_load, :]
            if scale != 1.0:
                rows = rows * scale
            if n_load != n:
                rows = jnp.where(lax.broadcasted_iota(jnp.int32, rows.shape, 0) < n, rows, 0.0)
            o_ref[off:off + n_load, :] = rows.astype(BF16)
            if n_out > n_load:
                o_ref[off + n_load:off + n_out, :] = jnp.zeros((n_out - n_load, o_ref.shape[1]), BF16)
            off += n_out


def _pack_rows(w_t, plans, tc=256):
    W, D = w_t.shape
    plans = tuple(tuple((p + (p[1],))[:4] for p in plan) for plan in plans)
    for plan in plans:
        for src, n, _, n_out in plan:
            assert src % SUBLANES == 0 and n_out % PACK == 0 and n <= n_out and src + n_out <= W
    assert D % tc == 0
    rows = [sum(p[3] for p in plan) for plan in plans]
    return pl.pallas_call(
        functools.partial(_pack_rows_kernel, plans=plans),
        grid=(D // tc,),
        in_specs=[pl.BlockSpec((W, tc), lambda j: (0, j))],
        out_specs=[pl.BlockSpec((r, tc), lambda j: (0, j)) for r in rows],
        out_shape=[jax.ShapeDtypeStruct((r, D), BF16) for r in rows],
        compiler_params=pltpu.CompilerParams(dimension_semantics=("parallel",),
                                             vmem_limit_bytes=VMEM_LIMIT),
        name="pack_rows",
    )(w_t)


def _key_features(S, block):
    s = np.arange(S)
    f = np.zeros((S, LANES), np.float32)
    f[s, s // block] = 1.0
    f[:, FEAT + 0] = 1.0
    f[:, FEAT + 1] = 1.0
    f[:, FEAT + 2] = FEAT * (s // FEAT)
    f[:, FEAT + 3] = s % FEAT
    return jnp.asarray(f, BF16)


def _pad_to(a, n, axis):
    pad = [(0, 0)] * a.ndim
    pad[axis] = (0, n - a.shape[axis])
    return jnp.pad(a, pad)


def _even_layer(x2d, B, S, w_in, cmp_pos, cmp_w1, cmp_b1, cmp_w2, w_out, ln_g, ln_b, alpha):
    T, D = x2d.shape
    G, hd = NSA_KV_GROUPS, HEAD_DIM
    scale = hd ** -0.5
    kvw, (oq, okv, og, onz, omq, omk, omv, omz) = NSA_KV_W, np.cumsum((0,) + EVEN_SPLITS[:-1]).tolist()
    okc, ovc, oks, ovs, okw, ovw = (okv + i * kvw for i in range(6))
    w_tok, w_tr = _pack_rows(w_in.T, (
        ((okc, kvw, 1.0), (ovc, kvw, 1.0), (oks, kvw, 1.0), (okw, kvw, 1.0), (omk, MOBA_W, 1.0)),
        ((oq, NSA_W, scale), (ovs, kvw, 1.0), (ovw, kvw, 1.0), (omq, MOBA_W, scale), (omv, MOBA_W, 1.0),
         (onz, NSA_W, 1.0), (omz, MOBA_W, 1.0), (og, 3 * NSA_HEADS, 1.0, GATE_ROWS))))
    (kcmp, vcmp, ksel, kwin, mk, nq_t, vsel_t, vwin_t, mq_t, mv_t, nz_t, mz_t, gate_t) = _proj(
        x2d, w_tok, (NSA_KV_W,) * 4 + (MOBA_W,), (F32, F32, BF16, BF16, BF16),
        w_tr, (NSA_W, NSA_KV_W, NSA_KV_W, MOBA_W, MOBA_W, NSA_W, MOBA_W, GATE_ROWS),
        (BF16,) * 7 + (F32,))

    L, d, HID = NSA_CMP_BLOCK, NSA_CMP_STRIDE, NSA_CMP_HIDDEN
    ncmp = (S - L) // d + 1
    NC = S // d
    def group_diagonal(w, axis):
        z = jnp.zeros_like(w)
        return jnp.stack([jnp.concatenate([w if h == g else z for h in range(G)], axis=-1) for g in range(G)], axis=axis)

    w1r = cmp_w1.reshape(2, 2, d, hd, HID).astype(BF16)
    w1p = group_diagonal(w1r, 3).reshape(2, 2, d * G * hd, G * HID)
    posp = jnp.broadcast_to(cmp_pos.reshape(2, 2, d, 1, hd), (2, 2, d, G, hd)).reshape(2, 2, 1, d * G * hd)
    b1p = jnp.tile(cmp_b1, (1, G)).reshape(2, 1, G * HID)
    w2p = group_diagonal(_pad_to(cmp_w2, LANES, 2).astype(BF16), 1).reshape(2, G * HID, G * LANES)
    cfeat = np.zeros((NC, G, LANES), np.float32)
    cfeat[:, :, hd + 0] = 1.0
    cfeat[:, :, hd + 1] = 1.0
    cfeat[:, :, hd + 2] = (np.arange(NC) * d)[:, None]
    cfeat[:, :, hd + 3] = L - 1
    kvc = _cmp_mlp(kcmp, vcmp, w1p, posp, b1p, w2p, jnp.asarray(cfeat.reshape(NC, G * LANES)), ncmp,
                   B=B, S=S)

    ns = S // NSA_SEL_BLOCK
    cmp_start = np.arange(NC) * d
    sel_start = np.arange(FEAT) * NSA_SEL_BLOCK
    ovt = ((cmp_start[None, :] <= sel_start[:, None] + NSA_SEL_BLOCK - 1)
           & (cmp_start[None, :] + L - 1 >= sel_start[:, None])
           & (np.arange(NC)[None, :] < ncmp) & (np.arange(FEAT)[:, None] < ns))
    o_cmp, bias_sel = _cmp_attn(nq_t, kvc, jnp.asarray(ovt, BF16), B=B, S=S)

    feats_sel = _key_features(S, NSA_SEL_BLOCK)
    o_sel = _flash(nq_t, bias_sel, ksel, feats_sel, vsel_t,
                   B=B, S=S, units=G, tq=TQ, nh=NSA_HG, pair=False, window=None)
    o_win = _flash(nq_t, None, kwin, feats_sel, vwin_t,
                   B=B, S=S, units=G, tq=TQ, nh=NSA_HG, pair=False, window=NSA_WINDOW)

    avg = np.zeros((MOBA_NBP, S), np.float32)
    avg[np.arange(S) // MOBA_BLOCK, np.arange(S)] = 1.0 / MOBA_BLOCK
    bias_m = _moba_gate(mq_t, mk, jnp.asarray(avg, BF16), B=B, S=S)
    o_m = _flash(mq_t, bias_m, mk, _key_features(S, MOBA_BLOCK), mv_t,
                 B=B, S=S, units=MOBA_HEADS // MOBA_STEP_HEADS, tq=TQM, nh=MOBA_STEP_HEADS, pair=True, window=None)

    return _even_out(x2d, o_cmp, o_sel, o_win, o_m, gate_t, nz_t, mz_t, w_out.astype(BF16),
                     ln_g.reshape(1, D), ln_b.reshape(1, D), alpha, B=B, S=S)


def _odd_layer(x2d, B, S, w_in, gate_w2, gate_b, gn_g, w_out, ln_g, ln_b, alpha):
    T, D = x2d.shape
    H, dk, dv, C = GLA_HEADS, GLA_DK, GLA_DV, GLA_CHUNK
    n_qkv, n_gate, n_z = sum(ODD_SPLITS[:3]), ODD_SPLITS[3], ODD_SPLITS[4]
    w_perm, = _pack_rows(w_in.T, (((0, n_qkv, 1.0), (n_qkv + n_gate, n_z, 1.0), (n_qkv, n_gate, 1.0, LANES)),))
    q, k, v, z, glr = _proj(x2d, w_perm, (H * dk, H * dk, H * dv, H * dv, LANES),
                            (BF16, BF16, BF16, BF16, F32), tm=1024)
    w2p = _pad_to(gate_w2, LANES, 0)
    ncs = min(GLA_CUMSUM_CHUNKS, GLA_STEP_CHUNKS, S // C)
    tril = jnp.asarray(np.kron(np.eye(ncs), np.tril(np.ones((C, C)))), BF16)
    o = _gla(q, k, v, glr, w2p, gate_b.reshape(1, H * dk), tril, B=B, S=S)
    return _odd_out(x2d, o, z, gn_g.reshape(1, H * dv), w_out.astype(BF16),
                    ln_g.reshape(1, D), ln_b.reshape(1, D), alpha)


def kernel(x, ev_w_in, ev_cmp_pos, ev_cmp_w1, ev_cmp_b1, ev_cmp_w2, ev_w_out, ev_ln_g, ev_ln_b,
           od_w_in, od_gate_w2, od_gate_b, od_gn_g, od_w_out, od_ln_g, od_ln_b):
    B, S, D = x.shape
    depth = ev_w_in.shape[0] + od_w_in.shape[0]
    alpha = (2.0 * depth) ** 0.25
    h = x.reshape(B * S, D)
    for layer in range(depth):
        i = layer // 2
        if layer % 2 == 0:
            h = _even_layer(h, B, S, ev_w_in[i], ev_cmp_pos[i], ev_cmp_w1[i], ev_cmp_b1[i], ev_cmp_w2[i],
                            ev_w_out[i], ev_ln_g[i], ev_ln_b[i], alpha)
        else:
            h = _odd_layer(h, B, S, od_w_in[i], od_gate_w2[i], od_gate_b[i], od_gn_g[i],
                           od_w_out[i], od_ln_g[i], od_ln_b[i], alpha)
    return h.reshape(B, S, D)
```
